```python
import math
import jax
import jax.numpy as jnp
from jax import lax
import numpy as np

D_MODEL = 1024
BATCH = 4
SEQ = 4096
DEPTH = 1

GRID_W = 64
CTX_LEN = 256
H_RET = 4
RET_DK = 128
RET_DV = 128
RET_W = H_RET * RET_DV
H_GDN = 4
GDN_DK = 128
GDN_DV = 128
GDN_W = H_GDN * GDN_DV
SHORT_CONV = 3
CHUNK = 128
ROPE_BASE = 10000.0
N_STATE = 2 * RET_W + 2 * GDN_W + 4 * H_GDN
N_QUERY = 2 * RET_W + 2 * GDN_W + 2 * D_MODEL
N_IN = N_STATE + N_QUERY
N_EXPERTS = 64
TOP_K = 8
N_GROUPS = 8
TOPK_GROUPS = 4
D_EXPERT = 256
D_SHARED = 256
ROUTED_SCALE = 2.5
EXPERT_BLOCK = 128
EPS = 1e-6

kernel_name = 'hybrid_retention_gdn_moe_prefix_dit'


def _rmsnorm(t, g):
    tf = t.astype(jnp.float32)
    y = tf * lax.rsqrt(jnp.mean(tf * tf, axis=-1, keepdims=True) + EPS)
    return (y * g.astype(jnp.float32)).astype(t.dtype)


def _l2norm(t):
    return t * lax.rsqrt(jnp.sum(t * t, axis=-1, keepdims=True) + EPS)


def _group_norm(t):
    tc = t - jnp.mean(t, axis=-1, keepdims=True)
    return tc * lax.rsqrt(jnp.mean(tc * tc, axis=-1, keepdims=True) + EPS)


def _head_rms(t):
    return t * lax.rsqrt(jnp.mean(t * t, axis=-1, keepdims=True) + EPS)


def _split_heads(t, n_heads):
    b, l, w = t.shape
    return t.reshape(b, l, n_heads, w // n_heads).transpose(0, 2, 1, 3)


def _merge_heads(t):
    b, h, l, d = t.shape
    return t.transpose(0, 2, 1, 3).reshape(b, l, h * d)


def _dwconv(t, w):
    return lax.conv_general_dilated(t, w[:, None, :].astype(t.dtype), (1,), 'SAME',
                                    dimension_numbers=('NWC', 'WIO', 'NWC'),
                                    feature_group_count=t.shape[-1])


def _modulation(cvec, w, b, n):
    m = jax.nn.silu(cvec) @ w[:, : n * D_MODEL] + b[: n * D_MODEL]
    return jnp.split(m, n, axis=-1)


def _axial_rope(n):
    rows = n // GRID_W
    pos_r = jnp.repeat(jnp.arange(rows, dtype=jnp.float32), GRID_W)
    pos_c = jnp.tile(jnp.arange(GRID_W, dtype=jnp.float32), rows)
    n_freq = RET_DK // 4
    inv = ROPE_BASE ** (-jnp.arange(n_freq, dtype=jnp.float32) / n_freq)
    ang = jnp.concatenate([pos_r[:, None] * inv, pos_c[:, None] * inv], axis=-1)
    return jnp.cos(ang), jnp.sin(ang)


def _apply_rope(t, cos, sin):
    half = RET_DK // 2
    t1, t2 = t[..., :half], t[..., half:]
    return jnp.concatenate([t1 * cos - t2 * sin, t1 * sin + t2 * cos], axis=-1)


def _retention_scan(q, k, v, log_g, s0):
    b, h, l, dk = k.shape
    n = l // CHUNK
    kc = k.reshape(b, h, n, CHUNK, dk)
    vc = v.reshape(b, h, n, CHUNK, -1)
    lg = log_g.astype(jnp.float32)[:, None]
    idx = jnp.arange(CHUNK, dtype=jnp.float32)
    kv = jnp.einsum('bhncd,hc,bhnce->bhnde', kc, jnp.exp(lg * (CHUNK - 1 - idx)), vc)
    chunk_decay = jnp.exp(lg[:, 0] * CHUNK)[None, :, None, None]

    def step(s, kv_n):
        return chunk_decay * s + kv_n, (None if q is None else s)

    s_fin, s_prev = lax.scan(step, s0, jnp.moveaxis(kv, 2, 0))
    if q is None:
        return s_fin
    qc = q.reshape(b, h, n, CHUNK, dk)
    dist = idx[:, None] - idx[None, :]
    dmat = jnp.where(dist >= 0, jnp.exp(lg[:, :, None] * jnp.maximum(dist, 0.0)), 0.0)
    scores = jnp.einsum('bhnid,bhnjd->bhnij', qc, kc) * dmat[None, :, None]
    o = jnp.einsum('bhnij,bhnje->bhnie', scores, vc)
    o = o + jnp.einsum('bhnid,nbhde->bhnie', qc, s_prev) * jnp.exp(lg * (idx + 1.0))[None, :, None, :, None]
    return o.reshape(b, h, l, -1), s_fin


def _gdn_scan(q, k, v, log_a, beta, s0):
    b, h, l, dk = k.shape
    dv = v.shape[-1]
    n = l // CHUNK
    kc = k.reshape(b, h, n, CHUNK, dk)
    vc = v.reshape(b, h, n, CHUNK, dv)
    bc = beta.reshape(b, h, n, CHUNK, 1)
    g = jnp.cumsum(log_a.reshape(b, h, n, CHUNK), axis=-1)
    idx = jnp.arange(CHUNK)
    strict = idx[:, None] > idx[None, :]
    rel = g[..., :, None] - g[..., None, :]
    a = jnp.einsum('bhnid,bhnjd->bhnij', kc, kc) * bc * jnp.where(strict, jnp.exp(jnp.where(strict, rel, 0.0)), 0.0)
    rhs = jnp.concatenate([bc * vc, bc * jnp.exp(g)[..., None] * kc], axis=-1)
    sol = lax.linalg.triangular_solve(a + jnp.eye(CHUNK, dtype=a.dtype), rhs,
                                      left_side=True, lower=True, unit_diagonal=True)
    u, wk = sol[..., :dv], sol[..., dv:]
    g_last = g[..., -1]
    k_tail = kc * jnp.exp(g_last[..., None] - g)[..., None]

    def step(s, inp):
        u_n, w_n, kt_n, gl_n = inp
        v_new = u_n - jnp.einsum('bhck,bhkv->bhcv', w_n, s)
        s_next = jnp.exp(gl_n)[..., None, None] * s + jnp.einsum('bhck,bhcv->bhkv', kt_n, v_new)
        return s_next, (None if q is None else (v_new, s))

    xs = tuple(jnp.moveaxis(t, 2, 0) for t in (u, wk, k_tail, g_last))
    s_fin, ys = lax.scan(step, s0, xs)
    if q is None:
        return s_fin
    v_new, s_prev = ys
    qc = q.reshape(b, h, n, CHUNK, dk)
    incl = idx[:, None] >= idx[None, :]
    qk = jnp.einsum('bhnid,bhnjd->bhnij', qc, kc) * jnp.where(incl, jnp.exp(jnp.where(incl, rel, 0.0)), 0.0)
    o = jnp.einsum('bhnij,nbhjv->bhniv', qk, v_new)
    o = o + jnp.einsum('bhnik,nbhkv->bhniv', qc * jnp.exp(g)[..., None], s_prev)
    return o.reshape(b, h, l, -1), s_fin


def _state_features(h, w_in_l, conv_l, a_log_l, dt_bias_l):
    p = h @ w_in_l[:, :N_STATE]
    rk = _split_heads(p[..., :RET_W].astype(jnp.float32), H_RET) * RET_DK ** -0.5
    rv = _split_heads(p[..., RET_W:2 * RET_W].astype(jnp.float32), H_RET)
    gkv = jax.nn.silu(_dwconv(p[..., 2 * RET_W:2 * RET_W + 2 * GDN_W], conv_l[:, GDN_W:])).astype(jnp.float32)
    gk = _l2norm(_split_heads(gkv[..., :GDN_W], H_GDN))
    gv = _split_heads(gkv[..., GDN_W:], H_GDN)
    b, l = h.shape[:2]
    gab = p[..., 2 * RET_W + 2 * GDN_W:].astype(jnp.float32).reshape(b, l, 2, 2, H_GDN).transpose(2, 3, 0, 4, 1)
    log_a = -jnp.exp(a_log_l.astype(jnp.float32))[:, None, :, None] * jax.nn.softplus(
        gab[0] + dt_bias_l.astype(jnp.float32)[:, None, :, None])
    beta = jax.nn.sigmoid(gab[1])
    return rk, rv, gk, gv, log_a, beta


def _query_features(h, w_in_l, conv_l):
    p = h @ w_in_l[:, N_STATE:]
    rq = _split_heads(p[..., :RET_W].astype(jnp.float32), H_RET)
    rg = p[..., RET_W:2 * RET_W]
    gq = jax.nn.silu(_dwconv(p[..., 2 * RET_W:2 * RET_W + GDN_W], conv_l[:, :GDN_W])).astype(jnp.float32)
    gq = _l2norm(_split_heads(gq, H_GDN)) * GDN_DK ** -0.5
    gz = p[..., 2 * RET_W + GDN_W:2 * RET_W + 2 * GDN_W]
    gates = jax.nn.sigmoid(p[..., 2 * RET_W + 2 * GDN_W:].astype(jnp.float32))
    return rq, rg, gq, gz, gates


def _recurrent_mixers(sf, rq, gq, init, ret_log_decay_l):
    rk, rv, gk, gv, log_a, beta = sf
    flip = lambda t: jnp.flip(t, axis=2)
    r_f = _retention_scan(rq, rk, rv, ret_log_decay_l[0], init[0])
    r_b = _retention_scan(None if rq is None else flip(rq), flip(rk), flip(rv), ret_log_decay_l[1], init[1])
    g_f = _gdn_scan(gq, gk, gv, log_a[0], beta[0], init[2])
    g_b = _gdn_scan(None if gq is None else flip(gq), flip(gk), flip(gv), flip(log_a[1]), flip(beta[1]), init[3])
    if rq is None:
        return (r_f, r_b, g_f, g_b)
    return r_f[0] + flip(r_b[0]), g_f[0] + flip(g_b[0]), (r_f[1], r_b[1], g_f[1], g_b[1])


def _mixer_out(ret_o, gdn_o, rg, gz, gates, ret_gn_w_l, gdn_norm_w_l, w_ret_out_l, w_gdn_out_l, w_o_l):
    dt = w_o_l.dtype
    ret_y = _merge_heads(_group_norm(ret_o)) * ret_gn_w_l.astype(jnp.float32) * jax.nn.silu(rg.astype(jnp.float32))
    gdn_y = _merge_heads(_head_rms(gdn_o)) * gdn_norm_w_l.astype(jnp.float32) * jax.nn.silu(gz.astype(jnp.float32))
    merged = (gates[..., :D_MODEL] * (ret_y.astype(dt) @ w_ret_out_l)
              + gates[..., D_MODEL:] * (gdn_y.astype(dt) @ w_gdn_out_l))
    return merged.astype(dt) @ w_o_l


def _moe(h, w_router_l, bias_l, w_gate_l, w_up_l, w_down_l, w_sh_gate_l, w_sh_up_l, w_sh_down_l):
    b, l, d = h.shape
    t = b * l
    xt = h.reshape(t, d)
    scores = jax.nn.sigmoid((xt @ w_router_l).astype(jnp.float32))
    biased = scores + bias_l.astype(jnp.float32)
    per_group = N_EXPERTS // N_GROUPS
    grp_score = lax.top_k(biased.reshape(t, N_GROUPS, per_group), 2)[0].sum(-1)
    grp_keep = jax.nn.one_hot(lax.top_k(grp_score, TOPK_GROUPS)[1], N_GROUPS, dtype=jnp.float32).sum(1) > 0
    cand = jnp.where(jnp.repeat(grp_keep, per_group, axis=1), biased, -jnp.inf)
    sel = lax.top_k(cand, TOP_K)[1]
    wts = jnp.take_along_axis(scores, sel, axis=1)
    wts = wts / jnp.sum(wts, axis=-1, keepdims=True) * ROUTED_SCALE
    n_assign = t * TOP_K
    flat_e = sel.reshape(-1)
    order = jnp.argsort(flat_e)
    e_sorted = flat_e[order]
    sizes = jnp.bincount(flat_e, length=N_EXPERTS).astype(jnp.int32)
    padded = (sizes + EXPERT_BLOCK - 1) // EXPERT_BLOCK * EXPERT_BLOCK
    pad_end = jnp.cumsum(padded)
    slot = (pad_end - padded)[e_sorted] + jnp.arange(n_assign, dtype=jnp.int32) - (jnp.cumsum(sizes) - sizes)[e_sorted]
    n_blocks = -(-n_assign // EXPERT_BLOCK) + N_EXPERTS
    n_slots = n_blocks * EXPERT_BLOCK
    slot_tok = jnp.full((n_slots,), t, jnp.int32).at[slot].set((order // TOP_K).astype(jnp.int32))
    slot_w = jnp.zeros((n_slots,), jnp.float32).at[slot].set(wts.reshape(-1)[order])
    block_e = jnp.minimum(jnp.searchsorted(pad_end, jnp.arange(n_blocks, dtype=jnp.int32) * EXPERT_BLOCK, side='right'),
                          N_EXPERTS - 1)
    xb = jnp.concatenate([xt, jnp.zeros((1, d), xt.dtype)])[slot_tok].reshape(n_blocks, EXPERT_BLOCK, d)

    def expert_block(args):
        xe, e = args
        return (jax.nn.silu(xe @ w_gate_l[e]) * (xe @ w_up_l[e])) @ w_down_l[e]

    yb = lax.map(expert_block, (xb, block_e)).reshape(n_slots, d)
    routed = jax.ops.segment_sum(yb * slot_w[:, None].astype(yb.dtype), slot_tok, num_segments=t + 1)[:t]
    shared = (jax.nn.silu(xt @ w_sh_gate_l) * (xt @ w_sh_up_l)) @ w_sh_down_l
    return (routed + shared).reshape(b, l, d)


def setup_inputs(seed: int = 0) -> dict:
    key = jax.random.key(seed)
    ks = jax.random.split(key, 32)
    f32 = jnp.float32
    D = D_MODEL

    def nrm(k, shape, fan_in, scale=1.0):
        return jax.random.normal(k, shape, f32) * (scale * fan_in ** -0.5)

    def gain(k, shape):
        return 1.0 + 0.05 * jax.random.normal(k, shape, f32)

    base_decay = jnp.log1p(-jnp.power(2.0, -5.0 - jnp.arange(H_RET, dtype=f32)))
    dt = jnp.exp(jax.random.uniform(ks[12], (DEPTH, 2, H_GDN), f32, math.log(1e-3), math.log(1e-1)))
    return {
        'x': jax.random.normal(ks[0], (BATCH, SEQ, D), f32),
        'c': jax.random.normal(ks[1], (BATCH, D), f32),
        'ctx': jax.random.normal(ks[2], (BATCH, CTX_LEN, D), f32),
        'c_ctx': jax.random.normal(ks[3], (D,), f32),
        'w_mod': nrm(ks[4], (DEPTH, D, 6 * D), D, 0.3),
        'b_mod': 0.02 * jax.random.normal(ks[5], (DEPTH, 6 * D), f32),
        'norm_mix_pre': gain(ks[6], (DEPTH, D)),
        'norm_mix_post': gain(ks[7], (DEPTH, D)),
        'norm_ffn_pre': gain(ks[8], (DEPTH, D)),
        'norm_ffn_post': gain(ks[9], (DEPTH, D)),
        'w_in': nrm(ks[10], (DEPTH, D, N_IN), D),
        'gdn_conv': nrm(ks[11], (DEPTH, SHORT_CONV, 3 * GDN_W), SHORT_CONV),
        'ret_log_decay': base_decay * (1.0 + 0.05 * jax.random.normal(ks[13], (DEPTH, 2, H_RET), f32)),
        'gdn_a_log': jnp.log(jax.random.uniform(ks[14], (DEPTH, 2, H_GDN), f32, 1.0, 16.0)),
        'gdn_dt_bias': dt + jnp.log(-jnp.expm1(-dt)),
        'ret_gn_w': gain(ks[15], (DEPTH, RET_W)),
        'gdn_norm_w': gain(ks[16], (DEPTH, GDN_W)),
        'w_ret_out': nrm(ks[17], (DEPTH, RET_W, D), RET_W),
        'w_gdn_out': nrm(ks[18], (DEPTH, GDN_W, D), GDN_W),
        'w_o': nrm(ks[19], (DEPTH, D, D), D),
        'w_router': nrm(ks[20], (DEPTH, D, N_EXPERTS), D),
        'router_bias': 0.01 * jax.random.normal(ks[21], (DEPTH, N_EXPERTS), f32),
        'w_gate': nrm(ks[22], (DEPTH, N_EXPERTS, D, D_EXPERT), D),
        'w_up': nrm(ks[23], (DEPTH, N_EXPERTS, D, D_EXPERT), D),
        'w_down': nrm(ks[24], (DEPTH, N_EXPERTS, D_EXPERT, D), D_EXPERT),
        'w_sh_gate': nrm(ks[25], (DEPTH, D, D_SHARED), D),
        'w_sh_up': nrm(ks[26], (DEPTH, D, D_SHARED), D),
        'w_sh_down': nrm(ks[27], (DEPTH, D_SHARED, D), D_SHARED),
    }


def reference(x, c, ctx, c_ctx, w_mod, b_mod, norm_mix_pre, norm_mix_post, norm_ffn_pre, norm_ffn_post,
              w_in, gdn_conv, ret_log_decay, gdn_a_log, gdn_dt_bias, ret_gn_w, gdn_norm_w,
              w_ret_out, w_gdn_out, w_o, w_router, router_bias, w_gate, w_up, w_down,
              w_sh_gate, w_sh_up, w_sh_down):
    b, n, _ = x.shape
    cos, sin = _axial_rope(n)
    zr = jnp.zeros((b, H_RET, RET_DK, RET_DV), jnp.float32)
    zg = jnp.zeros((b, H_GDN, GDN_DK, GDN_DV), jnp.float32)
    zero_state = (zr, zr, zg, zg)
    for i in range(DEPTH):
        last = i == DEPTH - 1
        out_p = (ret_gn_w[i], gdn_norm_w[i], w_ret_out[i], w_gdn_out[i], w_o[i])
        moe_p = (w_router[i], router_bias[i], w_gate[i], w_up[i], w_down[i], w_sh_gate[i], w_sh_up[i], w_sh_down[i])
        sh1, sc1, g1, sh2, sc2, g2 = [m[:, None, :] for m in _modulation(c, w_mod[i], b_mod[i], 6)]
        cm = _modulation(c_ctx, w_mod[i], b_mod[i], 2 if last else 6)
        hc = _rmsnorm(ctx, norm_mix_pre[i]) * (1.0 + cm[1]) + cm[0]
        sf_c = _state_features(hc, w_in[i], gdn_conv[i], gdn_a_log[i], gdn_dt_bias[i])
        if last:
            init = _recurrent_mixers(sf_c, None, None, zero_state, ret_log_decay[i])
        else:
            rq_c, rg_c, gq_c, gz_c, gates_c = _query_features(hc, w_in[i], gdn_conv[i])
            ret_c, gdn_c, init = _recurrent_mixers(sf_c, rq_c, gq_c, zero_state, ret_log_decay[i])
            ctx = ctx + cm[2] * _rmsnorm(_mixer_out(ret_c, gdn_c, rg_c, gz_c, gates_c, *out_p), norm_mix_post[i])
            h2c = _rmsnorm(ctx, norm_ffn_pre[i]) * (1.0 + cm[4]) + cm[3]
            ctx = ctx + cm[5] * _rmsnorm(_moe(h2c, *moe_p), norm_ffn_post[i])
        hx = _rmsnorm(x, norm_mix_pre[i]) * (1.0 + sc1) + sh1
        rk, rv, gk, gv, log_a, beta = _state_features(hx, w_in[i], gdn_conv[i], gdn_a_log[i], gdn_dt_bias[i])
        rq, rg, gq, gz, gates = _query_features(hx, w_in[i], gdn_conv[i])
        sf_x = (_apply_rope(rk, cos, sin), rv, gk, gv, log_a, beta)
        ret_x, gdn_x, _ = _recurrent_mixers(sf_x, _apply_rope(rq, cos, sin), gq, init, ret_log_decay[i])
        x = x + g1 * _rmsnorm(_mixer_out(ret_x, gdn_x, rg, gz, gates, *out_p), norm_mix_post[i])
        h2 = _rmsnorm(x, norm_ffn_pre[i]) * (1.0 + sc2) + sh2
        x = x + g2 * _rmsnorm(_moe(h2, *moe_p), norm_ffn_post[i])
    return x
```

```python
import functools
import math

import jax
import jax.numpy as jnp
from jax import lax
from jax.experimental import pallas as pl
from jax.experimental.pallas import tpu as pltpu

F32 = jnp.float32
BF16 = jnp.bfloat16
I32 = jnp.int32
HIGHEST = lax.Precision.HIGHEST

N_HEADS = 4
HEAD_DIM = 128
MIX_W = N_HEADS * HEAD_DIM
CHUNK = 128
SHORT_CONV = 3
ROPE_BASE = 10000.0
GRID_W = 64
N_EXPERTS = 64
TOP_K = 8
N_GROUPS = 8
TOPK_GROUPS = 4
GROUP_SIZE = N_EXPERTS // N_GROUPS
ROUTED_SCALE = 2.5
EPS = 1e-6
N_GAB = 4 * N_HEADS

LANES = 128
BF16_TILE_ROWS = 16
VMEM_LIMIT_BYTES = 56 * 1024 * 1024

PROJ_TM = 256
MOE_TM = 256
MB_ROWS = BF16_TILE_ROWS
_MB_SHIFT = MB_ROWS.bit_length() - 1
_TILE_MB = -(-(TOP_K * MOE_TM // MB_ROWS + N_EXPERTS * (MB_ROWS - 1) // MB_ROWS + 1) // 8) * 8
TILE_ROWS = _TILE_MB * MB_ROWS
EXP_BM = 256
EXP_MB = EXP_BM // MB_ROWS


def _cparams(sem):
    return pltpu.CompilerParams(dimension_semantics=sem, vmem_limit_bytes=VMEM_LIMIT_BYTES)


def _silu(v):
    return v * jax.nn.sigmoid(v)


def _softplus(v):
    return jnp.maximum(v, 0.0) + jnp.log1p(jnp.exp(-jnp.abs(v)))


def _iota(shape, dim):
    return lax.broadcasted_iota(I32, shape, dim)


def _dot(a, b, **kw):
    return jnp.dot(a, b, preferred_element_type=F32, **kw)


def _dot_nt(a, b, **kw):
    return lax.dot_general(a, b, (((1,), (1,)), ((), ())), preferred_element_type=F32, **kw)


def _dot_tn(a, b, **kw):
    return lax.dot_general(a, b, (((0,), (0,)), ((), ())), preferred_element_type=F32, **kw)


def _rms(v, gain):
    return v * lax.rsqrt(jnp.mean(v * v, axis=-1, keepdims=True) + EPS) * gain


def _mod_kernel(c_ref, w_ref, b_ref, o_ref):
    o_ref[...] = _dot(_silu(c_ref[...]), w_ref[...], precision=HIGHEST) + b_ref[...]


def _modulation(cvec, w_mod, b_mod):
    rows, d = cvec.shape
    n = w_mod.shape[1]
    tn = 1024
    return pl.pallas_call(
        _mod_kernel,
        grid=(n // tn,),
        in_specs=[pl.BlockSpec((rows, d), lambda j: (0, 0)),
                  pl.BlockSpec((d, tn), lambda j: (0, j)),
                  pl.BlockSpec((1, tn), lambda j: (0, j))],
        out_specs=pl.BlockSpec((rows, tn), lambda j: (0, j)),
        out_shape=jax.ShapeDtypeStruct((rows, n), F32),
        compiler_params=_cparams(("arbitrary",)),
        name="adaln_modulation",
    )(cvec, w_mod, b_mod)


_PLAIN, _ROPE, _ROPE_SCALED, _SIGMOID = 0, 1, 2, 3


def _proj_kernel(x_ref, gain_ref, sc_ref, sh_ref, w_ref, wg_ref, wgt_ref, prow_ref, pcol_ref, cos_ref, sin_ref,
                 *out_refs, kinds, rope):
    n_feat = len(kinds)
    feat_refs = out_refs[:n_feat]
    base_ref, eg_ref, et_ref, grow_ref, gl_ref = out_refs[n_feat:]
    tm = x_ref.shape[0]

    x = x_ref[...]
    h = (_rms(x, gain_ref[...]) * (1.0 + sc_ref[...]) + sh_ref[...]).astype(BF16)

    if rope:
        cos2 = cos_ref[...]
        sin2 = sin_ref[...]

    for g, kind in enumerate(kinds):
        p = _dot(h, w_ref[:, g * MIX_W:(g + 1) * MIX_W])
        if kind == _SIGMOID:
            p = jax.nn.sigmoid(p)
        elif kind in (_ROPE, _ROPE_SCALED) and rope:
            heads = []
            for hh in range(N_HEADS):
                t = p[:, hh * HEAD_DIM:(hh + 1) * HEAD_DIM]
                heads.append(t * cos2 + pltpu.roll(t, HEAD_DIM // 2, 1) * sin2)
            p = jnp.concatenate(heads, axis=1)
        if kind == _ROPE_SCALED:
            p = p * (HEAD_DIM ** -0.5)
        feat_refs[g][...] = p.astype(feat_refs[g].dtype)

    pg = _dot(h, wg_ref[...])
    col = _iota((tm, N_GAB), 1) & 3
    is_alpha = col < 2
    la = jnp.where(is_alpha, prow_ref[0:1, :] * _softplus(pg + prow_ref[1:2, :]), 0.0)
    beta = jax.nn.sigmoid(pg)
    pgt = _dot_nt(wgt_ref[...], h)
    rowq = _iota((N_HEADS * 8, tm), 0) & 7
    lat = jnp.where(rowq < 2, pcol_ref[:, 0:1] * _softplus(pgt + pcol_ref[:, 1:2]), 0.0)

    r_i = _iota((CHUNK, CHUNK), 0)
    c_i = _iota((CHUNK, CHUNK), 1)
    lower_incl = (c_i <= r_i).astype(F32)
    upper_incl = (c_i >= r_i).astype(F32)
    colc = _iota((CHUNK, N_GAB), 1) & 3
    rowc = _iota((N_HEADS * 8, CHUNK), 0) & 7
    for c in range(tm // CHUNK):
        sl = slice(c * CHUNK, (c + 1) * CHUNK)
        la_c = la[sl]
        pre = _dot(lower_incl, la_c, precision=HIGHEST)
        suf = _dot(upper_incl, la_c, precision=HIGHEST)
        fwd = (colc == 0)
        g_c = jnp.where(fwd, pre, suf)
        rest = jnp.where(fwd, suf, pre) - la_c
        base_ref[sl, :] = jnp.where(colc < 2, g_c, beta[sl])
        eg_ref[sl, :] = jnp.exp(g_c)
        et_ref[sl, :] = jnp.exp(rest)
        gl_ref[c:c + 1, :] = jnp.exp(g_c[0:1, :] + rest[0:1, :])
        lat_c = lat[:, sl]
        pre_t = _dot(lat_c, upper_incl, precision=HIGHEST)
        suf_t = _dot(lat_c, lower_incl, precision=HIGHEST)
        grow_ref[:, :, sl] = jnp.where(rowc == 0, pre_t, suf_t).reshape(N_HEADS, 8, CHUNK)


def _in_projection(x, gain, scale, shift, w_main, w_gab, w_gab_t, prow, pcol, cos2, sin2, kinds, rope):
    b, l, d = x.shape
    tm = min(PROJ_TM, l)
    tiles = l // tm
    n_chunk = tm // CHUNK
    feat_shapes = [jax.ShapeDtypeStruct((b, l, MIX_W), BF16) for _ in kinds]
    feat_specs = [pl.BlockSpec((None, tm, MIX_W), lambda i, j: (i, j, 0)) for _ in kinds]
    col_shape = jax.ShapeDtypeStruct((b, l, N_GAB), F32)
    col_spec = pl.BlockSpec((None, tm, N_GAB), lambda i, j: (i, j, 0))
    out_shape = feat_shapes + [col_shape, col_shape, col_shape,
                               jax.ShapeDtypeStruct((b, N_HEADS, 8, l), F32),
                               jax.ShapeDtypeStruct((b, tiles, n_chunk, N_GAB), F32)]
    out_specs = feat_specs + [col_spec, col_spec, col_spec,
                              pl.BlockSpec((None, N_HEADS, 8, tm), lambda i, j: (i, 0, 0, j)),
                              pl.BlockSpec((None, None, n_chunk, N_GAB), lambda i, j: (i, j, 0, 0))]
    const = functools.partial(pl.BlockSpec, pipeline_mode=pl.Buffered(1))
    ncol = w_main.shape[1]
    in_specs = [
        pl.BlockSpec((None, tm, d), lambda i, j: (i, j, 0)),
        const((1, d), lambda i, j: (0, 0)),
        pl.BlockSpec((None, 1, d), lambda i, j: (i, 0, 0)),
        pl.BlockSpec((None, 1, d), lambda i, j: (i, 0, 0)),
        const((d, ncol), lambda i, j: (0, 0)),
        const((d, N_GAB), lambda i, j: (0, 0)),
        const((N_HEADS * 8, d), lambda i, j: (0, 0)),
        const((2, N_GAB), lambda i, j: (0, 0)),
        const((N_HEADS * 8, 2), lambda i, j: (0, 0)),
        pl.BlockSpec((tm, HEAD_DIM), lambda i, j: (j, 0)),
        pl.BlockSpec((tm, HEAD_DIM), lambda i, j: (j, 0)),
    ]
    outs = pl.pallas_call(
        functools.partial(_proj_kernel, kinds=tuple(kinds), rope=rope),
        grid=(b, tiles),
        in_specs=in_specs,
        out_specs=out_specs,
        out_shape=out_shape,
        compiler_params=_cparams(("arbitrary", "arbitrary")),
        name="in_projection_rope" if rope else "in_projection_ctx",
    )(x, gain, scale, shift, w_main, w_gab, w_gab_t, prow, pcol, cos2, sin2)
    feats = outs[:len(kinds)]
    base, eg, et, grow, gl = outs[len(kinds):]
    return feats, (base, eg, et, grow, gl.reshape(b * (l // CHUNK), N_GAB))


def _unit_triangular_inverse(a):
    r = _iota((CHUNK, CHUNK), 0)
    c = _iota((CHUNK, CHUNK), 1)
    eye = (r == c).astype(F32)
    inv = eye - jnp.where((r >> 1) == (c >> 1), a, 0.0)
    for level in range(1, int(math.log2(CHUNK))):
        off = jnp.where(((r >> (level + 1)) == (c >> (level + 1))) & ((r >> level) != (c >> level)), a, 0.0)
        invb = inv.astype(BF16)
        inv = inv - _dot(invb, _dot(off.astype(BF16), invb).astype(BF16))
    return inv


def _mixer_kernel(*refs, seq_len, with_query):
    n_chunk = seq_len // CHUNK
    if with_query:
        (lg_ref, gl_ref, rk_ref, rv_ref, gk_ref, gv_ref, rq_ref, gq_ref, rg_ref, gz_ref,
         base_ref, eg_ref, et_ref, grow_ref, cq_ref, ck_ref, cv_ref, s0_ref, gnw_ref, rmsw_ref,
         yret_ref, ygdn_ref,
         gks, gvs, gqs, sel8, oret, ogdn, rtile, state) = refs
    else:
        (lg_ref, gl_ref, rk_ref, rv_ref, gk_ref, gv_ref,
         base_ref, eg_ref, et_ref, grow_ref, ck_ref, cv_ref, s0_ref,
         sfin_ref,
         gks, gvs, sel8, rtile, state) = refs
    bi = pl.program_id(0)
    hi = pl.program_id(1)

    row = _iota((CHUNK, CHUNK), 0)
    colm = _iota((CHUNK, CHUNK), 1)
    rowf = row.astype(F32)
    colf = colm.astype(F32)

    r16 = _iota((N_GAB, 8), 0)
    c8 = _iota((N_GAB, 8), 1)
    sel_base = ((r16 == 4 * hi + c8) & (c8 < 4)).astype(F32)
    sel_eg = ((r16 == 4 * hi + c8 - 4) & (c8 >= 4) & (c8 < 6)).astype(F32)
    sel_et = ((r16 == 4 * hi + c8 - 6) & (c8 >= 6)).astype(F32)
    blk = min(seq_len, 512)

    def select_cols(i, carry):
        s = pl.multiple_of(i * blk, blk)
        sel8[pl.ds(s, blk), :] = (_dot(base_ref[pl.ds(s, blk), :], sel_base, precision=HIGHEST)
                                  + _dot(eg_ref[pl.ds(s, blk), :], sel_eg, precision=HIGHEST)
                                  + _dot(et_ref[pl.ds(s, blk), :], sel_et, precision=HIGHEST))
        return carry

    lax.fori_loop(0, seq_len // blk, select_cols, 0)

    def conv_chunk(src_ref, w_ref, n):
        s = pl.multiple_of(n * CHUNK, CHUNK)
        x = src_ref[pl.ds(s, CHUNK), :].astype(F32)
        ps = pl.multiple_of(jnp.maximum(s - BF16_TILE_ROWS, 0), BF16_TILE_ROWS)
        ns = pl.multiple_of(jnp.minimum(s + CHUNK, seq_len - BF16_TILE_ROWS), BF16_TILE_ROWS)
        prev_row = src_ref[pl.ds(ps, BF16_TILE_ROWS), :].astype(F32)[BF16_TILE_ROWS - 1:BF16_TILE_ROWS, :]
        next_row = src_ref[pl.ds(ns, BF16_TILE_ROWS), :].astype(F32)[0:1, :]
        prev_row = prev_row * jnp.where(n > 0, 1.0, 0.0)
        next_row = next_row * jnp.where(n < n_chunk - 1, 1.0, 0.0)
        xp = jnp.where(row == 0, jnp.broadcast_to(prev_row, (CHUNK, HEAD_DIM)), pltpu.roll(x, 1, 0))
        xn = jnp.where(row == CHUNK - 1, jnp.broadcast_to(next_row, (CHUNK, HEAD_DIM)), pltpu.roll(x, CHUNK - 1, 0))
        return _silu(w_ref[0:1, :] * xp + w_ref[1:2, :] * x + w_ref[2:3, :] * xn)

    def l2n(v):
        return v * lax.rsqrt(jnp.sum(v * v, axis=-1, keepdims=True) + EPS)

    def preprocess(n, carry):
        s = pl.multiple_of(n * CHUNK, CHUNK)
        gks[pl.ds(s, CHUNK), :] = l2n(conv_chunk(gk_ref, ck_ref, n)).astype(BF16)
        gvs[pl.ds(s, CHUNK), :] = conv_chunk(gv_ref, cv_ref, n).astype(BF16)
        if with_query:
            gqs[pl.ds(s, CHUNK), :] = (l2n(conv_chunk(gq_ref, cq_ref, n)) * (HEAD_DIM ** -0.5)).astype(BF16)
            oret[pl.ds(s, CHUNK), :] = jnp.zeros((CHUNK, HEAD_DIM), F32)
            ogdn[pl.ds(s, CHUNK), :] = jnp.zeros((CHUNK, HEAD_DIM), F32)
        return carry

    lax.fori_loop(0, n_chunk, preprocess, 0)

    for d in range(2):
        lg = lg_ref[d, hi]
        if d == 0:
            dist, pos_q, pos_k = rowf - colf, rowf + 1.0, (CHUNK - 1.0) - rowf
        else:
            dist, pos_q, pos_k = colf - rowf, CHUNK - rowf, rowf
        rtile[4 * d + 0] = jnp.where(dist >= 0, jnp.exp(lg * jnp.maximum(dist, 0.0)), 0.0)
        rtile[4 * d + 1] = jnp.exp(lg * pos_q)
        rtile[4 * d + 2] = jnp.exp(lg * pos_k)
        rtile[4 * d + 3] = jnp.exp(lg * jnp.full((CHUNK, CHUNK), float(CHUNK), F32))

    state[...] = s0_ref[...]

    def scan_step(n, carry):
        for d in range(2):
            nd = n if d == 0 else n_chunk - 1 - n
            s = pl.multiple_of(nd * CHUNK, CHUNK)
            cs = pl.ds(s, CHUNK)
            k = rk_ref[cs, :]
            v = rv_ref[cs, :]
            st = state[d]
            if with_query:
                q = rq_ref[cs, :]
                sc = _dot_nt(q, k) * rtile[4 * d + 0]
                o = _dot(sc.astype(BF16), v) + _dot(q, st.astype(BF16)) * rtile[4 * d + 1]
                oret[cs, :] += o
            kv = _dot_tn((k.astype(F32) * rtile[4 * d + 2]).astype(BF16), v)
            state[d] = rtile[4 * d + 3] * st + kv

            k = gks[cs, :]
            v = gvs[cs, :]
            kf = k.astype(F32)
            st = state[2 + d]
            stb = st.astype(BF16)
            cols = sel8[cs, :]
            gcol = jnp.broadcast_to(cols[:, d:d + 1], (CHUNK, CHUNK))
            beta = jnp.broadcast_to(cols[:, 2 + d:3 + d], (CHUNK, CHUNK))
            egc = jnp.broadcast_to(cols[:, 4 + d:5 + d], (CHUNK, CHUNK))
            etc = jnp.broadcast_to(cols[:, 6 + d:7 + d], (CHUNK, CHUNK))
            grow = jnp.broadcast_to(grow_ref[d:d + 1, cs], (CHUNK, CHUNK))
            incl = (row >= colm) if d == 0 else (row <= colm)
            strict = (row > colm) if d == 0 else (row < colm)
            dec = jnp.exp(jnp.where(incl, gcol - grow, 0.0))
            a = _dot_nt(k, k) * beta * jnp.where(strict, dec, 0.0)
            tinv = _unit_triangular_inverse(a).astype(BF16)
            u = _dot(tinv, (beta * v.astype(F32)).astype(BF16))
            wk = _dot(tinv, (beta * egc * kf).astype(BF16))
            v_new = u - _dot(wk.astype(BF16), stb)
            vnb = v_new.astype(BF16)
            if with_query:
                q = gqs[cs, :]
                qk = _dot_nt(q, k) * jnp.where(incl, dec, 0.0)
                o = _dot(qk.astype(BF16), vnb) + _dot((q.astype(F32) * egc).astype(BF16), stb)
                ogdn[cs, :] += o
            gl = gl_ref[bi * n_chunk + nd, 4 * hi + d]
            state[2 + d] = gl * st + _dot_tn((kf * etc).astype(BF16), vnb)
        return carry

    lax.fori_loop(0, n_chunk, scan_step, 0)

    if not with_query:
        sfin_ref[...] = state[...]
        return

    def finish(n, carry):
        s = pl.multiple_of(n * CHUNK, CHUNK)
        cs = pl.ds(s, CHUNK)
        ro = oret[cs, :]
        rc = ro - jnp.mean(ro, axis=-1, keepdims=True)
        ry = rc * lax.rsqrt(jnp.mean(rc * rc, axis=-1, keepdims=True) + EPS)
        yret_ref[cs, :] = (ry * gnw_ref[...] * _silu(rg_ref[cs, :].astype(F32))).astype(BF16)
        go = ogdn[cs, :]
        gy = go * lax.rsqrt(jnp.mean(go * go, axis=-1, keepdims=True) + EPS)
        ygdn_ref[cs, :] = (gy * rmsw_ref[...] * _silu(gz_ref[cs, :].astype(F32))).astype(BF16)
        return carry

    lax.fori_loop(0, n_chunk, finish, 0)


def _mixers(lg, gl, feats, dec, conv, s0, gnw, rmsw, with_query):
    base, eg, et, grow = dec
    b, l, _ = feats["rk"].shape
    smem = pl.BlockSpec(memory_space=pltpu.SMEM)
    head = pl.BlockSpec((None, l, HEAD_DIM), lambda i, j: (i, 0, j))
    colspec = pl.BlockSpec((None, l, N_GAB), lambda i, j: (i, 0, 0))
    rowspec = pl.BlockSpec((None, None, 8, l), lambda i, j: (i, j, 0, 0))
    st_spec = pl.BlockSpec((None, None, 4, HEAD_DIM, HEAD_DIM), lambda i, j: (i, j, 0, 0, 0))

    def conv_spec(which):
        return pl.BlockSpec((SHORT_CONV, HEAD_DIM), lambda i, j, w=which: (0, w * N_HEADS + j))

    def gain_spec():
        return pl.BlockSpec((1, HEAD_DIM), lambda i, j: (0, j))

    if with_query:
        args = [lg, gl, feats["rk"], feats["rv"], feats["gk"], feats["gv"], feats["rq"], feats["gq"], feats["rg"],
                feats["gz"], base, eg, et, grow, conv, conv, conv, s0, gnw, rmsw]
        in_specs = [smem, smem] + [head] * 8 + [colspec] * 3 + [rowspec, conv_spec(0), conv_spec(1), conv_spec(2),
                                                               st_spec, gain_spec(), gain_spec()]
        out_shape = [jax.ShapeDtypeStruct((b, l, MIX_W), BF16)] * 2
        out_specs = [head, head]
        scratch = [pltpu.VMEM((l, HEAD_DIM), BF16)] * 3 + [pltpu.VMEM((l, 8), F32),
                                                           pltpu.VMEM((l, HEAD_DIM), F32), pltpu.VMEM((l, HEAD_DIM), F32)]
    else:
        args = [lg, gl, feats["rk"], feats["rv"], feats["gk"], feats["gv"], base, eg, et, grow, conv, conv, s0]
        in_specs = [smem, smem] + [head] * 4 + [colspec] * 3 + [rowspec, conv_spec(1), conv_spec(2), st_spec]
        out_shape = jax.ShapeDtypeStruct((b, N_HEADS, 4, HEAD_DIM, HEAD_DIM), F32)
        out_specs = st_spec
        scratch = [pltpu.VMEM((l, HEAD_DIM), BF16)] * 2 + [pltpu.VMEM((l, 8), F32)]
    scratch = scratch + [pltpu.VMEM((8, CHUNK, CHUNK), F32), pltpu.VMEM((4, HEAD_DIM, HEAD_DIM), F32)]
    return pl.pallas_call(
        functools.partial(_mixer_kernel, seq_len=l, with_query=with_query),
        grid=(b, N_HEADS),
        in_specs=in_specs,
        out_specs=out_specs,
        out_shape=out_shape,
        scratch_shapes=scratch,
        compiler_params=_cparams(("arbitrary", "arbitrary")),
        name="mixers_latent" if with_query else "mixers_context",
    )(*args)


def _split_bf16(v):
    hi = v.astype(BF16)
    return hi, (v - hi.astype(F32)).astype(BF16)


def _route(h2, wrt_ref, bias_ref, cand_ref, sel_ref, wd_ref):
    tm = h2.shape[0]
    h_hi, h_lo = _split_bf16(h2)
    w_hi, w_lo = _split_bf16(wrt_ref[...])
    logits = _dot_nt(w_hi, h_hi) + (_dot_nt(w_hi, h_lo) + _dot_nt(w_lo, h_hi))
    scores = jax.nn.sigmoid(logits)
    biased = scores + bias_ref[...]
    neg_inf = float("-inf")
    sub = _iota((GROUP_SIZE, tm), 0).astype(F32)
    group_score = []
    for g in range(N_GROUPS):
        blk = biased[g * GROUP_SIZE:(g + 1) * GROUP_SIZE, :]
        m1 = jnp.max(blk, axis=0, keepdims=True)
        first = jnp.min(jnp.where(blk == m1, sub, float(GROUP_SIZE)), axis=0, keepdims=True)
        m2 = jnp.max(jnp.where(sub == first, neg_inf, blk), axis=0, keepdims=True)
        group_score.append(m1 + m2)
    for g in range(N_GROUPS):
        ahead = jnp.zeros((1, tm), I32)
        for g2 in range(N_GROUPS):
            if g2 == g:
                continue
            before = (group_score[g2] > group_score[g])
            if g2 < g:
                before = before | (group_score[g2] == group_score[g])
            ahead = ahead + before.astype(I32)
        keep = jnp.broadcast_to(ahead, (GROUP_SIZE, tm)) < TOPK_GROUPS
        cand_ref[g * GROUP_SIZE:(g + 1) * GROUP_SIZE, :] = jnp.where(
            keep, biased[g * GROUP_SIZE:(g + 1) * GROUP_SIZE, :], neg_inf)
    cand = cand_ref[...]
    eidx = _iota((N_EXPERTS, tm), 0)

    def count_ahead(e2, rank):
        c2 = cand_ref[pl.ds(e2, 1), :]
        before = (c2 > cand) | ((c2 == cand) & (e2 < eidx))
        return rank + before.astype(I32)

    rank = lax.fori_loop(0, N_EXPERTS, count_ahead, jnp.zeros((N_EXPERTS, tm), I32))
    sel = rank < TOP_K
    picked = jnp.where(sel, scores, 0.0)
    wsum = jnp.sum(picked, axis=0, keepdims=True)
    sel_ref[...] = sel.astype(F32)
    wd_ref[...] = picked / wsum * ROUTED_SCALE


def _mixout_kernel(x_ref, yr_ref, yg_ref, g0_ref, g1_ref, g2_ref, g3_ref, wr_ref, wg_ref, wo_ref,
                   npost_ref, gate1_ref, nffn_ref, sc2_ref, sh2_ref, wrt_ref, bias_ref,
                   x1_ref, h2_ref, sel_ref, wd_ref, cand_ref):
    r = _dot(yr_ref[...], wr_ref[...])
    g = _dot(yg_ref[...], wg_ref[...])
    half = r.shape[1] // 2
    merged = jnp.concatenate(
        [g0_ref[...].astype(F32) * r[:, :half] + g2_ref[...].astype(F32) * g[:, :half],
         g1_ref[...].astype(F32) * r[:, half:] + g3_ref[...].astype(F32) * g[:, half:]], axis=1)
    mo = _dot(merged.astype(BF16), wo_ref[...])
    x1 = x_ref[...] + gate1_ref[...] * _rms(mo, npost_ref[...])
    x1_ref[...] = x1
    h2 = _rms(x1, nffn_ref[...]) * (1.0 + sc2_ref[...]) + sh2_ref[...]
    h2_ref[...] = h2.astype(BF16)
    _route(h2, wrt_ref, bias_ref, cand_ref, sel_ref, wd_ref)


def _mixer_out(x, yr, yg, gates, wr, wg, wo, npost, gate1, nffn, sc2, sh2, wrt, bias):
    b, l, d = x.shape
    tm = PROJ_TM
    tiles = l // tm
    tok = lambda w: pl.BlockSpec((None, tm, w), lambda i, j: (i, j, 0))
    const = functools.partial(pl.BlockSpec, pipeline_mode=pl.Buffered(1))
    vec = lambda: const((1, d), lambda i, j: (0, 0))
    bvec = lambda: pl.BlockSpec((None, 1, d), lambda i, j: (i, 0, 0))
    emap = pl.BlockSpec((N_EXPERTS, tm), lambda i, j: (0, i * tiles + j))
    in_specs = ([tok(d), tok(MIX_W), tok(MIX_W)] + [tok(MIX_W)] * 4
                + [const((MIX_W, d), lambda i, j: (0, 0)), const((MIX_W, d), lambda i, j: (0, 0)),
                   const((d, d), lambda i, j: (0, 0)),
                   vec(), bvec(), vec(), bvec(), bvec(),
                   const((N_EXPERTS, d), lambda i, j: (0, 0)), const((N_EXPERTS, 1), lambda i, j: (0, 0))])
    return pl.pallas_call(
        _mixout_kernel,
        grid=(b, tiles),
        in_specs=in_specs,
        out_specs=[tok(d), tok(d), emap, emap],
        out_shape=[jax.ShapeDtypeStruct((b, l, d), F32), jax.ShapeDtypeStruct((b, l, d), BF16),
                   jax.ShapeDtypeStruct((N_EXPERTS, b * l), F32), jax.ShapeDtypeStruct((N_EXPERTS, b * l), F32)],
        scratch_shapes=[pltpu.VMEM((N_EXPERTS, tm), F32)],
        compiler_params=_cparams(("arbitrary", "arbitrary")),
        name="mixer_out_router",
    )(x, yr, yg, *gates, wr, wg, wo, npost, gate1, nffn, sc2, sh2, wrt, bias)


def _tile_positions(sel):
    tm = sel.shape[1]
    selb = sel.astype(BF16)
    earlier = (_iota((tm, tm), 0) < _iota((tm, tm), 1)).astype(BF16)
    rank = _dot(selb, earlier)
    cnt = _dot(selb, jnp.ones((tm, tm), BF16))
    nmb = jnp.floor((cnt + (MB_ROWS - 1)) * (1.0 / MB_ROWS))
    below = (_iota((N_EXPERTS, N_EXPERTS), 1) < _iota((N_EXPERTS, N_EXPERTS), 0)).astype(BF16)
    offmb = _dot(below, nmb.astype(BF16))
    rank_hi = jnp.floor(rank * (1.0 / MB_ROWS))
    hi = jnp.where(sel > 0.0, offmb + rank_hi, 255.0)
    lo = rank - rank_hi * MB_ROWS
    return hi, lo, cnt, offmb


def _dispatch_kernel(h2_ref, sel_ref, xs_ref, cnt_ref):
    tm = h2_ref.shape[0]
    sel = sel_ref[...]
    hi, lo, _, _ = _tile_positions(sel)
    hib = hi.astype(BF16)
    lob = lo.astype(BF16)
    selb = sel.astype(BF16)
    cnt_row = _dot_nt(jnp.ones((8, tm), BF16), selb)
    nmb_row = jnp.floor((cnt_row + (MB_ROWS - 1)) * (1.0 / MB_ROWS))
    before = (_iota((N_EXPERTS, N_EXPERTS), 0) < _iota((N_EXPERTS, N_EXPERTS), 1)).astype(BF16)
    start_row = _dot(nmb_row.astype(BF16), before) * MB_ROWS
    cnt_ref[...] = cnt_row
    start = start_row[0:1, :]
    end = start + cnt_row[0:1, :]
    x = h2_ref[...]
    rch = 256
    for rc in range(TILE_ROWS // rch):
        r_e = (rc * rch + _iota((rch, N_EXPERTS), 0)).astype(F32)
        owner = ((r_e >= start) & (r_e < end)).astype(BF16)
        r_t = rc * rch + _iota((rch, tm), 0)
        match = ((_dot(owner, hib) == (r_t >> _MB_SHIFT).astype(F32))
                 & (_dot(owner, lob) == (r_t & (MB_ROWS - 1)).astype(F32)))
        xs_ref[rc * rch:(rc + 1) * rch, :] = _dot(match.astype(BF16), x).astype(BF16)


def _dispatch(h2, sel_t):
    t, d = h2.shape
    tm = MOE_TM
    nt = t // tm
    return pl.pallas_call(
        _dispatch_kernel,
        grid=(nt,),
        in_specs=[pl.BlockSpec((tm, d), lambda i: (i, 0)), pl.BlockSpec((N_EXPERTS, tm), lambda i: (0, i))],
        out_specs=[pl.BlockSpec((TILE_ROWS, d), lambda i: (i, 0)), pl.BlockSpec((None, 8, N_EXPERTS), lambda i: (i, 0, 0))],
        out_shape=[jax.ShapeDtypeStruct((nt * TILE_ROWS, d), BF16), jax.ShapeDtypeStruct((nt, 8, N_EXPERTS), F32)],
        compiler_params=_cparams(("arbitrary",)),
        name="moe_dispatch",
    )(h2, sel_t)


def _expert_plan(cnt, n_blk):
    nt = cnt.shape[0]
    nmb = (cnt + (MB_ROWS - 1)) // MB_ROWS
    offmb = jnp.cumsum(nmb, axis=1) - nmb
    per_e = nmb.T
    incl = jnp.cumsum(per_e, axis=1)
    excl = incl - per_e
    tot = incl[:, -1]
    nb = (tot + (EXP_MB - 1)) // EXP_MB
    bend = jnp.cumsum(nb)
    bstart = bend - nb
    b = jnp.arange(n_blk, dtype=I32)
    valid = b < bend[-1]
    last = jnp.maximum(bend[-1] - 1, 0)
    bq = jnp.where(valid, b, last)
    blk_e = jnp.minimum(jnp.sum((bend[None, :] <= bq[:, None]).astype(I32), axis=1), N_EXPERTS - 1)
    p0 = (bq - bstart[blk_e]) * EXP_MB
    blk_nmb = jnp.where(valid, jnp.clip(tot[blk_e] - p0, 0, EXP_MB), 0).astype(I32)
    p = p0[:, None] + jnp.arange(EXP_MB, dtype=I32)[None, :]
    tile = jnp.sum((incl[blk_e][:, None, :] <= p[:, :, None]).astype(I32), axis=2)
    tile = jnp.minimum(tile, nt - 1)
    e2 = jnp.broadcast_to(blk_e[:, None], tile.shape)
    src = tile * _TILE_MB + offmb[tile, e2] + (p - excl[e2, tile])
    src = jnp.where(jnp.arange(EXP_MB, dtype=I32)[None, :] < blk_nmb[:, None], src, 0)
    return blk_e.astype(I32), blk_nmb, src.reshape(-1).astype(I32)


def _expert_kernel(blk_e_ref, blk_nmb_ref, src_ref, xs_hbm, wg_ref, wu_ref, wd_ref, ys_hbm,
                   xbuf, ybuf, in_sem, out_sem):
    del blk_e_ref
    i = pl.program_id(0)
    n_blk = pl.num_programs(0)

    def micro_copy(blk, slot, j, into_vmem):
        mb = src_ref[blk * EXP_MB + j]
        hbm_rows = pl.ds(pl.multiple_of(mb * MB_ROWS, MB_ROWS), MB_ROWS)
        vmem_rows = pl.ds(pl.multiple_of(j * MB_ROWS, MB_ROWS), MB_ROWS)
        if into_vmem:
            return pltpu.make_async_copy(xs_hbm.at[hbm_rows, :], xbuf.at[slot, vmem_rows, :], in_sem.at[slot])
        return pltpu.make_async_copy(ybuf.at[slot, vmem_rows, :], ys_hbm.at[hbm_rows, :], out_sem.at[slot])

    def for_each_micro(blk, fn):
        def body(j, c):
            fn(j)
            return c
        lax.fori_loop(0, blk_nmb_ref[blk], body, 0)

    def start_gather(blk):
        for_each_micro(blk, lambda j: micro_copy(blk, blk % 2, j, True).start())

    def wait_gather(blk):
        for_each_micro(blk, lambda j: micro_copy(blk, blk % 2, j, True).wait())

    def start_scatter(blk):
        for_each_micro(blk, lambda j: micro_copy(blk, blk % 2, j, False).start())

    def wait_scatter(blk):
        for_each_micro(blk, lambda j: micro_copy(blk, blk % 2, j, False).wait())

    @pl.when(i == 0)
    def _():
        xbuf[...] = jnp.zeros_like(xbuf)
        start_gather(0)

    @pl.when(blk_nmb_ref[i] > 0)
    def _():
        slot = i % 2

        @pl.when(i + 1 < n_blk)
        def _():
            start_gather(i + 1)

        wait_gather(i)

        @pl.when(i >= 2)
        def _():
            wait_scatter(i - 2)

        x = xbuf[slot]
        act = (_silu(_dot(x, wg_ref[...])) * _dot(x, wu_ref[...])).astype(BF16)
        ybuf[slot] = _dot(act, wd_ref[...]).astype(BF16)
        start_scatter(i)

        is_last = jnp.logical_or(i + 1 == n_blk, blk_nmb_ref[jnp.minimum(i + 1, n_blk - 1)] == 0)

        @pl.when(is_last)
        def _():
            @pl.when(i >= 1)
            def _():
                wait_scatter(i - 1)
            wait_scatter(i)


def _expert_ffn(xs, blk_e, blk_nmb, src, wg, wu, wd):
    rows, d = xs.shape
    n_blk = blk_e.shape[0]
    de = wg.shape[2]
    grid_spec = pltpu.PrefetchScalarGridSpec(
        num_scalar_prefetch=3,
        grid=(n_blk,),
        in_specs=[pl.BlockSpec(memory_space=pl.ANY),
                  pl.BlockSpec((None, d, de), lambda i, be, bn, sr: (be[i], 0, 0)),
                  pl.BlockSpec((None, d, de), lambda i, be, bn, sr: (be[i], 0, 0)),
                  pl.BlockSpec((None, de, d), lambda i, be, bn, sr: (be[i], 0, 0))],
        out_specs=pl.BlockSpec(memory_space=pl.ANY),
        scratch_shapes=[pltpu.VMEM((2, EXP_BM, d), BF16), pltpu.VMEM((2, EXP_BM, d), BF16),
                        pltpu.SemaphoreType.DMA((2,)), pltpu.SemaphoreType.DMA((2,))],
    )
    return pl.pallas_call(
        _expert_kernel,
        grid_spec=grid_spec,
        out_shape=jax.ShapeDtypeStruct((rows, d), BF16),
        input_output_aliases={3: 0},
        compiler_params=_cparams(("arbitrary",)),
        name="moe_experts",
    )(blk_e, blk_nmb, src, xs, wg, wu, wd)


def _combine_kernel(ys_ref, sel_ref, wd_ref, h2_ref, x1_ref, wsg_ref, wsu_ref, wsd_ref, npost_ref, gate2_ref,
                    o_ref, c_ref):
    tm = h2_ref.shape[0]
    sel = sel_ref[...]
    hi, lo, cnt, offmb = _tile_positions(sel)
    hit = hi.T.astype(BF16)
    lot = lo.T.astype(BF16)
    wt = wd_ref[...].T
    wt_hi = wt.astype(BF16)
    wt_lo = (wt - wt_hi.astype(F32)).astype(BF16)
    start = offmb[:, :LANES] * MB_ROWS
    end = start + cnt[:, :LANES]
    for cc in range(TILE_ROWS // LANES):
        r_e = (cc * LANES + _iota((N_EXPERTS, LANES), 1)).astype(F32)
        owner = ((r_e >= start) & (r_e < end)).astype(BF16)
        r_t = cc * LANES + _iota((tm, LANES), 1)
        match = ((_dot(hit, owner) == (r_t >> _MB_SHIFT).astype(F32))
                 & (_dot(lot, owner) == (r_t & (MB_ROWS - 1)).astype(F32)))
        w_sel = _dot(wt_hi, owner) + _dot(wt_lo, owner)
        c_ref[:, cc * LANES:(cc + 1) * LANES] = jnp.where(match, w_sel, 0.0).astype(BF16)
    routed = _dot(c_ref[...], ys_ref[...])
    h2 = h2_ref[...]
    shared = _dot((_silu(_dot(h2, wsg_ref[...])) * _dot(h2, wsu_ref[...])).astype(BF16), wsd_ref[...])
    o_ref[...] = x1_ref[...] + gate2_ref[...] * _rms(routed + shared, npost_ref[...])


def _combine(ys, sel_t, wd_t, h2, x1, wsg, wsu, wsd, npost, gate2, seq_len):
    t, d = h2.shape
    tm = MOE_TM
    nt = t // tm
    per_seq = seq_len // tm
    ds = wsg.shape[1]
    const = functools.partial(pl.BlockSpec, pipeline_mode=pl.Buffered(1))
    emap = pl.BlockSpec((N_EXPERTS, tm), lambda i: (0, i))
    return pl.pallas_call(
        _combine_kernel,
        grid=(nt,),
        in_specs=[pl.BlockSpec((TILE_ROWS, d), lambda i: (i, 0)), emap, emap,
                  pl.BlockSpec((tm, d), lambda i: (i, 0)), pl.BlockSpec((tm, d), lambda i: (i, 0)),
                  const((d, ds), lambda i: (0, 0)), const((d, ds), lambda i: (0, 0)), const((ds, d), lambda i: (0, 0)),
                  const((1, d), lambda i: (0, 0)),
                  pl.BlockSpec((None, 1, d), lambda i: (i // per_seq, 0, 0))],
        out_specs=pl.BlockSpec((tm, d), lambda i: (i, 0)),
        out_shape=jax.ShapeDtypeStruct((t, d), F32),
        scratch_shapes=[pltpu.VMEM((tm, TILE_ROWS), BF16)],
        compiler_params=_cparams(("arbitrary",)),
        name="moe_combine",
    )(ys, sel_t, wd_t, h2, x1, wsg, wsu, wsd, npost, gate2)


def _rope_tables(n):
    rows = n // GRID_W
    pos_r = jnp.repeat(jnp.arange(rows, dtype=F32), GRID_W)
    pos_c = jnp.tile(jnp.arange(GRID_W, dtype=F32), rows)
    n_freq = HEAD_DIM // 4
    inv = ROPE_BASE ** (-jnp.arange(n_freq, dtype=F32) / n_freq)
    ang = jnp.concatenate([pos_r[:, None] * inv, pos_c[:, None] * inv], axis=-1)
    cos, sin = jnp.cos(ang), jnp.sin(ang)
    return jnp.concatenate([cos, cos], axis=-1), jnp.concatenate([-sin, sin], axis=-1)


def kernel(x, c, ctx, c_ctx, w_mod, b_mod, norm_mix_pre, norm_mix_post, norm_ffn_pre, norm_ffn_post, w_in, gdn_conv, ret_log_decay, gdn_a_log, gdn_dt_bias, ret_gn_w, gdn_norm_w, w_ret_out, w_gdn_out, w_o, w_router, router_bias, w_gate, w_up, w_down, w_sh_gate, w_sh_up, w_sh_down):
    b, n, d = x.shape
    depth = w_mod.shape[0]
    assert depth == 1, "single-layer block"
    assert n % max(PROJ_TM, MOE_TM) == 0 and ctx.shape[1] % CHUNK == 0
    assert _TILE_MB < 255

    rows = -(-(b + 1) // 8) * 8
    cvec = jnp.zeros((rows, d), F32).at[:b].set(c).at[b].set(c_ctx)
    mod = _modulation(cvec, w_mod[0], b_mod[0][None, :])
    sh1, sc1, g1, sh2, sc2, g2 = [mod[:b, k * d:(k + 1) * d][:, None, :] for k in range(6)]
    ctx_shift = jnp.broadcast_to(mod[b, 0:d][None, None, :], (b, 1, d))
    ctx_scale = jnp.broadcast_to(mod[b, d:2 * d][None, None, :], (b, 1, d))

    w_in0 = w_in[0]
    n_main = 4 * MIX_W
    w_state = w_in0[:, :n_main].astype(BF16)
    gab_cols = n_main + jnp.array([ab * 2 * N_HEADS + dr * N_HEADS + hh
                                   for hh in range(N_HEADS) for ab in range(2) for dr in range(2)], I32)
    w_gab = w_in0[:, gab_cols].astype(BF16)
    n_state = n_main + N_GAB
    w_query = w_in0[:, n_state:].astype(BF16)
    w_all = jnp.concatenate([w_state, w_query], axis=1)
    a_coef = -jnp.exp(gdn_a_log[0].astype(F32))
    dtb = gdn_dt_bias[0].astype(F32)
    zeros_h = jnp.zeros((N_HEADS,), F32)
    coef16 = jnp.stack([a_coef[0], a_coef[1], zeros_h, zeros_h], axis=1).reshape(-1)
    dtb16 = jnp.stack([dtb[0], dtb[1], zeros_h, zeros_h], axis=1).reshape(-1)
    prow = jnp.stack([coef16, dtb16], axis=0)
    pad4 = lambda a: jnp.pad(a.reshape(N_HEADS, 4, -1), ((0, 0), (0, 4), (0, 0))).reshape(N_HEADS * 8, -1)
    w_gab_t = pad4(w_gab.T)
    pcol = pad4(prow.T)
    gain_mix = norm_mix_pre[0][None, :]
    cos2, sin2 = _rope_tables(n)

    state_kinds = (_ROPE_SCALED, _PLAIN, _PLAIN, _PLAIN)
    query_kinds = (_ROPE, _PLAIN, _PLAIN, _PLAIN, _SIGMOID, _SIGMOID, _SIGMOID, _SIGMOID)
    lg = ret_log_decay[0].astype(F32)
    conv = gdn_conv[0].astype(F32)

    lc = ctx.shape[1]
    cfe, (cbase, ceg, cet, cgrow, cgl) = _in_projection(
        ctx, gain_mix, ctx_scale, ctx_shift, w_state, w_gab, w_gab_t, prow, pcol,
        cos2[:lc], sin2[:lc], state_kinds, rope=False)
    cfeats = dict(zip(("rk", "rv", "gk", "gv"), cfe))
    zero_state = jnp.zeros((b, N_HEADS, 4, HEAD_DIM, HEAD_DIM), F32)
    init = _mixers(lg, cgl, cfeats, (cbase, ceg, cet, cgrow), conv, zero_state, None, None, with_query=False)

    fe, (base, eg, et, grow, gl) = _in_projection(
        x, gain_mix, sc1, sh1, w_all, w_gab, w_gab_t, prow, pcol, cos2, sin2,
        state_kinds + query_kinds, rope=True)
    feats = dict(zip(("rk", "rv", "gk", "gv", "rq", "rg", "gq", "gz"), fe[:8]))
    gates = fe[8:]
    y_ret, y_gdn = _mixers(lg, gl, feats, (base, eg, et, grow), conv, init,
                           ret_gn_w[0][None, :], gdn_norm_w[0][None, :], with_query=True)
    x1, h2, sel_t, wd_t = _mixer_out(
        x, y_ret, y_gdn, gates, w_ret_out[0].astype(BF16), w_gdn_out[0].astype(BF16), w_o[0].astype(BF16),
        norm_mix_post[0][None, :], g1, norm_ffn_pre[0][None, :], sc2, sh2,
        w_router[0].T.astype(F32), router_bias[0].astype(F32)[:, None])

    t = b * n
    h2f = h2.reshape(t, d)
    xs, cnt = _dispatch(h2f, sel_t)
    nt = t // MOE_TM
    n_blk = nt * _TILE_MB // EXP_MB + N_EXPERTS
    blk_e, blk_nmb, src = _expert_plan(cnt[:, 0, :].astype(I32), n_blk)
    ys = _expert_ffn(xs, blk_e, blk_nmb, src, w_gate[0].astype(BF16), w_up[0].astype(BF16), w_down[0].astype(BF16))
    out = _combine(ys, sel_t, wd_t, h2f, x1.reshape(t, d), w_sh_gate[0].astype(BF16), w_sh_up[0].astype(BF16),
                   w_sh_down[0].astype(BF16), norm_ffn_post[0][None, :], g2, n)
    return out.reshape(b, n, d)
```

```python
import functools
import math

import jax
import jax.numpy as jnp
from jax import lax
from jax.experimental import pallas as pl
from jax.experimental.pallas import tpu as pltpu

F32 = jnp.float32
BF16 = jnp.bfloat16
I32 = jnp.int32
HIGHEST = lax.Precision.HIGHEST

N_HEADS = 4
HEAD_DIM = 128
MIX_W = N_HEADS * HEAD_DIM
CHUNK = 128
SHORT_CONV = 3
ROPE_BASE = 10000.0
GRID_W = 64
N_EXPERTS = 64
TOP_K = 8
N_GROUPS = 8
TOPK_GROUPS = 4
GROUP_SIZE = N_EXPERTS // N_GROUPS
ROUTED_SCALE = 2.5
EPS = 1e-6
N_GAB = 4 * N_HEADS

LANES = 128
BF16_TILE_ROWS = 16
VMEM_LIMIT_BYTES = 56 * 1024 * 1024

PROJ_TM = 256
MOE_TM = 256
MB_ROWS = BF16_TILE_ROWS
_MB_SHIFT = MB_ROWS.bit_length() - 1
_TILE_MB = -(-(TOP_K * MOE_TM // MB_ROWS + N_EXPERTS * (MB_ROWS - 1) // MB_ROWS + 1) // 8) * 8
TILE_ROWS = _TILE_MB * MB_ROWS
EXP_BM = 256
EXP_MB = EXP_BM // MB_ROWS


def _cparams(sem):
    return pltpu.CompilerParams(dimension_semantics=sem, vmem_limit_bytes=VMEM_LIMIT_BYTES)


def _silu(v):
    return v * jax.nn.sigmoid(v)


def _softplus(v):
    return jnp.maximum(v, 0.0) + jnp.log1p(jnp.exp(-jnp.abs(v)))


def _iota(shape, dim):
    return lax.broadcasted_iota(I32, shape, dim)


def _dot(a, b, **kw):
    return jnp.dot(a, b, preferred_element_type=F32, **kw)


def _dot_nt(a, b, **kw):
    return lax.dot_general(a, b, (((1,), (1,)), ((), ())), preferred_element_type=F32, **kw)


def _dot_tn(a, b, **kw):
    return lax.dot_general(a, b, (((0,), (0,)), ((), ())), preferred_element_type=F32, **kw)


def _rms(v, gain):
    return v * lax.rsqrt(jnp.mean(v * v, axis=-1, keepdims=True) + EPS) * gain


def _mod_kernel(c_ref, w_ref, b_ref, o_ref):
    o_ref[...] = _dot(_silu(c_ref[...]), w_ref[...], precision=HIGHEST) + b_ref[...]


def _modulation(cvec, w_mod, b_mod):
    rows, d = cvec.shape
    n = w_mod.shape[1]
    tn = 1024
    return pl.pallas_call(
        _mod_kernel,
        grid=(n // tn,),
        in_specs=[pl.BlockSpec((rows, d), lambda j: (0, 0)),
                  pl.BlockSpec((d, tn), lambda j: (0, j)),
                  pl.BlockSpec((1, tn), lambda j: (0, j))],
        out_specs=pl.BlockSpec((rows, tn), lambda j: (0, j)),
        out_shape=jax.ShapeDtypeStruct((rows, n), F32),
        compiler_params=_cparams(("arbitrary",)),
        name="adaln_modulation",
    )(cvec, w_mod, b_mod)


_PLAIN, _ROPE, _ROPE_SCALED, _SIGMOID = 0, 1, 2, 3


def _proj_kernel(x_ref, gain_ref, sc_ref, sh_ref, w_ref, wg_ref, wgt_ref, prow_ref, pcol_ref, cos_ref, sin_ref,
                 *out_refs, kinds, rope):
    n_feat = len(kinds)
    feat_refs = out_refs[:n_feat]
    cols_ref, grow_ref, gl_ref = out_refs[n_feat:]
    tm = x_ref.shape[0]

    x = x_ref[...]
    h = (_rms(x, gain_ref[...]) * (1.0 + sc_ref[...]) + sh_ref[...]).astype(BF16)

    if rope:
        cos2 = cos_ref[...]
        sin2 = sin_ref[...]

    for g, kind in enumerate(kinds):
        p = _dot(h, w_ref[:, g * MIX_W:(g + 1) * MIX_W])
        if kind == _SIGMOID:
            p = jax.nn.sigmoid(p)
        elif kind in (_ROPE, _ROPE_SCALED) and rope:
            heads = []
            for hh in range(N_HEADS):
                t = p[:, hh * HEAD_DIM:(hh + 1) * HEAD_DIM]
                heads.append(t * cos2 + pltpu.roll(t, HEAD_DIM // 2, 1) * sin2)
            p = jnp.concatenate(heads, axis=1)
        if kind == _ROPE_SCALED:
            p = p * (HEAD_DIM ** -0.5)
        feat_refs[g][...] = p.astype(feat_refs[g].dtype)

    r_i = _iota((CHUNK, CHUNK), 0)
    c_i = _iota((CHUNK, CHUNK), 1)
    lower_incl = (c_i <= r_i).astype(F32)
    upper_incl = (c_i >= r_i).astype(F32)

    colt = _iota((tm, 8), 1)
    colc = _iota((CHUNK, 8), 1)
    fwd_col = (colc & 1) == 0
    for hh in range(N_HEADS):
        pg = _dot(h, wg_ref[hh])
        par = prow_ref[hh]
        la = jnp.where((colt == 2) | (colt == 3), 0.0, par[0:1, :] * _softplus(pg + par[1:2, :]))
        beta = jax.nn.sigmoid(pg)
        for c in range(tm // CHUNK):
            sl = slice(c * CHUNK, (c + 1) * CHUNK)
            la_c = la[sl]
            pre = _dot(lower_incl, la_c, precision=HIGHEST)
            suf = _dot(upper_incl, la_c, precision=HIGHEST)
            g_c = jnp.where(fwd_col, pre, suf)
            rest = jnp.where(fwd_col, suf, pre) - la_c
            cols_ref[hh, sl, :] = jnp.where(colc < 2, g_c, jnp.where(colc < 4, beta[sl], jnp.where(
                colc < 6, jnp.exp(g_c), jnp.exp(rest))))
            gl_ref[c, hh:hh + 1, :] = jnp.exp(g_c[0:1, :] + rest[0:1, :])

    pgt = _dot_nt(wgt_ref[...], h)
    rowq = _iota((N_HEADS * 8, tm), 0) & 7
    lat = jnp.where(rowq < 2, pcol_ref[:, 0:1] * _softplus(pgt + pcol_ref[:, 1:2]), 0.0)
    rowc = _iota((N_HEADS * 8, CHUNK), 0) & 7
    for c in range(tm // CHUNK):
        sl = slice(c * CHUNK, (c + 1) * CHUNK)
        lat_c = lat[:, sl]
        pre_t = _dot(lat_c, upper_incl, precision=HIGHEST)
        suf_t = _dot(lat_c, lower_incl, precision=HIGHEST)
        grow_ref[:, :, sl] = jnp.where(rowc == 0, pre_t, suf_t).reshape(N_HEADS, 8, CHUNK)


def _in_projection(x, gain, scale, shift, w_main, w_gab, w_gab_t, prow, pcol, cos2, sin2, kinds, rope):
    b, l, d = x.shape
    tm = min(PROJ_TM, l)
    tiles = l // tm
    n_chunk = tm // CHUNK
    feat_shapes = [jax.ShapeDtypeStruct((b, l, MIX_W), BF16) for _ in kinds]
    feat_specs = [pl.BlockSpec((None, tm, MIX_W), lambda i, j: (i, j, 0)) for _ in kinds]
    out_shape = feat_shapes + [jax.ShapeDtypeStruct((b, N_HEADS, l, 8), F32),
                               jax.ShapeDtypeStruct((b, N_HEADS, 8, l), F32),
                               jax.ShapeDtypeStruct((b, tiles, n_chunk, N_HEADS, 8), F32)]
    out_specs = feat_specs + [pl.BlockSpec((None, N_HEADS, tm, 8), lambda i, j: (i, 0, j, 0)),
                              pl.BlockSpec((None, N_HEADS, 8, tm), lambda i, j: (i, 0, 0, j)),
                              pl.BlockSpec((None, None, n_chunk, N_HEADS, 8), lambda i, j: (i, j, 0, 0, 0))]
    const = functools.partial(pl.BlockSpec, pipeline_mode=pl.Buffered(1))
    ncol = w_main.shape[1]
    in_specs = [
        pl.BlockSpec((None, tm, d), lambda i, j: (i, j, 0)),
        const((1, d), lambda i, j: (0, 0)),
        pl.BlockSpec((None, 1, d), lambda i, j: (i, 0, 0)),
        pl.BlockSpec((None, 1, d), lambda i, j: (i, 0, 0)),
        const((d, ncol), lambda i, j: (0, 0)),
        const((N_HEADS, d, 8), lambda i, j: (0, 0, 0)),
        const((N_HEADS * 8, d), lambda i, j: (0, 0)),
        const((N_HEADS, 2, 8), lambda i, j: (0, 0, 0)),
        const((N_HEADS * 8, 2), lambda i, j: (0, 0)),
        pl.BlockSpec((tm, HEAD_DIM), lambda i, j: (j, 0)),
        pl.BlockSpec((tm, HEAD_DIM), lambda i, j: (j, 0)),
    ]
    outs = pl.pallas_call(
        functools.partial(_proj_kernel, kinds=tuple(kinds), rope=rope),
        grid=(b, tiles),
        in_specs=in_specs,
        out_specs=out_specs,
        out_shape=out_shape,
        compiler_params=_cparams(("arbitrary", "arbitrary")),
        name="in_projection_rope" if rope else "in_projection_ctx",
    )(x, gain, scale, shift, w_main, w_gab, w_gab_t, prow, pcol, cos2, sin2)
    feats = outs[:len(kinds)]
    cols, grow, gl = outs[len(kinds):]
    return feats, (cols, grow, gl.reshape(b * (l // CHUNK), N_HEADS * 8))


def _unit_triangular_inverses(mats):
    r = _iota((CHUNK, CHUNK), 0)
    c = _iota((CHUNK, CHUNK), 1)
    eye = (r == c).astype(F32)
    invs = [eye - jnp.where((r >> 1) == (c >> 1), a, 0.0) for a in mats]
    for level in range(1, int(math.log2(CHUNK))):
        mask = ((r >> (level + 1)) == (c >> (level + 1))) & ((r >> level) != (c >> level))
        invb = [inv.astype(BF16) for inv in invs]
        half = [_dot(jnp.where(mask, a, 0.0).astype(BF16), ib).astype(BF16) for a, ib in zip(mats, invb)]
        invs = [inv - _dot(ib, hf) for inv, ib, hf in zip(invs, invb, half)]
    return invs


def _mixer_kernel(*refs, seq_len, with_query):
    n_chunk = seq_len // CHUNK
    if with_query:
        (lg_ref, gl_ref, rk_ref, rv_ref, gk_ref, gv_ref, rq_ref, gq_ref, rg_ref, gz_ref,
         cols_ref, grow_ref, cq_ref, ck_ref, cv_ref, s0_ref, gnw_ref, rmsw_ref,
         yret_ref, ygdn_ref,
         gks, gvs, gqs, oret, ogdn, ubuf, wbuf, pbuf, kvbuf, rtile, state) = refs
    else:
        (lg_ref, gl_ref, rk_ref, rv_ref, gk_ref, gv_ref,
         cols_ref, grow_ref, ck_ref, cv_ref, s0_ref,
         sfin_ref,
         gks, gvs, ubuf, wbuf, kvbuf, rtile, state) = refs
    bi = pl.program_id(0)
    hi = pl.program_id(1)

    row = _iota((CHUNK, CHUNK), 0)
    colm = _iota((CHUNK, CHUNK), 1)
    rowf = row.astype(F32)
    colf = colm.astype(F32)

    def conv_chunk(src_ref, w_ref, n):
        s = pl.multiple_of(n * CHUNK, CHUNK)
        x = src_ref[pl.ds(s, CHUNK), :].astype(F32)
        ps = pl.multiple_of(jnp.maximum(s - BF16_TILE_ROWS, 0), BF16_TILE_ROWS)
        ns = pl.multiple_of(jnp.minimum(s + CHUNK, seq_len - BF16_TILE_ROWS), BF16_TILE_ROWS)
        prev_row = src_ref[pl.ds(ps, BF16_TILE_ROWS), :].astype(F32)[BF16_TILE_ROWS - 1:BF16_TILE_ROWS, :]
        next_row = src_ref[pl.ds(ns, BF16_TILE_ROWS), :].astype(F32)[0:1, :]
        prev_row = prev_row * jnp.where(n > 0, 1.0, 0.0)
        next_row = next_row * jnp.where(n < n_chunk - 1, 1.0, 0.0)
        xp = jnp.where(row == 0, jnp.broadcast_to(prev_row, (CHUNK, HEAD_DIM)), pltpu.roll(x, 1, 0))
        xn = jnp.where(row == CHUNK - 1, jnp.broadcast_to(next_row, (CHUNK, HEAD_DIM)), pltpu.roll(x, CHUNK - 1, 0))
        return _silu(w_ref[0:1, :] * xp + w_ref[1:2, :] * x + w_ref[2:3, :] * xn)

    def l2n(v):
        return v * lax.rsqrt(jnp.sum(v * v, axis=-1, keepdims=True) + EPS)

    def preprocess(n, carry):
        s = pl.multiple_of(n * CHUNK, CHUNK)
        gks[pl.ds(s, CHUNK), :] = l2n(conv_chunk(gk_ref, ck_ref, n)).astype(BF16)
        gvs[pl.ds(s, CHUNK), :] = conv_chunk(gv_ref, cv_ref, n).astype(BF16)
        if with_query:
            gqs[pl.ds(s, CHUNK), :] = (l2n(conv_chunk(gq_ref, cq_ref, n)) * (HEAD_DIM ** -0.5)).astype(BF16)
            oret[pl.ds(s, CHUNK), :] = jnp.zeros((CHUNK, HEAD_DIM), F32)
            ogdn[pl.ds(s, CHUNK), :] = jnp.zeros((CHUNK, HEAD_DIM), F32)
        return carry

    lax.fori_loop(0, n_chunk, preprocess, 0)

    for d in range(2):
        lg = lg_ref[d, hi]
        if d == 0:
            dist, pos_q, pos_k = rowf - colf, rowf + 1.0, (CHUNK - 1.0) - rowf
        else:
            dist, pos_q, pos_k = colf - rowf, CHUNK - rowf, rowf
        rtile[4 * d + 0] = jnp.where(dist >= 0, jnp.exp(lg * jnp.maximum(dist, 0.0)), 0.0)
        rtile[4 * d + 1] = jnp.exp(lg * pos_q)
        rtile[4 * d + 2] = jnp.exp(lg * pos_k)
        rtile[4 * d + 3] = jnp.exp(lg * jnp.full((CHUNK, CHUNK), float(CHUNK), F32))

    state[...] = s0_ref[...]

    def bcast_col(cols, j):
        return jnp.broadcast_to(cols[:, j:j + 1], (CHUNK, CHUNK))

    pre_chunks = min(4, n_chunk)

    def prepass(m, carry):
        jobs = [(m * pre_chunks + j, d) for j in range(pre_chunks) for d in range(2)]
        css = [pl.ds(pl.multiple_of(n * CHUNK, CHUNK), CHUNK) for n, _ in jobs]
        slots = [d * n_chunk + n for n, d in jobs]
        dirs = [d for _, d in jobs]
        ks = [rk_ref[cs, :] for cs in css]
        vs = [rv_ref[cs, :] for cs in css]
        if with_query:
            scs = [(_dot_nt(rq_ref[cs, :], k) * rtile[4 * d + 0]).astype(BF16) for cs, k, d in zip(css, ks, dirs)]
            outs = [_dot(sc, v) for sc, v in zip(scs, vs)]
            for cs, o in zip(css, outs):
                oret[cs, :] += o
        kvs = [_dot_tn((k.astype(F32) * rtile[4 * d + 2]).astype(BF16), v) for k, v, d in zip(ks, vs, dirs)]
        for slot, kv in zip(slots, kvs):
            kvbuf[slot] = kv
        ks = [gks[cs, :] for cs in css]
        vs = [gvs[cs, :] for cs in css]
        colss = [cols_ref[cs, :] for cs in css]
        betas = [bcast_col(cols, 2 + d) for cols, d in zip(colss, dirs)]
        incls = [(row >= colm) if d == 0 else (row <= colm) for d in dirs]
        stricts = [(row > colm) if d == 0 else (row < colm) for d in dirs]
        decs = [jnp.exp(jnp.where(incl, bcast_col(cols, d) - jnp.broadcast_to(grow_ref[d:d + 1, cs], (CHUNK, CHUNK)), 0.0))
                for cols, d, cs, incl in zip(colss, dirs, css, incls)]
        kks = [_dot_nt(k, k) for k in ks]
        mats = [kk * beta * jnp.where(strict, dec, 0.0) for kk, beta, strict, dec in zip(kks, betas, stricts, decs)]
        tinvs = [t.astype(BF16) for t in _unit_triangular_inverses(mats)]
        rhs_v = [(beta * v.astype(F32)).astype(BF16) for beta, v in zip(betas, vs)]
        rhs_k = [(beta * bcast_col(cols, 4 + d) * k.astype(F32)).astype(BF16)
                 for beta, cols, d, k in zip(betas, colss, dirs, ks)]
        us = [_dot(t, r) for t, r in zip(tinvs, rhs_v)]
        ws = [_dot(t, r) for t, r in zip(tinvs, rhs_k)]
        for slot, u, w in zip(slots, us, ws):
            ubuf[slot] = u
            wbuf[slot] = w.astype(BF16)
        if with_query:
            qks = [_dot_nt(gqs[cs, :], k) for cs, k in zip(css, ks)]
            for slot, qk, incl, dec in zip(slots, qks, incls, decs):
                pbuf[slot] = (qk * jnp.where(incl, dec, 0.0)).astype(BF16)
        return carry

    lax.fori_loop(0, n_chunk // pre_chunks, prepass, 0)

    def scan_step(n, carry):
        for d in range(2):
            nd = n if d == 0 else n_chunk - 1 - n
            s = pl.multiple_of(nd * CHUNK, CHUNK)
            cs = pl.ds(s, CHUNK)
            slot = d * n_chunk + nd
            st = state[d]
            if with_query:
                oret[cs, :] += _dot(rq_ref[cs, :], st.astype(BF16)) * rtile[4 * d + 1]
            state[d] = rtile[4 * d + 3] * st + kvbuf[slot]

            st = state[2 + d]
            stb = st.astype(BF16)
            cols = cols_ref[cs, :]
            vnb = (ubuf[slot] - _dot(wbuf[slot], stb)).astype(BF16)
            if with_query:
                qe = (gqs[cs, :].astype(F32) * bcast_col(cols, 4 + d)).astype(BF16)
                ogdn[cs, :] += _dot(pbuf[slot], vnb) + _dot(qe, stb)
            kt = (gks[cs, :].astype(F32) * bcast_col(cols, 6 + d)).astype(BF16)
            state[2 + d] = gl_ref[bi * n_chunk + nd, 8 * hi + d] * st + _dot_tn(kt, vnb)
        return carry

    lax.fori_loop(0, n_chunk, scan_step, 0)

    if not with_query:
        sfin_ref[...] = state[...]
        return

    def finish(n, carry):
        s = pl.multiple_of(n * CHUNK, CHUNK)
        cs = pl.ds(s, CHUNK)
        ro = oret[cs, :]
        rc = ro - jnp.mean(ro, axis=-1, keepdims=True)
        ry = rc * lax.rsqrt(jnp.mean(rc * rc, axis=-1, keepdims=True) + EPS)
        yret_ref[cs, :] = (ry * gnw_ref[...] * _silu(rg_ref[cs, :].astype(F32))).astype(BF16)
        go = ogdn[cs, :]
        gy = go * lax.rsqrt(jnp.mean(go * go, axis=-1, keepdims=True) + EPS)
        ygdn_ref[cs, :] = (gy * rmsw_ref[...] * _silu(gz_ref[cs, :].astype(F32))).astype(BF16)
        return carry

    lax.fori_loop(0, n_chunk, finish, 0)


def _mixers(lg, gl, feats, dec, conv, s0, gnw, rmsw, with_query):
    cols, grow = dec
    b, l, _ = feats["rk"].shape
    n_slot = 2 * (l // CHUNK)
    assert (l // CHUNK) % min(4, l // CHUNK) == 0
    smem = pl.BlockSpec(memory_space=pltpu.SMEM)
    head = pl.BlockSpec((None, l, HEAD_DIM), lambda i, j: (i, 0, j))
    colspec = pl.BlockSpec((None, None, l, 8), lambda i, j: (i, j, 0, 0))
    rowspec = pl.BlockSpec((None, None, 8, l), lambda i, j: (i, j, 0, 0))
    tile_f32 = pltpu.VMEM((n_slot, CHUNK, CHUNK), F32)
    tile_bf16 = pltpu.VMEM((n_slot, CHUNK, CHUNK), BF16)
    st_spec = pl.BlockSpec((None, None, 4, HEAD_DIM, HEAD_DIM), lambda i, j: (i, j, 0, 0, 0))

    def conv_spec(which):
        return pl.BlockSpec((SHORT_CONV, HEAD_DIM), lambda i, j, w=which: (0, w * N_HEADS + j))

    def gain_spec():
        return pl.BlockSpec((1, HEAD_DIM), lambda i, j: (0, j))

    if with_query:
        args = [lg, gl, feats["rk"], feats["rv"], feats["gk"], feats["gv"], feats["rq"], feats["gq"], feats["rg"],
                feats["gz"], cols, grow, conv, conv, conv, s0, gnw, rmsw]
        in_specs = [smem, smem] + [head] * 8 + [colspec, rowspec, conv_spec(0), conv_spec(1), conv_spec(2),
                                                st_spec, gain_spec(), gain_spec()]
        out_shape = [jax.ShapeDtypeStruct((b, l, MIX_W), BF16)] * 2
        out_specs = [head, head]
        scratch = ([pltpu.VMEM((l, HEAD_DIM), BF16)] * 3 + [pltpu.VMEM((l, HEAD_DIM), F32)] * 2
                   + [tile_f32, tile_bf16, tile_bf16, tile_f32])
    else:
        args = [lg, gl, feats["rk"], feats["rv"], feats["gk"], feats["gv"], cols, grow, conv, conv, s0]
        in_specs = [smem, smem] + [head] * 4 + [colspec, rowspec, conv_spec(1), conv_spec(2), st_spec]
        out_shape = jax.ShapeDtypeStruct((b, N_HEADS, 4, HEAD_DIM, HEAD_DIM), F32)
        out_specs = st_spec
        scratch = [pltpu.VMEM((l, HEAD_DIM), BF16)] * 2 + [tile_f32, tile_bf16, tile_f32]
    scratch = scratch + [pltpu.VMEM((8, CHUNK, CHUNK), F32), pltpu.VMEM((4, HEAD_DIM, HEAD_DIM), F32)]
    return pl.pallas_call(
        functools.partial(_mixer_kernel, seq_len=l, with_query=with_query),
        grid=(b, N_HEADS),
        in_specs=in_specs,
        out_specs=out_specs,
        out_shape=out_shape,
        scratch_shapes=scratch,
        compiler_params=_cparams(("arbitrary", "arbitrary")),
        name="mixers_latent" if with_query else "mixers_context",
    )(*args)


def _split_bf16(v):
    hi = v.astype(BF16)
    return hi, (v - hi.astype(F32)).astype(BF16)


def _route(h2, wrt_ref, bias_ref, cand_ref, sel_ref, wd_ref):
    tm = h2.shape[0]
    h_hi, h_lo = _split_bf16(h2)
    w_hi, w_lo = _split_bf16(wrt_ref[...])
    logits = _dot_nt(w_hi, h_hi) + (_dot_nt(w_hi, h_lo) + _dot_nt(w_lo, h_hi))
    scores = jax.nn.sigmoid(logits)
    biased = scores + bias_ref[...]
    neg_inf = float("-inf")
    sub = _iota((GROUP_SIZE, tm), 0).astype(F32)
    group_score = []
    for g in range(N_GROUPS):
        blk = biased[g * GROUP_SIZE:(g + 1) * GROUP_SIZE, :]
        m1 = jnp.max(blk, axis=0, keepdims=True)
        first = jnp.min(jnp.where(blk == m1, sub, float(GROUP_SIZE)), axis=0, keepdims=True)
        m2 = jnp.max(jnp.where(sub == first, neg_inf, blk), axis=0, keepdims=True)
        group_score.append(m1 + m2)
    for g in range(N_GROUPS):
        ahead = jnp.zeros((1, tm), I32)
        for g2 in range(N_GROUPS):
            if g2 == g:
                continue
            before = (group_score[g2] > group_score[g])
            if g2 < g:
                before = before | (group_score[g2] == group_score[g])
            ahead = ahead + before.astype(I32)
        keep = jnp.broadcast_to(ahead, (GROUP_SIZE, tm)) < TOPK_GROUPS
        cand_ref[g * GROUP_SIZE:(g + 1) * GROUP_SIZE, :] = jnp.where(
            keep, biased[g * GROUP_SIZE:(g + 1) * GROUP_SIZE, :], neg_inf)
    cand = cand_ref[...]
    eidx = _iota((N_EXPERTS, tm), 0)

    def count_ahead(e2, rank):
        c2 = cand_ref[pl.ds(e2, 1), :]
        before = (c2 > cand) | ((c2 == cand) & (e2 < eidx))
        return rank + before.astype(I32)

    rank = lax.fori_loop(0, N_EXPERTS, count_ahead, jnp.zeros((N_EXPERTS, tm), I32))
    sel = rank < TOP_K
    picked = jnp.where(sel, scores, 0.0)
    wsum = jnp.sum(picked, axis=0, keepdims=True)
    sel_ref[...] = sel.astype(F32)
    wd_ref[...] = picked / wsum * ROUTED_SCALE


def _mixout_kernel(x_ref, yr_ref, yg_ref, g0_ref, g1_ref, g2_ref, g3_ref, wr_ref, wg_ref, wo_ref,
                   npost_ref, gate1_ref, nffn_ref, sc2_ref, sh2_ref, wrt_ref, bias_ref,
                   x1_ref, h2_ref, sel_ref, wd_ref, cand_ref):
    r = _dot(yr_ref[...], wr_ref[...])
    g = _dot(yg_ref[...], wg_ref[...])
    half = r.shape[1] // 2
    merged = jnp.concatenate(
        [g0_ref[...].astype(F32) * r[:, :half] + g2_ref[...].astype(F32) * g[:, :half],
         g1_ref[...].astype(F32) * r[:, half:] + g3_ref[...].astype(F32) * g[:, half:]], axis=1)
    mo = _dot(merged.astype(BF16), wo_ref[...])
    x1 = x_ref[...] + gate1_ref[...] * _rms(mo, npost_ref[...])
    x1_ref[...] = x1
    h2 = _rms(x1, nffn_ref[...]) * (1.0 + sc2_ref[...]) + sh2_ref[...]
    h2_ref[...] = h2.astype(BF16)
    _route(h2, wrt_ref, bias_ref, cand_ref, sel_ref, wd_ref)


def _mixer_out(x, yr, yg, gates, wr, wg, wo, npost, gate1, nffn, sc2, sh2, wrt, bias):
    b, l, d = x.shape
    tm = PROJ_TM
    tiles = l // tm
    tok = lambda w: pl.BlockSpec((None, tm, w), lambda i, j: (i, j, 0))
    const = functools.partial(pl.BlockSpec, pipeline_mode=pl.Buffered(1))
    vec = lambda: const((1, d), lambda i, j: (0, 0))
    bvec = lambda: pl.BlockSpec((None, 1, d), lambda i, j: (i, 0, 0))
    emap = pl.BlockSpec((N_EXPERTS, tm), lambda i, j: (0, i * tiles + j))
    in_specs = ([tok(d), tok(MIX_W), tok(MIX_W)] + [tok(MIX_W)] * 4
                + [const((MIX_W, d), lambda i, j: (0, 0)), const((MIX_W, d), lambda i, j: (0, 0)),
                   const((d, d), lambda i, j: (0, 0)),
                   vec(), bvec(), vec(), bvec(), bvec(),
                   const((N_EXPERTS, d), lambda i, j: (0, 0)), const((N_EXPERTS, 1), lambda i, j: (0, 0))])
    return pl.pallas_call(
        _mixout_kernel,
        grid=(b, tiles),
        in_specs=in_specs,
        out_specs=[tok(d), tok(d), emap, emap],
        out_shape=[jax.ShapeDtypeStruct((b, l, d), F32), jax.ShapeDtypeStruct((b, l, d), BF16),
                   jax.ShapeDtypeStruct((N_EXPERTS, b * l), F32), jax.ShapeDtypeStruct((N_EXPERTS, b * l), F32)],
        scratch_shapes=[pltpu.VMEM((N_EXPERTS, tm), F32)],
        compiler_params=_cparams(("arbitrary", "arbitrary")),
        name="mixer_out_router",
    )(x, yr, yg, *gates, wr, wg, wo, npost, gate1, nffn, sc2, sh2, wrt, bias)


def _tile_positions(sel):
    tm = sel.shape[1]
    selb = sel.astype(BF16)
    earlier = (_iota((tm, tm), 0) < _iota((tm, tm), 1)).astype(BF16)
    rank = _dot(selb, earlier)
    cnt = _dot(selb, jnp.ones((tm, tm), BF16))
    nmb = jnp.floor((cnt + (MB_ROWS - 1)) * (1.0 / MB_ROWS))
    below = (_iota((N_EXPERTS, N_EXPERTS), 1) < _iota((N_EXPERTS, N_EXPERTS), 0)).astype(BF16)
    offmb = _dot(below, nmb.astype(BF16))
    rank_hi = jnp.floor(rank * (1.0 / MB_ROWS))
    hi = jnp.where(sel > 0.0, offmb + rank_hi, 255.0)
    lo = rank - rank_hi * MB_ROWS
    return hi, lo, cnt, offmb


def _dispatch_kernel(h2_ref, sel_ref, xs_ref, cnt_ref):
    tm = h2_ref.shape[0]
    sel = sel_ref[...]
    hi, lo, _, _ = _tile_positions(sel)
    hib = hi.astype(BF16)
    lob = lo.astype(BF16)
    selb = sel.astype(BF16)
    cnt_row = _dot_nt(jnp.ones((8, tm), BF16), selb)
    nmb_row = jnp.floor((cnt_row + (MB_ROWS - 1)) * (1.0 / MB_ROWS))
    before = (_iota((N_EXPERTS, N_EXPERTS), 0) < _iota((N_EXPERTS, N_EXPERTS), 1)).astype(BF16)
    start_row = _dot(nmb_row.astype(BF16), before) * MB_ROWS
    cnt_ref[...] = cnt_row
    start = start_row[0:1, :]
    end = start + cnt_row[0:1, :]
    x = h2_ref[...]
    rch = 256
    for rc in range(TILE_ROWS // rch):
        r_e = (rc * rch + _iota((rch, N_EXPERTS), 0)).astype(F32)
        owner = ((r_e >= start) & (r_e < end)).astype(BF16)
        r_t = rc * rch + _iota((rch, tm), 0)
        match = ((_dot(owner, hib) == (r_t >> _MB_SHIFT).astype(F32))
                 & (_dot(owner, lob) == (r_t & (MB_ROWS - 1)).astype(F32)))
        xs_ref[rc * rch:(rc + 1) * rch, :] = _dot(match.astype(BF16), x).astype(BF16)


def _dispatch(h2, sel_t):
    t, d = h2.shape
    tm = MOE_TM
    nt = t // tm
    return pl.pallas_call(
        _dispatch_kernel,
        grid=(nt,),
        in_specs=[pl.BlockSpec((tm, d), lambda i: (i, 0)), pl.BlockSpec((N_EXPERTS, tm), lambda i: (0, i))],
        out_specs=[pl.BlockSpec((TILE_ROWS, d), lambda i: (i, 0)), pl.BlockSpec((None, 8, N_EXPERTS), lambda i: (i, 0, 0))],
        out_shape=[jax.ShapeDtypeStruct((nt * TILE_ROWS, d), BF16), jax.ShapeDtypeStruct((nt, 8, N_EXPERTS), F32)],
        compiler_params=_cparams(("arbitrary",)),
        name="moe_dispatch",
    )(h2, sel_t)


def _expert_plan(cnt, n_blk):
    nt = cnt.shape[0]
    nmb = (cnt + (MB_ROWS - 1)) // MB_ROWS
    offmb = jnp.cumsum(nmb, axis=1) - nmb
    per_e = nmb.T
    incl = jnp.cumsum(per_e, axis=1)
    excl = incl - per_e
    tot = incl[:, -1]
    nb = (tot + (EXP_MB - 1)) // EXP_MB
    bend = jnp.cumsum(nb)
    bstart = bend - nb
    b = jnp.arange(n_blk, dtype=I32)
    valid = b < bend[-1]
    last = jnp.maximum(bend[-1] - 1, 0)
    bq = jnp.where(valid, b, last)
    blk_e = jnp.minimum(jnp.sum((bend[None, :] <= bq[:, None]).astype(I32), axis=1), N_EXPERTS - 1)
    onehot_e = (blk_e[:, None] == jnp.arange(N_EXPERTS, dtype=I32)[None, :]).astype(I32)
    pick = lambda table: jnp.sum(onehot_e[:, :, None] * table[None, :, :], axis=1)
    bstart_b = jnp.sum(onehot_e * bstart[None, :], axis=1)
    tot_b = jnp.sum(onehot_e * tot[None, :], axis=1)
    p0 = (bq - bstart_b) * EXP_MB
    blk_nmb = jnp.where(valid, jnp.clip(tot_b - p0, 0, EXP_MB), 0).astype(I32)
    p = p0[:, None] + jnp.arange(EXP_MB, dtype=I32)[None, :]
    incl_b = pick(incl)
    tile = jnp.minimum(jnp.sum((incl_b[:, None, :] <= p[:, :, None]).astype(I32), axis=2), nt - 1)
    onehot_t = (tile[:, :, None] == jnp.arange(nt, dtype=I32)[None, None, :]).astype(I32)
    shift_b = pick(offmb.T - excl)[:, None, :]
    src = tile * _TILE_MB + p + jnp.sum(onehot_t * shift_b, axis=2)
    src = jnp.where(jnp.arange(EXP_MB, dtype=I32)[None, :] < blk_nmb[:, None], src, 0)
    return blk_e.astype(I32), blk_nmb, src.reshape(-1).astype(I32)


def _expert_kernel(blk_e_ref, blk_nmb_ref, src_ref, xs_hbm, wg_ref, wu_ref, wd_ref, ys_hbm,
                   xbuf, ybuf, in_sem, out_sem):
    del blk_e_ref
    i = pl.program_id(0)
    n_blk = pl.num_programs(0)

    def micro_copy(blk, slot, j, into_vmem):
        mb = src_ref[blk * EXP_MB + j]
        hbm_rows = pl.ds(pl.multiple_of(mb * MB_ROWS, MB_ROWS), MB_ROWS)
        vmem_rows = pl.ds(pl.multiple_of(j * MB_ROWS, MB_ROWS), MB_ROWS)
        if into_vmem:
            return pltpu.make_async_copy(xs_hbm.at[hbm_rows, :], xbuf.at[slot, vmem_rows, :], in_sem.at[slot])
        return pltpu.make_async_copy(ybuf.at[slot, vmem_rows, :], ys_hbm.at[hbm_rows, :], out_sem.at[slot])

    def for_each_micro(blk, fn):
        def body(j, c):
            fn(j)
            return c
        lax.fori_loop(0, blk_nmb_ref[blk], body, 0)

    def start_gather(blk):
        for_each_micro(blk, lambda j: micro_copy(blk, blk % 2, j, True).start())

    def wait_gather(blk):
        for_each_micro(blk, lambda j: micro_copy(blk, blk % 2, j, True).wait())

    def start_scatter(blk):
        for_each_micro(blk, lambda j: micro_copy(blk, blk % 2, j, False).start())

    def wait_scatter(blk):
        for_each_micro(blk, lambda j: micro_copy(blk, blk % 2, j, False).wait())

    @pl.when(i == 0)
    def _():
        xbuf[...] = jnp.zeros_like(xbuf)
        start_gather(0)

    @pl.when(blk_nmb_ref[i] > 0)
    def _():
        slot = i % 2

        @pl.when(i + 1 < n_blk)
        def _():
            start_gather(i + 1)

        wait_gather(i)

        @pl.when(i >= 2)
        def _():
            wait_scatter(i - 2)

        x = xbuf[slot]
        act = (_silu(_dot(x, wg_ref[...])) * _dot(x, wu_ref[...])).astype(BF16)
        ybuf[slot] = _dot(act, wd_ref[...]).astype(BF16)
        start_scatter(i)

        is_last = jnp.logical_or(i + 1 == n_blk, blk_nmb_ref[jnp.minimum(i + 1, n_blk - 1)] == 0)

        @pl.when(is_last)
        def _():
            @pl.when(i >= 1)
            def _():
                wait_scatter(i - 1)
            wait_scatter(i)


def _expert_ffn(xs, blk_e, blk_nmb, src, wg, wu, wd):
    rows, d = xs.shape
    n_blk = blk_e.shape[0]
    de = wg.shape[2]
    grid_spec = pltpu.PrefetchScalarGridSpec(
        num_scalar_prefetch=3,
        grid=(n_blk,),
        in_specs=[pl.BlockSpec(memory_space=pl.ANY),
                  pl.BlockSpec((None, d, de), lambda i, be, bn, sr: (be[i], 0, 0)),
                  pl.BlockSpec((None, d, de), lambda i, be, bn, sr: (be[i], 0, 0)),
                  pl.BlockSpec((None, de, d), lambda i, be, bn, sr: (be[i], 0, 0))],
        out_specs=pl.BlockSpec(memory_space=pl.ANY),
        scratch_shapes=[pltpu.VMEM((2, EXP_BM, d), BF16), pltpu.VMEM((2, EXP_BM, d), BF16),
                        pltpu.SemaphoreType.DMA((2,)), pltpu.SemaphoreType.DMA((2,))],
    )
    return pl.pallas_call(
        _expert_kernel,
        grid_spec=grid_spec,
        out_shape=jax.ShapeDtypeStruct((rows, d), BF16),
        input_output_aliases={3: 0},
        compiler_params=_cparams(("arbitrary",)),
        name="moe_experts",
    )(blk_e, blk_nmb, src, xs, wg, wu, wd)


def _combine_kernel(ys_ref, sel_ref, wd_ref, h2_ref, x1_ref, wsg_ref, wsu_ref, wsd_ref, npost_ref, gate2_ref,
                    o_ref, c_ref):
    tm = h2_ref.shape[0]
    sel = sel_ref[...]
    hi, lo, cnt, offmb = _tile_positions(sel)
    hit = hi.T.astype(BF16)
    lot = lo.T.astype(BF16)
    wt = wd_ref[...].T
    wt_hi = wt.astype(BF16)
    wt_lo = (wt - wt_hi.astype(F32)).astype(BF16)
    start = offmb[:, :LANES] * MB_ROWS
    end = start + cnt[:, :LANES]
    for cc in range(TILE_ROWS // LANES):
        r_e = (cc * LANES + _iota((N_EXPERTS, LANES), 1)).astype(F32)
        owner = ((r_e >= start) & (r_e < end)).astype(BF16)
        r_t = cc * LANES + _iota((tm, LANES), 1)
        match = ((_dot(hit, owner) == (r_t >> _MB_SHIFT).astype(F32))
                 & (_dot(lot, owner) == (r_t & (MB_ROWS - 1)).astype(F32)))
        w_sel = _dot(wt_hi, owner) + _dot(wt_lo, owner)
        c_ref[:, cc * LANES:(cc + 1) * LANES] = jnp.where(match, w_sel, 0.0).astype(BF16)
    routed = _dot(c_ref[...], ys_ref[...])
    h2 = h2_ref[...]
    shared = _dot((_silu(_dot(h2, wsg_ref[...])) * _dot(h2, wsu_ref[...])).astype(BF16), wsd_ref[...])
    o_ref[...] = x1_ref[...] + gate2_ref[...] * _rms(routed + shared, npost_ref[...])


def _combine(ys, sel_t, wd_t, h2, x1, wsg, wsu, wsd, npost, gate2, seq_len):
    t, d = h2.shape
    tm = MOE_TM
    nt = t // tm
    per_seq = seq_len // tm
    ds = wsg.shape[1]
    const = functools.partial(pl.BlockSpec, pipeline_mode=pl.Buffered(1))
    emap = pl.BlockSpec((N_EXPERTS, tm), lambda i: (0, i))
    return pl.pallas_call(
        _combine_kernel,
        grid=(nt,),
        in_specs=[pl.BlockSpec((TILE_ROWS, d), lambda i: (i, 0)), emap, emap,
                  pl.BlockSpec((tm, d), lambda i: (i, 0)), pl.BlockSpec((tm, d), lambda i: (i, 0)),
                  const((d, ds), lambda i: (0, 0)), const((d, ds), lambda i: (0, 0)), const((ds, d), lambda i: (0, 0)),
                  const((1, d), lambda i: (0, 0)),
                  pl.BlockSpec((None, 1, d), lambda i: (i // per_seq, 0, 0))],
        out_specs=pl.BlockSpec((tm, d), lambda i: (i, 0)),
        out_shape=jax.ShapeDtypeStruct((t, d), F32),
        scratch_shapes=[pltpu.VMEM((tm, TILE_ROWS), BF16)],
        compiler_params=_cparams(("arbitrary",)),
        name="moe_combine",
    )(ys, sel_t, wd_t, h2, x1, wsg, wsu, wsd, npost, gate2)


def _rope_tables(n):
    rows = n // GRID_W
    pos_r = jnp.repeat(jnp.arange(rows, dtype=F32), GRID_W)
    pos_c = jnp.tile(jnp.arange(GRID_W, dtype=F32), rows)
    n_freq = HEAD_DIM // 4
    inv = ROPE_BASE ** (-jnp.arange(n_freq, dtype=F32) / n_freq)
    ang = jnp.concatenate([pos_r[:, None] * inv, pos_c[:, None] * inv], axis=-1)
    cos, sin = jnp.cos(ang), jnp.sin(ang)
    return jnp.concatenate([cos, cos], axis=-1), jnp.concatenate([-sin, sin], axis=-1)


def kernel(x, c, ctx, c_ctx, w_mod, b_mod, norm_mix_pre, norm_mix_post, norm_ffn_pre, norm_ffn_post, w_in, gdn_conv, ret_log_decay, gdn_a_log, gdn_dt_bias, ret_gn_w, gdn_norm_w, w_ret_out, w_gdn_out, w_o, w_router, router_bias, w_gate, w_up, w_down, w_sh_gate, w_sh_up, w_sh_down):
    b, n, d = x.shape
    depth = w_mod.shape[0]
    assert depth == 1, "single-layer block"
    assert n % max(PROJ_TM, MOE_TM) == 0 and ctx.shape[1] % CHUNK == 0
    assert _TILE_MB < 255

    rows = -(-(b + 1) // 8) * 8
    cvec = jnp.zeros((rows, d), F32).at[:b].set(c).at[b].set(c_ctx)
    mod = _modulation(cvec, w_mod[0], b_mod[0][None, :])
    sh1, sc1, g1, sh2, sc2, g2 = [mod[:b, k * d:(k + 1) * d][:, None, :] for k in range(6)]
    ctx_shift = jnp.broadcast_to(mod[b, 0:d][None, None, :], (b, 1, d))
    ctx_scale = jnp.broadcast_to(mod[b, d:2 * d][None, None, :], (b, 1, d))

    w_in0 = w_in[0]
    n_main = 4 * MIX_W
    w_state = w_in0[:, :n_main].astype(BF16)
    slots = ((0, 0), (0, 1), (1, 0), (1, 1), (0, 0), (0, 1), (0, 0), (0, 1))
    gab_cols = n_main + jnp.array([[ab * 2 * N_HEADS + dr * N_HEADS + hh for ab, dr in slots]
                                   for hh in range(N_HEADS)], I32)
    w_gab = jnp.transpose(w_in0[:, gab_cols], (1, 0, 2)).astype(BF16)
    n_state = n_main + N_GAB
    w_query = w_in0[:, n_state:].astype(BF16)
    w_all = jnp.concatenate([w_state, w_query], axis=1)
    a_coef = -jnp.exp(gdn_a_log[0].astype(F32))
    dtb = gdn_dt_bias[0].astype(F32)
    is_alpha = jnp.array([ab == 0 for ab, _ in slots])
    dirs = jnp.array([dr for _, dr in slots], I32)
    prow = jnp.stack([jnp.where(is_alpha[None, :], a_coef.T[:, dirs], 0.0),
                      jnp.where(is_alpha[None, :], dtb.T[:, dirs], 0.0)], axis=1)
    first2 = (jnp.arange(8) < 2)
    w_gab_t = (jnp.transpose(w_gab, (0, 2, 1)) * first2[None, :, None].astype(BF16)).reshape(N_HEADS * 8, d)
    pcol = (jnp.transpose(prow, (0, 2, 1)) * first2[None, :, None]).reshape(N_HEADS * 8, 2)
    gain_mix = norm_mix_pre[0][None, :]
    cos2, sin2 = _rope_tables(n)

    state_kinds = (_ROPE_SCALED, _PLAIN, _PLAIN, _PLAIN)
    query_kinds = (_ROPE, _PLAIN, _PLAIN, _PLAIN, _SIGMOID, _SIGMOID, _SIGMOID, _SIGMOID)
    lg = ret_log_decay[0].astype(F32)
    conv = gdn_conv[0].astype(F32)

    lc = ctx.shape[1]
    cfe, (ccols, cgrow, cgl) = _in_projection(
        ctx, gain_mix, ctx_scale, ctx_shift, w_state, w_gab, w_gab_t, prow, pcol,
        cos2[:lc], sin2[:lc], state_kinds, rope=False)
    cfeats = dict(zip(("rk", "rv", "gk", "gv"), cfe))
    zero_state = jnp.zeros((b, N_HEADS, 4, HEAD_DIM, HEAD_DIM), F32)
    init = _mixers(lg, cgl, cfeats, (ccols, cgrow), conv, zero_state, None, None, with_query=False)

    fe, (cols, grow, gl) = _in_projection(
        x, gain_mix, sc1, sh1, w_all, w_gab, w_gab_t, prow, pcol, cos2, sin2,
        state_kinds + query_kinds, rope=True)
    feats = dict(zip(("rk", "rv", "gk", "gv", "rq", "rg", "gq", "gz"), fe[:8]))
    gates = fe[8:]
    y_ret, y_gdn = _mixers(lg, gl, feats, (cols, grow), conv, init,
                           ret_gn_w[0][None, :], gdn_norm_w[0][None, :], with_query=True)
    x1, h2, sel_t, wd_t = _mixer_out(
        x, y_ret, y_gdn, gates, w_ret_out[0].astype(BF16), w_gdn_out[0].astype(BF16), w_o[0].astype(BF16),
        norm_mix_post[0][None, :], g1, norm_ffn_pre[0][None, :], sc2, sh2,
        w_router[0].T.astype(F32), router_bias[0].astype(F32)[:, None])

    t = b * n
    h2f = h2.reshape(t, d)
    xs, cnt = _dispatch(h2f, sel_t)
    nt = t // MOE_TM
    n_blk = nt * _TILE_MB // EXP_MB + N_EXPERTS
    blk_e, blk_nmb, src = _expert_plan(cnt[:, 0, :].astype(I32), n_blk)
    ys = _expert_ffn(xs, blk_e, blk_nmb, src, w_gate[0].astype(BF16), w_up[0].astype(BF16), w_down[0].astype(BF16))
    out = _combine(ys, sel_t, wd_t, h2f, x1.reshape(t, d), w_sh_gate[0].astype(BF16), w_sh_up[0].astype(BF16),
                   w_sh_down[0].astype(BF16), norm_ffn_post[0][None, :], g2, n)
    return out.reshape(b, n, d)
```

```python
import functools
import math

import jax
import jax.numpy as jnp
from jax import lax
from jax.experimental import pallas as pl
from jax.experimental.pallas import tpu as pltpu

F32 = jnp.float32
BF16 = jnp.bfloat16
I32 = jnp.int32
HIGHEST = lax.Precision.HIGHEST

N_HEADS = 4
HEAD_DIM = 128
MIX_W = N_HEADS * HEAD_DIM
CHUNK = 128
SHORT_CONV = 3
ROPE_BASE = 10000.0
GRID_W = 64
N_EXPERTS = 64
TOP_K = 8
N_GROUPS = 8
TOPK_GROUPS = 4
GROUP_SIZE = N_EXPERTS // N_GROUPS
ROUTED_SCALE = 2.5
EPS = 1e-6
N_GAB = 4 * N_HEADS

LANES = 128
BF16_TILE_ROWS = 16
VMEM_LIMIT_BYTES = 56 * 1024 * 1024

PROJ_TM = 256
MOE_TM = 256
MB_ROWS = BF16_TILE_ROWS
_MB_SHIFT = MB_ROWS.bit_length() - 1
_TILE_MB = -(-(TOP_K * MOE_TM // MB_ROWS + N_EXPERTS * (MB_ROWS - 1) // MB_ROWS + 1) // 8) * 8
TILE_ROWS = _TILE_MB * MB_ROWS
EXP_BM = 512
EXP_MB = EXP_BM // MB_ROWS


def _cparams(sem):
    return pltpu.CompilerParams(dimension_semantics=sem, vmem_limit_bytes=VMEM_LIMIT_BYTES)


def _sigmoid(v):
    return 0.5 * jnp.tanh(0.5 * v) + 0.5


def _silu(v):
    return v * _sigmoid(v)


def _softplus(v):
    return jnp.maximum(v, 0.0) + jnp.log1p(jnp.exp(-jnp.abs(v)))


def _iota(shape, dim):
    return lax.broadcasted_iota(I32, shape, dim)


def _dot(a, b, **kw):
    return jnp.dot(a, b, preferred_element_type=F32, **kw)


def _dot_nt(a, b, **kw):
    return lax.dot_general(a, b, (((1,), (1,)), ((), ())), preferred_element_type=F32, **kw)


def _dot_tn(a, b, **kw):
    return lax.dot_general(a, b, (((0,), (0,)), ((), ())), preferred_element_type=F32, **kw)


def _rms(v, gain):
    return v * lax.rsqrt(jnp.mean(v * v, axis=-1, keepdims=True) + EPS) * gain


def _mod_kernel(c_ref, w_ref, b_ref, o_ref):
    o_ref[...] = _dot(_silu(c_ref[...]), w_ref[...], precision=HIGHEST) + b_ref[...]


def _modulation(cvec, w_mod, b_mod):
    rows, d = cvec.shape
    n = w_mod.shape[1]
    tn = 1024
    return pl.pallas_call(
        _mod_kernel,
        grid=(n // tn,),
        in_specs=[pl.BlockSpec((rows, d), lambda j: (0, 0)),
                  pl.BlockSpec((d, tn), lambda j: (0, j)),
                  pl.BlockSpec((1, tn), lambda j: (0, j))],
        out_specs=pl.BlockSpec((rows, tn), lambda j: (0, j)),
        out_shape=jax.ShapeDtypeStruct((rows, n), F32),
        compiler_params=_cparams(("arbitrary",)),
        name="adaln_modulation",
    )(cvec, w_mod, b_mod)


_PLAIN, _ROPE, _ROPE_SCALED, _SIGMOID = 0, 1, 2, 3


def _proj_kernel(x_ref, gain_ref, sc_ref, sh_ref, w_ref, wg_ref, wgt_ref, prow_ref, pcol_ref, cos_ref, sin_ref,
                 *out_refs, kinds, rope):
    n_feat = len(kinds)
    feat_refs = out_refs[:n_feat]
    cols_ref, grow_ref, gl_ref = out_refs[n_feat:]
    tm = x_ref.shape[0]

    x = x_ref[...]
    h = (_rms(x, gain_ref[...]) * (1.0 + sc_ref[...]) + sh_ref[...]).astype(BF16)

    if rope:
        cos2 = cos_ref[...]
        sin2 = sin_ref[...]

    for g, kind in enumerate(kinds):
        p = _dot(h, w_ref[:, g * MIX_W:(g + 1) * MIX_W])
        if kind == _SIGMOID:
            p = _sigmoid(p)
        elif kind in (_ROPE, _ROPE_SCALED) and rope:
            heads = []
            for hh in range(N_HEADS):
                t = p[:, hh * HEAD_DIM:(hh + 1) * HEAD_DIM]
                heads.append(t * cos2 + pltpu.roll(t, HEAD_DIM // 2, 1) * sin2)
            p = jnp.concatenate(heads, axis=1)
        if kind == _ROPE_SCALED:
            p = p * (HEAD_DIM ** -0.5)
        feat_refs[g][...] = p.astype(feat_refs[g].dtype)

    r_i = _iota((CHUNK, CHUNK), 0)
    c_i = _iota((CHUNK, CHUNK), 1)
    lower_incl = (c_i <= r_i).astype(BF16)
    upper_incl = (c_i >= r_i).astype(BF16)

    def prefix_rows(v):
        hi, lo = _split_bf16(v)
        return _dot(lower_incl, hi) + _dot(lower_incl, lo)

    def prefix_lanes(v):
        hi, lo = _split_bf16(v)
        return _dot(hi, upper_incl) + _dot(lo, upper_incl)

    colt = _iota((tm, 8), 1)
    colc = _iota((CHUNK, 8), 1)
    fwd_col = (colc & 1) == 0
    for hh in range(N_HEADS):
        pg = _dot(h, wg_ref[hh])
        par = prow_ref[hh]
        la = jnp.where((colt == 2) | (colt == 3), 0.0, par[0:1, :] * _softplus(pg + par[1:2, :]))
        beta = _sigmoid(pg)
        for c in range(tm // CHUNK):
            sl = slice(c * CHUNK, (c + 1) * CHUNK)
            la_c = la[sl]
            pre = prefix_rows(la_c)
            suf = pre[CHUNK - 1:CHUNK, :] - pre + la_c
            g_c = jnp.where(fwd_col, pre, suf)
            rest = jnp.where(fwd_col, suf, pre) - la_c
            cols_ref[hh, sl, :] = jnp.where(colc < 2, g_c, jnp.where(colc < 4, beta[sl], jnp.where(
                colc < 6, jnp.exp(g_c), jnp.exp(rest))))
            gl_ref[c, hh:hh + 1, :] = jnp.exp(g_c[0:1, :] + rest[0:1, :])

    pgt = _dot_nt(wgt_ref[...], h)
    rowq = _iota((N_HEADS * 8, tm), 0) & 7
    lat = jnp.where(rowq < 2, pcol_ref[:, 0:1] * _softplus(pgt + pcol_ref[:, 1:2]), 0.0)
    rowc = _iota((N_HEADS * 8, CHUNK), 0) & 7
    for c in range(tm // CHUNK):
        sl = slice(c * CHUNK, (c + 1) * CHUNK)
        lat_c = lat[:, sl]
        pre_t = prefix_lanes(lat_c)
        suf_t = pre_t[:, CHUNK - 1:CHUNK] - pre_t + lat_c
        grow_ref[:, :, sl] = jnp.where(rowc == 0, pre_t, suf_t).reshape(N_HEADS, 8, CHUNK)


def _in_projection(x, gain, scale, shift, w_main, w_gab, w_gab_t, prow, pcol, cos2, sin2, kinds, rope):
    b, l, d = x.shape
    tm = min(PROJ_TM, l)
    tiles = l // tm
    n_chunk = tm // CHUNK
    feat_shapes = [jax.ShapeDtypeStruct((b, l, MIX_W), BF16) for _ in kinds]
    feat_specs = [pl.BlockSpec((None, tm, MIX_W), lambda i, j: (i, j, 0)) for _ in kinds]
    out_shape = feat_shapes + [jax.ShapeDtypeStruct((b, N_HEADS, l, 8), F32),
                               jax.ShapeDtypeStruct((b, N_HEADS, 8, l), F32),
                               jax.ShapeDtypeStruct((b, tiles, n_chunk, N_HEADS, 8), F32)]
    out_specs = feat_specs + [pl.BlockSpec((None, N_HEADS, tm, 8), lambda i, j: (i, 0, j, 0)),
                              pl.BlockSpec((None, N_HEADS, 8, tm), lambda i, j: (i, 0, 0, j)),
                              pl.BlockSpec((None, None, n_chunk, N_HEADS, 8), lambda i, j: (i, j, 0, 0, 0))]
    const = functools.partial(pl.BlockSpec, pipeline_mode=pl.Buffered(1))
    ncol = w_main.shape[1]
    in_specs = [
        pl.BlockSpec((None, tm, d), lambda i, j: (i, j, 0)),
        const((1, d), lambda i, j: (0, 0)),
        pl.BlockSpec((None, 1, d), lambda i, j: (i, 0, 0)),
        pl.BlockSpec((None, 1, d), lambda i, j: (i, 0, 0)),
        const((d, ncol), lambda i, j: (0, 0)),
        const((N_HEADS, d, 8), lambda i, j: (0, 0, 0)),
        const((N_HEADS * 8, d), lambda i, j: (0, 0)),
        const((N_HEADS, 2, 8), lambda i, j: (0, 0, 0)),
        const((N_HEADS * 8, 2), lambda i, j: (0, 0)),
        pl.BlockSpec((tm, HEAD_DIM), lambda i, j: (j, 0)),
        pl.BlockSpec((tm, HEAD_DIM), lambda i, j: (j, 0)),
    ]
    outs = pl.pallas_call(
        functools.partial(_proj_kernel, kinds=tuple(kinds), rope=rope),
        grid=(b, tiles),
        in_specs=in_specs,
        out_specs=out_specs,
        out_shape=out_shape,
        compiler_params=_cparams(("arbitrary", "arbitrary")),
        name="in_projection_rope" if rope else "in_projection_ctx",
    )(x, gain, scale, shift, w_main, w_gab, w_gab_t, prow, pcol, cos2, sin2)
    feats = outs[:len(kinds)]
    cols, grow, gl = outs[len(kinds):]
    return feats, (cols, grow, gl.reshape(b * (l // CHUNK), N_HEADS * 8))


def _unit_triangular_inverses(mats):
    r = _iota((CHUNK, CHUNK), 0)
    c = _iota((CHUNK, CHUNK), 1)
    eye = (r == c).astype(F32)
    invs = [eye - jnp.where((r >> 1) == (c >> 1), a, 0.0) for a in mats]
    for level in range(1, int(math.log2(CHUNK))):
        mask = ((r >> (level + 1)) == (c >> (level + 1))) & ((r >> level) != (c >> level))
        invb = [inv.astype(BF16) for inv in invs]
        half = [_dot(jnp.where(mask, a, 0.0).astype(BF16), ib).astype(BF16) for a, ib in zip(mats, invb)]
        invs = [inv - _dot(ib, hf) for inv, ib, hf in zip(invs, invb, half)]
    return invs


def _mixer_kernel(*refs, seq_len, with_query):
    n_chunk = seq_len // CHUNK
    if with_query:
        (lg_ref, gl_ref, rk_ref, rv_ref, gk_ref, gv_ref, rq_ref, gq_ref, rg_ref, gz_ref,
         cols_ref, grow_ref, cq_ref, ck_ref, cv_ref, s0_ref, gnw_ref, rmsw_ref,
         yret_ref, ygdn_ref,
         gks, gqs, oret, ogdn, ubuf, wbuf, pbuf, kvbuf, rtile, state) = refs
    else:
        (lg_ref, gl_ref, rk_ref, rv_ref, gk_ref, gv_ref,
         cols_ref, grow_ref, ck_ref, cv_ref, s0_ref,
         sfin_ref,
         gks, ubuf, wbuf, kvbuf, rtile, state) = refs
    bi = pl.program_id(0)
    hi = pl.program_id(1)

    row = _iota((CHUNK, CHUNK), 0)
    colm = _iota((CHUNK, CHUNK), 1)
    rowf = row.astype(F32)
    colf = colm.astype(F32)

    def conv_chunk(src_ref, w_ref, n):
        s = pl.multiple_of(n * CHUNK, CHUNK)
        x = src_ref[pl.ds(s, CHUNK), :].astype(F32)
        ps = pl.multiple_of(jnp.maximum(s - BF16_TILE_ROWS, 0), BF16_TILE_ROWS)
        ns = pl.multiple_of(jnp.minimum(s + CHUNK, seq_len - BF16_TILE_ROWS), BF16_TILE_ROWS)
        prev_row = src_ref[pl.ds(ps, BF16_TILE_ROWS), :].astype(F32)[BF16_TILE_ROWS - 1:BF16_TILE_ROWS, :]
        next_row = src_ref[pl.ds(ns, BF16_TILE_ROWS), :].astype(F32)[0:1, :]
        prev_row = prev_row * jnp.where(n > 0, 1.0, 0.0)
        next_row = next_row * jnp.where(n < n_chunk - 1, 1.0, 0.0)
        xp = jnp.where(row == 0, jnp.broadcast_to(prev_row, (CHUNK, HEAD_DIM)), pltpu.roll(x, 1, 0))
        xn = jnp.where(row == CHUNK - 1, jnp.broadcast_to(next_row, (CHUNK, HEAD_DIM)), pltpu.roll(x, CHUNK - 1, 0))
        return _silu(w_ref[0:1, :] * xp + w_ref[1:2, :] * x + w_ref[2:3, :] * xn)

    def l2n(v):
        return v * lax.rsqrt(jnp.sum(v * v, axis=-1, keepdims=True) + EPS)

    for d in range(2):
        lg = lg_ref[d, hi]
        if d == 0:
            dist, pos_q, pos_k = rowf - colf, rowf + 1.0, (CHUNK - 1.0) - rowf
        else:
            dist, pos_q, pos_k = colf - rowf, CHUNK - rowf, rowf
        rtile[4 * d + 0] = jnp.where(dist >= 0, jnp.exp(lg * jnp.maximum(dist, 0.0)), 0.0)
        rtile[4 * d + 1] = jnp.exp(lg * pos_q)
        rtile[4 * d + 2] = jnp.exp(lg * pos_k)
        rtile[4 * d + 3] = jnp.exp(lg * jnp.full((CHUNK, CHUNK), float(CHUNK), F32))

    state[...] = s0_ref[...]

    def bcast_col(cols, j):
        return jnp.broadcast_to(cols[:, j:j + 1], (CHUNK, CHUNK))

    pre_chunks = min(4, n_chunk)

    def prepass(m, carry):
        chunks = [m * pre_chunks + j for j in range(pre_chunks)]
        chunk_cs = [pl.ds(pl.multiple_of(n * CHUNK, CHUNK), CHUNK) for n in chunks]
        conv_k, conv_v, conv_q = [], [], []
        for n, cs in zip(chunks, chunk_cs):
            conv_k.append(l2n(conv_chunk(gk_ref, ck_ref, n)).astype(BF16))
            conv_v.append(conv_chunk(gv_ref, cv_ref, n).astype(BF16))
            gks[cs, :] = conv_k[-1]
            if with_query:
                conv_q.append((l2n(conv_chunk(gq_ref, cq_ref, n)) * (HEAD_DIM ** -0.5)).astype(BF16))
                gqs[cs, :] = conv_q[-1]
                ogdn[cs, :] = jnp.zeros((CHUNK, HEAD_DIM), F32)

        jobs = [(j, d) for j in range(pre_chunks) for d in range(2)]
        css = [chunk_cs[j] for j, _ in jobs]
        slots = [d * n_chunk + chunks[j] for j, d in jobs]
        dirs = [d for _, d in jobs]
        ks = [rk_ref[cs, :] for cs in css]
        vs = [rv_ref[cs, :] for cs in css]
        if with_query:
            scs = [(_dot_nt(rq_ref[cs, :], k) * rtile[4 * d + 0]).astype(BF16) for cs, k, d in zip(css, ks, dirs)]
            outs = [_dot(sc, v) for sc, v in zip(scs, vs)]
            for j, cs in enumerate(chunk_cs):
                oret[cs, :] = outs[2 * j] + outs[2 * j + 1]
        kvs = [_dot_tn((k.astype(F32) * rtile[4 * d + 2]).astype(BF16), v) for k, v, d in zip(ks, vs, dirs)]
        for slot, kv in zip(slots, kvs):
            kvbuf[slot] = kv
        ks = [conv_k[j] for j, _ in jobs]
        vs = [conv_v[j] for j, _ in jobs]
        colss = [cols_ref[cs, :] for cs in css]
        betas = [bcast_col(cols, 2 + d) for cols, d in zip(colss, dirs)]
        incls = [(row >= colm) if d == 0 else (row <= colm) for d in dirs]
        stricts = [(row > colm) if d == 0 else (row < colm) for d in dirs]
        decs = [jnp.exp(jnp.where(incl, bcast_col(cols, d) - jnp.broadcast_to(grow_ref[d:d + 1, cs], (CHUNK, CHUNK)), 0.0))
                for cols, d, cs, incl in zip(colss, dirs, css, incls)]
        kks = [_dot_nt(k, k) for k in ks]
        mats = [kk * beta * jnp.where(strict, dec, 0.0) for kk, beta, strict, dec in zip(kks, betas, stricts, decs)]
        tinvs = [t.astype(BF16) for t in _unit_triangular_inverses(mats)]
        rhs_v = [(beta * v.astype(F32)).astype(BF16) for beta, v in zip(betas, vs)]
        rhs_k = [(beta * bcast_col(cols, 4 + d) * k.astype(F32)).astype(BF16)
                 for beta, cols, d, k in zip(betas, colss, dirs, ks)]
        us = [_dot(t, r) for t, r in zip(tinvs, rhs_v)]
        ws = [_dot(t, r) for t, r in zip(tinvs, rhs_k)]
        for slot, u, w in zip(slots, us, ws):
            ubuf[slot] = u
            wbuf[slot] = w.astype(BF16)
        if with_query:
            qks = [_dot_nt(conv_q[j], k) for (j, _), k in zip(jobs, ks)]
            for slot, qk, incl, dec in zip(slots, qks, incls, decs):
                pbuf[slot] = (qk * jnp.where(incl, dec, 0.0)).astype(BF16)
        return carry

    lax.fori_loop(0, n_chunk // pre_chunks, prepass, 0)

    def finish(cs):
        ro = oret[cs, :]
        rc = ro - jnp.mean(ro, axis=-1, keepdims=True)
        ry = rc * lax.rsqrt(jnp.mean(rc * rc, axis=-1, keepdims=True) + EPS)
        yret_ref[cs, :] = (ry * gnw_ref[...] * _silu(rg_ref[cs, :].astype(F32))).astype(BF16)
        go = ogdn[cs, :]
        gy = go * lax.rsqrt(jnp.mean(go * go, axis=-1, keepdims=True) + EPS)
        ygdn_ref[cs, :] = (gy * rmsw_ref[...] * _silu(gz_ref[cs, :].astype(F32))).astype(BF16)

    def scan_step(n, carry, finishing):
        for d in range(2):
            nd = n if d == 0 else n_chunk - 1 - n
            s = pl.multiple_of(nd * CHUNK, CHUNK)
            cs = pl.ds(s, CHUNK)
            slot = d * n_chunk + nd
            st = state[d]
            if with_query:
                oret[cs, :] += _dot(rq_ref[cs, :], st.astype(BF16)) * rtile[4 * d + 1]
            state[d] = rtile[4 * d + 3] * st + kvbuf[slot]

            st = state[2 + d]
            stb = st.astype(BF16)
            cols = cols_ref[cs, :]
            vnb = (ubuf[slot] - _dot(wbuf[slot], stb)).astype(BF16)
            if with_query:
                qe = (gqs[cs, :].astype(F32) * bcast_col(cols, 4 + d)).astype(BF16)
                ogdn[cs, :] += _dot(pbuf[slot], vnb) + _dot(qe, stb)
            kt = (gks[cs, :].astype(F32) * bcast_col(cols, 6 + d)).astype(BF16)
            state[2 + d] = gl_ref[bi * n_chunk + nd, 8 * hi + d] * st + _dot_tn(kt, vnb)
            if finishing:
                finish(cs)
        return carry

    if with_query:
        lax.fori_loop(0, n_chunk // 2, functools.partial(scan_step, finishing=False), 0)
        lax.fori_loop(n_chunk // 2, n_chunk, functools.partial(scan_step, finishing=True), 0)
    else:
        lax.fori_loop(0, n_chunk, functools.partial(scan_step, finishing=False), 0)
        sfin_ref[...] = state[...]


def _mixers(lg, gl, feats, dec, conv, s0, gnw, rmsw, with_query):
    cols, grow = dec
    b, l, _ = feats["rk"].shape
    n_slot = 2 * (l // CHUNK)
    assert (l // CHUNK) % min(4, l // CHUNK) == 0
    smem = pl.BlockSpec(memory_space=pltpu.SMEM)
    head = pl.BlockSpec((None, l, HEAD_DIM), lambda i, j: (i, 0, j))
    colspec = pl.BlockSpec((None, None, l, 8), lambda i, j: (i, j, 0, 0))
    rowspec = pl.BlockSpec((None, None, 8, l), lambda i, j: (i, j, 0, 0))
    tile_f32 = pltpu.VMEM((n_slot, CHUNK, CHUNK), F32)
    tile_bf16 = pltpu.VMEM((n_slot, CHUNK, CHUNK), BF16)
    st_spec = pl.BlockSpec((None, None, 4, HEAD_DIM, HEAD_DIM), lambda i, j: (i, j, 0, 0, 0))

    def conv_spec(which):
        return pl.BlockSpec((SHORT_CONV, HEAD_DIM), lambda i, j, w=which: (0, w * N_HEADS + j))

    def gain_spec():
        return pl.BlockSpec((1, HEAD_DIM), lambda i, j: (0, j))

    if with_query:
        args = [lg, gl, feats["rk"], feats["rv"], feats["gk"], feats["gv"], feats["rq"], feats["gq"], feats["rg"],
                feats["gz"], cols, grow, conv, conv, conv, s0, gnw, rmsw]
        in_specs = [smem, smem] + [head] * 8 + [colspec, rowspec, conv_spec(0), conv_spec(1), conv_spec(2),
                                                st_spec, gain_spec(), gain_spec()]
        out_shape = [jax.ShapeDtypeStruct((b, l, MIX_W), BF16)] * 2
        out_specs = [head, head]
        scratch = ([pltpu.VMEM((l, HEAD_DIM), BF16)] * 2 + [pltpu.VMEM((l, HEAD_DIM), F32)] * 2
                   + [tile_f32, tile_bf16, tile_bf16, tile_f32])
    else:
        args = [lg, gl, feats["rk"], feats["rv"], feats["gk"], feats["gv"], cols, grow, conv, conv, s0]
        in_specs = [smem, smem] + [head] * 4 + [colspec, rowspec, conv_spec(1), conv_spec(2), st_spec]
        out_shape = jax.ShapeDtypeStruct((b, N_HEADS, 4, HEAD_DIM, HEAD_DIM), F32)
        out_specs = st_spec
        scratch = [pltpu.VMEM((l, HEAD_DIM), BF16)] + [tile_f32, tile_bf16, tile_f32]
    scratch = scratch + [pltpu.VMEM((8, CHUNK, CHUNK), F32), pltpu.VMEM((4, HEAD_DIM, HEAD_DIM), F32)]
    return pl.pallas_call(
        functools.partial(_mixer_kernel, seq_len=l, with_query=with_query),
        grid=(b, N_HEADS),
        in_specs=in_specs,
        out_specs=out_specs,
        out_shape=out_shape,
        scratch_shapes=scratch,
        compiler_params=_cparams(("arbitrary", "arbitrary")),
        name="mixers_latent" if with_query else "mixers_context",
    )(*args)


def _split_bf16(v):
    hi = v.astype(BF16)
    return hi, (v - hi.astype(F32)).astype(BF16)


def _route(h2, wrt_ref, bias_ref, cand_ref, sel_ref, wd_ref):
    tm = h2.shape[0]
    h_hi, h_lo = _split_bf16(h2)
    w_hi, w_lo = _split_bf16(wrt_ref[...])
    logits = _dot_nt(w_hi, h_hi) + (_dot_nt(w_hi, h_lo) + _dot_nt(w_lo, h_hi))
    scores = _sigmoid(logits)
    biased = scores + bias_ref[...]
    neg_inf = float("-inf")
    sub = _iota((GROUP_SIZE, tm), 0).astype(F32)
    group_score = []
    for g in range(N_GROUPS):
        blk = biased[g * GROUP_SIZE:(g + 1) * GROUP_SIZE, :]
        m1 = jnp.max(blk, axis=0, keepdims=True)
        first = jnp.min(jnp.where(blk == m1, sub, float(GROUP_SIZE)), axis=0, keepdims=True)
        m2 = jnp.max(jnp.where(sub == first, neg_inf, blk), axis=0, keepdims=True)
        group_score.append(m1 + m2)
    for g in range(N_GROUPS):
        ahead = jnp.zeros((1, tm), I32)
        for g2 in range(N_GROUPS):
            if g2 == g:
                continue
            before = (group_score[g2] > group_score[g])
            if g2 < g:
                before = before | (group_score[g2] == group_score[g])
            ahead = ahead + before.astype(I32)
        keep = jnp.broadcast_to(ahead, (GROUP_SIZE, tm)) < TOPK_GROUPS
        cand_ref[g * GROUP_SIZE:(g + 1) * GROUP_SIZE, :] = jnp.where(
            keep, biased[g * GROUP_SIZE:(g + 1) * GROUP_SIZE, :], neg_inf)
    work = cand_ref[...]
    eidx = _iota((N_EXPERTS, tm), 0).astype(F32)
    sel = jnp.zeros((N_EXPERTS, tm), jnp.bool_)
    for _ in range(TOP_K):
        best = jnp.max(work, axis=0, keepdims=True)
        first = jnp.min(jnp.where(work == best, eidx, float(N_EXPERTS)), axis=0, keepdims=True)
        pick = eidx == first
        sel = sel | pick
        work = jnp.where(pick, neg_inf, work)
    picked = jnp.where(sel, scores, 0.0)
    wsum = jnp.sum(picked, axis=0, keepdims=True)
    sel_ref[...] = sel.astype(F32)
    wd_ref[...] = picked / wsum * ROUTED_SCALE


def _mixout_kernel(x_ref, yr_ref, yg_ref, g0_ref, g1_ref, g2_ref, g3_ref, wr_ref, wg_ref, wo_ref,
                   npost_ref, gate1_ref, nffn_ref, sc2_ref, sh2_ref, wrt_ref, bias_ref,
                   x1_ref, h2_ref, sel_ref, wd_ref, cand_ref):
    r = _dot(yr_ref[...], wr_ref[...])
    g = _dot(yg_ref[...], wg_ref[...])
    half = r.shape[1] // 2
    merged = jnp.concatenate(
        [g0_ref[...].astype(F32) * r[:, :half] + g2_ref[...].astype(F32) * g[:, :half],
         g1_ref[...].astype(F32) * r[:, half:] + g3_ref[...].astype(F32) * g[:, half:]], axis=1)
    mo = _dot(merged.astype(BF16), wo_ref[...])
    x1 = x_ref[...] + gate1_ref[...] * _rms(mo, npost_ref[...])
    x1_ref[...] = x1
    h2 = _rms(x1, nffn_ref[...]) * (1.0 + sc2_ref[...]) + sh2_ref[...]
    h2_ref[...] = h2.astype(BF16)
    _route(h2, wrt_ref, bias_ref, cand_ref, sel_ref, wd_ref)


def _mixer_out(x, yr, yg, gates, wr, wg, wo, npost, gate1, nffn, sc2, sh2, wrt, bias):
    b, l, d = x.shape
    tm = PROJ_TM
    tiles = l // tm
    tok = lambda w: pl.BlockSpec((None, tm, w), lambda i, j: (i, j, 0))
    const = functools.partial(pl.BlockSpec, pipeline_mode=pl.Buffered(1))
    vec = lambda: const((1, d), lambda i, j: (0, 0))
    bvec = lambda: pl.BlockSpec((None, 1, d), lambda i, j: (i, 0, 0))
    emap = pl.BlockSpec((N_EXPERTS, tm), lambda i, j: (0, i * tiles + j))
    in_specs = ([tok(d), tok(MIX_W), tok(MIX_W)] + [tok(MIX_W)] * 4
                + [const((MIX_W, d), lambda i, j: (0, 0)), const((MIX_W, d), lambda i, j: (0, 0)),
                   const((d, d), lambda i, j: (0, 0)),
                   vec(), bvec(), vec(), bvec(), bvec(),
                   const((N_EXPERTS, d), lambda i, j: (0, 0)), const((N_EXPERTS, 1), lambda i, j: (0, 0))])
    return pl.pallas_call(
        _mixout_kernel,
        grid=(b, tiles),
        in_specs=in_specs,
        out_specs=[tok(d), tok(d), emap, emap],
        out_shape=[jax.ShapeDtypeStruct((b, l, d), F32), jax.ShapeDtypeStruct((b, l, d), BF16),
                   jax.ShapeDtypeStruct((N_EXPERTS, b * l), F32), jax.ShapeDtypeStruct((N_EXPERTS, b * l), F32)],
        scratch_shapes=[pltpu.VMEM((N_EXPERTS, tm), F32)],
        compiler_params=_cparams(("arbitrary", "arbitrary")),
        name="mixer_out_router",
    )(x, yr, yg, *gates, wr, wg, wo, npost, gate1, nffn, sc2, sh2, wrt, bias)


def _tile_positions(sel):
    tm = sel.shape[1]
    selb = sel.astype(BF16)
    earlier = (_iota((tm, tm), 0) < _iota((tm, tm), 1)).astype(BF16)
    rank = _dot(selb, earlier)
    cnt = _dot(selb, jnp.ones((tm, tm), BF16))
    nmb = jnp.floor((cnt + (MB_ROWS - 1)) * (1.0 / MB_ROWS))
    below = (_iota((N_EXPERTS, N_EXPERTS), 1) < _iota((N_EXPERTS, N_EXPERTS), 0)).astype(BF16)
    offmb = _dot(below, nmb.astype(BF16))
    rank_hi = jnp.floor(rank * (1.0 / MB_ROWS))
    hi = jnp.where(sel > 0.0, offmb + rank_hi, 255.0)
    lo = rank - rank_hi * MB_ROWS
    return hi, lo, cnt, offmb


def _dispatch_kernel(h2_ref, sel_ref, xs_ref, cnt_ref):
    last = pl.num_programs(0) - 1

    @pl.when(pl.program_id(0) < last)
    def _():
        _dispatch_tile(h2_ref, sel_ref, xs_ref, cnt_ref)

    @pl.when(pl.program_id(0) == last)
    def _():
        xs_ref[...] = jnp.zeros_like(xs_ref)


def _dispatch_tile(h2_ref, sel_ref, xs_ref, cnt_ref):
    tm = h2_ref.shape[0]
    sel = sel_ref[...]
    hi, lo, _, _ = _tile_positions(sel)
    hib = hi.astype(BF16)
    lob = lo.astype(BF16)
    selb = sel.astype(BF16)
    cnt_row = _dot_nt(jnp.ones((8, tm), BF16), selb)
    nmb_row = jnp.floor((cnt_row + (MB_ROWS - 1)) * (1.0 / MB_ROWS))
    before = (_iota((N_EXPERTS, N_EXPERTS), 0) < _iota((N_EXPERTS, N_EXPERTS), 1)).astype(BF16)
    start_row = _dot(nmb_row.astype(BF16), before) * MB_ROWS
    cnt_ref[...] = cnt_row
    start = start_row[0:1, :]
    end = start + cnt_row[0:1, :]
    x = h2_ref[...]
    rch = 256
    for rc in range(TILE_ROWS // rch):
        r_e = (rc * rch + _iota((rch, N_EXPERTS), 0)).astype(F32)
        owner = ((r_e >= start) & (r_e < end)).astype(BF16)
        r_t = rc * rch + _iota((rch, tm), 0)
        match = ((_dot(owner, hib) == (r_t >> _MB_SHIFT).astype(F32))
                 & (_dot(owner, lob) == (r_t & (MB_ROWS - 1)).astype(F32)))
        xs_ref[rc * rch:(rc + 1) * rch, :] = _dot(match.astype(BF16), x).astype(BF16)


def _dispatch(h2, sel_t):
    t, d = h2.shape
    tm = MOE_TM
    nt = t // tm
    return pl.pallas_call(
        _dispatch_kernel,
        grid=(nt + 1,),
        in_specs=[pl.BlockSpec((tm, d), lambda i: (jnp.minimum(i, nt - 1), 0)),
                  pl.BlockSpec((N_EXPERTS, tm), lambda i: (0, jnp.minimum(i, nt - 1)))],
        out_specs=[pl.BlockSpec((TILE_ROWS, d), lambda i: (i, 0)),
                   pl.BlockSpec((None, 8, N_EXPERTS), lambda i: (jnp.minimum(i, nt - 1), 0, 0))],
        out_shape=[jax.ShapeDtypeStruct((nt * TILE_ROWS + 2 * EXP_BM, d), BF16),
                   jax.ShapeDtypeStruct((nt, 8, N_EXPERTS), F32)],
        compiler_params=_cparams(("arbitrary",)),
        name="moe_dispatch",
    )(h2, sel_t)


def _expert_plan(cnt, n_blk):
    nt = cnt.shape[0]
    nmb = (cnt + (MB_ROWS - 1)) // MB_ROWS
    offmb = jnp.cumsum(nmb, axis=1) - nmb
    per_e = nmb.T
    incl = jnp.cumsum(per_e, axis=1)
    excl = incl - per_e
    tot = incl[:, -1]
    nb = (tot + (EXP_MB - 1)) // EXP_MB
    bend = jnp.cumsum(nb)
    bstart = bend - nb
    b = jnp.arange(n_blk, dtype=I32)
    valid = b < bend[-1]
    last = jnp.maximum(bend[-1] - 1, 0)
    bq = jnp.where(valid, b, last)
    blk_e = jnp.minimum(jnp.sum((bend[None, :] <= bq[:, None]).astype(I32), axis=1), N_EXPERTS - 1)
    onehot_e = (blk_e[:, None] == jnp.arange(N_EXPERTS, dtype=I32)[None, :]).astype(I32)
    pick = lambda table: jnp.sum(onehot_e[:, :, None] * table[None, :, :], axis=1)
    bstart_b = jnp.sum(onehot_e * bstart[None, :], axis=1)
    tot_b = jnp.sum(onehot_e * tot[None, :], axis=1)
    p0 = (bq - bstart_b) * EXP_MB
    blk_nmb = jnp.where(valid, jnp.clip(tot_b - p0, 0, EXP_MB), 0).astype(I32)
    p = p0[:, None] + jnp.arange(EXP_MB, dtype=I32)[None, :]
    incl_b = pick(incl)
    tile = jnp.minimum(jnp.sum((incl_b[:, None, :] <= p[:, :, None]).astype(I32), axis=2), nt - 1)
    onehot_t = (tile[:, :, None] == jnp.arange(nt, dtype=I32)[None, None, :]).astype(I32)
    shift_b = pick(offmb.T - excl)[:, None, :]
    where = tile * _TILE_MB + p + jnp.sum(onehot_t * shift_b, axis=2)
    j = jnp.arange(EXP_MB, dtype=I32)[None, :]
    used = j < blk_nmb[:, None]
    spare = nt * _TILE_MB + (b[:, None] % 2) * EXP_MB + j
    src = jnp.where(used, where, where[:, :1])
    dst = jnp.where(used, where, spare)
    return blk_e.astype(I32), blk_nmb, src.reshape(-1).astype(I32), dst.reshape(-1).astype(I32)


def _expert_kernel(blk_e_ref, blk_nmb_ref, src_ref, dst_ref, xs_hbm, wg_ref, wu_ref, wd_ref, ys_hbm,
                   xbuf, ybuf, wgb, wub, wdb, in_sem, out_sem):
    i = pl.program_id(0)
    n_blk = pl.num_programs(0)

    def gather_copy(blk, j):
        slot = blk % 2
        rows = pl.ds(pl.multiple_of(src_ref[blk * EXP_MB + j] * MB_ROWS, MB_ROWS), MB_ROWS)
        return pltpu.make_async_copy(xs_hbm.at[rows, :], xbuf.at[slot, j * MB_ROWS:(j + 1) * MB_ROWS, :],
                                     in_sem.at[slot])

    def scatter_copy(blk, j):
        slot = blk % 2
        rows = pl.ds(pl.multiple_of(dst_ref[blk * EXP_MB + j] * MB_ROWS, MB_ROWS), MB_ROWS)
        return pltpu.make_async_copy(ybuf.at[slot, j * MB_ROWS:(j + 1) * MB_ROWS, :], ys_hbm.at[rows, :],
                                     out_sem.at[slot])

    def start_gather(blk):
        for j in range(EXP_MB):
            gather_copy(blk, j).start()

    def wait_gather(blk):
        for j in range(EXP_MB):
            gather_copy(blk, j).wait()

    def start_scatter(blk):
        for j in range(EXP_MB):
            scatter_copy(blk, j).start()

    def wait_scatter(blk):
        for j in range(EXP_MB):
            scatter_copy(blk, j).wait()

    @pl.when(jnp.logical_and(i == 0, blk_nmb_ref[0] > 0))
    def _():
        ybuf[...] = jnp.zeros_like(ybuf)
        start_gather(0)

    @pl.when(blk_nmb_ref[i] > 0)
    def _():
        slot = i % 2
        nxt = jnp.minimum(i + 1, n_blk - 1)
        has_next = jnp.logical_and(i + 1 < n_blk, blk_nmb_ref[nxt] > 0)

        @pl.when(has_next)
        def _():
            start_gather(i + 1)

        @pl.when(jnp.logical_or(i == 0, blk_e_ref[i] != blk_e_ref[jnp.maximum(i - 1, 0)]))
        def _():
            wgb[...] = wg_ref[...].astype(BF16)
            wub[...] = wu_ref[...].astype(BF16)
            wdb[...] = wd_ref[...].astype(BF16)

        wait_gather(i)

        @pl.when(i >= 2)
        def _():
            wait_scatter(i - 2)

        half = EXP_BM // 2

        def ffn(lo):
            x = xbuf[slot, lo:lo + half, :]
            act = (_silu(_dot(x, wgb[...])) * _dot(x, wub[...])).astype(BF16)
            ybuf[slot, lo:lo + half, :] = _dot(act, wdb[...]).astype(BF16)

        ffn(0)

        @pl.when(blk_nmb_ref[i] > EXP_MB // 2)
        def _():
            ffn(half)

        start_scatter(i)

        @pl.when(jnp.logical_not(has_next))
        def _():
            @pl.when(i >= 1)
            def _():
                wait_scatter(i - 1)
            wait_scatter(i)


def _expert_ffn(xs, blk_e, blk_nmb, src, dst, wg, wu, wd):
    rows, d = xs.shape
    n_blk = blk_e.shape[0]
    de = wg.shape[2]
    grid_spec = pltpu.PrefetchScalarGridSpec(
        num_scalar_prefetch=4,
        grid=(n_blk,),
        in_specs=[pl.BlockSpec(memory_space=pl.ANY),
                  pl.BlockSpec((None, d, de), lambda i, be, bn, sr, ds: (be[i], 0, 0)),
                  pl.BlockSpec((None, d, de), lambda i, be, bn, sr, ds: (be[i], 0, 0)),
                  pl.BlockSpec((None, de, d), lambda i, be, bn, sr, ds: (be[i], 0, 0))],
        out_specs=pl.BlockSpec(memory_space=pl.ANY),
        scratch_shapes=[pltpu.VMEM((2, EXP_BM, d), BF16), pltpu.VMEM((2, EXP_BM, d), BF16),
                        pltpu.VMEM((d, de), BF16), pltpu.VMEM((d, de), BF16), pltpu.VMEM((de, d), BF16),
                        pltpu.SemaphoreType.DMA((2,)), pltpu.SemaphoreType.DMA((2,))],
    )
    return pl.pallas_call(
        _expert_kernel,
        grid_spec=grid_spec,
        out_shape=jax.ShapeDtypeStruct((rows, d), BF16),
        input_output_aliases={4: 0},
        compiler_params=_cparams(("arbitrary",)),
        name="moe_experts",
    )(blk_e, blk_nmb, src, dst, xs, wg, wu, wd)


def _combine_kernel(ys_ref, sel_ref, wd_ref, h2_ref, x1_ref, wsg_ref, wsu_ref, wsd_ref, npost_ref, gate2_ref,
                    o_ref, c_ref):
    tm = h2_ref.shape[0]
    sel = sel_ref[...]
    hi, lo, cnt, offmb = _tile_positions(sel)
    hit = hi.T.astype(BF16)
    lot = lo.T.astype(BF16)
    wt = wd_ref[...].T
    wt_hi = wt.astype(BF16)
    wt_lo = (wt - wt_hi.astype(F32)).astype(BF16)
    start = offmb[:, :LANES] * MB_ROWS
    end = start + cnt[:, :LANES]
    for cc in range(TILE_ROWS // LANES):
        r_e = (cc * LANES + _iota((N_EXPERTS, LANES), 1)).astype(F32)
        owner = ((r_e >= start) & (r_e < end)).astype(BF16)
        r_t = cc * LANES + _iota((tm, LANES), 1)
        match = ((_dot(hit, owner) == (r_t >> _MB_SHIFT).astype(F32))
                 & (_dot(lot, owner) == (r_t & (MB_ROWS - 1)).astype(F32)))
        w_sel = _dot(wt_hi, owner) + _dot(wt_lo, owner)
        c_ref[:, cc * LANES:(cc + 1) * LANES] = jnp.where(match, w_sel, 0.0).astype(BF16)
    routed = _dot(c_ref[...], ys_ref[...])
    h2 = h2_ref[...]
    shared = _dot((_silu(_dot(h2, wsg_ref[...])) * _dot(h2, wsu_ref[...])).astype(BF16), wsd_ref[...])
    o_ref[...] = x1_ref[...] + gate2_ref[...] * _rms(routed + shared, npost_ref[...])


def _combine(ys, sel_t, wd_t, h2, x1, wsg, wsu, wsd, npost, gate2, seq_len):
    t, d = h2.shape
    tm = MOE_TM
    nt = t // tm
    per_seq = seq_len // tm
    ds = wsg.shape[1]
    const = functools.partial(pl.BlockSpec, pipeline_mode=pl.Buffered(1))
    emap = pl.BlockSpec((N_EXPERTS, tm), lambda i: (0, i))
    return pl.pallas_call(
        _combine_kernel,
        grid=(nt,),
        in_specs=[pl.BlockSpec((TILE_ROWS, d), lambda i: (i, 0)), emap, emap,
                  pl.BlockSpec((tm, d), lambda i: (i, 0)), pl.BlockSpec((tm, d), lambda i: (i, 0)),
                  const((d, ds), lambda i: (0, 0)), const((d, ds), lambda i: (0, 0)), const((ds, d), lambda i: (0, 0)),
                  const((1, d), lambda i: (0, 0)),
                  pl.BlockSpec((None, 1, d), lambda i: (i // per_seq, 0, 0))],
        out_specs=pl.BlockSpec((tm, d), lambda i: (i, 0)),
        out_shape=jax.ShapeDtypeStruct((t, d), F32),
        scratch_shapes=[pltpu.VMEM((tm, TILE_ROWS), BF16)],
        compiler_params=_cparams(("arbitrary",)),
        name="moe_combine",
    )(ys, sel_t, wd_t, h2, x1, wsg, wsu, wsd, npost, gate2)


def _rope_tables(n):
    rows = n // GRID_W
    pos_r = jnp.repeat(jnp.arange(rows, dtype=F32), GRID_W)
    pos_c = jnp.tile(jnp.arange(GRID_W, dtype=F32), rows)
    n_freq = HEAD_DIM // 4
    inv = ROPE_BASE ** (-jnp.arange(n_freq, dtype=F32) / n_freq)
    ang = jnp.concatenate([pos_r[:, None] * inv, pos_c[:, None] * inv], axis=-1)
    cos, sin = jnp.cos(ang), jnp.sin(ang)
    return jnp.concatenate([cos, cos], axis=-1), jnp.concatenate([-sin, sin], axis=-1)


def kernel(x, c, ctx, c_ctx, w_mod, b_mod, norm_mix_pre, norm_mix_post, norm_ffn_pre, norm_ffn_post, w_in, gdn_conv, ret_log_decay, gdn_a_log, gdn_dt_bias, ret_gn_w, gdn_norm_w, w_ret_out, w_gdn_out, w_o, w_router, router_bias, w_gate, w_up, w_down, w_sh_gate, w_sh_up, w_sh_down):
    b, n, d = x.shape
    depth = w_mod.shape[0]
    assert depth == 1, "single-layer block"
    assert n % max(PROJ_TM, MOE_TM) == 0 and ctx.shape[1] % CHUNK == 0
    assert _TILE_MB < 255

    rows = -(-(b + 1) // 8) * 8
    cvec = jnp.zeros((rows, d), F32).at[:b].set(c).at[b].set(c_ctx)
    mod = _modulation(cvec, w_mod[0], b_mod[0][None, :])
    sh1, sc1, g1, sh2, sc2, g2 = [mod[:b, k * d:(k + 1) * d][:, None, :] for k in range(6)]
    ctx_shift = jnp.broadcast_to(mod[b, 0:d][None, None, :], (b, 1, d))
    ctx_scale = jnp.broadcast_to(mod[b, d:2 * d][None, None, :], (b, 1, d))

    w_in0 = w_in[0]
    n_main = 4 * MIX_W
    w_state = w_in0[:, :n_main].astype(BF16)
    slots = ((0, 0), (0, 1), (1, 0), (1, 1), (0, 0), (0, 1), (0, 0), (0, 1))
    gab_cols = n_main + jnp.array([[ab * 2 * N_HEADS + dr * N_HEADS + hh for ab, dr in slots]
                                   for hh in range(N_HEADS)], I32)
    w_gab = jnp.transpose(w_in0[:, gab_cols], (1, 0, 2)).astype(BF16)
    n_state = n_main + N_GAB
    w_query = w_in0[:, n_state:].astype(BF16)
    w_all = jnp.concatenate([w_state, w_query], axis=1)
    a_coef = -jnp.exp(gdn_a_log[0].astype(F32))
    dtb = gdn_dt_bias[0].astype(F32)
    is_alpha = jnp.array([ab == 0 for ab, _ in slots])
    dirs = jnp.array([dr for _, dr in slots], I32)
    prow = jnp.stack([jnp.where(is_alpha[None, :], a_coef.T[:, dirs], 0.0),
                      jnp.where(is_alpha[None, :], dtb.T[:, dirs], 0.0)], axis=1)
    first2 = (jnp.arange(8) < 2)
    w_gab_t = (jnp.transpose(w_gab, (0, 2, 1)) * first2[None, :, None].astype(BF16)).reshape(N_HEADS * 8, d)
    pcol = (jnp.transpose(prow, (0, 2, 1)) * first2[None, :, None]).reshape(N_HEADS * 8, 2)
    gain_mix = norm_mix_pre[0][None, :]
    cos2, sin2 = _rope_tables(n)

    state_kinds = (_ROPE_SCALED, _PLAIN, _PLAIN, _PLAIN)
    query_kinds = (_ROPE, _PLAIN, _PLAIN, _PLAIN, _SIGMOID, _SIGMOID, _SIGMOID, _SIGMOID)
    lg = ret_log_decay[0].astype(F32)
    conv = gdn_conv[0].astype(F32)

    lc = ctx.shape[1]
    cfe, (ccols, cgrow, cgl) = _in_projection(
        ctx, gain_mix, ctx_scale, ctx_shift, w_state, w_gab, w_gab_t, prow, pcol,
        cos2[:lc], sin2[:lc], state_kinds, rope=False)
    cfeats = dict(zip(("rk", "rv", "gk", "gv"), cfe))
    zero_state = jnp.zeros((b, N_HEADS, 4, HEAD_DIM, HEAD_DIM), F32)
    init = _mixers(lg, cgl, cfeats, (ccols, cgrow), conv, zero_state, None, None, with_query=False)

    fe, (cols, grow, gl) = _in_projection(
        x, gain_mix, sc1, sh1, w_all, w_gab, w_gab_t, prow, pcol, cos2, sin2,
        state_kinds + query_kinds, rope=True)
    feats = dict(zip(("rk", "rv", "gk", "gv", "rq", "rg", "gq", "gz"), fe[:8]))
    gates = fe[8:]
    y_ret, y_gdn = _mixers(lg, gl, feats, (cols, grow), conv, init,
                           ret_gn_w[0][None, :], gdn_norm_w[0][None, :], with_query=True)
    x1, h2, sel_t, wd_t = _mixer_out(
        x, y_ret, y_gdn, gates, w_ret_out[0].astype(BF16), w_gdn_out[0].astype(BF16), w_o[0].astype(BF16),
        norm_mix_post[0][None, :], g1, norm_ffn_pre[0][None, :], sc2, sh2,
        w_router[0].T.astype(F32), router_bias[0].astype(F32)[:, None])

    t = b * n
    h2f = h2.reshape(t, d)
    xs, cnt = _dispatch(h2f, sel_t)
    nt = t // MOE_TM
    n_blk = nt * _TILE_MB // EXP_MB + N_EXPERTS
    blk_e, blk_nmb, src, dst = _expert_plan(cnt[:, 0, :].astype(I32), n_blk)
    ys = _expert_ffn(xs, blk_e, blk_nmb, src, dst, w_gate[0], w_up[0], w_down[0])
    out = _combine(ys, sel_t, wd_t, h2f, x1.reshape(t, d), w_sh_gate[0].astype(BF16), w_sh_up[0].astype(BF16),
                   w_sh_down[0].astype(BF16), norm_ffn_post[0][None, :], g2, n)
    return out.reshape(b, n, d)
```

```python
import functools
import math

import jax
import jax.numpy as jnp
from jax import lax
from jax.experimental import pallas as pl
from jax.experimental.pallas import tpu as pltpu

F32 = jnp.float32
BF16 = jnp.bfloat16
I32 = jnp.int32
HIGHEST = lax.Precision.HIGHEST

N_HEADS = 4
HEAD_DIM = 128
MIX_W = N_HEADS * HEAD_DIM
CHUNK = 128
SHORT_CONV = 3
ROPE_BASE = 10000.0
GRID_W = 64
N_EXPERTS = 64
TOP_K = 8
N_GROUPS = 8
TOPK_GROUPS = 4
GROUP_SIZE = N_EXPERTS // N_GROUPS
ROUTED_SCALE = 2.5
EPS = 1e-6
N_GAB = 4 * N_HEADS

LANES = 128
BF16_TILE_ROWS = 16
VMEM_LIMIT_BYTES = 56 * 1024 * 1024

PROJ_TM = 512
MIXOUT_TM = 512
MOE_TM = 256
MB_ROWS = BF16_TILE_ROWS
_MB_SHIFT = MB_ROWS.bit_length() - 1
_TILE_MB = -(-(TOP_K * MOE_TM // MB_ROWS + N_EXPERTS * (MB_ROWS - 1) // MB_ROWS + 1) // 8) * 8
TILE_ROWS = _TILE_MB * MB_ROWS
EXP_BM = 512
EXP_MB = EXP_BM // MB_ROWS


def _cparams(sem):
    return pltpu.CompilerParams(dimension_semantics=sem, vmem_limit_bytes=VMEM_LIMIT_BYTES)


def _sigmoid(v):
    return 0.5 * jnp.tanh(0.5 * v) + 0.5


def _silu(v):
    return v * _sigmoid(v)


def _softplus(v):
    return jnp.maximum(v, 0.0) + jnp.log1p(jnp.exp(-jnp.abs(v)))


def _iota(shape, dim):
    return lax.broadcasted_iota(I32, shape, dim)


def _dot(a, b, **kw):
    return jnp.dot(a, b, preferred_element_type=F32, **kw)


def _dot_nt(a, b, **kw):
    return lax.dot_general(a, b, (((1,), (1,)), ((), ())), preferred_element_type=F32, **kw)


def _dot_tn(a, b, **kw):
    return lax.dot_general(a, b, (((0,), (0,)), ((), ())), preferred_element_type=F32, **kw)


def _rms(v, gain):
    return v * lax.rsqrt(jnp.mean(v * v, axis=-1, keepdims=True) + EPS) * gain


def _mod_kernel(c_ref, w_ref, b_ref, o_ref):
    o_ref[...] = _dot(_silu(c_ref[...]), w_ref[...], precision=HIGHEST) + b_ref[...]


def _modulation(cvec, w_mod, b_mod):
    rows, d = cvec.shape
    n = w_mod.shape[1]
    tn = 1024
    return pl.pallas_call(
        _mod_kernel,
        grid=(n // tn,),
        in_specs=[pl.BlockSpec((rows, d), lambda j: (0, 0)),
                  pl.BlockSpec((d, tn), lambda j: (0, j)),
                  pl.BlockSpec((1, tn), lambda j: (0, j))],
        out_specs=pl.BlockSpec((rows, tn), lambda j: (0, j)),
        out_shape=jax.ShapeDtypeStruct((rows, n), F32),
        compiler_params=_cparams(("arbitrary",)),
        name="adaln_modulation",
    )(cvec, w_mod, b_mod)


_PLAIN, _ROPE, _ROPE_SCALED, _SIGMOID = 0, 1, 2, 3


def _proj_kernel(x_ref, gain_ref, sc_ref, sh_ref, w_ref, wg_ref, wgt_ref, prow_ref, pcol_ref, cos_ref, sin_ref,
                 *out_refs, kinds, rope):
    n_feat = len(kinds)
    feat_refs = out_refs[:n_feat]
    cols_ref, grow_ref, gl_ref = out_refs[n_feat:]
    tm = x_ref.shape[0]

    x = x_ref[...]
    h = (_rms(x, gain_ref[...]) * (1.0 + sc_ref[...]) + sh_ref[...]).astype(BF16)

    if rope:
        cos2 = cos_ref[...]
        sin2 = sin_ref[...]

    for g, kind in enumerate(kinds):
        p = _dot(h, w_ref[:, g * MIX_W:(g + 1) * MIX_W])
        if kind == _SIGMOID:
            p = _sigmoid(p)
        elif kind in (_ROPE, _ROPE_SCALED) and rope:
            heads = []
            for hh in range(N_HEADS):
                t = p[:, hh * HEAD_DIM:(hh + 1) * HEAD_DIM]
                heads.append(t * cos2 + pltpu.roll(t, HEAD_DIM // 2, 1) * sin2)
            p = jnp.concatenate(heads, axis=1)
        if kind == _ROPE_SCALED:
            p = p * (HEAD_DIM ** -0.5)
        feat_refs[g][...] = p.astype(feat_refs[g].dtype)

    r_i = _iota((CHUNK, CHUNK), 0)
    c_i = _iota((CHUNK, CHUNK), 1)
    lower_incl = (c_i <= r_i).astype(BF16)
    upper_incl = (c_i >= r_i).astype(BF16)

    def prefix_rows(v):
        hi, lo = _split_bf16(v)
        return _dot(lower_incl, hi) + _dot(lower_incl, lo)

    def prefix_lanes(v):
        hi, lo = _split_bf16(v)
        return _dot(hi, upper_incl) + _dot(lo, upper_incl)

    nc = N_HEADS * 8
    pg = _dot(h, wg_ref[...])
    colt = _iota((tm, nc), 1) & 7
    la = jnp.where((colt == 2) | (colt == 3), 0.0, prow_ref[0:1, :] * _softplus(pg + prow_ref[1:2, :]))
    beta = _sigmoid(pg)
    colc = _iota((CHUNK, nc), 1) & 7
    fwd_col = (colc & 1) == 0
    for c in range(tm // CHUNK):
        sl = slice(c * CHUNK, (c + 1) * CHUNK)
        la_c = la[sl]
        pre = prefix_rows(la_c)
        suf = pre[CHUNK - 1:CHUNK, :] - pre + la_c
        g_c = jnp.where(fwd_col, pre, suf)
        rest = jnp.where(fwd_col, suf, pre) - la_c
        vals = jnp.where(colc < 2, g_c, jnp.where(colc < 4, beta[sl], jnp.where(
            colc < 6, jnp.exp(g_c), jnp.exp(rest))))
        for hh in range(N_HEADS):
            cols_ref[hh, sl, :] = vals[:, 8 * hh:8 * hh + 8]
        gl_ref[c:c + 1, :] = jnp.exp(g_c[0:1, :] + rest[0:1, :])

    pgt = _dot_nt(wgt_ref[...], h)
    rowq = _iota((N_HEADS * 8, tm), 0) & 7
    lat = jnp.where(rowq < 2, pcol_ref[:, 0:1] * _softplus(pgt + pcol_ref[:, 1:2]), 0.0)
    rowc = _iota((N_HEADS * 8, CHUNK), 0) & 7
    for c in range(tm // CHUNK):
        sl = slice(c * CHUNK, (c + 1) * CHUNK)
        lat_c = lat[:, sl]
        pre_t = prefix_lanes(lat_c)
        suf_t = pre_t[:, CHUNK - 1:CHUNK] - pre_t + lat_c
        grow_ref[:, :, sl] = jnp.where(rowc == 0, pre_t, suf_t).reshape(N_HEADS, 8, CHUNK)


def _in_projection(x, gain, scale, shift, w_main, w_gab, w_gab_t, prow, pcol, cos2, sin2, kinds, rope):
    b, l, d = x.shape
    tm = min(PROJ_TM, l)
    tiles = l // tm
    n_chunk = tm // CHUNK
    feat_shapes = [jax.ShapeDtypeStruct((b, l, MIX_W), BF16) for _ in kinds]
    feat_specs = [pl.BlockSpec((None, tm, MIX_W), lambda i, j: (i, j, 0)) for _ in kinds]
    out_shape = feat_shapes + [jax.ShapeDtypeStruct((b, N_HEADS, l, 8), F32),
                               jax.ShapeDtypeStruct((b, N_HEADS, 8, l), F32),
                               jax.ShapeDtypeStruct((b, tiles, n_chunk, N_HEADS * 8), F32)]
    out_specs = feat_specs + [pl.BlockSpec((None, N_HEADS, tm, 8), lambda i, j: (i, 0, j, 0)),
                              pl.BlockSpec((None, N_HEADS, 8, tm), lambda i, j: (i, 0, 0, j)),
                              pl.BlockSpec((None, None, n_chunk, N_HEADS * 8), lambda i, j: (i, j, 0, 0))]
    const = functools.partial(pl.BlockSpec, pipeline_mode=pl.Buffered(1))
    ncol = w_main.shape[1]
    in_specs = [
        pl.BlockSpec((None, tm, d), lambda i, j: (i, j, 0)),
        const((1, d), lambda i, j: (0, 0)),
        pl.BlockSpec((None, 1, d), lambda i, j: (i, 0, 0)),
        pl.BlockSpec((None, 1, d), lambda i, j: (i, 0, 0)),
        const((d, ncol), lambda i, j: (0, 0)),
        const((d, N_HEADS * 8), lambda i, j: (0, 0)),
        const((N_HEADS * 8, d), lambda i, j: (0, 0)),
        const((2, N_HEADS * 8), lambda i, j: (0, 0)),
        const((N_HEADS * 8, 2), lambda i, j: (0, 0)),
        pl.BlockSpec((tm, HEAD_DIM), lambda i, j: (j, 0)),
        pl.BlockSpec((tm, HEAD_DIM), lambda i, j: (j, 0)),
    ]
    outs = pl.pallas_call(
        functools.partial(_proj_kernel, kinds=tuple(kinds), rope=rope),
        grid=(b, tiles),
        in_specs=in_specs,
        out_specs=out_specs,
        out_shape=out_shape,
        compiler_params=_cparams(("arbitrary", "arbitrary")),
        name="in_projection_rope" if rope else "in_projection_ctx",
    )(x, gain, scale, shift, w_main, w_gab, w_gab_t, prow, pcol, cos2, sin2)
    feats = outs[:len(kinds)]
    cols, grow, gl = outs[len(kinds):]
    return feats, (cols, grow, gl.reshape(b * (l // CHUNK), N_HEADS * 8))


def _unit_triangular_inverses(mats):
    r = _iota((CHUNK, CHUNK), 0)
    c = _iota((CHUNK, CHUNK), 1)
    eye = (r == c).astype(F32)
    invs = [eye - jnp.where((r >> 1) == (c >> 1), a, 0.0) for a in mats]
    for level in range(1, int(math.log2(CHUNK))):
        mask = ((r >> (level + 1)) == (c >> (level + 1))) & ((r >> level) != (c >> level))
        invb = [inv.astype(BF16) for inv in invs]
        half = [_dot(jnp.where(mask, a, 0.0).astype(BF16), ib).astype(BF16) for a, ib in zip(mats, invb)]
        invs = [inv - _dot(ib, hf) for inv, ib, hf in zip(invs, invb, half)]
    return invs


def _mixer_kernel(*refs, seq_len, with_query):
    n_chunk = seq_len // CHUNK
    if with_query:
        (lg_ref, gl_ref, rk_ref, rv_ref, gk_ref, gv_ref, rq_ref, gq_ref, rg_ref, gz_ref,
         cols_ref, grow_ref, cq_ref, ck_ref, cv_ref, s0_ref, gnw_ref, rmsw_ref,
         yret_ref, ygdn_ref,
         qes, oret, ogdn, ubuf, wbuf, pbuf, nbuf, cbuf, kvbuf, rtile, state) = refs
    else:
        (lg_ref, gl_ref, rk_ref, rv_ref, gk_ref, gv_ref,
         cols_ref, grow_ref, ck_ref, cv_ref, s0_ref,
         sfin_ref,
         nbuf, cbuf, kvbuf, rtile, state) = refs
    bi = pl.program_id(0)
    hi = pl.program_id(1)

    row = _iota((CHUNK, CHUNK), 0)
    colm = _iota((CHUNK, CHUNK), 1)
    rowf = row.astype(F32)
    colf = colm.astype(F32)

    def conv_chunk(src_ref, w_ref, n):
        s = pl.multiple_of(n * CHUNK, CHUNK)
        x = src_ref[pl.ds(s, CHUNK), :].astype(F32)
        ps = pl.multiple_of(jnp.maximum(s - BF16_TILE_ROWS, 0), BF16_TILE_ROWS)
        ns = pl.multiple_of(jnp.minimum(s + CHUNK, seq_len - BF16_TILE_ROWS), BF16_TILE_ROWS)
        prev_row = src_ref[pl.ds(ps, BF16_TILE_ROWS), :].astype(F32)[BF16_TILE_ROWS - 1:BF16_TILE_ROWS, :]
        next_row = src_ref[pl.ds(ns, BF16_TILE_ROWS), :].astype(F32)[0:1, :]
        prev_row = prev_row * jnp.where(n > 0, 1.0, 0.0)
        next_row = next_row * jnp.where(n < n_chunk - 1, 1.0, 0.0)
        xp = jnp.where(row == 0, jnp.broadcast_to(prev_row, (CHUNK, HEAD_DIM)), pltpu.roll(x, 1, 0))
        xn = jnp.where(row == CHUNK - 1, jnp.broadcast_to(next_row, (CHUNK, HEAD_DIM)), pltpu.roll(x, CHUNK - 1, 0))
        return _silu(w_ref[0:1, :] * xp + w_ref[1:2, :] * x + w_ref[2:3, :] * xn)

    def l2n(v):
        return v * lax.rsqrt(jnp.sum(v * v, axis=-1, keepdims=True) + EPS)

    for d in range(2):
        lg = lg_ref[d, hi]
        if d == 0:
            dist, pos_q, pos_k = rowf - colf, rowf + 1.0, (CHUNK - 1.0) - rowf
        else:
            dist, pos_q, pos_k = colf - rowf, CHUNK - rowf, rowf
        rtile[4 * d + 0] = jnp.where(dist >= 0, jnp.exp(lg * jnp.maximum(dist, 0.0)), 0.0)
        rtile[4 * d + 1] = jnp.exp(lg * pos_q)
        rtile[4 * d + 2] = jnp.exp(lg * pos_k)
        rtile[4 * d + 3] = jnp.exp(lg * jnp.full((CHUNK, CHUNK), float(CHUNK), F32))

    state[...] = s0_ref[...]

    def bcast_col(cols, j):
        return jnp.broadcast_to(cols[:, j:j + 1], (CHUNK, CHUNK))

    pre_chunks = min(4, n_chunk)

    def prepass(m, carry):
        chunks = [m * pre_chunks + j for j in range(pre_chunks)]
        chunk_cs = [pl.ds(pl.multiple_of(n * CHUNK, CHUNK), CHUNK) for n in chunks]
        conv_k, conv_v, conv_q = [], [], []
        for n, cs in zip(chunks, chunk_cs):
            conv_k.append(l2n(conv_chunk(gk_ref, ck_ref, n)).astype(BF16))
            conv_v.append(conv_chunk(gv_ref, cv_ref, n).astype(BF16))
            if with_query:
                conv_q.append((l2n(conv_chunk(gq_ref, cq_ref, n)) * (HEAD_DIM ** -0.5)).astype(BF16))
                ogdn[cs, :] = jnp.zeros((CHUNK, HEAD_DIM), F32)

        jobs = [(j, d) for j in range(pre_chunks) for d in range(2)]
        css = [chunk_cs[j] for j, _ in jobs]
        slots = [d * n_chunk + chunks[j] for j, d in jobs]
        dirs = [d for _, d in jobs]
        ks = [rk_ref[cs, :] for cs in css]
        vs = [rv_ref[cs, :] for cs in css]
        if with_query:
            scs = [(_dot_nt(rq_ref[cs, :], k) * rtile[4 * d + 0]).astype(BF16) for cs, k, d in zip(css, ks, dirs)]
            outs = [_dot(sc, v) for sc, v in zip(scs, vs)]
            for j, cs in enumerate(chunk_cs):
                oret[cs, :] = outs[2 * j] + outs[2 * j + 1]
        kvs = [_dot_tn((k.astype(F32) * rtile[4 * d + 2]).astype(BF16), v) for k, v, d in zip(ks, vs, dirs)]
        for slot, kv in zip(slots, kvs):
            kvbuf[slot] = kv
        ks = [conv_k[j] for j, _ in jobs]
        vs = [conv_v[j] for j, _ in jobs]
        colss = [cols_ref[cs, :] for cs in css]
        betas = [bcast_col(cols, 2 + d) for cols, d in zip(colss, dirs)]
        incls = [(row >= colm) if d == 0 else (row <= colm) for d in dirs]
        stricts = [(row > colm) if d == 0 else (row < colm) for d in dirs]
        decs = [jnp.exp(jnp.where(incl, bcast_col(cols, d) - jnp.broadcast_to(grow_ref[d:d + 1, cs], (CHUNK, CHUNK)), 0.0))
                for cols, d, cs, incl in zip(colss, dirs, css, incls)]
        kks = [_dot_nt(k, k) for k in ks]
        mats = [kk * beta * jnp.where(strict, dec, 0.0) for kk, beta, strict, dec in zip(kks, betas, stricts, decs)]
        tinvs = [t.astype(BF16) for t in _unit_triangular_inverses(mats)]
        rhs_v = [(beta * v.astype(F32)).astype(BF16) for beta, v in zip(betas, vs)]
        rhs_k = [(beta * bcast_col(cols, 4 + d) * k.astype(F32)).astype(BF16)
                 for beta, cols, d, k in zip(betas, colss, dirs, ks)]
        us = [_dot(t, r) for t, r in zip(tinvs, rhs_v)]
        ws = [_dot(t, r).astype(BF16) for t, r in zip(tinvs, rhs_k)]
        kts = [(k.astype(F32) * bcast_col(cols, 6 + d)).astype(BF16) for k, cols, d in zip(ks, colss, dirs)]
        ns = [_dot_tn(kt, w) for kt, w in zip(kts, ws)]
        cns = [_dot_tn(kt, u.astype(BF16)) for kt, u in zip(kts, us)]
        for slot, n_mat, c_mat in zip(slots, ns, cns):
            nbuf[slot] = n_mat.astype(BF16)
            cbuf[slot] = c_mat
        if with_query:
            for slot, u, w in zip(slots, us, ws):
                ubuf[slot] = u
                wbuf[slot] = w
            qks = [_dot_nt(conv_q[j], k) for (j, _), k in zip(jobs, ks)]
            for slot, qk, incl, dec in zip(slots, qks, incls, decs):
                pbuf[slot] = (qk * jnp.where(incl, dec, 0.0)).astype(BF16)
            for (j, d), cs, cols in zip(jobs, css, colss):
                qes[d, cs, :] = (conv_q[j].astype(F32) * bcast_col(cols, 4 + d)).astype(BF16)
        return carry

    lax.fori_loop(0, n_chunk // pre_chunks, prepass, 0)

    def finish(cs):
        ro = oret[cs, :]
        rc = ro - jnp.mean(ro, axis=-1, keepdims=True)
        ry = rc * lax.rsqrt(jnp.mean(rc * rc, axis=-1, keepdims=True) + EPS)
        yret_ref[cs, :] = (ry * gnw_ref[...] * _silu(rg_ref[cs, :].astype(F32))).astype(BF16)
        go = ogdn[cs, :]
        gy = go * lax.rsqrt(jnp.mean(go * go, axis=-1, keepdims=True) + EPS)
        ygdn_ref[cs, :] = (gy * rmsw_ref[...] * _silu(gz_ref[cs, :].astype(F32))).astype(BF16)

    def scan_step(n, carry, finishing):
        nds = [n, n_chunk - 1 - n]
        css = [pl.ds(pl.multiple_of(nd * CHUNK, CHUNK), CHUNK) for nd in nds]
        slots = [d * n_chunk + nd for d, nd in enumerate(nds)]
        ret_st = [state[d] for d in range(2)]
        gdn_st = [state[2 + d] for d in range(2)]
        gdn_stb = [st.astype(BF16) for st in gdn_st]
        shrink = [_dot(nbuf[slot], stb) for slot, stb in zip(slots, gdn_stb)]
        for d in range(2):
            state[2 + d] = gl_ref[bi * n_chunk + nds[d], 8 * hi + d] * gdn_st[d] - shrink[d] + cbuf[slots[d]]
            state[d] = rtile[4 * d + 3] * ret_st[d] + kvbuf[slots[d]]
        if with_query:
            vnb = [(ubuf[slot] - _dot(wbuf[slot], stb)).astype(BF16) for slot, stb in zip(slots, gdn_stb)]
            for d in range(2):
                cs = css[d]
                oret[cs, :] += _dot(rq_ref[cs, :], ret_st[d].astype(BF16)) * rtile[4 * d + 1]
                ogdn[cs, :] += _dot(pbuf[slots[d]], vnb[d]) + _dot(qes[d, cs, :], gdn_stb[d])
                if finishing:
                    finish(cs)
        return carry

    if with_query:
        lax.fori_loop(0, n_chunk // 2, functools.partial(scan_step, finishing=False), 0)
        lax.fori_loop(n_chunk // 2, n_chunk, functools.partial(scan_step, finishing=True), 0)
    else:
        lax.fori_loop(0, n_chunk, functools.partial(scan_step, finishing=False), 0)
        sfin_ref[...] = state[...]


def _mixers(lg, gl, feats, dec, conv, s0, gnw, rmsw, with_query):
    cols, grow = dec
    b, l, _ = feats["rk"].shape
    n_slot = 2 * (l // CHUNK)
    assert (l // CHUNK) % min(4, l // CHUNK) == 0
    smem = pl.BlockSpec(memory_space=pltpu.SMEM)
    head = pl.BlockSpec((None, l, HEAD_DIM), lambda i, j: (i, 0, j))
    colspec = pl.BlockSpec((None, None, l, 8), lambda i, j: (i, j, 0, 0))
    rowspec = pl.BlockSpec((None, None, 8, l), lambda i, j: (i, j, 0, 0))
    tile_f32 = pltpu.VMEM((n_slot, CHUNK, CHUNK), F32)
    tile_bf16 = pltpu.VMEM((n_slot, CHUNK, CHUNK), BF16)
    st_spec = pl.BlockSpec((None, None, 4, HEAD_DIM, HEAD_DIM), lambda i, j: (i, j, 0, 0, 0))

    def conv_spec(which):
        return pl.BlockSpec((SHORT_CONV, HEAD_DIM), lambda i, j, w=which: (0, w * N_HEADS + j))

    def gain_spec():
        return pl.BlockSpec((1, HEAD_DIM), lambda i, j: (0, j))

    if with_query:
        args = [lg, gl, feats["rk"], feats["rv"], feats["gk"], feats["gv"], feats["rq"], feats["gq"], feats["rg"],
                feats["gz"], cols, grow, conv, conv, conv, s0, gnw, rmsw]
        in_specs = [smem, smem] + [head] * 8 + [colspec, rowspec, conv_spec(0), conv_spec(1), conv_spec(2),
                                                st_spec, gain_spec(), gain_spec()]
        out_shape = [jax.ShapeDtypeStruct((b, l, MIX_W), BF16)] * 2
        out_specs = [head, head]
        scratch = ([pltpu.VMEM((2, l, HEAD_DIM), BF16)] + [pltpu.VMEM((l, HEAD_DIM), F32)] * 2
                   + [tile_f32, tile_bf16, tile_bf16, tile_bf16, tile_f32, tile_f32])
    else:
        args = [lg, gl, feats["rk"], feats["rv"], feats["gk"], feats["gv"], cols, grow, conv, conv, s0]
        in_specs = [smem, smem] + [head] * 4 + [colspec, rowspec, conv_spec(1), conv_spec(2), st_spec]
        out_shape = jax.ShapeDtypeStruct((b, N_HEADS, 4, HEAD_DIM, HEAD_DIM), F32)
        out_specs = st_spec
        scratch = [tile_bf16, tile_f32, tile_f32]
    scratch = scratch + [pltpu.VMEM((8, CHUNK, CHUNK), F32), pltpu.VMEM((4, HEAD_DIM, HEAD_DIM), F32)]
    return pl.pallas_call(
        functools.partial(_mixer_kernel, seq_len=l, with_query=with_query),
        grid=(b, N_HEADS),
        in_specs=in_specs,
        out_specs=out_specs,
        out_shape=out_shape,
        scratch_shapes=scratch,
        compiler_params=_cparams(("arbitrary", "arbitrary")),
        name="mixers_latent" if with_query else "mixers_context",
    )(*args)


def _split_bf16(v):
    hi = v.astype(BF16)
    return hi, (v - hi.astype(F32)).astype(BF16)


def _route(h2, wrt_ref, bias_ref, cand_ref, sel_ref, wd_ref):
    tm = h2.shape[0]
    h_hi, h_lo = _split_bf16(h2)
    w_hi, w_lo = _split_bf16(wrt_ref[...])
    logits = _dot_nt(w_hi, h_hi) + (_dot_nt(w_hi, h_lo) + _dot_nt(w_lo, h_hi))
    scores = _sigmoid(logits)
    biased = scores + bias_ref[...]
    neg_inf = float("-inf")
    sub = _iota((GROUP_SIZE, tm), 0).astype(F32)
    group_score = []
    for g in range(N_GROUPS):
        blk = biased[g * GROUP_SIZE:(g + 1) * GROUP_SIZE, :]
        m1 = jnp.max(blk, axis=0, keepdims=True)
        first = jnp.min(jnp.where(blk == m1, sub, float(GROUP_SIZE)), axis=0, keepdims=True)
        m2 = jnp.max(jnp.where(sub == first, neg_inf, blk), axis=0, keepdims=True)
        group_score.append(m1 + m2)
    for g in range(N_GROUPS):
        ahead = jnp.zeros((1, tm), I32)
        for g2 in range(N_GROUPS):
            if g2 == g:
                continue
            before = (group_score[g2] > group_score[g])
            if g2 < g:
                before = before | (group_score[g2] == group_score[g])
            ahead = ahead + before.astype(I32)
        keep = jnp.broadcast_to(ahead, (GROUP_SIZE, tm)) < TOPK_GROUPS
        cand_ref[g * GROUP_SIZE:(g + 1) * GROUP_SIZE, :] = jnp.where(
            keep, biased[g * GROUP_SIZE:(g + 1) * GROUP_SIZE, :], neg_inf)
    work = cand_ref[...]
    eidx = _iota((N_EXPERTS, tm), 0).astype(F32)
    sel = jnp.zeros((N_EXPERTS, tm), jnp.bool_)
    for _ in range(TOP_K):
        best = jnp.max(work, axis=0, keepdims=True)
        first = jnp.min(jnp.where(work == best, eidx, float(N_EXPERTS)), axis=0, keepdims=True)
        pick = eidx == first
        sel = sel | pick
        work = jnp.where(pick, neg_inf, work)
    picked = jnp.where(sel, scores, 0.0)
    wsum = jnp.sum(picked, axis=0, keepdims=True)
    sel_ref[...] = sel.astype(F32)
    wd_ref[...] = picked / wsum * ROUTED_SCALE


def _mixout_kernel(x_ref, yr_ref, yg_ref, g0_ref, g1_ref, g2_ref, g3_ref, wr_ref, wg_ref, wo_ref,
                   npost_ref, gate1_ref, nffn_ref, sc2_ref, sh2_ref, wrt_ref, bias_ref,
                   x1_ref, h2_ref, sel_ref, wd_ref, cand_ref):
    r = _dot(yr_ref[...], wr_ref[...])
    g = _dot(yg_ref[...], wg_ref[...])
    half = r.shape[1] // 2
    merged = jnp.concatenate(
        [g0_ref[...].astype(F32) * r[:, :half] + g2_ref[...].astype(F32) * g[:, :half],
         g1_ref[...].astype(F32) * r[:, half:] + g3_ref[...].astype(F32) * g[:, half:]], axis=1)
    mo = _dot(merged.astype(BF16), wo_ref[...])
    x1 = x_ref[...] + gate1_ref[...] * _rms(mo, npost_ref[...])
    x1_ref[...] = x1
    h2 = _rms(x1, nffn_ref[...]) * (1.0 + sc2_ref[...]) + sh2_ref[...]
    h2_ref[...] = h2.astype(BF16)
    _route(h2, wrt_ref, bias_ref, cand_ref, sel_ref, wd_ref)


def _mixer_out(x, yr, yg, gates, wr, wg, wo, npost, gate1, nffn, sc2, sh2, wrt, bias):
    b, l, d = x.shape
    tm = MIXOUT_TM
    tiles = l // tm
    tok = lambda w: pl.BlockSpec((None, tm, w), lambda i, j: (i, j, 0))
    const = functools.partial(pl.BlockSpec, pipeline_mode=pl.Buffered(1))
    vec = lambda: const((1, d), lambda i, j: (0, 0))
    bvec = lambda: pl.BlockSpec((None, 1, d), lambda i, j: (i, 0, 0))
    emap = pl.BlockSpec((N_EXPERTS, tm), lambda i, j: (0, i * tiles + j))
    in_specs = ([tok(d), tok(MIX_W), tok(MIX_W)] + [tok(MIX_W)] * 4
                + [const((MIX_W, d), lambda i, j: (0, 0)), const((MIX_W, d), lambda i, j: (0, 0)),
                   const((d, d), lambda i, j: (0, 0)),
                   vec(), bvec(), vec(), bvec(), bvec(),
                   const((N_EXPERTS, d), lambda i, j: (0, 0)), const((N_EXPERTS, 1), lambda i, j: (0, 0))])
    return pl.pallas_call(
        _mixout_kernel,
        grid=(b, tiles),
        in_specs=in_specs,
        out_specs=[tok(d), tok(d), emap, emap],
        out_shape=[jax.ShapeDtypeStruct((b, l, d), F32), jax.ShapeDtypeStruct((b, l, d), BF16),
                   jax.ShapeDtypeStruct((N_EXPERTS, b * l), F32), jax.ShapeDtypeStruct((N_EXPERTS, b * l), F32)],
        scratch_shapes=[pltpu.VMEM((N_EXPERTS, tm), F32)],
        compiler_params=_cparams(("arbitrary", "arbitrary")),
        name="mixer_out_router",
    )(x, yr, yg, *gates, wr, wg, wo, npost, gate1, nffn, sc2, sh2, wrt, bias)


def _tile_positions(sel):
    tm = sel.shape[1]
    selb = sel.astype(BF16)
    earlier = (_iota((tm, tm), 0) < _iota((tm, tm), 1)).astype(BF16)
    rank = _dot(selb, earlier)
    cnt = _dot(selb, jnp.ones((tm, tm), BF16))
    nmb = jnp.floor((cnt + (MB_ROWS - 1)) * (1.0 / MB_ROWS))
    below = (_iota((N_EXPERTS, N_EXPERTS), 1) < _iota((N_EXPERTS, N_EXPERTS), 0)).astype(BF16)
    offmb = _dot(below, nmb.astype(BF16))
    rank_hi = jnp.floor(rank * (1.0 / MB_ROWS))
    hi = jnp.where(sel > 0.0, offmb + rank_hi, 255.0)
    lo = rank - rank_hi * MB_ROWS
    return hi, lo, cnt, offmb, rank


def _dispatch_kernel(h2_ref, sel_ref, xs_ref, cnt_ref):
    last = pl.num_programs(0) - 1

    @pl.when(pl.program_id(0) < last)
    def _():
        _dispatch_tile(h2_ref, sel_ref, xs_ref, cnt_ref)

    @pl.when(pl.program_id(0) == last)
    def _():
        xs_ref[...] = jnp.zeros_like(xs_ref)


def _dispatch_tile(h2_ref, sel_ref, xs_ref, cnt_ref):
    tm = h2_ref.shape[0]
    sel = sel_ref[...]
    hi, lo, _, _, _ = _tile_positions(sel)
    hib = hi.astype(BF16)
    lob = lo.astype(BF16)
    selb = sel.astype(BF16)
    cnt_row = _dot_nt(jnp.ones((8, tm), BF16), selb)
    nmb_row = jnp.floor((cnt_row + (MB_ROWS - 1)) * (1.0 / MB_ROWS))
    before = (_iota((N_EXPERTS, N_EXPERTS), 0) < _iota((N_EXPERTS, N_EXPERTS), 1)).astype(BF16)
    start_row = _dot(nmb_row.astype(BF16), before) * MB_ROWS
    cnt_ref[...] = cnt_row
    start = start_row[0:1, :]
    end = start + cnt_row[0:1, :]
    x = h2_ref[...]
    rch = 256
    for rc in range(TILE_ROWS // rch):
        r_e = (rc * rch + _iota((rch, N_EXPERTS), 0)).astype(F32)
        owner = ((r_e >= start) & (r_e < end)).astype(BF16)
        r_t = rc * rch + _iota((rch, tm), 0)
        match = ((_dot(owner, hib) == (r_t >> _MB_SHIFT).astype(F32))
                 & (_dot(owner, lob) == (r_t & (MB_ROWS - 1)).astype(F32)))
        xs_ref[rc * rch:(rc + 1) * rch, :] = _dot(match.astype(BF16), x).astype(BF16)


def _dispatch(h2, sel_t):
    t, d = h2.shape
    tm = MOE_TM
    nt = t // tm
    return pl.pallas_call(
        _dispatch_kernel,
        grid=(nt + 1,),
        in_specs=[pl.BlockSpec((tm, d), lambda i: (jnp.minimum(i, nt - 1), 0)),
                  pl.BlockSpec((N_EXPERTS, tm), lambda i: (0, jnp.minimum(i, nt - 1)))],
        out_specs=[pl.BlockSpec((TILE_ROWS, d), lambda i: (i, 0)),
                   pl.BlockSpec((None, 8, N_EXPERTS), lambda i: (jnp.minimum(i, nt - 1), 0, 0))],
        out_shape=[jax.ShapeDtypeStruct((nt * TILE_ROWS + 3 * EXP_BM, d), BF16),
                   jax.ShapeDtypeStruct((nt, 8, N_EXPERTS), F32)],
        compiler_params=_cparams(("arbitrary",)),
        name="moe_dispatch",
    )(h2, sel_t)


def _expert_plan(cnt, n_blk):
    nt = cnt.shape[0]
    nmb = (cnt + (MB_ROWS - 1)) // MB_ROWS
    offmb = jnp.cumsum(nmb, axis=1) - nmb
    per_e = nmb.T
    incl = jnp.cumsum(per_e, axis=1)
    excl = incl - per_e
    tot = incl[:, -1]
    nb = (tot + (EXP_MB - 1)) // EXP_MB
    bend = jnp.cumsum(nb)
    bstart = bend - nb
    b = jnp.arange(n_blk, dtype=I32)
    valid = b < bend[-1]
    last = jnp.maximum(bend[-1] - 1, 0)
    bq = jnp.where(valid, b, last)
    blk_e = jnp.minimum(jnp.sum((bend[None, :] <= bq[:, None]).astype(I32), axis=1), N_EXPERTS - 1)
    onehot_e = (blk_e[:, None] == jnp.arange(N_EXPERTS, dtype=I32)[None, :]).astype(I32)
    pick = lambda table: jnp.sum(onehot_e[:, :, None] * table[None, :, :], axis=1)
    bstart_b = jnp.sum(onehot_e * bstart[None, :], axis=1)
    tot_b = jnp.sum(onehot_e * tot[None, :], axis=1)
    p0 = (bq - bstart_b) * EXP_MB
    blk_nmb = jnp.where(valid, jnp.clip(tot_b - p0, 0, EXP_MB), 0).astype(I32)
    p = p0[:, None] + jnp.arange(EXP_MB, dtype=I32)[None, :]
    incl_b = pick(incl)
    tile = jnp.minimum(jnp.sum((incl_b[:, None, :] <= p[:, :, None]).astype(I32), axis=2), nt - 1)
    onehot_t = (tile[:, :, None] == jnp.arange(nt, dtype=I32)[None, None, :]).astype(I32)
    shift_b = pick(offmb.T - excl)[:, None, :]
    where = tile * _TILE_MB + p + jnp.sum(onehot_t * shift_b, axis=2)
    j = jnp.arange(EXP_MB, dtype=I32)[None, :]
    used = j < blk_nmb[:, None]
    spare = nt * _TILE_MB + (b[:, None] % 2) * EXP_MB + j
    zeros_mb = nt * _TILE_MB + 2 * EXP_MB + j
    src = jnp.where(used, where, jnp.where(valid[:, None], where[:, :1], zeros_mb))
    dst = jnp.where(used, where, spare)
    return blk_e.astype(I32), blk_nmb, src.reshape(-1).astype(I32), dst.reshape(-1).astype(I32)


def _expert_kernel(blk_e_ref, blk_nmb_ref, src_ref, dst_ref, xs_hbm, wg_ref, wu_ref, wd_ref, ys_hbm,
                   xbuf, ybuf, wgb, wub, wdb, in_sem, out_sem):
    i = pl.program_id(0)

    def gather_copy(blk, slot, j):
        rows = pl.ds(pl.multiple_of(src_ref[blk * EXP_MB + j] * MB_ROWS, MB_ROWS), MB_ROWS)
        return pltpu.make_async_copy(xs_hbm.at[rows, :], xbuf.at[slot, j * MB_ROWS:(j + 1) * MB_ROWS, :],
                                     in_sem.at[slot])

    def scatter_copy(blk, slot, j):
        rows = pl.ds(pl.multiple_of(dst_ref[blk * EXP_MB + j] * MB_ROWS, MB_ROWS), MB_ROWS)
        return pltpu.make_async_copy(ybuf.at[slot, j * MB_ROWS:(j + 1) * MB_ROWS, :], ys_hbm.at[rows, :],
                                     out_sem.at[slot])

    def start_gather(blk):
        for j in range(EXP_MB):
            gather_copy(blk, blk % 2, j).start()

    def wait_gather(blk):
        for j in range(EXP_MB):
            gather_copy(blk, blk % 2, j).wait()

    def start_scatter(blk):
        for j in range(EXP_MB):
            scatter_copy(blk, blk % 2, j).start()

    def wait_scatter(slot):
        for j in range(EXP_MB):
            scatter_copy(0, slot, j).wait()

    @pl.when(i == 0)
    def _():
        ybuf[...] = jnp.zeros_like(ybuf)
        start_gather(0)
        spare_row0 = ys_hbm.shape[0] - 3 * EXP_BM
        for slot in range(2):
            for j in range(EXP_MB):
                spare = pl.ds(spare_row0 + (slot * EXP_MB + j) * MB_ROWS, MB_ROWS)
                pltpu.make_async_copy(ybuf.at[slot, j * MB_ROWS:(j + 1) * MB_ROWS, :], ys_hbm.at[spare, :],
                                      out_sem.at[slot]).start()

    @pl.when(blk_nmb_ref[i] > 0)
    def _():
        slot = i % 2

        @pl.when(jnp.logical_or(i == 0, blk_e_ref[i] != blk_e_ref[jnp.maximum(i - 1, 0)]))
        def _():
            wgb[...] = wg_ref[...].astype(BF16)
            wub[...] = wu_ref[...].astype(BF16)
            wdb[...] = wd_ref[...].astype(BF16)

        wait_gather(i)
        wait_scatter(slot)
        x = xbuf[slot]
        act = (_silu(_dot(x, wgb[...])) * _dot(x, wub[...])).astype(BF16)
        start_gather(i + 1)
        ybuf[slot] = _dot(act, wdb[...]).astype(BF16)
        start_scatter(i)

        @pl.when(blk_nmb_ref[i + 1] == 0)
        def _():
            wait_gather(i + 1)
            wait_scatter(1 - slot)
            wait_scatter(slot)


def _expert_ffn(xs, blk_e, blk_nmb, src, dst, wg, wu, wd):
    rows, d = xs.shape
    n_blk = blk_e.shape[0]
    de = wg.shape[2]
    grid_spec = pltpu.PrefetchScalarGridSpec(
        num_scalar_prefetch=4,
        grid=(n_blk,),
        in_specs=[pl.BlockSpec(memory_space=pl.ANY),
                  pl.BlockSpec((None, d, de), lambda i, be, bn, sr, ds: (be[i], 0, 0)),
                  pl.BlockSpec((None, d, de), lambda i, be, bn, sr, ds: (be[i], 0, 0)),
                  pl.BlockSpec((None, de, d), lambda i, be, bn, sr, ds: (be[i], 0, 0))],
        out_specs=pl.BlockSpec(memory_space=pl.ANY),
        scratch_shapes=[pltpu.VMEM((2, EXP_BM, d), BF16), pltpu.VMEM((2, EXP_BM, d), BF16),
                        pltpu.VMEM((d, de), BF16), pltpu.VMEM((d, de), BF16), pltpu.VMEM((de, d), BF16),
                        pltpu.SemaphoreType.DMA((2,)), pltpu.SemaphoreType.DMA((2,))],
    )
    return pl.pallas_call(
        _expert_kernel,
        grid_spec=grid_spec,
        out_shape=jax.ShapeDtypeStruct((rows, d), BF16),
        input_output_aliases={4: 0},
        compiler_params=_cparams(("arbitrary",)),
        name="moe_experts",
    )(blk_e, blk_nmb, src, dst, xs, wg, wu, wd)


def _combine_kernel(ys_ref, sel_ref, wd_ref, h2_ref, x1_ref, wsg_ref, wsu_ref, wsd_ref, npost_ref, gate2_ref,
                    o_ref, c_ref):
    tm = h2_ref.shape[0]
    sel = sel_ref[...]
    _, _, cnt, offmb, rank = _tile_positions(sel)
    no_rank = 1.5 * tm
    lhs = jnp.concatenate([jnp.where(sel > 0.0, rank, no_rank).T, wd_ref[...].T], axis=1).astype(BF16)
    start = offmb[:, :LANES] * MB_ROWS
    end = start + cnt[:, :LANES]
    zeros = jnp.zeros((N_EXPERTS, LANES), BF16)
    for cc in range(TILE_ROWS // LANES):
        r_e = (cc * LANES + _iota((N_EXPERTS, LANES), 1)).astype(F32)
        owned = (r_e >= start) & (r_e < end)
        owner = owned.astype(BF16)
        local = r_e[0:1, :] - jnp.sum(jnp.where(owned, start, 0.0), axis=0, keepdims=True)
        rhs = jnp.concatenate([jnp.concatenate([owner, zeros], axis=1),
                               jnp.concatenate([zeros, owner], axis=1)], axis=0)
        got = _dot(lhs, rhs)
        c_ref[:, cc * LANES:(cc + 1) * LANES] = jnp.where(got[:, :LANES] == local, got[:, LANES:], 0.0).astype(BF16)
    routed = _dot(c_ref[...], ys_ref[...])
    h2 = h2_ref[...]
    shared = _dot((_silu(_dot(h2, wsg_ref[...])) * _dot(h2, wsu_ref[...])).astype(BF16), wsd_ref[...])
    o_ref[...] = x1_ref[...] + gate2_ref[...] * _rms(routed + shared, npost_ref[...])


def _combine(ys, sel_t, wd_t, h2, x1, wsg, wsu, wsd, npost, gate2, seq_len):
    t, d = h2.shape
    tm = MOE_TM
    nt = t // tm
    per_seq = seq_len // tm
    ds = wsg.shape[1]
    const = functools.partial(pl.BlockSpec, pipeline_mode=pl.Buffered(1))
    emap = pl.BlockSpec((N_EXPERTS, tm), lambda i: (0, i))
    return pl.pallas_call(
        _combine_kernel,
        grid=(nt,),
        in_specs=[pl.BlockSpec((TILE_ROWS, d), lambda i: (i, 0)), emap, emap,
                  pl.BlockSpec((tm, d), lambda i: (i, 0)), pl.BlockSpec((tm, d), lambda i: (i, 0)),
                  const((d, ds), lambda i: (0, 0)), const((d, ds), lambda i: (0, 0)), const((ds, d), lambda i: (0, 0)),
                  const((1, d), lambda i: (0, 0)),
                  pl.BlockSpec((None, 1, d), lambda i: (i // per_seq, 0, 0))],
        out_specs=pl.BlockSpec((tm, d), lambda i: (i, 0)),
        out_shape=jax.ShapeDtypeStruct((t, d), F32),
        scratch_shapes=[pltpu.VMEM((tm, TILE_ROWS), BF16)],
        compiler_params=_cparams(("arbitrary",)),
        name="moe_combine",
    )(ys, sel_t, wd_t, h2, x1, wsg, wsu, wsd, npost, gate2)


def _rope_tables(n):
    rows = n // GRID_W
    pos_r = jnp.repeat(jnp.arange(rows, dtype=F32), GRID_W)
    pos_c = jnp.tile(jnp.arange(GRID_W, dtype=F32), rows)
    n_freq = HEAD_DIM // 4
    inv = ROPE_BASE ** (-jnp.arange(n_freq, dtype=F32) / n_freq)
    ang = jnp.concatenate([pos_r[:, None] * inv, pos_c[:, None] * inv], axis=-1)
    cos, sin = jnp.cos(ang), jnp.sin(ang)
    return jnp.concatenate([cos, cos], axis=-1), jnp.concatenate([-sin, sin], axis=-1)


def kernel(x, c, ctx, c_ctx, w_mod, b_mod, norm_mix_pre, norm_mix_post, norm_ffn_pre, norm_ffn_post, w_in, gdn_conv, ret_log_decay, gdn_a_log, gdn_dt_bias, ret_gn_w, gdn_norm_w, w_ret_out, w_gdn_out, w_o, w_router, router_bias, w_gate, w_up, w_down, w_sh_gate, w_sh_up, w_sh_down):
    b, n, d = x.shape
    depth = w_mod.shape[0]
    assert depth == 1, "single-layer block"
    assert n % max(PROJ_TM, MOE_TM) == 0 and ctx.shape[1] % CHUNK == 0
    assert _TILE_MB < 255

    rows = -(-(b + 1) // 8) * 8
    cvec = jnp.zeros((rows, d), F32).at[:b].set(c).at[b].set(c_ctx)
    mod = _modulation(cvec, w_mod[0], b_mod[0][None, :])
    sh1, sc1, g1, sh2, sc2, g2 = [mod[:b, k * d:(k + 1) * d][:, None, :] for k in range(6)]
    ctx_shift = jnp.broadcast_to(mod[b, 0:d][None, None, :], (b, 1, d))
    ctx_scale = jnp.broadcast_to(mod[b, d:2 * d][None, None, :], (b, 1, d))

    w_in0 = w_in[0]
    n_main = 4 * MIX_W
    w_state = w_in0[:, :n_main].astype(BF16)
    slots = ((0, 0), (0, 1), (1, 0), (1, 1), (0, 0), (0, 1), (0, 0), (0, 1))
    gab_cols = n_main + jnp.array([[ab * 2 * N_HEADS + dr * N_HEADS + hh for ab, dr in slots]
                                   for hh in range(N_HEADS)], I32)
    w_gab = w_in0[:, gab_cols.reshape(-1)].astype(BF16)
    n_state = n_main + N_GAB
    w_query = w_in0[:, n_state:].astype(BF16)
    w_all = jnp.concatenate([w_state, w_query], axis=1)
    a_coef = -jnp.exp(gdn_a_log[0].astype(F32))
    dtb = gdn_dt_bias[0].astype(F32)
    is_alpha = jnp.array([ab == 0 for ab, _ in slots])
    dirs = jnp.array([dr for _, dr in slots], I32)
    prow = jnp.stack([jnp.where(is_alpha[None, :], a_coef.T[:, dirs], 0.0).reshape(-1),
                      jnp.where(is_alpha[None, :], dtb.T[:, dirs], 0.0).reshape(-1)], axis=0)
    first2 = jnp.tile(jnp.arange(8) < 2, N_HEADS)
    w_gab_t = w_gab.T * first2[:, None].astype(BF16)
    pcol = prow.T * first2[:, None]
    gain_mix = norm_mix_pre[0][None, :]
    cos2, sin2 = _rope_tables(n)

    state_kinds = (_ROPE_SCALED, _PLAIN, _PLAIN, _PLAIN)
    query_kinds = (_ROPE, _PLAIN, _PLAIN, _PLAIN, _SIGMOID, _SIGMOID, _SIGMOID, _SIGMOID)
    lg = ret_log_decay[0].astype(F32)
    conv = gdn_conv[0].astype(F32)

    lc = ctx.shape[1]
    cfe, (ccols, cgrow, cgl) = _in_projection(
        ctx, gain_mix, ctx_scale, ctx_shift, w_state, w_gab, w_gab_t, prow, pcol,
        cos2[:lc], sin2[:lc], state_kinds, rope=False)
    cfeats = dict(zip(("rk", "rv", "gk", "gv"), cfe))
    zero_state = jnp.zeros((b, N_HEADS, 4, HEAD_DIM, HEAD_DIM), F32)
    init = _mixers(lg, cgl, cfeats, (ccols, cgrow), conv, zero_state, None, None, with_query=False)

    fe, (cols, grow, gl) = _in_projection(
        x, gain_mix, sc1, sh1, w_all, w_gab, w_gab_t, prow, pcol, cos2, sin2,
        state_kinds + query_kinds, rope=True)
    feats = dict(zip(("rk", "rv", "gk", "gv", "rq", "rg", "gq", "gz"), fe[:8]))
    gates = fe[8:]
    y_ret, y_gdn = _mixers(lg, gl, feats, (cols, grow), conv, init,
                           ret_gn_w[0][None, :], gdn_norm_w[0][None, :], with_query=True)
    x1, h2, sel_t, wd_t = _mixer_out(
        x, y_ret, y_gdn, gates, w_ret_out[0].astype(BF16), w_gdn_out[0].astype(BF16), w_o[0].astype(BF16),
        norm_mix_post[0][None, :], g1, norm_ffn_pre[0][None, :], sc2, sh2,
        w_router[0].T.astype(F32), router_bias[0].astype(F32)[:, None])

    t = b * n
    h2f = h2.reshape(t, d)
    xs, cnt = _dispatch(h2f, sel_t)
    nt = t // MOE_TM
    n_blk = nt * _TILE_MB // EXP_MB + N_EXPERTS + 1
    blk_e, blk_nmb, src, dst = _expert_plan(cnt[:, 0, :].astype(I32), n_blk)
    ys = _expert_ffn(xs, blk_e, blk_nmb, src, dst, w_gate[0], w_up[0], w_down[0])
    out = _combine(ys, sel_t, wd_t, h2f, x1.reshape(t, d), w_sh_gate[0].astype(BF16), w_sh_up[0].astype(BF16),
                   w_sh_down[0].astype(BF16), norm_ffn_post[0][None, :], g2, n)
    return out.reshape(b, n, d)
```

```python
import functools
import math

import jax
import jax.numpy as jnp
from jax import lax
from jax.experimental import pallas as pl
from jax.experimental.pallas import tpu as pltpu

F32 = jnp.float32
BF16 = jnp.bfloat16
I32 = jnp.int32
HIGHEST = lax.Precision.HIGHEST

N_HEADS = 4
HEAD_DIM = 128
MIX_W = N_HEADS * HEAD_DIM
CHUNK = 128
SHORT_CONV = 3
ROPE_BASE = 10000.0
GRID_W = 64
N_EXPERTS = 64
TOP_K = 8
N_GROUPS = 8
TOPK_GROUPS = 4
GROUP_SIZE = N_EXPERTS // N_GROUPS
ROUTED_SCALE = 2.5
EPS = 1e-6
N_GAB = 4 * N_HEADS

LANES = 128
BF16_TILE_ROWS = 16
VMEM_LIMIT_BYTES = 56 * 1024 * 1024

PROJ_TM = 512
MIXOUT_TM = 512
MOE_TM = 256
MB_ROWS = BF16_TILE_ROWS
_MB_SHIFT = MB_ROWS.bit_length() - 1
_TILE_MB = -(-(TOP_K * MOE_TM // MB_ROWS + N_EXPERTS * (MB_ROWS - 1) // MB_ROWS + 1) // 8) * 8
TILE_ROWS = _TILE_MB * MB_ROWS
EXP_BM = 512
EXP_MB = EXP_BM // MB_ROWS


def _cparams(sem):
    return pltpu.CompilerParams(dimension_semantics=sem, vmem_limit_bytes=VMEM_LIMIT_BYTES)


def _sigmoid(v):
    return 0.5 * jnp.tanh(0.5 * v) + 0.5


def _silu(v):
    return v * _sigmoid(v)


def _softplus(v):
    return jnp.maximum(v, 0.0) + jnp.log1p(jnp.exp(-jnp.abs(v)))


def _iota(shape, dim):
    return lax.broadcasted_iota(I32, shape, dim)


def _dot(a, b, **kw):
    return jnp.dot(a, b, preferred_element_type=F32, **kw)


def _dot_nt(a, b, **kw):
    return lax.dot_general(a, b, (((1,), (1,)), ((), ())), preferred_element_type=F32, **kw)


def _dot_tn(a, b, **kw):
    return lax.dot_general(a, b, (((0,), (0,)), ((), ())), preferred_element_type=F32, **kw)


def _rms(v, gain):
    return v * lax.rsqrt(jnp.mean(v * v, axis=-1, keepdims=True) + EPS) * gain


def _mod_kernel(c_ref, w_ref, b_ref, o_ref):
    o_ref[...] = _dot(_silu(c_ref[...]), w_ref[...], precision=HIGHEST) + b_ref[...]


def _modulation(cvec, w_mod, b_mod):
    rows, d = cvec.shape
    n = w_mod.shape[1]
    tn = 1024
    return pl.pallas_call(
        _mod_kernel,
        grid=(n // tn,),
        in_specs=[pl.BlockSpec((rows, d), lambda j: (0, 0)),
                  pl.BlockSpec((d, tn), lambda j: (0, j)),
                  pl.BlockSpec((1, tn), lambda j: (0, j))],
        out_specs=pl.BlockSpec((rows, tn), lambda j: (0, j)),
        out_shape=jax.ShapeDtypeStruct((rows, n), F32),
        compiler_params=_cparams(("arbitrary",)),
        name="adaln_modulation",
    )(cvec, w_mod, b_mod)


_PLAIN, _ROPE, _ROPE_SCALED, _SIGMOID = 0, 1, 2, 3


def _proj_kernel(x_ref, gain_ref, sc_ref, sh_ref, w_ref, wg_ref, wgt_ref, prow_ref, pcol_ref, cos_ref, sin_ref,
                 *out_refs, kinds, rope):
    n_feat = len(kinds)
    feat_refs = out_refs[:n_feat]
    cols_ref, grow_ref, gl_ref = out_refs[n_feat:]
    tm = x_ref.shape[0]

    x = x_ref[...]
    h = (_rms(x, gain_ref[...]) * (1.0 + sc_ref[...]) + sh_ref[...]).astype(BF16)

    if rope:
        cos2 = cos_ref[...]
        sin2 = sin_ref[...]

    for g, kind in enumerate(kinds):
        p = _dot(h, w_ref[:, g * MIX_W:(g + 1) * MIX_W])
        if kind == _SIGMOID:
            p = _sigmoid(p)
        elif kind in (_ROPE, _ROPE_SCALED) and rope:
            heads = []
            for hh in range(N_HEADS):
                t = p[:, hh * HEAD_DIM:(hh + 1) * HEAD_DIM]
                heads.append(t * cos2 + pltpu.roll(t, HEAD_DIM // 2, 1) * sin2)
            p = jnp.concatenate(heads, axis=1)
        if kind == _ROPE_SCALED:
            p = p * (HEAD_DIM ** -0.5)
        feat_refs[g][...] = p.astype(feat_refs[g].dtype)

    r_i = _iota((CHUNK, CHUNK), 0)
    c_i = _iota((CHUNK, CHUNK), 1)
    lower_incl = (c_i <= r_i).astype(BF16)
    upper_incl = (c_i >= r_i).astype(BF16)

    def prefix_rows(v):
        hi, lo = _split_bf16(v)
        return _dot(lower_incl, hi) + _dot(lower_incl, lo)

    def prefix_lanes(v):
        hi, lo = _split_bf16(v)
        return _dot(hi, upper_incl) + _dot(lo, upper_incl)

    nc = N_HEADS * 8
    pg = _dot(h, wg_ref[...])
    colt = _iota((tm, nc), 1) & 7
    la = jnp.where((colt == 2) | (colt == 3), 0.0, prow_ref[0:1, :] * _softplus(pg + prow_ref[1:2, :]))
    beta = _sigmoid(pg)
    colc = _iota((CHUNK, nc), 1) & 7
    fwd_col = (colc & 1) == 0
    for c in range(tm // CHUNK):
        sl = slice(c * CHUNK, (c + 1) * CHUNK)
        la_c = la[sl]
        pre = prefix_rows(la_c)
        suf = pre[CHUNK - 1:CHUNK, :] - pre + la_c
        g_c = jnp.where(fwd_col, pre, suf)
        rest = jnp.where(fwd_col, suf, pre) - la_c
        vals = jnp.where(colc < 2, g_c, jnp.where(colc < 4, beta[sl], jnp.where(
            colc < 6, jnp.exp(g_c), jnp.exp(rest))))
        for hh in range(N_HEADS):
            cols_ref[hh, sl, :] = vals[:, 8 * hh:8 * hh + 8]
        gl_ref[c:c + 1, :] = jnp.exp(g_c[0:1, :] + rest[0:1, :])

    pgt = _dot_nt(wgt_ref[...], h)
    rowq = _iota((N_HEADS * 8, tm), 0) & 7
    lat = jnp.where(rowq < 2, pcol_ref[:, 0:1] * _softplus(pgt + pcol_ref[:, 1:2]), 0.0)
    rowc = _iota((N_HEADS * 8, CHUNK), 0) & 7
    for c in range(tm // CHUNK):
        sl = slice(c * CHUNK, (c + 1) * CHUNK)
        lat_c = lat[:, sl]
        pre_t = prefix_lanes(lat_c)
        suf_t = pre_t[:, CHUNK - 1:CHUNK] - pre_t + lat_c
        grow_ref[:, :, sl] = jnp.where(rowc == 0, pre_t, suf_t).reshape(N_HEADS, 8, CHUNK)


def _in_projection(x, gain, scale, shift, w_main, w_gab, w_gab_t, prow, pcol, cos2, sin2, kinds, rope):
    b, l, d = x.shape
    tm = min(PROJ_TM, l)
    tiles = l // tm
    n_chunk = tm // CHUNK
    feat_shapes = [jax.ShapeDtypeStruct((b, l, MIX_W), BF16) for _ in kinds]
    feat_specs = [pl.BlockSpec((None, tm, MIX_W), lambda i, j: (i, j, 0)) for _ in kinds]
    out_shape = feat_shapes + [jax.ShapeDtypeStruct((b, N_HEADS, l, 8), F32),
                               jax.ShapeDtypeStruct((b, N_HEADS, 8, l), F32),
                               jax.ShapeDtypeStruct((b, tiles, n_chunk, N_HEADS * 8), F32)]
    out_specs = feat_specs + [pl.BlockSpec((None, N_HEADS, tm, 8), lambda i, j: (i, 0, j, 0)),
                              pl.BlockSpec((None, N_HEADS, 8, tm), lambda i, j: (i, 0, 0, j)),
                              pl.BlockSpec((None, None, n_chunk, N_HEADS * 8), lambda i, j: (i, j, 0, 0))]
    const = functools.partial(pl.BlockSpec, pipeline_mode=pl.Buffered(1))
    ncol = w_main.shape[1]
    in_specs = [
        pl.BlockSpec((None, tm, d), lambda i, j: (i, j, 0)),
        const((1, d), lambda i, j: (0, 0)),
        pl.BlockSpec((None, 1, d), lambda i, j: (i, 0, 0)),
        pl.BlockSpec((None, 1, d), lambda i, j: (i, 0, 0)),
        const((d, ncol), lambda i, j: (0, 0)),
        const((d, N_HEADS * 8), lambda i, j: (0, 0)),
        const((N_HEADS * 8, d), lambda i, j: (0, 0)),
        const((2, N_HEADS * 8), lambda i, j: (0, 0)),
        const((N_HEADS * 8, 2), lambda i, j: (0, 0)),
        pl.BlockSpec((tm, HEAD_DIM), lambda i, j: (j, 0)),
        pl.BlockSpec((tm, HEAD_DIM), lambda i, j: (j, 0)),
    ]
    outs = pl.pallas_call(
        functools.partial(_proj_kernel, kinds=tuple(kinds), rope=rope),
        grid=(b, tiles),
        in_specs=in_specs,
        out_specs=out_specs,
        out_shape=out_shape,
        compiler_params=_cparams(("arbitrary", "arbitrary")),
        name="in_projection_rope" if rope else "in_projection_ctx",
    )(x, gain, scale, shift, w_main, w_gab, w_gab_t, prow, pcol, cos2, sin2)
    feats = outs[:len(kinds)]
    cols, grow, gl = outs[len(kinds):]
    return feats, (cols, grow, gl.reshape(b * (l // CHUNK), N_HEADS * 8))


def _unit_triangular_inverses(mats):
    r = _iota((CHUNK, CHUNK), 0)
    c = _iota((CHUNK, CHUNK), 1)
    eye = (r == c).astype(F32)
    invs = [eye - jnp.where((r >> 1) == (c >> 1), a, 0.0) for a in mats]
    for level in range(1, int(math.log2(CHUNK))):
        mask = ((r >> (level + 1)) == (c >> (level + 1))) & ((r >> level) != (c >> level))
        invb = [inv.astype(BF16) for inv in invs]
        half = [_dot(jnp.where(mask, a, 0.0).astype(BF16), ib).astype(BF16) for a, ib in zip(mats, invb)]
        invs = [inv - _dot(ib, hf) for inv, ib, hf in zip(invs, invb, half)]
    return invs


def _mixer_kernel(*refs, seq_len, with_query):
    n_chunk = seq_len // CHUNK
    if with_query:
        (lg_ref, gl_ref, rk_ref, rv_ref, gk_ref, gv_ref, rq_ref, gq_ref, rg_ref, gz_ref,
         cols_ref, grow_ref, cq_ref, ck_ref, cv_ref, s0_ref, gnw_ref, rmsw_ref,
         yret_ref, ygdn_ref,
         qes, oret, ogdn, ubuf, wbuf, pbuf, nbuf, cbuf, kvbuf, rtile, state) = refs
    else:
        (lg_ref, gl_ref, rk_ref, rv_ref, gk_ref, gv_ref,
         cols_ref, grow_ref, ck_ref, cv_ref, s0_ref,
         sfin_ref,
         nbuf, cbuf, kvbuf, rtile, state) = refs
    bi = pl.program_id(0)
    hi = pl.program_id(1)

    row = _iota((CHUNK, CHUNK), 0)
    colm = _iota((CHUNK, CHUNK), 1)
    rowf = row.astype(F32)
    colf = colm.astype(F32)

    def conv_chunk(src_ref, w_ref, n):
        s = pl.multiple_of(n * CHUNK, CHUNK)
        x = src_ref[pl.ds(s, CHUNK), :].astype(F32)
        ps = pl.multiple_of(jnp.maximum(s - BF16_TILE_ROWS, 0), BF16_TILE_ROWS)
        ns = pl.multiple_of(jnp.minimum(s + CHUNK, seq_len - BF16_TILE_ROWS), BF16_TILE_ROWS)
        prev_row = src_ref[pl.ds(ps, BF16_TILE_ROWS), :].astype(F32)[BF16_TILE_ROWS - 1:BF16_TILE_ROWS, :]
        next_row = src_ref[pl.ds(ns, BF16_TILE_ROWS), :].astype(F32)[0:1, :]
        prev_row = prev_row * jnp.where(n > 0, 1.0, 0.0)
        next_row = next_row * jnp.where(n < n_chunk - 1, 1.0, 0.0)
        xp = jnp.where(row == 0, jnp.broadcast_to(prev_row, (CHUNK, HEAD_DIM)), pltpu.roll(x, 1, 0))
        xn = jnp.where(row == CHUNK - 1, jnp.broadcast_to(next_row, (CHUNK, HEAD_DIM)), pltpu.roll(x, CHUNK - 1, 0))
        return _silu(w_ref[0:1, :] * xp + w_ref[1:2, :] * x + w_ref[2:3, :] * xn)

    def l2n(v):
        return v * lax.rsqrt(jnp.sum(v * v, axis=-1, keepdims=True) + EPS)

    for d in range(2):
        lg = lg_ref[d, hi]
        if d == 0:
            dist, pos_q, pos_k = rowf - colf, rowf + 1.0, (CHUNK - 1.0) - rowf
        else:
            dist, pos_q, pos_k = colf - rowf, CHUNK - rowf, rowf
        rtile[4 * d + 0] = jnp.where(dist >= 0, jnp.exp(lg * jnp.maximum(dist, 0.0)), 0.0)
        rtile[4 * d + 1] = jnp.exp(lg * pos_q)
        rtile[4 * d + 2] = jnp.exp(lg * pos_k)
        rtile[4 * d + 3] = jnp.exp(lg * jnp.full((CHUNK, CHUNK), float(CHUNK), F32))

    state[...] = s0_ref[...]

    def bcast_col(cols, j):
        return jnp.broadcast_to(cols[:, j:j + 1], (CHUNK, CHUNK))

    pre_chunks = min(4, n_chunk)

    def prepass(m, carry):
        chunks = [m * pre_chunks + j for j in range(pre_chunks)]
        chunk_cs = [pl.ds(pl.multiple_of(n * CHUNK, CHUNK), CHUNK) for n in chunks]
        conv_k, conv_v, conv_q = [], [], []
        for n, cs in zip(chunks, chunk_cs):
            conv_k.append(l2n(conv_chunk(gk_ref, ck_ref, n)).astype(BF16))
            conv_v.append(conv_chunk(gv_ref, cv_ref, n).astype(BF16))
            if with_query:
                conv_q.append((l2n(conv_chunk(gq_ref, cq_ref, n)) * (HEAD_DIM ** -0.5)).astype(BF16))
                ogdn[cs, :] = jnp.zeros((CHUNK, HEAD_DIM), F32)

        jobs = [(j, d) for j in range(pre_chunks) for d in range(2)]
        css = [chunk_cs[j] for j, _ in jobs]
        slots = [d * n_chunk + chunks[j] for j, d in jobs]
        dirs = [d for _, d in jobs]
        ks = [rk_ref[cs, :] for cs in css]
        vs = [rv_ref[cs, :] for cs in css]
        if with_query:
            scs = [(_dot_nt(rq_ref[cs, :], k) * rtile[4 * d + 0]).astype(BF16) for cs, k, d in zip(css, ks, dirs)]
            outs = [_dot(sc, v) for sc, v in zip(scs, vs)]
            for j, cs in enumerate(chunk_cs):
                oret[cs, :] = outs[2 * j] + outs[2 * j + 1]
        kvs = [_dot_tn((k.astype(F32) * rtile[4 * d + 2]).astype(BF16), v) for k, v, d in zip(ks, vs, dirs)]
        for slot, kv in zip(slots, kvs):
            kvbuf[slot] = kv
        ks = [conv_k[j] for j, _ in jobs]
        vs = [conv_v[j] for j, _ in jobs]
        colss = [cols_ref[cs, :] for cs in css]
        betas = [bcast_col(cols, 2 + d) for cols, d in zip(colss, dirs)]
        incls = [(row >= colm) if d == 0 else (row <= colm) for d in dirs]
        stricts = [(row > colm) if d == 0 else (row < colm) for d in dirs]
        decs = [jnp.exp(jnp.where(incl, bcast_col(cols, d) - jnp.broadcast_to(grow_ref[d:d + 1, cs], (CHUNK, CHUNK)), 0.0))
                for cols, d, cs, incl in zip(colss, dirs, css, incls)]
        kks = [_dot_nt(k, k) for k in ks]
        mats = [kk * beta * jnp.where(strict, dec, 0.0) for kk, beta, strict, dec in zip(kks, betas, stricts, decs)]
        tinvs = [t.astype(BF16) for t in _unit_triangular_inverses(mats)]
        rhs_v = [(beta * v.astype(F32)).astype(BF16) for beta, v in zip(betas, vs)]
        rhs_k = [(beta * bcast_col(cols, 4 + d) * k.astype(F32)).astype(BF16)
                 for beta, cols, d, k in zip(betas, colss, dirs, ks)]
        us = [_dot(t, r) for t, r in zip(tinvs, rhs_v)]
        ws = [_dot(t, r).astype(BF16) for t, r in zip(tinvs, rhs_k)]
        kts = [(k.astype(F32) * bcast_col(cols, 6 + d)).astype(BF16) for k, cols, d in zip(ks, colss, dirs)]
        ns = [_dot_tn(kt, w) for kt, w in zip(kts, ws)]
        cns = [_dot_tn(kt, u.astype(BF16)) for kt, u in zip(kts, us)]
        for slot, n_mat, c_mat in zip(slots, ns, cns):
            nbuf[slot] = n_mat.astype(BF16)
            cbuf[slot] = c_mat
        if with_query:
            for slot, u, w in zip(slots, us, ws):
                ubuf[slot] = u
                wbuf[slot] = w
            qks = [_dot_nt(conv_q[j], k) for (j, _), k in zip(jobs, ks)]
            for slot, qk, incl, dec in zip(slots, qks, incls, decs):
                pbuf[slot] = (qk * jnp.where(incl, dec, 0.0)).astype(BF16)
            for (j, d), cs, cols in zip(jobs, css, colss):
                qes[d, cs, :] = (conv_q[j].astype(F32) * bcast_col(cols, 4 + d)).astype(BF16)
        return carry

    lax.fori_loop(0, n_chunk // pre_chunks, prepass, 0)

    def finish(cs):
        ro = oret[cs, :]
        rc = ro - jnp.mean(ro, axis=-1, keepdims=True)
        ry = rc * lax.rsqrt(jnp.mean(rc * rc, axis=-1, keepdims=True) + EPS)
        yret_ref[cs, :] = (ry * gnw_ref[...] * _silu(rg_ref[cs, :].astype(F32))).astype(BF16)
        go = ogdn[cs, :]
        gy = go * lax.rsqrt(jnp.mean(go * go, axis=-1, keepdims=True) + EPS)
        ygdn_ref[cs, :] = (gy * rmsw_ref[...] * _silu(gz_ref[cs, :].astype(F32))).astype(BF16)

    def scan_step(n, carry, finishing):
        nds = [n, n_chunk - 1 - n]
        css = [pl.ds(pl.multiple_of(nd * CHUNK, CHUNK), CHUNK) for nd in nds]
        slots = [d * n_chunk + nd for d, nd in enumerate(nds)]
        ret_st = [state[d] for d in range(2)]
        gdn_st = [state[2 + d] for d in range(2)]
        gdn_stb = [st.astype(BF16) for st in gdn_st]
        shrink = [_dot(nbuf[slot], stb) for slot, stb in zip(slots, gdn_stb)]
        for d in range(2):
            state[2 + d] = gl_ref[bi * n_chunk + nds[d], 8 * hi + d] * gdn_st[d] - shrink[d] + cbuf[slots[d]]
            state[d] = rtile[4 * d + 3] * ret_st[d] + kvbuf[slots[d]]
        if with_query:
            vnb = [(ubuf[slot] - _dot(wbuf[slot], stb)).astype(BF16) for slot, stb in zip(slots, gdn_stb)]
            for d in range(2):
                cs = css[d]
                oret[cs, :] += _dot(rq_ref[cs, :], ret_st[d].astype(BF16)) * rtile[4 * d + 1]
                ogdn[cs, :] += _dot(pbuf[slots[d]], vnb[d]) + _dot(qes[d, cs, :], gdn_stb[d])
                if finishing:
                    finish(cs)
        return carry

    if with_query:
        lax.fori_loop(0, n_chunk // 2, functools.partial(scan_step, finishing=False), 0)
        lax.fori_loop(n_chunk // 2, n_chunk, functools.partial(scan_step, finishing=True), 0)
    else:
        lax.fori_loop(0, n_chunk, functools.partial(scan_step, finishing=False), 0)
        sfin_ref[...] = state[...]


def _mixers(lg, gl, feats, dec, conv, s0, gnw, rmsw, with_query):
    cols, grow = dec
    b, l, _ = feats["rk"].shape
    n_slot = 2 * (l // CHUNK)
    assert (l // CHUNK) % min(4, l // CHUNK) == 0
    smem = pl.BlockSpec(memory_space=pltpu.SMEM)
    head = pl.BlockSpec((None, l, HEAD_DIM), lambda i, j: (i, 0, j))
    colspec = pl.BlockSpec((None, None, l, 8), lambda i, j: (i, j, 0, 0))
    rowspec = pl.BlockSpec((None, None, 8, l), lambda i, j: (i, j, 0, 0))
    tile_f32 = pltpu.VMEM((n_slot, CHUNK, CHUNK), F32)
    tile_bf16 = pltpu.VMEM((n_slot, CHUNK, CHUNK), BF16)
    st_spec = pl.BlockSpec((None, None, 4, HEAD_DIM, HEAD_DIM), lambda i, j: (i, j, 0, 0, 0))

    def conv_spec(which):
        return pl.BlockSpec((SHORT_CONV, HEAD_DIM), lambda i, j, w=which: (0, w * N_HEADS + j))

    def gain_spec():
        return pl.BlockSpec((1, HEAD_DIM), lambda i, j: (0, j))

    if with_query:
        args = [lg, gl, feats["rk"], feats["rv"], feats["gk"], feats["gv"], feats["rq"], feats["gq"], feats["rg"],
                feats["gz"], cols, grow, conv, conv, conv, s0, gnw, rmsw]
        in_specs = [smem, smem] + [head] * 8 + [colspec, rowspec, conv_spec(0), conv_spec(1), conv_spec(2),
                                                st_spec, gain_spec(), gain_spec()]
        out_shape = [jax.ShapeDtypeStruct((b, l, MIX_W), BF16)] * 2
        out_specs = [head, head]
        scratch = ([pltpu.VMEM((2, l, HEAD_DIM), BF16)] + [pltpu.VMEM((l, HEAD_DIM), F32)] * 2
                   + [tile_f32, tile_bf16, tile_bf16, tile_bf16, tile_f32, tile_f32])
    else:
        args = [lg, gl, feats["rk"], feats["rv"], feats["gk"], feats["gv"], cols, grow, conv, conv, s0]
        in_specs = [smem, smem] + [head] * 4 + [colspec, rowspec, conv_spec(1), conv_spec(2), st_spec]
        out_shape = jax.ShapeDtypeStruct((b, N_HEADS, 4, HEAD_DIM, HEAD_DIM), F32)
        out_specs = st_spec
        scratch = [tile_bf16, tile_f32, tile_f32]
    scratch = scratch + [pltpu.VMEM((8, CHUNK, CHUNK), F32), pltpu.VMEM((4, HEAD_DIM, HEAD_DIM), F32)]
    return pl.pallas_call(
        functools.partial(_mixer_kernel, seq_len=l, with_query=with_query),
        grid=(b, N_HEADS),
        in_specs=in_specs,
        out_specs=out_specs,
        out_shape=out_shape,
        scratch_shapes=scratch,
        compiler_params=_cparams(("arbitrary", "arbitrary")),
        name="mixers_latent" if with_query else "mixers_context",
    )(*args)


def _split_bf16(v):
    hi = v.astype(BF16)
    return hi, (v - hi.astype(F32)).astype(BF16)


def _route(h2, wrt_ref, bias_ref, cand_ref, sel_ref, wd_ref):
    tm = h2.shape[0]
    h_hi, h_lo = _split_bf16(h2)
    w_hi, w_lo = _split_bf16(wrt_ref[...])
    logits = _dot_nt(w_hi, h_hi) + (_dot_nt(w_hi, h_lo) + _dot_nt(w_lo, h_hi))
    scores = _sigmoid(logits)
    biased = scores + bias_ref[...]
    neg_inf = float("-inf")
    sub = _iota((GROUP_SIZE, tm), 0).astype(F32)
    group_score = []
    for g in range(N_GROUPS):
        blk = biased[g * GROUP_SIZE:(g + 1) * GROUP_SIZE, :]
        m1 = jnp.max(blk, axis=0, keepdims=True)
        first = jnp.min(jnp.where(blk == m1, sub, float(GROUP_SIZE)), axis=0, keepdims=True)
        m2 = jnp.max(jnp.where(sub == first, neg_inf, blk), axis=0, keepdims=True)
        group_score.append(m1 + m2)
    for g in range(N_GROUPS):
        ahead = jnp.zeros((1, tm), I32)
        for g2 in range(N_GROUPS):
            if g2 == g:
                continue
            before = (group_score[g2] > group_score[g])
            if g2 < g:
                before = before | (group_score[g2] == group_score[g])
            ahead = ahead + before.astype(I32)
        keep = jnp.broadcast_to(ahead, (GROUP_SIZE, tm)) < TOPK_GROUPS
        cand_ref[g * GROUP_SIZE:(g + 1) * GROUP_SIZE, :] = jnp.where(
            keep, biased[g * GROUP_SIZE:(g + 1) * GROUP_SIZE, :], neg_inf)
    work = cand_ref[...]
    eidx = _iota((N_EXPERTS, tm), 0).astype(F32)
    sel = jnp.zeros((N_EXPERTS, tm), jnp.bool_)
    for _ in range(TOP_K):
        best = jnp.max(work, axis=0, keepdims=True)
        first = jnp.min(jnp.where(work == best, eidx, float(N_EXPERTS)), axis=0, keepdims=True)
        pick = eidx == first
        sel = sel | pick
        work = jnp.where(pick, neg_inf, work)
    picked = jnp.where(sel, scores, 0.0)
    wsum = jnp.sum(picked, axis=0, keepdims=True)
    sel_ref[...] = sel.astype(F32)
    wd_ref[...] = picked / wsum * ROUTED_SCALE


def _mixout_kernel(x_ref, yr_ref, yg_ref, g0_ref, g1_ref, g2_ref, g3_ref, wr_ref, wg_ref, wo_ref,
                   npost_ref, gate1_ref, nffn_ref, sc2_ref, sh2_ref, wrt_ref, bias_ref,
                   x1_ref, h2_ref, sel_ref, wd_ref, cand_ref):
    r = _dot(yr_ref[...], wr_ref[...])
    g = _dot(yg_ref[...], wg_ref[...])
    half = r.shape[1] // 2
    merged = jnp.concatenate(
        [g0_ref[...].astype(F32) * r[:, :half] + g2_ref[...].astype(F32) * g[:, :half],
         g1_ref[...].astype(F32) * r[:, half:] + g3_ref[...].astype(F32) * g[:, half:]], axis=1)
    mo = _dot(merged.astype(BF16), wo_ref[...])
    x1 = x_ref[...] + gate1_ref[...] * _rms(mo, npost_ref[...])
    x1_ref[...] = x1
    h2 = _rms(x1, nffn_ref[...]) * (1.0 + sc2_ref[...]) + sh2_ref[...]
    h2_ref[...] = h2.astype(BF16)
    _route(h2, wrt_ref, bias_ref, cand_ref, sel_ref, wd_ref)


def _mixer_out(x, yr, yg, gates, wr, wg, wo, npost, gate1, nffn, sc2, sh2, wrt, bias):
    b, l, d = x.shape
    tm = MIXOUT_TM
    tiles = l // tm
    tok = lambda w: pl.BlockSpec((None, tm, w), lambda i, j: (i, j, 0))
    const = functools.partial(pl.BlockSpec, pipeline_mode=pl.Buffered(1))
    vec = lambda: const((1, d), lambda i, j: (0, 0))
    bvec = lambda: pl.BlockSpec((None, 1, d), lambda i, j: (i, 0, 0))
    emap = pl.BlockSpec((N_EXPERTS, tm), lambda i, j: (0, i * tiles + j))
    in_specs = ([tok(d), tok(MIX_W), tok(MIX_W)] + [tok(MIX_W)] * 4
                + [const((MIX_W, d), lambda i, j: (0, 0)), const((MIX_W, d), lambda i, j: (0, 0)),
                   const((d, d), lambda i, j: (0, 0)),
                   vec(), bvec(), vec(), bvec(), bvec(),
                   const((N_EXPERTS, d), lambda i, j: (0, 0)), const((N_EXPERTS, 1), lambda i, j: (0, 0))])
    return pl.pallas_call(
        _mixout_kernel,
        grid=(b, tiles),
        in_specs=in_specs,
        out_specs=[tok(d), tok(d), emap, emap],
        out_shape=[jax.ShapeDtypeStruct((b, l, d), F32), jax.ShapeDtypeStruct((b, l, d), BF16),
                   jax.ShapeDtypeStruct((N_EXPERTS, b * l), F32), jax.ShapeDtypeStruct((N_EXPERTS, b * l), F32)],
        scratch_shapes=[pltpu.VMEM((N_EXPERTS, tm), F32)],
        compiler_params=_cparams(("arbitrary", "arbitrary")),
        name="mixer_out_router",
    )(x, yr, yg, *gates, wr, wg, wo, npost, gate1, nffn, sc2, sh2, wrt, bias)


def _tile_positions(sel):
    tm = sel.shape[1]
    selb = sel.astype(BF16)
    earlier = (_iota((tm, tm), 0) < _iota((tm, tm), 1)).astype(BF16)
    rank = _dot(selb, earlier)
    cnt = _dot(selb, jnp.ones((tm, tm), BF16))
    nmb = jnp.floor((cnt + (MB_ROWS - 1)) * (1.0 / MB_ROWS))
    below = (_iota((N_EXPERTS, N_EXPERTS), 1) < _iota((N_EXPERTS, N_EXPERTS), 0)).astype(BF16)
    offmb = _dot(below, nmb.astype(BF16))
    rank_hi = jnp.floor(rank * (1.0 / MB_ROWS))
    hi = jnp.where(sel > 0.0, offmb + rank_hi, 255.0)
    lo = rank - rank_hi * MB_ROWS
    return hi, lo, cnt, offmb, rank


def _dispatch_kernel(h2_ref, sel_ref, xs_ref, cnt_ref):
    last = pl.num_programs(0) - 1

    @pl.when(pl.program_id(0) < last)
    def _():
        _dispatch_tile(h2_ref, sel_ref, xs_ref, cnt_ref)

    @pl.when(pl.program_id(0) == last)
    def _():
        xs_ref[...] = jnp.zeros_like(xs_ref)


def _dispatch_tile(h2_ref, sel_ref, xs_ref, cnt_ref):
    tm = h2_ref.shape[0]
    sel = sel_ref[...]
    hi, lo, _, _, _ = _tile_positions(sel)
    hib = hi.astype(BF16)
    lob = lo.astype(BF16)
    selb = sel.astype(BF16)
    cnt_row = _dot_nt(jnp.ones((8, tm), BF16), selb)
    nmb_row = jnp.floor((cnt_row + (MB_ROWS - 1)) * (1.0 / MB_ROWS))
    before = (_iota((N_EXPERTS, N_EXPERTS), 0) < _iota((N_EXPERTS, N_EXPERTS), 1)).astype(BF16)
    start_row = _dot(nmb_row.astype(BF16), before) * MB_ROWS
    cnt_ref[...] = cnt_row
    start = start_row[0:1, :]
    end = start + cnt_row[0:1, :]
    x = h2_ref[...]
    rch = 256
    for rc in range(TILE_ROWS // rch):
        r_e = (rc * rch + _iota((rch, N_EXPERTS), 0)).astype(F32)
        owner = ((r_e >= start) & (r_e < end)).astype(BF16)
        r_t = rc * rch + _iota((rch, tm), 0)
        match = ((_dot(owner, hib) == (r_t >> _MB_SHIFT).astype(F32))
                 & (_dot(owner, lob) == (r_t & (MB_ROWS - 1)).astype(F32)))
        xs_ref[rc * rch:(rc + 1) * rch, :] = _dot(match.astype(BF16), x).astype(BF16)


def _dispatch(h2, sel_t):
    t, d = h2.shape
    tm = MOE_TM
    nt = t // tm
    return pl.pallas_call(
        _dispatch_kernel,
        grid=(nt + 1,),
        in_specs=[pl.BlockSpec((tm, d), lambda i: (jnp.minimum(i, nt - 1), 0)),
                  pl.BlockSpec((N_EXPERTS, tm), lambda i: (0, jnp.minimum(i, nt - 1)))],
        out_specs=[pl.BlockSpec((TILE_ROWS, d), lambda i: (i, 0)),
                   pl.BlockSpec((None, 8, N_EXPERTS), lambda i: (jnp.minimum(i, nt - 1), 0, 0))],
        out_shape=[jax.ShapeDtypeStruct((nt * TILE_ROWS + 3 * EXP_BM, d), BF16),
                   jax.ShapeDtypeStruct((nt, 8, N_EXPERTS), F32)],
        compiler_params=_cparams(("arbitrary",)),
        name="moe_dispatch",
    )(h2, sel_t)


def _expert_plan(cnt, n_blk):
    nt = cnt.shape[0]
    nmb = (cnt + (MB_ROWS - 1)) // MB_ROWS
    offmb = jnp.cumsum(nmb, axis=1) - nmb
    per_e = nmb.T
    incl = jnp.cumsum(per_e, axis=1)
    excl = incl - per_e
    tot = incl[:, -1]
    nb = (tot + (EXP_MB - 1)) // EXP_MB
    bend = jnp.cumsum(nb)
    bstart = bend - nb
    b = jnp.arange(n_blk, dtype=I32)
    valid = b < bend[-1]
    last = jnp.maximum(bend[-1] - 1, 0)
    bq = jnp.where(valid, b, last)
    blk_e = jnp.minimum(jnp.sum((bend[None, :] <= bq[:, None]).astype(I32), axis=1), N_EXPERTS - 1)
    onehot_e = (blk_e[:, None] == jnp.arange(N_EXPERTS, dtype=I32)[None, :]).astype(I32)
    pick = lambda table: jnp.sum(onehot_e[:, :, None] * table[None, :, :], axis=1)
    bstart_b = jnp.sum(onehot_e * bstart[None, :], axis=1)
    tot_b = jnp.sum(onehot_e * tot[None, :], axis=1)
    p0 = (bq - bstart_b) * EXP_MB
    blk_nmb = jnp.where(valid, jnp.clip(tot_b - p0, 0, EXP_MB), 0).astype(I32)
    p = p0[:, None] + jnp.arange(EXP_MB, dtype=I32)[None, :]
    incl_b = pick(incl)
    tile = jnp.minimum(jnp.sum((incl_b[:, None, :] <= p[:, :, None]).astype(I32), axis=2), nt - 1)
    onehot_t = (tile[:, :, None] == jnp.arange(nt, dtype=I32)[None, None, :]).astype(I32)
    shift_b = pick(offmb.T - excl)[:, None, :]
    where = tile * _TILE_MB + p + jnp.sum(onehot_t * shift_b, axis=2)
    j = jnp.arange(EXP_MB, dtype=I32)[None, :]
    used = j < blk_nmb[:, None]
    spare = nt * _TILE_MB + (b[:, None] % 2) * EXP_MB + j
    zeros_mb = nt * _TILE_MB + 2 * EXP_MB + j
    src = jnp.where(used, where, jnp.where(valid[:, None], where[:, :1], zeros_mb))
    dst = jnp.where(used, where, spare)
    return blk_e.astype(I32), blk_nmb, src.reshape(-1).astype(I32), dst.reshape(-1).astype(I32)


def _expert_kernel(blk_e_ref, blk_nmb_ref, src_ref, dst_ref, xs_hbm, wg_ref, wu_ref, wd_ref, ys_hbm,
                   xbuf, ybuf, wgb, wub, wdb, in_sem, out_sem):
    i = pl.program_id(0)
    n_in = xbuf.shape[0]

    def gather_copy(blk, slot, j):
        rows = pl.ds(pl.multiple_of(src_ref[blk * EXP_MB + j] * MB_ROWS, MB_ROWS), MB_ROWS)
        return pltpu.make_async_copy(xs_hbm.at[rows, :], xbuf.at[slot, j * MB_ROWS:(j + 1) * MB_ROWS, :],
                                     in_sem.at[slot])

    def scatter_copy(blk, slot, j):
        rows = pl.ds(pl.multiple_of(dst_ref[blk * EXP_MB + j] * MB_ROWS, MB_ROWS), MB_ROWS)
        return pltpu.make_async_copy(ybuf.at[slot, j * MB_ROWS:(j + 1) * MB_ROWS, :], ys_hbm.at[rows, :],
                                     out_sem.at[slot])

    def start_gather(blk):
        for j in range(EXP_MB):
            gather_copy(blk, lax.rem(blk, n_in), j).start()

    def wait_gather(blk):
        for j in range(EXP_MB):
            gather_copy(blk, lax.rem(blk, n_in), j).wait()

    def start_scatter(blk):
        for j in range(EXP_MB):
            scatter_copy(blk, blk % 2, j).start()

    def wait_scatter(slot):
        for j in range(EXP_MB):
            scatter_copy(0, slot, j).wait()

    @pl.when(i == 0)
    def _():
        ybuf[...] = jnp.zeros_like(ybuf)
        start_gather(0)
        start_gather(1)
        spare_row0 = ys_hbm.shape[0] - 3 * EXP_BM
        for slot in range(2):
            for j in range(EXP_MB):
                spare = pl.ds(spare_row0 + (slot * EXP_MB + j) * MB_ROWS, MB_ROWS)
                pltpu.make_async_copy(ybuf.at[slot, j * MB_ROWS:(j + 1) * MB_ROWS, :], ys_hbm.at[spare, :],
                                      out_sem.at[slot]).start()

    @pl.when(blk_nmb_ref[i] > 0)
    def _():
        slot = i % 2

        @pl.when(jnp.logical_or(i == 0, blk_e_ref[i] != blk_e_ref[jnp.maximum(i - 1, 0)]))
        def _():
            wgb[...] = wg_ref[...].astype(BF16)
            wub[...] = wu_ref[...].astype(BF16)
            wdb[...] = wd_ref[...].astype(BF16)

        wait_gather(i)
        wait_scatter(slot)
        x = xbuf[lax.rem(i, n_in)]
        act = (_silu(_dot(x, wgb[...])) * _dot(x, wub[...])).astype(BF16)
        start_gather(i + 2)
        ybuf[slot] = _dot(act, wdb[...]).astype(BF16)
        start_scatter(i)

        @pl.when(blk_nmb_ref[i + 1] == 0)
        def _():
            wait_gather(i + 1)
            wait_gather(i + 2)
            wait_scatter(1 - slot)
            wait_scatter(slot)


def _expert_ffn(xs, blk_e, blk_nmb, src, dst, wg, wu, wd):
    rows, d = xs.shape
    n_blk = blk_e.shape[0]
    de = wg.shape[2]
    grid_spec = pltpu.PrefetchScalarGridSpec(
        num_scalar_prefetch=4,
        grid=(n_blk,),
        in_specs=[pl.BlockSpec(memory_space=pl.ANY),
                  pl.BlockSpec((None, d, de), lambda i, be, bn, sr, ds: (be[i], 0, 0)),
                  pl.BlockSpec((None, d, de), lambda i, be, bn, sr, ds: (be[i], 0, 0)),
                  pl.BlockSpec((None, de, d), lambda i, be, bn, sr, ds: (be[i], 0, 0))],
        out_specs=pl.BlockSpec(memory_space=pl.ANY),
        scratch_shapes=[pltpu.VMEM((3, EXP_BM, d), BF16), pltpu.VMEM((2, EXP_BM, d), BF16),
                        pltpu.VMEM((d, de), BF16), pltpu.VMEM((d, de), BF16), pltpu.VMEM((de, d), BF16),
                        pltpu.SemaphoreType.DMA((3,)), pltpu.SemaphoreType.DMA((2,))],
    )
    return pl.pallas_call(
        _expert_kernel,
        grid_spec=grid_spec,
        out_shape=jax.ShapeDtypeStruct((rows, d), BF16),
        input_output_aliases={4: 0},
        compiler_params=_cparams(("arbitrary",)),
        name="moe_experts",
    )(blk_e, blk_nmb, src, dst, xs, wg, wu, wd)


def _combine_kernel(ys_ref, sel_ref, wd_ref, h2_ref, x1_ref, wsg_ref, wsu_ref, wsd_ref, npost_ref, gate2_ref,
                    o_ref, c_ref):
    tm = h2_ref.shape[0]
    sel = sel_ref[...]
    _, _, cnt, offmb, rank = _tile_positions(sel)
    no_rank = 1.5 * tm
    lhs = jnp.concatenate([jnp.where(sel > 0.0, rank, no_rank).T, wd_ref[...].T], axis=1).astype(BF16)
    start = offmb[:, :LANES] * MB_ROWS
    end = start + cnt[:, :LANES]
    zeros = jnp.zeros((N_EXPERTS, LANES), BF16)
    for cc in range(TILE_ROWS // LANES):
        r_e = (cc * LANES + _iota((N_EXPERTS, LANES), 1)).astype(F32)
        owned = (r_e >= start) & (r_e < end)
        owner = owned.astype(BF16)
        local = r_e[0:1, :] - jnp.sum(jnp.where(owned, start, 0.0), axis=0, keepdims=True)
        rhs = jnp.concatenate([jnp.concatenate([owner, zeros], axis=1),
                               jnp.concatenate([zeros, owner], axis=1)], axis=0)
        got = _dot(lhs, rhs)
        c_ref[:, cc * LANES:(cc + 1) * LANES] = jnp.where(got[:, :LANES] == local, got[:, LANES:], 0.0).astype(BF16)
    routed = _dot(c_ref[...], ys_ref[...])
    h2 = h2_ref[...]
    shared = _dot((_silu(_dot(h2, wsg_ref[...])) * _dot(h2, wsu_ref[...])).astype(BF16), wsd_ref[...])
    o_ref[...] = x1_ref[...] + gate2_ref[...] * _rms(routed + shared, npost_ref[...])


def _combine(ys, sel_t, wd_t, h2, x1, wsg, wsu, wsd, npost, gate2, seq_len):
    t, d = h2.shape
    tm = MOE_TM
    nt = t // tm
    per_seq = seq_len // tm
    ds = wsg.shape[1]
    const = functools.partial(pl.BlockSpec, pipeline_mode=pl.Buffered(1))
    emap = pl.BlockSpec((N_EXPERTS, tm), lambda i: (0, i))
    return pl.pallas_call(
        _combine_kernel,
        grid=(nt,),
        in_specs=[pl.BlockSpec((TILE_ROWS, d), lambda i: (i, 0)), emap, emap,
                  pl.BlockSpec((tm, d), lambda i: (i, 0)), pl.BlockSpec((tm, d), lambda i: (i, 0)),
                  const((d, ds), lambda i: (0, 0)), const((d, ds), lambda i: (0, 0)), const((ds, d), lambda i: (0, 0)),
                  const((1, d), lambda i: (0, 0)),
                  pl.BlockSpec((None, 1, d), lambda i: (i // per_seq, 0, 0))],
        out_specs=pl.BlockSpec((tm, d), lambda i: (i, 0)),
        out_shape=jax.ShapeDtypeStruct((t, d), F32),
        scratch_shapes=[pltpu.VMEM((tm, TILE_ROWS), BF16)],
        compiler_params=_cparams(("arbitrary",)),
        name="moe_combine",
    )(ys, sel_t, wd_t, h2, x1, wsg, wsu, wsd, npost, gate2)


def _rope_tables(n):
    rows = n // GRID_W
    pos_r = jnp.repeat(jnp.arange(rows, dtype=F32), GRID_W)
    pos_c = jnp.tile(jnp.arange(GRID_W, dtype=F32), rows)
    n_freq = HEAD_DIM // 4
    inv = ROPE_BASE ** (-jnp.arange(n_freq, dtype=F32) / n_freq)
    ang = jnp.concatenate([pos_r[:, None] * inv, pos_c[:, None] * inv], axis=-1)
    cos, sin = jnp.cos(ang), jnp.sin(ang)
    return jnp.concatenate([cos, cos], axis=-1), jnp.concatenate([-sin, sin], axis=-1)


def kernel(x, c, ctx, c_ctx, w_mod, b_mod, norm_mix_pre, norm_mix_post, norm_ffn_pre, norm_ffn_post, w_in, gdn_conv, ret_log_decay, gdn_a_log, gdn_dt_bias, ret_gn_w, gdn_norm_w, w_ret_out, w_gdn_out, w_o, w_router, router_bias, w_gate, w_up, w_down, w_sh_gate, w_sh_up, w_sh_down):
    b, n, d = x.shape
    depth = w_mod.shape[0]
    assert depth == 1, "single-layer block"
    assert n % max(PROJ_TM, MOE_TM) == 0 and ctx.shape[1] % CHUNK == 0
    assert _TILE_MB < 255

    rows = -(-(b + 1) // 8) * 8
    cvec = jnp.zeros((rows, d), F32).at[:b].set(c).at[b].set(c_ctx)
    mod = _modulation(cvec, w_mod[0], b_mod[0][None, :])
    sh1, sc1, g1, sh2, sc2, g2 = [mod[:b, k * d:(k + 1) * d][:, None, :] for k in range(6)]
    ctx_shift = jnp.broadcast_to(mod[b, 0:d][None, None, :], (b, 1, d))
    ctx_scale = jnp.broadcast_to(mod[b, d:2 * d][None, None, :], (b, 1, d))

    w_in0 = w_in[0]
    n_main = 4 * MIX_W
    w_state = w_in0[:, :n_main].astype(BF16)
    slots = ((0, 0), (0, 1), (1, 0), (1, 1), (0, 0), (0, 1), (0, 0), (0, 1))
    gab_cols = n_main + jnp.array([[ab * 2 * N_HEADS + dr * N_HEADS + hh for ab, dr in slots]
                                   for hh in range(N_HEADS)], I32)
    w_gab = w_in0[:, gab_cols.reshape(-1)].astype(BF16)
    n_state = n_main + N_GAB
    w_query = w_in0[:, n_state:].astype(BF16)
    w_all = jnp.concatenate([w_state, w_query], axis=1)
    a_coef = -jnp.exp(gdn_a_log[0].astype(F32))
    dtb = gdn_dt_bias[0].astype(F32)
    is_alpha = jnp.array([ab == 0 for ab, _ in slots])
    dirs = jnp.array([dr for _, dr in slots], I32)
    prow = jnp.stack([jnp.where(is_alpha[None, :], a_coef.T[:, dirs], 0.0).reshape(-1),
                      jnp.where(is_alpha[None, :], dtb.T[:, dirs], 0.0).reshape(-1)], axis=0)
    first2 = jnp.tile(jnp.arange(8) < 2, N_HEADS)
    w_gab_t = w_gab.T * first2[:, None].astype(BF16)
    pcol = prow.T * first2[:, None]
    gain_mix = norm_mix_pre[0][None, :]
    cos2, sin2 = _rope_tables(n)

    state_kinds = (_ROPE_SCALED, _PLAIN, _PLAIN, _PLAIN)
    query_kinds = (_ROPE, _PLAIN, _PLAIN, _PLAIN, _SIGMOID, _SIGMOID, _SIGMOID, _SIGMOID)
    lg = ret_log_decay[0].astype(F32)
    conv = gdn_conv[0].astype(F32)

    lc = ctx.shape[1]
    cfe, (ccols, cgrow, cgl) = _in_projection(
        ctx, gain_mix, ctx_scale, ctx_shift, w_state, w_gab, w_gab_t, prow, pcol,
        cos2[:lc], sin2[:lc], state_kinds, rope=False)
    cfeats = dict(zip(("rk", "rv", "gk", "gv"), cfe))
    zero_state = jnp.zeros((b, N_HEADS, 4, HEAD_DIM, HEAD_DIM), F32)
    init = _mixers(lg, cgl, cfeats, (ccols, cgrow), conv, zero_state, None, None, with_query=False)

    fe, (cols, grow, gl) = _in_projection(
        x, gain_mix, sc1, sh1, w_all, w_gab, w_gab_t, prow, pcol, cos2, sin2,
        state_kinds + query_kinds, rope=True)
    feats = dict(zip(("rk", "rv", "gk", "gv", "rq", "rg", "gq", "gz"), fe[:8]))
    gates = fe[8:]
    y_ret, y_gdn = _mixers(lg, gl, feats, (cols, grow), conv, init,
                           ret_gn_w[0][None, :], gdn_norm_w[0][None, :], with_query=True)
    x1, h2, sel_t, wd_t = _mixer_out(
        x, y_ret, y_gdn, gates, w_ret_out[0].astype(BF16), w_gdn_out[0].astype(BF16), w_o[0].astype(BF16),
        norm_mix_post[0][None, :], g1, norm_ffn_pre[0][None, :], sc2, sh2,
        w_router[0].T.astype(F32), router_bias[0].astype(F32)[:, None])

    t = b * n
    h2f = h2.reshape(t, d)
    xs, cnt = _dispatch(h2f, sel_t)
    nt = t // MOE_TM
    n_blk = nt * _TILE_MB // EXP_MB + N_EXPERTS + 2
    blk_e, blk_nmb, src, dst = _expert_plan(cnt[:, 0, :].astype(I32), n_blk)
    ys = _expert_ffn(xs, blk_e, blk_nmb, src, dst, w_gate[0], w_up[0], w_down[0])
    out = _combine(ys, sel_t, wd_t, h2f, x1.reshape(t, d), w_sh_gate[0].astype(BF16), w_sh_up[0].astype(BF16),
                   w_sh_down[0].astype(BF16), norm_ffn_post[0][None, :], g2, n)
    return out.reshape(b, n, d)
```

```python
import functools
import math

import jax
import jax.numpy as jnp
from jax import lax
from jax.experimental import pallas as pl
from jax.experimental.pallas import tpu as pltpu

F32 = jnp.float32
BF16 = jnp.bfloat16
I32 = jnp.int32
HIGHEST = lax.Precision.HIGHEST

N_HEADS = 4
HEAD_DIM = 128
MIX_W = N_HEADS * HEAD_DIM
CHUNK = 128
SHORT_CONV = 3
ROPE_BASE = 10000.0
GRID_W = 64
N_EXPERTS = 64
TOP_K = 8
N_GROUPS = 8
TOPK_GROUPS = 4
GROUP_SIZE = N_EXPERTS // N_GROUPS
ROUTED_SCALE = 2.5
EPS = 1e-6
N_GAB = 4 * N_HEADS

LANES = 128
BF16_TILE_ROWS = 16
VMEM_LIMIT_BYTES = 56 * 1024 * 1024

PROJ_TM = 512
MIXOUT_TM = 512
MOE_TM = 256
MB_ROWS = BF16_TILE_ROWS
_MB_SHIFT = MB_ROWS.bit_length() - 1
_TILE_MB = -(-(TOP_K * MOE_TM // MB_ROWS + N_EXPERTS * (MB_ROWS - 1) // MB_ROWS + 1) // 8) * 8
TILE_ROWS = _TILE_MB * MB_ROWS
EXP_BM = 512
EXP_MB = EXP_BM // MB_ROWS
EXP_LEAD = 3


def _cparams(sem):
    return pltpu.CompilerParams(dimension_semantics=sem, vmem_limit_bytes=VMEM_LIMIT_BYTES)


def _sigmoid(v):
    return 0.5 * jnp.tanh(0.5 * v) + 0.5


def _silu(v):
    return v * _sigmoid(v)


def _softplus(v):
    return jnp.maximum(v, 0.0) + jnp.log1p(jnp.exp(-jnp.abs(v)))


def _iota(shape, dim):
    return lax.broadcasted_iota(I32, shape, dim)


def _dot(a, b, **kw):
    return jnp.dot(a, b, preferred_element_type=F32, **kw)


def _dot_nt(a, b, **kw):
    return lax.dot_general(a, b, (((1,), (1,)), ((), ())), preferred_element_type=F32, **kw)


def _dot_tn(a, b, **kw):
    return lax.dot_general(a, b, (((0,), (0,)), ((), ())), preferred_element_type=F32, **kw)


def _rms(v, gain):
    return v * lax.rsqrt(jnp.mean(v * v, axis=-1, keepdims=True) + EPS) * gain


def _mod_kernel(c_ref, w_ref, b_ref, o_ref):
    o_ref[...] = _dot(_silu(c_ref[...]), w_ref[...], precision=HIGHEST) + b_ref[...]


def _modulation(cvec, w_mod, b_mod):
    rows, d = cvec.shape
    n = w_mod.shape[1]
    tn = 1024
    return pl.pallas_call(
        _mod_kernel,
        grid=(n // tn,),
        in_specs=[pl.BlockSpec((rows, d), lambda j: (0, 0)),
                  pl.BlockSpec((d, tn), lambda j: (0, j)),
                  pl.BlockSpec((1, tn), lambda j: (0, j))],
        out_specs=pl.BlockSpec((rows, tn), lambda j: (0, j)),
        out_shape=jax.ShapeDtypeStruct((rows, n), F32),
        compiler_params=_cparams(("arbitrary",)),
        name="adaln_modulation",
    )(cvec, w_mod, b_mod)


_PLAIN, _ROPE, _ROPE_SCALED, _SIGMOID = 0, 1, 2, 3


def _proj_kernel(x_ref, gain_ref, sc_ref, sh_ref, w_ref, wg_ref, wgt_ref, prow_ref, pcol_ref, cos_ref, sin_ref,
                 *out_refs, kinds, rope):
    n_feat = len(kinds)
    feat_refs = out_refs[:n_feat]
    cols_ref, grow_ref, gl_ref = out_refs[n_feat:]
    tm = x_ref.shape[0]

    x = x_ref[...]
    h = (_rms(x, gain_ref[...]) * (1.0 + sc_ref[...]) + sh_ref[...]).astype(BF16)

    if rope:
        cos2 = cos_ref[...]
        sin2 = sin_ref[...]

    for g, kind in enumerate(kinds):
        p = _dot(h, w_ref[:, g * MIX_W:(g + 1) * MIX_W])
        if kind == _SIGMOID:
            p = _sigmoid(p)
        elif kind in (_ROPE, _ROPE_SCALED) and rope:
            heads = []
            for hh in range(N_HEADS):
                t = p[:, hh * HEAD_DIM:(hh + 1) * HEAD_DIM]
                heads.append(t * cos2 + pltpu.roll(t, HEAD_DIM // 2, 1) * sin2)
            p = jnp.concatenate(heads, axis=1)
        if kind == _ROPE_SCALED:
            p = p * (HEAD_DIM ** -0.5)
        feat_refs[g][...] = p.astype(feat_refs[g].dtype)

    r_i = _iota((CHUNK, CHUNK), 0)
    c_i = _iota((CHUNK, CHUNK), 1)
    lower_incl = (c_i <= r_i).astype(BF16)
    upper_incl = (c_i >= r_i).astype(BF16)

    def prefix_rows(v):
        hi, lo = _split_bf16(v)
        return _dot(lower_incl, hi) + _dot(lower_incl, lo)

    def prefix_lanes(v):
        hi, lo = _split_bf16(v)
        return _dot(hi, upper_incl) + _dot(lo, upper_incl)

    nc = N_HEADS * 8
    pg = _dot(h, wg_ref[...])
    colt = _iota((tm, nc), 1) & 7
    la = jnp.where((colt == 2) | (colt == 3), 0.0, prow_ref[0:1, :] * _softplus(pg + prow_ref[1:2, :]))
    beta = _sigmoid(pg)
    colc = _iota((CHUNK, nc), 1) & 7
    fwd_col = (colc & 1) == 0
    for c in range(tm // CHUNK):
        sl = slice(c * CHUNK, (c + 1) * CHUNK)
        la_c = la[sl]
        pre = prefix_rows(la_c)
        suf = pre[CHUNK - 1:CHUNK, :] - pre + la_c
        g_c = jnp.where(fwd_col, pre, suf)
        rest = jnp.where(fwd_col, suf, pre) - la_c
        vals = jnp.where(colc < 2, g_c, jnp.where(colc < 4, beta[sl], jnp.where(
            colc < 6, jnp.exp(g_c), jnp.exp(rest))))
        for hh in range(N_HEADS):
            cols_ref[hh, sl, :] = vals[:, 8 * hh:8 * hh + 8]
        gl_ref[c:c + 1, :] = jnp.exp(g_c[0:1, :] + rest[0:1, :])

    pgt = _dot_nt(wgt_ref[...], h)
    rowq = _iota((N_HEADS * 8, tm), 0) & 7
    lat = jnp.where(rowq < 2, pcol_ref[:, 0:1] * _softplus(pgt + pcol_ref[:, 1:2]), 0.0)
    rowc = _iota((N_HEADS * 8, CHUNK), 0) & 7
    for c in range(tm // CHUNK):
        sl = slice(c * CHUNK, (c + 1) * CHUNK)
        lat_c = lat[:, sl]
        pre_t = prefix_lanes(lat_c)
        suf_t = pre_t[:, CHUNK - 1:CHUNK] - pre_t + lat_c
        grow_ref[:, :, sl] = jnp.where(rowc == 0, pre_t, suf_t).reshape(N_HEADS, 8, CHUNK)


def _in_projection(x, gain, scale, shift, w_main, w_gab, w_gab_t, prow, pcol, cos2, sin2, kinds, rope):
    b, l, d = x.shape
    tm = min(PROJ_TM, l)
    tiles = l // tm
    n_chunk = tm // CHUNK
    feat_shapes = [jax.ShapeDtypeStruct((b, l, MIX_W), BF16) for _ in kinds]
    feat_specs = [pl.BlockSpec((None, tm, MIX_W), lambda i, j: (i, j, 0)) for _ in kinds]
    out_shape = feat_shapes + [jax.ShapeDtypeStruct((b, N_HEADS, l, 8), F32),
                               jax.ShapeDtypeStruct((b, N_HEADS, 8, l), F32),
                               jax.ShapeDtypeStruct((b, tiles, n_chunk, N_HEADS * 8), F32)]
    out_specs = feat_specs + [pl.BlockSpec((None, N_HEADS, tm, 8), lambda i, j: (i, 0, j, 0)),
                              pl.BlockSpec((None, N_HEADS, 8, tm), lambda i, j: (i, 0, 0, j)),
                              pl.BlockSpec((None, None, n_chunk, N_HEADS * 8), lambda i, j: (i, j, 0, 0))]
    const = functools.partial(pl.BlockSpec, pipeline_mode=pl.Buffered(1))
    ncol = w_main.shape[1]
    in_specs = [
        pl.BlockSpec((None, tm, d), lambda i, j: (i, j, 0)),
        const((1, d), lambda i, j: (0, 0)),
        pl.BlockSpec((None, 1, d), lambda i, j: (i, 0, 0)),
        pl.BlockSpec((None, 1, d), lambda i, j: (i, 0, 0)),
        const((d, ncol), lambda i, j: (0, 0)),
        const((d, N_HEADS * 8), lambda i, j: (0, 0)),
        const((N_HEADS * 8, d), lambda i, j: (0, 0)),
        const((2, N_HEADS * 8), lambda i, j: (0, 0)),
        const((N_HEADS * 8, 2), lambda i, j: (0, 0)),
        pl.BlockSpec((tm, HEAD_DIM), lambda i, j: (j, 0)),
        pl.BlockSpec((tm, HEAD_DIM), lambda i, j: (j, 0)),
    ]
    outs = pl.pallas_call(
        functools.partial(_proj_kernel, kinds=tuple(kinds), rope=rope),
        grid=(b, tiles),
        in_specs=in_specs,
        out_specs=out_specs,
        out_shape=out_shape,
        compiler_params=_cparams(("arbitrary", "arbitrary")),
        name="in_projection_rope" if rope else "in_projection_ctx",
    )(x, gain, scale, shift, w_main, w_gab, w_gab_t, prow, pcol, cos2, sin2)
    feats = outs[:len(kinds)]
    cols, grow, gl = outs[len(kinds):]
    return feats, (cols, grow, gl.reshape(b * (l // CHUNK), N_HEADS * 8))


def _unit_triangular_inverses(mats, lowers):
    r = _iota((CHUNK, CHUNK), 0)
    c = _iota((CHUNK, CHUNK), 1)
    eye = (r == c).astype(F32)
    invs = [eye - jnp.where((r >> 1) == (c >> 1), a, 0.0) for a in mats]
    for level in range(1, int(math.log2(CHUNK))):
        s = 1 << level
        mask = ((r >> (level + 1)) == (c >> (level + 1))) & ((r >> level) != (c >> level))
        invb = [inv.astype(BF16) for inv in invs]
        offs = [jnp.where(mask, a, 0.0).astype(BF16) for a in mats]
        if s < 8:
            half = [_dot(off, ib).astype(BF16) for off, ib in zip(offs, invb)]
            invs = [inv - _dot(ib, hf) for inv, ib, hf in zip(invs, invb, half)]
            continue
        def rows_of(x, lower, moving):
            first = s if (lower == moving) else 0
            return [x[g * 2 * s + first:g * 2 * s + first + s] for g in range(CHUNK // (2 * s))]

        half = [_dot(jnp.concatenate(rows_of(off, lo, True), axis=0), ib).astype(BF16)
                for off, ib, lo in zip(offs, invb, lowers)]
        zero = jnp.zeros((s, CHUNK), BF16)
        full = []
        for hf, lo in zip(half, lowers):
            pieces = []
            for g in range(CHUNK // (2 * s)):
                piece = hf[g * s:(g + 1) * s]
                pieces += [zero, piece] if lo else [piece, zero]
            full.append(jnp.concatenate(pieces, axis=0))
        corr = [_dot(jnp.concatenate(rows_of(ib, lo, True), axis=0), hf) for ib, hf, lo in zip(invb, full, lowers)]
        new = []
        for inv, cr, lo in zip(invs, corr, lowers):
            keep = rows_of(inv, lo, False)
            moved = [m - cr[g * s:(g + 1) * s] for g, m in enumerate(rows_of(inv, lo, True))]
            pieces = []
            for k, m in zip(keep, moved):
                pieces += [k, m] if lo else [m, k]
            new.append(jnp.concatenate(pieces, axis=0))
        invs = new
    return invs


def _mixer_kernel(*refs, seq_len, with_query):
    n_chunk = seq_len // CHUNK
    if with_query:
        (lg_ref, gl_ref, rk_ref, rv_ref, gk_ref, gv_ref, rq_ref, gq_ref, rg_ref, gz_ref,
         cols_ref, grow_ref, cq_ref, ck_ref, cv_ref, s0_ref, gnw_ref, rmsw_ref,
         yret_ref, ygdn_ref,
         qes, oret, ogdn, ubuf, wbuf, pbuf, nbuf, cbuf, kvbuf, rtile, state) = refs
    else:
        (lg_ref, gl_ref, rk_ref, rv_ref, gk_ref, gv_ref,
         cols_ref, grow_ref, ck_ref, cv_ref, s0_ref,
         sfin_ref,
         nbuf, cbuf, kvbuf, rtile, state) = refs
    bi = pl.program_id(0)
    hi = pl.program_id(1)

    row = _iota((CHUNK, CHUNK), 0)
    colm = _iota((CHUNK, CHUNK), 1)
    rowf = row.astype(F32)
    colf = colm.astype(F32)

    def conv_chunk(src_ref, w_ref, n):
        s = pl.multiple_of(n * CHUNK, CHUNK)
        x = src_ref[pl.ds(s, CHUNK), :].astype(F32)
        ps = pl.multiple_of(jnp.maximum(s - BF16_TILE_ROWS, 0), BF16_TILE_ROWS)
        ns = pl.multiple_of(jnp.minimum(s + CHUNK, seq_len - BF16_TILE_ROWS), BF16_TILE_ROWS)
        prev_row = src_ref[pl.ds(ps, BF16_TILE_ROWS), :].astype(F32)[BF16_TILE_ROWS - 1:BF16_TILE_ROWS, :]
        next_row = src_ref[pl.ds(ns, BF16_TILE_ROWS), :].astype(F32)[0:1, :]
        prev_row = prev_row * jnp.where(n > 0, 1.0, 0.0)
        next_row = next_row * jnp.where(n < n_chunk - 1, 1.0, 0.0)
        xp = jnp.where(row == 0, jnp.broadcast_to(prev_row, (CHUNK, HEAD_DIM)), pltpu.roll(x, 1, 0))
        xn = jnp.where(row == CHUNK - 1, jnp.broadcast_to(next_row, (CHUNK, HEAD_DIM)), pltpu.roll(x, CHUNK - 1, 0))
        return _silu(w_ref[0:1, :] * xp + w_ref[1:2, :] * x + w_ref[2:3, :] * xn)

    def l2n(v):
        return v * lax.rsqrt(jnp.sum(v * v, axis=-1, keepdims=True) + EPS)

    for d in range(2):
        lg = lg_ref[d, hi]
        if d == 0:
            dist, pos_q, pos_k = rowf - colf, rowf + 1.0, (CHUNK - 1.0) - rowf
        else:
            dist, pos_q, pos_k = colf - rowf, CHUNK - rowf, rowf
        rtile[4 * d + 0] = jnp.where(dist >= 0, jnp.exp(lg * jnp.maximum(dist, 0.0)), 0.0)
        rtile[4 * d + 1] = jnp.exp(lg * pos_q)
        rtile[4 * d + 2] = jnp.exp(lg * pos_k)
        rtile[4 * d + 3] = jnp.exp(lg * jnp.full((CHUNK, CHUNK), float(CHUNK), F32))

    state[...] = s0_ref[...]

    def bcast_col(cols, j):
        return jnp.broadcast_to(cols[:, j:j + 1], (CHUNK, CHUNK))

    pre_chunks = min(4, n_chunk)

    def prepass(m, carry):
        chunks = [m * pre_chunks + j for j in range(pre_chunks)]
        chunk_cs = [pl.ds(pl.multiple_of(n * CHUNK, CHUNK), CHUNK) for n in chunks]
        conv_k, conv_v, conv_q = [], [], []
        for n, cs in zip(chunks, chunk_cs):
            conv_k.append(l2n(conv_chunk(gk_ref, ck_ref, n)).astype(BF16))
            conv_v.append(conv_chunk(gv_ref, cv_ref, n).astype(BF16))
            if with_query:
                conv_q.append((l2n(conv_chunk(gq_ref, cq_ref, n)) * (HEAD_DIM ** -0.5)).astype(BF16))
                ogdn[cs, :] = jnp.zeros((CHUNK, HEAD_DIM), F32)

        jobs = [(j, d) for j in range(pre_chunks) for d in range(2)]
        css = [chunk_cs[j] for j, _ in jobs]
        slots = [d * n_chunk + chunks[j] for j, d in jobs]
        dirs = [d for _, d in jobs]
        ks = [rk_ref[cs, :] for cs in css]
        vs = [rv_ref[cs, :] for cs in css]
        if with_query:
            scs = [(_dot_nt(rq_ref[cs, :], k) * rtile[4 * d + 0]).astype(BF16) for cs, k, d in zip(css, ks, dirs)]
            outs = [_dot(sc, v) for sc, v in zip(scs, vs)]
            for j, cs in enumerate(chunk_cs):
                oret[cs, :] = outs[2 * j] + outs[2 * j + 1]
        kvs = [_dot_tn((k.astype(F32) * rtile[4 * d + 2]).astype(BF16), v) for k, v, d in zip(ks, vs, dirs)]
        for slot, kv in zip(slots, kvs):
            kvbuf[slot] = kv
        ks = [conv_k[j] for j, _ in jobs]
        vs = [conv_v[j] for j, _ in jobs]
        colss = [cols_ref[cs, :] for cs in css]
        betas = [bcast_col(cols, 2 + d) for cols, d in zip(colss, dirs)]
        incls = [(row >= colm) if d == 0 else (row <= colm) for d in dirs]
        stricts = [(row > colm) if d == 0 else (row < colm) for d in dirs]
        decs = [jnp.exp(jnp.where(incl, bcast_col(cols, d) - jnp.broadcast_to(grow_ref[d:d + 1, cs], (CHUNK, CHUNK)), 0.0))
                for cols, d, cs, incl in zip(colss, dirs, css, incls)]
        kks = [_dot_nt(k, k) for k in ks]
        mats = [kk * beta * jnp.where(strict, dec, 0.0) for kk, beta, strict, dec in zip(kks, betas, stricts, decs)]
        tinvs = [t.astype(BF16) for t in _unit_triangular_inverses(mats, [d == 0 for d in dirs])]
        rhs = [jnp.concatenate([(beta * v.astype(F32)).astype(BF16),
                                (beta * bcast_col(cols, 4 + d) * k.astype(F32)).astype(BF16)], axis=1)
               for beta, v, cols, d, k in zip(betas, vs, colss, dirs, ks)]
        uws = [_dot(t, r) for t, r in zip(tinvs, rhs)]
        us = [uw[:, :HEAD_DIM] for uw in uws]
        ws = [uw[:, HEAD_DIM:].astype(BF16) for uw in uws]
        kts = [(k.astype(F32) * bcast_col(cols, 6 + d)).astype(BF16) for k, cols, d in zip(ks, colss, dirs)]
        ncs = [_dot_tn(kt, uw.astype(BF16)) for kt, uw in zip(kts, uws)]
        for slot, nc_mat in zip(slots, ncs):
            cbuf[slot] = nc_mat[:, :HEAD_DIM]
            nbuf[slot] = nc_mat[:, HEAD_DIM:].astype(BF16)
        if with_query:
            for slot, u, w in zip(slots, us, ws):
                ubuf[slot] = u
                wbuf[slot] = w
            qks = [_dot_nt(conv_q[j], k) for (j, _), k in zip(jobs, ks)]
            for slot, qk, incl, dec in zip(slots, qks, incls, decs):
                pbuf[slot] = (qk * jnp.where(incl, dec, 0.0)).astype(BF16)
            for (j, d), cs, cols in zip(jobs, css, colss):
                qes[d, cs, :] = (conv_q[j].astype(F32) * bcast_col(cols, 4 + d)).astype(BF16)
        return carry

    lax.fori_loop(0, n_chunk // pre_chunks, prepass, 0)

    def finish(cs):
        ro = oret[cs, :]
        rc = ro - jnp.mean(ro, axis=-1, keepdims=True)
        ry = rc * lax.rsqrt(jnp.mean(rc * rc, axis=-1, keepdims=True) + EPS)
        yret_ref[cs, :] = (ry * gnw_ref[...] * _silu(rg_ref[cs, :].astype(F32))).astype(BF16)
        go = ogdn[cs, :]
        gy = go * lax.rsqrt(jnp.mean(go * go, axis=-1, keepdims=True) + EPS)
        ygdn_ref[cs, :] = (gy * rmsw_ref[...] * _silu(gz_ref[cs, :].astype(F32))).astype(BF16)

    def chunk_ids(n):
        nds = [n, n_chunk - 1 - n]
        css = [pl.ds(nd * CHUNK if isinstance(nd, int) else pl.multiple_of(nd * CHUNK, CHUNK), CHUNK) for nd in nds]
        slots = [d * n_chunk + nd for d, nd in enumerate(nds)]
        return nds, css, slots

    def late_outputs(n, vnb, finishing):
        _, css, slots = chunk_ids(n)
        for d in range(2):
            ogdn[css[d], :] += _dot(pbuf[slots[d]], vnb[d])
            if finishing:
                finish(css[d])

    def scan_step(n, vnb_prev, has_prev, finish_prev):
        nds, css, slots = chunk_ids(n)
        ret_st = [state[d] for d in range(2)]
        gdn_st = [state[2 + d] for d in range(2)]
        gdn_stb = [st.astype(BF16) for st in gdn_st]
        shrink = [_dot(nbuf[slot], stb) for slot, stb in zip(slots, gdn_stb)]
        for d in range(2):
            state[2 + d] = gl_ref[bi * n_chunk + nds[d], 8 * hi + d] * gdn_st[d] - shrink[d] + cbuf[slots[d]]
            state[d] = rtile[4 * d + 3] * ret_st[d] + kvbuf[slots[d]]
        if not with_query:
            return vnb_prev
        vnb = tuple((ubuf[slot] - _dot(wbuf[slot], stb)).astype(BF16) for slot, stb in zip(slots, gdn_stb))
        for d in range(2):
            oret[css[d], :] += _dot(rq_ref[css[d], :], ret_st[d].astype(BF16)) * rtile[4 * d + 1]
            ogdn[css[d], :] += _dot(qes[d, css[d], :], gdn_stb[d])
        if has_prev:
            late_outputs(n - 1, vnb_prev, finish_prev)
        return vnb

    if with_query:
        half = n_chunk // 2
        zero = jnp.zeros((CHUNK, HEAD_DIM), BF16)
        vnb = scan_step(0, (zero, zero), has_prev=False, finish_prev=False)
        vnb = lax.fori_loop(1, half + 1, functools.partial(scan_step, has_prev=True, finish_prev=False), vnb)
        vnb = lax.fori_loop(half + 1, n_chunk, functools.partial(scan_step, has_prev=True, finish_prev=True), vnb)
        late_outputs(n_chunk - 1, vnb, True)
    else:
        lax.fori_loop(0, n_chunk, functools.partial(scan_step, has_prev=False, finish_prev=False), 0)
        sfin_ref[...] = state[...]


def _mixers(lg, gl, feats, dec, conv, s0, gnw, rmsw, with_query):
    cols, grow = dec
    b, l, _ = feats["rk"].shape
    n_slot = 2 * (l // CHUNK)
    assert (l // CHUNK) % min(4, l // CHUNK) == 0
    smem = pl.BlockSpec(memory_space=pltpu.SMEM)
    head = pl.BlockSpec((None, l, HEAD_DIM), lambda i, j: (i, 0, j))
    colspec = pl.BlockSpec((None, None, l, 8), lambda i, j: (i, j, 0, 0))
    rowspec = pl.BlockSpec((None, None, 8, l), lambda i, j: (i, j, 0, 0))
    tile_f32 = pltpu.VMEM((n_slot, CHUNK, CHUNK), F32)
    tile_bf16 = pltpu.VMEM((n_slot, CHUNK, CHUNK), BF16)
    st_spec = pl.BlockSpec((None, None, 4, HEAD_DIM, HEAD_DIM), lambda i, j: (i, j, 0, 0, 0))

    def conv_spec(which):
        return pl.BlockSpec((SHORT_CONV, HEAD_DIM), lambda i, j, w=which: (0, w * N_HEADS + j))

    def gain_spec():
        return pl.BlockSpec((1, HEAD_DIM), lambda i, j: (0, j))

    if with_query:
        args = [lg, gl, feats["rk"], feats["rv"], feats["gk"], feats["gv"], feats["rq"], feats["gq"], feats["rg"],
                feats["gz"], cols, grow, conv, conv, conv, s0, gnw, rmsw]
        in_specs = [smem, smem] + [head] * 8 + [colspec, rowspec, conv_spec(0), conv_spec(1), conv_spec(2),
                                                st_spec, gain_spec(), gain_spec()]
        out_shape = [jax.ShapeDtypeStruct((b, l, MIX_W), BF16)] * 2
        out_specs = [head, head]
        scratch = ([pltpu.VMEM((2, l, HEAD_DIM), BF16)] + [pltpu.VMEM((l, HEAD_DIM), F32)] * 2
                   + [tile_f32, tile_bf16, tile_bf16, tile_bf16, tile_f32, tile_f32])
    else:
        args = [lg, gl, feats["rk"], feats["rv"], feats["gk"], feats["gv"], cols, grow, conv, conv, s0]
        in_specs = [smem, smem] + [head] * 4 + [colspec, rowspec, conv_spec(1), conv_spec(2), st_spec]
        out_shape = jax.ShapeDtypeStruct((b, N_HEADS, 4, HEAD_DIM, HEAD_DIM), F32)
        out_specs = st_spec
        scratch = [tile_bf16, tile_f32, tile_f32]
    scratch = scratch + [pltpu.VMEM((8, CHUNK, CHUNK), F32), pltpu.VMEM((4, HEAD_DIM, HEAD_DIM), F32)]
    return pl.pallas_call(
        functools.partial(_mixer_kernel, seq_len=l, with_query=with_query),
        grid=(b, N_HEADS),
        in_specs=in_specs,
        out_specs=out_specs,
        out_shape=out_shape,
        scratch_shapes=scratch,
        compiler_params=_cparams(("arbitrary", "arbitrary")),
        name="mixers_latent" if with_query else "mixers_context",
    )(*args)


def _split_bf16(v):
    hi = v.astype(BF16)
    return hi, (v - hi.astype(F32)).astype(BF16)


def _route(h2, wrt_ref, bias_ref, cand_ref, sel_ref, wd_ref):
    tm = h2.shape[0]
    h_hi, h_lo = _split_bf16(h2)
    w_hi, w_lo = _split_bf16(wrt_ref[...])
    logits = _dot_nt(w_hi, h_hi) + (_dot_nt(w_hi, h_lo) + _dot_nt(w_lo, h_hi))
    scores = _sigmoid(logits)
    biased = scores + bias_ref[...]
    neg_inf = float("-inf")
    sub = _iota((GROUP_SIZE, tm), 0).astype(F32)
    group_score = []
    for g in range(N_GROUPS):
        blk = biased[g * GROUP_SIZE:(g + 1) * GROUP_SIZE, :]
        m1 = jnp.max(blk, axis=0, keepdims=True)
        first = jnp.min(jnp.where(blk == m1, sub, float(GROUP_SIZE)), axis=0, keepdims=True)
        m2 = jnp.max(jnp.where(sub == first, neg_inf, blk), axis=0, keepdims=True)
        group_score.append(m1 + m2)
    for g in range(N_GROUPS):
        ahead = jnp.zeros((1, tm), I32)
        for g2 in range(N_GROUPS):
            if g2 == g:
                continue
            before = (group_score[g2] > group_score[g])
            if g2 < g:
                before = before | (group_score[g2] == group_score[g])
            ahead = ahead + before.astype(I32)
        keep = jnp.broadcast_to(ahead, (GROUP_SIZE, tm)) < TOPK_GROUPS
        cand_ref[g * GROUP_SIZE:(g + 1) * GROUP_SIZE, :] = jnp.where(
            keep, biased[g * GROUP_SIZE:(g + 1) * GROUP_SIZE, :], neg_inf)
    work = cand_ref[...]
    eidx = _iota((N_EXPERTS, tm), 0).astype(F32)
    sel = jnp.zeros((N_EXPERTS, tm), jnp.bool_)
    for _ in range(TOP_K):
        best = jnp.max(work, axis=0, keepdims=True)
        first = jnp.min(jnp.where(work == best, eidx, float(N_EXPERTS)), axis=0, keepdims=True)
        pick = eidx == first
        sel = sel | pick
        work = jnp.where(pick, neg_inf, work)
    picked = jnp.where(sel, scores, 0.0)
    wsum = jnp.sum(picked, axis=0, keepdims=True)
    sel_ref[...] = sel.astype(F32)
    wd_ref[...] = picked / wsum * ROUTED_SCALE


def _mixout_kernel(x_ref, yr_ref, yg_ref, g0_ref, g1_ref, g2_ref, g3_ref, wr_ref, wg_ref, wo_ref,
                   npost_ref, gate1_ref, nffn_ref, sc2_ref, sh2_ref, wrt_ref, bias_ref,
                   x1_ref, h2_ref, sel_ref, wd_ref, cand_ref):
    r = _dot(yr_ref[...], wr_ref[...])
    g = _dot(yg_ref[...], wg_ref[...])
    half = r.shape[1] // 2
    merged = jnp.concatenate(
        [g0_ref[...].astype(F32) * r[:, :half] + g2_ref[...].astype(F32) * g[:, :half],
         g1_ref[...].astype(F32) * r[:, half:] + g3_ref[...].astype(F32) * g[:, half:]], axis=1)
    mo = _dot(merged.astype(BF16), wo_ref[...])
    x1 = x_ref[...] + gate1_ref[...] * _rms(mo, npost_ref[...])
    x1_ref[...] = x1
    h2 = _rms(x1, nffn_ref[...]) * (1.0 + sc2_ref[...]) + sh2_ref[...]
    h2_ref[...] = h2.astype(BF16)
    _route(h2, wrt_ref, bias_ref, cand_ref, sel_ref, wd_ref)


def _mixer_out(x, yr, yg, gates, wr, wg, wo, npost, gate1, nffn, sc2, sh2, wrt, bias):
    b, l, d = x.shape
    tm = MIXOUT_TM
    tiles = l // tm
    tok = lambda w: pl.BlockSpec((None, tm, w), lambda i, j: (i, j, 0))
    const = functools.partial(pl.BlockSpec, pipeline_mode=pl.Buffered(1))
    vec = lambda: const((1, d), lambda i, j: (0, 0))
    bvec = lambda: pl.BlockSpec((None, 1, d), lambda i, j: (i, 0, 0))
    emap = pl.BlockSpec((N_EXPERTS, tm), lambda i, j: (0, i * tiles + j))
    in_specs = ([tok(d), tok(MIX_W), tok(MIX_W)] + [tok(MIX_W)] * 4
                + [const((MIX_W, d), lambda i, j: (0, 0)), const((MIX_W, d), lambda i, j: (0, 0)),
                   const((d, d), lambda i, j: (0, 0)),
                   vec(), bvec(), vec(), bvec(), bvec(),
                   const((N_EXPERTS, d), lambda i, j: (0, 0)), const((N_EXPERTS, 1), lambda i, j: (0, 0))])
    return pl.pallas_call(
        _mixout_kernel,
        grid=(b, tiles),
        in_specs=in_specs,
        out_specs=[tok(d), tok(d), emap, emap],
        out_shape=[jax.ShapeDtypeStruct((b, l, d), F32), jax.ShapeDtypeStruct((b, l, d), BF16),
                   jax.ShapeDtypeStruct((N_EXPERTS, b * l), F32), jax.ShapeDtypeStruct((N_EXPERTS, b * l), F32)],
        scratch_shapes=[pltpu.VMEM((N_EXPERTS, tm), F32)],
        compiler_params=_cparams(("arbitrary", "arbitrary")),
        name="mixer_out_router",
    )(x, yr, yg, *gates, wr, wg, wo, npost, gate1, nffn, sc2, sh2, wrt, bias)


def _tile_positions(sel):
    tm = sel.shape[1]
    selb = sel.astype(BF16)
    earlier = (_iota((tm, tm), 0) < _iota((tm, tm), 1)).astype(BF16)
    rank = _dot(selb, earlier)
    cnt = _dot(selb, jnp.ones((tm, tm), BF16))
    nmb = jnp.floor((cnt + (MB_ROWS - 1)) * (1.0 / MB_ROWS))
    below = (_iota((N_EXPERTS, N_EXPERTS), 1) < _iota((N_EXPERTS, N_EXPERTS), 0)).astype(BF16)
    offmb = _dot(below, nmb.astype(BF16))
    rank_hi = jnp.floor(rank * (1.0 / MB_ROWS))
    hi = jnp.where(sel > 0.0, offmb + rank_hi, 255.0)
    lo = rank - rank_hi * MB_ROWS
    return hi, lo, cnt, offmb, rank


def _dispatch_kernel(h2_ref, sel_ref, xs_ref, cnt_ref):
    last = pl.num_programs(0) - 1

    @pl.when(pl.program_id(0) < last)
    def _():
        _dispatch_tile(h2_ref, sel_ref, xs_ref, cnt_ref)

    @pl.when(pl.program_id(0) == last)
    def _():
        xs_ref[...] = jnp.zeros_like(xs_ref)


def _dispatch_tile(h2_ref, sel_ref, xs_ref, cnt_ref):
    tm = h2_ref.shape[0]
    sel = sel_ref[...]
    hi, lo, _, _, _ = _tile_positions(sel)
    hib = hi.astype(BF16)
    lob = lo.astype(BF16)
    selb = sel.astype(BF16)
    cnt_row = _dot_nt(jnp.ones((8, tm), BF16), selb)
    nmb_row = jnp.floor((cnt_row + (MB_ROWS - 1)) * (1.0 / MB_ROWS))
    before = (_iota((N_EXPERTS, N_EXPERTS), 0) < _iota((N_EXPERTS, N_EXPERTS), 1)).astype(BF16)
    start_row = _dot(nmb_row.astype(BF16), before) * MB_ROWS
    cnt_ref[...] = cnt_row
    start = start_row[0:1, :]
    end = start + cnt_row[0:1, :]
    x = h2_ref[...]
    rch = 256
    for rc in range(TILE_ROWS // rch):
        r_e = (rc * rch + _iota((rch, N_EXPERTS), 0)).astype(F32)
        owner = ((r_e >= start) & (r_e < end)).astype(BF16)
        r_t = rc * rch + _iota((rch, tm), 0)
        match = ((_dot(owner, hib) == (r_t >> _MB_SHIFT).astype(F32))
                 & (_dot(owner, lob) == (r_t & (MB_ROWS - 1)).astype(F32)))
        xs_ref[rc * rch:(rc + 1) * rch, :] = _dot(match.astype(BF16), x).astype(BF16)


def _dispatch(h2, sel_t):
    t, d = h2.shape
    tm = MOE_TM
    nt = t // tm
    return pl.pallas_call(
        _dispatch_kernel,
        grid=(nt + 1,),
        in_specs=[pl.BlockSpec((tm, d), lambda i: (jnp.minimum(i, nt - 1), 0)),
                  pl.BlockSpec((N_EXPERTS, tm), lambda i: (0, jnp.minimum(i, nt - 1)))],
        out_specs=[pl.BlockSpec((TILE_ROWS, d), lambda i: (i, 0)),
                   pl.BlockSpec((None, 8, N_EXPERTS), lambda i: (jnp.minimum(i, nt - 1), 0, 0))],
        out_shape=[jax.ShapeDtypeStruct((nt * TILE_ROWS + 3 * EXP_BM, d), BF16),
                   jax.ShapeDtypeStruct((nt, 8, N_EXPERTS), F32)],
        compiler_params=_cparams(("arbitrary",)),
        name="moe_dispatch",
    )(h2, sel_t)


def _expert_plan(cnt, n_blk):
    nt = cnt.shape[0]
    nmb = (cnt + (MB_ROWS - 1)) // MB_ROWS
    offmb = jnp.cumsum(nmb, axis=1) - nmb
    per_e = nmb.T
    incl = jnp.cumsum(per_e, axis=1)
    excl = incl - per_e
    tot = incl[:, -1]
    nb = (tot + (EXP_MB - 1)) // EXP_MB
    bend = jnp.cumsum(nb)
    bstart = bend - nb
    b = jnp.arange(n_blk, dtype=I32)
    valid = b < bend[-1]
    last = jnp.maximum(bend[-1] - 1, 0)
    bq = jnp.where(valid, b, last)
    blk_e = jnp.minimum(jnp.sum((bend[None, :] <= bq[:, None]).astype(I32), axis=1), N_EXPERTS - 1)
    onehot_e = (blk_e[:, None] == jnp.arange(N_EXPERTS, dtype=I32)[None, :]).astype(I32)
    pick = lambda table: jnp.sum(onehot_e[:, :, None] * table[None, :, :], axis=1)
    bstart_b = jnp.sum(onehot_e * bstart[None, :], axis=1)
    tot_b = jnp.sum(onehot_e * tot[None, :], axis=1)
    p0 = (bq - bstart_b) * EXP_MB
    blk_nmb = jnp.where(valid, jnp.clip(tot_b - p0, 0, EXP_MB), 0).astype(I32)
    p = p0[:, None] + jnp.arange(EXP_MB, dtype=I32)[None, :]
    incl_b = pick(incl)
    tile = jnp.minimum(jnp.sum((incl_b[:, None, :] <= p[:, :, None]).astype(I32), axis=2), nt - 1)
    onehot_t = (tile[:, :, None] == jnp.arange(nt, dtype=I32)[None, None, :]).astype(I32)
    shift_b = pick(offmb.T - excl)[:, None, :]
    where = tile * _TILE_MB + p + jnp.sum(onehot_t * shift_b, axis=2)
    j = jnp.arange(EXP_MB, dtype=I32)[None, :]
    used = j < blk_nmb[:, None]
    spare = nt * _TILE_MB + (b[:, None] % 2) * EXP_MB + j
    zeros_mb = nt * _TILE_MB + 2 * EXP_MB + j
    src = jnp.where(used, where, jnp.where(valid[:, None], where[:, :1], zeros_mb))
    dst = jnp.where(used, where, spare)
    return blk_e.astype(I32), blk_nmb, src.reshape(-1).astype(I32), dst.reshape(-1).astype(I32)


def _expert_kernel(blk_e_ref, blk_nmb_ref, src_ref, dst_ref, xs_hbm, wg_ref, wu_ref, wd_ref, ys_hbm,
                   xbuf, ybuf, wgb, wub, wdb, in_sem, out_sem):
    i = pl.program_id(0)
    n_in = xbuf.shape[0]

    def gather_copy(blk, slot, j):
        rows = pl.ds(pl.multiple_of(src_ref[blk * EXP_MB + j] * MB_ROWS, MB_ROWS), MB_ROWS)
        return pltpu.make_async_copy(xs_hbm.at[rows, :], xbuf.at[slot, j * MB_ROWS:(j + 1) * MB_ROWS, :],
                                     in_sem.at[slot])

    def scatter_copy(blk, slot, j):
        rows = pl.ds(pl.multiple_of(dst_ref[blk * EXP_MB + j] * MB_ROWS, MB_ROWS), MB_ROWS)
        return pltpu.make_async_copy(ybuf.at[slot, j * MB_ROWS:(j + 1) * MB_ROWS, :], ys_hbm.at[rows, :],
                                     out_sem.at[slot])

    def start_gather(blk):
        for j in range(EXP_MB):
            gather_copy(blk, lax.rem(blk, n_in), j).start()

    def wait_gather(blk):
        for j in range(EXP_MB):
            gather_copy(blk, lax.rem(blk, n_in), j).wait()

    def start_scatter(blk):
        for j in range(EXP_MB):
            scatter_copy(blk, blk % 2, j).start()

    def wait_scatter(slot):
        for j in range(EXP_MB):
            scatter_copy(0, slot, j).wait()

    @pl.when(i == 0)
    def _():
        ybuf[...] = jnp.zeros_like(ybuf)
        for ahead in range(EXP_LEAD):
            start_gather(ahead)
        spare_row0 = ys_hbm.shape[0] - 3 * EXP_BM
        for slot in range(2):
            for j in range(EXP_MB):
                spare = pl.ds(spare_row0 + (slot * EXP_MB + j) * MB_ROWS, MB_ROWS)
                pltpu.make_async_copy(ybuf.at[slot, j * MB_ROWS:(j + 1) * MB_ROWS, :], ys_hbm.at[spare, :],
                                      out_sem.at[slot]).start()

    @pl.when(blk_nmb_ref[i] > 0)
    def _():
        slot = i % 2

        @pl.when(jnp.logical_or(i == 0, blk_e_ref[i] != blk_e_ref[jnp.maximum(i - 1, 0)]))
        def _():
            wgb[...] = wg_ref[...].astype(BF16)
            wub[...] = wu_ref[...].astype(BF16)
            wdb[...] = wd_ref[...].astype(BF16)

        wait_gather(i)
        wait_scatter(slot)
        x = xbuf[lax.rem(i, n_in)]
        act = (_silu(_dot(x, wgb[...])) * _dot(x, wub[...])).astype(BF16)
        start_gather(i + EXP_LEAD)
        ybuf[slot] = _dot(act, wdb[...]).astype(BF16)
        start_scatter(i)

        @pl.when(blk_nmb_ref[i + 1] == 0)
        def _():
            for ahead in range(1, EXP_LEAD + 1):
                wait_gather(i + ahead)
            wait_scatter(1 - slot)
            wait_scatter(slot)


def _expert_ffn(xs, blk_e, blk_nmb, src, dst, wg, wu, wd):
    rows, d = xs.shape
    n_blk = blk_e.shape[0]
    de = wg.shape[2]
    grid_spec = pltpu.PrefetchScalarGridSpec(
        num_scalar_prefetch=4,
        grid=(n_blk,),
        in_specs=[pl.BlockSpec(memory_space=pl.ANY),
                  pl.BlockSpec((None, d, de), lambda i, be, bn, sr, ds: (be[i], 0, 0)),
                  pl.BlockSpec((None, d, de), lambda i, be, bn, sr, ds: (be[i], 0, 0)),
                  pl.BlockSpec((None, de, d), lambda i, be, bn, sr, ds: (be[i], 0, 0))],
        out_specs=pl.BlockSpec(memory_space=pl.ANY),
        scratch_shapes=[pltpu.VMEM((EXP_LEAD + 1, EXP_BM, d), BF16), pltpu.VMEM((2, EXP_BM, d), BF16),
                        pltpu.VMEM((d, de), BF16), pltpu.VMEM((d, de), BF16), pltpu.VMEM((de, d), BF16),
                        pltpu.SemaphoreType.DMA((EXP_LEAD + 1,)), pltpu.SemaphoreType.DMA((2,))],
    )
    return pl.pallas_call(
        _expert_kernel,
        grid_spec=grid_spec,
        out_shape=jax.ShapeDtypeStruct((rows, d), BF16),
        input_output_aliases={4: 0},
        compiler_params=_cparams(("arbitrary",)),
        name="moe_experts",
    )(blk_e, blk_nmb, src, dst, xs, wg, wu, wd)


def _combine_kernel(ys_ref, sel_ref, wd_ref, h2_ref, x1_ref, wsg_ref, wsu_ref, wsd_ref, npost_ref, gate2_ref,
                    o_ref, c_ref):
    tm = h2_ref.shape[0]
    sel = sel_ref[...]
    _, _, cnt, offmb, rank = _tile_positions(sel)
    no_rank = 1.5 * tm
    lhs = jnp.concatenate([jnp.where(sel > 0.0, rank, no_rank).T, wd_ref[...].T], axis=1).astype(BF16)
    start = offmb[:, :LANES] * MB_ROWS
    end = start + cnt[:, :LANES]
    zeros = jnp.zeros((N_EXPERTS, LANES), BF16)
    for cc in range(TILE_ROWS // LANES):
        r_e = (cc * LANES + _iota((N_EXPERTS, LANES), 1)).astype(F32)
        owned = (r_e >= start) & (r_e < end)
        owner = owned.astype(BF16)
        local = r_e[0:1, :] - jnp.sum(jnp.where(owned, start, 0.0), axis=0, keepdims=True)
        rhs = jnp.concatenate([jnp.concatenate([owner, zeros], axis=1),
                               jnp.concatenate([zeros, owner], axis=1)], axis=0)
        got = _dot(lhs, rhs)
        c_ref[:, cc * LANES:(cc + 1) * LANES] = jnp.where(got[:, :LANES] == local, got[:, LANES:], 0.0).astype(BF16)
    routed = _dot(c_ref[...], ys_ref[...])
    h2 = h2_ref[...]
    shared = _dot((_silu(_dot(h2, wsg_ref[...])) * _dot(h2, wsu_ref[...])).astype(BF16), wsd_ref[...])
    o_ref[...] = x1_ref[...] + gate2_ref[...] * _rms(routed + shared, npost_ref[...])


def _combine(ys, sel_t, wd_t, h2, x1, wsg, wsu, wsd, npost, gate2, seq_len):
    t, d = h2.shape
    tm = MOE_TM
    nt = t // tm
    per_seq = seq_len // tm
    ds = wsg.shape[1]
    const = functools.partial(pl.BlockSpec, pipeline_mode=pl.Buffered(1))
    emap = pl.BlockSpec((N_EXPERTS, tm), lambda i: (0, i))
    return pl.pallas_call(
        _combine_kernel,
        grid=(nt,),
        in_specs=[pl.BlockSpec((TILE_ROWS, d), lambda i: (i, 0)), emap, emap,
                  pl.BlockSpec((tm, d), lambda i: (i, 0)), pl.BlockSpec((tm, d), lambda i: (i, 0)),
                  const((d, ds), lambda i: (0, 0)), const((d, ds), lambda i: (0, 0)), const((ds, d), lambda i: (0, 0)),
                  const((1, d), lambda i: (0, 0)),
                  pl.BlockSpec((None, 1, d), lambda i: (i // per_seq, 0, 0))],
        out_specs=pl.BlockSpec((tm, d), lambda i: (i, 0)),
        out_shape=jax.ShapeDtypeStruct((t, d), F32),
        scratch_shapes=[pltpu.VMEM((tm, TILE_ROWS), BF16)],
        compiler_params=_cparams(("arbitrary",)),
        name="moe_combine",
    )(ys, sel_t, wd_t, h2, x1, wsg, wsu, wsd, npost, gate2)


def _rope_tables(n):
    rows = n // GRID_W
    pos_r = jnp.repeat(jnp.arange(rows, dtype=F32), GRID_W)
    pos_c = jnp.tile(jnp.arange(GRID_W, dtype=F32), rows)
    n_freq = HEAD_DIM // 4
    inv = ROPE_BASE ** (-jnp.arange(n_freq, dtype=F32) / n_freq)
    ang = jnp.concatenate([pos_r[:, None] * inv, pos_c[:, None] * inv], axis=-1)
    cos, sin = jnp.cos(ang), jnp.sin(ang)
    return jnp.concatenate([cos, cos], axis=-1), jnp.concatenate([-sin, sin], axis=-1)


def kernel(x, c, ctx, c_ctx, w_mod, b_mod, norm_mix_pre, norm_mix_post, norm_ffn_pre, norm_ffn_post, w_in, gdn_conv, ret_log_decay, gdn_a_log, gdn_dt_bias, ret_gn_w, gdn_norm_w, w_ret_out, w_gdn_out, w_o, w_router, router_bias, w_gate, w_up, w_down, w_sh_gate, w_sh_up, w_sh_down):
    b, n, d = x.shape
    depth = w_mod.shape[0]
    assert depth == 1, "single-layer block"
    assert n % max(PROJ_TM, MOE_TM) == 0 and ctx.shape[1] % CHUNK == 0
    assert _TILE_MB < 255

    rows = -(-(b + 1) // 8) * 8
    cvec = jnp.zeros((rows, d), F32).at[:b].set(c).at[b].set(c_ctx)
    mod = _modulation(cvec, w_mod[0], b_mod[0][None, :])
    sh1, sc1, g1, sh2, sc2, g2 = [mod[:b, k * d:(k + 1) * d][:, None, :] for k in range(6)]
    ctx_shift = jnp.broadcast_to(mod[b, 0:d][None, None, :], (b, 1, d))
    ctx_scale = jnp.broadcast_to(mod[b, d:2 * d][None, None, :], (b, 1, d))

    w_in0 = w_in[0]
    n_main = 4 * MIX_W
    w_state = w_in0[:, :n_main].astype(BF16)
    slots = ((0, 0), (0, 1), (1, 0), (1, 1), (0, 0), (0, 1), (0, 0), (0, 1))
    gab_cols = n_main + jnp.array([[ab * 2 * N_HEADS + dr * N_HEADS + hh for ab, dr in slots]
                                   for hh in range(N_HEADS)], I32)
    w_gab = w_in0[:, gab_cols.reshape(-1)].astype(BF16)
    n_state = n_main + N_GAB
    w_query = w_in0[:, n_state:].astype(BF16)
    w_all = jnp.concatenate([w_state, w_query], axis=1)
    a_coef = -jnp.exp(gdn_a_log[0].astype(F32))
    dtb = gdn_dt_bias[0].astype(F32)
    is_alpha = jnp.array([ab == 0 for ab, _ in slots])
    dirs = jnp.array([dr for _, dr in slots], I32)
    prow = jnp.stack([jnp.where(is_alpha[None, :], a_coef.T[:, dirs], 0.0).reshape(-1),
                      jnp.where(is_alpha[None, :], dtb.T[:, dirs], 0.0).reshape(-1)], axis=0)
    first2 = jnp.tile(jnp.arange(8) < 2, N_HEADS)
    w_gab_t = w_gab.T * first2[:, None].astype(BF16)
    pcol = prow.T * first2[:, None]
    gain_mix = norm_mix_pre[0][None, :]
    cos2, sin2 = _rope_tables(n)

    state_kinds = (_ROPE_SCALED, _PLAIN, _PLAIN, _PLAIN)
    query_kinds = (_ROPE, _PLAIN, _PLAIN, _PLAIN, _SIGMOID, _SIGMOID, _SIGMOID, _SIGMOID)
    lg = ret_log_decay[0].astype(F32)
    conv = gdn_conv[0].astype(F32)

    lc = ctx.shape[1]
    cfe, (ccols, cgrow, cgl) = _in_projection(
        ctx, gain_mix, ctx_scale, ctx_shift, w_state, w_gab, w_gab_t, prow, pcol,
        cos2[:lc], sin2[:lc], state_kinds, rope=False)
    cfeats = dict(zip(("rk", "rv", "gk", "gv"), cfe))
    zero_state = jnp.zeros((b, N_HEADS, 4, HEAD_DIM, HEAD_DIM), F32)
    init = _mixers(lg, cgl, cfeats, (ccols, cgrow), conv, zero_state, None, None, with_query=False)

    fe, (cols, grow, gl) = _in_projection(
        x, gain_mix, sc1, sh1, w_all, w_gab, w_gab_t, prow, pcol, cos2, sin2,
        state_kinds + query_kinds, rope=True)
    feats = dict(zip(("rk", "rv", "gk", "gv", "rq", "rg", "gq", "gz"), fe[:8]))
    gates = fe[8:]
    y_ret, y_gdn = _mixers(lg, gl, feats, (cols, grow), conv, init,
                           ret_gn_w[0][None, :], gdn_norm_w[0][None, :], with_query=True)
    x1, h2, sel_t, wd_t = _mixer_out(
        x, y_ret, y_gdn, gates, w_ret_out[0].astype(BF16), w_gdn_out[0].astype(BF16), w_o[0].astype(BF16),
        norm_mix_post[0][None, :], g1, norm_ffn_pre[0][None, :], sc2, sh2,
        w_router[0].T.astype(F32), router_bias[0].astype(F32)[:, None])

    t = b * n
    h2f = h2.reshape(t, d)
    xs, cnt = _dispatch(h2f, sel_t)
    nt = t // MOE_TM
    n_blk = nt * _TILE_MB // EXP_MB + N_EXPERTS + EXP_LEAD
    blk_e, blk_nmb, src, dst = _expert_plan(cnt[:, 0, :].astype(I32), n_blk)
    ys = _expert_ffn(xs, blk_e, blk_nmb, src, dst, w_gate[0], w_up[0], w_down[0])
    out = _combine(ys, sel_t, wd_t, h2f, x1.reshape(t, d), w_sh_gate[0].astype(BF16), w_sh_up[0].astype(BF16),
                   w_sh_down[0].astype(BF16), norm_ffn_post[0][None, :], g2, n)
    return out.reshape(b, n, d)
```

```python
import functools
import math

import jax
import jax.numpy as jnp
from jax import lax
from jax.experimental import pallas as pl
from jax.experimental.pallas import tpu as pltpu

F32 = jnp.float32
BF16 = jnp.bfloat16
I32 = jnp.int32
HIGHEST = lax.Precision.HIGHEST

N_HEADS = 4
HEAD_DIM = 128
MIX_W = N_HEADS * HEAD_DIM
CHUNK = 128
SHORT_CONV = 3
ROPE_BASE = 10000.0
GRID_W = 64
N_EXPERTS = 64
TOP_K = 8
N_GROUPS = 8
TOPK_GROUPS = 4
GROUP_SIZE = N_EXPERTS // N_GROUPS
ROUTED_SCALE = 2.5
EPS = 1e-6
N_GAB = 4 * N_HEADS

LANES = 128
BF16_TILE_ROWS = 16
VMEM_LIMIT_BYTES = 56 * 1024 * 1024

PROJ_TM = 512
MIXOUT_TM = 512
MOE_TM = 256
MB_ROWS = BF16_TILE_ROWS
_MB_SHIFT = MB_ROWS.bit_length() - 1
_TILE_MB = -(-(TOP_K * MOE_TM // MB_ROWS + N_EXPERTS * (MB_ROWS - 1) // MB_ROWS + 1) // 8) * 8
TILE_ROWS = _TILE_MB * MB_ROWS
EXP_BM = 512
EXP_MB = EXP_BM // MB_ROWS
EXP_LEAD = 3


def _cparams(sem):
    return pltpu.CompilerParams(dimension_semantics=sem, vmem_limit_bytes=VMEM_LIMIT_BYTES)


def _sigmoid(v):
    return 0.5 * jnp.tanh(0.5 * v) + 0.5


def _silu(v):
    return v * _sigmoid(v)


def _softplus(v):
    return jnp.maximum(v, 0.0) + jnp.log1p(jnp.exp(-jnp.abs(v)))


def _iota(shape, dim):
    return lax.broadcasted_iota(I32, shape, dim)


def _dot(a, b, **kw):
    return jnp.dot(a, b, preferred_element_type=F32, **kw)


def _dot_nt(a, b, **kw):
    return lax.dot_general(a, b, (((1,), (1,)), ((), ())), preferred_element_type=F32, **kw)


def _dot_tn(a, b, **kw):
    return lax.dot_general(a, b, (((0,), (0,)), ((), ())), preferred_element_type=F32, **kw)


def _rms(v, gain):
    return v * lax.rsqrt(jnp.mean(v * v, axis=-1, keepdims=True) + EPS) * gain


def _mod_kernel(c_ref, w_ref, b_ref, o_ref):
    o_ref[...] = _dot(_silu(c_ref[...]), w_ref[...], precision=HIGHEST) + b_ref[...]


def _modulation(cvec, w_mod, b_mod):
    rows, d = cvec.shape
    n = w_mod.shape[1]
    tn = 1024
    return pl.pallas_call(
        _mod_kernel,
        grid=(n // tn,),
        in_specs=[pl.BlockSpec((rows, d), lambda j: (0, 0)),
                  pl.BlockSpec((d, tn), lambda j: (0, j)),
                  pl.BlockSpec((1, tn), lambda j: (0, j))],
        out_specs=pl.BlockSpec((rows, tn), lambda j: (0, j)),
        out_shape=jax.ShapeDtypeStruct((rows, n), F32),
        compiler_params=_cparams(("arbitrary",)),
        name="adaln_modulation",
    )(cvec, w_mod, b_mod)


_PLAIN, _ROPE, _ROPE_SCALED, _SIGMOID = 0, 1, 2, 3


def _proj_kernel(x_ref, gain_ref, sc_ref, sh_ref, w_ref, wg_ref, wgt_ref, prow_ref, pcol_ref, cos_ref, sin_ref,
                 *out_refs, kinds, rope):
    n_feat = len(kinds)
    feat_refs = out_refs[:n_feat]
    cols_ref, grow_ref, gl_ref = out_refs[n_feat:]
    tm = x_ref.shape[0]

    x = x_ref[...]
    h = (_rms(x, gain_ref[...]) * (1.0 + sc_ref[...]) + sh_ref[...]).astype(BF16)

    if rope:
        cos2 = cos_ref[...]
        sin2 = sin_ref[...]

    for g, kind in enumerate(kinds):
        p = _dot(h, w_ref[:, g * MIX_W:(g + 1) * MIX_W])
        if kind == _SIGMOID:
            p = _sigmoid(p)
        elif kind in (_ROPE, _ROPE_SCALED) and rope:
            heads = []
            for hh in range(N_HEADS):
                t = p[:, hh * HEAD_DIM:(hh + 1) * HEAD_DIM]
                heads.append(t * cos2 + pltpu.roll(t, HEAD_DIM // 2, 1) * sin2)
            p = jnp.concatenate(heads, axis=1)
        if kind == _ROPE_SCALED:
            p = p * (HEAD_DIM ** -0.5)
        feat_refs[g][...] = p.astype(feat_refs[g].dtype)

    r_i = _iota((CHUNK, CHUNK), 0)
    c_i = _iota((CHUNK, CHUNK), 1)
    lower_incl = (c_i <= r_i).astype(BF16)
    upper_incl = (c_i >= r_i).astype(BF16)

    def prefix_rows(v):
        hi, lo = _split_bf16(v)
        return _dot(lower_incl, hi) + _dot(lower_incl, lo)

    def prefix_lanes(v):
        hi, lo = _split_bf16(v)
        return _dot(hi, upper_incl) + _dot(lo, upper_incl)

    nc = N_HEADS * 8
    pg = _dot(h, wg_ref[...])
    colt = _iota((tm, nc), 1) & 7
    la = jnp.where((colt == 2) | (colt == 3), 0.0, prow_ref[0:1, :] * _softplus(pg + prow_ref[1:2, :]))
    beta = _sigmoid(pg)
    colc = _iota((CHUNK, nc), 1) & 7
    fwd_col = (colc & 1) == 0
    for c in range(tm // CHUNK):
        sl = slice(c * CHUNK, (c + 1) * CHUNK)
        la_c = la[sl]
        pre = prefix_rows(la_c)
        suf = pre[CHUNK - 1:CHUNK, :] - pre + la_c
        g_c = jnp.where(fwd_col, pre, suf)
        rest = jnp.where(fwd_col, suf, pre) - la_c
        vals = jnp.where(colc < 2, g_c, jnp.where(colc < 4, beta[sl], jnp.where(
            colc < 6, jnp.exp(g_c), jnp.exp(rest))))
        for hh in range(N_HEADS):
            cols_ref[hh, sl, :] = vals[:, 8 * hh:8 * hh + 8]
        gl_ref[c:c + 1, :] = jnp.exp(g_c[0:1, :] + rest[0:1, :])

    pgt = _dot_nt(wgt_ref[...], h)
    rowq = _iota((N_HEADS * 8, tm), 0) & 7
    lat = jnp.where(rowq < 2, pcol_ref[:, 0:1] * _softplus(pgt + pcol_ref[:, 1:2]), 0.0)
    rowc = _iota((N_HEADS * 8, CHUNK), 0) & 7
    for c in range(tm // CHUNK):
        sl = slice(c * CHUNK, (c + 1) * CHUNK)
        lat_c = lat[:, sl]
        pre_t = prefix_lanes(lat_c)
        suf_t = pre_t[:, CHUNK - 1:CHUNK] - pre_t + lat_c
        grow_ref[:, :, sl] = jnp.where(rowc == 0, pre_t, suf_t).reshape(N_HEADS, 8, CHUNK)


def _in_projection(x, gain, scale, shift, w_main, w_gab, w_gab_t, prow, pcol, cos2, sin2, kinds, rope):
    b, l, d = x.shape
    tm = min(PROJ_TM, l)
    tiles = l // tm
    n_chunk = tm // CHUNK
    feat_shapes = [jax.ShapeDtypeStruct((b, l, MIX_W), BF16) for _ in kinds]
    feat_specs = [pl.BlockSpec((None, tm, MIX_W), lambda i, j: (i, j, 0)) for _ in kinds]
    out_shape = feat_shapes + [jax.ShapeDtypeStruct((b, N_HEADS, l, 8), F32),
                               jax.ShapeDtypeStruct((b, N_HEADS, 8, l), F32),
                               jax.ShapeDtypeStruct((b, tiles, n_chunk, N_HEADS * 8), F32)]
    out_specs = feat_specs + [pl.BlockSpec((None, N_HEADS, tm, 8), lambda i, j: (i, 0, j, 0)),
                              pl.BlockSpec((None, N_HEADS, 8, tm), lambda i, j: (i, 0, 0, j)),
                              pl.BlockSpec((None, None, n_chunk, N_HEADS * 8), lambda i, j: (i, j, 0, 0))]
    const = functools.partial(pl.BlockSpec, pipeline_mode=pl.Buffered(1))
    ncol = w_main.shape[1]
    in_specs = [
        pl.BlockSpec((None, tm, d), lambda i, j: (i, j, 0)),
        const((1, d), lambda i, j: (0, 0)),
        pl.BlockSpec((None, 1, d), lambda i, j: (i, 0, 0)),
        pl.BlockSpec((None, 1, d), lambda i, j: (i, 0, 0)),
        const((d, ncol), lambda i, j: (0, 0)),
        const((d, N_HEADS * 8), lambda i, j: (0, 0)),
        const((N_HEADS * 8, d), lambda i, j: (0, 0)),
        const((2, N_HEADS * 8), lambda i, j: (0, 0)),
        const((N_HEADS * 8, 2), lambda i, j: (0, 0)),
        pl.BlockSpec((tm, HEAD_DIM), lambda i, j: (j, 0)),
        pl.BlockSpec((tm, HEAD_DIM), lambda i, j: (j, 0)),
    ]
    outs = pl.pallas_call(
        functools.partial(_proj_kernel, kinds=tuple(kinds), rope=rope),
        grid=(b, tiles),
        in_specs=in_specs,
        out_specs=out_specs,
        out_shape=out_shape,
        compiler_params=_cparams(("arbitrary", "arbitrary")),
        name="in_projection_rope" if rope else "in_projection_ctx",
    )(x, gain, scale, shift, w_main, w_gab, w_gab_t, prow, pcol, cos2, sin2)
    feats = outs[:len(kinds)]
    cols, grow, gl = outs[len(kinds):]
    return feats, (cols, grow, gl.reshape(b * (l // CHUNK), N_HEADS * 8))


def _unit_triangular_inverses(mats, lowers):
    r = _iota((CHUNK, CHUNK), 0)
    c = _iota((CHUNK, CHUNK), 1)
    eye = (r == c).astype(F32)
    invs = [eye - jnp.where((r >> 1) == (c >> 1), a, 0.0) for a in mats]
    for level in range(1, int(math.log2(CHUNK))):
        s = 1 << level
        mask = ((r >> (level + 1)) == (c >> (level + 1))) & ((r >> level) != (c >> level))
        invb = [inv.astype(BF16) for inv in invs]
        offs = [jnp.where(mask, a, 0.0).astype(BF16) for a in mats]
        if s < 8:
            half = [_dot(off, ib).astype(BF16) for off, ib in zip(offs, invb)]
            invs = [inv - _dot(ib, hf) for inv, ib, hf in zip(invs, invb, half)]
            continue
        def rows_of(x, lower, moving):
            first = s if (lower == moving) else 0
            return [x[g * 2 * s + first:g * 2 * s + first + s] for g in range(CHUNK // (2 * s))]

        half = [_dot(jnp.concatenate(rows_of(off, lo, True), axis=0), ib).astype(BF16)
                for off, ib, lo in zip(offs, invb, lowers)]
        zero = jnp.zeros((s, CHUNK), BF16)
        full = []
        for hf, lo in zip(half, lowers):
            pieces = []
            for g in range(CHUNK // (2 * s)):
                piece = hf[g * s:(g + 1) * s]
                pieces += [zero, piece] if lo else [piece, zero]
            full.append(jnp.concatenate(pieces, axis=0))
        corr = [_dot(jnp.concatenate(rows_of(ib, lo, True), axis=0), hf) for ib, hf, lo in zip(invb, full, lowers)]
        new = []
        for inv, cr, lo in zip(invs, corr, lowers):
            keep = rows_of(inv, lo, False)
            moved = [m - cr[g * s:(g + 1) * s] for g, m in enumerate(rows_of(inv, lo, True))]
            pieces = []
            for k, m in zip(keep, moved):
                pieces += [k, m] if lo else [m, k]
            new.append(jnp.concatenate(pieces, axis=0))
        invs = new
    return invs


def _mixer_kernel(*refs, seq_len, with_query):
    n_chunk = seq_len // CHUNK
    if with_query:
        (lg_ref, gl_ref, rk_ref, rv_ref, gk_ref, gv_ref, rq_ref, gq_ref, rg_ref, gz_ref,
         cols_ref, grow_ref, cq_ref, ck_ref, cv_ref, s0_ref, gnw_ref, rmsw_ref,
         yret_ref, ygdn_ref,
         qes, oret, ogdn, ubuf, wbuf, pbuf, nbuf, cbuf, kvbuf, rtile, state) = refs
    else:
        (lg_ref, gl_ref, rk_ref, rv_ref, gk_ref, gv_ref,
         cols_ref, grow_ref, ck_ref, cv_ref, s0_ref,
         sfin_ref,
         nbuf, cbuf, kvbuf, rtile, state) = refs
    bi = pl.program_id(0)
    hi = pl.program_id(1)

    row = _iota((CHUNK, CHUNK), 0)
    colm = _iota((CHUNK, CHUNK), 1)
    rowf = row.astype(F32)
    colf = colm.astype(F32)

    def conv_chunk(src_ref, w_ref, n):
        s = pl.multiple_of(n * CHUNK, CHUNK)
        x = src_ref[pl.ds(s, CHUNK), :].astype(F32)
        ps = pl.multiple_of(jnp.maximum(s - BF16_TILE_ROWS, 0), BF16_TILE_ROWS)
        ns = pl.multiple_of(jnp.minimum(s + CHUNK, seq_len - BF16_TILE_ROWS), BF16_TILE_ROWS)
        prev_row = src_ref[pl.ds(ps, BF16_TILE_ROWS), :].astype(F32)[BF16_TILE_ROWS - 1:BF16_TILE_ROWS, :]
        next_row = src_ref[pl.ds(ns, BF16_TILE_ROWS), :].astype(F32)[0:1, :]
        prev_row = prev_row * jnp.where(n > 0, 1.0, 0.0)
        next_row = next_row * jnp.where(n < n_chunk - 1, 1.0, 0.0)
        xp = jnp.where(row == 0, jnp.broadcast_to(prev_row, (CHUNK, HEAD_DIM)), pltpu.roll(x, 1, 0))
        xn = jnp.where(row == CHUNK - 1, jnp.broadcast_to(next_row, (CHUNK, HEAD_DIM)), pltpu.roll(x, CHUNK - 1, 0))
        return _silu(w_ref[0:1, :] * xp + w_ref[1:2, :] * x + w_ref[2:3, :] * xn)

    def l2n(v):
        return v * lax.rsqrt(jnp.sum(v * v, axis=-1, keepdims=True) + EPS)

    for d in range(2):
        lg = lg_ref[d, hi]
        if d == 0:
            dist, pos_q, pos_k = rowf - colf, rowf + 1.0, (CHUNK - 1.0) - rowf
        else:
            dist, pos_q, pos_k = colf - rowf, CHUNK - rowf, rowf
        rtile[4 * d + 0] = jnp.where(dist >= 0, jnp.exp(lg * jnp.maximum(dist, 0.0)), 0.0)
        rtile[4 * d + 1] = jnp.exp(lg * pos_q)
        rtile[4 * d + 2] = jnp.exp(lg * pos_k)
        rtile[4 * d + 3] = jnp.exp(lg * jnp.full((CHUNK, CHUNK), float(CHUNK), F32))

    state[...] = s0_ref[...]

    def bcast_col(cols, j):
        return jnp.broadcast_to(cols[:, j:j + 1], (CHUNK, CHUNK))

    pre_chunks = min(4, n_chunk)

    def prepass(m, carry):
        chunks = [m * pre_chunks + j for j in range(pre_chunks)]
        chunk_cs = [pl.ds(pl.multiple_of(n * CHUNK, CHUNK), CHUNK) for n in chunks]
        conv_k, conv_v, conv_q = [], [], []
        for n, cs in zip(chunks, chunk_cs):
            conv_k.append(l2n(conv_chunk(gk_ref, ck_ref, n)).astype(BF16))
            conv_v.append(conv_chunk(gv_ref, cv_ref, n).astype(BF16))
            if with_query:
                conv_q.append((l2n(conv_chunk(gq_ref, cq_ref, n)) * (HEAD_DIM ** -0.5)).astype(BF16))
                ogdn[cs, :] = jnp.zeros((CHUNK, HEAD_DIM), F32)

        jobs = [(j, d) for j in range(pre_chunks) for d in range(2)]
        css = [chunk_cs[j] for j, _ in jobs]
        slots = [d * n_chunk + chunks[j] for j, d in jobs]
        dirs = [d for _, d in jobs]
        ks = [rk_ref[cs, :] for cs in css]
        vs = [rv_ref[cs, :] for cs in css]
        if with_query:
            scs = [(_dot_nt(rq_ref[cs, :], k) * rtile[4 * d + 0]).astype(BF16) for cs, k, d in zip(css, ks, dirs)]
            outs = [_dot(sc, v) for sc, v in zip(scs, vs)]
            for j, cs in enumerate(chunk_cs):
                oret[cs, :] = outs[2 * j] + outs[2 * j + 1]
        kvs = [_dot_tn((k.astype(F32) * rtile[4 * d + 2]).astype(BF16), v) for k, v, d in zip(ks, vs, dirs)]
        for slot, kv in zip(slots, kvs):
            kvbuf[slot] = kv
        ks = [conv_k[j] for j, _ in jobs]
        vs = [conv_v[j] for j, _ in jobs]
        colss = [cols_ref[cs, :] for cs in css]
        betas = [bcast_col(cols, 2 + d) for cols, d in zip(colss, dirs)]
        incls = [(row >= colm) if d == 0 else (row <= colm) for d in dirs]
        stricts = [(row > colm) if d == 0 else (row < colm) for d in dirs]
        decs = [jnp.exp(jnp.where(incl, bcast_col(cols, d) - jnp.broadcast_to(grow_ref[d:d + 1, cs], (CHUNK, CHUNK)), 0.0))
                for cols, d, cs, incl in zip(colss, dirs, css, incls)]
        kks = [_dot_nt(k, k) for k in ks]
        mats = [kk * beta * jnp.where(strict, dec, 0.0) for kk, beta, strict, dec in zip(kks, betas, stricts, decs)]
        tinvs = [t.astype(BF16) for t in _unit_triangular_inverses(mats, [d == 0 for d in dirs])]
        rhs = [jnp.concatenate([(beta * v.astype(F32)).astype(BF16),
                                (beta * bcast_col(cols, 4 + d) * k.astype(F32)).astype(BF16)], axis=1)
               for beta, v, cols, d, k in zip(betas, vs, colss, dirs, ks)]
        uws = [_dot(t, r) for t, r in zip(tinvs, rhs)]
        us = [uw[:, :HEAD_DIM] for uw in uws]
        ws = [uw[:, HEAD_DIM:].astype(BF16) for uw in uws]
        kts = [(k.astype(F32) * bcast_col(cols, 6 + d)).astype(BF16) for k, cols, d in zip(ks, colss, dirs)]
        ncs = [_dot_tn(kt, uw.astype(BF16)) for kt, uw in zip(kts, uws)]
        for slot, nc_mat in zip(slots, ncs):
            cbuf[slot] = nc_mat[:, :HEAD_DIM]
            nbuf[slot] = nc_mat[:, HEAD_DIM:].astype(BF16)
        if with_query:
            for slot, u, w in zip(slots, us, ws):
                ubuf[slot] = u
                wbuf[slot] = w
            qks = [_dot_nt(conv_q[j], k) for (j, _), k in zip(jobs, ks)]
            for slot, qk, incl, dec in zip(slots, qks, incls, decs):
                pbuf[slot] = (qk * jnp.where(incl, dec, 0.0)).astype(BF16)
            for (j, d), cs, cols in zip(jobs, css, colss):
                qes[d, cs, :] = (conv_q[j].astype(F32) * bcast_col(cols, 4 + d)).astype(BF16)
        return carry

    lax.fori_loop(0, n_chunk // pre_chunks, prepass, 0)

    def finish(cs):
        ro = oret[cs, :]
        rc = ro - jnp.mean(ro, axis=-1, keepdims=True)
        ry = rc * lax.rsqrt(jnp.mean(rc * rc, axis=-1, keepdims=True) + EPS)
        yret_ref[cs, :] = (ry * gnw_ref[...] * _silu(rg_ref[cs, :].astype(F32))).astype(BF16)
        go = ogdn[cs, :]
        gy = go * lax.rsqrt(jnp.mean(go * go, axis=-1, keepdims=True) + EPS)
        ygdn_ref[cs, :] = (gy * rmsw_ref[...] * _silu(gz_ref[cs, :].astype(F32))).astype(BF16)

    def chunk_ids(n):
        nds = [n, n_chunk - 1 - n]
        css = [pl.ds(nd * CHUNK if isinstance(nd, int) else pl.multiple_of(nd * CHUNK, CHUNK), CHUNK) for nd in nds]
        slots = [d * n_chunk + nd for d, nd in enumerate(nds)]
        return nds, css, slots

    def late_outputs(n, vnb, finishing):
        _, css, slots = chunk_ids(n)
        for d in range(2):
            ogdn[css[d], :] += _dot(pbuf[slots[d]], vnb[d])
            if finishing:
                finish(css[d])

    def scan_step(n, vnb_prev, has_prev, finish_prev):
        nds, css, slots = chunk_ids(n)
        ret_st = [state[d] for d in range(2)]
        gdn_st = [state[2 + d] for d in range(2)]
        gdn_stb = [st.astype(BF16) for st in gdn_st]
        shrink = [_dot(nbuf[slot], stb) for slot, stb in zip(slots, gdn_stb)]
        for d in range(2):
            state[2 + d] = gl_ref[bi * n_chunk + nds[d], 8 * hi + d] * gdn_st[d] - shrink[d] + cbuf[slots[d]]
            state[d] = rtile[4 * d + 3] * ret_st[d] + kvbuf[slots[d]]
        if not with_query:
            return vnb_prev
        vnb = tuple((ubuf[slot] - _dot(wbuf[slot], stb)).astype(BF16) for slot, stb in zip(slots, gdn_stb))
        for d in range(2):
            oret[css[d], :] += _dot(rq_ref[css[d], :], ret_st[d].astype(BF16)) * rtile[4 * d + 1]
            ogdn[css[d], :] += _dot(qes[d, css[d], :], gdn_stb[d])
        if has_prev:
            late_outputs(n - 1, vnb_prev, finish_prev)
        return vnb

    if with_query:
        half = n_chunk // 2
        zero = jnp.zeros((CHUNK, HEAD_DIM), BF16)
        vnb = scan_step(0, (zero, zero), has_prev=False, finish_prev=False)
        vnb = lax.fori_loop(1, half + 1, functools.partial(scan_step, has_prev=True, finish_prev=False), vnb)
        vnb = lax.fori_loop(half + 1, n_chunk, functools.partial(scan_step, has_prev=True, finish_prev=True), vnb)
        late_outputs(n_chunk - 1, vnb, True)
    else:
        lax.fori_loop(0, n_chunk, functools.partial(scan_step, has_prev=False, finish_prev=False), 0)
        sfin_ref[...] = state[...]


def _mixers(lg, gl, feats, dec, conv, s0, gnw, rmsw, with_query):
    cols, grow = dec
    b, l, _ = feats["rk"].shape
    n_slot = 2 * (l // CHUNK)
    assert (l // CHUNK) % min(4, l // CHUNK) == 0
    smem = pl.BlockSpec(memory_space=pltpu.SMEM)
    head = pl.BlockSpec((None, l, HEAD_DIM), lambda i, j: (i, 0, j))
    colspec = pl.BlockSpec((None, None, l, 8), lambda i, j: (i, j, 0, 0))
    rowspec = pl.BlockSpec((None, None, 8, l), lambda i, j: (i, j, 0, 0))
    tile_f32 = pltpu.VMEM((n_slot, CHUNK, CHUNK), F32)
    tile_bf16 = pltpu.VMEM((n_slot, CHUNK, CHUNK), BF16)
    st_spec = pl.BlockSpec((None, None, 4, HEAD_DIM, HEAD_DIM), lambda i, j: (i, j, 0, 0, 0))

    def conv_spec(which):
        return pl.BlockSpec((SHORT_CONV, HEAD_DIM), lambda i, j, w=which: (0, w * N_HEADS + j))

    def gain_spec():
        return pl.BlockSpec((1, HEAD_DIM), lambda i, j: (0, j))

    if with_query:
        args = [lg, gl, feats["rk"], feats["rv"], feats["gk"], feats["gv"], feats["rq"], feats["gq"], feats["rg"],
                feats["gz"], cols, grow, conv, conv, conv, s0, gnw, rmsw]
        in_specs = [smem, smem] + [head] * 8 + [colspec, rowspec, conv_spec(0), conv_spec(1), conv_spec(2),
                                                st_spec, gain_spec(), gain_spec()]
        out_shape = [jax.ShapeDtypeStruct((b, l, MIX_W), BF16)] * 2
        out_specs = [head, head]
        scratch = ([pltpu.VMEM((2, l, HEAD_DIM), BF16)] + [pltpu.VMEM((l, HEAD_DIM), F32)] * 2
                   + [tile_f32, tile_bf16, tile_bf16, tile_bf16, tile_f32, tile_f32])
    else:
        args = [lg, gl, feats["rk"], feats["rv"], feats["gk"], feats["gv"], cols, grow, conv, conv, s0]
        in_specs = [smem, smem] + [head] * 4 + [colspec, rowspec, conv_spec(1), conv_spec(2), st_spec]
        out_shape = jax.ShapeDtypeStruct((b, N_HEADS, 4, HEAD_DIM, HEAD_DIM), F32)
        out_specs = st_spec
        scratch = [tile_bf16, tile_f32, tile_f32]
    scratch = scratch + [pltpu.VMEM((8, CHUNK, CHUNK), F32), pltpu.VMEM((4, HEAD_DIM, HEAD_DIM), F32)]
    return pl.pallas_call(
        functools.partial(_mixer_kernel, seq_len=l, with_query=with_query),
        grid=(b, N_HEADS),
        in_specs=in_specs,
        out_specs=out_specs,
        out_shape=out_shape,
        scratch_shapes=scratch,
        compiler_params=_cparams(("arbitrary", "arbitrary")),
        name="mixers_latent" if with_query else "mixers_context",
    )(*args)


def _split_bf16(v):
    hi = v.astype(BF16)
    return hi, (v - hi.astype(F32)).astype(BF16)


def _route(h2, wrt_ref, bias_ref, cand_ref, sel_ref, wd_ref):
    tm = h2.shape[0]
    h_hi, h_lo = _split_bf16(h2)
    w_hi, w_lo = _split_bf16(wrt_ref[...])
    logits = _dot_nt(w_hi, h_hi) + (_dot_nt(w_hi, h_lo) + _dot_nt(w_lo, h_hi))
    scores = _sigmoid(logits)
    biased = scores + bias_ref[...]
    neg_inf = float("-inf")
    sub = _iota((GROUP_SIZE, tm), 0).astype(F32)
    group_score = []
    for g in range(N_GROUPS):
        blk = biased[g * GROUP_SIZE:(g + 1) * GROUP_SIZE, :]
        m1 = jnp.max(blk, axis=0, keepdims=True)
        first = jnp.min(jnp.where(blk == m1, sub, float(GROUP_SIZE)), axis=0, keepdims=True)
        m2 = jnp.max(jnp.where(sub == first, neg_inf, blk), axis=0, keepdims=True)
        group_score.append(m1 + m2)
    for g in range(N_GROUPS):
        ahead = jnp.zeros((1, tm), I32)
        for g2 in range(N_GROUPS):
            if g2 == g:
                continue
            before = (group_score[g2] > group_score[g])
            if g2 < g:
                before = before | (group_score[g2] == group_score[g])
            ahead = ahead + before.astype(I32)
        keep = jnp.broadcast_to(ahead, (GROUP_SIZE, tm)) < TOPK_GROUPS
        cand_ref[g * GROUP_SIZE:(g + 1) * GROUP_SIZE, :] = jnp.where(
            keep, biased[g * GROUP_SIZE:(g + 1) * GROUP_SIZE, :], neg_inf)
    work = cand_ref[...]
    eidx = _iota((N_EXPERTS, tm), 0).astype(F32)
    sel = jnp.zeros((N_EXPERTS, tm), jnp.bool_)
    for _ in range(TOP_K):
        best = jnp.max(work, axis=0, keepdims=True)
        first = jnp.min(jnp.where(work == best, eidx, float(N_EXPERTS)), axis=0, keepdims=True)
        pick = eidx == first
        sel = sel | pick
        work = jnp.where(pick, neg_inf, work)
    picked = jnp.where(sel, scores, 0.0)
    wsum = jnp.sum(picked, axis=0, keepdims=True)
    sel_ref[...] = sel.astype(F32)
    wd_ref[...] = picked / wsum * ROUTED_SCALE


def _mixout_kernel(x_ref, yr_ref, yg_ref, g0_ref, g1_ref, g2_ref, g3_ref, wr_ref, wg_ref, wo_ref,
                   npost_ref, gate1_ref, nffn_ref, sc2_ref, sh2_ref, wrt_ref, bias_ref,
                   x1_ref, h2_ref, sel_ref, wd_ref, cand_ref):
    r = _dot(yr_ref[...], wr_ref[...])
    g = _dot(yg_ref[...], wg_ref[...])
    half = r.shape[1] // 2
    merged = jnp.concatenate(
        [g0_ref[...].astype(F32) * r[:, :half] + g2_ref[...].astype(F32) * g[:, :half],
         g1_ref[...].astype(F32) * r[:, half:] + g3_ref[...].astype(F32) * g[:, half:]], axis=1)
    mo = _dot(merged.astype(BF16), wo_ref[...])
    x1 = x_ref[...] + gate1_ref[...] * _rms(mo, npost_ref[...])
    x1_ref[...] = x1
    h2 = _rms(x1, nffn_ref[...]) * (1.0 + sc2_ref[...]) + sh2_ref[...]
    h2_ref[...] = h2.astype(BF16)
    _route(h2, wrt_ref, bias_ref, cand_ref, sel_ref, wd_ref)


def _mixer_out(x, yr, yg, gates, wr, wg, wo, npost, gate1, nffn, sc2, sh2, wrt, bias):
    b, l, d = x.shape
    tm = MIXOUT_TM
    tiles = l // tm
    tok = lambda w: pl.BlockSpec((None, tm, w), lambda i, j: (i, j, 0))
    const = functools.partial(pl.BlockSpec, pipeline_mode=pl.Buffered(1))
    vec = lambda: const((1, d), lambda i, j: (0, 0))
    bvec = lambda: pl.BlockSpec((None, 1, d), lambda i, j: (i, 0, 0))
    emap = pl.BlockSpec((N_EXPERTS, tm), lambda i, j: (0, i * tiles + j))
    in_specs = ([tok(d), tok(MIX_W), tok(MIX_W)] + [tok(MIX_W)] * 4
                + [const((MIX_W, d), lambda i, j: (0, 0)), const((MIX_W, d), lambda i, j: (0, 0)),
                   const((d, d), lambda i, j: (0, 0)),
                   vec(), bvec(), vec(), bvec(), bvec(),
                   const((N_EXPERTS, d), lambda i, j: (0, 0)), const((N_EXPERTS, 1), lambda i, j: (0, 0))])
    return pl.pallas_call(
        _mixout_kernel,
        grid=(b, tiles),
        in_specs=in_specs,
        out_specs=[tok(d), tok(d), emap, emap],
        out_shape=[jax.ShapeDtypeStruct((b, l, d), F32), jax.ShapeDtypeStruct((b, l, d), BF16),
                   jax.ShapeDtypeStruct((N_EXPERTS, b * l), F32), jax.ShapeDtypeStruct((N_EXPERTS, b * l), F32)],
        scratch_shapes=[pltpu.VMEM((N_EXPERTS, tm), F32)],
        compiler_params=_cparams(("arbitrary", "arbitrary")),
        name="mixer_out_router",
    )(x, yr, yg, *gates, wr, wg, wo, npost, gate1, nffn, sc2, sh2, wrt, bias)


def _tile_positions(sel):
    tm = sel.shape[1]
    selb = sel.astype(BF16)
    earlier = (_iota((tm, tm), 0) < _iota((tm, tm), 1)).astype(BF16)
    rank = _dot(selb, earlier)
    cnt = _dot(selb, jnp.ones((tm, tm), BF16))
    nmb = jnp.floor((cnt + (MB_ROWS - 1)) * (1.0 / MB_ROWS))
    below = (_iota((N_EXPERTS, N_EXPERTS), 1) < _iota((N_EXPERTS, N_EXPERTS), 0)).astype(BF16)
    offmb = _dot(below, nmb.astype(BF16))
    rank_hi = jnp.floor(rank * (1.0 / MB_ROWS))
    hi = jnp.where(sel > 0.0, offmb + rank_hi, 255.0)
    lo = rank - rank_hi * MB_ROWS
    return hi, lo, cnt, offmb, rank


def _dispatch_kernel(h2_ref, sel_ref, xs_ref, cnt_ref):
    last = pl.num_programs(0) - 1

    @pl.when(pl.program_id(0) < last)
    def _():
        _dispatch_tile(h2_ref, sel_ref, xs_ref, cnt_ref)

    @pl.when(pl.program_id(0) == last)
    def _():
        xs_ref[...] = jnp.zeros_like(xs_ref)


def _dispatch_tile(h2_ref, sel_ref, xs_ref, cnt_ref):
    tm = h2_ref.shape[0]
    sel = sel_ref[...]
    hi, lo, _, _, _ = _tile_positions(sel)
    hib = hi.astype(BF16)
    lob = lo.astype(BF16)
    selb = sel.astype(BF16)
    cnt_row = _dot_nt(jnp.ones((8, tm), BF16), selb)
    nmb_row = jnp.floor((cnt_row + (MB_ROWS - 1)) * (1.0 / MB_ROWS))
    before = (_iota((N_EXPERTS, N_EXPERTS), 0) < _iota((N_EXPERTS, N_EXPERTS), 1)).astype(BF16)
    start_row = _dot(nmb_row.astype(BF16), before) * MB_ROWS
    cnt_ref[...] = cnt_row
    start = start_row[0:1, :]
    end = start + cnt_row[0:1, :]
    x = h2_ref[...]
    rch = 256
    for rc in range(TILE_ROWS // rch):
        r_e = (rc * rch + _iota((rch, N_EXPERTS), 0)).astype(F32)
        owner = ((r_e >= start) & (r_e < end)).astype(BF16)
        r_t = rc * rch + _iota((rch, tm), 0)
        match = ((_dot(owner, hib) == (r_t >> _MB_SHIFT).astype(F32))
                 & (_dot(owner, lob) == (r_t & (MB_ROWS - 1)).astype(F32)))
        xs_ref[rc * rch:(rc + 1) * rch, :] = _dot(match.astype(BF16), x).astype(BF16)


def _dispatch(h2, sel_t):
    t, d = h2.shape
    tm = MOE_TM
    nt = t // tm
    return pl.pallas_call(
        _dispatch_kernel,
        grid=(nt + 1,),
        in_specs=[pl.BlockSpec((tm, d), lambda i: (jnp.minimum(i, nt - 1), 0)),
                  pl.BlockSpec((N_EXPERTS, tm), lambda i: (0, jnp.minimum(i, nt - 1)))],
        out_specs=[pl.BlockSpec((TILE_ROWS, d), lambda i: (i, 0)),
                   pl.BlockSpec((None, 8, N_EXPERTS), lambda i: (jnp.minimum(i, nt - 1), 0, 0))],
        out_shape=[jax.ShapeDtypeStruct((nt * TILE_ROWS + 3 * EXP_BM, d), BF16),
                   jax.ShapeDtypeStruct((nt, 8, N_EXPERTS), F32)],
        compiler_params=_cparams(("arbitrary",)),
        name="moe_dispatch",
    )(h2, sel_t)


def _expert_plan(cnt, n_blk):
    nt = cnt.shape[0]
    nmb = (cnt + (MB_ROWS - 1)) // MB_ROWS
    offmb = jnp.cumsum(nmb, axis=1) - nmb
    per_e = nmb.T
    incl = jnp.cumsum(per_e, axis=1)
    excl = incl - per_e
    tot = incl[:, -1]
    nb = (tot + (EXP_MB - 1)) // EXP_MB
    bend = jnp.cumsum(nb)
    bstart = bend - nb
    b = jnp.arange(n_blk, dtype=I32)
    valid = b < bend[-1]
    last = jnp.maximum(bend[-1] - 1, 0)
    bq = jnp.where(valid, b, last)
    blk_e = jnp.minimum(jnp.sum((bend[None, :] <= bq[:, None]).astype(I32), axis=1), N_EXPERTS - 1)
    onehot_e = (blk_e[:, None] == jnp.arange(N_EXPERTS, dtype=I32)[None, :]).astype(I32)
    pick = lambda table: jnp.sum(onehot_e[:, :, None] * table[None, :, :], axis=1)
    bstart_b = jnp.sum(onehot_e * bstart[None, :], axis=1)
    tot_b = jnp.sum(onehot_e * tot[None, :], axis=1)
    p0 = (bq - bstart_b) * EXP_MB
    blk_nmb = jnp.where(valid, jnp.clip(tot_b - p0, 0, EXP_MB), 0).astype(I32)
    p = p0[:, None] + jnp.arange(EXP_MB, dtype=I32)[None, :]
    incl_b = pick(incl)
    tile = jnp.minimum(jnp.sum((incl_b[:, None, :] <= p[:, :, None]).astype(I32), axis=2), nt - 1)
    onehot_t = (tile[:, :, None] == jnp.arange(nt, dtype=I32)[None, None, :]).astype(I32)
    shift_b = pick(offmb.T - excl)[:, None, :]
    where = tile * _TILE_MB + p + jnp.sum(onehot_t * shift_b, axis=2)
    j = jnp.arange(EXP_MB, dtype=I32)[None, :]
    used = j < blk_nmb[:, None]
    spare = nt * _TILE_MB + (b[:, None] % 2) * EXP_MB + j
    zeros_mb = nt * _TILE_MB + 2 * EXP_MB + j
    src = jnp.where(used, where, jnp.where(valid[:, None], where[:, :1], zeros_mb))
    dst = jnp.where(used, where, spare)
    return bstart.astype(I32), nb.astype(I32), src.reshape(-1).astype(I32), dst.reshape(-1).astype(I32)


def _expert_kernel(bstart_ref, nb_ref, src_ref, dst_ref, xs_hbm, wg_ref, wu_ref, wd_ref, ys_hbm,
                   xbuf, ybuf, wgb, wub, wdb, in_sem, out_sem):
    e = pl.program_id(0)
    n_in = xbuf.shape[0]

    def gather_copy(blk, slot, j):
        rows = pl.ds(pl.multiple_of(src_ref[blk * EXP_MB + j] * MB_ROWS, MB_ROWS), MB_ROWS)
        return pltpu.make_async_copy(xs_hbm.at[rows, :], xbuf.at[slot, j * MB_ROWS:(j + 1) * MB_ROWS, :],
                                     in_sem.at[slot])

    def scatter_copy(blk, slot, j):
        rows = pl.ds(pl.multiple_of(dst_ref[blk * EXP_MB + j] * MB_ROWS, MB_ROWS), MB_ROWS)
        return pltpu.make_async_copy(ybuf.at[slot, j * MB_ROWS:(j + 1) * MB_ROWS, :], ys_hbm.at[rows, :],
                                     out_sem.at[slot])

    def start_gather(blk):
        for j in range(EXP_MB):
            gather_copy(blk, lax.rem(blk, n_in), j).start()

    def wait_gather(blk):
        for j in range(EXP_MB):
            gather_copy(blk, lax.rem(blk, n_in), j).wait()

    def start_scatter(blk):
        for j in range(EXP_MB):
            scatter_copy(blk, blk % 2, j).start()

    def wait_scatter(slot):
        for j in range(EXP_MB):
            scatter_copy(0, slot, j).wait()

    @pl.when(e == 0)
    def _():
        ybuf[...] = jnp.zeros_like(ybuf)
        for ahead in range(EXP_LEAD):
            start_gather(ahead)
        spare_row0 = ys_hbm.shape[0] - 3 * EXP_BM
        for slot in range(2):
            for j in range(EXP_MB):
                spare = pl.ds(spare_row0 + (slot * EXP_MB + j) * MB_ROWS, MB_ROWS)
                pltpu.make_async_copy(ybuf.at[slot, j * MB_ROWS:(j + 1) * MB_ROWS, :], ys_hbm.at[spare, :],
                                      out_sem.at[slot]).start()

    wgb[...] = wg_ref[...].astype(BF16)
    wub[...] = wu_ref[...].astype(BF16)
    wdb[...] = wd_ref[...].astype(BF16)

    def block(b, carry):
        slot = b % 2
        wait_gather(b)
        wait_scatter(slot)
        x = xbuf[lax.rem(b, n_in)]
        act = (_silu(_dot(x, wgb[...])) * _dot(x, wub[...])).astype(BF16)
        start_gather(b + EXP_LEAD)
        ybuf[slot] = _dot(act, wdb[...]).astype(BF16)
        start_scatter(b)
        return carry

    first = bstart_ref[e]
    lax.fori_loop(first, first + nb_ref[e], block, 0)

    @pl.when(e == pl.num_programs(0) - 1)
    def _():
        total = first + nb_ref[e]
        for ahead in range(EXP_LEAD):
            wait_gather(total + ahead)
        wait_scatter(0)
        wait_scatter(1)


def _expert_ffn(xs, bstart, nb, src, dst, wg, wu, wd):
    rows, d = xs.shape
    de = wg.shape[2]
    grid_spec = pltpu.PrefetchScalarGridSpec(
        num_scalar_prefetch=4,
        grid=(wg.shape[0],),
        in_specs=[pl.BlockSpec(memory_space=pl.ANY),
                  pl.BlockSpec((None, d, de), lambda i, bs, bn, sr, ds: (i, 0, 0)),
                  pl.BlockSpec((None, d, de), lambda i, bs, bn, sr, ds: (i, 0, 0)),
                  pl.BlockSpec((None, de, d), lambda i, bs, bn, sr, ds: (i, 0, 0))],
        out_specs=pl.BlockSpec(memory_space=pl.ANY),
        scratch_shapes=[pltpu.VMEM((EXP_LEAD + 1, EXP_BM, d), BF16), pltpu.VMEM((2, EXP_BM, d), BF16),
                        pltpu.VMEM((d, de), BF16), pltpu.VMEM((d, de), BF16), pltpu.VMEM((de, d), BF16),
                        pltpu.SemaphoreType.DMA((EXP_LEAD + 1,)), pltpu.SemaphoreType.DMA((2,))],
    )
    return pl.pallas_call(
        _expert_kernel,
        grid_spec=grid_spec,
        out_shape=jax.ShapeDtypeStruct((rows, d), BF16),
        input_output_aliases={4: 0},
        compiler_params=_cparams(("arbitrary",)),
        name="moe_experts",
    )(bstart, nb, src, dst, xs, wg, wu, wd)


def _combine_kernel(ys_ref, sel_ref, wd_ref, h2_ref, x1_ref, wsg_ref, wsu_ref, wsd_ref, npost_ref, gate2_ref,
                    o_ref, c_ref):
    tm = h2_ref.shape[0]
    sel = sel_ref[...]
    _, _, cnt, offmb, rank = _tile_positions(sel)
    no_rank = 1.5 * tm
    lhs = jnp.concatenate([jnp.where(sel > 0.0, rank, no_rank).T, wd_ref[...].T], axis=1).astype(BF16)
    start = offmb[:, :LANES] * MB_ROWS
    end = start + cnt[:, :LANES]
    zeros = jnp.zeros((N_EXPERTS, LANES), BF16)
    for cc in range(TILE_ROWS // LANES):
        r_e = (cc * LANES + _iota((N_EXPERTS, LANES), 1)).astype(F32)
        owned = (r_e >= start) & (r_e < end)
        owner = owned.astype(BF16)
        local = r_e[0:1, :] - jnp.sum(jnp.where(owned, start, 0.0), axis=0, keepdims=True)
        rhs = jnp.concatenate([jnp.concatenate([owner, zeros], axis=1),
                               jnp.concatenate([zeros, owner], axis=1)], axis=0)
        got = _dot(lhs, rhs)
        c_ref[:, cc * LANES:(cc + 1) * LANES] = jnp.where(got[:, :LANES] == local, got[:, LANES:], 0.0).astype(BF16)
    routed = _dot(c_ref[...], ys_ref[...])
    h2 = h2_ref[...]
    shared = _dot((_silu(_dot(h2, wsg_ref[...])) * _dot(h2, wsu_ref[...])).astype(BF16), wsd_ref[...])
    o_ref[...] = x1_ref[...] + gate2_ref[...] * _rms(routed + shared, npost_ref[...])


def _combine(ys, sel_t, wd_t, h2, x1, wsg, wsu, wsd, npost, gate2, seq_len):
    t, d = h2.shape
    tm = MOE_TM
    nt = t // tm
    per_seq = seq_len // tm
    ds = wsg.shape[1]
    const = functools.partial(pl.BlockSpec, pipeline_mode=pl.Buffered(1))
    emap = pl.BlockSpec((N_EXPERTS, tm), lambda i: (0, i))
    return pl.pallas_call(
        _combine_kernel,
        grid=(nt,),
        in_specs=[pl.BlockSpec((TILE_ROWS, d), lambda i: (i, 0)), emap, emap,
                  pl.BlockSpec((tm, d), lambda i: (i, 0)), pl.BlockSpec((tm, d), lambda i: (i, 0)),
                  const((d, ds), lambda i: (0, 0)), const((d, ds), lambda i: (0, 0)), const((ds, d), lambda i: (0, 0)),
                  const((1, d), lambda i: (0, 0)),
                  pl.BlockSpec((None, 1, d), lambda i: (i // per_seq, 0, 0))],
        out_specs=pl.BlockSpec((tm, d), lambda i: (i, 0)),
        out_shape=jax.ShapeDtypeStruct((t, d), F32),
        scratch_shapes=[pltpu.VMEM((tm, TILE_ROWS), BF16)],
        compiler_params=_cparams(("arbitrary",)),
        name="moe_combine",
    )(ys, sel_t, wd_t, h2, x1, wsg, wsu, wsd, npost, gate2)


def _rope_tables(n):
    rows = n // GRID_W
    pos_r = jnp.repeat(jnp.arange(rows, dtype=F32), GRID_W)
    pos_c = jnp.tile(jnp.arange(GRID_W, dtype=F32), rows)
    n_freq = HEAD_DIM // 4
    inv = ROPE_BASE ** (-jnp.arange(n_freq, dtype=F32) / n_freq)
    ang = jnp.concatenate([pos_r[:, None] * inv, pos_c[:, None] * inv], axis=-1)
    cos, sin = jnp.cos(ang), jnp.sin(ang)
    return jnp.concatenate([cos, cos], axis=-1), jnp.concatenate([-sin, sin], axis=-1)


def kernel(x, c, ctx, c_ctx, w_mod, b_mod, norm_mix_pre, norm_mix_post, norm_ffn_pre, norm_ffn_post, w_in, gdn_conv, ret_log_decay, gdn_a_log, gdn_dt_bias, ret_gn_w, gdn_norm_w, w_ret_out, w_gdn_out, w_o, w_router, router_bias, w_gate, w_up, w_down, w_sh_gate, w_sh_up, w_sh_down):
    b, n, d = x.shape
    depth = w_mod.shape[0]
    assert depth == 1, "single-layer block"
    assert n % max(PROJ_TM, MOE_TM) == 0 and ctx.shape[1] % CHUNK == 0
    assert _TILE_MB < 255

    rows = -(-(b + 1) // 8) * 8
    cvec = jnp.zeros((rows, d), F32).at[:b].set(c).at[b].set(c_ctx)
    mod = _modulation(cvec, w_mod[0], b_mod[0][None, :])
    sh1, sc1, g1, sh2, sc2, g2 = [mod[:b, k * d:(k + 1) * d][:, None, :] for k in range(6)]
    ctx_shift = jnp.broadcast_to(mod[b, 0:d][None, None, :], (b, 1, d))
    ctx_scale = jnp.broadcast_to(mod[b, d:2 * d][None, None, :], (b, 1, d))

    w_in0 = w_in[0]
    n_main = 4 * MIX_W
    w_state = w_in0[:, :n_main].astype(BF16)
    slots = ((0, 0), (0, 1), (1, 0), (1, 1), (0, 0), (0, 1), (0, 0), (0, 1))
    gab_cols = n_main + jnp.array([[ab * 2 * N_HEADS + dr * N_HEADS + hh for ab, dr in slots]
                                   for hh in range(N_HEADS)], I32)
    w_gab = w_in0[:, gab_cols.reshape(-1)].astype(BF16)
    n_state = n_main + N_GAB
    w_query = w_in0[:, n_state:].astype(BF16)
    w_all = jnp.concatenate([w_state, w_query], axis=1)
    a_coef = -jnp.exp(gdn_a_log[0].astype(F32))
    dtb = gdn_dt_bias[0].astype(F32)
    is_alpha = jnp.array([ab == 0 for ab, _ in slots])
    dirs = jnp.array([dr for _, dr in slots], I32)
    prow = jnp.stack([jnp.where(is_alpha[None, :], a_coef.T[:, dirs], 0.0).reshape(-1),
                      jnp.where(is_alpha[None, :], dtb.T[:, dirs], 0.0).reshape(-1)], axis=0)
    first2 = jnp.tile(jnp.arange(8) < 2, N_HEADS)
    w_gab_t = w_gab.T * first2[:, None].astype(BF16)
    pcol = prow.T * first2[:, None]
    gain_mix = norm_mix_pre[0][None, :]
    cos2, sin2 = _rope_tables(n)

    state_kinds = (_ROPE_SCALED, _PLAIN, _PLAIN, _PLAIN)
    query_kinds = (_ROPE, _PLAIN, _PLAIN, _PLAIN, _SIGMOID, _SIGMOID, _SIGMOID, _SIGMOID)
    lg = ret_log_decay[0].astype(F32)
    conv = gdn_conv[0].astype(F32)

    lc = ctx.shape[1]
    cfe, (ccols, cgrow, cgl) = _in_projection(
        ctx, gain_mix, ctx_scale, ctx_shift, w_state, w_gab, w_gab_t, prow, pcol,
        cos2[:lc], sin2[:lc], state_kinds, rope=False)
    cfeats = dict(zip(("rk", "rv", "gk", "gv"), cfe))
    zero_state = jnp.zeros((b, N_HEADS, 4, HEAD_DIM, HEAD_DIM), F32)
    init = _mixers(lg, cgl, cfeats, (ccols, cgrow), conv, zero_state, None, None, with_query=False)

    fe, (cols, grow, gl) = _in_projection(
        x, gain_mix, sc1, sh1, w_all, w_gab, w_gab_t, prow, pcol, cos2, sin2,
        state_kinds + query_kinds, rope=True)
    feats = dict(zip(("rk", "rv", "gk", "gv", "rq", "rg", "gq", "gz"), fe[:8]))
    gates = fe[8:]
    y_ret, y_gdn = _mixers(lg, gl, feats, (cols, grow), conv, init,
                           ret_gn_w[0][None, :], gdn_norm_w[0][None, :], with_query=True)
    x1, h2, sel_t, wd_t = _mixer_out(
        x, y_ret, y_gdn, gates, w_ret_out[0].astype(BF16), w_gdn_out[0].astype(BF16), w_o[0].astype(BF16),
        norm_mix_post[0][None, :], g1, norm_ffn_pre[0][None, :], sc2, sh2,
        w_router[0].T.astype(F32), router_bias[0].astype(F32)[:, None])

    t = b * n
    h2f = h2.reshape(t, d)
    xs, cnt = _dispatch(h2f, sel_t)
    nt = t // MOE_TM
    n_blk = nt * _TILE_MB // EXP_MB + N_EXPERTS + EXP_LEAD
    bstart, nb, src, dst = _expert_plan(cnt[:, 0, :].astype(I32), n_blk)
    ys = _expert_ffn(xs, bstart, nb, src, dst, w_gate[0], w_up[0], w_down[0])
    out = _combine(ys, sel_t, wd_t, h2f, x1.reshape(t, d), w_sh_gate[0].astype(BF16), w_sh_up[0].astype(BF16),
                   w_sh_down[0].astype(BF16), norm_ffn_post[0][None, :], g2, n)
    return out.reshape(b, n, d)
```

```python
import functools
import math

import jax
import jax.numpy as jnp
from jax import lax
from jax.experimental import pallas as pl
from jax.experimental.pallas import tpu as pltpu

F32 = jnp.float32
BF16 = jnp.bfloat16
I32 = jnp.int32
HIGHEST = lax.Precision.HIGHEST

N_HEADS = 4
HEAD_DIM = 128
MIX_W = N_HEADS * HEAD_DIM
CHUNK = 128
SHORT_CONV = 3
ROPE_BASE = 10000.0
GRID_W = 64
N_EXPERTS = 64
TOP_K = 8
N_GROUPS = 8
TOPK_GROUPS = 4
GROUP_SIZE = N_EXPERTS // N_GROUPS
ROUTED_SCALE = 2.5
EPS = 1e-6
N_GAB = 4 * N_HEADS

LANES = 128
BF16_TILE_ROWS = 16
VMEM_LIMIT_BYTES = 56 * 1024 * 1024

PROJ_TM = 512
MIXOUT_TM = 512
MOE_TM = 256
MB_ROWS = BF16_TILE_ROWS
_MB_SHIFT = MB_ROWS.bit_length() - 1
_TILE_MB = -(-(TOP_K * MOE_TM // MB_ROWS + N_EXPERTS * (MB_ROWS - 1) // MB_ROWS + 1) // 8) * 8
TILE_ROWS = _TILE_MB * MB_ROWS
EXP_BM = 512
EXP_MB = EXP_BM // MB_ROWS
EXP_LEAD = 3


def _cparams(sem):
    return pltpu.CompilerParams(dimension_semantics=sem, vmem_limit_bytes=VMEM_LIMIT_BYTES)


def _sigmoid(v):
    return 0.5 * jnp.tanh(0.5 * v) + 0.5


def _silu(v):
    return v * _sigmoid(v)


def _softplus(v):
    return jnp.maximum(v, 0.0) + jnp.log1p(jnp.exp(-jnp.abs(v)))


def _iota(shape, dim):
    return lax.broadcasted_iota(I32, shape, dim)


def _dot(a, b, **kw):
    return jnp.dot(a, b, preferred_element_type=F32, **kw)


def _dot_nt(a, b, **kw):
    return lax.dot_general(a, b, (((1,), (1,)), ((), ())), preferred_element_type=F32, **kw)


def _dot_tn(a, b, **kw):
    return lax.dot_general(a, b, (((0,), (0,)), ((), ())), preferred_element_type=F32, **kw)


def _rms(v, gain):
    return v * lax.rsqrt(jnp.mean(v * v, axis=-1, keepdims=True) + EPS) * gain


def _mod_kernel(c_ref, w_ref, b_ref, o_ref):
    o_ref[...] = _dot(_silu(c_ref[...]), w_ref[...], precision=HIGHEST) + b_ref[...]


def _modulation(cvec, w_mod, b_mod):
    rows, d = cvec.shape
    n = w_mod.shape[1]
    tn = 1024
    return pl.pallas_call(
        _mod_kernel,
        grid=(n // tn,),
        in_specs=[pl.BlockSpec((rows, d), lambda j: (0, 0)),
                  pl.BlockSpec((d, tn), lambda j: (0, j)),
                  pl.BlockSpec((1, tn), lambda j: (0, j))],
        out_specs=pl.BlockSpec((rows, tn), lambda j: (0, j)),
        out_shape=jax.ShapeDtypeStruct((rows, n), F32),
        compiler_params=_cparams(("arbitrary",)),
        name="adaln_modulation",
    )(cvec, w_mod, b_mod)


_PLAIN, _ROPE, _ROPE_SCALED, _SIGMOID = 0, 1, 2, 3


def _proj_kernel(x_ref, gain_ref, sc_ref, sh_ref, w_ref, wg_ref, wgt_ref, prow_ref, pcol_ref, cos_ref, sin_ref,
                 *out_refs, kinds, rope):
    n_feat = len(kinds)
    feat_refs = out_refs[:n_feat]
    cols_ref, grow_ref, gl_ref = out_refs[n_feat:]
    tm = x_ref.shape[0]

    x = x_ref[...]
    h = (_rms(x, gain_ref[...]) * (1.0 + sc_ref[...]) + sh_ref[...]).astype(BF16)

    if rope:
        cos2 = cos_ref[...]
        sin2 = sin_ref[...]

    for g, kind in enumerate(kinds):
        p = _dot(h, w_ref[:, g * MIX_W:(g + 1) * MIX_W])
        if kind == _SIGMOID:
            p = _sigmoid(p)
        elif kind in (_ROPE, _ROPE_SCALED) and rope:
            heads = []
            for hh in range(N_HEADS):
                t = p[:, hh * HEAD_DIM:(hh + 1) * HEAD_DIM]
                heads.append(t * cos2 + pltpu.roll(t, HEAD_DIM // 2, 1) * sin2)
            p = jnp.concatenate(heads, axis=1)
        if kind == _ROPE_SCALED:
            p = p * (HEAD_DIM ** -0.5)
        feat_refs[g][...] = p.astype(feat_refs[g].dtype)

    r_i = _iota((CHUNK, CHUNK), 0)
    c_i = _iota((CHUNK, CHUNK), 1)
    lower_incl = (c_i <= r_i).astype(BF16)
    upper_incl = (c_i >= r_i).astype(BF16)

    def prefix_rows(v):
        hi, lo = _split_bf16(v)
        return _dot(lower_incl, hi) + _dot(lower_incl, lo)

    def prefix_lanes(v):
        hi, lo = _split_bf16(v)
        return _dot(hi, upper_incl) + _dot(lo, upper_incl)

    nc = N_HEADS * 8
    pg = _dot(h, wg_ref[...])
    colt = _iota((tm, nc), 1) & 7
    la = jnp.where((colt == 2) | (colt == 3), 0.0, prow_ref[0:1, :] * _softplus(pg + prow_ref[1:2, :]))
    beta = _sigmoid(pg)
    colc = _iota((CHUNK, nc), 1) & 7
    fwd_col = (colc & 1) == 0
    for c in range(tm // CHUNK):
        sl = slice(c * CHUNK, (c + 1) * CHUNK)
        la_c = la[sl]
        pre = prefix_rows(la_c)
        suf = pre[CHUNK - 1:CHUNK, :] - pre + la_c
        g_c = jnp.where(fwd_col, pre, suf)
        rest = jnp.where(fwd_col, suf, pre) - la_c
        vals = jnp.where(colc < 2, g_c, jnp.where(colc < 4, beta[sl], jnp.where(
            colc < 6, jnp.exp(g_c), jnp.exp(rest))))
        for hh in range(N_HEADS):
            cols_ref[hh, sl, :] = vals[:, 8 * hh:8 * hh + 8]
        gl_ref[c:c + 1, :] = jnp.exp(g_c[0:1, :] + rest[0:1, :])

    pgt = _dot_nt(wgt_ref[...], h)
    rowq = _iota((N_HEADS * 8, tm), 0) & 7
    lat = jnp.where(rowq < 2, pcol_ref[:, 0:1] * _softplus(pgt + pcol_ref[:, 1:2]), 0.0)
    rowc = _iota((N_HEADS * 8, CHUNK), 0) & 7
    for c in range(tm // CHUNK):
        sl = slice(c * CHUNK, (c + 1) * CHUNK)
        lat_c = lat[:, sl]
        pre_t = prefix_lanes(lat_c)
        suf_t = pre_t[:, CHUNK - 1:CHUNK] - pre_t + lat_c
        grow_ref[:, :, sl] = jnp.where(rowc == 0, pre_t, suf_t).reshape(N_HEADS, 8, CHUNK)


def _in_projection(x, gain, scale, shift, w_main, w_gab, w_gab_t, prow, pcol, cos2, sin2, kinds, rope):
    b, l, d = x.shape
    tm = min(PROJ_TM, l)
    tiles = l // tm
    n_chunk = tm // CHUNK
    feat_shapes = [jax.ShapeDtypeStruct((b, l, MIX_W), BF16) for _ in kinds]
    feat_specs = [pl.BlockSpec((None, tm, MIX_W), lambda i, j: (i, j, 0)) for _ in kinds]
    out_shape = feat_shapes + [jax.ShapeDtypeStruct((b, N_HEADS, l, 8), F32),
                               jax.ShapeDtypeStruct((b, N_HEADS, 8, l), F32),
                               jax.ShapeDtypeStruct((b, tiles, n_chunk, N_HEADS * 8), F32)]
    out_specs = feat_specs + [pl.BlockSpec((None, N_HEADS, tm, 8), lambda i, j: (i, 0, j, 0)),
                              pl.BlockSpec((None, N_HEADS, 8, tm), lambda i, j: (i, 0, 0, j)),
                              pl.BlockSpec((None, None, n_chunk, N_HEADS * 8), lambda i, j: (i, j, 0, 0))]
    const = functools.partial(pl.BlockSpec, pipeline_mode=pl.Buffered(1))
    ncol = w_main.shape[1]
    in_specs = [
        pl.BlockSpec((None, tm, d), lambda i, j: (i, j, 0)),
        const((1, d), lambda i, j: (0, 0)),
        pl.BlockSpec((None, 1, d), lambda i, j: (i, 0, 0)),
        pl.BlockSpec((None, 1, d), lambda i, j: (i, 0, 0)),
        const((d, ncol), lambda i, j: (0, 0)),
        const((d, N_HEADS * 8), lambda i, j: (0, 0)),
        const((N_HEADS * 8, d), lambda i, j: (0, 0)),
        const((2, N_HEADS * 8), lambda i, j: (0, 0)),
        const((N_HEADS * 8, 2), lambda i, j: (0, 0)),
        pl.BlockSpec((tm, HEAD_DIM), lambda i, j: (j, 0)),
        pl.BlockSpec((tm, HEAD_DIM), lambda i, j: (j, 0)),
    ]
    outs = pl.pallas_call(
        functools.partial(_proj_kernel, kinds=tuple(kinds), rope=rope),
        grid=(b, tiles),
        in_specs=in_specs,
        out_specs=out_specs,
        out_shape=out_shape,
        compiler_params=_cparams(("arbitrary", "arbitrary")),
        name="in_projection_rope" if rope else "in_projection_ctx",
    )(x, gain, scale, shift, w_main, w_gab, w_gab_t, prow, pcol, cos2, sin2)
    feats = outs[:len(kinds)]
    cols, grow, gl = outs[len(kinds):]
    return feats, (cols, grow, gl.reshape(b * (l // CHUNK), N_HEADS * 8))


def _unit_triangular_inverses(mats, lowers):
    r = _iota((CHUNK, CHUNK), 0)
    c = _iota((CHUNK, CHUNK), 1)
    eye = (r == c).astype(F32)
    invs = [eye - jnp.where((r >> 1) == (c >> 1), a, 0.0) for a in mats]
    for level in range(1, int(math.log2(CHUNK))):
        s = 1 << level
        mask = ((r >> (level + 1)) == (c >> (level + 1))) & ((r >> level) != (c >> level))
        invb = [inv.astype(BF16) for inv in invs]
        offs = [jnp.where(mask, a, 0.0).astype(BF16) for a in mats]
        if s < 8:
            half = [_dot(off, ib).astype(BF16) for off, ib in zip(offs, invb)]
            invs = [inv - _dot(ib, hf) for inv, ib, hf in zip(invs, invb, half)]
            continue
        def rows_of(x, lower, moving):
            first = s if (lower == moving) else 0
            return [x[g * 2 * s + first:g * 2 * s + first + s] for g in range(CHUNK // (2 * s))]

        half = [_dot(jnp.concatenate(rows_of(off, lo, True), axis=0), ib).astype(BF16)
                for off, ib, lo in zip(offs, invb, lowers)]
        zero = jnp.zeros((s, CHUNK), BF16)
        full = []
        for hf, lo in zip(half, lowers):
            pieces = []
            for g in range(CHUNK // (2 * s)):
                piece = hf[g * s:(g + 1) * s]
                pieces += [zero, piece] if lo else [piece, zero]
            full.append(jnp.concatenate(pieces, axis=0))
        corr = [_dot(jnp.concatenate(rows_of(ib, lo, True), axis=0), hf) for ib, hf, lo in zip(invb, full, lowers)]
        new = []
        for inv, cr, lo in zip(invs, corr, lowers):
            keep = rows_of(inv, lo, False)
            moved = [m - cr[g * s:(g + 1) * s] for g, m in enumerate(rows_of(inv, lo, True))]
            pieces = []
            for k, m in zip(keep, moved):
                pieces += [k, m] if lo else [m, k]
            new.append(jnp.concatenate(pieces, axis=0))
        invs = new
    return invs


def _mixer_kernel(*refs, seq_len, with_query):
    n_chunk = seq_len // CHUNK
    if with_query:
        (lg_ref, gl_ref, rk_ref, rv_ref, gk_ref, gv_ref, rq_ref, gq_ref, rg_ref, gz_ref,
         cols_ref, grow_ref, cq_ref, ck_ref, cv_ref, s0_ref, gnw_ref, rmsw_ref,
         yret_ref, ygdn_ref,
         qes, oret, ogdn, ubuf, wbuf, pbuf, nbuf, cbuf, kvbuf, rtile, state) = refs
    else:
        (lg_ref, gl_ref, rk_ref, rv_ref, gk_ref, gv_ref,
         cols_ref, grow_ref, ck_ref, cv_ref, s0_ref,
         sfin_ref,
         nbuf, cbuf, kvbuf, rtile, state) = refs
    bi = pl.program_id(0)
    hi = pl.program_id(1)

    row = _iota((CHUNK, CHUNK), 0)
    colm = _iota((CHUNK, CHUNK), 1)
    rowf = row.astype(F32)
    colf = colm.astype(F32)

    def conv_chunk(src_ref, w_ref, n):
        s = pl.multiple_of(n * CHUNK, CHUNK)
        x = src_ref[pl.ds(s, CHUNK), :].astype(F32)
        ps = pl.multiple_of(jnp.maximum(s - BF16_TILE_ROWS, 0), BF16_TILE_ROWS)
        ns = pl.multiple_of(jnp.minimum(s + CHUNK, seq_len - BF16_TILE_ROWS), BF16_TILE_ROWS)
        prev_row = src_ref[pl.ds(ps, BF16_TILE_ROWS), :].astype(F32)[BF16_TILE_ROWS - 1:BF16_TILE_ROWS, :]
        next_row = src_ref[pl.ds(ns, BF16_TILE_ROWS), :].astype(F32)[0:1, :]
        prev_row = prev_row * jnp.where(n > 0, 1.0, 0.0)
        next_row = next_row * jnp.where(n < n_chunk - 1, 1.0, 0.0)
        xp = jnp.where(row == 0, jnp.broadcast_to(prev_row, (CHUNK, HEAD_DIM)), pltpu.roll(x, 1, 0))
        xn = jnp.where(row == CHUNK - 1, jnp.broadcast_to(next_row, (CHUNK, HEAD_DIM)), pltpu.roll(x, CHUNK - 1, 0))
        return _silu(w_ref[0:1, :] * xp + w_ref[1:2, :] * x + w_ref[2:3, :] * xn)

    def l2n(v):
        return v * lax.rsqrt(jnp.sum(v * v, axis=-1, keepdims=True) + EPS)

    for d in range(2):
        lg = lg_ref[d, hi]
        if d == 0:
            dist, pos_q, pos_k = rowf - colf, rowf + 1.0, (CHUNK - 1.0) - rowf
        else:
            dist, pos_q, pos_k = colf - rowf, CHUNK - rowf, rowf
        rtile[4 * d + 0] = jnp.where(dist >= 0, jnp.exp(lg * jnp.maximum(dist, 0.0)), 0.0)
        rtile[4 * d + 1] = jnp.exp(lg * pos_q)
        rtile[4 * d + 2] = jnp.exp(lg * pos_k)
        rtile[4 * d + 3] = jnp.exp(lg * jnp.full((CHUNK, CHUNK), float(CHUNK), F32))

    state[...] = s0_ref[...]

    def bcast_col(cols, j):
        return jnp.broadcast_to(cols[:, j:j + 1], (CHUNK, CHUNK))

    pre_chunks = min(8, n_chunk)

    def prepass(m, carry):
        chunks = [m * pre_chunks + j for j in range(pre_chunks)]
        chunk_cs = [pl.ds(pl.multiple_of(n * CHUNK, CHUNK), CHUNK) for n in chunks]
        conv_k, conv_v, conv_q = [], [], []
        for n, cs in zip(chunks, chunk_cs):
            conv_k.append(l2n(conv_chunk(gk_ref, ck_ref, n)).astype(BF16))
            conv_v.append(conv_chunk(gv_ref, cv_ref, n).astype(BF16))
            if with_query:
                conv_q.append((l2n(conv_chunk(gq_ref, cq_ref, n)) * (HEAD_DIM ** -0.5)).astype(BF16))
                ogdn[cs, :] = jnp.zeros((CHUNK, HEAD_DIM), F32)

        jobs = [(j, d) for j in range(pre_chunks) for d in range(2)]
        css = [chunk_cs[j] for j, _ in jobs]
        slots = [d * n_chunk + chunks[j] for j, d in jobs]
        dirs = [d for _, d in jobs]
        ks = [rk_ref[cs, :] for cs in css]
        vs = [rv_ref[cs, :] for cs in css]
        if with_query:
            scs = [(_dot_nt(rq_ref[cs, :], k) * rtile[4 * d + 0]).astype(BF16) for cs, k, d in zip(css, ks, dirs)]
            outs = [_dot(sc, v) for sc, v in zip(scs, vs)]
            for j, cs in enumerate(chunk_cs):
                oret[cs, :] = outs[2 * j] + outs[2 * j + 1]
        kvs = [_dot_tn((k.astype(F32) * rtile[4 * d + 2]).astype(BF16), v) for k, v, d in zip(ks, vs, dirs)]
        for slot, kv in zip(slots, kvs):
            kvbuf[slot] = kv
        ks = [conv_k[j] for j, _ in jobs]
        vs = [conv_v[j] for j, _ in jobs]
        colss = [cols_ref[cs, :] for cs in css]
        betas = [bcast_col(cols, 2 + d) for cols, d in zip(colss, dirs)]
        incls = [(row >= colm) if d == 0 else (row <= colm) for d in dirs]
        stricts = [(row > colm) if d == 0 else (row < colm) for d in dirs]
        decs = [jnp.exp(jnp.where(incl, bcast_col(cols, d) - jnp.broadcast_to(grow_ref[d:d + 1, cs], (CHUNK, CHUNK)), 0.0))
                for cols, d, cs, incl in zip(colss, dirs, css, incls)]
        kks = [_dot_nt(k, k) for k in ks]
        mats = [kk * beta * jnp.where(strict, dec, 0.0) for kk, beta, strict, dec in zip(kks, betas, stricts, decs)]
        tinvs = [t.astype(BF16) for t in _unit_triangular_inverses(mats, [d == 0 for d in dirs])]
        rhs = [jnp.concatenate([(beta * v.astype(F32)).astype(BF16),
                                (beta * bcast_col(cols, 4 + d) * k.astype(F32)).astype(BF16)], axis=1)
               for beta, v, cols, d, k in zip(betas, vs, colss, dirs, ks)]
        uws = [_dot(t, r) for t, r in zip(tinvs, rhs)]
        us = [uw[:, :HEAD_DIM] for uw in uws]
        ws = [uw[:, HEAD_DIM:].astype(BF16) for uw in uws]
        kts = [(k.astype(F32) * bcast_col(cols, 6 + d)).astype(BF16) for k, cols, d in zip(ks, colss, dirs)]
        ncs = [_dot_tn(kt, uw.astype(BF16)) for kt, uw in zip(kts, uws)]
        for slot, nc_mat in zip(slots, ncs):
            cbuf[slot] = nc_mat[:, :HEAD_DIM]
            nbuf[slot] = nc_mat[:, HEAD_DIM:].astype(BF16)
        if with_query:
            for slot, u, w in zip(slots, us, ws):
                ubuf[slot] = u.astype(BF16)
                wbuf[slot] = w
            qks = [_dot_nt(conv_q[j], k) for (j, _), k in zip(jobs, ks)]
            for slot, qk, incl, dec in zip(slots, qks, incls, decs):
                pbuf[slot] = (qk * jnp.where(incl, dec, 0.0)).astype(BF16)
            for (j, d), cs, cols in zip(jobs, css, colss):
                qes[d, cs, :] = (conv_q[j].astype(F32) * bcast_col(cols, 4 + d)).astype(BF16)
        return carry

    lax.fori_loop(0, n_chunk // pre_chunks, prepass, 0)

    def finish(cs):
        ro = oret[cs, :]
        rc = ro - jnp.mean(ro, axis=-1, keepdims=True)
        ry = rc * lax.rsqrt(jnp.mean(rc * rc, axis=-1, keepdims=True) + EPS)
        yret_ref[cs, :] = (ry * gnw_ref[...] * _silu(rg_ref[cs, :].astype(F32))).astype(BF16)
        go = ogdn[cs, :]
        gy = go * lax.rsqrt(jnp.mean(go * go, axis=-1, keepdims=True) + EPS)
        ygdn_ref[cs, :] = (gy * rmsw_ref[...] * _silu(gz_ref[cs, :].astype(F32))).astype(BF16)

    def chunk_ids(n):
        nds = [n, n_chunk - 1 - n]
        css = [pl.ds(nd * CHUNK if isinstance(nd, int) else pl.multiple_of(nd * CHUNK, CHUNK), CHUNK) for nd in nds]
        slots = [d * n_chunk + nd for d, nd in enumerate(nds)]
        return nds, css, slots

    def late_outputs(n, vnb, finishing):
        _, css, slots = chunk_ids(n)
        for d in range(2):
            ogdn[css[d], :] += _dot(pbuf[slots[d]], vnb[d])
            if finishing:
                finish(css[d])

    def scan_step(n, vnb_prev, has_prev, finish_prev):
        nds, css, slots = chunk_ids(n)
        ret_st = [state[d] for d in range(2)]
        gdn_st = [state[2 + d] for d in range(2)]
        gdn_stb = [st.astype(BF16) for st in gdn_st]
        shrink = [_dot(nbuf[slot], stb) for slot, stb in zip(slots, gdn_stb)]
        for d in range(2):
            state[2 + d] = gl_ref[bi * n_chunk + nds[d], 8 * hi + d] * gdn_st[d] - shrink[d] + cbuf[slots[d]]
            state[d] = rtile[4 * d + 3] * ret_st[d] + kvbuf[slots[d]]
        if not with_query:
            return vnb_prev
        vnb = tuple((ubuf[slot].astype(F32) - _dot(wbuf[slot], stb)).astype(BF16)
                    for slot, stb in zip(slots, gdn_stb))
        for d in range(2):
            oret[css[d], :] += _dot(rq_ref[css[d], :], ret_st[d].astype(BF16)) * rtile[4 * d + 1]
            ogdn[css[d], :] += _dot(qes[d, css[d], :], gdn_stb[d])
        if has_prev:
            late_outputs(n - 1, vnb_prev, finish_prev)
        return vnb

    if with_query:
        half = n_chunk // 2
        zero = jnp.zeros((CHUNK, HEAD_DIM), BF16)
        vnb = scan_step(0, (zero, zero), has_prev=False, finish_prev=False)
        vnb = lax.fori_loop(1, half + 1, functools.partial(scan_step, has_prev=True, finish_prev=False), vnb)
        vnb = lax.fori_loop(half + 1, n_chunk, functools.partial(scan_step, has_prev=True, finish_prev=True), vnb)
        late_outputs(n_chunk - 1, vnb, True)
    else:
        lax.fori_loop(0, n_chunk, functools.partial(scan_step, has_prev=False, finish_prev=False), 0)
        sfin_ref[...] = state[...]


def _mixers(lg, gl, feats, dec, conv, s0, gnw, rmsw, with_query):
    cols, grow = dec
    b, l, _ = feats["rk"].shape
    n_slot = 2 * (l // CHUNK)
    assert (l // CHUNK) % min(8, l // CHUNK) == 0
    smem = pl.BlockSpec(memory_space=pltpu.SMEM)
    head = pl.BlockSpec((None, l, HEAD_DIM), lambda i, j: (i, 0, j))
    colspec = pl.BlockSpec((None, None, l, 8), lambda i, j: (i, j, 0, 0))
    rowspec = pl.BlockSpec((None, None, 8, l), lambda i, j: (i, j, 0, 0))
    tile_f32 = pltpu.VMEM((n_slot, CHUNK, CHUNK), F32)
    tile_bf16 = pltpu.VMEM((n_slot, CHUNK, CHUNK), BF16)
    st_spec = pl.BlockSpec((None, None, 4, HEAD_DIM, HEAD_DIM), lambda i, j: (i, j, 0, 0, 0))

    def conv_spec(which):
        return pl.BlockSpec((SHORT_CONV, HEAD_DIM), lambda i, j, w=which: (0, w * N_HEADS + j))

    def gain_spec():
        return pl.BlockSpec((1, HEAD_DIM), lambda i, j: (0, j))

    if with_query:
        args = [lg, gl, feats["rk"], feats["rv"], feats["gk"], feats["gv"], feats["rq"], feats["gq"], feats["rg"],
                feats["gz"], cols, grow, conv, conv, conv, s0, gnw, rmsw]
        in_specs = [smem, smem] + [head] * 8 + [colspec, rowspec, conv_spec(0), conv_spec(1), conv_spec(2),
                                                st_spec, gain_spec(), gain_spec()]
        out_shape = [jax.ShapeDtypeStruct((b, l, MIX_W), BF16)] * 2
        out_specs = [head, head]
        scratch = ([pltpu.VMEM((2, l, HEAD_DIM), BF16)] + [pltpu.VMEM((l, HEAD_DIM), F32)] * 2
                   + [tile_bf16, tile_bf16, tile_bf16, tile_bf16, tile_f32, tile_f32])
    else:
        args = [lg, gl, feats["rk"], feats["rv"], feats["gk"], feats["gv"], cols, grow, conv, conv, s0]
        in_specs = [smem, smem] + [head] * 4 + [colspec, rowspec, conv_spec(1), conv_spec(2), st_spec]
        out_shape = jax.ShapeDtypeStruct((b, N_HEADS, 4, HEAD_DIM, HEAD_DIM), F32)
        out_specs = st_spec
        scratch = [tile_bf16, tile_f32, tile_f32]
    scratch = scratch + [pltpu.VMEM((8, CHUNK, CHUNK), F32), pltpu.VMEM((4, HEAD_DIM, HEAD_DIM), F32)]
    return pl.pallas_call(
        functools.partial(_mixer_kernel, seq_len=l, with_query=with_query),
        grid=(b, N_HEADS),
        in_specs=in_specs,
        out_specs=out_specs,
        out_shape=out_shape,
        scratch_shapes=scratch,
        compiler_params=_cparams(("arbitrary", "arbitrary")),
        name="mixers_latent" if with_query else "mixers_context",
    )(*args)


def _split_bf16(v):
    hi = v.astype(BF16)
    return hi, (v - hi.astype(F32)).astype(BF16)


def _route(h2, wrt_ref, bias_ref, cand_ref, sel_ref, wd_ref):
    tm = h2.shape[0]
    h_hi, h_lo = _split_bf16(h2)
    w_hi, w_lo = _split_bf16(wrt_ref[...])
    logits = _dot_nt(w_hi, h_hi) + (_dot_nt(w_hi, h_lo) + _dot_nt(w_lo, h_hi))
    scores = _sigmoid(logits)
    biased = scores + bias_ref[...]
    neg_inf = float("-inf")
    sub = _iota((GROUP_SIZE, tm), 0).astype(F32)
    group_score = []
    for g in range(N_GROUPS):
        blk = biased[g * GROUP_SIZE:(g + 1) * GROUP_SIZE, :]
        m1 = jnp.max(blk, axis=0, keepdims=True)
        first = jnp.min(jnp.where(blk == m1, sub, float(GROUP_SIZE)), axis=0, keepdims=True)
        m2 = jnp.max(jnp.where(sub == first, neg_inf, blk), axis=0, keepdims=True)
        group_score.append(m1 + m2)
    for g in range(N_GROUPS):
        ahead = jnp.zeros((1, tm), I32)
        for g2 in range(N_GROUPS):
            if g2 == g:
                continue
            before = (group_score[g2] > group_score[g])
            if g2 < g:
                before = before | (group_score[g2] == group_score[g])
            ahead = ahead + before.astype(I32)
        keep = jnp.broadcast_to(ahead, (GROUP_SIZE, tm)) < TOPK_GROUPS
        cand_ref[g * GROUP_SIZE:(g + 1) * GROUP_SIZE, :] = jnp.where(
            keep, biased[g * GROUP_SIZE:(g + 1) * GROUP_SIZE, :], neg_inf)
    work = cand_ref[...]
    eidx = _iota((N_EXPERTS, tm), 0).astype(F32)
    sel = jnp.zeros((N_EXPERTS, tm), jnp.bool_)
    for _ in range(TOP_K):
        best = jnp.max(work, axis=0, keepdims=True)
        first = jnp.min(jnp.where(work == best, eidx, float(N_EXPERTS)), axis=0, keepdims=True)
        pick = eidx == first
        sel = sel | pick
        work = jnp.where(pick, neg_inf, work)
    picked = jnp.where(sel, scores, 0.0)
    wsum = jnp.sum(picked, axis=0, keepdims=True)
    sel_ref[...] = sel.astype(F32)
    wd_ref[...] = picked / wsum * ROUTED_SCALE


def _mixout_kernel(x_ref, yr_ref, yg_ref, g0_ref, g1_ref, g2_ref, g3_ref, wr_ref, wg_ref, wo_ref,
                   npost_ref, gate1_ref, nffn_ref, sc2_ref, sh2_ref, wrt_ref, bias_ref,
                   x1_ref, h2_ref, sel_ref, wd_ref, cand_ref):
    r = _dot(yr_ref[...], wr_ref[...])
    g = _dot(yg_ref[...], wg_ref[...])
    half = r.shape[1] // 2
    merged = jnp.concatenate(
        [g0_ref[...].astype(F32) * r[:, :half] + g2_ref[...].astype(F32) * g[:, :half],
         g1_ref[...].astype(F32) * r[:, half:] + g3_ref[...].astype(F32) * g[:, half:]], axis=1)
    mo = _dot(merged.astype(BF16), wo_ref[...])
    x1 = x_ref[...] + gate1_ref[...] * _rms(mo, npost_ref[...])
    x1_ref[...] = x1
    h2 = _rms(x1, nffn_ref[...]) * (1.0 + sc2_ref[...]) + sh2_ref[...]
    h2_ref[...] = h2.astype(BF16)
    _route(h2, wrt_ref, bias_ref, cand_ref, sel_ref, wd_ref)


def _mixer_out(x, yr, yg, gates, wr, wg, wo, npost, gate1, nffn, sc2, sh2, wrt, bias):
    b, l, d = x.shape
    tm = MIXOUT_TM
    tiles = l // tm
    tok = lambda w: pl.BlockSpec((None, tm, w), lambda i, j: (i, j, 0))
    const = functools.partial(pl.BlockSpec, pipeline_mode=pl.Buffered(1))
    vec = lambda: const((1, d), lambda i, j: (0, 0))
    bvec = lambda: pl.BlockSpec((None, 1, d), lambda i, j: (i, 0, 0))
    emap = pl.BlockSpec((N_EXPERTS, tm), lambda i, j: (0, i * tiles + j))
    in_specs = ([tok(d), tok(MIX_W), tok(MIX_W)] + [tok(MIX_W)] * 4
                + [const((MIX_W, d), lambda i, j: (0, 0)), const((MIX_W, d), lambda i, j: (0, 0)),
                   const((d, d), lambda i, j: (0, 0)),
                   vec(), bvec(), vec(), bvec(), bvec(),
                   const((N_EXPERTS, d), lambda i, j: (0, 0)), const((N_EXPERTS, 1), lambda i, j: (0, 0))])
    return pl.pallas_call(
        _mixout_kernel,
        grid=(b, tiles),
        in_specs=in_specs,
        out_specs=[tok(d), tok(d), emap, emap],
        out_shape=[jax.ShapeDtypeStruct((b, l, d), F32), jax.ShapeDtypeStruct((b, l, d), BF16),
                   jax.ShapeDtypeStruct((N_EXPERTS, b * l), F32), jax.ShapeDtypeStruct((N_EXPERTS, b * l), F32)],
        scratch_shapes=[pltpu.VMEM((N_EXPERTS, tm), F32)],
        compiler_params=_cparams(("arbitrary", "arbitrary")),
        name="mixer_out_router",
    )(x, yr, yg, *gates, wr, wg, wo, npost, gate1, nffn, sc2, sh2, wrt, bias)


def _tile_positions(sel):
    tm = sel.shape[1]
    selb = sel.astype(BF16)
    earlier = (_iota((tm, tm), 0) < _iota((tm, tm), 1)).astype(BF16)
    rank = _dot(selb, earlier)
    cnt = _dot(selb, jnp.ones((tm, tm), BF16))
    nmb = jnp.floor((cnt + (MB_ROWS - 1)) * (1.0 / MB_ROWS))
    below = (_iota((N_EXPERTS, N_EXPERTS), 1) < _iota((N_EXPERTS, N_EXPERTS), 0)).astype(BF16)
    offmb = _dot(below, nmb.astype(BF16))
    rank_hi = jnp.floor(rank * (1.0 / MB_ROWS))
    hi = jnp.where(sel > 0.0, offmb + rank_hi, 255.0)
    lo = rank - rank_hi * MB_ROWS
    return hi, lo, cnt, offmb, rank


def _dispatch_kernel(h2_ref, sel_ref, xs_ref, cnt_ref):
    last = pl.num_programs(0) - 1

    @pl.when(pl.program_id(0) < last)
    def _():
        _dispatch_tile(h2_ref, sel_ref, xs_ref, cnt_ref)

    @pl.when(pl.program_id(0) == last)
    def _():
        xs_ref[...] = jnp.zeros_like(xs_ref)


def _dispatch_tile(h2_ref, sel_ref, xs_ref, cnt_ref):
    tm = h2_ref.shape[0]
    sel = sel_ref[...]
    hi, lo, _, _, _ = _tile_positions(sel)
    hib = hi.astype(BF16)
    lob = lo.astype(BF16)
    selb = sel.astype(BF16)
    cnt_row = _dot_nt(jnp.ones((8, tm), BF16), selb)
    nmb_row = jnp.floor((cnt_row + (MB_ROWS - 1)) * (1.0 / MB_ROWS))
    before = (_iota((N_EXPERTS, N_EXPERTS), 0) < _iota((N_EXPERTS, N_EXPERTS), 1)).astype(BF16)
    start_row = _dot(nmb_row.astype(BF16), before) * MB_ROWS
    cnt_ref[...] = cnt_row
    start = start_row[0:1, :]
    end = start + cnt_row[0:1, :]
    x = h2_ref[...]
    rch = 256
    for rc in range(TILE_ROWS // rch):
        r_e = (rc * rch + _iota((rch, N_EXPERTS), 0)).astype(F32)
        owner = ((r_e >= start) & (r_e < end)).astype(BF16)
        r_t = rc * rch + _iota((rch, tm), 0)
        match = ((_dot(owner, hib) == (r_t >> _MB_SHIFT).astype(F32))
                 & (_dot(owner, lob) == (r_t & (MB_ROWS - 1)).astype(F32)))
        xs_ref[rc * rch:(rc + 1) * rch, :] = _dot(match.astype(BF16), x).astype(BF16)


def _dispatch(h2, sel_t):
    t, d = h2.shape
    tm = MOE_TM
    nt = t // tm
    return pl.pallas_call(
        _dispatch_kernel,
        grid=(nt + 1,),
        in_specs=[pl.BlockSpec((tm, d), lambda i: (jnp.minimum(i, nt - 1), 0)),
                  pl.BlockSpec((N_EXPERTS, tm), lambda i: (0, jnp.minimum(i, nt - 1)))],
        out_specs=[pl.BlockSpec((TILE_ROWS, d), lambda i: (i, 0)),
                   pl.BlockSpec((None, 8, N_EXPERTS), lambda i: (jnp.minimum(i, nt - 1), 0, 0))],
        out_shape=[jax.ShapeDtypeStruct((nt * TILE_ROWS + 3 * EXP_BM, d), BF16),
                   jax.ShapeDtypeStruct((nt, 8, N_EXPERTS), F32)],
        compiler_params=_cparams(("arbitrary",)),
        name="moe_dispatch",
    )(h2, sel_t)


def _expert_plan(cnt, n_blk):
    nt = cnt.shape[0]
    nmb = (cnt + (MB_ROWS - 1)) // MB_ROWS
    offmb = jnp.cumsum(nmb, axis=1) - nmb
    per_e = nmb.T
    incl = jnp.cumsum(per_e, axis=1)
    excl = incl - per_e
    tot = incl[:, -1]
    nb = (tot + (EXP_MB - 1)) // EXP_MB
    bend = jnp.cumsum(nb)
    bstart = bend - nb
    b = jnp.arange(n_blk, dtype=I32)
    valid = b < bend[-1]
    last = jnp.maximum(bend[-1] - 1, 0)
    bq = jnp.where(valid, b, last)
    blk_e = jnp.minimum(jnp.sum((bend[None, :] <= bq[:, None]).astype(I32), axis=1), N_EXPERTS - 1)
    onehot_e = (blk_e[:, None] == jnp.arange(N_EXPERTS, dtype=I32)[None, :]).astype(I32)
    pick = lambda table: jnp.sum(onehot_e[:, :, None] * table[None, :, :], axis=1)
    bstart_b = jnp.sum(onehot_e * bstart[None, :], axis=1)
    tot_b = jnp.sum(onehot_e * tot[None, :], axis=1)
    p0 = (bq - bstart_b) * EXP_MB
    blk_nmb = jnp.where(valid, jnp.clip(tot_b - p0, 0, EXP_MB), 0).astype(I32)
    p = p0[:, None] + jnp.arange(EXP_MB, dtype=I32)[None, :]
    incl_b = pick(incl)
    tile = jnp.minimum(jnp.sum((incl_b[:, None, :] <= p[:, :, None]).astype(I32), axis=2), nt - 1)
    onehot_t = (tile[:, :, None] == jnp.arange(nt, dtype=I32)[None, None, :]).astype(I32)
    shift_b = pick(offmb.T - excl)[:, None, :]
    where = tile * _TILE_MB + p + jnp.sum(onehot_t * shift_b, axis=2)
    j = jnp.arange(EXP_MB, dtype=I32)[None, :]
    used = j < blk_nmb[:, None]
    spare = nt * _TILE_MB + (b[:, None] % 2) * EXP_MB + j
    zeros_mb = nt * _TILE_MB + 2 * EXP_MB + j
    src = jnp.where(used, where, jnp.where(valid[:, None], where[:, :1], zeros_mb))
    dst = jnp.where(used, where, spare)
    return bstart.astype(I32), nb.astype(I32), src.reshape(-1).astype(I32), dst.reshape(-1).astype(I32)


def _expert_kernel(bstart_ref, nb_ref, src_ref, dst_ref, xs_hbm, wg_ref, wu_ref, wd_ref, ys_hbm,
                   xbuf, ybuf, wgub, wdb, in_sem, out_sem):
    e = pl.program_id(0)
    n_in = xbuf.shape[0]

    def gather_copy(blk, slot, j):
        rows = pl.ds(pl.multiple_of(src_ref[blk * EXP_MB + j] * MB_ROWS, MB_ROWS), MB_ROWS)
        return pltpu.make_async_copy(xs_hbm.at[rows, :], xbuf.at[slot, j * MB_ROWS:(j + 1) * MB_ROWS, :],
                                     in_sem.at[slot])

    def scatter_copy(blk, slot, j):
        rows = pl.ds(pl.multiple_of(dst_ref[blk * EXP_MB + j] * MB_ROWS, MB_ROWS), MB_ROWS)
        return pltpu.make_async_copy(ybuf.at[slot, j * MB_ROWS:(j + 1) * MB_ROWS, :], ys_hbm.at[rows, :],
                                     out_sem.at[slot])

    def start_gather(blk):
        for j in range(EXP_MB):
            gather_copy(blk, lax.rem(blk, n_in), j).start()

    def wait_gather(blk):
        for j in range(EXP_MB):
            gather_copy(blk, lax.rem(blk, n_in), j).wait()

    def start_scatter(blk):
        for j in range(EXP_MB):
            scatter_copy(blk, blk % 2, j).start()

    def wait_scatter(slot):
        for j in range(EXP_MB):
            scatter_copy(0, slot, j).wait()

    @pl.when(e == 0)
    def _():
        ybuf[...] = jnp.zeros_like(ybuf)
        for ahead in range(EXP_LEAD):
            start_gather(ahead)
        spare_row0 = ys_hbm.shape[0] - 3 * EXP_BM
        for slot in range(2):
            for j in range(EXP_MB):
                spare = pl.ds(spare_row0 + (slot * EXP_MB + j) * MB_ROWS, MB_ROWS)
                pltpu.make_async_copy(ybuf.at[slot, j * MB_ROWS:(j + 1) * MB_ROWS, :], ys_hbm.at[spare, :],
                                      out_sem.at[slot]).start()

    de = wg_ref.shape[1]
    wgub[:, :de] = wg_ref[...].astype(BF16)
    wgub[:, de:] = wu_ref[...].astype(BF16)
    wdb[...] = wd_ref[...].astype(BF16)

    def block(b, carry):
        slot = b % 2
        wait_gather(b)
        wait_scatter(slot)
        x = xbuf[lax.rem(b, n_in)]
        gu = _dot(x, wgub[...])
        act = (_silu(gu[:, :de]) * gu[:, de:]).astype(BF16)
        start_gather(b + EXP_LEAD)
        ybuf[slot] = _dot(act, wdb[...]).astype(BF16)
        start_scatter(b)
        return carry

    first = bstart_ref[e]
    lax.fori_loop(first, first + nb_ref[e], block, 0)

    @pl.when(e == pl.num_programs(0) - 1)
    def _():
        total = first + nb_ref[e]
        for ahead in range(EXP_LEAD):
            wait_gather(total + ahead)
        wait_scatter(0)
        wait_scatter(1)


def _expert_ffn(xs, bstart, nb, src, dst, wg, wu, wd):
    rows, d = xs.shape
    de = wg.shape[2]
    grid_spec = pltpu.PrefetchScalarGridSpec(
        num_scalar_prefetch=4,
        grid=(wg.shape[0],),
        in_specs=[pl.BlockSpec(memory_space=pl.ANY),
                  pl.BlockSpec((None, d, de), lambda i, bs, bn, sr, ds: (i, 0, 0)),
                  pl.BlockSpec((None, d, de), lambda i, bs, bn, sr, ds: (i, 0, 0)),
                  pl.BlockSpec((None, de, d), lambda i, bs, bn, sr, ds: (i, 0, 0))],
        out_specs=pl.BlockSpec(memory_space=pl.ANY),
        scratch_shapes=[pltpu.VMEM((EXP_LEAD + 1, EXP_BM, d), BF16), pltpu.VMEM((2, EXP_BM, d), BF16),
                        pltpu.VMEM((d, 2 * de), BF16), pltpu.VMEM((de, d), BF16),
                        pltpu.SemaphoreType.DMA((EXP_LEAD + 1,)), pltpu.SemaphoreType.DMA((2,))],
    )
    return pl.pallas_call(
        _expert_kernel,
        grid_spec=grid_spec,
        out_shape=jax.ShapeDtypeStruct((rows, d), BF16),
        input_output_aliases={4: 0},
        compiler_params=_cparams(("arbitrary",)),
        name="moe_experts",
    )(bstart, nb, src, dst, xs, wg, wu, wd)


def _combine_kernel(ys_ref, sel_ref, wd_ref, h2_ref, x1_ref, wsg_ref, wsu_ref, wsd_ref, npost_ref, gate2_ref,
                    o_ref, c_ref):
    tm = h2_ref.shape[0]
    sel = sel_ref[...]
    _, _, cnt, offmb, rank = _tile_positions(sel)
    no_rank = 1.5 * tm
    lhs = jnp.concatenate([jnp.where(sel > 0.0, rank, no_rank).T, wd_ref[...].T], axis=1).astype(BF16)
    start = offmb[:, :LANES] * MB_ROWS
    end = start + cnt[:, :LANES]
    zeros = jnp.zeros((N_EXPERTS, LANES), BF16)
    for cc in range(TILE_ROWS // LANES):
        r_e = (cc * LANES + _iota((N_EXPERTS, LANES), 1)).astype(F32)
        owned = (r_e >= start) & (r_e < end)
        owner = owned.astype(BF16)
        local = r_e[0:1, :] - jnp.sum(jnp.where(owned, start, 0.0), axis=0, keepdims=True)
        rhs = jnp.concatenate([jnp.concatenate([owner, zeros], axis=1),
                               jnp.concatenate([zeros, owner], axis=1)], axis=0)
        got = _dot(lhs, rhs)
        c_ref[:, cc * LANES:(cc + 1) * LANES] = jnp.where(got[:, :LANES] == local, got[:, LANES:], 0.0).astype(BF16)
    routed = _dot(c_ref[...], ys_ref[...])
    h2 = h2_ref[...]
    shared = _dot((_silu(_dot(h2, wsg_ref[...])) * _dot(h2, wsu_ref[...])).astype(BF16), wsd_ref[...])
    o_ref[...] = x1_ref[...] + gate2_ref[...] * _rms(routed + shared, npost_ref[...])


def _combine(ys, sel_t, wd_t, h2, x1, wsg, wsu, wsd, npost, gate2, seq_len):
    t, d = h2.shape
    tm = MOE_TM
    nt = t // tm
    per_seq = seq_len // tm
    ds = wsg.shape[1]
    const = functools.partial(pl.BlockSpec, pipeline_mode=pl.Buffered(1))
    emap = pl.BlockSpec((N_EXPERTS, tm), lambda i: (0, i))
    return pl.pallas_call(
        _combine_kernel,
        grid=(nt,),
        in_specs=[pl.BlockSpec((TILE_ROWS, d), lambda i: (i, 0)), emap, emap,
                  pl.BlockSpec((tm, d), lambda i: (i, 0)), pl.BlockSpec((tm, d), lambda i: (i, 0)),
                  const((d, ds), lambda i: (0, 0)), const((d, ds), lambda i: (0, 0)), const((ds, d), lambda i: (0, 0)),
                  const((1, d), lambda i: (0, 0)),
                  pl.BlockSpec((None, 1, d), lambda i: (i // per_seq, 0, 0))],
        out_specs=pl.BlockSpec((tm, d), lambda i: (i, 0)),
        out_shape=jax.ShapeDtypeStruct((t, d), F32),
        scratch_shapes=[pltpu.VMEM((tm, TILE_ROWS), BF16)],
        compiler_params=_cparams(("arbitrary",)),
        name="moe_combine",
    )(ys, sel_t, wd_t, h2, x1, wsg, wsu, wsd, npost, gate2)


def _rope_tables(n):
    rows = n // GRID_W
    pos_r = jnp.repeat(jnp.arange(rows, dtype=F32), GRID_W)
    pos_c = jnp.tile(jnp.arange(GRID_W, dtype=F32), rows)
    n_freq = HEAD_DIM // 4
    inv = ROPE_BASE ** (-jnp.arange(n_freq, dtype=F32) / n_freq)
    ang = jnp.concatenate([pos_r[:, None] * inv, pos_c[:, None] * inv], axis=-1)
    cos, sin = jnp.cos(ang), jnp.sin(ang)
    return jnp.concatenate([cos, cos], axis=-1), jnp.concatenate([-sin, sin], axis=-1)


def kernel(x, c, ctx, c_ctx, w_mod, b_mod, norm_mix_pre, norm_mix_post, norm_ffn_pre, norm_ffn_post, w_in, gdn_conv, ret_log_decay, gdn_a_log, gdn_dt_bias, ret_gn_w, gdn_norm_w, w_ret_out, w_gdn_out, w_o, w_router, router_bias, w_gate, w_up, w_down, w_sh_gate, w_sh_up, w_sh_down):
    b, n, d = x.shape
    depth = w_mod.shape[0]
    assert depth == 1, "single-layer block"
    assert n % max(PROJ_TM, MOE_TM) == 0 and ctx.shape[1] % CHUNK == 0
    assert _TILE_MB < 255

    rows = -(-(b + 1) // 8) * 8
    cvec = jnp.zeros((rows, d), F32).at[:b].set(c).at[b].set(c_ctx)
    mod = _modulation(cvec, w_mod[0], b_mod[0][None, :])
    sh1, sc1, g1, sh2, sc2, g2 = [mod[:b, k * d:(k + 1) * d][:, None, :] for k in range(6)]
    ctx_shift = jnp.broadcast_to(mod[b, 0:d][None, None, :], (b, 1, d))
    ctx_scale = jnp.broadcast_to(mod[b, d:2 * d][None, None, :], (b, 1, d))

    w_in0 = w_in[0]
    n_main = 4 * MIX_W
    w_state = w_in0[:, :n_main].astype(BF16)
    slots = ((0, 0), (0, 1), (1, 0), (1, 1), (0, 0), (0, 1), (0, 0), (0, 1))
    gab_cols = n_main + jnp.array([[ab * 2 * N_HEADS + dr * N_HEADS + hh for ab, dr in slots]
                                   for hh in range(N_HEADS)], I32)
    w_gab = w_in0[:, gab_cols.reshape(-1)].astype(BF16)
    n_state = n_main + N_GAB
    w_query = w_in0[:, n_state:].astype(BF16)
    w_all = jnp.concatenate([w_state, w_query], axis=1)
    a_coef = -jnp.exp(gdn_a_log[0].astype(F32))
    dtb = gdn_dt_bias[0].astype(F32)
    is_alpha = jnp.array([ab == 0 for ab, _ in slots])
    dirs = jnp.array([dr for _, dr in slots], I32)
    prow = jnp.stack([jnp.where(is_alpha[None, :], a_coef.T[:, dirs], 0.0).reshape(-1),
                      jnp.where(is_alpha[None, :], dtb.T[:, dirs], 0.0).reshape(-1)], axis=0)
    first2 = jnp.tile(jnp.arange(8) < 2, N_HEADS)
    w_gab_t = w_gab.T * first2[:, None].astype(BF16)
    pcol = prow.T * first2[:, None]
    gain_mix = norm_mix_pre[0][None, :]
    cos2, sin2 = _rope_tables(n)

    state_kinds = (_ROPE_SCALED, _PLAIN, _PLAIN, _PLAIN)
    query_kinds = (_ROPE, _PLAIN, _PLAIN, _PLAIN, _SIGMOID, _SIGMOID, _SIGMOID, _SIGMOID)
    lg = ret_log_decay[0].astype(F32)
    conv = gdn_conv[0].astype(F32)

    lc = ctx.shape[1]
    cfe, (ccols, cgrow, cgl) = _in_projection(
        ctx, gain_mix, ctx_scale, ctx_shift, w_state, w_gab, w_gab_t, prow, pcol,
        cos2[:lc], sin2[:lc], state_kinds, rope=False)
    cfeats = dict(zip(("rk", "rv", "gk", "gv"), cfe))
    zero_state = jnp.zeros((b, N_HEADS, 4, HEAD_DIM, HEAD_DIM), F32)
    init = _mixers(lg, cgl, cfeats, (ccols, cgrow), conv, zero_state, None, None, with_query=False)

    fe, (cols, grow, gl) = _in_projection(
        x, gain_mix, sc1, sh1, w_all, w_gab, w_gab_t, prow, pcol, cos2, sin2,
        state_kinds + query_kinds, rope=True)
    feats = dict(zip(("rk", "rv", "gk", "gv", "rq", "rg", "gq", "gz"), fe[:8]))
    gates = fe[8:]
    y_ret, y_gdn = _mixers(lg, gl, feats, (cols, grow), conv, init,
                           ret_gn_w[0][None, :], gdn_norm_w[0][None, :], with_query=True)
    x1, h2, sel_t, wd_t = _mixer_out(
        x, y_ret, y_gdn, gates, w_ret_out[0].astype(BF16), w_gdn_out[0].astype(BF16), w_o[0].astype(BF16),
        norm_mix_post[0][None, :], g1, norm_ffn_pre[0][None, :], sc2, sh2,
        w_router[0].T.astype(F32), router_bias[0].astype(F32)[:, None])

    t = b * n
    h2f = h2.reshape(t, d)
    xs, cnt = _dispatch(h2f, sel_t)
    nt = t // MOE_TM
    n_blk = nt * _TILE_MB // EXP_MB + N_EXPERTS + EXP_LEAD
    bstart, nb, src, dst = _expert_plan(cnt[:, 0, :].astype(I32), n_blk)
    ys = _expert_ffn(xs, bstart, nb, src, dst, w_gate[0], w_up[0], w_down[0])
    out = _combine(ys, sel_t, wd_t, h2f, x1.reshape(t, d), w_sh_gate[0].astype(BF16), w_sh_up[0].astype(BF16),
                   w_sh_down[0].astype(BF16), norm_ffn_post[0][None, :], g2, n)
    return out.reshape(b, n, d)
```

```python
import functools
import math

import jax
import jax.numpy as jnp
from jax import lax
from jax.experimental import pallas as pl
from jax.experimental.pallas import tpu as pltpu

F32 = jnp.float32
BF16 = jnp.bfloat16
I32 = jnp.int32
HIGHEST = lax.Precision.HIGHEST

N_HEADS = 4
HEAD_DIM = 128
MIX_W = N_HEADS * HEAD_DIM
CHUNK = 128
SHORT_CONV = 3
ROPE_BASE = 10000.0
GRID_W = 64
N_EXPERTS = 64
TOP_K = 8
N_GROUPS = 8
TOPK_GROUPS = 4
GROUP_SIZE = N_EXPERTS // N_GROUPS
ROUTED_SCALE = 2.5
EPS = 1e-6
N_GAB = 4 * N_HEADS

LANES = 128
BF16_TILE_ROWS = 16
VMEM_LIMIT_BYTES = 56 * 1024 * 1024

PROJ_TM = 512
MIXOUT_TM = 1024
MOE_TM = 256
MB_ROWS = BF16_TILE_ROWS
_MB_SHIFT = MB_ROWS.bit_length() - 1
_TILE_MB = -(-(TOP_K * MOE_TM // MB_ROWS + N_EXPERTS * (MB_ROWS - 1) // MB_ROWS + 1) // 8) * 8
TILE_ROWS = _TILE_MB * MB_ROWS
EXP_BM = 512
EXP_MB = EXP_BM // MB_ROWS
EXP_LEAD = 3
EXP_OUT = 3


def _cparams(sem):
    return pltpu.CompilerParams(dimension_semantics=sem, vmem_limit_bytes=VMEM_LIMIT_BYTES)


def _sigmoid(v):
    return 0.5 * jnp.tanh(0.5 * v) + 0.5


def _silu(v):
    return v * _sigmoid(v)


def _softplus(v):
    return jnp.maximum(v, 0.0) + jnp.log1p(jnp.exp(-jnp.abs(v)))


def _iota(shape, dim):
    return lax.broadcasted_iota(I32, shape, dim)


def _dot(a, b, **kw):
    return jnp.dot(a, b, preferred_element_type=F32, **kw)


def _dot_nt(a, b, **kw):
    return lax.dot_general(a, b, (((1,), (1,)), ((), ())), preferred_element_type=F32, **kw)


def _dot_tn(a, b, **kw):
    return lax.dot_general(a, b, (((0,), (0,)), ((), ())), preferred_element_type=F32, **kw)


def _rms(v, gain):
    return v * lax.rsqrt(jnp.mean(v * v, axis=-1, keepdims=True) + EPS) * gain


def _mod_kernel(c_ref, w_ref, b_ref, o_ref):
    o_ref[...] = _dot(_silu(c_ref[...]), w_ref[...], precision=HIGHEST) + b_ref[...]


def _modulation(cvec, w_mod, b_mod):
    rows, d = cvec.shape
    n = w_mod.shape[1]
    tn = 1024
    return pl.pallas_call(
        _mod_kernel,
        grid=(n // tn,),
        in_specs=[pl.BlockSpec((rows, d), lambda j: (0, 0)),
                  pl.BlockSpec((d, tn), lambda j: (0, j)),
                  pl.BlockSpec((1, tn), lambda j: (0, j))],
        out_specs=pl.BlockSpec((rows, tn), lambda j: (0, j)),
        out_shape=jax.ShapeDtypeStruct((rows, n), F32),
        compiler_params=_cparams(("arbitrary",)),
        name="adaln_modulation",
    )(cvec, w_mod, b_mod)


_PLAIN, _ROPE, _ROPE_SCALED, _SIGMOID = 0, 1, 2, 3


def _proj_kernel(x_ref, gain_ref, sc_ref, sh_ref, *refs, kinds, rope, n_w):
    w_refs = refs[:n_w]
    wg_ref, wgt_ref, prow_ref, pcol_ref, cos_ref, sin_ref = refs[n_w:n_w + 6]
    out_refs = refs[n_w + 6:]
    groups = [(w, c) for w in w_refs for c in range(0, w.shape[1], MIX_W)]
    n_feat = len(kinds)
    feat_refs = out_refs[:n_feat]
    cols_ref, grow_ref, gl_ref = out_refs[n_feat:]
    tm = x_ref.shape[0]

    x = x_ref[...]
    h = (_rms(x, gain_ref[...]) * (1.0 + sc_ref[...]) + sh_ref[...]).astype(BF16)

    if rope:
        cos2 = cos_ref[...]
        sin2 = sin_ref[...]

    for g, kind in enumerate(kinds):
        w_ref, c0 = groups[g]
        p = _dot(h, w_ref[:, c0:c0 + MIX_W])
        if kind == _SIGMOID:
            p = _sigmoid(p)
        elif kind in (_ROPE, _ROPE_SCALED) and rope:
            heads = []
            for hh in range(N_HEADS):
                t = p[:, hh * HEAD_DIM:(hh + 1) * HEAD_DIM]
                heads.append(t * cos2 + pltpu.roll(t, HEAD_DIM // 2, 1) * sin2)
            p = jnp.concatenate(heads, axis=1)
        if kind == _ROPE_SCALED:
            p = p * (HEAD_DIM ** -0.5)
        feat_refs[g][...] = p.astype(feat_refs[g].dtype)

    r_i = _iota((CHUNK, CHUNK), 0)
    c_i = _iota((CHUNK, CHUNK), 1)
    lower_incl = (c_i <= r_i).astype(BF16)
    upper_incl = (c_i >= r_i).astype(BF16)

    def prefix_rows(v):
        hi, lo = _split_bf16(v)
        return _dot(lower_incl, hi) + _dot(lower_incl, lo)

    def prefix_lanes(v):
        hi, lo = _split_bf16(v)
        return _dot(hi, upper_incl) + _dot(lo, upper_incl)

    nc = N_HEADS * 8
    pg = _dot(h, wg_ref[...])
    colt = _iota((tm, nc), 1) & 7
    la = jnp.where((colt == 2) | (colt == 3), 0.0, prow_ref[0:1, :] * _softplus(pg + prow_ref[1:2, :]))
    beta = _sigmoid(pg)
    colc = _iota((CHUNK, nc), 1) & 7
    fwd_col = (colc & 1) == 0
    for c in range(tm // CHUNK):
        sl = slice(c * CHUNK, (c + 1) * CHUNK)
        la_c = la[sl]
        pre = prefix_rows(la_c)
        suf = pre[CHUNK - 1:CHUNK, :] - pre + la_c
        g_c = jnp.where(fwd_col, pre, suf)
        rest = jnp.where(fwd_col, suf, pre) - la_c
        vals = jnp.where(colc < 2, g_c, jnp.where(colc < 4, beta[sl], jnp.where(
            colc < 6, jnp.exp(g_c), jnp.exp(rest))))
        for hh in range(N_HEADS):
            cols_ref[hh, sl, :] = vals[:, 8 * hh:8 * hh + 8]
        gl_ref[c:c + 1, :] = jnp.exp(g_c[0:1, :] + rest[0:1, :])

    pgt = _dot_nt(wgt_ref[...], h)
    rowq = _iota((N_HEADS * 8, tm), 0) & 7
    lat = jnp.where(rowq < 2, pcol_ref[:, 0:1] * _softplus(pgt + pcol_ref[:, 1:2]), 0.0)
    rowc = _iota((N_HEADS * 8, CHUNK), 0) & 7
    for c in range(tm // CHUNK):
        sl = slice(c * CHUNK, (c + 1) * CHUNK)
        lat_c = lat[:, sl]
        pre_t = prefix_lanes(lat_c)
        suf_t = pre_t[:, CHUNK - 1:CHUNK] - pre_t + lat_c
        grow_ref[:, :, sl] = jnp.where(rowc == 0, pre_t, suf_t).reshape(N_HEADS, 8, CHUNK)


def _in_projection(x, gain, scale, shift, w_mains, w_gab, w_gab_t, prow, pcol, cos2, sin2, kinds, rope):
    b, l, d = x.shape
    tm = min(PROJ_TM, l)
    tiles = l // tm
    n_chunk = tm // CHUNK
    feat_shapes = [jax.ShapeDtypeStruct((b, l, MIX_W), BF16) for _ in kinds]
    feat_specs = [pl.BlockSpec((None, tm, MIX_W), lambda i, j: (i, j, 0)) for _ in kinds]
    out_shape = feat_shapes + [jax.ShapeDtypeStruct((b, N_HEADS, l, 8), F32),
                               jax.ShapeDtypeStruct((b, N_HEADS, 8, l), F32),
                               jax.ShapeDtypeStruct((b, tiles, n_chunk, N_HEADS * 8), F32)]
    out_specs = feat_specs + [pl.BlockSpec((None, N_HEADS, tm, 8), lambda i, j: (i, 0, j, 0)),
                              pl.BlockSpec((None, N_HEADS, 8, tm), lambda i, j: (i, 0, 0, j)),
                              pl.BlockSpec((None, None, n_chunk, N_HEADS * 8), lambda i, j: (i, j, 0, 0))]
    const = functools.partial(pl.BlockSpec, pipeline_mode=pl.Buffered(1))
    in_specs = [
        pl.BlockSpec((None, tm, d), lambda i, j: (i, j, 0)),
        const((1, d), lambda i, j: (0, 0)),
        pl.BlockSpec((None, 1, d), lambda i, j: (i, 0, 0)),
        pl.BlockSpec((None, 1, d), lambda i, j: (i, 0, 0)),
    ] + [const(w.shape, lambda i, j: (0, 0)) for w in w_mains] + [
        const((d, N_HEADS * 8), lambda i, j: (0, 0)),
        const((N_HEADS * 8, d), lambda i, j: (0, 0)),
        const((2, N_HEADS * 8), lambda i, j: (0, 0)),
        const((N_HEADS * 8, 2), lambda i, j: (0, 0)),
        pl.BlockSpec((tm, HEAD_DIM), lambda i, j: (j, 0)),
        pl.BlockSpec((tm, HEAD_DIM), lambda i, j: (j, 0)),
    ]
    outs = pl.pallas_call(
        functools.partial(_proj_kernel, kinds=tuple(kinds), rope=rope, n_w=len(w_mains)),
        grid=(b, tiles),
        in_specs=in_specs,
        out_specs=out_specs,
        out_shape=out_shape,
        compiler_params=_cparams(("arbitrary", "arbitrary")),
        name="in_projection_rope" if rope else "in_projection_ctx",
    )(x, gain, scale, shift, *w_mains, w_gab, w_gab_t, prow, pcol, cos2, sin2)
    feats = outs[:len(kinds)]
    cols, grow, gl = outs[len(kinds):]
    return feats, (cols, grow, gl.reshape(b * (l // CHUNK), N_HEADS * 8))


def _unit_triangular_inverses(mats, lowers):
    r = _iota((CHUNK, CHUNK), 0)
    c = _iota((CHUNK, CHUNK), 1)
    eye = (r == c).astype(F32)
    invs = [eye - jnp.where((r >> 1) == (c >> 1), a, 0.0) for a in mats]
    for level in range(1, int(math.log2(CHUNK))):
        s = 1 << level
        mask = ((r >> (level + 1)) == (c >> (level + 1))) & ((r >> level) != (c >> level))
        invb = [inv.astype(BF16) for inv in invs]
        offs = [jnp.where(mask, a, 0.0).astype(BF16) for a in mats]
        if s < 8:
            half = [_dot(off, ib).astype(BF16) for off, ib in zip(offs, invb)]
            invs = [inv - _dot(ib, hf) for inv, ib, hf in zip(invs, invb, half)]
            continue
        def rows_of(x, lower, moving):
            first = s if (lower == moving) else 0
            return [x[g * 2 * s + first:g * 2 * s + first + s] for g in range(CHUNK // (2 * s))]

        half = [_dot(jnp.concatenate(rows_of(off, lo, True), axis=0), ib).astype(BF16)
                for off, ib, lo in zip(offs, invb, lowers)]
        zero = jnp.zeros((s, CHUNK), BF16)
        full = []
        for hf, lo in zip(half, lowers):
            pieces = []
            for g in range(CHUNK // (2 * s)):
                piece = hf[g * s:(g + 1) * s]
                pieces += [zero, piece] if lo else [piece, zero]
            full.append(jnp.concatenate(pieces, axis=0))
        corr = [_dot(jnp.concatenate(rows_of(ib, lo, True), axis=0), hf) for ib, hf, lo in zip(invb, full, lowers)]
        new = []
        for inv, cr, lo in zip(invs, corr, lowers):
            keep = rows_of(inv, lo, False)
            moved = [m - cr[g * s:(g + 1) * s] for g, m in enumerate(rows_of(inv, lo, True))]
            pieces = []
            for k, m in zip(keep, moved):
                pieces += [k, m] if lo else [m, k]
            new.append(jnp.concatenate(pieces, axis=0))
        invs = new
    return invs


def _mixer_kernel(*refs, seq_len, with_query):
    n_chunk = seq_len // CHUNK
    if with_query:
        (lg_ref, gl_ref, rk_ref, rv_ref, gk_ref, gv_ref, rq_ref, gq_ref, rg_ref, gz_ref,
         cols_ref, grow_ref, cq_ref, ck_ref, cv_ref, s0_ref, gnw_ref, rmsw_ref,
         yret_ref, ygdn_ref,
         qes, oret, ogdn, ubuf, wbuf, pbuf, nbuf, cbuf, kvbuf, rtile, state) = refs
    else:
        (lg_ref, gl_ref, rk_ref, rv_ref, gk_ref, gv_ref,
         cols_ref, grow_ref, ck_ref, cv_ref, s0_ref,
         sfin_ref,
         nbuf, cbuf, kvbuf, rtile, state) = refs
    bi = pl.program_id(0)
    hi = pl.program_id(1)

    row = _iota((CHUNK, CHUNK), 0)
    colm = _iota((CHUNK, CHUNK), 1)
    rowf = row.astype(F32)
    colf = colm.astype(F32)

    def conv_chunk(src_ref, w_ref, n):
        s = pl.multiple_of(n * CHUNK, CHUNK)
        x = src_ref[pl.ds(s, CHUNK), :].astype(F32)
        ps = pl.multiple_of(jnp.maximum(s - BF16_TILE_ROWS, 0), BF16_TILE_ROWS)
        ns = pl.multiple_of(jnp.minimum(s + CHUNK, seq_len - BF16_TILE_ROWS), BF16_TILE_ROWS)
        prev_row = src_ref[pl.ds(ps, BF16_TILE_ROWS), :].astype(F32)[BF16_TILE_ROWS - 1:BF16_TILE_ROWS, :]
        next_row = src_ref[pl.ds(ns, BF16_TILE_ROWS), :].astype(F32)[0:1, :]
        prev_row = prev_row * jnp.where(n > 0, 1.0, 0.0)
        next_row = next_row * jnp.where(n < n_chunk - 1, 1.0, 0.0)
        xp = jnp.where(row == 0, jnp.broadcast_to(prev_row, (CHUNK, HEAD_DIM)), pltpu.roll(x, 1, 0))
        xn = jnp.where(row == CHUNK - 1, jnp.broadcast_to(next_row, (CHUNK, HEAD_DIM)), pltpu.roll(x, CHUNK - 1, 0))
        return _silu(w_ref[0:1, :] * xp + w_ref[1:2, :] * x + w_ref[2:3, :] * xn)

    def l2n(v):
        return v * lax.rsqrt(jnp.sum(v * v, axis=-1, keepdims=True) + EPS)

    for d in range(2):
        lg = lg_ref[d, hi]
        if d == 0:
            dist, pos_q, pos_k = rowf - colf, rowf + 1.0, (CHUNK - 1.0) - rowf
        else:
            dist, pos_q, pos_k = colf - rowf, CHUNK - rowf, rowf
        rtile[4 * d + 0] = jnp.where(dist >= 0, jnp.exp(lg * jnp.maximum(dist, 0.0)), 0.0)
        rtile[4 * d + 1] = jnp.exp(lg * pos_q)
        rtile[4 * d + 2] = jnp.exp(lg * pos_k)
        rtile[4 * d + 3] = jnp.exp(lg * jnp.full((CHUNK, CHUNK), float(CHUNK), F32))

    state[...] = s0_ref[...]

    def bcast_col(cols, j):
        return jnp.broadcast_to(cols[:, j:j + 1], (CHUNK, CHUNK))

    pre_chunks = min(8, n_chunk)

    def prepass(m, carry):
        chunks = [m * pre_chunks + j for j in range(pre_chunks)]
        chunk_cs = [pl.ds(pl.multiple_of(n * CHUNK, CHUNK), CHUNK) for n in chunks]
        conv_k, conv_v, conv_q = [], [], []
        for n, cs in zip(chunks, chunk_cs):
            conv_k.append(l2n(conv_chunk(gk_ref, ck_ref, n)).astype(BF16))
            conv_v.append(conv_chunk(gv_ref, cv_ref, n).astype(BF16))
            if with_query:
                conv_q.append((l2n(conv_chunk(gq_ref, cq_ref, n)) * (HEAD_DIM ** -0.5)).astype(BF16))
                ogdn[cs, :] = jnp.zeros((CHUNK, HEAD_DIM), F32)

        jobs = [(j, d) for j in range(pre_chunks) for d in range(2)]
        css = [chunk_cs[j] for j, _ in jobs]
        slots = [d * n_chunk + chunks[j] for j, d in jobs]
        dirs = [d for _, d in jobs]
        ks = [rk_ref[cs, :] for cs in css]
        vs = [rv_ref[cs, :] for cs in css]
        if with_query:
            scs = [(_dot_nt(rq_ref[cs, :], k) * rtile[4 * d + 0]).astype(BF16) for cs, k, d in zip(css, ks, dirs)]
            outs = [_dot(sc, v) for sc, v in zip(scs, vs)]
            for j, cs in enumerate(chunk_cs):
                oret[cs, :] = outs[2 * j] + outs[2 * j + 1]
        kvs = [_dot_tn((k.astype(F32) * rtile[4 * d + 2]).astype(BF16), v) for k, v, d in zip(ks, vs, dirs)]
        for slot, kv in zip(slots, kvs):
            kvbuf[slot] = kv
        ks = [conv_k[j] for j, _ in jobs]
        vs = [conv_v[j] for j, _ in jobs]
        colss = [cols_ref[cs, :] for cs in css]
        betas = [bcast_col(cols, 2 + d) for cols, d in zip(colss, dirs)]
        incls = [(row >= colm) if d == 0 else (row <= colm) for d in dirs]
        stricts = [(row > colm) if d == 0 else (row < colm) for d in dirs]
        decs = [jnp.exp(jnp.where(incl, bcast_col(cols, d) - jnp.broadcast_to(grow_ref[d:d + 1, cs], (CHUNK, CHUNK)), 0.0))
                for cols, d, cs, incl in zip(colss, dirs, css, incls)]
        kks = [_dot_nt(k, k) for k in ks]
        mats = [kk * beta * jnp.where(strict, dec, 0.0) for kk, beta, strict, dec in zip(kks, betas, stricts, decs)]
        tinvs = [t.astype(BF16) for t in _unit_triangular_inverses(mats, [d == 0 for d in dirs])]
        rhs = [jnp.concatenate([(beta * v.astype(F32)).astype(BF16),
                                (beta * bcast_col(cols, 4 + d) * k.astype(F32)).astype(BF16)], axis=1)
               for beta, v, cols, d, k in zip(betas, vs, colss, dirs, ks)]
        uws = [_dot(t, r) for t, r in zip(tinvs, rhs)]
        us = [uw[:, :HEAD_DIM] for uw in uws]
        ws = [uw[:, HEAD_DIM:].astype(BF16) for uw in uws]
        kts = [(k.astype(F32) * bcast_col(cols, 6 + d)).astype(BF16) for k, cols, d in zip(ks, colss, dirs)]
        ncs = [_dot_tn(kt, uw.astype(BF16)) for kt, uw in zip(kts, uws)]
        for slot, nc_mat in zip(slots, ncs):
            cbuf[slot] = nc_mat[:, :HEAD_DIM]
            nbuf[slot] = nc_mat[:, HEAD_DIM:].astype(BF16)
        if with_query:
            for slot, u, w in zip(slots, us, ws):
                ubuf[slot] = u.astype(BF16)
                wbuf[slot] = w
            qks = [_dot_nt(conv_q[j], k) for (j, _), k in zip(jobs, ks)]
            for slot, qk, incl, dec in zip(slots, qks, incls, decs):
                pbuf[slot] = (qk * jnp.where(incl, dec, 0.0)).astype(BF16)
            for (j, d), cs, cols in zip(jobs, css, colss):
                qes[d, cs, :] = (conv_q[j].astype(F32) * bcast_col(cols, 4 + d)).astype(BF16)
        return carry

    lax.fori_loop(0, n_chunk // pre_chunks, prepass, 0)

    def finish(cs):
        ro = oret[cs, :]
        rc = ro - jnp.mean(ro, axis=-1, keepdims=True)
        ry = rc * lax.rsqrt(jnp.mean(rc * rc, axis=-1, keepdims=True) + EPS)
        yret_ref[cs, :] = (ry * gnw_ref[...] * _silu(rg_ref[cs, :].astype(F32))).astype(BF16)
        go = ogdn[cs, :]
        gy = go * lax.rsqrt(jnp.mean(go * go, axis=-1, keepdims=True) + EPS)
        ygdn_ref[cs, :] = (gy * rmsw_ref[...] * _silu(gz_ref[cs, :].astype(F32))).astype(BF16)

    def chunk_ids(n):
        nds = [n, n_chunk - 1 - n]
        css = [pl.ds(nd * CHUNK if isinstance(nd, int) else pl.multiple_of(nd * CHUNK, CHUNK), CHUNK) for nd in nds]
        slots = [d * n_chunk + nd for d, nd in enumerate(nds)]
        return nds, css, slots

    def late_outputs(n, vnb, finishing):
        _, css, slots = chunk_ids(n)
        for d in range(2):
            ogdn[css[d], :] += _dot(pbuf[slots[d]], vnb[d])
            if finishing:
                finish(css[d])

    def scan_step(n, vnb_prev, has_prev, finish_prev):
        nds, css, slots = chunk_ids(n)
        ret_st = [state[d] for d in range(2)]
        gdn_st = [state[2 + d] for d in range(2)]
        gdn_stb = [st.astype(BF16) for st in gdn_st]
        shrink = [_dot(nbuf[slot], stb) for slot, stb in zip(slots, gdn_stb)]
        for d in range(2):
            state[2 + d] = gl_ref[bi * n_chunk + nds[d], 8 * hi + d] * gdn_st[d] - shrink[d] + cbuf[slots[d]]
            state[d] = rtile[4 * d + 3] * ret_st[d] + kvbuf[slots[d]]
        if not with_query:
            return vnb_prev
        vnb = tuple((ubuf[slot].astype(F32) - _dot(wbuf[slot], stb)).astype(BF16)
                    for slot, stb in zip(slots, gdn_stb))
        for d in range(2):
            oret[css[d], :] += _dot(rq_ref[css[d], :], ret_st[d].astype(BF16)) * rtile[4 * d + 1]
            ogdn[css[d], :] += _dot(qes[d, css[d], :], gdn_stb[d])
        if has_prev:
            late_outputs(n - 1, vnb_prev, finish_prev)
        return vnb

    if with_query:
        half = n_chunk // 2
        zero = jnp.zeros((CHUNK, HEAD_DIM), BF16)
        vnb = scan_step(0, (zero, zero), has_prev=False, finish_prev=False)
        vnb = lax.fori_loop(1, half + 1, functools.partial(scan_step, has_prev=True, finish_prev=False), vnb)
        vnb = lax.fori_loop(half + 1, n_chunk, functools.partial(scan_step, has_prev=True, finish_prev=True), vnb)
        late_outputs(n_chunk - 1, vnb, True)
    else:
        lax.fori_loop(0, n_chunk, functools.partial(scan_step, has_prev=False, finish_prev=False), 0)
        sfin_ref[...] = state[...]


def _mixers(lg, gl, feats, dec, conv, s0, gnw, rmsw, with_query):
    cols, grow = dec
    b, l, _ = feats["rk"].shape
    n_slot = 2 * (l // CHUNK)
    assert (l // CHUNK) % min(8, l // CHUNK) == 0
    smem = pl.BlockSpec(memory_space=pltpu.SMEM)
    head = pl.BlockSpec((None, l, HEAD_DIM), lambda i, j: (i, 0, j))
    colspec = pl.BlockSpec((None, None, l, 8), lambda i, j: (i, j, 0, 0))
    rowspec = pl.BlockSpec((None, None, 8, l), lambda i, j: (i, j, 0, 0))
    tile_f32 = pltpu.VMEM((n_slot, CHUNK, CHUNK), F32)
    tile_bf16 = pltpu.VMEM((n_slot, CHUNK, CHUNK), BF16)
    st_spec = pl.BlockSpec((None, None, 4, HEAD_DIM, HEAD_DIM), lambda i, j: (i, j, 0, 0, 0))

    def conv_spec(which):
        return pl.BlockSpec((SHORT_CONV, HEAD_DIM), lambda i, j, w=which: (0, w * N_HEADS + j))

    def gain_spec():
        return pl.BlockSpec((1, HEAD_DIM), lambda i, j: (0, j))

    if with_query:
        args = [lg, gl, feats["rk"], feats["rv"], feats["gk"], feats["gv"], feats["rq"], feats["gq"], feats["rg"],
                feats["gz"], cols, grow, conv, conv, conv, s0, gnw, rmsw]
        in_specs = [smem, smem] + [head] * 8 + [colspec, rowspec, conv_spec(0), conv_spec(1), conv_spec(2),
                                                st_spec, gain_spec(), gain_spec()]
        out_shape = [jax.ShapeDtypeStruct((b, l, MIX_W), BF16)] * 2
        out_specs = [head, head]
        scratch = ([pltpu.VMEM((2, l, HEAD_DIM), BF16)] + [pltpu.VMEM((l, HEAD_DIM), F32)] * 2
                   + [tile_bf16, tile_bf16, tile_bf16, tile_bf16, tile_f32, tile_f32])
    else:
        args = [lg, gl, feats["rk"], feats["rv"], feats["gk"], feats["gv"], cols, grow, conv, conv, s0]
        in_specs = [smem, smem] + [head] * 4 + [colspec, rowspec, conv_spec(1), conv_spec(2), st_spec]
        out_shape = jax.ShapeDtypeStruct((b, N_HEADS, 4, HEAD_DIM, HEAD_DIM), F32)
        out_specs = st_spec
        scratch = [tile_bf16, tile_f32, tile_f32]
    scratch = scratch + [pltpu.VMEM((8, CHUNK, CHUNK), F32), pltpu.VMEM((4, HEAD_DIM, HEAD_DIM), F32)]
    return pl.pallas_call(
        functools.partial(_mixer_kernel, seq_len=l, with_query=with_query),
        grid=(b, N_HEADS),
        in_specs=in_specs,
        out_specs=out_specs,
        out_shape=out_shape,
        scratch_shapes=scratch,
        compiler_params=_cparams(("arbitrary", "arbitrary")),
        name="mixers_latent" if with_query else "mixers_context",
    )(*args)


def _split_bf16(v):
    hi = v.astype(BF16)
    return hi, (v - hi.astype(F32)).astype(BF16)


def _route(h2, wrt_ref, bias_ref, cand_ref, sel_ref, wd_ref):
    tm = h2.shape[0]
    h_hi, h_lo = _split_bf16(h2)
    w_hi, w_lo = _split_bf16(wrt_ref[...])
    logits = _dot_nt(w_hi, h_hi) + (_dot_nt(w_hi, h_lo) + _dot_nt(w_lo, h_hi))
    scores = _sigmoid(logits)
    biased = scores + bias_ref[...]
    neg_inf = float("-inf")
    sub = _iota((GROUP_SIZE, tm), 0).astype(F32)
    group_score = []
    for g in range(N_GROUPS):
        blk = biased[g * GROUP_SIZE:(g + 1) * GROUP_SIZE, :]
        m1 = jnp.max(blk, axis=0, keepdims=True)
        first = jnp.min(jnp.where(blk == m1, sub, float(GROUP_SIZE)), axis=0, keepdims=True)
        m2 = jnp.max(jnp.where(sub == first, neg_inf, blk), axis=0, keepdims=True)
        group_score.append(m1 + m2)
    for g in range(N_GROUPS):
        ahead = jnp.zeros((1, tm), I32)
        for g2 in range(N_GROUPS):
            if g2 == g:
                continue
            before = (group_score[g2] > group_score[g])
            if g2 < g:
                before = before | (group_score[g2] == group_score[g])
            ahead = ahead + before.astype(I32)
        keep = jnp.broadcast_to(ahead, (GROUP_SIZE, tm)) < TOPK_GROUPS
        cand_ref[g * GROUP_SIZE:(g + 1) * GROUP_SIZE, :] = jnp.where(
            keep, biased[g * GROUP_SIZE:(g + 1) * GROUP_SIZE, :], neg_inf)
    work = cand_ref[...]
    eidx = _iota((N_EXPERTS, tm), 0).astype(F32)
    sel = jnp.zeros((N_EXPERTS, tm), jnp.bool_)
    for _ in range(TOP_K):
        best = jnp.max(work, axis=0, keepdims=True)
        first = jnp.min(jnp.where(work == best, eidx, float(N_EXPERTS)), axis=0, keepdims=True)
        pick = eidx == first
        sel = sel | pick
        work = jnp.where(pick, neg_inf, work)
    picked = jnp.where(sel, scores, 0.0)
    wsum = jnp.sum(picked, axis=0, keepdims=True)
    sel_ref[...] = sel.astype(F32)
    wd_ref[...] = picked / wsum * ROUTED_SCALE


def _mixout_kernel(x_ref, yr_ref, yg_ref, g0_ref, g1_ref, g2_ref, g3_ref, wr_ref, wg_ref, wo_ref,
                   npost_ref, gate1_ref, nffn_ref, sc2_ref, sh2_ref, wrt_ref, bias_ref,
                   x1_ref, h2_ref, sel_ref, wd_ref, cand_ref):
    r = _dot(yr_ref[...], wr_ref[...])
    g = _dot(yg_ref[...], wg_ref[...])
    half = r.shape[1] // 2
    merged = jnp.concatenate(
        [g0_ref[...].astype(F32) * r[:, :half] + g2_ref[...].astype(F32) * g[:, :half],
         g1_ref[...].astype(F32) * r[:, half:] + g3_ref[...].astype(F32) * g[:, half:]], axis=1)
    mo = _dot(merged.astype(BF16), wo_ref[...])
    x1 = x_ref[...] + gate1_ref[...] * _rms(mo, npost_ref[...])
    x1_ref[...] = x1
    h2 = _rms(x1, nffn_ref[...]) * (1.0 + sc2_ref[...]) + sh2_ref[...]
    h2_ref[...] = h2.astype(BF16)
    _route(h2, wrt_ref, bias_ref, cand_ref, sel_ref, wd_ref)


def _mixer_out(x, yr, yg, gates, wr, wg, wo, npost, gate1, nffn, sc2, sh2, wrt, bias):
    b, l, d = x.shape
    tm = min(MIXOUT_TM, l)
    tiles = l // tm
    tok = lambda w: pl.BlockSpec((None, tm, w), lambda i, j: (i, j, 0))
    const = functools.partial(pl.BlockSpec, pipeline_mode=pl.Buffered(1))
    vec = lambda: const((1, d), lambda i, j: (0, 0))
    bvec = lambda: pl.BlockSpec((None, 1, d), lambda i, j: (i, 0, 0))
    emap = pl.BlockSpec((N_EXPERTS, tm), lambda i, j: (0, i * tiles + j))
    in_specs = ([tok(d), tok(MIX_W), tok(MIX_W)] + [tok(MIX_W)] * 4
                + [const((MIX_W, d), lambda i, j: (0, 0)), const((MIX_W, d), lambda i, j: (0, 0)),
                   const((d, d), lambda i, j: (0, 0)),
                   vec(), bvec(), vec(), bvec(), bvec(),
                   const((N_EXPERTS, d), lambda i, j: (0, 0)), const((N_EXPERTS, 1), lambda i, j: (0, 0))])
    return pl.pallas_call(
        _mixout_kernel,
        grid=(b, tiles),
        in_specs=in_specs,
        out_specs=[tok(d), tok(d), emap, emap],
        out_shape=[jax.ShapeDtypeStruct((b, l, d), F32), jax.ShapeDtypeStruct((b, l, d), BF16),
                   jax.ShapeDtypeStruct((N_EXPERTS, b * l), F32), jax.ShapeDtypeStruct((N_EXPERTS, b * l), F32)],
        scratch_shapes=[pltpu.VMEM((N_EXPERTS, tm), F32)],
        compiler_params=_cparams(("arbitrary", "arbitrary")),
        name="mixer_out_router",
    )(x, yr, yg, *gates, wr, wg, wo, npost, gate1, nffn, sc2, sh2, wrt, bias)


def _tile_positions(sel):
    tm = sel.shape[1]
    selb = sel.astype(BF16)
    earlier = (_iota((tm, tm), 0) < _iota((tm, tm), 1)).astype(BF16)
    rank = _dot(selb, earlier)
    cnt = _dot(selb, jnp.ones((tm, tm), BF16))
    nmb = jnp.floor((cnt + (MB_ROWS - 1)) * (1.0 / MB_ROWS))
    below = (_iota((N_EXPERTS, N_EXPERTS), 1) < _iota((N_EXPERTS, N_EXPERTS), 0)).astype(BF16)
    offmb = _dot(below, nmb.astype(BF16))
    rank_hi = jnp.floor(rank * (1.0 / MB_ROWS))
    hi = jnp.where(sel > 0.0, offmb + rank_hi, 255.0)
    lo = rank - rank_hi * MB_ROWS
    return hi, lo, cnt, offmb, rank


def _dispatch_kernel(h2_ref, sel_ref, xs_ref, cnt_ref):
    last = pl.num_programs(0) - 1

    @pl.when(pl.program_id(0) < last)
    def _():
        _dispatch_tile(h2_ref, sel_ref, xs_ref, cnt_ref)

    @pl.when(pl.program_id(0) == last)
    def _():
        xs_ref[...] = jnp.zeros_like(xs_ref)


def _dispatch_tile(h2_ref, sel_ref, xs_ref, cnt_ref):
    tm = h2_ref.shape[0]
    sel = sel_ref[...]
    hi, lo, _, _, _ = _tile_positions(sel)
    hib = hi.astype(BF16)
    lob = lo.astype(BF16)
    selb = sel.astype(BF16)
    cnt_row = _dot_nt(jnp.ones((8, tm), BF16), selb)
    nmb_row = jnp.floor((cnt_row + (MB_ROWS - 1)) * (1.0 / MB_ROWS))
    before = (_iota((N_EXPERTS, N_EXPERTS), 0) < _iota((N_EXPERTS, N_EXPERTS), 1)).astype(BF16)
    start_row = _dot(nmb_row.astype(BF16), before) * MB_ROWS
    cnt_ref[...] = cnt_row
    start = start_row[0:1, :]
    end = start + cnt_row[0:1, :]
    x = h2_ref[...]
    rch = 256
    def one_hot(rc):
        r_e = (rc * rch + _iota((rch, N_EXPERTS), 0)).astype(F32)
        owner = ((r_e >= start) & (r_e < end)).astype(BF16)
        r_t = rc * rch + _iota((rch, tm), 0)
        match = ((_dot(owner, hib) == (r_t >> _MB_SHIFT).astype(F32))
                 & (_dot(owner, lob) == (r_t & (MB_ROWS - 1)).astype(F32)))
        return match.astype(BF16)

    n_rc = TILE_ROWS // rch
    nxt = one_hot(0)
    for rc in range(n_rc):
        cur = nxt
        if rc + 1 < n_rc:
            nxt = one_hot(rc + 1)
        xs_ref[rc * rch:(rc + 1) * rch, :] = _dot(cur, x).astype(BF16)


def _dispatch(h2, sel_t):
    t, d = h2.shape
    tm = MOE_TM
    nt = t // tm
    return pl.pallas_call(
        _dispatch_kernel,
        grid=(nt + 1,),
        in_specs=[pl.BlockSpec((tm, d), lambda i: (jnp.minimum(i, nt - 1), 0)),
                  pl.BlockSpec((N_EXPERTS, tm), lambda i: (0, jnp.minimum(i, nt - 1)))],
        out_specs=[pl.BlockSpec((TILE_ROWS, d), lambda i: (i, 0)),
                   pl.BlockSpec((None, 8, N_EXPERTS), lambda i: (jnp.minimum(i, nt - 1), 0, 0))],
        out_shape=[jax.ShapeDtypeStruct((nt * TILE_ROWS + (EXP_OUT + 1) * EXP_BM, d), BF16),
                   jax.ShapeDtypeStruct((nt, 8, N_EXPERTS), F32)],
        compiler_params=_cparams(("arbitrary",)),
        name="moe_dispatch",
    )(h2, sel_t)


def _expert_plan(cnt, n_blk):
    nt = cnt.shape[0]
    nmb = (cnt + (MB_ROWS - 1)) // MB_ROWS
    offmb = jnp.cumsum(nmb, axis=1) - nmb
    per_e = nmb.T
    incl = jnp.cumsum(per_e, axis=1)
    excl = incl - per_e
    tot = incl[:, -1]
    nb = (tot + (EXP_MB - 1)) // EXP_MB
    bend = jnp.cumsum(nb)
    bstart = bend - nb
    b = jnp.arange(n_blk, dtype=I32)
    valid = b < bend[-1]
    last = jnp.maximum(bend[-1] - 1, 0)
    bq = jnp.where(valid, b, last)
    blk_e = jnp.minimum(jnp.sum((bend[None, :] <= bq[:, None]).astype(I32), axis=1), N_EXPERTS - 1)
    onehot_e = (blk_e[:, None] == jnp.arange(N_EXPERTS, dtype=I32)[None, :]).astype(I32)
    pick = lambda table: jnp.dot(onehot_e.astype(F32), table.astype(F32), precision=HIGHEST).astype(I32)
    bstart_b = jnp.sum(onehot_e * bstart[None, :], axis=1)
    tot_b = jnp.sum(onehot_e * tot[None, :], axis=1)
    p0 = (bq - bstart_b) * EXP_MB
    blk_nmb = jnp.where(valid, jnp.clip(tot_b - p0, 0, EXP_MB), 0).astype(I32)
    p = p0[:, None] + jnp.arange(EXP_MB, dtype=I32)[None, :]
    incl_b = pick(incl)
    passed = (incl_b[:, None, :] <= p[:, :, None]).astype(I32)
    tile = jnp.minimum(jnp.sum(passed, axis=2), nt - 1)
    shift_b = pick(offmb.T - excl)
    step_b = jnp.concatenate([shift_b[:, 1:] - shift_b[:, :-1], jnp.zeros_like(shift_b[:, :1])], axis=1)
    shift = shift_b[:, :1] + jnp.sum(passed * step_b[:, None, :], axis=2)
    where = tile * _TILE_MB + p + shift
    j = jnp.arange(EXP_MB, dtype=I32)[None, :]
    used = j < blk_nmb[:, None]
    spare = nt * _TILE_MB + (b[:, None] % EXP_OUT) * EXP_MB + j
    zeros_mb = nt * _TILE_MB + EXP_OUT * EXP_MB + j
    src = jnp.where(used, where, jnp.where(valid[:, None], where[:, :1], zeros_mb))
    dst = jnp.where(used, where, spare)
    dst = jnp.concatenate([nt * _TILE_MB + (EXP_OUT - 1) * EXP_MB + j, dst], axis=0)
    return bstart.astype(I32), nb.astype(I32), src.reshape(-1).astype(I32), dst.reshape(-1).astype(I32)


def _expert_kernel(bstart_ref, nb_ref, src_ref, dst_ref, xs_hbm, wg_ref, wu_ref, wd_ref, ys_hbm,
                   xbuf, ybuf, wgub, wdb, in_sem, out_sem):
    e = pl.program_id(0)
    n_in = xbuf.shape[0]
    n_out = ybuf.shape[0]

    def gather_copy(blk, slot, j):
        rows = pl.ds(pl.multiple_of(src_ref[blk * EXP_MB + j] * MB_ROWS, MB_ROWS), MB_ROWS)
        return pltpu.make_async_copy(xs_hbm.at[rows, :], xbuf.at[slot, j * MB_ROWS:(j + 1) * MB_ROWS, :],
                                     in_sem.at[slot])

    def scatter_copy(blk, slot, j):
        rows = pl.ds(pl.multiple_of(dst_ref[(blk + 1) * EXP_MB + j] * MB_ROWS, MB_ROWS), MB_ROWS)
        return pltpu.make_async_copy(ybuf.at[slot, j * MB_ROWS:(j + 1) * MB_ROWS, :], ys_hbm.at[rows, :],
                                     out_sem.at[slot])

    def start_gather(blk):
        for j in range(EXP_MB):
            gather_copy(blk, lax.rem(blk, n_in), j).start()

    def wait_gather(blk):
        for j in range(EXP_MB):
            gather_copy(blk, lax.rem(blk, n_in), j).wait()

    def start_scatter(blk):
        for j in range(EXP_MB):
            scatter_copy(blk, lax.rem(blk + n_out, n_out), j).start()

    def wait_scatter(slot):
        for j in range(EXP_MB):
            scatter_copy(0, slot, j).wait()

    @pl.when(e == 0)
    def _():
        ybuf[...] = jnp.zeros_like(ybuf)
        for ahead in range(EXP_LEAD):
            start_gather(ahead)
        spare_row0 = ys_hbm.shape[0] - (n_out + 1) * EXP_BM
        for slot in range(n_out - 1):
            for j in range(EXP_MB):
                spare = pl.ds(spare_row0 + (slot * EXP_MB + j) * MB_ROWS, MB_ROWS)
                pltpu.make_async_copy(ybuf.at[slot, j * MB_ROWS:(j + 1) * MB_ROWS, :], ys_hbm.at[spare, :],
                                      out_sem.at[slot]).start()

    de = wg_ref.shape[1]
    wgub[:, :de] = wg_ref[...].astype(BF16)
    wgub[:, de:] = wu_ref[...].astype(BF16)
    wdb[...] = wd_ref[...].astype(BF16)

    def block(b, carry):
        slot = lax.rem(b, n_out)
        wait_gather(b)
        wait_scatter(slot)
        x = xbuf[lax.rem(b, n_in)]
        gu = _dot(x, wgub[...])
        act = (_silu(gu[:, :de]) * gu[:, de:]).astype(BF16)
        start_gather(b + EXP_LEAD)
        start_scatter(b - 1)
        ybuf[slot] = _dot(act, wdb[...]).astype(BF16)
        return carry

    first = bstart_ref[e]
    lax.fori_loop(first, first + nb_ref[e], block, 0)

    @pl.when(e == pl.num_programs(0) - 1)
    def _():
        total = first + nb_ref[e]
        start_scatter(total - 1)
        for ahead in range(EXP_LEAD):
            wait_gather(total + ahead)
        for slot in range(n_out):
            wait_scatter(slot)


def _expert_ffn(xs, bstart, nb, src, dst, wg, wu, wd):
    rows, d = xs.shape
    de = wg.shape[2]
    grid_spec = pltpu.PrefetchScalarGridSpec(
        num_scalar_prefetch=4,
        grid=(wg.shape[0],),
        in_specs=[pl.BlockSpec(memory_space=pl.ANY),
                  pl.BlockSpec((None, d, de), lambda i, bs, bn, sr, ds: (i, 0, 0)),
                  pl.BlockSpec((None, d, de), lambda i, bs, bn, sr, ds: (i, 0, 0)),
                  pl.BlockSpec((None, de, d), lambda i, bs, bn, sr, ds: (i, 0, 0))],
        out_specs=pl.BlockSpec(memory_space=pl.ANY),
        scratch_shapes=[pltpu.VMEM((EXP_LEAD + 1, EXP_BM, d), BF16), pltpu.VMEM((EXP_OUT, EXP_BM, d), BF16),
                        pltpu.VMEM((d, 2 * de), BF16), pltpu.VMEM((de, d), BF16),
                        pltpu.SemaphoreType.DMA((EXP_LEAD + 1,)), pltpu.SemaphoreType.DMA((EXP_OUT,))],
    )
    return pl.pallas_call(
        _expert_kernel,
        grid_spec=grid_spec,
        out_shape=jax.ShapeDtypeStruct((rows, d), BF16),
        input_output_aliases={4: 0},
        compiler_params=_cparams(("arbitrary",)),
        name="moe_experts",
    )(bstart, nb, src, dst, xs, wg, wu, wd)


def _combine_kernel(ys_ref, sel_ref, wd_ref, h2_ref, x1_ref, wsg_ref, wsu_ref, wsd_ref, npost_ref, gate2_ref,
                    o_ref, c_ref):
    tm = h2_ref.shape[0]
    sel = sel_ref[...]
    _, _, cnt, offmb, rank = _tile_positions(sel)
    no_rank = 1.5 * tm
    lhs = jnp.concatenate([jnp.where(sel > 0.0, rank, no_rank).T, wd_ref[...].T], axis=1).astype(BF16)
    start = offmb[:, :LANES] * MB_ROWS
    end = start + cnt[:, :LANES]
    zeros = jnp.zeros((N_EXPERTS, LANES), BF16)
    for cc in range(TILE_ROWS // LANES):
        r_e = (cc * LANES + _iota((N_EXPERTS, LANES), 1)).astype(F32)
        owned = (r_e >= start) & (r_e < end)
        owner = owned.astype(BF16)
        local = r_e[0:1, :] - jnp.sum(jnp.where(owned, start, 0.0), axis=0, keepdims=True)
        rhs = jnp.concatenate([jnp.concatenate([owner, zeros], axis=1),
                               jnp.concatenate([zeros, owner], axis=1)], axis=0)
        got = _dot(lhs, rhs)
        c_ref[:, cc * LANES:(cc + 1) * LANES] = jnp.where(got[:, :LANES] == local, got[:, LANES:], 0.0).astype(BF16)
    routed = _dot(c_ref[...], ys_ref[...])
    h2 = h2_ref[...]
    shared = _dot((_silu(_dot(h2, wsg_ref[...])) * _dot(h2, wsu_ref[...])).astype(BF16), wsd_ref[...])
    o_ref[...] = x1_ref[...] + gate2_ref[...] * _rms(routed + shared, npost_ref[...])


def _combine(ys, sel_t, wd_t, h2, x1, wsg, wsu, wsd, npost, gate2, seq_len):
    t, d = h2.shape
    tm = MOE_TM
    nt = t // tm
    per_seq = seq_len // tm
    ds = wsg.shape[1]
    const = functools.partial(pl.BlockSpec, pipeline_mode=pl.Buffered(1))
    emap = pl.BlockSpec((N_EXPERTS, tm), lambda i: (0, i))
    return pl.pallas_call(
        _combine_kernel,
        grid=(nt,),
        in_specs=[pl.BlockSpec((TILE_ROWS, d), lambda i: (i, 0)), emap, emap,
                  pl.BlockSpec((tm, d), lambda i: (i, 0)), pl.BlockSpec((tm, d), lambda i: (i, 0)),
                  const((d, ds), lambda i: (0, 0)), const((d, ds), lambda i: (0, 0)), const((ds, d), lambda i: (0, 0)),
                  const((1, d), lambda i: (0, 0)),
                  pl.BlockSpec((None, 1, d), lambda i: (i // per_seq, 0, 0))],
        out_specs=pl.BlockSpec((tm, d), lambda i: (i, 0)),
        out_shape=jax.ShapeDtypeStruct((t, d), F32),
        scratch_shapes=[pltpu.VMEM((tm, TILE_ROWS), BF16)],
        compiler_params=_cparams(("arbitrary",)),
        name="moe_combine",
    )(ys, sel_t, wd_t, h2, x1, wsg, wsu, wsd, npost, gate2)


def _rope_tables(n):
    rows = n // GRID_W
    pos_r = jnp.repeat(jnp.arange(rows, dtype=F32), GRID_W)
    pos_c = jnp.tile(jnp.arange(GRID_W, dtype=F32), rows)
    n_freq = HEAD_DIM // 4
    inv = ROPE_BASE ** (-jnp.arange(n_freq, dtype=F32) / n_freq)
    ang = jnp.concatenate([pos_r[:, None] * inv, pos_c[:, None] * inv], axis=-1)
    cos, sin = jnp.cos(ang), jnp.sin(ang)
    return jnp.concatenate([cos, cos], axis=-1), jnp.concatenate([-sin, sin], axis=-1)


def kernel(x, c, ctx, c_ctx, w_mod, b_mod, norm_mix_pre, norm_mix_post, norm_ffn_pre, norm_ffn_post, w_in, gdn_conv, ret_log_decay, gdn_a_log, gdn_dt_bias, ret_gn_w, gdn_norm_w, w_ret_out, w_gdn_out, w_o, w_router, router_bias, w_gate, w_up, w_down, w_sh_gate, w_sh_up, w_sh_down):
    b, n, d = x.shape
    depth = w_mod.shape[0]
    assert depth == 1, "single-layer block"
    assert all(n % min(tile, n) == 0 for tile in (PROJ_TM, MIXOUT_TM, MOE_TM)) and n % MOE_TM == 0
    assert ctx.shape[1] % CHUNK == 0 and (EXP_OUT + 1) * EXP_BM <= TILE_ROWS
    assert _TILE_MB < 255

    rows = -(-(b + 1) // 8) * 8
    cvec = jnp.zeros((rows, d), F32).at[:b].set(c).at[b].set(c_ctx)
    mod = _modulation(cvec, w_mod[0], b_mod[0][None, :])
    sh1, sc1, g1, sh2, sc2, g2 = [mod[:b, k * d:(k + 1) * d][:, None, :] for k in range(6)]
    ctx_shift = jnp.broadcast_to(mod[b, 0:d][None, None, :], (b, 1, d))
    ctx_scale = jnp.broadcast_to(mod[b, d:2 * d][None, None, :], (b, 1, d))

    w_in0 = w_in[0]
    n_main = 4 * MIX_W
    w_state = w_in0[:, :n_main].astype(BF16)
    slots = ((0, 0), (0, 1), (1, 0), (1, 1), (0, 0), (0, 1), (0, 0), (0, 1))
    gab_cols = n_main + jnp.array([[ab * 2 * N_HEADS + dr * N_HEADS + hh for ab, dr in slots]
                                   for hh in range(N_HEADS)], I32)
    w_gab = w_in0[:, gab_cols.reshape(-1)].astype(BF16)
    n_state = n_main + N_GAB
    w_query = w_in0[:, n_state:].astype(BF16)
    a_coef = -jnp.exp(gdn_a_log[0].astype(F32))
    dtb = gdn_dt_bias[0].astype(F32)
    is_alpha = jnp.array([ab == 0 for ab, _ in slots])
    dirs = jnp.array([dr for _, dr in slots], I32)
    prow = jnp.stack([jnp.where(is_alpha[None, :], a_coef.T[:, dirs], 0.0).reshape(-1),
                      jnp.where(is_alpha[None, :], dtb.T[:, dirs], 0.0).reshape(-1)], axis=0)
    first2 = jnp.tile(jnp.arange(8) < 2, N_HEADS)
    w_gab_t = w_gab.T * first2[:, None].astype(BF16)
    pcol = prow.T * first2[:, None]
    gain_mix = norm_mix_pre[0][None, :]
    cos2, sin2 = _rope_tables(n)

    state_kinds = (_ROPE_SCALED, _PLAIN, _PLAIN, _PLAIN)
    query_kinds = (_ROPE, _PLAIN, _PLAIN, _PLAIN, _SIGMOID, _SIGMOID, _SIGMOID, _SIGMOID)
    lg = ret_log_decay[0].astype(F32)
    conv = gdn_conv[0].astype(F32)

    lc = ctx.shape[1]
    cfe, (ccols, cgrow, cgl) = _in_projection(
        ctx, gain_mix, ctx_scale, ctx_shift, (w_state,), w_gab, w_gab_t, prow, pcol,
        cos2[:lc], sin2[:lc], state_kinds, rope=False)
    cfeats = dict(zip(("rk", "rv", "gk", "gv"), cfe))
    zero_state = jnp.zeros((b, N_HEADS, 4, HEAD_DIM, HEAD_DIM), F32)
    init = _mixers(lg, cgl, cfeats, (ccols, cgrow), conv, zero_state, None, None, with_query=False)

    fe, (cols, grow, gl) = _in_projection(
        x, gain_mix, sc1, sh1, (w_state, w_query), w_gab, w_gab_t, prow, pcol, cos2, sin2,
        state_kinds + query_kinds, rope=True)
    feats = dict(zip(("rk", "rv", "gk", "gv", "rq", "rg", "gq", "gz"), fe[:8]))
    gates = fe[8:]
    y_ret, y_gdn = _mixers(lg, gl, feats, (cols, grow), conv, init,
                           ret_gn_w[0][None, :], gdn_norm_w[0][None, :], with_query=True)
    x1, h2, sel_t, wd_t = _mixer_out(
        x, y_ret, y_gdn, gates, w_ret_out[0].astype(BF16), w_gdn_out[0].astype(BF16), w_o[0].astype(BF16),
        norm_mix_post[0][None, :], g1, norm_ffn_pre[0][None, :], sc2, sh2,
        w_router[0].T.astype(F32), router_bias[0].astype(F32)[:, None])

    t = b * n
    h2f = h2.reshape(t, d)
    xs, cnt = _dispatch(h2f, sel_t)
    nt = t // MOE_TM
    n_blk = nt * _TILE_MB // EXP_MB + N_EXPERTS + EXP_LEAD
    bstart, nb, src, dst = _expert_plan(cnt[:, 0, :].astype(I32), n_blk)
    ys = _expert_ffn(xs, bstart, nb, src, dst, w_gate[0], w_up[0], w_down[0])
    out = _combine(ys, sel_t, wd_t, h2f, x1.reshape(t, d), w_sh_gate[0].astype(BF16), w_sh_up[0].astype(BF16),
                   w_sh_down[0].astype(BF16), norm_ffn_post[0][None, :], g2, n)
    return out.reshape(b, n, d)
```

```python
import functools
import math

import jax
import jax.numpy as jnp
from jax import lax
from jax.experimental import pallas as pl
from jax.experimental.pallas import tpu as pltpu

F32 = jnp.float32
BF16 = jnp.bfloat16
I32 = jnp.int32
HIGHEST = lax.Precision.HIGHEST

N_HEADS = 4
HEAD_DIM = 128
MIX_W = N_HEADS * HEAD_DIM
CHUNK = 128
SHORT_CONV = 3
ROPE_BASE = 10000.0
GRID_W = 64
N_EXPERTS = 64
TOP_K = 8
N_GROUPS = 8
TOPK_GROUPS = 4
GROUP_SIZE = N_EXPERTS // N_GROUPS
ROUTED_SCALE = 2.5
EPS = 1e-6
N_GAB = 4 * N_HEADS

LANES = 128
BF16_TILE_ROWS = 16
VMEM_LIMIT_BYTES = 56 * 1024 * 1024

PROJ_TM = 512
MIXOUT_TM = 1024
MOE_TM = 256
MB_ROWS = BF16_TILE_ROWS
_MB_SHIFT = MB_ROWS.bit_length() - 1
_TILE_MB = -(-(TOP_K * MOE_TM // MB_ROWS + N_EXPERTS * (MB_ROWS - 1) // MB_ROWS + 1) // 8) * 8
TILE_ROWS = _TILE_MB * MB_ROWS
TAIL_CHUNKS = 2
EXP_BM = 512
EXP_MB = EXP_BM // MB_ROWS
EXP_LEAD = 3
EXP_OUT = 3


def _cparams(sem):
    return pltpu.CompilerParams(dimension_semantics=sem, vmem_limit_bytes=VMEM_LIMIT_BYTES)


def _sigmoid(v):
    return 0.5 * jnp.tanh(0.5 * v) + 0.5


def _silu(v):
    return v * _sigmoid(v)


def _softplus(v):
    return jnp.maximum(v, 0.0) + jnp.log1p(jnp.exp(-jnp.abs(v)))


def _iota(shape, dim):
    return lax.broadcasted_iota(I32, shape, dim)


def _dot(a, b, **kw):
    return jnp.dot(a, b, preferred_element_type=F32, **kw)


def _dot_nt(a, b, **kw):
    return lax.dot_general(a, b, (((1,), (1,)), ((), ())), preferred_element_type=F32, **kw)


def _dot_tn(a, b, **kw):
    return lax.dot_general(a, b, (((0,), (0,)), ((), ())), preferred_element_type=F32, **kw)


def _rms(v, gain):
    return v * lax.rsqrt(jnp.mean(v * v, axis=-1, keepdims=True) + EPS) * gain


def _mod_kernel(c_ref, w_ref, b_ref, o_ref):
    o_ref[...] = _dot(_silu(c_ref[...]), w_ref[...], precision=HIGHEST) + b_ref[...]


def _modulation(cvec, w_mod, b_mod):
    rows, d = cvec.shape
    n = w_mod.shape[1]
    tn = 1024
    return pl.pallas_call(
        _mod_kernel,
        grid=(n // tn,),
        in_specs=[pl.BlockSpec((rows, d), lambda j: (0, 0)),
                  pl.BlockSpec((d, tn), lambda j: (0, j)),
                  pl.BlockSpec((1, tn), lambda j: (0, j))],
        out_specs=pl.BlockSpec((rows, tn), lambda j: (0, j)),
        out_shape=jax.ShapeDtypeStruct((rows, n), F32),
        compiler_params=_cparams(("arbitrary",)),
        name="adaln_modulation",
    )(cvec, w_mod, b_mod)


_PLAIN, _ROPE, _ROPE_SCALED, _SIGMOID = 0, 1, 2, 3


def _proj_kernel(x_ref, gain_ref, sc_ref, sh_ref, *refs, kinds, rope, n_w):
    w_refs = refs[:n_w]
    wg_ref, wgt_ref, prow_ref, pcol_ref, cos_ref, sin_ref = refs[n_w:n_w + 6]
    out_refs = refs[n_w + 6:]
    groups = [(w, c) for w in w_refs for c in range(0, w.shape[1], MIX_W)]
    n_feat = len(kinds)
    feat_refs = out_refs[:n_feat]
    cols_ref, grow_ref, gl_ref = out_refs[n_feat:]
    tm = x_ref.shape[0]

    x = x_ref[...]
    h = (_rms(x, gain_ref[...]) * (1.0 + sc_ref[...]) + sh_ref[...]).astype(BF16)

    r_i = _iota((CHUNK, CHUNK), 0)
    c_i = _iota((CHUNK, CHUNK), 1)
    lower_incl = (c_i <= r_i).astype(BF16)
    upper_incl = (c_i >= r_i).astype(BF16)

    def prefix_rows(v):
        hi, lo = _split_bf16(v)
        return _dot(lower_incl, hi) + _dot(lower_incl, lo)

    def prefix_lanes(v):
        hi, lo = _split_bf16(v)
        return _dot(hi, upper_incl) + _dot(lo, upper_incl)

    nc = N_HEADS * 8
    pg = _dot(h, wg_ref[...])
    colt = _iota((tm, nc), 1) & 7
    la = jnp.where((colt == 2) | (colt == 3), 0.0, prow_ref[0:1, :] * _softplus(pg + prow_ref[1:2, :]))
    beta = _sigmoid(pg)
    colc = _iota((CHUNK, nc), 1) & 7
    fwd_col = (colc & 1) == 0
    for c in range(tm // CHUNK):
        sl = slice(c * CHUNK, (c + 1) * CHUNK)
        la_c = la[sl]
        pre = prefix_rows(la_c)
        suf = pre[CHUNK - 1:CHUNK, :] - pre + la_c
        g_c = jnp.where(fwd_col, pre, suf)
        rest = jnp.where(fwd_col, suf, pre) - la_c
        vals = jnp.where(colc < 2, g_c, jnp.where(colc < 4, beta[sl], jnp.where(
            colc < 6, jnp.exp(g_c), jnp.exp(rest))))
        for hh in range(N_HEADS):
            cols_ref[hh, sl, :] = vals[:, 8 * hh:8 * hh + 8]
        gl_ref[c:c + 1, :] = jnp.exp(g_c[0:1, :] + rest[0:1, :])

    pgt = _dot_nt(wgt_ref[...], h)
    rowq = _iota((N_HEADS * 8, tm), 0) & 7
    lat = jnp.where(rowq < 2, pcol_ref[:, 0:1] * _softplus(pgt + pcol_ref[:, 1:2]), 0.0)
    rowc = _iota((N_HEADS * 8, CHUNK), 0) & 7
    for c in range(tm // CHUNK):
        sl = slice(c * CHUNK, (c + 1) * CHUNK)
        lat_c = lat[:, sl]
        pre_t = prefix_lanes(lat_c)
        suf_t = pre_t[:, CHUNK - 1:CHUNK] - pre_t + lat_c
        grow_ref[:, :, sl] = jnp.where(rowc == 0, pre_t, suf_t).reshape(N_HEADS, 8, CHUNK)

    if rope:
        cos2 = cos_ref[...]
        sin2 = sin_ref[...]

    for g, kind in enumerate(kinds):
        w_ref, c0 = groups[g]
        p = _dot(h, w_ref[:, c0:c0 + MIX_W])
        if kind == _SIGMOID:
            p = _sigmoid(p)
        elif kind in (_ROPE, _ROPE_SCALED) and rope:
            heads = []
            for hh in range(N_HEADS):
                t = p[:, hh * HEAD_DIM:(hh + 1) * HEAD_DIM]
                heads.append(t * cos2 + pltpu.roll(t, HEAD_DIM // 2, 1) * sin2)
            p = jnp.concatenate(heads, axis=1)
        if kind == _ROPE_SCALED:
            p = p * (HEAD_DIM ** -0.5)
        feat_refs[g][...] = p.astype(feat_refs[g].dtype)


def _in_projection(x, gain, scale, shift, w_mains, w_gab, w_gab_t, prow, pcol, cos2, sin2, kinds, rope):
    b, l, d = x.shape
    tm = min(PROJ_TM, l)
    tiles = l // tm
    n_chunk = tm // CHUNK
    feat_shapes = [jax.ShapeDtypeStruct((b, l, MIX_W), BF16) for _ in kinds]
    feat_specs = [pl.BlockSpec((None, tm, MIX_W), lambda i, j: (i, j, 0)) for _ in kinds]
    out_shape = feat_shapes + [jax.ShapeDtypeStruct((b, N_HEADS, l, 8), F32),
                               jax.ShapeDtypeStruct((b, N_HEADS, 8, l), F32),
                               jax.ShapeDtypeStruct((b, tiles, n_chunk, N_HEADS * 8), F32)]
    out_specs = feat_specs + [pl.BlockSpec((None, N_HEADS, tm, 8), lambda i, j: (i, 0, j, 0)),
                              pl.BlockSpec((None, N_HEADS, 8, tm), lambda i, j: (i, 0, 0, j)),
                              pl.BlockSpec((None, None, n_chunk, N_HEADS * 8), lambda i, j: (i, j, 0, 0))]
    const = functools.partial(pl.BlockSpec, pipeline_mode=pl.Buffered(1))
    in_specs = [
        pl.BlockSpec((None, tm, d), lambda i, j: (i, j, 0)),
        const((1, d), lambda i, j: (0, 0)),
        pl.BlockSpec((None, 1, d), lambda i, j: (i, 0, 0)),
        pl.BlockSpec((None, 1, d), lambda i, j: (i, 0, 0)),
    ] + [const(w.shape, lambda i, j: (0, 0)) for w in w_mains] + [
        const((d, N_HEADS * 8), lambda i, j: (0, 0)),
        const((N_HEADS * 8, d), lambda i, j: (0, 0)),
        const((2, N_HEADS * 8), lambda i, j: (0, 0)),
        const((N_HEADS * 8, 2), lambda i, j: (0, 0)),
        pl.BlockSpec((tm, HEAD_DIM), lambda i, j: (j, 0)),
        pl.BlockSpec((tm, HEAD_DIM), lambda i, j: (j, 0)),
    ]
    outs = pl.pallas_call(
        functools.partial(_proj_kernel, kinds=tuple(kinds), rope=rope, n_w=len(w_mains)),
        grid=(b, tiles),
        in_specs=in_specs,
        out_specs=out_specs,
        out_shape=out_shape,
        compiler_params=_cparams(("arbitrary", "arbitrary")),
        name="in_projection_rope" if rope else "in_projection_ctx",
    )(x, gain, scale, shift, *w_mains, w_gab, w_gab_t, prow, pcol, cos2, sin2)
    feats = outs[:len(kinds)]
    cols, grow, gl = outs[len(kinds):]
    return feats, (cols, grow, gl.reshape(b * (l // CHUNK), N_HEADS * 8))


def _unit_triangular_inverses(mats, lowers):
    r = _iota((CHUNK, CHUNK), 0)
    c = _iota((CHUNK, CHUNK), 1)
    eye = (r == c).astype(F32)
    invs = [eye - jnp.where((r >> 1) == (c >> 1), a, 0.0) for a in mats]
    for level in range(1, int(math.log2(CHUNK))):
        s = 1 << level
        mask = ((r >> (level + 1)) == (c >> (level + 1))) & ((r >> level) != (c >> level))
        invb = [inv.astype(BF16) for inv in invs]
        offs = [jnp.where(mask, a, 0.0).astype(BF16) for a in mats]
        if s < 8:
            half = [_dot(off, ib).astype(BF16) for off, ib in zip(offs, invb)]
            invs = [inv - _dot(ib, hf) for inv, ib, hf in zip(invs, invb, half)]
            continue
        def rows_of(x, lower, moving):
            first = s if (lower == moving) else 0
            return [x[g * 2 * s + first:g * 2 * s + first + s] for g in range(CHUNK // (2 * s))]

        half = [_dot(jnp.concatenate(rows_of(off, lo, True), axis=0), ib).astype(BF16)
                for off, ib, lo in zip(offs, invb, lowers)]
        zero = jnp.zeros((s, CHUNK), BF16)
        full = []
        for hf, lo in zip(half, lowers):
            pieces = []
            for g in range(CHUNK // (2 * s)):
                piece = hf[g * s:(g + 1) * s]
                pieces += [zero, piece] if lo else [piece, zero]
            full.append(jnp.concatenate(pieces, axis=0))
        corr = [_dot(jnp.concatenate(rows_of(ib, lo, True), axis=0), hf) for ib, hf, lo in zip(invb, full, lowers)]
        new = []
        for inv, cr, lo in zip(invs, corr, lowers):
            keep = rows_of(inv, lo, False)
            moved = [m - cr[g * s:(g + 1) * s] for g, m in enumerate(rows_of(inv, lo, True))]
            pieces = []
            for k, m in zip(keep, moved):
                pieces += [k, m] if lo else [m, k]
            new.append(jnp.concatenate(pieces, axis=0))
        invs = new
    return invs


def _mixer_kernel(*refs, seq_len, with_query):
    n_chunk = seq_len // CHUNK
    if with_query:
        (lg_ref, gl_ref, rk_ref, rv_ref, gk_ref, gv_ref, rq_ref, gq_ref, rg_ref, gz_ref,
         cols_ref, grow_ref, cq_ref, ck_ref, cv_ref, s0_ref, gnw_ref, rmsw_ref,
         yret_ref, ygdn_ref,
         qes, oret, ogdn, ubuf, wbuf, pbuf, nbuf, cbuf, kvbuf, rtile, state) = refs
    else:
        (lg_ref, gl_ref, rk_ref, rv_ref, gk_ref, gv_ref,
         cols_ref, grow_ref, ck_ref, cv_ref, s0_ref,
         sfin_ref,
         nbuf, cbuf, kvbuf, rtile, state) = refs
    bi = pl.program_id(0)
    hi = pl.program_id(1)

    row = _iota((CHUNK, CHUNK), 0)
    colm = _iota((CHUNK, CHUNK), 1)
    rowf = row.astype(F32)
    colf = colm.astype(F32)

    def conv_chunk(src_ref, w_ref, n):
        s = pl.multiple_of(n * CHUNK, CHUNK)
        x = src_ref[pl.ds(s, CHUNK), :].astype(F32)
        ps = pl.multiple_of(jnp.maximum(s - BF16_TILE_ROWS, 0), BF16_TILE_ROWS)
        ns = pl.multiple_of(jnp.minimum(s + CHUNK, seq_len - BF16_TILE_ROWS), BF16_TILE_ROWS)
        prev_row = src_ref[pl.ds(ps, BF16_TILE_ROWS), :].astype(F32)[BF16_TILE_ROWS - 1:BF16_TILE_ROWS, :]
        next_row = src_ref[pl.ds(ns, BF16_TILE_ROWS), :].astype(F32)[0:1, :]
        prev_row = prev_row * jnp.where(n > 0, 1.0, 0.0)
        next_row = next_row * jnp.where(n < n_chunk - 1, 1.0, 0.0)
        xp = jnp.where(row == 0, jnp.broadcast_to(prev_row, (CHUNK, HEAD_DIM)), pltpu.roll(x, 1, 0))
        xn = jnp.where(row == CHUNK - 1, jnp.broadcast_to(next_row, (CHUNK, HEAD_DIM)), pltpu.roll(x, CHUNK - 1, 0))
        return _silu(w_ref[0:1, :] * xp + w_ref[1:2, :] * x + w_ref[2:3, :] * xn)

    def l2n(v):
        return v * lax.rsqrt(jnp.sum(v * v, axis=-1, keepdims=True) + EPS)

    for d in range(2):
        lg = lg_ref[d, hi]
        if d == 0:
            dist, pos_q, pos_k = rowf - colf, rowf + 1.0, (CHUNK - 1.0) - rowf
        else:
            dist, pos_q, pos_k = colf - rowf, CHUNK - rowf, rowf
        rtile[4 * d + 0] = jnp.where(dist >= 0, jnp.exp(lg * jnp.maximum(dist, 0.0)), 0.0)
        rtile[4 * d + 1] = jnp.exp(lg * pos_q)
        rtile[4 * d + 2] = jnp.exp(lg * pos_k)
        rtile[4 * d + 3] = jnp.exp(lg * jnp.full((CHUNK, CHUNK), float(CHUNK), F32))

    state[...] = s0_ref[...]

    def bcast_col(cols, j):
        return jnp.broadcast_to(cols[:, j:j + 1], (CHUNK, CHUNK))

    pre_chunks = min(8, n_chunk)

    def prepass(m, carry):
        chunks = [m * pre_chunks + j for j in range(pre_chunks)]
        chunk_cs = [pl.ds(pl.multiple_of(n * CHUNK, CHUNK), CHUNK) for n in chunks]
        conv_k, conv_v, conv_q = [], [], []
        for n, cs in zip(chunks, chunk_cs):
            conv_k.append(l2n(conv_chunk(gk_ref, ck_ref, n)).astype(BF16))
            conv_v.append(conv_chunk(gv_ref, cv_ref, n).astype(BF16))
            if with_query:
                conv_q.append((l2n(conv_chunk(gq_ref, cq_ref, n)) * (HEAD_DIM ** -0.5)).astype(BF16))
                ogdn[cs, :] = jnp.zeros((CHUNK, HEAD_DIM), F32)

        jobs = [(j, d) for j in range(pre_chunks) for d in range(2)]
        css = [chunk_cs[j] for j, _ in jobs]
        slots = [d * n_chunk + chunks[j] for j, d in jobs]
        dirs = [d for _, d in jobs]
        ks = [rk_ref[cs, :] for cs in css]
        vs = [rv_ref[cs, :] for cs in css]
        if with_query:
            scs = [(_dot_nt(rq_ref[cs, :], k) * rtile[4 * d + 0]).astype(BF16) for cs, k, d in zip(css, ks, dirs)]
            outs = [_dot(sc, v) for sc, v in zip(scs, vs)]
            for j, cs in enumerate(chunk_cs):
                oret[cs, :] = outs[2 * j] + outs[2 * j + 1]
        kvs = [_dot_tn((k.astype(F32) * rtile[4 * d + 2]).astype(BF16), v) for k, v, d in zip(ks, vs, dirs)]
        for slot, kv in zip(slots, kvs):
            kvbuf[slot] = kv
        ks = [conv_k[j] for j, _ in jobs]
        vs = [conv_v[j] for j, _ in jobs]
        colss = [cols_ref[cs, :] for cs in css]
        betas = [bcast_col(cols, 2 + d) for cols, d in zip(colss, dirs)]
        incls = [(row >= colm) if d == 0 else (row <= colm) for d in dirs]
        stricts = [(row > colm) if d == 0 else (row < colm) for d in dirs]
        decs = [jnp.exp(jnp.where(incl, bcast_col(cols, d) - jnp.broadcast_to(grow_ref[d:d + 1, cs], (CHUNK, CHUNK)), 0.0))
                for cols, d, cs, incl in zip(colss, dirs, css, incls)]
        kks = [_dot_nt(k, k) for k in ks]
        mats = [kk * beta * jnp.where(strict, dec, 0.0) for kk, beta, strict, dec in zip(kks, betas, stricts, decs)]
        tinvs = [t.astype(BF16) for t in _unit_triangular_inverses(mats, [d == 0 for d in dirs])]
        rhs = [jnp.concatenate([(beta * v.astype(F32)).astype(BF16),
                                (beta * bcast_col(cols, 4 + d) * k.astype(F32)).astype(BF16)], axis=1)
               for beta, v, cols, d, k in zip(betas, vs, colss, dirs, ks)]
        uws = [_dot(t, r) for t, r in zip(tinvs, rhs)]
        us = [uw[:, :HEAD_DIM] for uw in uws]
        ws = [uw[:, HEAD_DIM:].astype(BF16) for uw in uws]
        kts = [(k.astype(F32) * bcast_col(cols, 6 + d)).astype(BF16) for k, cols, d in zip(ks, colss, dirs)]
        ncs = [_dot_tn(kt, uw.astype(BF16)) for kt, uw in zip(kts, uws)]
        for slot, nc_mat in zip(slots, ncs):
            cbuf[slot] = nc_mat[:, :HEAD_DIM]
            nbuf[slot] = nc_mat[:, HEAD_DIM:].astype(BF16)
        if with_query:
            for slot, u, w in zip(slots, us, ws):
                ubuf[slot] = u.astype(BF16)
                wbuf[slot] = w
            qks = [_dot_nt(conv_q[j], k) for (j, _), k in zip(jobs, ks)]
            for slot, qk, incl, dec in zip(slots, qks, incls, decs):
                pbuf[slot] = (qk * jnp.where(incl, dec, 0.0)).astype(BF16)
            for (j, d), cs, cols in zip(jobs, css, colss):
                qes[d, cs, :] = (conv_q[j].astype(F32) * bcast_col(cols, 4 + d)).astype(BF16)
        return carry

    lax.fori_loop(0, n_chunk // pre_chunks, prepass, 0)

    def finish(cs):
        ro = oret[cs, :]
        rc = ro - jnp.mean(ro, axis=-1, keepdims=True)
        ry = rc * lax.rsqrt(jnp.mean(rc * rc, axis=-1, keepdims=True) + EPS)
        yret_ref[cs, :] = (ry * gnw_ref[...] * _silu(rg_ref[cs, :].astype(F32))).astype(BF16)
        go = ogdn[cs, :]
        gy = go * lax.rsqrt(jnp.mean(go * go, axis=-1, keepdims=True) + EPS)
        ygdn_ref[cs, :] = (gy * rmsw_ref[...] * _silu(gz_ref[cs, :].astype(F32))).astype(BF16)

    def chunk_ids(n):
        nds = [n, n_chunk - 1 - n]
        css = [pl.ds(nd * CHUNK if isinstance(nd, int) else pl.multiple_of(nd * CHUNK, CHUNK), CHUNK) for nd in nds]
        slots = [d * n_chunk + nd for d, nd in enumerate(nds)]
        return nds, css, slots

    def late_outputs(n, vnb, finishing):
        _, css, slots = chunk_ids(n)
        for d in range(2):
            ogdn[css[d], :] += _dot(pbuf[slots[d]], vnb[d])
            if finishing:
                finish(css[d])

    def scan_step(n, vnb_prev, has_prev, finish_prev):
        nds, css, slots = chunk_ids(n)
        ret_st = [state[d] for d in range(2)]
        gdn_st = [state[2 + d] for d in range(2)]
        gdn_stb = [st.astype(BF16) for st in gdn_st]
        shrink = [_dot(nbuf[slot], stb) for slot, stb in zip(slots, gdn_stb)]
        for d in range(2):
            state[2 + d] = gl_ref[bi * n_chunk + nds[d], 8 * hi + d] * gdn_st[d] - shrink[d] + cbuf[slots[d]]
            state[d] = rtile[4 * d + 3] * ret_st[d] + kvbuf[slots[d]]
        if not with_query:
            return vnb_prev
        vnb = tuple((ubuf[slot].astype(F32) - _dot(wbuf[slot], stb)).astype(BF16)
                    for slot, stb in zip(slots, gdn_stb))
        for d in range(2):
            oret[css[d], :] += _dot(rq_ref[css[d], :], ret_st[d].astype(BF16)) * rtile[4 * d + 1]
            ogdn[css[d], :] += _dot(qes[d, css[d], :], gdn_stb[d])
        if has_prev:
            late_outputs(n - 1, vnb_prev, finish_prev)
        return vnb

    if with_query:
        half = n_chunk // 2
        zero = jnp.zeros((CHUNK, HEAD_DIM), BF16)
        vnb = scan_step(0, (zero, zero), has_prev=False, finish_prev=False)
        vnb = lax.fori_loop(1, half + 1, functools.partial(scan_step, has_prev=True, finish_prev=False), vnb)
        vnb = lax.fori_loop(half + 1, n_chunk, functools.partial(scan_step, has_prev=True, finish_prev=True), vnb)
        late_outputs(n_chunk - 1, vnb, True)
    else:
        lax.fori_loop(0, n_chunk, functools.partial(scan_step, has_prev=False, finish_prev=False), 0)
        sfin_ref[...] = state[...]


def _mixers(lg, gl, feats, dec, conv, s0, gnw, rmsw, with_query):
    cols, grow = dec
    b, l, _ = feats["rk"].shape
    n_slot = 2 * (l // CHUNK)
    assert (l // CHUNK) % min(8, l // CHUNK) == 0
    smem = pl.BlockSpec(memory_space=pltpu.SMEM)
    head = pl.BlockSpec((None, l, HEAD_DIM), lambda i, j: (i, 0, j))
    colspec = pl.BlockSpec((None, None, l, 8), lambda i, j: (i, j, 0, 0))
    rowspec = pl.BlockSpec((None, None, 8, l), lambda i, j: (i, j, 0, 0))
    tile_f32 = pltpu.VMEM((n_slot, CHUNK, CHUNK), F32)
    tile_bf16 = pltpu.VMEM((n_slot, CHUNK, CHUNK), BF16)
    st_spec = pl.BlockSpec((None, None, 4, HEAD_DIM, HEAD_DIM), lambda i, j: (i, j, 0, 0, 0))

    def conv_spec(which):
        return pl.BlockSpec((SHORT_CONV, HEAD_DIM), lambda i, j, w=which: (0, w * N_HEADS + j))

    def gain_spec():
        return pl.BlockSpec((1, HEAD_DIM), lambda i, j: (0, j))

    if with_query:
        args = [lg, gl, feats["rk"], feats["rv"], feats["gk"], feats["gv"], feats["rq"], feats["gq"], feats["rg"],
                feats["gz"], cols, grow, conv, conv, conv, s0, gnw, rmsw]
        in_specs = [smem, smem] + [head] * 8 + [colspec, rowspec, conv_spec(0), conv_spec(1), conv_spec(2),
                                                st_spec, gain_spec(), gain_spec()]
        out_shape = [jax.ShapeDtypeStruct((b, l, MIX_W), BF16)] * 2
        out_specs = [head, head]
        scratch = ([pltpu.VMEM((2, l, HEAD_DIM), BF16)] + [pltpu.VMEM((l, HEAD_DIM), F32)] * 2
                   + [tile_bf16, tile_bf16, tile_bf16, tile_bf16, tile_f32, tile_f32])
    else:
        args = [lg, gl, feats["rk"], feats["rv"], feats["gk"], feats["gv"], cols, grow, conv, conv, s0]
        in_specs = [smem, smem] + [head] * 4 + [colspec, rowspec, conv_spec(1), conv_spec(2), st_spec]
        out_shape = jax.ShapeDtypeStruct((b, N_HEADS, 4, HEAD_DIM, HEAD_DIM), F32)
        out_specs = st_spec
        scratch = [tile_bf16, tile_f32, tile_f32]
    scratch = scratch + [pltpu.VMEM((8, CHUNK, CHUNK), F32), pltpu.VMEM((4, HEAD_DIM, HEAD_DIM), F32)]
    return pl.pallas_call(
        functools.partial(_mixer_kernel, seq_len=l, with_query=with_query),
        grid=(b, N_HEADS),
        in_specs=in_specs,
        out_specs=out_specs,
        out_shape=out_shape,
        scratch_shapes=scratch,
        compiler_params=_cparams(("arbitrary", "arbitrary")),
        name="mixers_latent" if with_query else "mixers_context",
    )(*args)


def _split_bf16(v):
    hi = v.astype(BF16)
    return hi, (v - hi.astype(F32)).astype(BF16)


def _route(h2, wrt_ref, bias_ref, cand_ref, sel_ref, wd_ref):
    tm = h2.shape[0]
    h_hi, h_lo = _split_bf16(h2)
    w_hi, w_lo = _split_bf16(wrt_ref[...])
    logits = _dot_nt(w_hi, h_hi) + (_dot_nt(w_hi, h_lo) + _dot_nt(w_lo, h_hi))
    scores = _sigmoid(logits)
    biased = scores + bias_ref[...]
    neg_inf = float("-inf")
    sub = _iota((GROUP_SIZE, tm), 0).astype(F32)
    group_score = []
    for g in range(N_GROUPS):
        blk = biased[g * GROUP_SIZE:(g + 1) * GROUP_SIZE, :]
        m1 = jnp.max(blk, axis=0, keepdims=True)
        first = jnp.min(jnp.where(blk == m1, sub, float(GROUP_SIZE)), axis=0, keepdims=True)
        m2 = jnp.max(jnp.where(sub == first, neg_inf, blk), axis=0, keepdims=True)
        group_score.append(m1 + m2)
    for g in range(N_GROUPS):
        ahead = jnp.zeros((1, tm), I32)
        for g2 in range(N_GROUPS):
            if g2 == g:
                continue
            before = (group_score[g2] > group_score[g])
            if g2 < g:
                before = before | (group_score[g2] == group_score[g])
            ahead = ahead + before.astype(I32)
        keep = jnp.broadcast_to(ahead, (GROUP_SIZE, tm)) < TOPK_GROUPS
        cand_ref[g * GROUP_SIZE:(g + 1) * GROUP_SIZE, :] = jnp.where(
            keep, biased[g * GROUP_SIZE:(g + 1) * GROUP_SIZE, :], neg_inf)
    work = cand_ref[...]
    eidx = _iota((N_EXPERTS, tm), 0).astype(F32)
    sel = jnp.zeros((N_EXPERTS, tm), jnp.bool_)
    for _ in range(TOP_K):
        best = jnp.max(work, axis=0, keepdims=True)
        first = jnp.min(jnp.where(work == best, eidx, float(N_EXPERTS)), axis=0, keepdims=True)
        pick = eidx == first
        sel = sel | pick
        work = jnp.where(pick, neg_inf, work)
    picked = jnp.where(sel, scores, 0.0)
    wsum = jnp.sum(picked, axis=0, keepdims=True)
    sel_ref[...] = sel.astype(F32)
    wd_ref[...] = picked / wsum * ROUTED_SCALE


def _mixout_kernel(x_ref, yr_ref, yg_ref, g0_ref, g1_ref, g2_ref, g3_ref, wr_ref, wg_ref, wo_ref,
                   npost_ref, gate1_ref, nffn_ref, sc2_ref, sh2_ref, wrt_ref, bias_ref,
                   x1_ref, h2_ref, sel_ref, wd_ref, cand_ref):
    r = _dot(yr_ref[...], wr_ref[...])
    g = _dot(yg_ref[...], wg_ref[...])
    half = r.shape[1] // 2
    merged = jnp.concatenate(
        [g0_ref[...].astype(F32) * r[:, :half] + g2_ref[...].astype(F32) * g[:, :half],
         g1_ref[...].astype(F32) * r[:, half:] + g3_ref[...].astype(F32) * g[:, half:]], axis=1)
    mo = _dot(merged.astype(BF16), wo_ref[...])
    x1 = x_ref[...] + gate1_ref[...] * _rms(mo, npost_ref[...])
    x1_ref[...] = x1
    h2 = _rms(x1, nffn_ref[...]) * (1.0 + sc2_ref[...]) + sh2_ref[...]
    h2_ref[...] = h2.astype(BF16)
    _route(h2, wrt_ref, bias_ref, cand_ref, sel_ref, wd_ref)


def _mixer_out(x, yr, yg, gates, wr, wg, wo, npost, gate1, nffn, sc2, sh2, wrt, bias):
    b, l, d = x.shape
    tm = min(MIXOUT_TM, l)
    tiles = l // tm
    tok = lambda w: pl.BlockSpec((None, tm, w), lambda i, j: (i, j, 0))
    const = functools.partial(pl.BlockSpec, pipeline_mode=pl.Buffered(1))
    vec = lambda: const((1, d), lambda i, j: (0, 0))
    bvec = lambda: pl.BlockSpec((None, 1, d), lambda i, j: (i, 0, 0))
    emap = pl.BlockSpec((N_EXPERTS, tm), lambda i, j: (0, i * tiles + j))
    in_specs = ([tok(d), tok(MIX_W), tok(MIX_W)] + [tok(MIX_W)] * 4
                + [const((MIX_W, d), lambda i, j: (0, 0)), const((MIX_W, d), lambda i, j: (0, 0)),
                   const((d, d), lambda i, j: (0, 0)),
                   vec(), bvec(), vec(), bvec(), bvec(),
                   const((N_EXPERTS, d), lambda i, j: (0, 0)), const((N_EXPERTS, 1), lambda i, j: (0, 0))])
    return pl.pallas_call(
        _mixout_kernel,
        grid=(b, tiles),
        in_specs=in_specs,
        out_specs=[tok(d), tok(d), emap, emap],
        out_shape=[jax.ShapeDtypeStruct((b, l, d), F32), jax.ShapeDtypeStruct((b, l, d), BF16),
                   jax.ShapeDtypeStruct((N_EXPERTS, b * l), F32), jax.ShapeDtypeStruct((N_EXPERTS, b * l), F32)],
        scratch_shapes=[pltpu.VMEM((N_EXPERTS, tm), F32)],
        compiler_params=_cparams(("arbitrary", "arbitrary")),
        name="mixer_out_router",
    )(x, yr, yg, *gates, wr, wg, wo, npost, gate1, nffn, sc2, sh2, wrt, bias)


def _tile_positions(sel):
    tm = sel.shape[1]
    selb = sel.astype(BF16)
    earlier = (_iota((tm, tm), 0) < _iota((tm, tm), 1)).astype(BF16)
    rank = _dot(selb, earlier)
    cnt = _dot(selb, jnp.ones((tm, tm), BF16))
    nmb = jnp.floor((cnt + (MB_ROWS - 1)) * (1.0 / MB_ROWS))
    below = (_iota((N_EXPERTS, N_EXPERTS), 1) < _iota((N_EXPERTS, N_EXPERTS), 0)).astype(BF16)
    offmb = _dot(below, nmb.astype(BF16))
    rank_hi = jnp.floor(rank * (1.0 / MB_ROWS))
    hi = jnp.where(sel > 0.0, offmb + rank_hi, 255.0)
    lo = rank - rank_hi * MB_ROWS
    return hi, lo, cnt, offmb, rank


def _dispatch_kernel(h2_ref, sel_ref, xs_ref, cnt_ref):
    last = pl.num_programs(0) - 1

    @pl.when(pl.program_id(0) < last)
    def _():
        _dispatch_tile(h2_ref, sel_ref, xs_ref, cnt_ref)

    @pl.when(pl.program_id(0) == last)
    def _():
        xs_ref[...] = jnp.zeros_like(xs_ref)


def _dispatch_tile(h2_ref, sel_ref, xs_ref, cnt_ref):
    tm = h2_ref.shape[0]
    sel = sel_ref[...]
    hi, lo, _, _, _ = _tile_positions(sel)
    hib = hi.astype(BF16)
    lob = lo.astype(BF16)
    selb = sel.astype(BF16)
    cnt_row = _dot_nt(jnp.ones((8, tm), BF16), selb)
    nmb_row = jnp.floor((cnt_row + (MB_ROWS - 1)) * (1.0 / MB_ROWS))
    before = (_iota((N_EXPERTS, N_EXPERTS), 0) < _iota((N_EXPERTS, N_EXPERTS), 1)).astype(BF16)
    start_row = _dot(nmb_row.astype(BF16), before) * MB_ROWS
    cnt_ref[...] = cnt_row
    start = start_row[0:1, :]
    end = start + cnt_row[0:1, :]
    x = h2_ref[...]
    rch = 256
    def one_hot(rc):
        r_e = (rc * rch + _iota((rch, N_EXPERTS), 0)).astype(F32)
        owner = ((r_e >= start) & (r_e < end)).astype(BF16)
        r_t = rc * rch + _iota((rch, tm), 0)
        match = ((_dot(owner, hib) == (r_t >> _MB_SHIFT).astype(F32))
                 & (_dot(owner, lob) == (r_t & (MB_ROWS - 1)).astype(F32)))
        return match.astype(BF16)

    n_rc = TILE_ROWS // rch
    n_sure = n_rc - TAIL_CHUNKS
    nxt = one_hot(0)
    for rc in range(n_sure):
        cur = nxt
        if rc + 1 < n_sure:
            nxt = one_hot(rc + 1)
        xs_ref[rc * rch:(rc + 1) * rch, :] = _dot(cur, x).astype(BF16)
    tail_used = jnp.sum(nmb_row[0:1, :]) * MB_ROWS > n_sure * rch

    @pl.when(tail_used)
    def _():
        for rc in range(n_sure, n_rc):
            xs_ref[rc * rch:(rc + 1) * rch, :] = _dot(one_hot(rc), x).astype(BF16)

    @pl.when(jnp.logical_not(tail_used))
    def _():
        xs_ref[n_sure * rch:, :] = jnp.zeros(((n_rc - n_sure) * rch, x.shape[1]), BF16)


def _dispatch(h2, sel_t):
    t, d = h2.shape
    tm = MOE_TM
    nt = t // tm
    return pl.pallas_call(
        _dispatch_kernel,
        grid=(nt + 1,),
        in_specs=[pl.BlockSpec((tm, d), lambda i: (jnp.minimum(i, nt - 1), 0)),
                  pl.BlockSpec((N_EXPERTS, tm), lambda i: (0, jnp.minimum(i, nt - 1)))],
        out_specs=[pl.BlockSpec((TILE_ROWS, d), lambda i: (i, 0)),
                   pl.BlockSpec((None, 8, N_EXPERTS), lambda i: (jnp.minimum(i, nt - 1), 0, 0))],
        out_shape=[jax.ShapeDtypeStruct((nt * TILE_ROWS + (EXP_OUT + 1) * EXP_BM, d), BF16),
                   jax.ShapeDtypeStruct((nt, 8, N_EXPERTS), F32)],
        compiler_params=_cparams(("arbitrary",)),
        name="moe_dispatch",
    )(h2, sel_t)


def _expert_plan(cnt, n_blk):
    nt = cnt.shape[0]
    nmb = (cnt + (MB_ROWS - 1)) // MB_ROWS
    offmb = jnp.cumsum(nmb, axis=1) - nmb
    per_e = nmb.T
    incl = jnp.cumsum(per_e, axis=1)
    excl = incl - per_e
    tot = incl[:, -1]
    nb = (tot + (EXP_MB - 1)) // EXP_MB
    bend = jnp.cumsum(nb)
    bstart = bend - nb
    b = jnp.arange(n_blk, dtype=I32)
    valid = b < bend[-1]
    last = jnp.maximum(bend[-1] - 1, 0)
    bq = jnp.where(valid, b, last)
    blk_e = jnp.minimum(jnp.sum((bend[None, :] <= bq[:, None]).astype(I32), axis=1), N_EXPERTS - 1)
    onehot_e = (blk_e[:, None] == jnp.arange(N_EXPERTS, dtype=I32)[None, :]).astype(I32)
    pick = lambda table: jnp.dot(onehot_e.astype(F32), table.astype(F32), precision=HIGHEST).astype(I32)
    bstart_b = jnp.sum(onehot_e * bstart[None, :], axis=1)
    tot_b = jnp.sum(onehot_e * tot[None, :], axis=1)
    p0 = (bq - bstart_b) * EXP_MB
    blk_nmb = jnp.where(valid, jnp.clip(tot_b - p0, 0, EXP_MB), 0).astype(I32)
    p = p0[:, None] + jnp.arange(EXP_MB, dtype=I32)[None, :]
    incl_b = pick(incl)
    passed = (incl_b[:, None, :] <= p[:, :, None]).astype(I32)
    tile = jnp.minimum(jnp.sum(passed, axis=2), nt - 1)
    shift_b = pick(offmb.T - excl)
    step_b = jnp.concatenate([shift_b[:, 1:] - shift_b[:, :-1], jnp.zeros_like(shift_b[:, :1])], axis=1)
    shift = shift_b[:, :1] + jnp.sum(passed * step_b[:, None, :], axis=2)
    where = tile * _TILE_MB + p + shift
    j = jnp.arange(EXP_MB, dtype=I32)[None, :]
    used = j < blk_nmb[:, None]
    spare = nt * _TILE_MB + (b[:, None] % EXP_OUT) * EXP_MB + j
    zeros_mb = nt * _TILE_MB + EXP_OUT * EXP_MB + j
    src = jnp.where(used, where, jnp.where(valid[:, None], where[:, :1], zeros_mb))
    dst = jnp.where(used, where, spare)
    dst = jnp.concatenate([nt * _TILE_MB + (EXP_OUT - 1) * EXP_MB + j, dst], axis=0)
    return bstart.astype(I32), nb.astype(I32), src.reshape(-1).astype(I32), dst.reshape(-1).astype(I32)


def _expert_kernel(bstart_ref, nb_ref, src_ref, dst_ref, xs_hbm, wg_ref, wu_ref, wd_ref, ys_hbm,
                   xbuf, ybuf, wgub, wdb, in_sem, out_sem):
    e = pl.program_id(0)
    n_in = xbuf.shape[0]
    n_out = ybuf.shape[0]

    def gather_copy(blk, slot, j):
        rows = pl.ds(pl.multiple_of(src_ref[blk * EXP_MB + j] * MB_ROWS, MB_ROWS), MB_ROWS)
        return pltpu.make_async_copy(xs_hbm.at[rows, :], xbuf.at[slot, j * MB_ROWS:(j + 1) * MB_ROWS, :],
                                     in_sem.at[slot])

    def scatter_copy(blk, slot, j):
        rows = pl.ds(pl.multiple_of(dst_ref[(blk + 1) * EXP_MB + j] * MB_ROWS, MB_ROWS), MB_ROWS)
        return pltpu.make_async_copy(ybuf.at[slot, j * MB_ROWS:(j + 1) * MB_ROWS, :], ys_hbm.at[rows, :],
                                     out_sem.at[slot])

    def start_gather(blk):
        for j in range(EXP_MB):
            gather_copy(blk, lax.rem(blk, n_in), j).start()

    def wait_gather(blk):
        for j in range(EXP_MB):
            gather_copy(blk, lax.rem(blk, n_in), j).wait()

    def start_scatter(blk):
        for j in range(EXP_MB):
            scatter_copy(blk, lax.rem(blk + n_out, n_out), j).start()

    def wait_scatter(slot):
        for j in range(EXP_MB):
            scatter_copy(0, slot, j).wait()

    @pl.when(e == 0)
    def _():
        ybuf[...] = jnp.zeros_like(ybuf)
        for ahead in range(EXP_LEAD):
            start_gather(ahead)
        spare_row0 = ys_hbm.shape[0] - (n_out + 1) * EXP_BM
        for slot in range(n_out - 1):
            for j in range(EXP_MB):
                spare = pl.ds(spare_row0 + (slot * EXP_MB + j) * MB_ROWS, MB_ROWS)
                pltpu.make_async_copy(ybuf.at[slot, j * MB_ROWS:(j + 1) * MB_ROWS, :], ys_hbm.at[spare, :],
                                      out_sem.at[slot]).start()

    de = wg_ref.shape[1]
    wgub[:, :de] = wg_ref[...].astype(BF16)
    wgub[:, de:] = wu_ref[...].astype(BF16)
    wdb[...] = wd_ref[...].astype(BF16)

    def block(b, carry):
        slot = lax.rem(b, n_out)
        wait_gather(b)
        wait_scatter(slot)
        x = xbuf[lax.rem(b, n_in)]
        gu = _dot(x, wgub[...])
        act = (_silu(gu[:, :de]) * gu[:, de:]).astype(BF16)
        start_gather(b + EXP_LEAD)
        start_scatter(b - 1)
        ybuf[slot] = _dot(act, wdb[...]).astype(BF16)
        return carry

    first = bstart_ref[e]
    lax.fori_loop(first, first + nb_ref[e], block, 0)

    @pl.when(e == pl.num_programs(0) - 1)
    def _():
        total = first + nb_ref[e]
        start_scatter(total - 1)
        for ahead in range(EXP_LEAD):
            wait_gather(total + ahead)
        for slot in range(n_out):
            wait_scatter(slot)


def _expert_ffn(xs, bstart, nb, src, dst, wg, wu, wd):
    rows, d = xs.shape
    de = wg.shape[2]
    grid_spec = pltpu.PrefetchScalarGridSpec(
        num_scalar_prefetch=4,
        grid=(wg.shape[0],),
        in_specs=[pl.BlockSpec(memory_space=pl.ANY),
                  pl.BlockSpec((None, d, de), lambda i, bs, bn, sr, ds: (i, 0, 0)),
                  pl.BlockSpec((None, d, de), lambda i, bs, bn, sr, ds: (i, 0, 0)),
                  pl.BlockSpec((None, de, d), lambda i, bs, bn, sr, ds: (i, 0, 0))],
        out_specs=pl.BlockSpec(memory_space=pl.ANY),
        scratch_shapes=[pltpu.VMEM((EXP_LEAD + 1, EXP_BM, d), BF16), pltpu.VMEM((EXP_OUT, EXP_BM, d), BF16),
                        pltpu.VMEM((d, 2 * de), BF16), pltpu.VMEM((de, d), BF16),
                        pltpu.SemaphoreType.DMA((EXP_LEAD + 1,)), pltpu.SemaphoreType.DMA((EXP_OUT,))],
    )
    return pl.pallas_call(
        _expert_kernel,
        grid_spec=grid_spec,
        out_shape=jax.ShapeDtypeStruct((rows, d), BF16),
        input_output_aliases={4: 0},
        compiler_params=_cparams(("arbitrary",)),
        name="moe_experts",
    )(bstart, nb, src, dst, xs, wg, wu, wd)


def _combine_kernel(ys_ref, sel_ref, wd_ref, h2_ref, x1_ref, wsg_ref, wsu_ref, wsd_ref, npost_ref, gate2_ref,
                    o_ref, c_ref, acc_ref):
    tm = h2_ref.shape[0]
    sel = sel_ref[...]
    _, _, cnt, offmb, rank = _tile_positions(sel)
    no_rank = 1.5 * tm
    lhs = jnp.concatenate([jnp.where(sel > 0.0, rank, no_rank).T, wd_ref[...].T], axis=1).astype(BF16)
    start = offmb[:, :LANES] * MB_ROWS
    end = start + cnt[:, :LANES]
    zeros = jnp.zeros((N_EXPERTS, LANES), BF16)

    def build(cc):
        r_e = (cc * LANES + _iota((N_EXPERTS, LANES), 1)).astype(F32)
        owned = (r_e >= start) & (r_e < end)
        owner = owned.astype(BF16)
        local = r_e[0:1, :] - jnp.sum(jnp.where(owned, start, 0.0), axis=0, keepdims=True)
        rhs = jnp.concatenate([jnp.concatenate([owner, zeros], axis=1),
                               jnp.concatenate([zeros, owner], axis=1)], axis=0)
        got = _dot(lhs, rhs)
        c_ref[:, cc * LANES:(cc + 1) * LANES] = jnp.where(got[:, :LANES] == local, got[:, LANES:], 0.0).astype(BF16)

    n_cc = TILE_ROWS // LANES
    sure = TILE_ROWS - TAIL_CHUNKS * 256
    for cc in range(sure // LANES):
        build(cc)
    h2 = h2_ref[...]
    shared = _dot((_silu(_dot(h2, wsg_ref[...])) * _dot(h2, wsu_ref[...])).astype(BF16), wsd_ref[...])
    acc_ref[...] = shared + _dot(c_ref[:, :sure], ys_ref[:sure, :])
    nmb = jnp.floor((cnt[:, 0:1] + (MB_ROWS - 1)) * (1.0 / MB_ROWS))

    @pl.when(jnp.sum(nmb) * MB_ROWS > sure)
    def _():
        for cc in range(sure // LANES, n_cc):
            build(cc)
        acc_ref[...] += _dot(c_ref[:, sure:], ys_ref[sure:, :])

    o_ref[...] = x1_ref[...] + gate2_ref[...] * _rms(acc_ref[...], npost_ref[...])


def _combine(ys, sel_t, wd_t, h2, x1, wsg, wsu, wsd, npost, gate2, seq_len):
    t, d = h2.shape
    tm = MOE_TM
    nt = t // tm
    per_seq = seq_len // tm
    ds = wsg.shape[1]
    const = functools.partial(pl.BlockSpec, pipeline_mode=pl.Buffered(1))
    emap = pl.BlockSpec((N_EXPERTS, tm), lambda i: (0, i))
    return pl.pallas_call(
        _combine_kernel,
        grid=(nt,),
        in_specs=[pl.BlockSpec((TILE_ROWS, d), lambda i: (i, 0)), emap, emap,
                  pl.BlockSpec((tm, d), lambda i: (i, 0)), pl.BlockSpec((tm, d), lambda i: (i, 0)),
                  const((d, ds), lambda i: (0, 0)), const((d, ds), lambda i: (0, 0)), const((ds, d), lambda i: (0, 0)),
                  const((1, d), lambda i: (0, 0)),
                  pl.BlockSpec((None, 1, d), lambda i: (i // per_seq, 0, 0))],
        out_specs=pl.BlockSpec((tm, d), lambda i: (i, 0)),
        out_shape=jax.ShapeDtypeStruct((t, d), F32),
        scratch_shapes=[pltpu.VMEM((tm, TILE_ROWS), BF16), pltpu.VMEM((tm, d), F32)],
        compiler_params=_cparams(("arbitrary",)),
        name="moe_combine",
    )(ys, sel_t, wd_t, h2, x1, wsg, wsu, wsd, npost, gate2)


def _rope_tables(n):
    rows = n // GRID_W
    pos_r = jnp.repeat(jnp.arange(rows, dtype=F32), GRID_W)
    pos_c = jnp.tile(jnp.arange(GRID_W, dtype=F32), rows)
    n_freq = HEAD_DIM // 4
    inv = ROPE_BASE ** (-jnp.arange(n_freq, dtype=F32) / n_freq)
    ang = jnp.concatenate([pos_r[:, None] * inv, pos_c[:, None] * inv], axis=-1)
    cos, sin = jnp.cos(ang), jnp.sin(ang)
    return jnp.concatenate([cos, cos], axis=-1), jnp.concatenate([-sin, sin], axis=-1)


def kernel(x, c, ctx, c_ctx, w_mod, b_mod, norm_mix_pre, norm_mix_post, norm_ffn_pre, norm_ffn_post, w_in, gdn_conv, ret_log_decay, gdn_a_log, gdn_dt_bias, ret_gn_w, gdn_norm_w, w_ret_out, w_gdn_out, w_o, w_router, router_bias, w_gate, w_up, w_down, w_sh_gate, w_sh_up, w_sh_down):
    b, n, d = x.shape
    depth = w_mod.shape[0]
    assert depth == 1, "single-layer block"
    assert all(n % min(tile, n) == 0 for tile in (PROJ_TM, MIXOUT_TM, MOE_TM)) and n % MOE_TM == 0
    assert ctx.shape[1] % CHUNK == 0 and (EXP_OUT + 1) * EXP_BM <= TILE_ROWS
    assert _TILE_MB < 255

    rows = -(-(b + 1) // 8) * 8
    cvec = jnp.zeros((rows, d), F32).at[:b].set(c).at[b].set(c_ctx)
    mod = _modulation(cvec, w_mod[0], b_mod[0][None, :])
    sh1, sc1, g1, sh2, sc2, g2 = [mod[:b, k * d:(k + 1) * d][:, None, :] for k in range(6)]
    ctx_shift = jnp.broadcast_to(mod[b, 0:d][None, None, :], (b, 1, d))
    ctx_scale = jnp.broadcast_to(mod[b, d:2 * d][None, None, :], (b, 1, d))

    w_in0 = w_in[0]
    n_main = 4 * MIX_W
    w_state = w_in0[:, :n_main].astype(BF16)
    slots = ((0, 0), (0, 1), (1, 0), (1, 1), (0, 0), (0, 1), (0, 0), (0, 1))
    gab_cols = n_main + jnp.array([[ab * 2 * N_HEADS + dr * N_HEADS + hh for ab, dr in slots]
                                   for hh in range(N_HEADS)], I32)
    w_gab = w_in0[:, gab_cols.reshape(-1)].astype(BF16)
    n_state = n_main + N_GAB
    w_query = w_in0[:, n_state:].astype(BF16)
    a_coef = -jnp.exp(gdn_a_log[0].astype(F32))
    dtb = gdn_dt_bias[0].astype(F32)
    is_alpha = jnp.array([ab == 0 for ab, _ in slots])
    dirs = jnp.array([dr for _, dr in slots], I32)
    prow = jnp.stack([jnp.where(is_alpha[None, :], a_coef.T[:, dirs], 0.0).reshape(-1),
                      jnp.where(is_alpha[None, :], dtb.T[:, dirs], 0.0).reshape(-1)], axis=0)
    first2 = jnp.tile(jnp.arange(8) < 2, N_HEADS)
    w_gab_t = w_gab.T * first2[:, None].astype(BF16)
    pcol = prow.T * first2[:, None]
    gain_mix = norm_mix_pre[0][None, :]
    cos2, sin2 = _rope_tables(n)

    state_kinds = (_ROPE_SCALED, _PLAIN, _PLAIN, _PLAIN)
    query_kinds = (_ROPE, _PLAIN, _PLAIN, _PLAIN, _SIGMOID, _SIGMOID, _SIGMOID, _SIGMOID)
    lg = ret_log_decay[0].astype(F32)
    conv = gdn_conv[0].astype(F32)

    lc = ctx.shape[1]
    cfe, (ccols, cgrow, cgl) = _in_projection(
        ctx, gain_mix, ctx_scale, ctx_shift, (w_state,), w_gab, w_gab_t, prow, pcol,
        cos2[:lc], sin2[:lc], state_kinds, rope=False)
    cfeats = dict(zip(("rk", "rv", "gk", "gv"), cfe))
    zero_state = jnp.zeros((b, N_HEADS, 4, HEAD_DIM, HEAD_DIM), F32)
    init = _mixers(lg, cgl, cfeats, (ccols, cgrow), conv, zero_state, None, None, with_query=False)

    fe, (cols, grow, gl) = _in_projection(
        x, gain_mix, sc1, sh1, (w_state, w_query), w_gab, w_gab_t, prow, pcol, cos2, sin2,
        state_kinds + query_kinds, rope=True)
    feats = dict(zip(("rk", "rv", "gk", "gv", "rq", "rg", "gq", "gz"), fe[:8]))
    gates = fe[8:]
    y_ret, y_gdn = _mixers(lg, gl, feats, (cols, grow), conv, init,
                           ret_gn_w[0][None, :], gdn_norm_w[0][None, :], with_query=True)
    x1, h2, sel_t, wd_t = _mixer_out(
        x, y_ret, y_gdn, gates, w_ret_out[0].astype(BF16), w_gdn_out[0].astype(BF16), w_o[0].astype(BF16),
        norm_mix_post[0][None, :], g1, norm_ffn_pre[0][None, :], sc2, sh2,
        w_router[0].T.astype(F32), router_bias[0].astype(F32)[:, None])

    t = b * n
    h2f = h2.reshape(t, d)
    xs, cnt = _dispatch(h2f, sel_t)
    nt = t // MOE_TM
    n_blk = nt * _TILE_MB // EXP_MB + N_EXPERTS + EXP_LEAD
    bstart, nb, src, dst = _expert_plan(cnt[:, 0, :].astype(I32), n_blk)
    ys = _expert_ffn(xs, bstart, nb, src, dst, w_gate[0], w_up[0], w_down[0])
    out = _combine(ys, sel_t, wd_t, h2f, x1.reshape(t, d), w_sh_gate[0].astype(BF16), w_sh_up[0].astype(BF16),
                   w_sh_down[0].astype(BF16), norm_ffn_post[0][None, :], g2, n)
    return out.reshape(b, n, d)
```

```python
import functools
import math

import jax
import jax.numpy as jnp
from jax import lax
from jax.experimental import pallas as pl
from jax.experimental.pallas import tpu as pltpu

F32 = jnp.float32
BF16 = jnp.bfloat16
I32 = jnp.int32
HIGHEST = lax.Precision.HIGHEST

N_HEADS = 4
HEAD_DIM = 128
MIX_W = N_HEADS * HEAD_DIM
CHUNK = 128
SHORT_CONV = 3
ROPE_BASE = 10000.0
GRID_W = 64
N_EXPERTS = 64
TOP_K = 8
N_GROUPS = 8
TOPK_GROUPS = 4
GROUP_SIZE = N_EXPERTS // N_GROUPS
ROUTED_SCALE = 2.5
EPS = 1e-6
N_GAB = 4 * N_HEADS

LANES = 128
BF16_TILE_ROWS = 16
VMEM_LIMIT_BYTES = 56 * 1024 * 1024

PROJ_TM = 512
MIXOUT_TM = 1024
MOE_TM = 256
MB_ROWS = BF16_TILE_ROWS
_MB_SHIFT = MB_ROWS.bit_length() - 1
_TILE_MB = -(-(TOP_K * MOE_TM // MB_ROWS + N_EXPERTS * (MB_ROWS - 1) // MB_ROWS + 1) // 8) * 8
TILE_ROWS = _TILE_MB * MB_ROWS
TAIL_CHUNKS = 2
EXP_BM = 512
EXP_MB = EXP_BM // MB_ROWS
EXP_LEAD = 3
EXP_OUT = 3


def _cparams(sem):
    return pltpu.CompilerParams(dimension_semantics=sem, vmem_limit_bytes=VMEM_LIMIT_BYTES)


def _sigmoid(v):
    return 0.5 * jnp.tanh(0.5 * v) + 0.5


def _silu(v):
    return v * _sigmoid(v)


def _softplus(v):
    return jnp.maximum(v, 0.0) + jnp.log1p(jnp.exp(-jnp.abs(v)))


def _iota(shape, dim):
    return lax.broadcasted_iota(I32, shape, dim)


def _dot(a, b, **kw):
    return jnp.dot(a, b, preferred_element_type=F32, **kw)


def _dot_nt(a, b, **kw):
    return lax.dot_general(a, b, (((1,), (1,)), ((), ())), preferred_element_type=F32, **kw)


def _dot_tn(a, b, **kw):
    return lax.dot_general(a, b, (((0,), (0,)), ((), ())), preferred_element_type=F32, **kw)


def _rms(v, gain):
    return v * lax.rsqrt(jnp.mean(v * v, axis=-1, keepdims=True) + EPS) * gain


def _mod_kernel(c_ref, w_ref, b_ref, o_ref):
    o_ref[...] = _dot(_silu(c_ref[...]), w_ref[...], precision=HIGHEST) + b_ref[...]


def _modulation(cvec, w_mod, b_mod):
    rows, d = cvec.shape
    n = w_mod.shape[1]
    tn = 1024
    return pl.pallas_call(
        _mod_kernel,
        grid=(n // tn,),
        in_specs=[pl.BlockSpec((rows, d), lambda j: (0, 0)),
                  pl.BlockSpec((d, tn), lambda j: (0, j)),
                  pl.BlockSpec((1, tn), lambda j: (0, j))],
        out_specs=pl.BlockSpec((rows, tn), lambda j: (0, j)),
        out_shape=jax.ShapeDtypeStruct((rows, n), F32),
        compiler_params=_cparams(("arbitrary",)),
        name="adaln_modulation",
    )(cvec, w_mod, b_mod)


_PLAIN, _ROPE, _ROPE_SCALED, _SIGMOID = 0, 1, 2, 3


def _proj_kernel(x_ref, gain_ref, sc_ref, sh_ref, *refs, kinds, rope, n_w, cast_experts):
    w_refs = refs[:n_w]
    wg_ref, wgt_ref, prow_ref, pcol_ref, cos_ref, sin_ref = refs[n_w:n_w + 6]
    out_refs = refs[n_w + 6:]
    if cast_experts:
        eg_ref, eu_ref, ed_ref = out_refs[:3]
        out_refs = out_refs[3:]
        egu_out, ed_out = out_refs[-2:]
        out_refs = out_refs[:-2]
        de = eg_ref.shape[2]
        egu_out[:, :, :de] = eg_ref[...].astype(BF16)
        egu_out[:, :, de:] = eu_ref[...].astype(BF16)
        ed_out[...] = ed_ref[...].astype(BF16)
    groups = [(w, c) for w in w_refs for c in range(0, w.shape[1], MIX_W)]
    n_feat = len(kinds)
    feat_refs = out_refs[:n_feat]
    cols_ref, grow_ref, gl_ref = out_refs[n_feat:]
    tm = x_ref.shape[0]

    x = x_ref[...]
    h = (_rms(x, gain_ref[...]) * (1.0 + sc_ref[...]) + sh_ref[...]).astype(BF16)

    r_i = _iota((CHUNK, CHUNK), 0)
    c_i = _iota((CHUNK, CHUNK), 1)
    lower_incl = (c_i <= r_i).astype(BF16)
    upper_incl = (c_i >= r_i).astype(BF16)

    def prefix_rows(v):
        hi, lo = _split_bf16(v)
        return _dot(lower_incl, hi) + _dot(lower_incl, lo)

    def prefix_lanes(v):
        hi, lo = _split_bf16(v)
        return _dot(hi, upper_incl) + _dot(lo, upper_incl)

    nc = N_HEADS * 8
    pg = _dot(h, wg_ref[...])
    colt = _iota((tm, nc), 1) & 7
    la = jnp.where((colt == 2) | (colt == 3), 0.0, prow_ref[0:1, :] * _softplus(pg + prow_ref[1:2, :]))
    beta = _sigmoid(pg)
    colc = _iota((CHUNK, nc), 1) & 7
    fwd_col = (colc & 1) == 0
    for c in range(tm // CHUNK):
        sl = slice(c * CHUNK, (c + 1) * CHUNK)
        la_c = la[sl]
        pre = prefix_rows(la_c)
        suf = pre[CHUNK - 1:CHUNK, :] - pre + la_c
        g_c = jnp.where(fwd_col, pre, suf)
        rest = jnp.where(fwd_col, suf, pre) - la_c
        vals = jnp.where(colc < 2, g_c, jnp.where(colc < 4, beta[sl], jnp.where(
            colc < 6, jnp.exp(g_c), jnp.exp(rest))))
        for hh in range(N_HEADS):
            cols_ref[hh, sl, :] = vals[:, 8 * hh:8 * hh + 8]
        gl_ref[c:c + 1, :] = jnp.exp(g_c[0:1, :] + rest[0:1, :])

    pgt = _dot_nt(wgt_ref[...], h)
    rowq = _iota((N_HEADS * 8, tm), 0) & 7
    lat = jnp.where(rowq < 2, pcol_ref[:, 0:1] * _softplus(pgt + pcol_ref[:, 1:2]), 0.0)
    rowc = _iota((N_HEADS * 8, CHUNK), 0) & 7
    for c in range(tm // CHUNK):
        sl = slice(c * CHUNK, (c + 1) * CHUNK)
        lat_c = lat[:, sl]
        pre_t = prefix_lanes(lat_c)
        suf_t = pre_t[:, CHUNK - 1:CHUNK] - pre_t + lat_c
        grow_ref[:, :, sl] = jnp.where(rowc == 0, pre_t, suf_t).reshape(N_HEADS, 8, CHUNK)

    if rope:
        cos2 = cos_ref[...]
        sin2 = sin_ref[...]

    for g, kind in enumerate(kinds):
        w_ref, c0 = groups[g]
        p = _dot(h, w_ref[:, c0:c0 + MIX_W])
        if kind == _SIGMOID:
            p = _sigmoid(p)
        elif kind in (_ROPE, _ROPE_SCALED) and rope:
            heads = []
            for hh in range(N_HEADS):
                t = p[:, hh * HEAD_DIM:(hh + 1) * HEAD_DIM]
                heads.append(t * cos2 + pltpu.roll(t, HEAD_DIM // 2, 1) * sin2)
            p = jnp.concatenate(heads, axis=1)
        if kind == _ROPE_SCALED:
            p = p * (HEAD_DIM ** -0.5)
        feat_refs[g][...] = p.astype(feat_refs[g].dtype)


def _in_projection(x, gain, scale, shift, w_mains, w_gab, w_gab_t, prow, pcol, cos2, sin2, kinds, rope,
                   expert_w=None):
    b, l, d = x.shape
    tm = min(PROJ_TM, l)
    tiles = l // tm
    n_chunk = tm // CHUNK
    feat_shapes = [jax.ShapeDtypeStruct((b, l, MIX_W), BF16) for _ in kinds]
    feat_specs = [pl.BlockSpec((None, tm, MIX_W), lambda i, j: (i, j, 0)) for _ in kinds]
    out_shape = feat_shapes + [jax.ShapeDtypeStruct((b, N_HEADS, l, 8), F32),
                               jax.ShapeDtypeStruct((b, N_HEADS, 8, l), F32),
                               jax.ShapeDtypeStruct((b, tiles, n_chunk, N_HEADS * 8), F32)]
    out_specs = feat_specs + [pl.BlockSpec((None, N_HEADS, tm, 8), lambda i, j: (i, 0, j, 0)),
                              pl.BlockSpec((None, N_HEADS, 8, tm), lambda i, j: (i, 0, 0, j)),
                              pl.BlockSpec((None, None, n_chunk, N_HEADS * 8), lambda i, j: (i, j, 0, 0))]
    const = functools.partial(pl.BlockSpec, pipeline_mode=pl.Buffered(1))
    in_specs = [
        pl.BlockSpec((None, tm, d), lambda i, j: (i, j, 0)),
        const((1, d), lambda i, j: (0, 0)),
        pl.BlockSpec((None, 1, d), lambda i, j: (i, 0, 0)),
        pl.BlockSpec((None, 1, d), lambda i, j: (i, 0, 0)),
    ] + [const(w.shape, lambda i, j: (0, 0)) for w in w_mains] + [
        const((d, N_HEADS * 8), lambda i, j: (0, 0)),
        const((N_HEADS * 8, d), lambda i, j: (0, 0)),
        const((2, N_HEADS * 8), lambda i, j: (0, 0)),
        const((N_HEADS * 8, 2), lambda i, j: (0, 0)),
        pl.BlockSpec((tm, HEAD_DIM), lambda i, j: (j, 0)),
        pl.BlockSpec((tm, HEAD_DIM), lambda i, j: (j, 0)),
    ]
    extra_args = ()
    if expert_w is not None:
        eg, eu, ed = expert_w
        n_e, _, de = eg.shape
        assert n_e % (b * tiles) == 0, "the expert weights are cast in equal shares by the grid steps"
        per = n_e // (b * tiles)
        share = lambda i, j: (i * tiles + j, 0, 0)
        in_specs += [pl.BlockSpec((per, d, de), share), pl.BlockSpec((per, d, de), share),
                     pl.BlockSpec((per, de, d), share)]
        out_specs += [pl.BlockSpec((per, d, 2 * de), share), pl.BlockSpec((per, de, d), share)]
        out_shape += [jax.ShapeDtypeStruct((n_e, d, 2 * de), BF16), jax.ShapeDtypeStruct((n_e, de, d), BF16)]
        extra_args = (eg, eu, ed)
    outs = pl.pallas_call(
        functools.partial(_proj_kernel, kinds=tuple(kinds), rope=rope, n_w=len(w_mains),
                          cast_experts=expert_w is not None),
        grid=(b, tiles),
        in_specs=in_specs,
        out_specs=out_specs,
        out_shape=out_shape,
        compiler_params=_cparams(("arbitrary", "arbitrary")),
        name="in_projection_rope" if rope else "in_projection_ctx",
    )(x, gain, scale, shift, *w_mains, w_gab, w_gab_t, prow, pcol, cos2, sin2, *extra_args)
    feats = outs[:len(kinds)]
    cols, grow, gl = outs[len(kinds):len(kinds) + 3]
    decay = (cols, grow, gl.reshape(b * (l // CHUNK), N_HEADS * 8))
    if expert_w is not None:
        return feats, decay, tuple(outs[len(kinds) + 3:])
    return feats, decay


def _unit_triangular_inverses(mats, lowers):
    r = _iota((CHUNK, CHUNK), 0)
    c = _iota((CHUNK, CHUNK), 1)
    eye = (r == c).astype(F32)
    invs = [eye - jnp.where((r >> 1) == (c >> 1), a, 0.0) for a in mats]
    for level in range(1, int(math.log2(CHUNK))):
        s = 1 << level
        mask = ((r >> (level + 1)) == (c >> (level + 1))) & ((r >> level) != (c >> level))
        invb = [inv.astype(BF16) for inv in invs]
        offs = [jnp.where(mask, a, 0.0).astype(BF16) for a in mats]
        if s < 8:
            half = [_dot(off, ib).astype(BF16) for off, ib in zip(offs, invb)]
            invs = [inv - _dot(ib, hf) for inv, ib, hf in zip(invs, invb, half)]
            continue
        def rows_of(x, lower, moving):
            first = s if (lower == moving) else 0
            return [x[g * 2 * s + first:g * 2 * s + first + s] for g in range(CHUNK // (2 * s))]

        half = [_dot(jnp.concatenate(rows_of(off, lo, True), axis=0), ib).astype(BF16)
                for off, ib, lo in zip(offs, invb, lowers)]
        zero = jnp.zeros((s, CHUNK), BF16)
        full = []
        for hf, lo in zip(half, lowers):
            pieces = []
            for g in range(CHUNK // (2 * s)):
                piece = hf[g * s:(g + 1) * s]
                pieces += [zero, piece] if lo else [piece, zero]
            full.append(jnp.concatenate(pieces, axis=0))
        corr = [_dot(jnp.concatenate(rows_of(ib, lo, True), axis=0), hf) for ib, hf, lo in zip(invb, full, lowers)]
        new = []
        for inv, cr, lo in zip(invs, corr, lowers):
            keep = rows_of(inv, lo, False)
            moved = [m - cr[g * s:(g + 1) * s] for g, m in enumerate(rows_of(inv, lo, True))]
            pieces = []
            for k, m in zip(keep, moved):
                pieces += [k, m] if lo else [m, k]
            new.append(jnp.concatenate(pieces, axis=0))
        invs = new
    return invs


def _mixer_kernel(*refs, seq_len, with_query):
    n_chunk = seq_len // CHUNK
    if with_query:
        (lg_ref, gl_ref, rk_ref, rv_ref, gk_ref, gv_ref, rq_ref, gq_ref, rg_ref, gz_ref,
         cols_ref, grow_ref, cq_ref, ck_ref, cv_ref, s0_ref, gnw_ref, rmsw_ref,
         yret_ref, ygdn_ref,
         qes, oret, ogdn, ubuf, wbuf, pbuf, nbuf, cbuf, kvbuf, rtile, state) = refs
    else:
        (lg_ref, gl_ref, rk_ref, rv_ref, gk_ref, gv_ref,
         cols_ref, grow_ref, ck_ref, cv_ref, s0_ref,
         sfin_ref,
         nbuf, cbuf, kvbuf, rtile, state) = refs
    bi = pl.program_id(0)
    hi = pl.program_id(1)

    row = _iota((CHUNK, CHUNK), 0)
    colm = _iota((CHUNK, CHUNK), 1)
    rowf = row.astype(F32)
    colf = colm.astype(F32)

    def conv_chunk(src_ref, w_ref, n):
        s = pl.multiple_of(n * CHUNK, CHUNK)
        x = src_ref[pl.ds(s, CHUNK), :].astype(F32)
        ps = pl.multiple_of(jnp.maximum(s - BF16_TILE_ROWS, 0), BF16_TILE_ROWS)
        ns = pl.multiple_of(jnp.minimum(s + CHUNK, seq_len - BF16_TILE_ROWS), BF16_TILE_ROWS)
        prev_row = src_ref[pl.ds(ps, BF16_TILE_ROWS), :].astype(F32)[BF16_TILE_ROWS - 1:BF16_TILE_ROWS, :]
        next_row = src_ref[pl.ds(ns, BF16_TILE_ROWS), :].astype(F32)[0:1, :]
        prev_row = prev_row * jnp.where(n > 0, 1.0, 0.0)
        next_row = next_row * jnp.where(n < n_chunk - 1, 1.0, 0.0)
        xp = jnp.where(row == 0, jnp.broadcast_to(prev_row, (CHUNK, HEAD_DIM)), pltpu.roll(x, 1, 0))
        xn = jnp.where(row == CHUNK - 1, jnp.broadcast_to(next_row, (CHUNK, HEAD_DIM)), pltpu.roll(x, CHUNK - 1, 0))
        return _silu(w_ref[0:1, :] * xp + w_ref[1:2, :] * x + w_ref[2:3, :] * xn)

    def l2n(v):
        return v * lax.rsqrt(jnp.sum(v * v, axis=-1, keepdims=True) + EPS)

    for d in range(2):
        lg = lg_ref[d, hi]
        if d == 0:
            dist, pos_q, pos_k = rowf - colf, rowf + 1.0, (CHUNK - 1.0) - rowf
        else:
            dist, pos_q, pos_k = colf - rowf, CHUNK - rowf, rowf
        rtile[4 * d + 0] = jnp.where(dist >= 0, jnp.exp(lg * jnp.maximum(dist, 0.0)), 0.0)
        rtile[4 * d + 1] = jnp.exp(lg * pos_q)
        rtile[4 * d + 2] = jnp.exp(lg * pos_k)
        rtile[4 * d + 3] = jnp.exp(lg * jnp.full((CHUNK, CHUNK), float(CHUNK), F32))

    state[...] = s0_ref[...]

    def bcast_col(cols, j):
        return jnp.broadcast_to(cols[:, j:j + 1], (CHUNK, CHUNK))

    pre_chunks = min(8, n_chunk)

    def prepass(m, carry):
        chunks = [m * pre_chunks + j for j in range(pre_chunks)]
        chunk_cs = [pl.ds(pl.multiple_of(n * CHUNK, CHUNK), CHUNK) for n in chunks]
        conv_k, conv_v, conv_q = [], [], []
        for n, cs in zip(chunks, chunk_cs):
            conv_k.append(l2n(conv_chunk(gk_ref, ck_ref, n)).astype(BF16))
            conv_v.append(conv_chunk(gv_ref, cv_ref, n).astype(BF16))
            if with_query:
                conv_q.append((l2n(conv_chunk(gq_ref, cq_ref, n)) * (HEAD_DIM ** -0.5)).astype(BF16))
                ogdn[cs, :] = jnp.zeros((CHUNK, HEAD_DIM), F32)

        jobs = [(j, d) for j in range(pre_chunks) for d in range(2)]
        css = [chunk_cs[j] for j, _ in jobs]
        slots = [d * n_chunk + chunks[j] for j, d in jobs]
        dirs = [d for _, d in jobs]
        ks = [rk_ref[cs, :] for cs in css]
        vs = [rv_ref[cs, :] for cs in css]
        if with_query:
            scs = [(_dot_nt(rq_ref[cs, :], k) * rtile[4 * d + 0]).astype(BF16) for cs, k, d in zip(css, ks, dirs)]
            outs = [_dot(sc, v) for sc, v in zip(scs, vs)]
            for j, cs in enumerate(chunk_cs):
                oret[cs, :] = outs[2 * j] + outs[2 * j + 1]
        kvs = [_dot_tn((k.astype(F32) * rtile[4 * d + 2]).astype(BF16), v) for k, v, d in zip(ks, vs, dirs)]
        for slot, kv in zip(slots, kvs):
            kvbuf[slot] = kv
        ks = [conv_k[j] for j, _ in jobs]
        vs = [conv_v[j] for j, _ in jobs]
        colss = [cols_ref[cs, :] for cs in css]
        betas = [bcast_col(cols, 2 + d) for cols, d in zip(colss, dirs)]
        incls = [(row >= colm) if d == 0 else (row <= colm) for d in dirs]
        stricts = [(row > colm) if d == 0 else (row < colm) for d in dirs]
        decs = [jnp.exp(jnp.where(incl, bcast_col(cols, d) - jnp.broadcast_to(grow_ref[d:d + 1, cs], (CHUNK, CHUNK)), 0.0))
                for cols, d, cs, incl in zip(colss, dirs, css, incls)]
        kks = [_dot_nt(k, k) for k in ks]
        mats = [kk * beta * jnp.where(strict, dec, 0.0) for kk, beta, strict, dec in zip(kks, betas, stricts, decs)]
        tinvs = [t.astype(BF16) for t in _unit_triangular_inverses(mats, [d == 0 for d in dirs])]
        rhs = [jnp.concatenate([(beta * v.astype(F32)).astype(BF16),
                                (beta * bcast_col(cols, 4 + d) * k.astype(F32)).astype(BF16)], axis=1)
               for beta, v, cols, d, k in zip(betas, vs, colss, dirs, ks)]
        uws = [_dot(t, r) for t, r in zip(tinvs, rhs)]
        us = [uw[:, :HEAD_DIM] for uw in uws]
        ws = [uw[:, HEAD_DIM:].astype(BF16) for uw in uws]
        kts = [(k.astype(F32) * bcast_col(cols, 6 + d)).astype(BF16) for k, cols, d in zip(ks, colss, dirs)]
        ncs = [_dot_tn(kt, uw.astype(BF16)) for kt, uw in zip(kts, uws)]
        for slot, nc_mat in zip(slots, ncs):
            cbuf[slot] = nc_mat[:, :HEAD_DIM]
            nbuf[slot] = nc_mat[:, HEAD_DIM:].astype(BF16)
        if with_query:
            for slot, u, w in zip(slots, us, ws):
                ubuf[slot] = u.astype(BF16)
                wbuf[slot] = w
            qks = [_dot_nt(conv_q[j], k) for (j, _), k in zip(jobs, ks)]
            for slot, qk, incl, dec in zip(slots, qks, incls, decs):
                pbuf[slot] = (qk * jnp.where(incl, dec, 0.0)).astype(BF16)
            for (j, d), cs, cols in zip(jobs, css, colss):
                qes[d, cs, :] = (conv_q[j].astype(F32) * bcast_col(cols, 4 + d)).astype(BF16)
        return carry

    lax.fori_loop(0, n_chunk // pre_chunks, prepass, 0)

    def finish(cs):
        ro = oret[cs, :]
        rc = ro - jnp.mean(ro, axis=-1, keepdims=True)
        ry = rc * lax.rsqrt(jnp.mean(rc * rc, axis=-1, keepdims=True) + EPS)
        yret_ref[cs, :] = (ry * gnw_ref[...] * _silu(rg_ref[cs, :].astype(F32))).astype(BF16)
        go = ogdn[cs, :]
        gy = go * lax.rsqrt(jnp.mean(go * go, axis=-1, keepdims=True) + EPS)
        ygdn_ref[cs, :] = (gy * rmsw_ref[...] * _silu(gz_ref[cs, :].astype(F32))).astype(BF16)

    def chunk_ids(n):
        nds = [n, n_chunk - 1 - n]
        css = [pl.ds(nd * CHUNK if isinstance(nd, int) else pl.multiple_of(nd * CHUNK, CHUNK), CHUNK) for nd in nds]
        slots = [d * n_chunk + nd for d, nd in enumerate(nds)]
        return nds, css, slots

    def late_outputs(n, vnb, finishing):
        _, css, slots = chunk_ids(n)
        for d in range(2):
            ogdn[css[d], :] += _dot(pbuf[slots[d]], vnb[d])
            if finishing:
                finish(css[d])

    def scan_step(n, vnb_prev, has_prev, finish_prev):
        nds, css, slots = chunk_ids(n)
        ret_st = [state[d] for d in range(2)]
        gdn_st = [state[2 + d] for d in range(2)]
        gdn_stb = [st.astype(BF16) for st in gdn_st]
        shrink = [_dot(nbuf[slot], stb) for slot, stb in zip(slots, gdn_stb)]
        for d in range(2):
            state[2 + d] = gl_ref[bi * n_chunk + nds[d], 8 * hi + d] * gdn_st[d] - shrink[d] + cbuf[slots[d]]
            state[d] = rtile[4 * d + 3] * ret_st[d] + kvbuf[slots[d]]
        if not with_query:
            return vnb_prev
        vnb = tuple((ubuf[slot].astype(F32) - _dot(wbuf[slot], stb)).astype(BF16)
                    for slot, stb in zip(slots, gdn_stb))
        for d in range(2):
            oret[css[d], :] += _dot(rq_ref[css[d], :], ret_st[d].astype(BF16)) * rtile[4 * d + 1]
            ogdn[css[d], :] += _dot(qes[d, css[d], :], gdn_stb[d])
        if has_prev:
            late_outputs(n - 1, vnb_prev, finish_prev)
        return vnb

    if with_query:
        half = n_chunk // 2
        zero = jnp.zeros((CHUNK, HEAD_DIM), BF16)
        vnb = scan_step(0, (zero, zero), has_prev=False, finish_prev=False)
        vnb = lax.fori_loop(1, half + 1, functools.partial(scan_step, has_prev=True, finish_prev=False), vnb)
        vnb = lax.fori_loop(half + 1, n_chunk, functools.partial(scan_step, has_prev=True, finish_prev=True), vnb)
        late_outputs(n_chunk - 1, vnb, True)
    else:
        lax.fori_loop(0, n_chunk, functools.partial(scan_step, has_prev=False, finish_prev=False), 0)
        sfin_ref[...] = state[...]


def _mixers(lg, gl, feats, dec, conv, s0, gnw, rmsw, with_query):
    cols, grow = dec
    b, l, _ = feats["rk"].shape
    n_slot = 2 * (l // CHUNK)
    assert (l // CHUNK) % min(8, l // CHUNK) == 0
    smem = pl.BlockSpec(memory_space=pltpu.SMEM)
    head = pl.BlockSpec((None, l, HEAD_DIM), lambda i, j: (i, 0, j))
    colspec = pl.BlockSpec((None, None, l, 8), lambda i, j: (i, j, 0, 0))
    rowspec = pl.BlockSpec((None, None, 8, l), lambda i, j: (i, j, 0, 0))
    tile_f32 = pltpu.VMEM((n_slot, CHUNK, CHUNK), F32)
    tile_bf16 = pltpu.VMEM((n_slot, CHUNK, CHUNK), BF16)
    st_spec = pl.BlockSpec((None, None, 4, HEAD_DIM, HEAD_DIM), lambda i, j: (i, j, 0, 0, 0))

    def conv_spec(which):
        return pl.BlockSpec((SHORT_CONV, HEAD_DIM), lambda i, j, w=which: (0, w * N_HEADS + j))

    def gain_spec():
        return pl.BlockSpec((1, HEAD_DIM), lambda i, j: (0, j))

    if with_query:
        args = [lg, gl, feats["rk"], feats["rv"], feats["gk"], feats["gv"], feats["rq"], feats["gq"], feats["rg"],
                feats["gz"], cols, grow, conv, conv, conv, s0, gnw, rmsw]
        in_specs = [smem, smem] + [head] * 8 + [colspec, rowspec, conv_spec(0), conv_spec(1), conv_spec(2),
                                                st_spec, gain_spec(), gain_spec()]
        out_shape = [jax.ShapeDtypeStruct((b, l, MIX_W), BF16)] * 2
        out_specs = [head, head]
        scratch = ([pltpu.VMEM((2, l, HEAD_DIM), BF16)] + [pltpu.VMEM((l, HEAD_DIM), F32)] * 2
                   + [tile_bf16, tile_bf16, tile_bf16, tile_bf16, tile_f32, tile_f32])
    else:
        args = [lg, gl, feats["rk"], feats["rv"], feats["gk"], feats["gv"], cols, grow, conv, conv, s0]
        in_specs = [smem, smem] + [head] * 4 + [colspec, rowspec, conv_spec(1), conv_spec(2), st_spec]
        out_shape = jax.ShapeDtypeStruct((b, N_HEADS, 4, HEAD_DIM, HEAD_DIM), F32)
        out_specs = st_spec
        scratch = [tile_bf16, tile_f32, tile_f32]
    scratch = scratch + [pltpu.VMEM((8, CHUNK, CHUNK), F32), pltpu.VMEM((4, HEAD_DIM, HEAD_DIM), F32)]
    return pl.pallas_call(
        functools.partial(_mixer_kernel, seq_len=l, with_query=with_query),
        grid=(b, N_HEADS),
        in_specs=in_specs,
        out_specs=out_specs,
        out_shape=out_shape,
        scratch_shapes=scratch,
        compiler_params=_cparams(("arbitrary", "arbitrary")),
        name="mixers_latent" if with_query else "mixers_context",
    )(*args)


def _split_bf16(v):
    hi = v.astype(BF16)
    return hi, (v - hi.astype(F32)).astype(BF16)


def _route(h2, wrt_ref, bias_ref, cand_ref, sel_ref, wd_ref):
    tm = h2.shape[0]
    h_hi, h_lo = _split_bf16(h2)
    w_hi, w_lo = _split_bf16(wrt_ref[...])
    logits = _dot_nt(w_hi, h_hi) + (_dot_nt(w_hi, h_lo) + _dot_nt(w_lo, h_hi))
    scores = _sigmoid(logits)
    biased = scores + bias_ref[...]
    neg_inf = float("-inf")
    sub = _iota((GROUP_SIZE, tm), 0).astype(F32)
    group_score = []
    for g in range(N_GROUPS):
        blk = biased[g * GROUP_SIZE:(g + 1) * GROUP_SIZE, :]
        m1 = jnp.max(blk, axis=0, keepdims=True)
        first = jnp.min(jnp.where(blk == m1, sub, float(GROUP_SIZE)), axis=0, keepdims=True)
        m2 = jnp.max(jnp.where(sub == first, neg_inf, blk), axis=0, keepdims=True)
        group_score.append(m1 + m2)
    for g in range(N_GROUPS):
        ahead = jnp.zeros((1, tm), I32)
        for g2 in range(N_GROUPS):
            if g2 == g:
                continue
            before = (group_score[g2] > group_score[g])
            if g2 < g:
                before = before | (group_score[g2] == group_score[g])
            ahead = ahead + before.astype(I32)
        keep = jnp.broadcast_to(ahead, (GROUP_SIZE, tm)) < TOPK_GROUPS
        cand_ref[g * GROUP_SIZE:(g + 1) * GROUP_SIZE, :] = jnp.where(
            keep, biased[g * GROUP_SIZE:(g + 1) * GROUP_SIZE, :], neg_inf)
    work = cand_ref[...]
    eidx = _iota((N_EXPERTS, tm), 0).astype(F32)
    sel = jnp.zeros((N_EXPERTS, tm), jnp.bool_)
    for _ in range(TOP_K):
        best = jnp.max(work, axis=0, keepdims=True)
        first = jnp.min(jnp.where(work == best, eidx, float(N_EXPERTS)), axis=0, keepdims=True)
        pick = eidx == first
        sel = sel | pick
        work = jnp.where(pick, neg_inf, work)
    picked = jnp.where(sel, scores, 0.0)
    wsum = jnp.sum(picked, axis=0, keepdims=True)
    sel_ref[...] = sel.astype(F32)
    wd_ref[...] = picked / wsum * ROUTED_SCALE


def _mixout_kernel(x_ref, yr_ref, yg_ref, g0_ref, g1_ref, g2_ref, g3_ref, wr_ref, wg_ref, wo_ref,
                   npost_ref, gate1_ref, nffn_ref, sc2_ref, sh2_ref, wrt_ref, bias_ref,
                   x1_ref, h2_ref, sel_ref, wd_ref, cand_ref):
    r = _dot(yr_ref[...], wr_ref[...])
    g = _dot(yg_ref[...], wg_ref[...])
    half = r.shape[1] // 2
    merged = jnp.concatenate(
        [g0_ref[...].astype(F32) * r[:, :half] + g2_ref[...].astype(F32) * g[:, :half],
         g1_ref[...].astype(F32) * r[:, half:] + g3_ref[...].astype(F32) * g[:, half:]], axis=1)
    mo = _dot(merged.astype(BF16), wo_ref[...])
    x1 = x_ref[...] + gate1_ref[...] * _rms(mo, npost_ref[...])
    x1_ref[...] = x1
    h2 = _rms(x1, nffn_ref[...]) * (1.0 + sc2_ref[...]) + sh2_ref[...]
    h2_ref[...] = h2.astype(BF16)
    _route(h2, wrt_ref, bias_ref, cand_ref, sel_ref, wd_ref)


def _mixer_out(x, yr, yg, gates, wr, wg, wo, npost, gate1, nffn, sc2, sh2, wrt, bias):
    b, l, d = x.shape
    tm = min(MIXOUT_TM, l)
    tiles = l // tm
    tok = lambda w: pl.BlockSpec((None, tm, w), lambda i, j: (i, j, 0))
    const = functools.partial(pl.BlockSpec, pipeline_mode=pl.Buffered(1))
    vec = lambda: const((1, d), lambda i, j: (0, 0))
    bvec = lambda: pl.BlockSpec((None, 1, d), lambda i, j: (i, 0, 0))
    emap = pl.BlockSpec((N_EXPERTS, tm), lambda i, j: (0, i * tiles + j))
    in_specs = ([tok(d), tok(MIX_W), tok(MIX_W)] + [tok(MIX_W)] * 4
                + [const((MIX_W, d), lambda i, j: (0, 0)), const((MIX_W, d), lambda i, j: (0, 0)),
                   const((d, d), lambda i, j: (0, 0)),
                   vec(), bvec(), vec(), bvec(), bvec(),
                   const((N_EXPERTS, d), lambda i, j: (0, 0)), const((N_EXPERTS, 1), lambda i, j: (0, 0))])
    return pl.pallas_call(
        _mixout_kernel,
        grid=(b, tiles),
        in_specs=in_specs,
        out_specs=[tok(d), tok(d), emap, emap],
        out_shape=[jax.ShapeDtypeStruct((b, l, d), F32), jax.ShapeDtypeStruct((b, l, d), BF16),
                   jax.ShapeDtypeStruct((N_EXPERTS, b * l), F32), jax.ShapeDtypeStruct((N_EXPERTS, b * l), F32)],
        scratch_shapes=[pltpu.VMEM((N_EXPERTS, tm), F32)],
        compiler_params=_cparams(("arbitrary", "arbitrary")),
        name="mixer_out_router",
    )(x, yr, yg, *gates, wr, wg, wo, npost, gate1, nffn, sc2, sh2, wrt, bias)


def _tile_positions(sel):
    tm = sel.shape[1]
    selb = sel.astype(BF16)
    earlier = (_iota((tm, tm), 0) < _iota((tm, tm), 1)).astype(BF16)
    rank = _dot(selb, earlier)
    cnt = _dot(selb, jnp.ones((tm, tm), BF16))
    nmb = jnp.floor((cnt + (MB_ROWS - 1)) * (1.0 / MB_ROWS))
    below = (_iota((N_EXPERTS, N_EXPERTS), 1) < _iota((N_EXPERTS, N_EXPERTS), 0)).astype(BF16)
    offmb = _dot(below, nmb.astype(BF16))
    rank_hi = jnp.floor(rank * (1.0 / MB_ROWS))
    hi = jnp.where(sel > 0.0, offmb + rank_hi, 255.0)
    lo = rank - rank_hi * MB_ROWS
    return hi, lo, cnt, offmb, rank


def _dispatch_kernel(h2_ref, sel_ref, xs_ref, cnt_ref):
    last = pl.num_programs(0) - 1

    @pl.when(pl.program_id(0) < last)
    def _():
        _dispatch_tile(h2_ref, sel_ref, xs_ref, cnt_ref)

    @pl.when(pl.program_id(0) == last)
    def _():
        xs_ref[...] = jnp.zeros_like(xs_ref)


def _dispatch_tile(h2_ref, sel_ref, xs_ref, cnt_ref):
    tm = h2_ref.shape[0]
    sel = sel_ref[...]
    hi, lo, _, _, _ = _tile_positions(sel)
    hib = hi.astype(BF16)
    lob = lo.astype(BF16)
    selb = sel.astype(BF16)
    cnt_row = _dot_nt(jnp.ones((8, tm), BF16), selb)
    nmb_row = jnp.floor((cnt_row + (MB_ROWS - 1)) * (1.0 / MB_ROWS))
    before = (_iota((N_EXPERTS, N_EXPERTS), 0) < _iota((N_EXPERTS, N_EXPERTS), 1)).astype(BF16)
    start_row = _dot(nmb_row.astype(BF16), before) * MB_ROWS
    cnt_ref[...] = cnt_row
    start = start_row[0:1, :]
    end = start + cnt_row[0:1, :]
    x = h2_ref[...]
    rch = 256
    def one_hot(rc):
        r_e = (rc * rch + _iota((rch, N_EXPERTS), 0)).astype(F32)
        owner = ((r_e >= start) & (r_e < end)).astype(BF16)
        r_t = rc * rch + _iota((rch, tm), 0)
        match = ((_dot(owner, hib) == (r_t >> _MB_SHIFT).astype(F32))
                 & (_dot(owner, lob) == (r_t & (MB_ROWS - 1)).astype(F32)))
        return match.astype(BF16)

    n_rc = TILE_ROWS // rch
    n_sure = n_rc - TAIL_CHUNKS
    nxt = one_hot(0)
    for rc in range(n_sure):
        cur = nxt
        if rc + 1 < n_sure:
            nxt = one_hot(rc + 1)
        xs_ref[rc * rch:(rc + 1) * rch, :] = _dot(cur, x).astype(BF16)
    tail_used = jnp.sum(nmb_row[0:1, :]) * MB_ROWS > n_sure * rch

    @pl.when(tail_used)
    def _():
        for rc in range(n_sure, n_rc):
            xs_ref[rc * rch:(rc + 1) * rch, :] = _dot(one_hot(rc), x).astype(BF16)

    @pl.when(jnp.logical_not(tail_used))
    def _():
        xs_ref[n_sure * rch:, :] = jnp.zeros(((n_rc - n_sure) * rch, x.shape[1]), BF16)


def _dispatch(h2, sel_t):
    t, d = h2.shape
    tm = MOE_TM
    nt = t // tm
    return pl.pallas_call(
        _dispatch_kernel,
        grid=(nt + 1,),
        in_specs=[pl.BlockSpec((tm, d), lambda i: (jnp.minimum(i, nt - 1), 0)),
                  pl.BlockSpec((N_EXPERTS, tm), lambda i: (0, jnp.minimum(i, nt - 1)))],
        out_specs=[pl.BlockSpec((TILE_ROWS, d), lambda i: (i, 0)),
                   pl.BlockSpec((None, 8, N_EXPERTS), lambda i: (jnp.minimum(i, nt - 1), 0, 0))],
        out_shape=[jax.ShapeDtypeStruct((nt * TILE_ROWS + (EXP_OUT + 1) * EXP_BM, d), BF16),
                   jax.ShapeDtypeStruct((nt, 8, N_EXPERTS), F32)],
        compiler_params=_cparams(("arbitrary",)),
        name="moe_dispatch",
    )(h2, sel_t)


def _expert_plan(cnt, n_blk):
    nt = cnt.shape[0]
    nmb = (cnt + (MB_ROWS - 1)) // MB_ROWS
    offmb = jnp.cumsum(nmb, axis=1) - nmb
    per_e = nmb.T
    incl = jnp.cumsum(per_e, axis=1)
    excl = incl - per_e
    tot = incl[:, -1]
    nb = (tot + (EXP_MB - 1)) // EXP_MB
    bend = jnp.cumsum(nb)
    bstart = bend - nb
    b = jnp.arange(n_blk, dtype=I32)
    valid = b < bend[-1]
    last = jnp.maximum(bend[-1] - 1, 0)
    bq = jnp.where(valid, b, last)
    blk_e = jnp.minimum(jnp.sum((bend[None, :] <= bq[:, None]).astype(I32), axis=1), N_EXPERTS - 1)
    onehot_e = (blk_e[:, None] == jnp.arange(N_EXPERTS, dtype=I32)[None, :]).astype(I32)
    pick = lambda table: jnp.dot(onehot_e.astype(F32), table.astype(F32), precision=HIGHEST).astype(I32)
    bstart_b = jnp.sum(onehot_e * bstart[None, :], axis=1)
    tot_b = jnp.sum(onehot_e * tot[None, :], axis=1)
    p0 = (bq - bstart_b) * EXP_MB
    blk_nmb = jnp.where(valid, jnp.clip(tot_b - p0, 0, EXP_MB), 0).astype(I32)
    p = p0[:, None] + jnp.arange(EXP_MB, dtype=I32)[None, :]
    incl_b = pick(incl)
    passed = (incl_b[:, None, :] <= p[:, :, None]).astype(I32)
    tile = jnp.minimum(jnp.sum(passed, axis=2), nt - 1)
    shift_b = pick(offmb.T - excl)
    step_b = jnp.concatenate([shift_b[:, 1:] - shift_b[:, :-1], jnp.zeros_like(shift_b[:, :1])], axis=1)
    shift = shift_b[:, :1] + jnp.sum(passed * step_b[:, None, :], axis=2)
    where = tile * _TILE_MB + p + shift
    j = jnp.arange(EXP_MB, dtype=I32)[None, :]
    used = j < blk_nmb[:, None]
    spare = nt * _TILE_MB + (b[:, None] % EXP_OUT) * EXP_MB + j
    zeros_mb = nt * _TILE_MB + EXP_OUT * EXP_MB + j
    src = jnp.where(used, where, jnp.where(valid[:, None], where[:, :1], zeros_mb))
    dst = jnp.where(used, where, spare)
    dst = jnp.concatenate([nt * _TILE_MB + (EXP_OUT - 1) * EXP_MB + j, dst], axis=0)
    return bstart.astype(I32), nb.astype(I32), src.reshape(-1).astype(I32), dst.reshape(-1).astype(I32)


def _expert_kernel(bstart_ref, nb_ref, src_ref, dst_ref, xs_hbm, wgu_ref, wd_ref, ys_hbm,
                   xbuf, ybuf, in_sem, out_sem):
    e = pl.program_id(0)
    n_in = xbuf.shape[0]
    n_out = ybuf.shape[0]

    def gather_copy(blk, slot, j):
        rows = pl.ds(pl.multiple_of(src_ref[blk * EXP_MB + j] * MB_ROWS, MB_ROWS), MB_ROWS)
        return pltpu.make_async_copy(xs_hbm.at[rows, :], xbuf.at[slot, j * MB_ROWS:(j + 1) * MB_ROWS, :],
                                     in_sem.at[slot])

    def scatter_copy(blk, slot, j):
        rows = pl.ds(pl.multiple_of(dst_ref[(blk + 1) * EXP_MB + j] * MB_ROWS, MB_ROWS), MB_ROWS)
        return pltpu.make_async_copy(ybuf.at[slot, j * MB_ROWS:(j + 1) * MB_ROWS, :], ys_hbm.at[rows, :],
                                     out_sem.at[slot])

    def start_gather(blk):
        for j in range(EXP_MB):
            gather_copy(blk, lax.rem(blk, n_in), j).start()

    def wait_gather(blk):
        for j in range(EXP_MB):
            gather_copy(blk, lax.rem(blk, n_in), j).wait()

    def start_scatter(blk):
        for j in range(EXP_MB):
            scatter_copy(blk, lax.rem(blk + n_out, n_out), j).start()

    def wait_scatter(slot):
        for j in range(EXP_MB):
            scatter_copy(0, slot, j).wait()

    @pl.when(e == 0)
    def _():
        ybuf[...] = jnp.zeros_like(ybuf)
        for ahead in range(EXP_LEAD):
            start_gather(ahead)
        spare_row0 = ys_hbm.shape[0] - (n_out + 1) * EXP_BM
        for slot in range(n_out - 1):
            for j in range(EXP_MB):
                spare = pl.ds(spare_row0 + (slot * EXP_MB + j) * MB_ROWS, MB_ROWS)
                pltpu.make_async_copy(ybuf.at[slot, j * MB_ROWS:(j + 1) * MB_ROWS, :], ys_hbm.at[spare, :],
                                      out_sem.at[slot]).start()

    de = wd_ref.shape[0]

    def block(b, carry):
        slot = lax.rem(b, n_out)
        wait_gather(b)
        wait_scatter(slot)
        x = xbuf[lax.rem(b, n_in)]
        gu = _dot(x, wgu_ref[...])
        act = (_silu(gu[:, :de]) * gu[:, de:]).astype(BF16)
        start_gather(b + EXP_LEAD)
        start_scatter(b - 1)
        ybuf[slot] = _dot(act, wd_ref[...]).astype(BF16)
        return carry

    first = bstart_ref[e]
    lax.fori_loop(first, first + nb_ref[e], block, 0)

    @pl.when(e == pl.num_programs(0) - 1)
    def _():
        total = first + nb_ref[e]
        start_scatter(total - 1)
        for ahead in range(EXP_LEAD):
            wait_gather(total + ahead)
        for slot in range(n_out):
            wait_scatter(slot)


def _expert_ffn(xs, bstart, nb, src, dst, wgu, wd):
    rows, d = xs.shape
    n_e, _, de2 = wgu.shape
    grid_spec = pltpu.PrefetchScalarGridSpec(
        num_scalar_prefetch=4,
        grid=(n_e,),
        in_specs=[pl.BlockSpec(memory_space=pl.ANY),
                  pl.BlockSpec((None, d, de2), lambda i, bs, bn, sr, ds: (i, 0, 0)),
                  pl.BlockSpec((None, de2 // 2, d), lambda i, bs, bn, sr, ds: (i, 0, 0))],
        out_specs=pl.BlockSpec(memory_space=pl.ANY),
        scratch_shapes=[pltpu.VMEM((EXP_LEAD + 1, EXP_BM, d), BF16), pltpu.VMEM((EXP_OUT, EXP_BM, d), BF16),
                        pltpu.SemaphoreType.DMA((EXP_LEAD + 1,)), pltpu.SemaphoreType.DMA((EXP_OUT,))],
    )
    return pl.pallas_call(
        _expert_kernel,
        grid_spec=grid_spec,
        out_shape=jax.ShapeDtypeStruct((rows, d), BF16),
        input_output_aliases={4: 0},
        compiler_params=_cparams(("arbitrary",)),
        name="moe_experts",
    )(bstart, nb, src, dst, xs, wgu, wd)


def _combine_kernel(ys_ref, sel_ref, wd_ref, h2_ref, x1_ref, wsg_ref, wsu_ref, wsd_ref, npost_ref, gate2_ref,
                    o_ref, c_ref, acc_ref):
    tm = h2_ref.shape[0]
    sel = sel_ref[...]
    _, _, cnt, offmb, rank = _tile_positions(sel)
    no_rank = 1.5 * tm
    lhs = jnp.concatenate([jnp.where(sel > 0.0, rank, no_rank).T, wd_ref[...].T], axis=1).astype(BF16)
    start = offmb[:, :LANES] * MB_ROWS
    end = start + cnt[:, :LANES]
    zeros = jnp.zeros((N_EXPERTS, LANES), BF16)

    def build(cc):
        r_e = (cc * LANES + _iota((N_EXPERTS, LANES), 1)).astype(F32)
        owned = (r_e >= start) & (r_e < end)
        owner = owned.astype(BF16)
        local = r_e[0:1, :] - jnp.sum(jnp.where(owned, start, 0.0), axis=0, keepdims=True)
        rhs = jnp.concatenate([jnp.concatenate([owner, zeros], axis=1),
                               jnp.concatenate([zeros, owner], axis=1)], axis=0)
        got = _dot(lhs, rhs)
        c_ref[:, cc * LANES:(cc + 1) * LANES] = jnp.where(got[:, :LANES] == local, got[:, LANES:], 0.0).astype(BF16)

    n_cc = TILE_ROWS // LANES
    sure = TILE_ROWS - TAIL_CHUNKS * 256
    for cc in range(sure // LANES):
        build(cc)
    h2 = h2_ref[...]
    shared = _dot((_silu(_dot(h2, wsg_ref[...])) * _dot(h2, wsu_ref[...])).astype(BF16), wsd_ref[...])
    acc_ref[...] = shared + _dot(c_ref[:, :sure], ys_ref[:sure, :])
    nmb = jnp.floor((cnt[:, 0:1] + (MB_ROWS - 1)) * (1.0 / MB_ROWS))

    @pl.when(jnp.sum(nmb) * MB_ROWS > sure)
    def _():
        for cc in range(sure // LANES, n_cc):
            build(cc)
        acc_ref[...] += _dot(c_ref[:, sure:], ys_ref[sure:, :])

    o_ref[...] = x1_ref[...] + gate2_ref[...] * _rms(acc_ref[...], npost_ref[...])


def _combine(ys, sel_t, wd_t, h2, x1, wsg, wsu, wsd, npost, gate2, seq_len):
    t, d = h2.shape
    tm = MOE_TM
    nt = t // tm
    per_seq = seq_len // tm
    ds = wsg.shape[1]
    const = functools.partial(pl.BlockSpec, pipeline_mode=pl.Buffered(1))
    emap = pl.BlockSpec((N_EXPERTS, tm), lambda i: (0, i))
    return pl.pallas_call(
        _combine_kernel,
        grid=(nt,),
        in_specs=[pl.BlockSpec((TILE_ROWS, d), lambda i: (i, 0)), emap, emap,
                  pl.BlockSpec((tm, d), lambda i: (i, 0)), pl.BlockSpec((tm, d), lambda i: (i, 0)),
                  const((d, ds), lambda i: (0, 0)), const((d, ds), lambda i: (0, 0)), const((ds, d), lambda i: (0, 0)),
                  const((1, d), lambda i: (0, 0)),
                  pl.BlockSpec((None, 1, d), lambda i: (i // per_seq, 0, 0))],
        out_specs=pl.BlockSpec((tm, d), lambda i: (i, 0)),
        out_shape=jax.ShapeDtypeStruct((t, d), F32),
        scratch_shapes=[pltpu.VMEM((tm, TILE_ROWS), BF16), pltpu.VMEM((tm, d), F32)],
        compiler_params=_cparams(("arbitrary",)),
        name="moe_combine",
    )(ys, sel_t, wd_t, h2, x1, wsg, wsu, wsd, npost, gate2)


def _rope_tables(n):
    rows = n // GRID_W
    pos_r = jnp.repeat(jnp.arange(rows, dtype=F32), GRID_W)
    pos_c = jnp.tile(jnp.arange(GRID_W, dtype=F32), rows)
    n_freq = HEAD_DIM // 4
    inv = ROPE_BASE ** (-jnp.arange(n_freq, dtype=F32) / n_freq)
    ang = jnp.concatenate([pos_r[:, None] * inv, pos_c[:, None] * inv], axis=-1)
    cos, sin = jnp.cos(ang), jnp.sin(ang)
    return jnp.concatenate([cos, cos], axis=-1), jnp.concatenate([-sin, sin], axis=-1)


def kernel(x, c, ctx, c_ctx, w_mod, b_mod, norm_mix_pre, norm_mix_post, norm_ffn_pre, norm_ffn_post, w_in, gdn_conv, ret_log_decay, gdn_a_log, gdn_dt_bias, ret_gn_w, gdn_norm_w, w_ret_out, w_gdn_out, w_o, w_router, router_bias, w_gate, w_up, w_down, w_sh_gate, w_sh_up, w_sh_down):
    b, n, d = x.shape
    depth = w_mod.shape[0]
    assert depth == 1, "single-layer block"
    assert all(n % min(tile, n) == 0 for tile in (PROJ_TM, MIXOUT_TM, MOE_TM)) and n % MOE_TM == 0
    assert ctx.shape[1] % CHUNK == 0 and (EXP_OUT + 1) * EXP_BM <= TILE_ROWS
    assert _TILE_MB < 255

    rows = -(-(b + 1) // 8) * 8
    cvec = jnp.zeros((rows, d), F32).at[:b].set(c).at[b].set(c_ctx)
    mod = _modulation(cvec, w_mod[0], b_mod[0][None, :])
    sh1, sc1, g1, sh2, sc2, g2 = [mod[:b, k * d:(k + 1) * d][:, None, :] for k in range(6)]
    ctx_shift = jnp.broadcast_to(mod[b, 0:d][None, None, :], (b, 1, d))
    ctx_scale = jnp.broadcast_to(mod[b, d:2 * d][None, None, :], (b, 1, d))

    w_in0 = w_in[0]
    n_main = 4 * MIX_W
    w_state = w_in0[:, :n_main].astype(BF16)
    slots = ((0, 0), (0, 1), (1, 0), (1, 1), (0, 0), (0, 1), (0, 0), (0, 1))
    gab_cols = n_main + jnp.array([[ab * 2 * N_HEADS + dr * N_HEADS + hh for ab, dr in slots]
                                   for hh in range(N_HEADS)], I32)
    w_gab = w_in0[:, gab_cols.reshape(-1)].astype(BF16)
    n_state = n_main + N_GAB
    w_query = w_in0[:, n_state:].astype(BF16)
    a_coef = -jnp.exp(gdn_a_log[0].astype(F32))
    dtb = gdn_dt_bias[0].astype(F32)
    is_alpha = jnp.array([ab == 0 for ab, _ in slots])
    dirs = jnp.array([dr for _, dr in slots], I32)
    prow = jnp.stack([jnp.where(is_alpha[None, :], a_coef.T[:, dirs], 0.0).reshape(-1),
                      jnp.where(is_alpha[None, :], dtb.T[:, dirs], 0.0).reshape(-1)], axis=0)
    first2 = jnp.tile(jnp.arange(8) < 2, N_HEADS)
    w_gab_t = w_gab.T * first2[:, None].astype(BF16)
    pcol = prow.T * first2[:, None]
    gain_mix = norm_mix_pre[0][None, :]
    cos2, sin2 = _rope_tables(n)

    state_kinds = (_ROPE_SCALED, _PLAIN, _PLAIN, _PLAIN)
    query_kinds = (_ROPE, _PLAIN, _PLAIN, _PLAIN, _SIGMOID, _SIGMOID, _SIGMOID, _SIGMOID)
    lg = ret_log_decay[0].astype(F32)
    conv = gdn_conv[0].astype(F32)

    lc = ctx.shape[1]
    cfe, (ccols, cgrow, cgl) = _in_projection(
        ctx, gain_mix, ctx_scale, ctx_shift, (w_state,), w_gab, w_gab_t, prow, pcol,
        cos2[:lc], sin2[:lc], state_kinds, rope=False)
    cfeats = dict(zip(("rk", "rv", "gk", "gv"), cfe))
    zero_state = jnp.zeros((b, N_HEADS, 4, HEAD_DIM, HEAD_DIM), F32)
    init = _mixers(lg, cgl, cfeats, (ccols, cgrow), conv, zero_state, None, None, with_query=False)

    fe, (cols, grow, gl), (w_gu_bf, w_down_bf) = _in_projection(
        x, gain_mix, sc1, sh1, (w_state, w_query), w_gab, w_gab_t, prow, pcol, cos2, sin2,
        state_kinds + query_kinds, rope=True, expert_w=(w_gate[0], w_up[0], w_down[0]))
    feats = dict(zip(("rk", "rv", "gk", "gv", "rq", "rg", "gq", "gz"), fe[:8]))
    gates = fe[8:]
    y_ret, y_gdn = _mixers(lg, gl, feats, (cols, grow), conv, init,
                           ret_gn_w[0][None, :], gdn_norm_w[0][None, :], with_query=True)
    x1, h2, sel_t, wd_t = _mixer_out(
        x, y_ret, y_gdn, gates, w_ret_out[0].astype(BF16), w_gdn_out[0].astype(BF16), w_o[0].astype(BF16),
        norm_mix_post[0][None, :], g1, norm_ffn_pre[0][None, :], sc2, sh2,
        w_router[0].T.astype(F32), router_bias[0].astype(F32)[:, None])

    t = b * n
    h2f = h2.reshape(t, d)
    xs, cnt = _dispatch(h2f, sel_t)
    nt = t // MOE_TM
    n_blk = nt * _TILE_MB // EXP_MB + N_EXPERTS + EXP_LEAD
    bstart, nb, src, dst = _expert_plan(cnt[:, 0, :].astype(I32), n_blk)
    ys = _expert_ffn(xs, bstart, nb, src, dst, w_gu_bf, w_down_bf)
    out = _combine(ys, sel_t, wd_t, h2f, x1.reshape(t, d), w_sh_gate[0].astype(BF16), w_sh_up[0].astype(BF16),
                   w_sh_down[0].astype(BF16), norm_ffn_post[0][None, :], g2, n)
    return out.reshape(b, n, d)
```

```python
import functools
import math

import jax
import jax.numpy as jnp
from jax import lax
from jax.experimental import pallas as pl
from jax.experimental.pallas import tpu as pltpu

F32 = jnp.float32
BF16 = jnp.bfloat16
I32 = jnp.int32
HIGHEST = lax.Precision.HIGHEST

N_HEADS = 4
HEAD_DIM = 128
MIX_W = N_HEADS * HEAD_DIM
CHUNK = 128
SHORT_CONV = 3
ROPE_BASE = 10000.0
GRID_W = 64
N_EXPERTS = 64
TOP_K = 8
N_GROUPS = 8
TOPK_GROUPS = 4
GROUP_SIZE = N_EXPERTS // N_GROUPS
ROUTED_SCALE = 2.5
EPS = 1e-6
N_GAB = 4 * N_HEADS

LANES = 128
BF16_TILE_ROWS = 16
VMEM_LIMIT_BYTES = 56 * 1024 * 1024

PROJ_TM = 1024
MIXOUT_TM = 1024
MOE_TM = 256
MB_ROWS = BF16_TILE_ROWS
_MB_SHIFT = MB_ROWS.bit_length() - 1
_TILE_MB = -(-(TOP_K * MOE_TM // MB_ROWS + N_EXPERTS * (MB_ROWS - 1) // MB_ROWS + 1) // 8) * 8
TILE_ROWS = _TILE_MB * MB_ROWS
TAIL_CHUNKS = 2
EXP_BM = 512
EXP_MB = EXP_BM // MB_ROWS
EXP_LEAD = 3
EXP_OUT = 3


def _cparams(sem):
    return pltpu.CompilerParams(dimension_semantics=sem, vmem_limit_bytes=VMEM_LIMIT_BYTES)


def _sigmoid(v):
    return 0.5 * jnp.tanh(0.5 * v) + 0.5


def _silu(v):
    return v * _sigmoid(v)


def _softplus(v):
    return jnp.maximum(v, 0.0) + jnp.log1p(jnp.exp(-jnp.abs(v)))


def _iota(shape, dim):
    return lax.broadcasted_iota(I32, shape, dim)


def _dot(a, b, **kw):
    return jnp.dot(a, b, preferred_element_type=F32, **kw)


def _dot_nt(a, b, **kw):
    return lax.dot_general(a, b, (((1,), (1,)), ((), ())), preferred_element_type=F32, **kw)


def _dot_tn(a, b, **kw):
    return lax.dot_general(a, b, (((0,), (0,)), ((), ())), preferred_element_type=F32, **kw)


def _rms(v, gain):
    return v * lax.rsqrt(jnp.mean(v * v, axis=-1, keepdims=True) + EPS) * gain


def _mod_kernel(c_ref, w_ref, b_ref, o_ref):
    o_ref[...] = _dot(_silu(c_ref[...]), w_ref[...], precision=HIGHEST) + b_ref[...]


def _modulation(cvec, w_mod, b_mod):
    rows, d = cvec.shape
    n = w_mod.shape[1]
    tn = 1024
    return pl.pallas_call(
        _mod_kernel,
        grid=(n // tn,),
        in_specs=[pl.BlockSpec((rows, d), lambda j: (0, 0)),
                  pl.BlockSpec((d, tn), lambda j: (0, j)),
                  pl.BlockSpec((1, tn), lambda j: (0, j))],
        out_specs=pl.BlockSpec((rows, tn), lambda j: (0, j)),
        out_shape=jax.ShapeDtypeStruct((rows, n), F32),
        compiler_params=_cparams(("arbitrary",)),
        name="adaln_modulation",
    )(cvec, w_mod, b_mod)


_PLAIN, _ROPE, _ROPE_SCALED, _SIGMOID = 0, 1, 2, 3


def _proj_kernel(x_ref, gain_ref, sc_ref, sh_ref, *refs, kinds, rope, n_w):
    w_refs = refs[:n_w]
    wg_ref, wgt_ref, prow_ref, pcol_ref, cos_ref, sin_ref = refs[n_w:n_w + 6]
    out_refs = refs[n_w + 6:]
    groups = [(w, c) for w in w_refs for c in range(0, w.shape[1], MIX_W)]
    n_feat = len(kinds)
    feat_refs = out_refs[:n_feat]
    cols_ref, grow_ref, gl_ref = out_refs[n_feat:]
    tm = x_ref.shape[0]

    x = x_ref[...]
    h = (_rms(x, gain_ref[...]) * (1.0 + sc_ref[...]) + sh_ref[...]).astype(BF16)

    r_i = _iota((CHUNK, CHUNK), 0)
    c_i = _iota((CHUNK, CHUNK), 1)
    lower_incl = (c_i <= r_i).astype(BF16)
    upper_incl = (c_i >= r_i).astype(BF16)

    def prefix_rows(v):
        hi, lo = _split_bf16(v)
        return _dot(lower_incl, hi) + _dot(lower_incl, lo)

    def prefix_lanes(v):
        hi, lo = _split_bf16(v)
        return _dot(hi, upper_incl) + _dot(lo, upper_incl)

    nc = N_HEADS * 8
    pg = _dot(h, wg_ref[...])
    colt = _iota((tm, nc), 1) & 7
    la = jnp.where((colt == 2) | (colt == 3), 0.0, prow_ref[0:1, :] * _softplus(pg + prow_ref[1:2, :]))
    beta = _sigmoid(pg)
    colc = _iota((CHUNK, nc), 1) & 7
    fwd_col = (colc & 1) == 0
    for c in range(tm // CHUNK):
        sl = slice(c * CHUNK, (c + 1) * CHUNK)
        la_c = la[sl]
        pre = prefix_rows(la_c)
        suf = pre[CHUNK - 1:CHUNK, :] - pre + la_c
        g_c = jnp.where(fwd_col, pre, suf)
        rest = jnp.where(fwd_col, suf, pre) - la_c
        vals = jnp.where(colc < 2, g_c, jnp.where(colc < 4, beta[sl], jnp.where(
            colc < 6, jnp.exp(g_c), jnp.exp(rest))))
        for hh in range(N_HEADS):
            cols_ref[hh, sl, :] = vals[:, 8 * hh:8 * hh + 8]
        gl_ref[c:c + 1, :] = jnp.exp(g_c[0:1, :] + rest[0:1, :])

    pgt = _dot_nt(wgt_ref[...], h)
    rowq = _iota((N_HEADS * 8, tm), 0) & 7
    lat = jnp.where(rowq < 2, pcol_ref[:, 0:1] * _softplus(pgt + pcol_ref[:, 1:2]), 0.0)
    rowc = _iota((N_HEADS * 8, CHUNK), 0) & 7
    for c in range(tm // CHUNK):
        sl = slice(c * CHUNK, (c + 1) * CHUNK)
        lat_c = lat[:, sl]
        pre_t = prefix_lanes(lat_c)
        suf_t = pre_t[:, CHUNK - 1:CHUNK] - pre_t + lat_c
        grow_ref[:, :, sl] = jnp.where(rowc == 0, pre_t, suf_t).reshape(N_HEADS, 8, CHUNK)

    if rope:
        cos2 = cos_ref[...]
        sin2 = sin_ref[...]

    for g, kind in enumerate(kinds):
        w_ref, c0 = groups[g]
        p = _dot(h, w_ref[:, c0:c0 + MIX_W])
        if kind == _SIGMOID:
            p = _sigmoid(p)
        elif kind in (_ROPE, _ROPE_SCALED) and rope:
            heads = []
            for hh in range(N_HEADS):
                t = p[:, hh * HEAD_DIM:(hh + 1) * HEAD_DIM]
                heads.append(t * cos2 + pltpu.roll(t, HEAD_DIM // 2, 1) * sin2)
            p = jnp.concatenate(heads, axis=1)
        if kind == _ROPE_SCALED:
            p = p * (HEAD_DIM ** -0.5)
        feat_refs[g][...] = p.astype(feat_refs[g].dtype)


def _in_projection(x, gain, scale, shift, w_mains, w_gab, w_gab_t, prow, pcol, cos2, sin2, kinds, rope):
    b, l, d = x.shape
    tm = min(PROJ_TM, l)
    tiles = l // tm
    n_chunk = tm // CHUNK
    feat_shapes = [jax.ShapeDtypeStruct((b, l, MIX_W), BF16) for _ in kinds]
    feat_specs = [pl.BlockSpec((None, tm, MIX_W), lambda i, j: (i, j, 0)) for _ in kinds]
    out_shape = feat_shapes + [jax.ShapeDtypeStruct((b, N_HEADS, l, 8), F32),
                               jax.ShapeDtypeStruct((b, N_HEADS, 8, l), F32),
                               jax.ShapeDtypeStruct((b, tiles, n_chunk, N_HEADS * 8), F32)]
    out_specs = feat_specs + [pl.BlockSpec((None, N_HEADS, tm, 8), lambda i, j: (i, 0, j, 0)),
                              pl.BlockSpec((None, N_HEADS, 8, tm), lambda i, j: (i, 0, 0, j)),
                              pl.BlockSpec((None, None, n_chunk, N_HEADS * 8), lambda i, j: (i, j, 0, 0))]
    const = functools.partial(pl.BlockSpec, pipeline_mode=pl.Buffered(1))
    in_specs = [
        pl.BlockSpec((None, tm, d), lambda i, j: (i, j, 0)),
        const((1, d), lambda i, j: (0, 0)),
        pl.BlockSpec((None, 1, d), lambda i, j: (i, 0, 0)),
        pl.BlockSpec((None, 1, d), lambda i, j: (i, 0, 0)),
    ] + [const(w.shape, lambda i, j: (0, 0)) for w in w_mains] + [
        const((d, N_HEADS * 8), lambda i, j: (0, 0)),
        const((N_HEADS * 8, d), lambda i, j: (0, 0)),
        const((2, N_HEADS * 8), lambda i, j: (0, 0)),
        const((N_HEADS * 8, 2), lambda i, j: (0, 0)),
        pl.BlockSpec((tm, HEAD_DIM), lambda i, j: (j, 0)),
        pl.BlockSpec((tm, HEAD_DIM), lambda i, j: (j, 0)),
    ]
    outs = pl.pallas_call(
        functools.partial(_proj_kernel, kinds=tuple(kinds), rope=rope, n_w=len(w_mains)),
        grid=(b, tiles),
        in_specs=in_specs,
        out_specs=out_specs,
        out_shape=out_shape,
        compiler_params=_cparams(("arbitrary", "arbitrary")),
        name="in_projection_rope" if rope else "in_projection_ctx",
    )(x, gain, scale, shift, *w_mains, w_gab, w_gab_t, prow, pcol, cos2, sin2)
    feats = outs[:len(kinds)]
    cols, grow, gl = outs[len(kinds):]
    return feats, (cols, grow, gl.reshape(b * (l // CHUNK), N_HEADS * 8))


def _unit_triangular_inverses(mats, lowers):
    r = _iota((CHUNK, CHUNK), 0)
    c = _iota((CHUNK, CHUNK), 1)
    eye = (r == c).astype(F32)
    invs = [eye - jnp.where((r >> 1) == (c >> 1), a, 0.0) for a in mats]
    for level in range(1, int(math.log2(CHUNK))):
        s = 1 << level
        mask = ((r >> (level + 1)) == (c >> (level + 1))) & ((r >> level) != (c >> level))
        invb = [inv.astype(BF16) for inv in invs]
        offs = [jnp.where(mask, a, 0.0).astype(BF16) for a in mats]
        if s < 8:
            half = [_dot(off, ib).astype(BF16) for off, ib in zip(offs, invb)]
            invs = [inv - _dot(ib, hf) for inv, ib, hf in zip(invs, invb, half)]
            continue
        def rows_of(x, lower, moving):
            first = s if (lower == moving) else 0
            return [x[g * 2 * s + first:g * 2 * s + first + s] for g in range(CHUNK // (2 * s))]

        half = [_dot(jnp.concatenate(rows_of(off, lo, True), axis=0), ib).astype(BF16)
                for off, ib, lo in zip(offs, invb, lowers)]
        zero = jnp.zeros((s, CHUNK), BF16)
        full = []
        for hf, lo in zip(half, lowers):
            pieces = []
            for g in range(CHUNK // (2 * s)):
                piece = hf[g * s:(g + 1) * s]
                pieces += [zero, piece] if lo else [piece, zero]
            full.append(jnp.concatenate(pieces, axis=0))
        corr = [_dot(jnp.concatenate(rows_of(ib, lo, True), axis=0), hf) for ib, hf, lo in zip(invb, full, lowers)]
        new = []
        for inv, cr, lo in zip(invs, corr, lowers):
            keep = rows_of(inv, lo, False)
            moved = [m - cr[g * s:(g + 1) * s] for g, m in enumerate(rows_of(inv, lo, True))]
            pieces = []
            for k, m in zip(keep, moved):
                pieces += [k, m] if lo else [m, k]
            new.append(jnp.concatenate(pieces, axis=0))
        invs = new
    return invs


def _mixer_kernel(*refs, seq_len, with_query):
    n_chunk = seq_len // CHUNK
    if with_query:
        (lg_ref, gl_ref, rk_ref, rv_ref, gk_ref, gv_ref, rq_ref, gq_ref, rg_ref, gz_ref,
         cols_ref, grow_ref, cq_ref, ck_ref, cv_ref, s0_ref, gnw_ref, rmsw_ref,
         yret_ref, ygdn_ref,
         qes, oret, ogdn, ubuf, wbuf, pbuf, nbuf, cbuf, kvbuf, rtile, state) = refs
    else:
        (lg_ref, gl_ref, rk_ref, rv_ref, gk_ref, gv_ref,
         cols_ref, grow_ref, ck_ref, cv_ref, s0_ref,
         sfin_ref,
         nbuf, cbuf, kvbuf, rtile, state) = refs
    bi = pl.program_id(0)
    hi = pl.program_id(1)

    row = _iota((CHUNK, CHUNK), 0)
    colm = _iota((CHUNK, CHUNK), 1)
    rowf = row.astype(F32)
    colf = colm.astype(F32)

    def conv_chunk(src_ref, w_ref, n):
        s = pl.multiple_of(n * CHUNK, CHUNK)
        x = src_ref[pl.ds(s, CHUNK), :].astype(F32)
        ps = pl.multiple_of(jnp.maximum(s - BF16_TILE_ROWS, 0), BF16_TILE_ROWS)
        ns = pl.multiple_of(jnp.minimum(s + CHUNK, seq_len - BF16_TILE_ROWS), BF16_TILE_ROWS)
        prev_row = src_ref[pl.ds(ps, BF16_TILE_ROWS), :].astype(F32)[BF16_TILE_ROWS - 1:BF16_TILE_ROWS, :]
        next_row = src_ref[pl.ds(ns, BF16_TILE_ROWS), :].astype(F32)[0:1, :]
        prev_row = prev_row * jnp.where(n > 0, 1.0, 0.0)
        next_row = next_row * jnp.where(n < n_chunk - 1, 1.0, 0.0)
        xp = jnp.where(row == 0, jnp.broadcast_to(prev_row, (CHUNK, HEAD_DIM)), pltpu.roll(x, 1, 0))
        xn = jnp.where(row == CHUNK - 1, jnp.broadcast_to(next_row, (CHUNK, HEAD_DIM)), pltpu.roll(x, CHUNK - 1, 0))
        return _silu(w_ref[0:1, :] * xp + w_ref[1:2, :] * x + w_ref[2:3, :] * xn)

    def l2n(v):
        return v * lax.rsqrt(jnp.sum(v * v, axis=-1, keepdims=True) + EPS)

    for d in range(2):
        lg = lg_ref[d, hi]
        if d == 0:
            dist, pos_q, pos_k = rowf - colf, rowf + 1.0, (CHUNK - 1.0) - rowf
        else:
            dist, pos_q, pos_k = colf - rowf, CHUNK - rowf, rowf
        rtile[4 * d + 0] = jnp.where(dist >= 0, jnp.exp(lg * jnp.maximum(dist, 0.0)), 0.0)
        rtile[4 * d + 1] = jnp.exp(lg * pos_q)
        rtile[4 * d + 2] = jnp.exp(lg * pos_k)
        rtile[4 * d + 3] = jnp.exp(lg * jnp.full((CHUNK, CHUNK), float(CHUNK), F32))

    state[...] = s0_ref[...]

    def bcast_col(cols, j):
        return jnp.broadcast_to(cols[:, j:j + 1], (CHUNK, CHUNK))

    pre_chunks = min(8, n_chunk)

    def prepass(m, carry):
        chunks = [m * pre_chunks + j for j in range(pre_chunks)]
        chunk_cs = [pl.ds(pl.multiple_of(n * CHUNK, CHUNK), CHUNK) for n in chunks]
        conv_k, conv_v, conv_q = [], [], []
        for n, cs in zip(chunks, chunk_cs):
            conv_k.append(l2n(conv_chunk(gk_ref, ck_ref, n)).astype(BF16))
            conv_v.append(conv_chunk(gv_ref, cv_ref, n).astype(BF16))
            if with_query:
                conv_q.append((l2n(conv_chunk(gq_ref, cq_ref, n)) * (HEAD_DIM ** -0.5)).astype(BF16))
                ogdn[cs, :] = jnp.zeros((CHUNK, HEAD_DIM), F32)

        jobs = [(j, d) for j in range(pre_chunks) for d in range(2)]
        css = [chunk_cs[j] for j, _ in jobs]
        slots = [d * n_chunk + chunks[j] for j, d in jobs]
        dirs = [d for _, d in jobs]
        ks = [rk_ref[cs, :] for cs in css]
        vs = [rv_ref[cs, :] for cs in css]
        if with_query:
            scs = [(_dot_nt(rq_ref[cs, :], k) * rtile[4 * d + 0]).astype(BF16) for cs, k, d in zip(css, ks, dirs)]
            outs = [_dot(sc, v) for sc, v in zip(scs, vs)]
            for j, cs in enumerate(chunk_cs):
                oret[cs, :] = outs[2 * j] + outs[2 * j + 1]
        kvs = [_dot_tn((k.astype(F32) * rtile[4 * d + 2]).astype(BF16), v) for k, v, d in zip(ks, vs, dirs)]
        for slot, kv in zip(slots, kvs):
            kvbuf[slot] = kv
        ks = [conv_k[j] for j, _ in jobs]
        vs = [conv_v[j] for j, _ in jobs]
        colss = [cols_ref[cs, :] for cs in css]
        betas = [bcast_col(cols, 2 + d) for cols, d in zip(colss, dirs)]
        incls = [(row >= colm) if d == 0 else (row <= colm) for d in dirs]
        stricts = [(row > colm) if d == 0 else (row < colm) for d in dirs]
        decs = [jnp.exp(jnp.where(incl, bcast_col(cols, d) - jnp.broadcast_to(grow_ref[d:d + 1, cs], (CHUNK, CHUNK)), 0.0))
                for cols, d, cs, incl in zip(colss, dirs, css, incls)]
        kks = [_dot_nt(k, k) for k in ks]
        mats = [kk * beta * jnp.where(strict, dec, 0.0) for kk, beta, strict, dec in zip(kks, betas, stricts, decs)]
        tinvs = [t.astype(BF16) for t in _unit_triangular_inverses(mats, [d == 0 for d in dirs])]
        rhs = [jnp.concatenate([(beta * v.astype(F32)).astype(BF16),
                                (beta * bcast_col(cols, 4 + d) * k.astype(F32)).astype(BF16)], axis=1)
               for beta, v, cols, d, k in zip(betas, vs, colss, dirs, ks)]
        uws = [_dot(t, r) for t, r in zip(tinvs, rhs)]
        us = [uw[:, :HEAD_DIM] for uw in uws]
        ws = [uw[:, HEAD_DIM:].astype(BF16) for uw in uws]
        kts = [(k.astype(F32) * bcast_col(cols, 6 + d)).astype(BF16) for k, cols, d in zip(ks, colss, dirs)]
        ncs = [_dot_tn(kt, uw.astype(BF16)) for kt, uw in zip(kts, uws)]
        for slot, nc_mat in zip(slots, ncs):
            cbuf[slot] = nc_mat[:, :HEAD_DIM]
            nbuf[slot] = nc_mat[:, HEAD_DIM:].astype(BF16)
        if with_query:
            for slot, u, w in zip(slots, us, ws):
                ubuf[slot] = u.astype(BF16)
                wbuf[slot] = w
            qks = [_dot_nt(conv_q[j], k) for (j, _), k in zip(jobs, ks)]
            for slot, qk, incl, dec in zip(slots, qks, incls, decs):
                pbuf[slot] = (qk * jnp.where(incl, dec, 0.0)).astype(BF16)
            for (j, d), cs, cols in zip(jobs, css, colss):
                qes[d, cs, :] = (conv_q[j].astype(F32) * bcast_col(cols, 4 + d)).astype(BF16)
        return carry

    lax.fori_loop(0, n_chunk // pre_chunks, prepass, 0)

    def finish(cs):
        ro = oret[cs, :]
        rc = ro - jnp.mean(ro, axis=-1, keepdims=True)
        ry = rc * lax.rsqrt(jnp.mean(rc * rc, axis=-1, keepdims=True) + EPS)
        yret_ref[cs, :] = (ry * gnw_ref[...] * _silu(rg_ref[cs, :].astype(F32))).astype(BF16)
        go = ogdn[cs, :]
        gy = go * lax.rsqrt(jnp.mean(go * go, axis=-1, keepdims=True) + EPS)
        ygdn_ref[cs, :] = (gy * rmsw_ref[...] * _silu(gz_ref[cs, :].astype(F32))).astype(BF16)

    def chunk_ids(n):
        nds = [n, n_chunk - 1 - n]
        css = [pl.ds(nd * CHUNK if isinstance(nd, int) else pl.multiple_of(nd * CHUNK, CHUNK), CHUNK) for nd in nds]
        slots = [d * n_chunk + nd for d, nd in enumerate(nds)]
        return nds, css, slots

    def late_outputs(n, vnb, finishing):
        _, css, slots = chunk_ids(n)
        for d in range(2):
            ogdn[css[d], :] += _dot(pbuf[slots[d]], vnb[d])
            if finishing:
                finish(css[d])

    def scan_step(n, vnb_prev, has_prev, finish_prev):
        nds, css, slots = chunk_ids(n)
        ret_st = [state[d] for d in range(2)]
        gdn_st = [state[2 + d] for d in range(2)]
        gdn_stb = [st.astype(BF16) for st in gdn_st]
        shrink = [_dot(nbuf[slot], stb) for slot, stb in zip(slots, gdn_stb)]
        for d in range(2):
            state[2 + d] = gl_ref[bi * n_chunk + nds[d], 8 * hi + d] * gdn_st[d] - shrink[d] + cbuf[slots[d]]
            state[d] = rtile[4 * d + 3] * ret_st[d] + kvbuf[slots[d]]
        if not with_query:
            return vnb_prev
        vnb = tuple((ubuf[slot].astype(F32) - _dot(wbuf[slot], stb)).astype(BF16)
                    for slot, stb in zip(slots, gdn_stb))
        for d in range(2):
            oret[css[d], :] += _dot(rq_ref[css[d], :], ret_st[d].astype(BF16)) * rtile[4 * d + 1]
            ogdn[css[d], :] += _dot(qes[d, css[d], :], gdn_stb[d])
        if has_prev:
            late_outputs(n - 1, vnb_prev, finish_prev)
        return vnb

    if with_query:
        half = n_chunk // 2
        zero = jnp.zeros((CHUNK, HEAD_DIM), BF16)
        vnb = scan_step(0, (zero, zero), has_prev=False, finish_prev=False)
        vnb = lax.fori_loop(1, half + 1, functools.partial(scan_step, has_prev=True, finish_prev=False), vnb)
        vnb = lax.fori_loop(half + 1, n_chunk, functools.partial(scan_step, has_prev=True, finish_prev=True), vnb)
        late_outputs(n_chunk - 1, vnb, True)
    else:
        lax.fori_loop(0, n_chunk, functools.partial(scan_step, has_prev=False, finish_prev=False), 0)
        sfin_ref[...] = state[...]


def _mixers(lg, gl, feats, dec, conv, s0, gnw, rmsw, with_query):
    cols, grow = dec
    b, l, _ = feats["rk"].shape
    n_slot = 2 * (l // CHUNK)
    assert (l // CHUNK) % min(8, l // CHUNK) == 0
    smem = pl.BlockSpec(memory_space=pltpu.SMEM)
    head = pl.BlockSpec((None, l, HEAD_DIM), lambda i, j: (i, 0, j))
    colspec = pl.BlockSpec((None, None, l, 8), lambda i, j: (i, j, 0, 0))
    rowspec = pl.BlockSpec((None, None, 8, l), lambda i, j: (i, j, 0, 0))
    tile_f32 = pltpu.VMEM((n_slot, CHUNK, CHUNK), F32)
    tile_bf16 = pltpu.VMEM((n_slot, CHUNK, CHUNK), BF16)
    st_spec = pl.BlockSpec((None, None, 4, HEAD_DIM, HEAD_DIM), lambda i, j: (i, j, 0, 0, 0))

    def conv_spec(which):
        return pl.BlockSpec((SHORT_CONV, HEAD_DIM), lambda i, j, w=which: (0, w * N_HEADS + j))

    def gain_spec():
        return pl.BlockSpec((1, HEAD_DIM), lambda i, j: (0, j))

    if with_query:
        args = [lg, gl, feats["rk"], feats["rv"], feats["gk"], feats["gv"], feats["rq"], feats["gq"], feats["rg"],
                feats["gz"], cols, grow, conv, conv, conv, s0, gnw, rmsw]
        in_specs = [smem, smem] + [head] * 8 + [colspec, rowspec, conv_spec(0), conv_spec(1), conv_spec(2),
                                                st_spec, gain_spec(), gain_spec()]
        out_shape = [jax.ShapeDtypeStruct((b, l, MIX_W), BF16)] * 2
        out_specs = [head, head]
        scratch = ([pltpu.VMEM((2, l, HEAD_DIM), BF16)] + [pltpu.VMEM((l, HEAD_DIM), F32)] * 2
                   + [tile_bf16, tile_bf16, tile_bf16, tile_bf16, tile_f32, tile_f32])
    else:
        args = [lg, gl, feats["rk"], feats["rv"], feats["gk"], feats["gv"], cols, grow, conv, conv, s0]
        in_specs = [smem, smem] + [head] * 4 + [colspec, rowspec, conv_spec(1), conv_spec(2), st_spec]
        out_shape = jax.ShapeDtypeStruct((b, N_HEADS, 4, HEAD_DIM, HEAD_DIM), F32)
        out_specs = st_spec
        scratch = [tile_bf16, tile_f32, tile_f32]
    scratch = scratch + [pltpu.VMEM((8, CHUNK, CHUNK), F32), pltpu.VMEM((4, HEAD_DIM, HEAD_DIM), F32)]
    return pl.pallas_call(
        functools.partial(_mixer_kernel, seq_len=l, with_query=with_query),
        grid=(b, N_HEADS),
        in_specs=in_specs,
        out_specs=out_specs,
        out_shape=out_shape,
        scratch_shapes=scratch,
        compiler_params=_cparams(("arbitrary", "arbitrary")),
        name="mixers_latent" if with_query else "mixers_context",
    )(*args)


def _split_bf16(v):
    hi = v.astype(BF16)
    return hi, (v - hi.astype(F32)).astype(BF16)


def _route(h2, wrt_ref, bias_ref, cand_ref, sel_ref, wd_ref):
    tm = h2.shape[0]
    h_hi, h_lo = _split_bf16(h2)
    w_hi, w_lo = _split_bf16(wrt_ref[...])
    logits = _dot_nt(w_hi, h_hi) + (_dot_nt(w_hi, h_lo) + _dot_nt(w_lo, h_hi))
    scores = _sigmoid(logits)
    biased = scores + bias_ref[...]
    neg_inf = float("-inf")
    sub = _iota((GROUP_SIZE, tm), 0).astype(F32)
    group_score = []
    for g in range(N_GROUPS):
        blk = biased[g * GROUP_SIZE:(g + 1) * GROUP_SIZE, :]
        m1 = jnp.max(blk, axis=0, keepdims=True)
        first = jnp.min(jnp.where(blk == m1, sub, float(GROUP_SIZE)), axis=0, keepdims=True)
        m2 = jnp.max(jnp.where(sub == first, neg_inf, blk), axis=0, keepdims=True)
        group_score.append(m1 + m2)
    for g in range(N_GROUPS):
        ahead = jnp.zeros((1, tm), I32)
        for g2 in range(N_GROUPS):
            if g2 == g:
                continue
            before = (group_score[g2] > group_score[g])
            if g2 < g:
                before = before | (group_score[g2] == group_score[g])
            ahead = ahead + before.astype(I32)
        keep = jnp.broadcast_to(ahead, (GROUP_SIZE, tm)) < TOPK_GROUPS
        cand_ref[g * GROUP_SIZE:(g + 1) * GROUP_SIZE, :] = jnp.where(
            keep, biased[g * GROUP_SIZE:(g + 1) * GROUP_SIZE, :], neg_inf)
    work = cand_ref[...]
    eidx = _iota((N_EXPERTS, tm), 0).astype(F32)
    sel = jnp.zeros((N_EXPERTS, tm), jnp.bool_)
    for _ in range(TOP_K):
        best = jnp.max(work, axis=0, keepdims=True)
        first = jnp.min(jnp.where(work == best, eidx, float(N_EXPERTS)), axis=0, keepdims=True)
        pick = eidx == first
        sel = sel | pick
        work = jnp.where(pick, neg_inf, work)
    picked = jnp.where(sel, scores, 0.0)
    wsum = jnp.sum(picked, axis=0, keepdims=True)
    sel_ref[...] = sel.astype(F32)
    wd_ref[...] = picked / wsum * ROUTED_SCALE


def _mixout_kernel(x_ref, yr_ref, yg_ref, g0_ref, g1_ref, g2_ref, g3_ref, wr_ref, wg_ref, wo_ref,
                   npost_ref, gate1_ref, nffn_ref, sc2_ref, sh2_ref, wrt_ref, bias_ref,
                   x1_ref, h2_ref, sel_ref, wd_ref, cand_ref):
    r = _dot(yr_ref[...], wr_ref[...])
    g = _dot(yg_ref[...], wg_ref[...])
    half = r.shape[1] // 2
    merged = jnp.concatenate(
        [g0_ref[...].astype(F32) * r[:, :half] + g2_ref[...].astype(F32) * g[:, :half],
         g1_ref[...].astype(F32) * r[:, half:] + g3_ref[...].astype(F32) * g[:, half:]], axis=1)
    mo = _dot(merged.astype(BF16), wo_ref[...])
    x1 = x_ref[...] + gate1_ref[...] * _rms(mo, npost_ref[...])
    x1_ref[...] = x1
    h2 = _rms(x1, nffn_ref[...]) * (1.0 + sc2_ref[...]) + sh2_ref[...]
    h2_ref[...] = h2.astype(BF16)
    _route(h2, wrt_ref, bias_ref, cand_ref, sel_ref, wd_ref)


def _mixer_out(x, yr, yg, gates, wr, wg, wo, npost, gate1, nffn, sc2, sh2, wrt, bias):
    b, l, d = x.shape
    tm = min(MIXOUT_TM, l)
    tiles = l // tm
    tok = lambda w: pl.BlockSpec((None, tm, w), lambda i, j: (i, j, 0))
    const = functools.partial(pl.BlockSpec, pipeline_mode=pl.Buffered(1))
    vec = lambda: const((1, d), lambda i, j: (0, 0))
    bvec = lambda: pl.BlockSpec((None, 1, d), lambda i, j: (i, 0, 0))
    emap = pl.BlockSpec((N_EXPERTS, tm), lambda i, j: (0, i * tiles + j))
    in_specs = ([tok(d), tok(MIX_W), tok(MIX_W)] + [tok(MIX_W)] * 4
                + [const((MIX_W, d), lambda i, j: (0, 0)), const((MIX_W, d), lambda i, j: (0, 0)),
                   const((d, d), lambda i, j: (0, 0)),
                   vec(), bvec(), vec(), bvec(), bvec(),
                   const((N_EXPERTS, d), lambda i, j: (0, 0)), const((N_EXPERTS, 1), lambda i, j: (0, 0))])
    return pl.pallas_call(
        _mixout_kernel,
        grid=(b, tiles),
        in_specs=in_specs,
        out_specs=[tok(d), tok(d), emap, emap],
        out_shape=[jax.ShapeDtypeStruct((b, l, d), F32), jax.ShapeDtypeStruct((b, l, d), BF16),
                   jax.ShapeDtypeStruct((N_EXPERTS, b * l), F32), jax.ShapeDtypeStruct((N_EXPERTS, b * l), F32)],
        scratch_shapes=[pltpu.VMEM((N_EXPERTS, tm), F32)],
        compiler_params=_cparams(("arbitrary", "arbitrary")),
        name="mixer_out_router",
    )(x, yr, yg, *gates, wr, wg, wo, npost, gate1, nffn, sc2, sh2, wrt, bias)


def _tile_positions(sel):
    tm = sel.shape[1]
    selb = sel.astype(BF16)
    earlier = (_iota((tm, tm), 0) < _iota((tm, tm), 1)).astype(BF16)
    rank = _dot(selb, earlier)
    cnt = _dot(selb, jnp.ones((tm, tm), BF16))
    nmb = jnp.floor((cnt + (MB_ROWS - 1)) * (1.0 / MB_ROWS))
    below = (_iota((N_EXPERTS, N_EXPERTS), 1) < _iota((N_EXPERTS, N_EXPERTS), 0)).astype(BF16)
    offmb = _dot(below, nmb.astype(BF16))
    rank_hi = jnp.floor(rank * (1.0 / MB_ROWS))
    hi = jnp.where(sel > 0.0, offmb + rank_hi, 255.0)
    lo = rank - rank_hi * MB_ROWS
    return hi, lo, cnt, offmb, rank


def _dispatch_kernel(h2_ref, sel_ref, xs_ref, cnt_ref):
    last = pl.num_programs(0) - 1

    @pl.when(pl.program_id(0) < last)
    def _():
        _dispatch_tile(h2_ref, sel_ref, xs_ref, cnt_ref)

    @pl.when(pl.program_id(0) == last)
    def _():
        xs_ref[...] = jnp.zeros_like(xs_ref)


def _dispatch_tile(h2_ref, sel_ref, xs_ref, cnt_ref):
    tm = h2_ref.shape[0]
    sel = sel_ref[...]
    hi, lo, _, _, _ = _tile_positions(sel)
    hib = hi.astype(BF16)
    lob = lo.astype(BF16)
    selb = sel.astype(BF16)
    cnt_row = _dot_nt(jnp.ones((8, tm), BF16), selb)
    nmb_row = jnp.floor((cnt_row + (MB_ROWS - 1)) * (1.0 / MB_ROWS))
    before = (_iota((N_EXPERTS, N_EXPERTS), 0) < _iota((N_EXPERTS, N_EXPERTS), 1)).astype(BF16)
    start_row = _dot(nmb_row.astype(BF16), before) * MB_ROWS
    cnt_ref[...] = cnt_row
    start = start_row[0:1, :]
    end = start + cnt_row[0:1, :]
    x = h2_ref[...]
    rch = 256
    def one_hot(rc):
        r_e = (rc * rch + _iota((rch, N_EXPERTS), 0)).astype(F32)
        owner = ((r_e >= start) & (r_e < end)).astype(BF16)
        r_t = rc * rch + _iota((rch, tm), 0)
        match = ((_dot(owner, hib) == (r_t >> _MB_SHIFT).astype(F32))
                 & (_dot(owner, lob) == (r_t & (MB_ROWS - 1)).astype(F32)))
        return match.astype(BF16)

    n_rc = TILE_ROWS // rch
    n_sure = n_rc - TAIL_CHUNKS
    nxt = one_hot(0)
    for rc in range(n_sure):
        cur = nxt
        if rc + 1 < n_sure:
            nxt = one_hot(rc + 1)
        xs_ref[rc * rch:(rc + 1) * rch, :] = _dot(cur, x).astype(BF16)
    tail_used = jnp.sum(nmb_row[0:1, :]) * MB_ROWS > n_sure * rch

    @pl.when(tail_used)
    def _():
        for rc in range(n_sure, n_rc):
            xs_ref[rc * rch:(rc + 1) * rch, :] = _dot(one_hot(rc), x).astype(BF16)

    @pl.when(jnp.logical_not(tail_used))
    def _():
        xs_ref[n_sure * rch:, :] = jnp.zeros(((n_rc - n_sure) * rch, x.shape[1]), BF16)


def _dispatch(h2, sel_t):
    t, d = h2.shape
    tm = MOE_TM
    nt = t // tm
    return pl.pallas_call(
        _dispatch_kernel,
        grid=(nt + 1,),
        in_specs=[pl.BlockSpec((tm, d), lambda i: (jnp.minimum(i, nt - 1), 0)),
                  pl.BlockSpec((N_EXPERTS, tm), lambda i: (0, jnp.minimum(i, nt - 1)))],
        out_specs=[pl.BlockSpec((TILE_ROWS, d), lambda i: (i, 0)),
                   pl.BlockSpec((None, 8, N_EXPERTS), lambda i: (jnp.minimum(i, nt - 1), 0, 0))],
        out_shape=[jax.ShapeDtypeStruct((nt * TILE_ROWS + (EXP_OUT + 1) * EXP_BM, d), BF16),
                   jax.ShapeDtypeStruct((nt, 8, N_EXPERTS), F32)],
        compiler_params=_cparams(("arbitrary",)),
        name="moe_dispatch",
    )(h2, sel_t)


def _expert_plan(cnt, n_blk):
    nt = cnt.shape[0]
    nmb = (cnt + (MB_ROWS - 1)) // MB_ROWS
    offmb = jnp.cumsum(nmb, axis=1) - nmb
    per_e = nmb.T
    incl = jnp.cumsum(per_e, axis=1)
    excl = incl - per_e
    tot = incl[:, -1]
    nb = (tot + (EXP_MB - 1)) // EXP_MB
    bend = jnp.cumsum(nb)
    bstart = bend - nb
    b = jnp.arange(n_blk, dtype=I32)
    valid = b < bend[-1]
    last = jnp.maximum(bend[-1] - 1, 0)
    bq = jnp.where(valid, b, last)
    blk_e = jnp.minimum(jnp.sum((bend[None, :] <= bq[:, None]).astype(I32), axis=1), N_EXPERTS - 1)
    onehot_e = (blk_e[:, None] == jnp.arange(N_EXPERTS, dtype=I32)[None, :]).astype(I32)
    pick = lambda table: jnp.dot(onehot_e.astype(F32), table.astype(F32), precision=HIGHEST).astype(I32)
    bstart_b = jnp.sum(onehot_e * bstart[None, :], axis=1)
    tot_b = jnp.sum(onehot_e * tot[None, :], axis=1)
    p0 = (bq - bstart_b) * EXP_MB
    blk_nmb = jnp.where(valid, jnp.clip(tot_b - p0, 0, EXP_MB), 0).astype(I32)
    p = p0[:, None] + jnp.arange(EXP_MB, dtype=I32)[None, :]
    incl_b = pick(incl)
    passed = (incl_b[:, None, :] <= p[:, :, None]).astype(I32)
    tile = jnp.minimum(jnp.sum(passed, axis=2), nt - 1)
    shift_b = pick(offmb.T - excl)
    step_b = jnp.concatenate([shift_b[:, 1:] - shift_b[:, :-1], jnp.zeros_like(shift_b[:, :1])], axis=1)
    shift = shift_b[:, :1] + jnp.sum(passed * step_b[:, None, :], axis=2)
    where = tile * _TILE_MB + p + shift
    j = jnp.arange(EXP_MB, dtype=I32)[None, :]
    used = j < blk_nmb[:, None]
    spare = nt * _TILE_MB + (b[:, None] % EXP_OUT) * EXP_MB + j
    zeros_mb = nt * _TILE_MB + EXP_OUT * EXP_MB + j
    src = jnp.where(used, where, jnp.where(valid[:, None], where[:, :1], zeros_mb))
    dst = jnp.where(used, where, spare)
    dst = jnp.concatenate([nt * _TILE_MB + (EXP_OUT - 1) * EXP_MB + j, dst], axis=0)
    return bstart.astype(I32), nb.astype(I32), src.reshape(-1).astype(I32), dst.reshape(-1).astype(I32)


def _expert_kernel(bstart_ref, nb_ref, src_ref, dst_ref, xs_hbm, wg_ref, wu_ref, wd_ref, ys_hbm,
                   xbuf, ybuf, wgub, wdb, in_sem, out_sem):
    e = pl.program_id(0)
    n_in = xbuf.shape[0]
    n_out = ybuf.shape[0]

    def gather_copy(blk, slot, j):
        rows = pl.ds(pl.multiple_of(src_ref[blk * EXP_MB + j] * MB_ROWS, MB_ROWS), MB_ROWS)
        return pltpu.make_async_copy(xs_hbm.at[rows, :], xbuf.at[slot, j * MB_ROWS:(j + 1) * MB_ROWS, :],
                                     in_sem.at[slot])

    def scatter_copy(blk, slot, j):
        rows = pl.ds(pl.multiple_of(dst_ref[(blk + 1) * EXP_MB + j] * MB_ROWS, MB_ROWS), MB_ROWS)
        return pltpu.make_async_copy(ybuf.at[slot, j * MB_ROWS:(j + 1) * MB_ROWS, :], ys_hbm.at[rows, :],
                                     out_sem.at[slot])

    def start_gather(blk):
        for j in range(EXP_MB):
            gather_copy(blk, lax.rem(blk, n_in), j).start()

    def wait_gather(blk):
        for j in range(EXP_MB):
            gather_copy(blk, lax.rem(blk, n_in), j).wait()

    def start_scatter(blk):
        for j in range(EXP_MB):
            scatter_copy(blk, lax.rem(blk + n_out, n_out), j).start()

    def wait_scatter(slot):
        for j in range(EXP_MB):
            scatter_copy(0, slot, j).wait()

    @pl.when(e == 0)
    def _():
        ybuf[...] = jnp.zeros_like(ybuf)
        for ahead in range(EXP_LEAD):
            start_gather(ahead)
        spare_row0 = ys_hbm.shape[0] - (n_out + 1) * EXP_BM
        for slot in range(n_out - 1):
            for j in range(EXP_MB):
                spare = pl.ds(spare_row0 + (slot * EXP_MB + j) * MB_ROWS, MB_ROWS)
                pltpu.make_async_copy(ybuf.at[slot, j * MB_ROWS:(j + 1) * MB_ROWS, :], ys_hbm.at[spare, :],
                                      out_sem.at[slot]).start()

    de = wg_ref.shape[1]
    wgub[:, :de] = wg_ref[...].astype(BF16)
    wgub[:, de:] = wu_ref[...].astype(BF16)
    wdb[...] = wd_ref[...].astype(BF16)

    def block(b, carry):
        slot = lax.rem(b, n_out)
        wait_gather(b)
        wait_scatter(slot)
        x = xbuf[lax.rem(b, n_in)]
        gu = _dot(x, wgub[...])
        act = (_silu(gu[:, :de]) * gu[:, de:]).astype(BF16)
        start_gather(b + EXP_LEAD)
        start_scatter(b - 1)
        ybuf[slot] = _dot(act, wdb[...]).astype(BF16)
        return carry

    first = bstart_ref[e]
    lax.fori_loop(first, first + nb_ref[e], block, 0)

    @pl.when(e == pl.num_programs(0) - 1)
    def _():
        total = first + nb_ref[e]
        start_scatter(total - 1)
        for ahead in range(EXP_LEAD):
            wait_gather(total + ahead)
        for slot in range(n_out):
            wait_scatter(slot)


def _expert_ffn(xs, bstart, nb, src, dst, wg, wu, wd):
    rows, d = xs.shape
    de = wg.shape[2]
    grid_spec = pltpu.PrefetchScalarGridSpec(
        num_scalar_prefetch=4,
        grid=(wg.shape[0],),
        in_specs=[pl.BlockSpec(memory_space=pl.ANY),
                  pl.BlockSpec((None, d, de), lambda i, bs, bn, sr, ds: (i, 0, 0)),
                  pl.BlockSpec((None, d, de), lambda i, bs, bn, sr, ds: (i, 0, 0)),
                  pl.BlockSpec((None, de, d), lambda i, bs, bn, sr, ds: (i, 0, 0))],
        out_specs=pl.BlockSpec(memory_space=pl.ANY),
        scratch_shapes=[pltpu.VMEM((EXP_LEAD + 1, EXP_BM, d), BF16), pltpu.VMEM((EXP_OUT, EXP_BM, d), BF16),
                        pltpu.VMEM((d, 2 * de), BF16), pltpu.VMEM((de, d), BF16),
                        pltpu.SemaphoreType.DMA((EXP_LEAD + 1,)), pltpu.SemaphoreType.DMA((EXP_OUT,))],
    )
    return pl.pallas_call(
        _expert_kernel,
        grid_spec=grid_spec,
        out_shape=jax.ShapeDtypeStruct((rows, d), BF16),
        input_output_aliases={4: 0},
        compiler_params=_cparams(("arbitrary",)),
        name="moe_experts",
    )(bstart, nb, src, dst, xs, wg, wu, wd)


def _combine_kernel(ys_ref, sel_ref, wd_ref, h2_ref, x1_ref, wsg_ref, wsu_ref, wsd_ref, npost_ref, gate2_ref,
                    o_ref, c_ref, acc_ref):
    tm = h2_ref.shape[0]
    sel = sel_ref[...]
    _, _, cnt, offmb, rank = _tile_positions(sel)
    no_rank = 1.5 * tm
    lhs = jnp.concatenate([jnp.where(sel > 0.0, rank, no_rank).T, wd_ref[...].T], axis=1).astype(BF16)
    start = offmb[:, :LANES] * MB_ROWS
    end = start + cnt[:, :LANES]
    zeros = jnp.zeros((N_EXPERTS, LANES), BF16)

    def build(cc):
        r_e = (cc * LANES + _iota((N_EXPERTS, LANES), 1)).astype(F32)
        owned = (r_e >= start) & (r_e < end)
        owner = owned.astype(BF16)
        local = r_e[0:1, :] - jnp.sum(jnp.where(owned, start, 0.0), axis=0, keepdims=True)
        rhs = jnp.concatenate([jnp.concatenate([owner, zeros], axis=1),
                               jnp.concatenate([zeros, owner], axis=1)], axis=0)
        got = _dot(lhs, rhs)
        c_ref[:, cc * LANES:(cc + 1) * LANES] = jnp.where(got[:, :LANES] == local, got[:, LANES:], 0.0).astype(BF16)

    n_cc = TILE_ROWS // LANES
    sure = TILE_ROWS - TAIL_CHUNKS * 256
    for cc in range(sure // LANES):
        build(cc)
    h2 = h2_ref[...]
    shared = _dot((_silu(_dot(h2, wsg_ref[...])) * _dot(h2, wsu_ref[...])).astype(BF16), wsd_ref[...])
    acc_ref[...] = shared + _dot(c_ref[:, :sure], ys_ref[:sure, :])
    nmb = jnp.floor((cnt[:, 0:1] + (MB_ROWS - 1)) * (1.0 / MB_ROWS))

    @pl.when(jnp.sum(nmb) * MB_ROWS > sure)
    def _():
        for cc in range(sure // LANES, n_cc):
            build(cc)
        acc_ref[...] += _dot(c_ref[:, sure:], ys_ref[sure:, :])

    o_ref[...] = x1_ref[...] + gate2_ref[...] * _rms(acc_ref[...], npost_ref[...])


def _combine(ys, sel_t, wd_t, h2, x1, wsg, wsu, wsd, npost, gate2, seq_len):
    t, d = h2.shape
    tm = MOE_TM
    nt = t // tm
    per_seq = seq_len // tm
    ds = wsg.shape[1]
    const = functools.partial(pl.BlockSpec, pipeline_mode=pl.Buffered(1))
    emap = pl.BlockSpec((N_EXPERTS, tm), lambda i: (0, i))
    return pl.pallas_call(
        _combine_kernel,
        grid=(nt,),
        in_specs=[pl.BlockSpec((TILE_ROWS, d), lambda i: (i, 0)), emap, emap,
                  pl.BlockSpec((tm, d), lambda i: (i, 0)), pl.BlockSpec((tm, d), lambda i: (i, 0)),
                  const((d, ds), lambda i: (0, 0)), const((d, ds), lambda i: (0, 0)), const((ds, d), lambda i: (0, 0)),
                  const((1, d), lambda i: (0, 0)),
                  pl.BlockSpec((None, 1, d), lambda i: (i // per_seq, 0, 0))],
        out_specs=pl.BlockSpec((tm, d), lambda i: (i, 0)),
        out_shape=jax.ShapeDtypeStruct((t, d), F32),
        scratch_shapes=[pltpu.VMEM((tm, TILE_ROWS), BF16), pltpu.VMEM((tm, d), F32)],
        compiler_params=_cparams(("arbitrary",)),
        name="moe_combine",
    )(ys, sel_t, wd_t, h2, x1, wsg, wsu, wsd, npost, gate2)


def _rope_tables(n):
    rows = n // GRID_W
    pos_r = jnp.repeat(jnp.arange(rows, dtype=F32), GRID_W)
    pos_c = jnp.tile(jnp.arange(GRID_W, dtype=F32), rows)
    n_freq = HEAD_DIM // 4
    inv = ROPE_BASE ** (-jnp.arange(n_freq, dtype=F32) / n_freq)
    ang = jnp.concatenate([pos_r[:, None] * inv, pos_c[:, None] * inv], axis=-1)
    cos, sin = jnp.cos(ang), jnp.sin(ang)
    return jnp.concatenate([cos, cos], axis=-1), jnp.concatenate([-sin, sin], axis=-1)


def kernel(x, c, ctx, c_ctx, w_mod, b_mod, norm_mix_pre, norm_mix_post, norm_ffn_pre, norm_ffn_post, w_in, gdn_conv, ret_log_decay, gdn_a_log, gdn_dt_bias, ret_gn_w, gdn_norm_w, w_ret_out, w_gdn_out, w_o, w_router, router_bias, w_gate, w_up, w_down, w_sh_gate, w_sh_up, w_sh_down):
    b, n, d = x.shape
    depth = w_mod.shape[0]
    assert depth == 1, "single-layer block"
    assert all(n % min(tile, n) == 0 for tile in (PROJ_TM, MIXOUT_TM, MOE_TM)) and n % MOE_TM == 0
    assert ctx.shape[1] % CHUNK == 0 and (EXP_OUT + 1) * EXP_BM <= TILE_ROWS
    assert _TILE_MB < 255

    rows = -(-(b + 1) // 8) * 8
    cvec = jnp.zeros((rows, d), F32).at[:b].set(c).at[b].set(c_ctx)
    mod = _modulation(cvec, w_mod[0], b_mod[0][None, :])
    sh1, sc1, g1, sh2, sc2, g2 = [mod[:b, k * d:(k + 1) * d][:, None, :] for k in range(6)]
    ctx_shift = jnp.broadcast_to(mod[b, 0:d][None, None, :], (b, 1, d))
    ctx_scale = jnp.broadcast_to(mod[b, d:2 * d][None, None, :], (b, 1, d))

    w_in0 = w_in[0]
    n_main = 4 * MIX_W
    w_state = w_in0[:, :n_main].astype(BF16)
    slots = ((0, 0), (0, 1), (1, 0), (1, 1), (0, 0), (0, 1), (0, 0), (0, 1))
    gab_cols = n_main + jnp.array([[ab * 2 * N_HEADS + dr * N_HEADS + hh for ab, dr in slots]
                                   for hh in range(N_HEADS)], I32)
    w_gab = w_in0[:, gab_cols.reshape(-1)].astype(BF16)
    n_state = n_main + N_GAB
    w_query = w_in0[:, n_state:].astype(BF16)
    a_coef = -jnp.exp(gdn_a_log[0].astype(F32))
    dtb = gdn_dt_bias[0].astype(F32)
    is_alpha = jnp.array([ab == 0 for ab, _ in slots])
    dirs = jnp.array([dr for _, dr in slots], I32)
    prow = jnp.stack([jnp.where(is_alpha[None, :], a_coef.T[:, dirs], 0.0).reshape(-1),
                      jnp.where(is_alpha[None, :], dtb.T[:, dirs], 0.0).reshape(-1)], axis=0)
    first2 = jnp.tile(jnp.arange(8) < 2, N_HEADS)
    w_gab_t = w_gab.T * first2[:, None].astype(BF16)
    pcol = prow.T * first2[:, None]
    gain_mix = norm_mix_pre[0][None, :]
    cos2, sin2 = _rope_tables(n)

    state_kinds = (_ROPE_SCALED, _PLAIN, _PLAIN, _PLAIN)
    query_kinds = (_ROPE, _PLAIN, _PLAIN, _PLAIN, _SIGMOID, _SIGMOID, _SIGMOID, _SIGMOID)
    lg = ret_log_decay[0].astype(F32)
    conv = gdn_conv[0].astype(F32)

    lc = ctx.shape[1]
    cfe, (ccols, cgrow, cgl) = _in_projection(
        ctx, gain_mix, ctx_scale, ctx_shift, (w_state,), w_gab, w_gab_t, prow, pcol,
        cos2[:lc], sin2[:lc], state_kinds, rope=False)
    cfeats = dict(zip(("rk", "rv", "gk", "gv"), cfe))
    zero_state = jnp.zeros((b, N_HEADS, 4, HEAD_DIM, HEAD_DIM), F32)
    init = _mixers(lg, cgl, cfeats, (ccols, cgrow), conv, zero_state, None, None, with_query=False)

    fe, (cols, grow, gl) = _in_projection(
        x, gain_mix, sc1, sh1, (w_state, w_query), w_gab, w_gab_t, prow, pcol, cos2, sin2,
        state_kinds + query_kinds, rope=True)
    feats = dict(zip(("rk", "rv", "gk", "gv", "rq", "rg", "gq", "gz"), fe[:8]))
    gates = fe[8:]
    y_ret, y_gdn = _mixers(lg, gl, feats, (cols, grow), conv, init,
                           ret_gn_w[0][None, :], gdn_norm_w[0][None, :], with_query=True)
    x1, h2, sel_t, wd_t = _mixer_out(
        x, y_ret, y_gdn, gates, w_ret_out[0].astype(BF16), w_gdn_out[0].astype(BF16), w_o[0].astype(BF16),
        norm_mix_post[0][None, :], g1, norm_ffn_pre[0][None, :], sc2, sh2,
        w_router[0].T.astype(F32), router_bias[0].astype(F32)[:, None])

    t = b * n
    h2f = h2.reshape(t, d)
    xs, cnt = _dispatch(h2f, sel_t)
    nt = t // MOE_TM
    n_blk = nt * _TILE_MB // EXP_MB + N_EXPERTS + EXP_LEAD
    bstart, nb, src, dst = _expert_plan(cnt[:, 0, :].astype(I32), n_blk)
    ys = _expert_ffn(xs, bstart, nb, src, dst, w_gate[0], w_up[0], w_down[0])
    out = _combine(ys, sel_t, wd_t, h2f, x1.reshape(t, d), w_sh_gate[0].astype(BF16), w_sh_up[0].astype(BF16),
                   w_sh_down[0].astype(BF16), norm_ffn_post[0][None, :], g2, n)
    return out.reshape(b, n, d)
```

```python
import functools
import math

import jax
import jax.numpy as jnp
from jax import lax
from jax.experimental import pallas as pl
from jax.experimental.pallas import tpu as pltpu

F32 = jnp.float32
BF16 = jnp.bfloat16
I32 = jnp.int32
HIGHEST = lax.Precision.HIGHEST

N_HEADS = 4
HEAD_DIM = 128
MIX_W = N_HEADS * HEAD_DIM
CHUNK = 128
SHORT_CONV = 3
ROPE_BASE = 10000.0
GRID_W = 64
N_EXPERTS = 64
TOP_K = 8
N_GROUPS = 8
TOPK_GROUPS = 4
GROUP_SIZE = N_EXPERTS // N_GROUPS
ROUTED_SCALE = 2.5
EPS = 1e-6
N_GAB = 4 * N_HEADS

LANES = 128
BF16_TILE_ROWS = 16
VMEM_LIMIT_BYTES = 56 * 1024 * 1024

PROJ_TM = 1024
MIXOUT_TM = 1024
MOE_TM = 256
MB_ROWS = BF16_TILE_ROWS
_MB_SHIFT = MB_ROWS.bit_length() - 1
_TILE_MB = -(-(TOP_K * MOE_TM // MB_ROWS + N_EXPERTS * (MB_ROWS - 1) // MB_ROWS + 1) // 8) * 8
TILE_ROWS = _TILE_MB * MB_ROWS
ROW_CHUNK = 256
TAIL_CHUNKS = 2
NO_SLOT = 255.0
EXP_BM = 512
EXP_MB = EXP_BM // MB_ROWS
EXP_LEAD = 3
EXP_OUT = 3


def _cparams(sem):
    return pltpu.CompilerParams(dimension_semantics=sem, vmem_limit_bytes=VMEM_LIMIT_BYTES)


def _sigmoid(v):
    return 0.5 * jnp.tanh(0.5 * v) + 0.5


def _silu(v):
    return v * _sigmoid(v)


def _softplus(v):
    return jnp.maximum(v, 0.0) + jnp.log1p(jnp.exp(-jnp.abs(v)))


def _iota(shape, dim):
    return lax.broadcasted_iota(I32, shape, dim)


def _dot(a, b, **kw):
    return jnp.dot(a, b, preferred_element_type=F32, **kw)


def _dot_nt(a, b, **kw):
    return lax.dot_general(a, b, (((1,), (1,)), ((), ())), preferred_element_type=F32, **kw)


def _dot_tn(a, b, **kw):
    return lax.dot_general(a, b, (((0,), (0,)), ((), ())), preferred_element_type=F32, **kw)


def _rms(v, gain):
    return v * lax.rsqrt(jnp.mean(v * v, axis=-1, keepdims=True) + EPS) * gain


def _mod_kernel(c_ref, w_ref, b_ref, o_ref):
    o_ref[...] = _dot(_silu(c_ref[...]), w_ref[...], precision=HIGHEST) + b_ref[...]


def _modulation(cvec, w_mod, b_mod):
    rows, d = cvec.shape
    n = w_mod.shape[1]
    tn = 1024
    return pl.pallas_call(
        _mod_kernel,
        grid=(n // tn,),
        in_specs=[pl.BlockSpec((rows, d), lambda j: (0, 0)),
                  pl.BlockSpec((d, tn), lambda j: (0, j)),
                  pl.BlockSpec((1, tn), lambda j: (0, j))],
        out_specs=pl.BlockSpec((rows, tn), lambda j: (0, j)),
        out_shape=jax.ShapeDtypeStruct((rows, n), F32),
        compiler_params=_cparams(("arbitrary",)),
        name="adaln_modulation",
    )(cvec, w_mod, b_mod)


_PLAIN, _ROPE, _ROPE_SCALED, _SIGMOID = 0, 1, 2, 3


def _proj_kernel(x_ref, gain_ref, sc_ref, sh_ref, *refs, kinds, rope, n_w):
    w_refs = refs[:n_w]
    wg_ref, wgt_ref, prow_ref, pcol_ref, cos_ref, sin_ref = refs[n_w:n_w + 6]
    out_refs = refs[n_w + 6:]
    groups = [(w, c) for w in w_refs for c in range(0, w.shape[1], MIX_W)]
    n_feat = len(kinds)
    feat_refs = out_refs[:n_feat]
    cols_ref, grow_ref, gl_ref = out_refs[n_feat:]
    tm = x_ref.shape[0]

    x = x_ref[...]
    h = (_rms(x, gain_ref[...]) * (1.0 + sc_ref[...]) + sh_ref[...]).astype(BF16)

    r_i = _iota((CHUNK, CHUNK), 0)
    c_i = _iota((CHUNK, CHUNK), 1)
    lower_incl = (c_i <= r_i).astype(BF16)
    upper_incl = (c_i >= r_i).astype(BF16)

    def prefix_rows(v):
        hi, lo = _split_bf16(v)
        return _dot(lower_incl, hi) + _dot(lower_incl, lo)

    def prefix_lanes(v):
        hi, lo = _split_bf16(v)
        return _dot(hi, upper_incl) + _dot(lo, upper_incl)

    nc = N_HEADS * 8
    pg = _dot(h, wg_ref[...])
    colt = _iota((tm, nc), 1) & 7
    la = jnp.where((colt == 2) | (colt == 3), 0.0, prow_ref[0:1, :] * _softplus(pg + prow_ref[1:2, :]))
    beta = _sigmoid(pg)
    colc = _iota((CHUNK, nc), 1) & 7
    fwd_col = (colc & 1) == 0
    for c in range(tm // CHUNK):
        sl = slice(c * CHUNK, (c + 1) * CHUNK)
        la_c = la[sl]
        pre = prefix_rows(la_c)
        suf = pre[CHUNK - 1:CHUNK, :] - pre + la_c
        g_c = jnp.where(fwd_col, pre, suf)
        rest = jnp.where(fwd_col, suf, pre) - la_c
        vals = jnp.where(colc < 2, g_c, jnp.where(colc < 4, beta[sl], jnp.where(
            colc < 6, jnp.exp(g_c), jnp.exp(rest))))
        for hh in range(N_HEADS):
            cols_ref[hh, sl, :] = vals[:, 8 * hh:8 * hh + 8]
        gl_ref[c:c + 1, :] = jnp.exp(g_c[0:1, :] + rest[0:1, :])

    pgt = _dot_nt(wgt_ref[...], h)
    rowq = _iota((N_HEADS * 8, tm), 0) & 7
    lat = jnp.where(rowq < 2, pcol_ref[:, 0:1] * _softplus(pgt + pcol_ref[:, 1:2]), 0.0)
    rowc = _iota((N_HEADS * 8, CHUNK), 0) & 7
    for c in range(tm // CHUNK):
        sl = slice(c * CHUNK, (c + 1) * CHUNK)
        lat_c = lat[:, sl]
        pre_t = prefix_lanes(lat_c)
        suf_t = pre_t[:, CHUNK - 1:CHUNK] - pre_t + lat_c
        grow_ref[:, :, sl] = jnp.where(rowc == 0, pre_t, suf_t).reshape(N_HEADS, 8, CHUNK)

    if rope:
        cos2 = cos_ref[...]
        sin2 = sin_ref[...]

    for g, kind in enumerate(kinds):
        w_ref, c0 = groups[g]
        p = _dot(h, w_ref[:, c0:c0 + MIX_W])
        if kind == _SIGMOID:
            p = _sigmoid(p)
        elif kind in (_ROPE, _ROPE_SCALED) and rope:
            heads = []
            for hh in range(N_HEADS):
                t = p[:, hh * HEAD_DIM:(hh + 1) * HEAD_DIM]
                heads.append(t * cos2 + pltpu.roll(t, HEAD_DIM // 2, 1) * sin2)
            p = jnp.concatenate(heads, axis=1)
        if kind == _ROPE_SCALED:
            p = p * (HEAD_DIM ** -0.5)
        feat_refs[g][...] = p.astype(feat_refs[g].dtype)


def _in_projection(x, gain, scale, shift, w_mains, w_gab, w_gab_t, prow, pcol, cos2, sin2, kinds, rope):
    b, l, d = x.shape
    tm = min(PROJ_TM, l)
    tiles = l // tm
    n_chunk = tm // CHUNK
    feat_shapes = [jax.ShapeDtypeStruct((b, l, MIX_W), BF16) for _ in kinds]
    feat_specs = [pl.BlockSpec((None, tm, MIX_W), lambda i, j: (i, j, 0)) for _ in kinds]
    out_shape = feat_shapes + [jax.ShapeDtypeStruct((b, N_HEADS, l, 8), F32),
                               jax.ShapeDtypeStruct((b, N_HEADS, 8, l), F32),
                               jax.ShapeDtypeStruct((b, tiles, n_chunk, N_HEADS * 8), F32)]
    out_specs = feat_specs + [pl.BlockSpec((None, N_HEADS, tm, 8), lambda i, j: (i, 0, j, 0)),
                              pl.BlockSpec((None, N_HEADS, 8, tm), lambda i, j: (i, 0, 0, j)),
                              pl.BlockSpec((None, None, n_chunk, N_HEADS * 8), lambda i, j: (i, j, 0, 0))]
    const = functools.partial(pl.BlockSpec, pipeline_mode=pl.Buffered(1))
    in_specs = [
        pl.BlockSpec((None, tm, d), lambda i, j: (i, j, 0)),
        const((1, d), lambda i, j: (0, 0)),
        pl.BlockSpec((None, 1, d), lambda i, j: (i, 0, 0)),
        pl.BlockSpec((None, 1, d), lambda i, j: (i, 0, 0)),
    ] + [const(w.shape, lambda i, j: (0, 0)) for w in w_mains] + [
        const((d, N_HEADS * 8), lambda i, j: (0, 0)),
        const((N_HEADS * 8, d), lambda i, j: (0, 0)),
        const((2, N_HEADS * 8), lambda i, j: (0, 0)),
        const((N_HEADS * 8, 2), lambda i, j: (0, 0)),
        pl.BlockSpec((tm, HEAD_DIM), lambda i, j: (j, 0)),
        pl.BlockSpec((tm, HEAD_DIM), lambda i, j: (j, 0)),
    ]
    outs = pl.pallas_call(
        functools.partial(_proj_kernel, kinds=tuple(kinds), rope=rope, n_w=len(w_mains)),
        grid=(b, tiles),
        in_specs=in_specs,
        out_specs=out_specs,
        out_shape=out_shape,
        compiler_params=_cparams(("arbitrary", "arbitrary")),
        name="in_projection_rope" if rope else "in_projection_ctx",
    )(x, gain, scale, shift, *w_mains, w_gab, w_gab_t, prow, pcol, cos2, sin2)
    feats = outs[:len(kinds)]
    cols, grow, gl = outs[len(kinds):]
    return feats, (cols, grow, gl.reshape(b * (l // CHUNK), N_HEADS * 8))


def _unit_triangular_inverses(mats, lowers):
    r = _iota((CHUNK, CHUNK), 0)
    c = _iota((CHUNK, CHUNK), 1)
    eye = (r == c).astype(F32)
    invs = [eye - jnp.where((r >> 1) == (c >> 1), a, 0.0) for a in mats]
    for level in range(1, int(math.log2(CHUNK))):
        s = 1 << level
        mask = ((r >> (level + 1)) == (c >> (level + 1))) & ((r >> level) != (c >> level))
        invb = [inv.astype(BF16) for inv in invs]
        offs = [jnp.where(mask, a, 0.0).astype(BF16) for a in mats]
        if s < 8:
            half = [_dot(off, ib).astype(BF16) for off, ib in zip(offs, invb)]
            invs = [inv - _dot(ib, hf) for inv, ib, hf in zip(invs, invb, half)]
            continue
        def rows_of(x, lower, moving):
            first = s if (lower == moving) else 0
            return [x[g * 2 * s + first:g * 2 * s + first + s] for g in range(CHUNK // (2 * s))]

        half = [_dot(jnp.concatenate(rows_of(off, lo, True), axis=0), ib).astype(BF16)
                for off, ib, lo in zip(offs, invb, lowers)]
        zero = jnp.zeros((s, CHUNK), BF16)
        full = []
        for hf, lo in zip(half, lowers):
            pieces = []
            for g in range(CHUNK // (2 * s)):
                piece = hf[g * s:(g + 1) * s]
                pieces += [zero, piece] if lo else [piece, zero]
            full.append(jnp.concatenate(pieces, axis=0))
        corr = [_dot(jnp.concatenate(rows_of(ib, lo, True), axis=0), hf) for ib, hf, lo in zip(invb, full, lowers)]
        new = []
        for inv, cr, lo in zip(invs, corr, lowers):
            keep = rows_of(inv, lo, False)
            moved = [m - cr[g * s:(g + 1) * s] for g, m in enumerate(rows_of(inv, lo, True))]
            pieces = []
            for k, m in zip(keep, moved):
                pieces += [k, m] if lo else [m, k]
            new.append(jnp.concatenate(pieces, axis=0))
        invs = new
    return invs


def _mixer_kernel(*refs, seq_len, with_query):
    n_chunk = seq_len // CHUNK
    if with_query:
        (lg_ref, gl_ref, rk_ref, rv_ref, gk_ref, gv_ref, rq_ref, gq_ref, rg_ref, gz_ref,
         cols_ref, grow_ref, cq_ref, ck_ref, cv_ref, s0_ref, gnw_ref, rmsw_ref,
         yret_ref, ygdn_ref,
         qes, oret, ogdn, ubuf, wbuf, pbuf, nbuf, cbuf, kvbuf, rtile, state) = refs
    else:
        (lg_ref, gl_ref, rk_ref, rv_ref, gk_ref, gv_ref,
         cols_ref, grow_ref, ck_ref, cv_ref, s0_ref,
         sfin_ref,
         nbuf, cbuf, kvbuf, rtile, state) = refs
    bi = pl.program_id(0)
    hi = pl.program_id(1)

    row = _iota((CHUNK, CHUNK), 0)
    colm = _iota((CHUNK, CHUNK), 1)
    rowf = row.astype(F32)
    colf = colm.astype(F32)

    def conv_chunk(src_ref, w_ref, n):
        s = pl.multiple_of(n * CHUNK, CHUNK)
        x = src_ref[pl.ds(s, CHUNK), :].astype(F32)
        ps = pl.multiple_of(jnp.maximum(s - BF16_TILE_ROWS, 0), BF16_TILE_ROWS)
        ns = pl.multiple_of(jnp.minimum(s + CHUNK, seq_len - BF16_TILE_ROWS), BF16_TILE_ROWS)
        prev_row = src_ref[pl.ds(ps, BF16_TILE_ROWS), :].astype(F32)[BF16_TILE_ROWS - 1:BF16_TILE_ROWS, :]
        next_row = src_ref[pl.ds(ns, BF16_TILE_ROWS), :].astype(F32)[0:1, :]
        prev_row = prev_row * jnp.where(n > 0, 1.0, 0.0)
        next_row = next_row * jnp.where(n < n_chunk - 1, 1.0, 0.0)
        xp = jnp.where(row == 0, jnp.broadcast_to(prev_row, (CHUNK, HEAD_DIM)), pltpu.roll(x, 1, 0))
        xn = jnp.where(row == CHUNK - 1, jnp.broadcast_to(next_row, (CHUNK, HEAD_DIM)), pltpu.roll(x, CHUNK - 1, 0))
        return _silu(w_ref[0:1, :] * xp + w_ref[1:2, :] * x + w_ref[2:3, :] * xn)

    def l2n(v):
        return v * lax.rsqrt(jnp.sum(v * v, axis=-1, keepdims=True) + EPS)

    for d in range(2):
        lg = lg_ref[d, hi]
        if d == 0:
            dist, pos_q, pos_k = rowf - colf, rowf + 1.0, (CHUNK - 1.0) - rowf
        else:
            dist, pos_q, pos_k = colf - rowf, CHUNK - rowf, rowf
        rtile[4 * d + 0] = jnp.where(dist >= 0, jnp.exp(lg * jnp.maximum(dist, 0.0)), 0.0)
        rtile[4 * d + 1] = jnp.exp(lg * pos_q)
        rtile[4 * d + 2] = jnp.exp(lg * pos_k)
        rtile[4 * d + 3] = jnp.exp(lg * jnp.full((CHUNK, CHUNK), float(CHUNK), F32))

    state[...] = s0_ref[...]

    def bcast_col(cols, j):
        return jnp.broadcast_to(cols[:, j:j + 1], (CHUNK, CHUNK))

    pre_chunks = min(8, n_chunk)

    def prepass(m, carry):
        chunks = [m * pre_chunks + j for j in range(pre_chunks)]
        chunk_cs = [pl.ds(pl.multiple_of(n * CHUNK, CHUNK), CHUNK) for n in chunks]
        conv_k, conv_v, conv_q = [], [], []
        for n, cs in zip(chunks, chunk_cs):
            conv_k.append(l2n(conv_chunk(gk_ref, ck_ref, n)).astype(BF16))
            conv_v.append(conv_chunk(gv_ref, cv_ref, n).astype(BF16))
            if with_query:
                conv_q.append((l2n(conv_chunk(gq_ref, cq_ref, n)) * (HEAD_DIM ** -0.5)).astype(BF16))
                ogdn[cs, :] = jnp.zeros((CHUNK, HEAD_DIM), F32)

        jobs = [(j, d) for j in range(pre_chunks) for d in range(2)]
        css = [chunk_cs[j] for j, _ in jobs]
        slots = [d * n_chunk + chunks[j] for j, d in jobs]
        dirs = [d for _, d in jobs]
        ks = [rk_ref[cs, :] for cs in css]
        vs = [rv_ref[cs, :] for cs in css]
        if with_query:
            scs = [(_dot_nt(rq_ref[cs, :], k) * rtile[4 * d + 0]).astype(BF16) for cs, k, d in zip(css, ks, dirs)]
            outs = [_dot(sc, v) for sc, v in zip(scs, vs)]
            for j, cs in enumerate(chunk_cs):
                oret[cs, :] = outs[2 * j] + outs[2 * j + 1]
        kvs = [_dot_tn((k.astype(F32) * rtile[4 * d + 2]).astype(BF16), v) for k, v, d in zip(ks, vs, dirs)]
        for slot, kv in zip(slots, kvs):
            kvbuf[slot] = kv
        ks = [conv_k[j] for j, _ in jobs]
        vs = [conv_v[j] for j, _ in jobs]
        colss = [cols_ref[cs, :] for cs in css]
        betas = [bcast_col(cols, 2 + d) for cols, d in zip(colss, dirs)]
        incls = [(row >= colm) if d == 0 else (row <= colm) for d in dirs]
        stricts = [(row > colm) if d == 0 else (row < colm) for d in dirs]
        decs = [jnp.exp(jnp.where(incl, bcast_col(cols, d) - jnp.broadcast_to(grow_ref[d:d + 1, cs], (CHUNK, CHUNK)), 0.0))
                for cols, d, cs, incl in zip(colss, dirs, css, incls)]
        kks = [_dot_nt(k, k) for k in ks]
        mats = [kk * beta * jnp.where(strict, dec, 0.0) for kk, beta, strict, dec in zip(kks, betas, stricts, decs)]
        tinvs = [t.astype(BF16) for t in _unit_triangular_inverses(mats, [d == 0 for d in dirs])]
        rhs = [jnp.concatenate([(beta * v.astype(F32)).astype(BF16),
                                (beta * bcast_col(cols, 4 + d) * k.astype(F32)).astype(BF16)], axis=1)
               for beta, v, cols, d, k in zip(betas, vs, colss, dirs, ks)]
        uws = [_dot(t, r) for t, r in zip(tinvs, rhs)]
        us = [uw[:, :HEAD_DIM] for uw in uws]
        ws = [uw[:, HEAD_DIM:].astype(BF16) for uw in uws]
        kts = [(k.astype(F32) * bcast_col(cols, 6 + d)).astype(BF16) for k, cols, d in zip(ks, colss, dirs)]
        ncs = [_dot_tn(kt, uw.astype(BF16)) for kt, uw in zip(kts, uws)]
        for slot, nc_mat in zip(slots, ncs):
            cbuf[slot] = nc_mat[:, :HEAD_DIM]
            nbuf[slot] = nc_mat[:, HEAD_DIM:].astype(BF16)
        if with_query:
            for slot, u, w in zip(slots, us, ws):
                ubuf[slot] = u.astype(BF16)
                wbuf[slot] = w
            qks = [_dot_nt(conv_q[j], k) for (j, _), k in zip(jobs, ks)]
            for slot, qk, incl, dec in zip(slots, qks, incls, decs):
                pbuf[slot] = (qk * jnp.where(incl, dec, 0.0)).astype(BF16)
            for (j, d), cs, cols in zip(jobs, css, colss):
                qes[d, cs, :] = (conv_q[j].astype(F32) * bcast_col(cols, 4 + d)).astype(BF16)
        return carry

    lax.fori_loop(0, n_chunk // pre_chunks, prepass, 0)

    def finish(cs):
        ro = oret[cs, :]
        rc = ro - jnp.mean(ro, axis=-1, keepdims=True)
        ry = rc * lax.rsqrt(jnp.mean(rc * rc, axis=-1, keepdims=True) + EPS)
        yret_ref[cs, :] = (ry * gnw_ref[...] * _silu(rg_ref[cs, :].astype(F32))).astype(BF16)
        go = ogdn[cs, :]
        gy = go * lax.rsqrt(jnp.mean(go * go, axis=-1, keepdims=True) + EPS)
        ygdn_ref[cs, :] = (gy * rmsw_ref[...] * _silu(gz_ref[cs, :].astype(F32))).astype(BF16)

    def chunk_ids(n):
        nds = [n, n_chunk - 1 - n]
        css = [pl.ds(nd * CHUNK if isinstance(nd, int) else pl.multiple_of(nd * CHUNK, CHUNK), CHUNK) for nd in nds]
        slots = [d * n_chunk + nd for d, nd in enumerate(nds)]
        return nds, css, slots

    def late_outputs(n, vnb, finishing):
        _, css, slots = chunk_ids(n)
        for d in range(2):
            ogdn[css[d], :] += _dot(pbuf[slots[d]], vnb[d])
            if finishing:
                finish(css[d])

    def scan_step(n, vnb_prev, has_prev, finish_prev):
        nds, css, slots = chunk_ids(n)
        ret_st = [state[d] for d in range(2)]
        gdn_st = [state[2 + d] for d in range(2)]
        gdn_stb = [st.astype(BF16) for st in gdn_st]
        shrink = [_dot(nbuf[slot], stb) for slot, stb in zip(slots, gdn_stb)]
        for d in range(2):
            state[2 + d] = gl_ref[bi * n_chunk + nds[d], 8 * hi + d] * gdn_st[d] - shrink[d] + cbuf[slots[d]]
            state[d] = rtile[4 * d + 3] * ret_st[d] + kvbuf[slots[d]]
        if not with_query:
            return vnb_prev
        vnb = tuple((ubuf[slot].astype(F32) - _dot(wbuf[slot], stb)).astype(BF16)
                    for slot, stb in zip(slots, gdn_stb))
        for d in range(2):
            oret[css[d], :] += _dot(rq_ref[css[d], :], ret_st[d].astype(BF16)) * rtile[4 * d + 1]
            ogdn[css[d], :] += _dot(qes[d, css[d], :], gdn_stb[d])
        if has_prev:
            late_outputs(n - 1, vnb_prev, finish_prev)
        return vnb

    if with_query:
        half = n_chunk // 2
        zero = jnp.zeros((CHUNK, HEAD_DIM), BF16)
        vnb = scan_step(0, (zero, zero), has_prev=False, finish_prev=False)
        vnb = lax.fori_loop(1, half + 1, functools.partial(scan_step, has_prev=True, finish_prev=False), vnb)
        vnb = lax.fori_loop(half + 1, n_chunk, functools.partial(scan_step, has_prev=True, finish_prev=True), vnb)
        late_outputs(n_chunk - 1, vnb, True)
    else:
        lax.fori_loop(0, n_chunk, functools.partial(scan_step, has_prev=False, finish_prev=False), 0)
        sfin_ref[...] = state[...]


def _mixers(lg, gl, feats, dec, conv, s0, gnw, rmsw, with_query):
    cols, grow = dec
    b, l, _ = feats["rk"].shape
    n_slot = 2 * (l // CHUNK)
    assert (l // CHUNK) % min(8, l // CHUNK) == 0
    smem = pl.BlockSpec(memory_space=pltpu.SMEM)
    head = pl.BlockSpec((None, l, HEAD_DIM), lambda i, j: (i, 0, j))
    colspec = pl.BlockSpec((None, None, l, 8), lambda i, j: (i, j, 0, 0))
    rowspec = pl.BlockSpec((None, None, 8, l), lambda i, j: (i, j, 0, 0))
    tile_f32 = pltpu.VMEM((n_slot, CHUNK, CHUNK), F32)
    tile_bf16 = pltpu.VMEM((n_slot, CHUNK, CHUNK), BF16)
    st_spec = pl.BlockSpec((None, None, 4, HEAD_DIM, HEAD_DIM), lambda i, j: (i, j, 0, 0, 0))

    def conv_spec(which):
        return pl.BlockSpec((SHORT_CONV, HEAD_DIM), lambda i, j, w=which: (0, w * N_HEADS + j))

    def gain_spec():
        return pl.BlockSpec((1, HEAD_DIM), lambda i, j: (0, j))

    if with_query:
        args = [lg, gl, feats["rk"], feats["rv"], feats["gk"], feats["gv"], feats["rq"], feats["gq"], feats["rg"],
                feats["gz"], cols, grow, conv, conv, conv, s0, gnw, rmsw]
        in_specs = [smem, smem] + [head] * 8 + [colspec, rowspec, conv_spec(0), conv_spec(1), conv_spec(2),
                                                st_spec, gain_spec(), gain_spec()]
        out_shape = [jax.ShapeDtypeStruct((b, l, MIX_W), BF16)] * 2
        out_specs = [head, head]
        scratch = ([pltpu.VMEM((2, l, HEAD_DIM), BF16)] + [pltpu.VMEM((l, HEAD_DIM), F32)] * 2
                   + [tile_bf16, tile_bf16, tile_bf16, tile_bf16, tile_f32, tile_f32])
    else:
        args = [lg, gl, feats["rk"], feats["rv"], feats["gk"], feats["gv"], cols, grow, conv, conv, s0]
        in_specs = [smem, smem] + [head] * 4 + [colspec, rowspec, conv_spec(1), conv_spec(2), st_spec]
        out_shape = jax.ShapeDtypeStruct((b, N_HEADS, 4, HEAD_DIM, HEAD_DIM), F32)
        out_specs = st_spec
        scratch = [tile_bf16, tile_f32, tile_f32]
    scratch = scratch + [pltpu.VMEM((8, CHUNK, CHUNK), F32), pltpu.VMEM((4, HEAD_DIM, HEAD_DIM), F32)]
    return pl.pallas_call(
        functools.partial(_mixer_kernel, seq_len=l, with_query=with_query),
        grid=(b, N_HEADS),
        in_specs=in_specs,
        out_specs=out_specs,
        out_shape=out_shape,
        scratch_shapes=scratch,
        compiler_params=_cparams(("arbitrary", "arbitrary")),
        name="mixers_latent" if with_query else "mixers_context",
    )(*args)


def _split_bf16(v):
    hi = v.astype(BF16)
    return hi, (v - hi.astype(F32)).astype(BF16)


def _route(h2, wrt_ref, bias_ref, cand_ref, sel_ref, wd_ref):
    tm = h2.shape[0]
    h_hi, h_lo = _split_bf16(h2)
    w_hi, w_lo = _split_bf16(wrt_ref[...])
    logits = _dot_nt(w_hi, h_hi) + (_dot_nt(w_hi, h_lo) + _dot_nt(w_lo, h_hi))
    scores = _sigmoid(logits)
    biased = scores + bias_ref[...]
    neg_inf = float("-inf")
    sub = _iota((GROUP_SIZE, tm), 0).astype(F32)
    group_score = []
    for g in range(N_GROUPS):
        blk = biased[g * GROUP_SIZE:(g + 1) * GROUP_SIZE, :]
        m1 = jnp.max(blk, axis=0, keepdims=True)
        first = jnp.min(jnp.where(blk == m1, sub, float(GROUP_SIZE)), axis=0, keepdims=True)
        m2 = jnp.max(jnp.where(sub == first, neg_inf, blk), axis=0, keepdims=True)
        group_score.append(m1 + m2)
    for g in range(N_GROUPS):
        ahead = jnp.zeros((1, tm), I32)
        for g2 in range(N_GROUPS):
            if g2 == g:
                continue
            before = (group_score[g2] > group_score[g])
            if g2 < g:
                before = before | (group_score[g2] == group_score[g])
            ahead = ahead + before.astype(I32)
        keep = jnp.broadcast_to(ahead, (GROUP_SIZE, tm)) < TOPK_GROUPS
        cand_ref[g * GROUP_SIZE:(g + 1) * GROUP_SIZE, :] = jnp.where(
            keep, biased[g * GROUP_SIZE:(g + 1) * GROUP_SIZE, :], neg_inf)
    work = cand_ref[...]
    eidx = _iota((N_EXPERTS, tm), 0).astype(F32)
    sel = jnp.zeros((N_EXPERTS, tm), jnp.bool_)
    for _ in range(TOP_K):
        best = jnp.max(work, axis=0, keepdims=True)
        first = jnp.min(jnp.where(work == best, eidx, float(N_EXPERTS)), axis=0, keepdims=True)
        pick = eidx == first
        sel = sel | pick
        work = jnp.where(pick, neg_inf, work)
    picked = jnp.where(sel, scores, 0.0)
    wsum = jnp.sum(picked, axis=0, keepdims=True)
    sel_ref[...] = sel.astype(F32)
    wd_ref[...] = picked / wsum * ROUTED_SCALE


def _mixout_kernel(x_ref, yr_ref, yg_ref, g0_ref, g1_ref, g2_ref, g3_ref, wr_ref, wg_ref, wo_ref,
                   npost_ref, gate1_ref, nffn_ref, sc2_ref, sh2_ref, wrt_ref, bias_ref,
                   x1_ref, h2_ref, sel_ref, wd_ref, cand_ref):
    r = _dot(yr_ref[...], wr_ref[...])
    g = _dot(yg_ref[...], wg_ref[...])
    half = r.shape[1] // 2
    merged = jnp.concatenate(
        [g0_ref[...].astype(F32) * r[:, :half] + g2_ref[...].astype(F32) * g[:, :half],
         g1_ref[...].astype(F32) * r[:, half:] + g3_ref[...].astype(F32) * g[:, half:]], axis=1)
    mo = _dot(merged.astype(BF16), wo_ref[...])
    x1 = x_ref[...] + gate1_ref[...] * _rms(mo, npost_ref[...])
    x1_ref[...] = x1
    h2 = _rms(x1, nffn_ref[...]) * (1.0 + sc2_ref[...]) + sh2_ref[...]
    h2_ref[...] = h2.astype(BF16)
    _route(h2, wrt_ref, bias_ref, cand_ref, sel_ref, wd_ref)


def _mixer_out(x, yr, yg, gates, wr, wg, wo, npost, gate1, nffn, sc2, sh2, wrt, bias):
    b, l, d = x.shape
    tm = min(MIXOUT_TM, l)
    tiles = l // tm
    tok = lambda w: pl.BlockSpec((None, tm, w), lambda i, j: (i, j, 0))
    const = functools.partial(pl.BlockSpec, pipeline_mode=pl.Buffered(1))
    vec = lambda: const((1, d), lambda i, j: (0, 0))
    bvec = lambda: pl.BlockSpec((None, 1, d), lambda i, j: (i, 0, 0))
    emap = pl.BlockSpec((N_EXPERTS, tm), lambda i, j: (0, i * tiles + j))
    in_specs = ([tok(d), tok(MIX_W), tok(MIX_W)] + [tok(MIX_W)] * 4
                + [const((MIX_W, d), lambda i, j: (0, 0)), const((MIX_W, d), lambda i, j: (0, 0)),
                   const((d, d), lambda i, j: (0, 0)),
                   vec(), bvec(), vec(), bvec(), bvec(),
                   const((N_EXPERTS, d), lambda i, j: (0, 0)), const((N_EXPERTS, 1), lambda i, j: (0, 0))])
    return pl.pallas_call(
        _mixout_kernel,
        grid=(b, tiles),
        in_specs=in_specs,
        out_specs=[tok(d), tok(d), emap, emap],
        out_shape=[jax.ShapeDtypeStruct((b, l, d), F32), jax.ShapeDtypeStruct((b, l, d), BF16),
                   jax.ShapeDtypeStruct((N_EXPERTS, b * l), F32), jax.ShapeDtypeStruct((N_EXPERTS, b * l), F32)],
        scratch_shapes=[pltpu.VMEM((N_EXPERTS, tm), F32)],
        compiler_params=_cparams(("arbitrary", "arbitrary")),
        name="mixer_out_router",
    )(x, yr, yg, *gates, wr, wg, wo, npost, gate1, nffn, sc2, sh2, wrt, bias)


def _tile_positions(sel):
    tm = sel.shape[1]
    selb = sel.astype(BF16)
    earlier = (_iota((tm, tm), 0) < _iota((tm, tm), 1)).astype(BF16)
    rank = _dot(selb, earlier)
    cnt = _dot(selb, jnp.ones((tm, tm), BF16))
    nmb = jnp.floor((cnt + (MB_ROWS - 1)) * (1.0 / MB_ROWS))
    below = (_iota((N_EXPERTS, N_EXPERTS), 1) < _iota((N_EXPERTS, N_EXPERTS), 0)).astype(BF16)
    offmb = _dot(below, nmb.astype(BF16))
    rank_hi = jnp.floor(rank * (1.0 / MB_ROWS))
    hi = jnp.where(sel > 0.0, offmb + rank_hi, NO_SLOT)
    lo = rank - rank_hi * MB_ROWS
    return hi, lo, cnt, offmb, rank


def _dispatch_kernel(h2_ref, sel_ref, xs_ref, cnt_ref):
    last = pl.num_programs(0) - 1

    @pl.when(pl.program_id(0) < last)
    def _():
        _dispatch_tile(h2_ref, sel_ref, xs_ref, cnt_ref)

    @pl.when(pl.program_id(0) == last)
    def _():
        xs_ref[...] = jnp.zeros_like(xs_ref)


def _dispatch_tile(h2_ref, sel_ref, xs_ref, cnt_ref):
    tm = h2_ref.shape[0]
    sel = sel_ref[...]
    hi, lo, _, _, _ = _tile_positions(sel)
    hib = hi.astype(BF16)
    lob = lo.astype(BF16)
    selb = sel.astype(BF16)
    cnt_row = _dot_nt(jnp.ones((8, tm), BF16), selb)
    nmb_row = jnp.floor((cnt_row + (MB_ROWS - 1)) * (1.0 / MB_ROWS))
    before = (_iota((N_EXPERTS, N_EXPERTS), 0) < _iota((N_EXPERTS, N_EXPERTS), 1)).astype(BF16)
    start_row = _dot(nmb_row.astype(BF16), before) * MB_ROWS
    cnt_ref[...] = cnt_row
    start = start_row[0:1, :]
    end = start + cnt_row[0:1, :]
    x = h2_ref[...]
    rch = ROW_CHUNK

    def one_hot(rc):
        r_e = (rc * rch + _iota((rch, N_EXPERTS), 0)).astype(F32)
        owner = ((r_e >= start) & (r_e < end)).astype(BF16)
        r_t = rc * rch + _iota((rch, tm), 0)
        match = ((_dot(owner, hib) == (r_t >> _MB_SHIFT).astype(F32))
                 & (_dot(owner, lob) == (r_t & (MB_ROWS - 1)).astype(F32)))
        return match.astype(BF16)

    n_rc = TILE_ROWS // rch
    n_sure = n_rc - TAIL_CHUNKS
    nxt = one_hot(0)
    for rc in range(n_sure):
        cur = nxt
        if rc + 1 < n_sure:
            nxt = one_hot(rc + 1)
        xs_ref[rc * rch:(rc + 1) * rch, :] = _dot(cur, x).astype(BF16)
    tail_used = jnp.sum(nmb_row[0:1, :]) * MB_ROWS > n_sure * rch

    @pl.when(tail_used)
    def _():
        for rc in range(n_sure, n_rc):
            xs_ref[rc * rch:(rc + 1) * rch, :] = _dot(one_hot(rc), x).astype(BF16)

    @pl.when(jnp.logical_not(tail_used))
    def _():
        xs_ref[n_sure * rch:, :] = jnp.zeros(((n_rc - n_sure) * rch, x.shape[1]), BF16)


def _dispatch(h2, sel_t):
    t, d = h2.shape
    tm = MOE_TM
    nt = t // tm
    return pl.pallas_call(
        _dispatch_kernel,
        grid=(nt + 1,),
        in_specs=[pl.BlockSpec((tm, d), lambda i: (jnp.minimum(i, nt - 1), 0)),
                  pl.BlockSpec((N_EXPERTS, tm), lambda i: (0, jnp.minimum(i, nt - 1)))],
        out_specs=[pl.BlockSpec((TILE_ROWS, d), lambda i: (i, 0)),
                   pl.BlockSpec((None, 8, N_EXPERTS), lambda i: (jnp.minimum(i, nt - 1), 0, 0))],
        out_shape=[jax.ShapeDtypeStruct((nt * TILE_ROWS + (EXP_OUT + 1) * EXP_BM, d), BF16),
                   jax.ShapeDtypeStruct((nt, 8, N_EXPERTS), F32)],
        compiler_params=_cparams(("arbitrary",)),
        name="moe_dispatch",
    )(h2, sel_t)


def _expert_plan(cnt, n_blk):
    nt = cnt.shape[0]
    nmb = (cnt + (MB_ROWS - 1)) // MB_ROWS
    offmb = jnp.cumsum(nmb, axis=1) - nmb
    per_e = nmb.T
    incl = jnp.cumsum(per_e, axis=1)
    excl = incl - per_e
    tot = incl[:, -1]
    nb = (tot + (EXP_MB - 1)) // EXP_MB
    bend = jnp.cumsum(nb)
    bstart = bend - nb
    b = jnp.arange(n_blk, dtype=I32)
    valid = b < bend[-1]
    last = jnp.maximum(bend[-1] - 1, 0)
    bq = jnp.where(valid, b, last)
    blk_e = jnp.minimum(jnp.sum((bend[None, :] <= bq[:, None]).astype(I32), axis=1), N_EXPERTS - 1)
    onehot_e = (blk_e[:, None] == jnp.arange(N_EXPERTS, dtype=I32)[None, :]).astype(I32)
    pick = lambda table: jnp.dot(onehot_e.astype(F32), table.astype(F32), precision=HIGHEST).astype(I32)
    bstart_b = jnp.sum(onehot_e * bstart[None, :], axis=1)
    tot_b = jnp.sum(onehot_e * tot[None, :], axis=1)
    p0 = (bq - bstart_b) * EXP_MB
    blk_nmb = jnp.where(valid, jnp.clip(tot_b - p0, 0, EXP_MB), 0).astype(I32)
    p = p0[:, None] + jnp.arange(EXP_MB, dtype=I32)[None, :]
    incl_b = pick(incl)
    passed = (incl_b[:, None, :] <= p[:, :, None]).astype(I32)
    tile = jnp.minimum(jnp.sum(passed, axis=2), nt - 1)
    shift_b = pick(offmb.T - excl)
    step_b = jnp.concatenate([shift_b[:, 1:] - shift_b[:, :-1], jnp.zeros_like(shift_b[:, :1])], axis=1)
    shift = shift_b[:, :1] + jnp.sum(passed * step_b[:, None, :], axis=2)
    where = tile * _TILE_MB + p + shift
    j = jnp.arange(EXP_MB, dtype=I32)[None, :]
    used = j < blk_nmb[:, None]
    spare = nt * _TILE_MB + (b[:, None] % EXP_OUT) * EXP_MB + j
    zeros_mb = nt * _TILE_MB + EXP_OUT * EXP_MB + j
    src = jnp.where(used, where, jnp.where(valid[:, None], where[:, :1], zeros_mb))
    dst = jnp.where(used, where, spare)
    dst = jnp.concatenate([nt * _TILE_MB + (EXP_OUT - 1) * EXP_MB + j, dst], axis=0)
    return bstart.astype(I32), nb.astype(I32), src.reshape(-1).astype(I32), dst.reshape(-1).astype(I32)


def _expert_kernel(bstart_ref, nb_ref, src_ref, dst_ref, xs_hbm, wg_ref, wu_ref, wd_ref, ys_hbm,
                   xbuf, ybuf, wgub, wdb, in_sem, out_sem):
    e = pl.program_id(0)
    n_in = xbuf.shape[0]
    n_out = ybuf.shape[0]

    def gather_copy(blk, slot, j):
        rows = pl.ds(pl.multiple_of(src_ref[blk * EXP_MB + j] * MB_ROWS, MB_ROWS), MB_ROWS)
        return pltpu.make_async_copy(xs_hbm.at[rows, :], xbuf.at[slot, j * MB_ROWS:(j + 1) * MB_ROWS, :],
                                     in_sem.at[slot])

    def scatter_copy(blk, slot, j):
        rows = pl.ds(pl.multiple_of(dst_ref[(blk + 1) * EXP_MB + j] * MB_ROWS, MB_ROWS), MB_ROWS)
        return pltpu.make_async_copy(ybuf.at[slot, j * MB_ROWS:(j + 1) * MB_ROWS, :], ys_hbm.at[rows, :],
                                     out_sem.at[slot])

    def start_gather(blk):
        for j in range(EXP_MB):
            gather_copy(blk, lax.rem(blk, n_in), j).start()

    def wait_gather(blk):
        for j in range(EXP_MB):
            gather_copy(blk, lax.rem(blk, n_in), j).wait()

    def start_scatter(blk):
        for j in range(EXP_MB):
            scatter_copy(blk, lax.rem(blk + n_out, n_out), j).start()

    def wait_scatter(slot):
        for j in range(EXP_MB):
            scatter_copy(0, slot, j).wait()

    @pl.when(e == 0)
    def _():
        ybuf[...] = jnp.zeros_like(ybuf)
        for ahead in range(EXP_LEAD):
            start_gather(ahead)
        spare_row0 = ys_hbm.shape[0] - (n_out + 1) * EXP_BM
        for slot in range(n_out - 1):
            for j in range(EXP_MB):
                spare = pl.ds(spare_row0 + (slot * EXP_MB + j) * MB_ROWS, MB_ROWS)
                pltpu.make_async_copy(ybuf.at[slot, j * MB_ROWS:(j + 1) * MB_ROWS, :], ys_hbm.at[spare, :],
                                      out_sem.at[slot]).start()

    de = wg_ref.shape[1]
    wgub[:, :de] = wg_ref[...].astype(BF16)
    wgub[:, de:] = wu_ref[...].astype(BF16)
    wdb[...] = wd_ref[...].astype(BF16)

    def block(b, carry):
        slot = lax.rem(b, n_out)
        wait_gather(b)
        wait_scatter(slot)
        x = xbuf[lax.rem(b, n_in)]
        gu = _dot(x, wgub[...])
        act = (_silu(gu[:, :de]) * gu[:, de:]).astype(BF16)
        start_gather(b + EXP_LEAD)
        start_scatter(b - 1)
        ybuf[slot] = _dot(act, wdb[...]).astype(BF16)
        return carry

    first = bstart_ref[e]
    lax.fori_loop(first, first + nb_ref[e], block, 0)

    @pl.when(e == pl.num_programs(0) - 1)
    def _():
        total = first + nb_ref[e]
        start_scatter(total - 1)
        for ahead in range(EXP_LEAD):
            wait_gather(total + ahead)
        for slot in range(n_out):
            wait_scatter(slot)


def _expert_ffn(xs, bstart, nb, src, dst, wg, wu, wd):
    rows, d = xs.shape
    de = wg.shape[2]
    grid_spec = pltpu.PrefetchScalarGridSpec(
        num_scalar_prefetch=4,
        grid=(wg.shape[0],),
        in_specs=[pl.BlockSpec(memory_space=pl.ANY),
                  pl.BlockSpec((None, d, de), lambda i, bs, bn, sr, ds: (i, 0, 0)),
                  pl.BlockSpec((None, d, de), lambda i, bs, bn, sr, ds: (i, 0, 0)),
                  pl.BlockSpec((None, de, d), lambda i, bs, bn, sr, ds: (i, 0, 0))],
        out_specs=pl.BlockSpec(memory_space=pl.ANY),
        scratch_shapes=[pltpu.VMEM((EXP_LEAD + 1, EXP_BM, d), BF16), pltpu.VMEM((EXP_OUT, EXP_BM, d), BF16),
                        pltpu.VMEM((d, 2 * de), BF16), pltpu.VMEM((de, d), BF16),
                        pltpu.SemaphoreType.DMA((EXP_LEAD + 1,)), pltpu.SemaphoreType.DMA((EXP_OUT,))],
    )
    return pl.pallas_call(
        _expert_kernel,
        grid_spec=grid_spec,
        out_shape=jax.ShapeDtypeStruct((rows, d), BF16),
        input_output_aliases={4: 0},
        compiler_params=_cparams(("arbitrary",)),
        name="moe_experts",
    )(bstart, nb, src, dst, xs, wg, wu, wd)


def _combine_kernel(ys_ref, sel_ref, wd_ref, h2_ref, x1_ref, wsg_ref, wsu_ref, wsd_ref, npost_ref, gate2_ref,
                    o_ref, c_ref, acc_ref):
    tm = h2_ref.shape[0]
    sel = sel_ref[...]
    _, _, cnt, offmb, rank = _tile_positions(sel)
    no_rank = 1.5 * tm
    lhs = jnp.concatenate([jnp.where(sel > 0.0, rank, no_rank).T, wd_ref[...].T], axis=1).astype(BF16)
    start = offmb[:, :LANES] * MB_ROWS
    end = start + cnt[:, :LANES]
    zeros = jnp.zeros((N_EXPERTS, LANES), BF16)

    def build(cc):
        r_e = (cc * LANES + _iota((N_EXPERTS, LANES), 1)).astype(F32)
        owned = (r_e >= start) & (r_e < end)
        owner = owned.astype(BF16)
        local = r_e[0:1, :] - jnp.sum(jnp.where(owned, start, 0.0), axis=0, keepdims=True)
        rhs = jnp.concatenate([jnp.concatenate([owner, zeros], axis=1),
                               jnp.concatenate([zeros, owner], axis=1)], axis=0)
        got = _dot(lhs, rhs)
        c_ref[:, cc * LANES:(cc + 1) * LANES] = jnp.where(got[:, :LANES] == local, got[:, LANES:], 0.0).astype(BF16)

    n_cc = TILE_ROWS // LANES
    sure = TILE_ROWS - TAIL_CHUNKS * ROW_CHUNK
    for cc in range(sure // LANES):
        build(cc)
    h2 = h2_ref[...]
    shared = _dot((_silu(_dot(h2, wsg_ref[...])) * _dot(h2, wsu_ref[...])).astype(BF16), wsd_ref[...])
    acc_ref[...] = shared + _dot(c_ref[:, :sure], ys_ref[:sure, :])
    nmb = jnp.floor((cnt[:, 0:1] + (MB_ROWS - 1)) * (1.0 / MB_ROWS))

    @pl.when(jnp.sum(nmb) * MB_ROWS > sure)
    def _():
        for cc in range(sure // LANES, n_cc):
            build(cc)
        acc_ref[...] += _dot(c_ref[:, sure:], ys_ref[sure:, :])

    o_ref[...] = x1_ref[...] + gate2_ref[...] * _rms(acc_ref[...], npost_ref[...])


def _combine(ys, sel_t, wd_t, h2, x1, wsg, wsu, wsd, npost, gate2, seq_len):
    t, d = h2.shape
    tm = MOE_TM
    nt = t // tm
    per_seq = seq_len // tm
    ds = wsg.shape[1]
    const = functools.partial(pl.BlockSpec, pipeline_mode=pl.Buffered(1))
    emap = pl.BlockSpec((N_EXPERTS, tm), lambda i: (0, i))
    return pl.pallas_call(
        _combine_kernel,
        grid=(nt,),
        in_specs=[pl.BlockSpec((TILE_ROWS, d), lambda i: (i, 0)), emap, emap,
                  pl.BlockSpec((tm, d), lambda i: (i, 0)), pl.BlockSpec((tm, d), lambda i: (i, 0)),
                  const((d, ds), lambda i: (0, 0)), const((d, ds), lambda i: (0, 0)), const((ds, d), lambda i: (0, 0)),
                  const((1, d), lambda i: (0, 0)),
                  pl.BlockSpec((None, 1, d), lambda i: (i // per_seq, 0, 0))],
        out_specs=pl.BlockSpec((tm, d), lambda i: (i, 0)),
        out_shape=jax.ShapeDtypeStruct((t, d), F32),
        scratch_shapes=[pltpu.VMEM((tm, TILE_ROWS), BF16), pltpu.VMEM((tm, d), F32)],
        compiler_params=_cparams(("arbitrary",)),
        name="moe_combine",
    )(ys, sel_t, wd_t, h2, x1, wsg, wsu, wsd, npost, gate2)


def _rope_tables(n):
    rows = n // GRID_W
    pos_r = jnp.repeat(jnp.arange(rows, dtype=F32), GRID_W)
    pos_c = jnp.tile(jnp.arange(GRID_W, dtype=F32), rows)
    n_freq = HEAD_DIM // 4
    inv = ROPE_BASE ** (-jnp.arange(n_freq, dtype=F32) / n_freq)
    ang = jnp.concatenate([pos_r[:, None] * inv, pos_c[:, None] * inv], axis=-1)
    cos, sin = jnp.cos(ang), jnp.sin(ang)
    return jnp.concatenate([cos, cos], axis=-1), jnp.concatenate([-sin, sin], axis=-1)


def kernel(x, c, ctx, c_ctx, w_mod, b_mod, norm_mix_pre, norm_mix_post, norm_ffn_pre, norm_ffn_post, w_in, gdn_conv, ret_log_decay, gdn_a_log, gdn_dt_bias, ret_gn_w, gdn_norm_w, w_ret_out, w_gdn_out, w_o, w_router, router_bias, w_gate, w_up, w_down, w_sh_gate, w_sh_up, w_sh_down):
    b, n, d = x.shape
    depth = w_mod.shape[0]
    assert depth == 1, "single-layer block"
    assert all(n % min(tile, n) == 0 for tile in (PROJ_TM, MIXOUT_TM, MOE_TM)) and n % MOE_TM == 0
    assert ctx.shape[1] % CHUNK == 0 and (EXP_OUT + 1) * EXP_BM <= TILE_ROWS
    assert _TILE_MB < NO_SLOT and MOE_TM <= 256, "slot indices and ranks must stay exact in bf16"

    rows = -(-(b + 1) // 8) * 8
    cvec = jnp.zeros((rows, d), F32).at[:b].set(c).at[b].set(c_ctx)
    mod = _modulation(cvec, w_mod[0], b_mod[0][None, :])
    sh1, sc1, g1, sh2, sc2, g2 = [mod[:b, k * d:(k + 1) * d][:, None, :] for k in range(6)]
    ctx_shift = jnp.broadcast_to(mod[b, 0:d][None, None, :], (b, 1, d))
    ctx_scale = jnp.broadcast_to(mod[b, d:2 * d][None, None, :], (b, 1, d))

    w_in0 = w_in[0]
    n_main = 4 * MIX_W
    w_state = w_in0[:, :n_main].astype(BF16)
    slots = ((0, 0), (0, 1), (1, 0), (1, 1), (0, 0), (0, 1), (0, 0), (0, 1))
    gab_cols = n_main + jnp.array([[ab * 2 * N_HEADS + dr * N_HEADS + hh for ab, dr in slots]
                                   for hh in range(N_HEADS)], I32)
    w_gab = w_in0[:, gab_cols.reshape(-1)].astype(BF16)
    n_state = n_main + N_GAB
    w_query = w_in0[:, n_state:].astype(BF16)
    a_coef = -jnp.exp(gdn_a_log[0].astype(F32))
    dtb = gdn_dt_bias[0].astype(F32)
    is_alpha = jnp.array([ab == 0 for ab, _ in slots])
    dirs = jnp.array([dr for _, dr in slots], I32)
    prow = jnp.stack([jnp.where(is_alpha[None, :], a_coef.T[:, dirs], 0.0).reshape(-1),
                      jnp.where(is_alpha[None, :], dtb.T[:, dirs], 0.0).reshape(-1)], axis=0)
    first2 = jnp.tile(jnp.arange(8) < 2, N_HEADS)
    w_gab_t = w_gab.T * first2[:, None].astype(BF16)
    pcol = prow.T * first2[:, None]
    gain_mix = norm_mix_pre[0][None, :]
    cos2, sin2 = _rope_tables(n)

    state_kinds = (_ROPE_SCALED, _PLAIN, _PLAIN, _PLAIN)
    query_kinds = (_ROPE, _PLAIN, _PLAIN, _PLAIN, _SIGMOID, _SIGMOID, _SIGMOID, _SIGMOID)
    lg = ret_log_decay[0].astype(F32)
    conv = gdn_conv[0].astype(F32)

    lc = ctx.shape[1]
    cfe, (ccols, cgrow, cgl) = _in_projection(
        ctx, gain_mix, ctx_scale, ctx_shift, (w_state,), w_gab, w_gab_t, prow, pcol,
        cos2[:lc], sin2[:lc], state_kinds, rope=False)
    cfeats = dict(zip(("rk", "rv", "gk", "gv"), cfe))
    zero_state = jnp.zeros((b, N_HEADS, 4, HEAD_DIM, HEAD_DIM), F32)
    init = _mixers(lg, cgl, cfeats, (ccols, cgrow), conv, zero_state, None, None, with_query=False)

    fe, (cols, grow, gl) = _in_projection(
        x, gain_mix, sc1, sh1, (w_state, w_query), w_gab, w_gab_t, prow, pcol, cos2, sin2,
        state_kinds + query_kinds, rope=True)
    feats = dict(zip(("rk", "rv", "gk", "gv", "rq", "rg", "gq", "gz"), fe[:8]))
    gates = fe[8:]
    y_ret, y_gdn = _mixers(lg, gl, feats, (cols, grow), conv, init,
                           ret_gn_w[0][None, :], gdn_norm_w[0][None, :], with_query=True)
    x1, h2, sel_t, wd_t = _mixer_out(
        x, y_ret, y_gdn, gates, w_ret_out[0].astype(BF16), w_gdn_out[0].astype(BF16), w_o[0].astype(BF16),
        norm_mix_post[0][None, :], g1, norm_ffn_pre[0][None, :], sc2, sh2,
        w_router[0].T.astype(F32), router_bias[0].astype(F32)[:, None])

    t = b * n
    h2f = h2.reshape(t, d)
    xs, cnt = _dispatch(h2f, sel_t)
    nt = t // MOE_TM
    n_blk = nt * _TILE_MB // EXP_MB + N_EXPERTS + EXP_LEAD
    bstart, nb, src, dst = _expert_plan(cnt[:, 0, :].astype(I32), n_blk)
    ys = _expert_ffn(xs, bstart, nb, src, dst, w_gate[0], w_up[0], w_down[0])
    out = _combine(ys, sel_t, wd_t, h2f, x1.reshape(t, d), w_sh_gate[0].astype(BF16), w_sh_up[0].astype(BF16),
                   w_sh_down[0].astype(BF16), norm_ffn_post[0][None, :], g2, n)
    return out.reshape(b, n, d)
```

```python
import functools
import math

import jax
import jax.numpy as jnp
from jax import lax
from jax.experimental import pallas as pl
from jax.experimental.pallas import tpu as pltpu

F32 = jnp.float32
BF16 = jnp.bfloat16
I32 = jnp.int32
HIGHEST = lax.Precision.HIGHEST

N_HEADS = 4
HEAD_DIM = 128
MIX_W = N_HEADS * HEAD_DIM
CHUNK = 128
SHORT_CONV = 3
ROPE_BASE = 10000.0
GRID_W = 64
N_EXPERTS = 64
TOP_K = 8
N_GROUPS = 8
TOPK_GROUPS = 4
GROUP_SIZE = N_EXPERTS // N_GROUPS
ROUTED_SCALE = 2.5
EPS = 1e-6
N_GAB = 4 * N_HEADS

LANES = 128
BF16_TILE_ROWS = 16
VMEM_LIMIT_BYTES = 56 * 1024 * 1024

PROJ_TM = 1024
MIXOUT_TM = 1024
MOE_TM = 256
MB_ROWS = BF16_TILE_ROWS
_MB_SHIFT = MB_ROWS.bit_length() - 1
_TILE_MB = -(-(TOP_K * MOE_TM // MB_ROWS + N_EXPERTS * (MB_ROWS - 1) // MB_ROWS + 1) // 8) * 8
TILE_ROWS = _TILE_MB * MB_ROWS
ROW_CHUNK = 256
TAIL_CHUNKS = 2
NO_SLOT = 255.0
EXP_BM = 512
EXP_MB = EXP_BM // MB_ROWS
EXP_LEAD = 3
EXP_OUT = 3


def _cparams(sem):
    return pltpu.CompilerParams(dimension_semantics=sem, vmem_limit_bytes=VMEM_LIMIT_BYTES)


def _sigmoid(v):
    return 0.5 * jnp.tanh(0.5 * v) + 0.5


def _silu(v):
    return v * _sigmoid(v)


def _softplus(v):
    return jnp.maximum(v, 0.0) + jnp.log1p(jnp.exp(-jnp.abs(v)))


def _iota(shape, dim):
    return lax.broadcasted_iota(I32, shape, dim)


def _dot(a, b, **kw):
    return jnp.dot(a, b, preferred_element_type=F32, **kw)


def _dot_nt(a, b, **kw):
    return lax.dot_general(a, b, (((1,), (1,)), ((), ())), preferred_element_type=F32, **kw)


def _dot_tn(a, b, **kw):
    return lax.dot_general(a, b, (((0,), (0,)), ((), ())), preferred_element_type=F32, **kw)


def _rms(v, gain):
    return v * lax.rsqrt(jnp.mean(v * v, axis=-1, keepdims=True) + EPS) * gain


def _mod_kernel(c_ref, w_ref, b_ref, o_ref):
    o_ref[...] = _dot(_silu(c_ref[...]), w_ref[...], precision=HIGHEST) + b_ref[...]


def _modulation(cvec, w_mod, b_mod):
    rows, d = cvec.shape
    n = w_mod.shape[1]
    tn = 1024
    return pl.pallas_call(
        _mod_kernel,
        grid=(n // tn,),
        in_specs=[pl.BlockSpec((rows, d), lambda j: (0, 0)),
                  pl.BlockSpec((d, tn), lambda j: (0, j)),
                  pl.BlockSpec((1, tn), lambda j: (0, j))],
        out_specs=pl.BlockSpec((rows, tn), lambda j: (0, j)),
        out_shape=jax.ShapeDtypeStruct((rows, n), F32),
        compiler_params=_cparams(("arbitrary",)),
        name="adaln_modulation",
    )(cvec, w_mod, b_mod)


_PLAIN, _ROPE, _ROPE_SCALED, _SIGMOID = 0, 1, 2, 3


def _proj_kernel(x_ref, gain_ref, sc_ref, sh_ref, *refs, kinds, rope, n_w):
    w_refs = refs[:n_w]
    wg_ref, wgt_ref, prow_ref, pcol_ref, cos_ref, sin_ref = refs[n_w:n_w + 6]
    out_refs = refs[n_w + 6:]
    groups = [(w, c) for w in w_refs for c in range(0, w.shape[1], MIX_W)]
    n_feat = len(kinds)
    feat_refs = out_refs[:n_feat]
    cols_ref, grow_ref, gl_ref = out_refs[n_feat:]
    tm = x_ref.shape[0]

    x = x_ref[...]
    h = (_rms(x, gain_ref[...]) * (1.0 + sc_ref[...]) + sh_ref[...]).astype(BF16)

    r_i = _iota((CHUNK, CHUNK), 0)
    c_i = _iota((CHUNK, CHUNK), 1)
    lower_incl = (c_i <= r_i).astype(BF16)
    upper_incl = (c_i >= r_i).astype(BF16)

    def prefix_rows(v):
        hi, lo = _split_bf16(v)
        return _dot(lower_incl, hi) + _dot(lower_incl, lo)

    def prefix_lanes(v):
        hi, lo = _split_bf16(v)
        return _dot(hi, upper_incl) + _dot(lo, upper_incl)

    nc = N_HEADS * 8
    pg = _dot(h, wg_ref[...])
    colt = _iota((tm, nc), 1) & 7
    la = jnp.where((colt == 2) | (colt == 3), 0.0, prow_ref[0:1, :] * _softplus(pg + prow_ref[1:2, :]))
    beta = _sigmoid(pg)
    colc = _iota((CHUNK, nc), 1) & 7
    fwd_col = (colc & 1) == 0
    for c in range(tm // CHUNK):
        sl = slice(c * CHUNK, (c + 1) * CHUNK)
        la_c = la[sl]
        pre = prefix_rows(la_c)
        suf = pre[CHUNK - 1:CHUNK, :] - pre + la_c
        g_c = jnp.where(fwd_col, pre, suf)
        rest = jnp.where(fwd_col, suf, pre) - la_c
        vals = jnp.where(colc < 2, g_c, jnp.where(colc < 4, beta[sl], jnp.where(
            colc < 6, jnp.exp(g_c), jnp.exp(rest))))
        for hh in range(N_HEADS):
            cols_ref[hh, sl, :] = vals[:, 8 * hh:8 * hh + 8]
        gl_ref[c:c + 1, :] = jnp.exp(g_c[0:1, :] + rest[0:1, :])

    pgt = _dot_nt(wgt_ref[...], h)
    rowq = _iota((N_HEADS * 8, tm), 0) & 7
    lat = jnp.where(rowq < 2, pcol_ref[:, 0:1] * _softplus(pgt + pcol_ref[:, 1:2]), 0.0)
    rowc = _iota((N_HEADS * 8, CHUNK), 0) & 7
    for c in range(tm // CHUNK):
        sl = slice(c * CHUNK, (c + 1) * CHUNK)
        lat_c = lat[:, sl]
        pre_t = prefix_lanes(lat_c)
        suf_t = pre_t[:, CHUNK - 1:CHUNK] - pre_t + lat_c
        grow_ref[:, :, sl] = jnp.where(rowc == 0, pre_t, suf_t).reshape(N_HEADS, 8, CHUNK)

    if rope:
        cos2 = cos_ref[...]
        sin2 = sin_ref[...]

    for g, kind in enumerate(kinds):
        w_ref, c0 = groups[g]
        p = _dot(h, w_ref[:, c0:c0 + MIX_W])
        if kind == _SIGMOID:
            p = _sigmoid(p)
        elif kind in (_ROPE, _ROPE_SCALED) and rope:
            heads = []
            for hh in range(N_HEADS):
                t = p[:, hh * HEAD_DIM:(hh + 1) * HEAD_DIM]
                heads.append(t * cos2 + pltpu.roll(t, HEAD_DIM // 2, 1) * sin2)
            p = jnp.concatenate(heads, axis=1)
        if kind == _ROPE_SCALED:
            p = p * (HEAD_DIM ** -0.5)
        feat_refs[g][...] = p.astype(feat_refs[g].dtype)


def _in_projection(x, gain, scale, shift, w_mains, w_gab, w_gab_t, prow, pcol, cos2, sin2, kinds, rope):
    b, l, d = x.shape
    tm = min(PROJ_TM, l)
    tiles = l // tm
    n_chunk = tm // CHUNK
    feat_shapes = [jax.ShapeDtypeStruct((b, l, MIX_W), BF16) for _ in kinds]
    feat_specs = [pl.BlockSpec((None, tm, MIX_W), lambda i, j: (i, j, 0)) for _ in kinds]
    out_shape = feat_shapes + [jax.ShapeDtypeStruct((b, N_HEADS, l, 8), F32),
                               jax.ShapeDtypeStruct((b, N_HEADS, 8, l), F32),
                               jax.ShapeDtypeStruct((b, tiles, n_chunk, N_HEADS * 8), F32)]
    out_specs = feat_specs + [pl.BlockSpec((None, N_HEADS, tm, 8), lambda i, j: (i, 0, j, 0)),
                              pl.BlockSpec((None, N_HEADS, 8, tm), lambda i, j: (i, 0, 0, j)),
                              pl.BlockSpec((None, None, n_chunk, N_HEADS * 8), lambda i, j: (i, j, 0, 0))]
    const = functools.partial(pl.BlockSpec, pipeline_mode=pl.Buffered(1))
    in_specs = [
        pl.BlockSpec((None, tm, d), lambda i, j: (i, j, 0)),
        const((1, d), lambda i, j: (0, 0)),
        pl.BlockSpec((None, 1, d), lambda i, j: (i, 0, 0)),
        pl.BlockSpec((None, 1, d), lambda i, j: (i, 0, 0)),
    ] + [const(w.shape, lambda i, j: (0, 0)) for w in w_mains] + [
        const((d, N_HEADS * 8), lambda i, j: (0, 0)),
        const((N_HEADS * 8, d), lambda i, j: (0, 0)),
        const((2, N_HEADS * 8), lambda i, j: (0, 0)),
        const((N_HEADS * 8, 2), lambda i, j: (0, 0)),
        pl.BlockSpec((tm, HEAD_DIM), lambda i, j: (j, 0)),
        pl.BlockSpec((tm, HEAD_DIM), lambda i, j: (j, 0)),
    ]
    outs = pl.pallas_call(
        functools.partial(_proj_kernel, kinds=tuple(kinds), rope=rope, n_w=len(w_mains)),
        grid=(b, tiles),
        in_specs=in_specs,
        out_specs=out_specs,
        out_shape=out_shape,
        compiler_params=_cparams(("arbitrary", "arbitrary")),
        name="in_projection_rope" if rope else "in_projection_ctx",
    )(x, gain, scale, shift, *w_mains, w_gab, w_gab_t, prow, pcol, cos2, sin2)
    feats = outs[:len(kinds)]
    cols, grow, gl = outs[len(kinds):]
    return feats, (cols, grow, gl.reshape(b * (l // CHUNK), N_HEADS * 8))


def _unit_triangular_inverses(mats, lowers):
    r = _iota((CHUNK, CHUNK), 0)
    c = _iota((CHUNK, CHUNK), 1)
    eye = (r == c).astype(F32)
    invs = [eye - jnp.where((r >> 1) == (c >> 1), a, 0.0) for a in mats]
    for level in range(1, int(math.log2(CHUNK))):
        s = 1 << level
        mask = ((r >> (level + 1)) == (c >> (level + 1))) & ((r >> level) != (c >> level))
        invb = [inv.astype(BF16) for inv in invs]
        offs = [jnp.where(mask, a, 0.0).astype(BF16) for a in mats]
        if s < 8:
            half = [_dot(off, ib).astype(BF16) for off, ib in zip(offs, invb)]
            invs = [inv - _dot(ib, hf) for inv, ib, hf in zip(invs, invb, half)]
            continue
        def rows_of(x, lower, moving):
            first = s if (lower == moving) else 0
            return [x[g * 2 * s + first:g * 2 * s + first + s] for g in range(CHUNK // (2 * s))]

        half = [_dot(jnp.concatenate(rows_of(off, lo, True), axis=0), ib).astype(BF16)
                for off, ib, lo in zip(offs, invb, lowers)]
        zero = jnp.zeros((s, CHUNK), BF16)
        full = []
        for hf, lo in zip(half, lowers):
            pieces = []
            for g in range(CHUNK // (2 * s)):
                piece = hf[g * s:(g + 1) * s]
                pieces += [zero, piece] if lo else [piece, zero]
            full.append(jnp.concatenate(pieces, axis=0))
        corr = [_dot(jnp.concatenate(rows_of(ib, lo, True), axis=0), hf) for ib, hf, lo in zip(invb, full, lowers)]
        new = []
        for inv, cr, lo in zip(invs, corr, lowers):
            keep = rows_of(inv, lo, False)
            moved = [m - cr[g * s:(g + 1) * s] for g, m in enumerate(rows_of(inv, lo, True))]
            pieces = []
            for k, m in zip(keep, moved):
                pieces += [k, m] if lo else [m, k]
            new.append(jnp.concatenate(pieces, axis=0))
        invs = new
    return invs


def _mixer_kernel(*refs, seq_len, with_query):
    n_chunk = seq_len // CHUNK
    if with_query:
        (lg_ref, gl_ref, rk_ref, rv_ref, gk_ref, gv_ref, rq_ref, gq_ref, rg_ref, gz_ref,
         cols_ref, grow_ref, cq_ref, ck_ref, cv_ref, s0_ref, gnw_ref, rmsw_ref,
         yret_ref, ygdn_ref,
         qes, oret, ogdn, ubuf, wbuf, pbuf, nbuf, cbuf, kvbuf, rtile, state) = refs
    else:
        (lg_ref, gl_ref, rk_ref, rv_ref, gk_ref, gv_ref,
         cols_ref, grow_ref, ck_ref, cv_ref, s0_ref,
         sfin_ref,
         nbuf, cbuf, kvbuf, rtile, state) = refs
    bi = pl.program_id(0)
    hi = pl.program_id(1)

    row = _iota((CHUNK, CHUNK), 0)
    colm = _iota((CHUNK, CHUNK), 1)
    rowf = row.astype(F32)
    colf = colm.astype(F32)

    def conv_chunk(src_ref, w_ref, n):
        s = pl.multiple_of(n * CHUNK, CHUNK)
        x = src_ref[pl.ds(s, CHUNK), :].astype(F32)
        ps = pl.multiple_of(jnp.maximum(s - BF16_TILE_ROWS, 0), BF16_TILE_ROWS)
        ns = pl.multiple_of(jnp.minimum(s + CHUNK, seq_len - BF16_TILE_ROWS), BF16_TILE_ROWS)
        prev_row = src_ref[pl.ds(ps, BF16_TILE_ROWS), :].astype(F32)[BF16_TILE_ROWS - 1:BF16_TILE_ROWS, :]
        next_row = src_ref[pl.ds(ns, BF16_TILE_ROWS), :].astype(F32)[0:1, :]
        prev_row = prev_row * jnp.where(n > 0, 1.0, 0.0)
        next_row = next_row * jnp.where(n < n_chunk - 1, 1.0, 0.0)
        xp = jnp.where(row == 0, jnp.broadcast_to(prev_row, (CHUNK, HEAD_DIM)), pltpu.roll(x, 1, 0))
        xn = jnp.where(row == CHUNK - 1, jnp.broadcast_to(next_row, (CHUNK, HEAD_DIM)), pltpu.roll(x, CHUNK - 1, 0))
        return _silu(w_ref[0:1, :] * xp + w_ref[1:2, :] * x + w_ref[2:3, :] * xn)

    def l2n(v):
        return v * lax.rsqrt(jnp.sum(v * v, axis=-1, keepdims=True) + EPS)

    for d in range(2):
        lg = lg_ref[d, hi]
        if d == 0:
            dist, pos_q, pos_k = rowf - colf, rowf + 1.0, (CHUNK - 1.0) - rowf
        else:
            dist, pos_q, pos_k = colf - rowf, CHUNK - rowf, rowf
        rtile[4 * d + 0] = jnp.where(dist >= 0, jnp.exp(lg * jnp.maximum(dist, 0.0)), 0.0)
        rtile[4 * d + 1] = jnp.exp(lg * pos_q)
        rtile[4 * d + 2] = jnp.exp(lg * pos_k)
        rtile[4 * d + 3] = jnp.exp(lg * jnp.full((CHUNK, CHUNK), float(CHUNK), F32))

    state[...] = s0_ref[...]

    def bcast_col(cols, j):
        return jnp.broadcast_to(cols[:, j:j + 1], (CHUNK, CHUNK))

    pre_chunks = min(8, n_chunk)

    def prepass(m, carry):
        chunks = [m * pre_chunks + j for j in range(pre_chunks)]
        chunk_cs = [pl.ds(pl.multiple_of(n * CHUNK, CHUNK), CHUNK) for n in chunks]
        conv_k, conv_v, conv_q = [], [], []
        for n, cs in zip(chunks, chunk_cs):
            conv_k.append(l2n(conv_chunk(gk_ref, ck_ref, n)).astype(BF16))
            conv_v.append(conv_chunk(gv_ref, cv_ref, n).astype(BF16))
            if with_query:
                conv_q.append((l2n(conv_chunk(gq_ref, cq_ref, n)) * (HEAD_DIM ** -0.5)).astype(BF16))
                ogdn[cs, :] = jnp.zeros((CHUNK, HEAD_DIM), F32)

        jobs = [(j, d) for j in range(pre_chunks) for d in range(2)]
        css = [chunk_cs[j] for j, _ in jobs]
        slots = [d * n_chunk + chunks[j] for j, d in jobs]
        dirs = [d for _, d in jobs]
        ks = [rk_ref[cs, :] for cs in css]
        vs = [rv_ref[cs, :] for cs in css]
        if with_query:
            scs = [(_dot_nt(rq_ref[cs, :], k) * rtile[4 * d + 0]).astype(BF16) for cs, k, d in zip(css, ks, dirs)]
            outs = [_dot(sc, v) for sc, v in zip(scs, vs)]
            for j, cs in enumerate(chunk_cs):
                oret[cs, :] = outs[2 * j] + outs[2 * j + 1]
        kvs = [_dot_tn((k.astype(F32) * rtile[4 * d + 2]).astype(BF16), v) for k, v, d in zip(ks, vs, dirs)]
        for slot, kv in zip(slots, kvs):
            kvbuf[slot] = kv
        ks = [conv_k[j] for j, _ in jobs]
        vs = [conv_v[j] for j, _ in jobs]
        colss = [cols_ref[cs, :] for cs in css]
        betas = [bcast_col(cols, 2 + d) for cols, d in zip(colss, dirs)]
        incls = [(row >= colm) if d == 0 else (row <= colm) for d in dirs]
        stricts = [(row > colm) if d == 0 else (row < colm) for d in dirs]
        decs = [jnp.exp(jnp.where(incl, bcast_col(cols, d) - jnp.broadcast_to(grow_ref[d:d + 1, cs], (CHUNK, CHUNK)), 0.0))
                for cols, d, cs, incl in zip(colss, dirs, css, incls)]
        kks = [_dot_nt(k, k) for k in ks]
        mats = [kk * beta * jnp.where(strict, dec, 0.0) for kk, beta, strict, dec in zip(kks, betas, stricts, decs)]
        tinvs = [t.astype(BF16) for t in _unit_triangular_inverses(mats, [d == 0 for d in dirs])]
        rhs = [jnp.concatenate([(beta * v.astype(F32)).astype(BF16),
                                (beta * bcast_col(cols, 4 + d) * k.astype(F32)).astype(BF16)], axis=1)
               for beta, v, cols, d, k in zip(betas, vs, colss, dirs, ks)]
        uws = [_dot(t, r) for t, r in zip(tinvs, rhs)]
        us = [uw[:, :HEAD_DIM] for uw in uws]
        ws = [uw[:, HEAD_DIM:].astype(BF16) for uw in uws]
        kts = [(k.astype(F32) * bcast_col(cols, 6 + d)).astype(BF16) for k, cols, d in zip(ks, colss, dirs)]
        ncs = [_dot_tn(kt, uw.astype(BF16)) for kt, uw in zip(kts, uws)]
        for slot, nc_mat in zip(slots, ncs):
            cbuf[slot] = nc_mat[:, :HEAD_DIM]
            nbuf[slot] = nc_mat[:, HEAD_DIM:].astype(BF16)
        if with_query:
            for slot, u, w in zip(slots, us, ws):
                ubuf[slot] = u.astype(BF16)
                wbuf[slot] = w
            qks = [_dot_nt(conv_q[j], k) for (j, _), k in zip(jobs, ks)]
            for slot, qk, incl, dec in zip(slots, qks, incls, decs):
                pbuf[slot] = (qk * jnp.where(incl, dec, 0.0)).astype(BF16)
            for (j, d), cs, cols in zip(jobs, css, colss):
                qes[d, cs, :] = (conv_q[j].astype(F32) * bcast_col(cols, 4 + d)).astype(BF16)
        return carry

    lax.fori_loop(0, n_chunk // pre_chunks, prepass, 0)

    def finish(cs):
        ro = oret[cs, :]
        rc = ro - jnp.mean(ro, axis=-1, keepdims=True)
        ry = rc * lax.rsqrt(jnp.mean(rc * rc, axis=-1, keepdims=True) + EPS)
        yret_ref[cs, :] = (ry * gnw_ref[...] * _silu(rg_ref[cs, :].astype(F32))).astype(BF16)
        go = ogdn[cs, :]
        gy = go * lax.rsqrt(jnp.mean(go * go, axis=-1, keepdims=True) + EPS)
        ygdn_ref[cs, :] = (gy * rmsw_ref[...] * _silu(gz_ref[cs, :].astype(F32))).astype(BF16)

    def chunk_ids(n):
        nds = [n, n_chunk - 1 - n]
        css = [pl.ds(nd * CHUNK if isinstance(nd, int) else pl.multiple_of(nd * CHUNK, CHUNK), CHUNK) for nd in nds]
        slots = [d * n_chunk + nd for d, nd in enumerate(nds)]
        return nds, css, slots

    def late_outputs(n, vnb, finishing):
        _, css, slots = chunk_ids(n)
        for d in range(2):
            ogdn[css[d], :] += _dot(pbuf[slots[d]], vnb[d])
            if finishing:
                finish(css[d])

    def scan_step(n, vnb_prev, has_prev, finish_prev):
        nds, css, slots = chunk_ids(n)
        ret_st = [state[d] for d in range(2)]
        gdn_st = [state[2 + d] for d in range(2)]
        gdn_stb = [st.astype(BF16) for st in gdn_st]
        shrink = [_dot(nbuf[slot], stb) for slot, stb in zip(slots, gdn_stb)]
        for d in range(2):
            state[2 + d] = gl_ref[bi * n_chunk + nds[d], 8 * hi + d] * gdn_st[d] - shrink[d] + cbuf[slots[d]]
            state[d] = rtile[4 * d + 3] * ret_st[d] + kvbuf[slots[d]]
        if not with_query:
            return vnb_prev
        vnb = tuple((ubuf[slot].astype(F32) - _dot(wbuf[slot], stb)).astype(BF16)
                    for slot, stb in zip(slots, gdn_stb))
        for d in range(2):
            oret[css[d], :] += _dot(rq_ref[css[d], :], ret_st[d].astype(BF16)) * rtile[4 * d + 1]
            ogdn[css[d], :] += _dot(qes[d, css[d], :], gdn_stb[d])
        if has_prev:
            late_outputs(n - 1, vnb_prev, finish_prev)
        return vnb

    if with_query:
        half = n_chunk // 2
        zero = jnp.zeros((CHUNK, HEAD_DIM), BF16)
        vnb = scan_step(0, (zero, zero), has_prev=False, finish_prev=False)
        vnb = lax.fori_loop(1, half + 1, functools.partial(scan_step, has_prev=True, finish_prev=False), vnb)
        vnb = lax.fori_loop(half + 1, n_chunk, functools.partial(scan_step, has_prev=True, finish_prev=True), vnb)
        late_outputs(n_chunk - 1, vnb, True)
    else:
        lax.fori_loop(0, n_chunk, functools.partial(scan_step, has_prev=False, finish_prev=False), 0)
        sfin_ref[...] = state[...]


def _mixers(lg, gl, feats, dec, conv, s0, gnw, rmsw, with_query):
    cols, grow = dec
    b, l, _ = feats["rk"].shape
    n_slot = 2 * (l // CHUNK)
    assert (l // CHUNK) % min(8, l // CHUNK) == 0
    smem = pl.BlockSpec(memory_space=pltpu.SMEM)
    head = pl.BlockSpec((None, l, HEAD_DIM), lambda i, j: (i, 0, j))
    colspec = pl.BlockSpec((None, None, l, 8), lambda i, j: (i, j, 0, 0))
    rowspec = pl.BlockSpec((None, None, 8, l), lambda i, j: (i, j, 0, 0))
    tile_f32 = pltpu.VMEM((n_slot, CHUNK, CHUNK), F32)
    tile_bf16 = pltpu.VMEM((n_slot, CHUNK, CHUNK), BF16)
    st_spec = pl.BlockSpec((None, None, 4, HEAD_DIM, HEAD_DIM), lambda i, j: (i, j, 0, 0, 0))

    def conv_spec(which):
        return pl.BlockSpec((SHORT_CONV, HEAD_DIM), lambda i, j, w=which: (0, w * N_HEADS + j))

    def gain_spec():
        return pl.BlockSpec((1, HEAD_DIM), lambda i, j: (0, j))

    if with_query:
        args = [lg, gl, feats["rk"], feats["rv"], feats["gk"], feats["gv"], feats["rq"], feats["gq"], feats["rg"],
                feats["gz"], cols, grow, conv, conv, conv, s0, gnw, rmsw]
        in_specs = [smem, smem] + [head] * 8 + [colspec, rowspec, conv_spec(0), conv_spec(1), conv_spec(2),
                                                st_spec, gain_spec(), gain_spec()]
        out_shape = [jax.ShapeDtypeStruct((b, l, MIX_W), BF16)] * 2
        out_specs = [head, head]
        scratch = ([pltpu.VMEM((2, l, HEAD_DIM), BF16)] + [pltpu.VMEM((l, HEAD_DIM), F32)] * 2
                   + [tile_bf16, tile_bf16, tile_bf16, tile_bf16, tile_f32, tile_f32])
    else:
        args = [lg, gl, feats["rk"], feats["rv"], feats["gk"], feats["gv"], cols, grow, conv, conv, s0]
        in_specs = [smem, smem] + [head] * 4 + [colspec, rowspec, conv_spec(1), conv_spec(2), st_spec]
        out_shape = jax.ShapeDtypeStruct((b, N_HEADS, 4, HEAD_DIM, HEAD_DIM), F32)
        out_specs = st_spec
        scratch = [tile_bf16, tile_f32, tile_f32]
    scratch = scratch + [pltpu.VMEM((8, CHUNK, CHUNK), F32), pltpu.VMEM((4, HEAD_DIM, HEAD_DIM), F32)]
    return pl.pallas_call(
        functools.partial(_mixer_kernel, seq_len=l, with_query=with_query),
        grid=(b, N_HEADS),
        in_specs=in_specs,
        out_specs=out_specs,
        out_shape=out_shape,
        scratch_shapes=scratch,
        compiler_params=_cparams(("arbitrary", "arbitrary")),
        name="mixers_latent" if with_query else "mixers_context",
    )(*args)


def _split_bf16(v):
    hi = v.astype(BF16)
    return hi, (v - hi.astype(F32)).astype(BF16)


def _route(h2, wrt_ref, bias_ref, cand_ref, sel_ref, wd_ref):
    tm = h2.shape[0]
    h_hi, h_lo = _split_bf16(h2)
    w_hi, w_lo = _split_bf16(wrt_ref[...])
    logits = _dot_nt(w_hi, h_hi) + (_dot_nt(w_hi, h_lo) + _dot_nt(w_lo, h_hi))
    scores = _sigmoid(logits)
    biased = scores + bias_ref[...]
    neg_inf = float("-inf")
    sub = _iota((GROUP_SIZE, tm), 0).astype(F32)
    group_score = []
    for g in range(N_GROUPS):
        blk = biased[g * GROUP_SIZE:(g + 1) * GROUP_SIZE, :]
        m1 = jnp.max(blk, axis=0, keepdims=True)
        first = jnp.min(jnp.where(blk == m1, sub, float(GROUP_SIZE)), axis=0, keepdims=True)
        m2 = jnp.max(jnp.where(sub == first, neg_inf, blk), axis=0, keepdims=True)
        group_score.append(m1 + m2)
    for g in range(N_GROUPS):
        ahead = jnp.zeros((1, tm), I32)
        for g2 in range(N_GROUPS):
            if g2 == g:
                continue
            before = (group_score[g2] > group_score[g])
            if g2 < g:
                before = before | (group_score[g2] == group_score[g])
            ahead = ahead + before.astype(I32)
        keep = jnp.broadcast_to(ahead, (GROUP_SIZE, tm)) < TOPK_GROUPS
        cand_ref[g * GROUP_SIZE:(g + 1) * GROUP_SIZE, :] = jnp.where(
            keep, biased[g * GROUP_SIZE:(g + 1) * GROUP_SIZE, :], neg_inf)
    work = cand_ref[...]
    eidx = _iota((N_EXPERTS, tm), 0).astype(F32)
    sel = jnp.zeros((N_EXPERTS, tm), jnp.bool_)
    for _ in range(TOP_K):
        best = jnp.max(work, axis=0, keepdims=True)
        first = jnp.min(jnp.where(work == best, eidx, float(N_EXPERTS)), axis=0, keepdims=True)
        pick = eidx == first
        sel = sel | pick
        work = jnp.where(pick, neg_inf, work)
    picked = jnp.where(sel, scores, 0.0)
    wsum = jnp.sum(picked, axis=0, keepdims=True)
    sel_ref[...] = sel.astype(F32)
    wd_ref[...] = picked / wsum * ROUTED_SCALE


def _mixout_kernel(x_ref, yr_ref, yg_ref, g0_ref, g1_ref, g2_ref, g3_ref, wr_ref, wg_ref, wo_ref,
                   npost_ref, gate1_ref, nffn_ref, sc2_ref, sh2_ref, wrt_ref, bias_ref,
                   x1_ref, h2_ref, sel_ref, wd_ref, cand_ref):
    r = _dot(yr_ref[...], wr_ref[...])
    g = _dot(yg_ref[...], wg_ref[...])
    half = r.shape[1] // 2
    merged = jnp.concatenate(
        [g0_ref[...].astype(F32) * r[:, :half] + g2_ref[...].astype(F32) * g[:, :half],
         g1_ref[...].astype(F32) * r[:, half:] + g3_ref[...].astype(F32) * g[:, half:]], axis=1)
    mo = _dot(merged.astype(BF16), wo_ref[...])
    x1 = x_ref[...] + gate1_ref[...] * _rms(mo, npost_ref[...])
    x1_ref[...] = x1
    h2 = _rms(x1, nffn_ref[...]) * (1.0 + sc2_ref[...]) + sh2_ref[...]
    h2_ref[...] = h2.astype(BF16)
    _route(h2, wrt_ref, bias_ref, cand_ref, sel_ref, wd_ref)


def _mixer_out(x, yr, yg, gates, wr, wg, wo, npost, gate1, nffn, sc2, sh2, wrt, bias):
    b, l, d = x.shape
    tm = min(MIXOUT_TM, l)
    tiles = l // tm
    tok = lambda w: pl.BlockSpec((None, tm, w), lambda i, j: (i, j, 0))
    const = functools.partial(pl.BlockSpec, pipeline_mode=pl.Buffered(1))
    vec = lambda: const((1, d), lambda i, j: (0, 0))
    bvec = lambda: pl.BlockSpec((None, 1, d), lambda i, j: (i, 0, 0))
    emap = pl.BlockSpec((N_EXPERTS, tm), lambda i, j: (0, i * tiles + j))
    in_specs = ([tok(d), tok(MIX_W), tok(MIX_W)] + [tok(MIX_W)] * 4
                + [const((MIX_W, d), lambda i, j: (0, 0)), const((MIX_W, d), lambda i, j: (0, 0)),
                   const((d, d), lambda i, j: (0, 0)),
                   vec(), bvec(), vec(), bvec(), bvec(),
                   const((N_EXPERTS, d), lambda i, j: (0, 0)), const((N_EXPERTS, 1), lambda i, j: (0, 0))])
    return pl.pallas_call(
        _mixout_kernel,
        grid=(b, tiles),
        in_specs=in_specs,
        out_specs=[tok(d), tok(d), emap, emap],
        out_shape=[jax.ShapeDtypeStruct((b, l, d), F32), jax.ShapeDtypeStruct((b, l, d), BF16),
                   jax.ShapeDtypeStruct((N_EXPERTS, b * l), F32), jax.ShapeDtypeStruct((N_EXPERTS, b * l), F32)],
        scratch_shapes=[pltpu.VMEM((N_EXPERTS, tm), F32)],
        compiler_params=_cparams(("arbitrary", "arbitrary")),
        name="mixer_out_router",
    )(x, yr, yg, *gates, wr, wg, wo, npost, gate1, nffn, sc2, sh2, wrt, bias)


def _tile_positions(sel):
    tm = sel.shape[1]
    selb = sel.astype(BF16)
    earlier = (_iota((tm, tm), 0) < _iota((tm, tm), 1)).astype(BF16)
    rank = _dot(selb, earlier)
    cnt = _dot(selb, jnp.ones((tm, tm), BF16))
    nmb = jnp.floor((cnt + (MB_ROWS - 1)) * (1.0 / MB_ROWS))
    below = (_iota((N_EXPERTS, N_EXPERTS), 1) < _iota((N_EXPERTS, N_EXPERTS), 0)).astype(BF16)
    offmb = _dot(below, nmb.astype(BF16))
    rank_hi = jnp.floor(rank * (1.0 / MB_ROWS))
    hi = jnp.where(sel > 0.0, offmb + rank_hi, NO_SLOT)
    lo = rank - rank_hi * MB_ROWS
    return hi, lo


def _micro_block_keys(sel, per_pair=()):
    tm = sel.shape[1]
    hi, lo = _tile_positions(sel)
    cnt_row = _dot_nt(jnp.ones((8, tm), BF16), sel.astype(BF16))
    nmb_row = jnp.floor((cnt_row + (MB_ROWS - 1)) * (1.0 / MB_ROWS))
    before = (_iota((N_EXPERTS, N_EXPERTS), 0) < _iota((N_EXPERTS, N_EXPERTS), 1)).astype(BF16)
    start = _dot(nmb_row.astype(BF16), before)[0:1, :]
    end = start + nmb_row[0:1, :]
    m_e = _iota((_TILE_MB, N_EXPERTS), 0).astype(F32)
    owner = ((m_e >= start) & (m_e < end)).astype(BF16)
    cols = [hi.astype(BF16), lo.astype(BF16)] + [v.astype(BF16) for v in per_pair]
    got = _dot(owner, jnp.concatenate(cols, axis=1))
    m_t = _iota((_TILE_MB, tm), 0).astype(F32)
    used = jnp.sum(nmb_row[0:1, :])
    key = jnp.where((got[:, :tm] == m_t) & (m_t < used), got[:, tm:2 * tm], -1.0)
    return key, cnt_row, used, [got[:, (2 + k) * tm:(3 + k) * tm] for k in range(len(per_pair))]


def _dispatch_kernel(h2_ref, sel_ref, xs_ref, cnt_ref, key_ref, hot_ref):
    last = pl.num_programs(0) - 1

    @pl.when(pl.program_id(0) < last)
    def _():
        _dispatch_tile(h2_ref, sel_ref, xs_ref, cnt_ref, key_ref, hot_ref)

    @pl.when(pl.program_id(0) == last)
    def _():
        xs_ref[...] = jnp.zeros_like(xs_ref)


def _dispatch_tile(h2_ref, sel_ref, xs_ref, cnt_ref, key_ref, hot_ref):
    tm = h2_ref.shape[0]
    key, cnt_row, used, _ = _micro_block_keys(sel_ref[...])
    key_ref[...] = key
    cnt_ref[...] = cnt_row
    x = h2_ref[...]
    rch = ROW_CHUNK
    in_block = _iota((MB_ROWS, tm), 0).astype(F32)

    def one_hot(rc):
        for m in range(rc * rch // MB_ROWS, (rc + 1) * rch // MB_ROWS):
            hit = jnp.broadcast_to(key_ref[m:m + 1, :], (MB_ROWS, tm)) == in_block
            hot_ref[m * MB_ROWS:(m + 1) * MB_ROWS, :] = jnp.where(hit, 1.0, 0.0).astype(BF16)

    n_rc = TILE_ROWS // rch
    n_sure = n_rc - TAIL_CHUNKS
    one_hot(0)
    for rc in range(n_sure):
        if rc + 1 < n_sure:
            one_hot(rc + 1)
        xs_ref[rc * rch:(rc + 1) * rch, :] = _dot(hot_ref[rc * rch:(rc + 1) * rch, :], x).astype(BF16)
    tail_used = used * MB_ROWS > n_sure * rch

    @pl.when(tail_used)
    def _():
        for rc in range(n_sure, n_rc):
            one_hot(rc)
            xs_ref[rc * rch:(rc + 1) * rch, :] = _dot(hot_ref[rc * rch:(rc + 1) * rch, :], x).astype(BF16)

    @pl.when(jnp.logical_not(tail_used))
    def _():
        xs_ref[n_sure * rch:, :] = jnp.zeros(((n_rc - n_sure) * rch, x.shape[1]), BF16)


def _dispatch(h2, sel_t):
    t, d = h2.shape
    tm = MOE_TM
    nt = t // tm
    return pl.pallas_call(
        _dispatch_kernel,
        grid=(nt + 1,),
        in_specs=[pl.BlockSpec((tm, d), lambda i: (jnp.minimum(i, nt - 1), 0)),
                  pl.BlockSpec((N_EXPERTS, tm), lambda i: (0, jnp.minimum(i, nt - 1)))],
        out_specs=[pl.BlockSpec((TILE_ROWS, d), lambda i: (i, 0)),
                   pl.BlockSpec((None, 8, N_EXPERTS), lambda i: (jnp.minimum(i, nt - 1), 0, 0))],
        out_shape=[jax.ShapeDtypeStruct((nt * TILE_ROWS + (EXP_OUT + 1) * EXP_BM, d), BF16),
                   jax.ShapeDtypeStruct((nt, 8, N_EXPERTS), F32)],
        scratch_shapes=[pltpu.VMEM((_TILE_MB, tm), F32), pltpu.VMEM((TILE_ROWS, tm), BF16)],
        compiler_params=_cparams(("arbitrary",)),
        name="moe_dispatch",
    )(h2, sel_t)


def _expert_plan(cnt, n_blk):
    nt = cnt.shape[0]
    nmb = (cnt + (MB_ROWS - 1)) // MB_ROWS
    offmb = jnp.cumsum(nmb, axis=1) - nmb
    per_e = nmb.T
    incl = jnp.cumsum(per_e, axis=1)
    excl = incl - per_e
    tot = incl[:, -1]
    nb = (tot + (EXP_MB - 1)) // EXP_MB
    bend = jnp.cumsum(nb)
    bstart = bend - nb
    b = jnp.arange(n_blk, dtype=I32)
    valid = b < bend[-1]
    last = jnp.maximum(bend[-1] - 1, 0)
    bq = jnp.where(valid, b, last)
    blk_e = jnp.minimum(jnp.sum((bend[None, :] <= bq[:, None]).astype(I32), axis=1), N_EXPERTS - 1)
    onehot_e = (blk_e[:, None] == jnp.arange(N_EXPERTS, dtype=I32)[None, :]).astype(I32)
    pick = lambda table: jnp.dot(onehot_e.astype(F32), table.astype(F32), precision=HIGHEST).astype(I32)
    bstart_b = jnp.sum(onehot_e * bstart[None, :], axis=1)
    tot_b = jnp.sum(onehot_e * tot[None, :], axis=1)
    p0 = (bq - bstart_b) * EXP_MB
    blk_nmb = jnp.where(valid, jnp.clip(tot_b - p0, 0, EXP_MB), 0).astype(I32)
    p = p0[:, None] + jnp.arange(EXP_MB, dtype=I32)[None, :]
    incl_b = pick(incl)
    passed = (incl_b[:, None, :] <= p[:, :, None]).astype(I32)
    tile = jnp.minimum(jnp.sum(passed, axis=2), nt - 1)
    shift_b = pick(offmb.T - excl)
    step_b = jnp.concatenate([shift_b[:, 1:] - shift_b[:, :-1], jnp.zeros_like(shift_b[:, :1])], axis=1)
    shift = shift_b[:, :1] + jnp.sum(passed * step_b[:, None, :], axis=2)
    where = tile * _TILE_MB + p + shift
    j = jnp.arange(EXP_MB, dtype=I32)[None, :]
    used = j < blk_nmb[:, None]
    spare = nt * _TILE_MB + (b[:, None] % EXP_OUT) * EXP_MB + j
    zeros_mb = nt * _TILE_MB + EXP_OUT * EXP_MB + j
    src = jnp.where(used, where, jnp.where(valid[:, None], where[:, :1], zeros_mb))
    dst = jnp.where(used, where, spare)
    dst = jnp.concatenate([nt * _TILE_MB + (EXP_OUT - 1) * EXP_MB + j, dst], axis=0)
    return bstart.astype(I32), nb.astype(I32), src.reshape(-1).astype(I32), dst.reshape(-1).astype(I32)


def _expert_kernel(bstart_ref, nb_ref, src_ref, dst_ref, xs_hbm, wg_ref, wu_ref, wd_ref, ys_hbm,
                   xbuf, ybuf, wgub, wdb, in_sem, out_sem):
    e = pl.program_id(0)
    n_in = xbuf.shape[0]
    n_out = ybuf.shape[0]

    def gather_copy(blk, slot, j):
        rows = pl.ds(pl.multiple_of(src_ref[blk * EXP_MB + j] * MB_ROWS, MB_ROWS), MB_ROWS)
        return pltpu.make_async_copy(xs_hbm.at[rows, :], xbuf.at[slot, j * MB_ROWS:(j + 1) * MB_ROWS, :],
                                     in_sem.at[slot])

    def scatter_copy(blk, slot, j):
        rows = pl.ds(pl.multiple_of(dst_ref[(blk + 1) * EXP_MB + j] * MB_ROWS, MB_ROWS), MB_ROWS)
        return pltpu.make_async_copy(ybuf.at[slot, j * MB_ROWS:(j + 1) * MB_ROWS, :], ys_hbm.at[rows, :],
                                     out_sem.at[slot])

    def start_gather(blk):
        for j in range(EXP_MB):
            gather_copy(blk, lax.rem(blk, n_in), j).start()

    def wait_gather(blk):
        for j in range(EXP_MB):
            gather_copy(blk, lax.rem(blk, n_in), j).wait()

    def start_scatter(blk):
        for j in range(EXP_MB):
            scatter_copy(blk, lax.rem(blk + n_out, n_out), j).start()

    def wait_scatter(slot):
        for j in range(EXP_MB):
            scatter_copy(0, slot, j).wait()

    @pl.when(e == 0)
    def _():
        ybuf[...] = jnp.zeros_like(ybuf)
        for ahead in range(EXP_LEAD):
            start_gather(ahead)
        spare_row0 = ys_hbm.shape[0] - (n_out + 1) * EXP_BM
        for slot in range(n_out - 1):
            for j in range(EXP_MB):
                spare = pl.ds(spare_row0 + (slot * EXP_MB + j) * MB_ROWS, MB_ROWS)
                pltpu.make_async_copy(ybuf.at[slot, j * MB_ROWS:(j + 1) * MB_ROWS, :], ys_hbm.at[spare, :],
                                      out_sem.at[slot]).start()

    de = wg_ref.shape[1]
    wgub[:, :de] = wg_ref[...].astype(BF16)
    wgub[:, de:] = wu_ref[...].astype(BF16)
    wdb[...] = wd_ref[...].astype(BF16)

    def block(b, carry):
        slot = lax.rem(b, n_out)
        wait_gather(b)
        wait_scatter(slot)
        x = xbuf[lax.rem(b, n_in)]
        gu = _dot(x, wgub[...])
        act = (_silu(gu[:, :de]) * gu[:, de:]).astype(BF16)
        start_gather(b + EXP_LEAD)
        start_scatter(b - 1)
        ybuf[slot] = _dot(act, wdb[...]).astype(BF16)
        return carry

    first = bstart_ref[e]
    lax.fori_loop(first, first + nb_ref[e], block, 0)

    @pl.when(e == pl.num_programs(0) - 1)
    def _():
        total = first + nb_ref[e]
        start_scatter(total - 1)
        for ahead in range(EXP_LEAD):
            wait_gather(total + ahead)
        for slot in range(n_out):
            wait_scatter(slot)


def _expert_ffn(xs, bstart, nb, src, dst, wg, wu, wd):
    rows, d = xs.shape
    de = wg.shape[2]
    grid_spec = pltpu.PrefetchScalarGridSpec(
        num_scalar_prefetch=4,
        grid=(wg.shape[0],),
        in_specs=[pl.BlockSpec(memory_space=pl.ANY),
                  pl.BlockSpec((None, d, de), lambda i, bs, bn, sr, ds: (i, 0, 0)),
                  pl.BlockSpec((None, d, de), lambda i, bs, bn, sr, ds: (i, 0, 0)),
                  pl.BlockSpec((None, de, d), lambda i, bs, bn, sr, ds: (i, 0, 0))],
        out_specs=pl.BlockSpec(memory_space=pl.ANY),
        scratch_shapes=[pltpu.VMEM((EXP_LEAD + 1, EXP_BM, d), BF16), pltpu.VMEM((EXP_OUT, EXP_BM, d), BF16),
                        pltpu.VMEM((d, 2 * de), BF16), pltpu.VMEM((de, d), BF16),
                        pltpu.SemaphoreType.DMA((EXP_LEAD + 1,)), pltpu.SemaphoreType.DMA((EXP_OUT,))],
    )
    return pl.pallas_call(
        _expert_kernel,
        grid_spec=grid_spec,
        out_shape=jax.ShapeDtypeStruct((rows, d), BF16),
        input_output_aliases={4: 0},
        compiler_params=_cparams(("arbitrary",)),
        name="moe_experts",
    )(bstart, nb, src, dst, xs, wg, wu, wd)


def _combine_kernel(ys_ref, sel_ref, wd_ref, h2_ref, x1_ref, wsg_ref, wsu_ref, wsd_ref, npost_ref, gate2_ref,
                    o_ref, key_ref, wmb_ref, c_ref, acc_ref):
    tm = h2_ref.shape[0]
    key, _, used, (wmb,) = _micro_block_keys(sel_ref[...], (wd_ref[...],))
    key_ref[...] = key
    wmb_ref[...] = wmb
    in_block = _iota((MB_ROWS, tm), 0).astype(F32)

    def build(lo_row, hi_row):
        for m in range(lo_row // MB_ROWS, hi_row // MB_ROWS):
            hit = jnp.broadcast_to(key_ref[m:m + 1, :], (MB_ROWS, tm)) == in_block
            weight = jnp.broadcast_to(wmb_ref[m:m + 1, :], (MB_ROWS, tm))
            c_ref[m * MB_ROWS:(m + 1) * MB_ROWS, :] = jnp.where(hit, weight, 0.0).astype(BF16)

    sure = TILE_ROWS - TAIL_CHUNKS * ROW_CHUNK
    build(0, sure)
    h2 = h2_ref[...]
    shared = _dot((_silu(_dot(h2, wsg_ref[...])) * _dot(h2, wsu_ref[...])).astype(BF16), wsd_ref[...])
    acc_ref[...] = shared + _dot_tn(c_ref[:sure, :], ys_ref[:sure, :])

    @pl.when(used * MB_ROWS > sure)
    def _():
        build(sure, TILE_ROWS)
        acc_ref[...] += _dot_tn(c_ref[sure:, :], ys_ref[sure:, :])

    o_ref[...] = x1_ref[...] + gate2_ref[...] * _rms(acc_ref[...], npost_ref[...])


def _combine(ys, sel_t, wd_t, h2, x1, wsg, wsu, wsd, npost, gate2, seq_len):
    t, d = h2.shape
    tm = MOE_TM
    nt = t // tm
    per_seq = seq_len // tm
    ds = wsg.shape[1]
    const = functools.partial(pl.BlockSpec, pipeline_mode=pl.Buffered(1))
    emap = pl.BlockSpec((N_EXPERTS, tm), lambda i: (0, i))
    return pl.pallas_call(
        _combine_kernel,
        grid=(nt,),
        in_specs=[pl.BlockSpec((TILE_ROWS, d), lambda i: (i, 0)), emap, emap,
                  pl.BlockSpec((tm, d), lambda i: (i, 0)), pl.BlockSpec((tm, d), lambda i: (i, 0)),
                  const((d, ds), lambda i: (0, 0)), const((d, ds), lambda i: (0, 0)), const((ds, d), lambda i: (0, 0)),
                  const((1, d), lambda i: (0, 0)),
                  pl.BlockSpec((None, 1, d), lambda i: (i // per_seq, 0, 0))],
        out_specs=pl.BlockSpec((tm, d), lambda i: (i, 0)),
        out_shape=jax.ShapeDtypeStruct((t, d), F32),
        scratch_shapes=[pltpu.VMEM((_TILE_MB, tm), F32), pltpu.VMEM((_TILE_MB, tm), F32),
                        pltpu.VMEM((TILE_ROWS, tm), BF16), pltpu.VMEM((tm, d), F32)],
        compiler_params=_cparams(("arbitrary",)),
        name="moe_combine",
    )(ys, sel_t, wd_t, h2, x1, wsg, wsu, wsd, npost, gate2)


def _rope_tables(n):
    rows = n // GRID_W
    pos_r = jnp.repeat(jnp.arange(rows, dtype=F32), GRID_W)
    pos_c = jnp.tile(jnp.arange(GRID_W, dtype=F32), rows)
    n_freq = HEAD_DIM // 4
    inv = ROPE_BASE ** (-jnp.arange(n_freq, dtype=F32) / n_freq)
    ang = jnp.concatenate([pos_r[:, None] * inv, pos_c[:, None] * inv], axis=-1)
    cos, sin = jnp.cos(ang), jnp.sin(ang)
    return jnp.concatenate([cos, cos], axis=-1), jnp.concatenate([-sin, sin], axis=-1)


def kernel(x, c, ctx, c_ctx, w_mod, b_mod, norm_mix_pre, norm_mix_post, norm_ffn_pre, norm_ffn_post, w_in, gdn_conv, ret_log_decay, gdn_a_log, gdn_dt_bias, ret_gn_w, gdn_norm_w, w_ret_out, w_gdn_out, w_o, w_router, router_bias, w_gate, w_up, w_down, w_sh_gate, w_sh_up, w_sh_down):
    b, n, d = x.shape
    depth = w_mod.shape[0]
    assert depth == 1, "single-layer block"
    assert all(n % min(tile, n) == 0 for tile in (PROJ_TM, MIXOUT_TM, MOE_TM)) and n % MOE_TM == 0
    assert ctx.shape[1] % CHUNK == 0 and (EXP_OUT + 1) * EXP_BM <= TILE_ROWS
    assert _TILE_MB < NO_SLOT and MOE_TM <= 256, "slot indices and ranks must stay exact in bf16"

    rows = -(-(b + 1) // 8) * 8
    cvec = jnp.zeros((rows, d), F32).at[:b].set(c).at[b].set(c_ctx)
    mod = _modulation(cvec, w_mod[0], b_mod[0][None, :])
    sh1, sc1, g1, sh2, sc2, g2 = [mod[:b, k * d:(k + 1) * d][:, None, :] for k in range(6)]
    ctx_shift = jnp.broadcast_to(mod[b, 0:d][None, None, :], (b, 1, d))
    ctx_scale = jnp.broadcast_to(mod[b, d:2 * d][None, None, :], (b, 1, d))

    w_in0 = w_in[0]
    n_main = 4 * MIX_W
    w_state = w_in0[:, :n_main].astype(BF16)
    slots = ((0, 0), (0, 1), (1, 0), (1, 1), (0, 0), (0, 1), (0, 0), (0, 1))
    gab_cols = n_main + jnp.array([[ab * 2 * N_HEADS + dr * N_HEADS + hh for ab, dr in slots]
                                   for hh in range(N_HEADS)], I32)
    w_gab = w_in0[:, gab_cols.reshape(-1)].astype(BF16)
    n_state = n_main + N_GAB
    w_query = w_in0[:, n_state:].astype(BF16)
    a_coef = -jnp.exp(gdn_a_log[0].astype(F32))
    dtb = gdn_dt_bias[0].astype(F32)
    is_alpha = jnp.array([ab == 0 for ab, _ in slots])
    dirs = jnp.array([dr for _, dr in slots], I32)
    prow = jnp.stack([jnp.where(is_alpha[None, :], a_coef.T[:, dirs], 0.0).reshape(-1),
                      jnp.where(is_alpha[None, :], dtb.T[:, dirs], 0.0).reshape(-1)], axis=0)
    first2 = jnp.tile(jnp.arange(8) < 2, N_HEADS)
    w_gab_t = w_gab.T * first2[:, None].astype(BF16)
    pcol = prow.T * first2[:, None]
    gain_mix = norm_mix_pre[0][None, :]
    cos2, sin2 = _rope_tables(n)

    state_kinds = (_ROPE_SCALED, _PLAIN, _PLAIN, _PLAIN)
    query_kinds = (_ROPE, _PLAIN, _PLAIN, _PLAIN, _SIGMOID, _SIGMOID, _SIGMOID, _SIGMOID)
    lg = ret_log_decay[0].astype(F32)
    conv = gdn_conv[0].astype(F32)

    lc = ctx.shape[1]
    cfe, (ccols, cgrow, cgl) = _in_projection(
        ctx, gain_mix, ctx_scale, ctx_shift, (w_state,), w_gab, w_gab_t, prow, pcol,
        cos2[:lc], sin2[:lc], state_kinds, rope=False)
    cfeats = dict(zip(("rk", "rv", "gk", "gv"), cfe))
    zero_state = jnp.zeros((b, N_HEADS, 4, HEAD_DIM, HEAD_DIM), F32)
    init = _mixers(lg, cgl, cfeats, (ccols, cgrow), conv, zero_state, None, None, with_query=False)

    fe, (cols, grow, gl) = _in_projection(
        x, gain_mix, sc1, sh1, (w_state, w_query), w_gab, w_gab_t, prow, pcol, cos2, sin2,
        state_kinds + query_kinds, rope=True)
    feats = dict(zip(("rk", "rv", "gk", "gv", "rq", "rg", "gq", "gz"), fe[:8]))
    gates = fe[8:]
    y_ret, y_gdn = _mixers(lg, gl, feats, (cols, grow), conv, init,
                           ret_gn_w[0][None, :], gdn_norm_w[0][None, :], with_query=True)
    x1, h2, sel_t, wd_t = _mixer_out(
        x, y_ret, y_gdn, gates, w_ret_out[0].astype(BF16), w_gdn_out[0].astype(BF16), w_o[0].astype(BF16),
        norm_mix_post[0][None, :], g1, norm_ffn_pre[0][None, :], sc2, sh2,
        w_router[0].T.astype(F32), router_bias[0].astype(F32)[:, None])

    t = b * n
    h2f = h2.reshape(t, d)
    xs, cnt = _dispatch(h2f, sel_t)
    nt = t // MOE_TM
    n_blk = nt * _TILE_MB // EXP_MB + N_EXPERTS + EXP_LEAD
    bstart, nb, src, dst = _expert_plan(cnt[:, 0, :].astype(I32), n_blk)
    ys = _expert_ffn(xs, bstart, nb, src, dst, w_gate[0], w_up[0], w_down[0])
    out = _combine(ys, sel_t, wd_t, h2f, x1.reshape(t, d), w_sh_gate[0].astype(BF16), w_sh_up[0].astype(BF16),
                   w_sh_down[0].astype(BF16), norm_ffn_post[0][None, :], g2, n)
    return out.reshape(b, n, d)
```

```python
import functools
import math

import jax
import jax.numpy as jnp
from jax import lax
from jax.experimental import pallas as pl
from jax.experimental.pallas import tpu as pltpu

F32 = jnp.float32
BF16 = jnp.bfloat16
I32 = jnp.int32
HIGHEST = lax.Precision.HIGHEST

N_HEADS = 4
HEAD_DIM = 128
MIX_W = N_HEADS * HEAD_DIM
CHUNK = 128
SHORT_CONV = 3
ROPE_BASE = 10000.0
GRID_W = 64
N_EXPERTS = 64
TOP_K = 8
N_GROUPS = 8
TOPK_GROUPS = 4
GROUP_SIZE = N_EXPERTS // N_GROUPS
ROUTED_SCALE = 2.5
EPS = 1e-6
N_GAB = 4 * N_HEADS

LANES = 128
F32_TILE_ROWS = 8
BF16_TILE_ROWS = 16
VMEM_LIMIT_BYTES = 56 * 1024 * 1024

PROJ_TM = 1024
MIXOUT_TM = 1024
MOE_TM = 256
MB_ROWS = BF16_TILE_ROWS
_MB_SHIFT = MB_ROWS.bit_length() - 1
_TILE_MB = -(-(TOP_K * MOE_TM // MB_ROWS + N_EXPERTS * (MB_ROWS - 1) // MB_ROWS + 1) // 8) * 8
TILE_ROWS = _TILE_MB * MB_ROWS
ROW_CHUNK = 256
TAIL_CHUNKS = 2
NO_SLOT = 255.0
EXP_BM = 512
EXP_MB = EXP_BM // MB_ROWS
EXP_LEAD = 3
EXP_OUT = 3


def _cparams(sem):
    return pltpu.CompilerParams(dimension_semantics=sem, vmem_limit_bytes=VMEM_LIMIT_BYTES)


def _sigmoid(v):
    return 0.5 * jnp.tanh(0.5 * v) + 0.5


def _silu(v):
    return v * _sigmoid(v)


def _softplus(v):
    return jnp.maximum(v, 0.0) + jnp.log1p(jnp.exp(-jnp.abs(v)))


def _iota(shape, dim):
    return lax.broadcasted_iota(I32, shape, dim)


def _dot(a, b, **kw):
    return jnp.dot(a, b, preferred_element_type=F32, **kw)


def _dot_nt(a, b, **kw):
    return lax.dot_general(a, b, (((1,), (1,)), ((), ())), preferred_element_type=F32, **kw)


def _dot_tn(a, b, **kw):
    return lax.dot_general(a, b, (((0,), (0,)), ((), ())), preferred_element_type=F32, **kw)


def _rms(v, gain):
    return v * lax.rsqrt(jnp.mean(v * v, axis=-1, keepdims=True) + EPS) * gain


def _mod_kernel(c_ref, w_ref, b_ref, o_ref):
    o_ref[...] = _dot(_silu(c_ref[...]), w_ref[...], precision=HIGHEST) + b_ref[...]


def _modulation(cvec, w_mod, b_mod):
    rows, d = cvec.shape
    n = w_mod.shape[1]
    tn = 1024
    return pl.pallas_call(
        _mod_kernel,
        grid=(n // tn,),
        in_specs=[pl.BlockSpec((rows, d), lambda j: (0, 0)),
                  pl.BlockSpec((d, tn), lambda j: (0, j)),
                  pl.BlockSpec((1, tn), lambda j: (0, j))],
        out_specs=pl.BlockSpec((rows, tn), lambda j: (0, j)),
        out_shape=jax.ShapeDtypeStruct((rows, n), F32),
        compiler_params=_cparams(("arbitrary",)),
        name="adaln_modulation",
    )(cvec, w_mod, b_mod)


_PLAIN, _ROPE, _ROPE_SCALED, _SIGMOID = 0, 1, 2, 3


def _proj_kernel(x_ref, gain_ref, sc_ref, sh_ref, *refs, kinds, rope, n_w):
    w_refs = refs[:n_w]
    wg_ref, wgt_ref, prow_ref, pcol_ref, cos_ref, sin_ref = refs[n_w:n_w + 6]
    out_refs = refs[n_w + 6:]
    groups = [(w, c) for w in w_refs for c in range(0, w.shape[1], MIX_W)]
    n_feat = len(kinds)
    feat_refs = out_refs[:n_feat]
    cols_ref, grow_ref, gl_ref = out_refs[n_feat:]
    tm = x_ref.shape[0]

    x = x_ref[...]
    h = (_rms(x, gain_ref[...]) * (1.0 + sc_ref[...]) + sh_ref[...]).astype(BF16)

    r_i = _iota((CHUNK, CHUNK), 0)
    c_i = _iota((CHUNK, CHUNK), 1)
    lower_incl = (c_i <= r_i).astype(BF16)
    upper_incl = (c_i >= r_i).astype(BF16)

    def prefix_rows(v):
        hi, lo = _split_bf16(v)
        return _dot(lower_incl, hi) + _dot(lower_incl, lo)

    def prefix_lanes(v):
        hi, lo = _split_bf16(v)
        return _dot(hi, upper_incl) + _dot(lo, upper_incl)

    nc = N_HEADS * 8
    pg = _dot(h, wg_ref[...])
    colt = _iota((tm, nc), 1) & 7
    la = jnp.where((colt == 2) | (colt == 3), 0.0, prow_ref[0:1, :] * _softplus(pg + prow_ref[1:2, :]))
    beta = _sigmoid(pg)
    colc = _iota((CHUNK, nc), 1) & 7
    fwd_col = (colc & 1) == 0
    for c in range(tm // CHUNK):
        sl = slice(c * CHUNK, (c + 1) * CHUNK)
        la_c = la[sl]
        pre = prefix_rows(la_c)
        suf = pre[CHUNK - 1:CHUNK, :] - pre + la_c
        g_c = jnp.where(fwd_col, pre, suf)
        rest = jnp.where(fwd_col, suf, pre) - la_c
        vals = jnp.where(colc < 2, g_c, jnp.where(colc < 4, beta[sl], jnp.where(
            colc < 6, jnp.exp(g_c), jnp.exp(rest))))
        for hh in range(N_HEADS):
            cols_ref[hh, sl, :] = vals[:, 8 * hh:8 * hh + 8]
        gl_ref[c:c + 1, :] = jnp.exp(g_c[0:1, :] + rest[0:1, :])

    pgt = _dot_nt(wgt_ref[...], h)
    rowq = _iota((N_HEADS * 8, tm), 0) & 7
    lat = jnp.where(rowq < 2, pcol_ref[:, 0:1] * _softplus(pgt + pcol_ref[:, 1:2]), 0.0)
    rowc = _iota((N_HEADS * 8, CHUNK), 0) & 7
    for c in range(tm // CHUNK):
        sl = slice(c * CHUNK, (c + 1) * CHUNK)
        lat_c = lat[:, sl]
        pre_t = prefix_lanes(lat_c)
        suf_t = pre_t[:, CHUNK - 1:CHUNK] - pre_t + lat_c
        grow_ref[:, :, sl] = jnp.where(rowc == 0, pre_t, suf_t).reshape(N_HEADS, 8, CHUNK)

    if rope:
        cos2 = cos_ref[...]
        sin2 = sin_ref[...]

    for g, kind in enumerate(kinds):
        w_ref, c0 = groups[g]
        p = _dot(h, w_ref[:, c0:c0 + MIX_W])
        if kind == _SIGMOID:
            p = _sigmoid(p)
        elif kind in (_ROPE, _ROPE_SCALED) and rope:
            heads = []
            for hh in range(N_HEADS):
                t = p[:, hh * HEAD_DIM:(hh + 1) * HEAD_DIM]
                heads.append(t * cos2 + pltpu.roll(t, HEAD_DIM // 2, 1) * sin2)
            p = jnp.concatenate(heads, axis=1)
        if kind == _ROPE_SCALED:
            p = p * (HEAD_DIM ** -0.5)
        feat_refs[g][...] = p.astype(feat_refs[g].dtype)


def _in_projection(x, gain, scale, shift, w_mains, w_gab, w_gab_t, prow, pcol, cos2, sin2, kinds, rope):
    b, l, d = x.shape
    tm = min(PROJ_TM, l)
    tiles = l // tm
    n_chunk = tm // CHUNK
    feat_shapes = [jax.ShapeDtypeStruct((b, l, MIX_W), BF16) for _ in kinds]
    feat_specs = [pl.BlockSpec((None, tm, MIX_W), lambda i, j: (i, j, 0)) for _ in kinds]
    out_shape = feat_shapes + [jax.ShapeDtypeStruct((b, N_HEADS, l, 8), F32),
                               jax.ShapeDtypeStruct((b, N_HEADS, 8, l), F32),
                               jax.ShapeDtypeStruct((b, tiles, n_chunk, N_HEADS * 8), F32)]
    out_specs = feat_specs + [pl.BlockSpec((None, N_HEADS, tm, 8), lambda i, j: (i, 0, j, 0)),
                              pl.BlockSpec((None, N_HEADS, 8, tm), lambda i, j: (i, 0, 0, j)),
                              pl.BlockSpec((None, None, n_chunk, N_HEADS * 8), lambda i, j: (i, j, 0, 0))]
    const = functools.partial(pl.BlockSpec, pipeline_mode=pl.Buffered(1))
    in_specs = [
        pl.BlockSpec((None, tm, d), lambda i, j: (i, j, 0)),
        const((1, d), lambda i, j: (0, 0)),
        pl.BlockSpec((None, 1, d), lambda i, j: (i, 0, 0)),
        pl.BlockSpec((None, 1, d), lambda i, j: (i, 0, 0)),
    ] + [const(w.shape, lambda i, j: (0, 0)) for w in w_mains] + [
        const((d, N_HEADS * 8), lambda i, j: (0, 0)),
        const((N_HEADS * 8, d), lambda i, j: (0, 0)),
        const((2, N_HEADS * 8), lambda i, j: (0, 0)),
        const((N_HEADS * 8, 2), lambda i, j: (0, 0)),
        pl.BlockSpec((tm, HEAD_DIM), lambda i, j: (j, 0)),
        pl.BlockSpec((tm, HEAD_DIM), lambda i, j: (j, 0)),
    ]
    outs = pl.pallas_call(
        functools.partial(_proj_kernel, kinds=tuple(kinds), rope=rope, n_w=len(w_mains)),
        grid=(b, tiles),
        in_specs=in_specs,
        out_specs=out_specs,
        out_shape=out_shape,
        compiler_params=_cparams(("arbitrary", "arbitrary")),
        name="in_projection_rope" if rope else "in_projection_ctx",
    )(x, gain, scale, shift, *w_mains, w_gab, w_gab_t, prow, pcol, cos2, sin2)
    feats = outs[:len(kinds)]
    cols, grow, gl = outs[len(kinds):]
    return feats, (cols, grow, gl.reshape(b * (l // CHUNK), N_HEADS * 8))


def _unit_triangular_inverses(mats, lowers):
    r = _iota((CHUNK, CHUNK), 0)
    c = _iota((CHUNK, CHUNK), 1)
    eye = (r == c).astype(F32)
    invs = [eye - jnp.where((r >> 1) == (c >> 1), a, 0.0) for a in mats]
    for level in range(1, int(math.log2(CHUNK))):
        s = 1 << level
        mask = ((r >> (level + 1)) == (c >> (level + 1))) & ((r >> level) != (c >> level))
        invb = [inv.astype(BF16) for inv in invs]
        offs = [jnp.where(mask, a, 0.0).astype(BF16) for a in mats]
        if s < 8:
            half = [_dot(off, ib).astype(BF16) for off, ib in zip(offs, invb)]
            invs = [inv - _dot(ib, hf) for inv, ib, hf in zip(invs, invb, half)]
            continue
        def rows_of(x, lower, moving):
            first = s if (lower == moving) else 0
            return [x[g * 2 * s + first:g * 2 * s + first + s] for g in range(CHUNK // (2 * s))]

        half = [_dot(jnp.concatenate(rows_of(off, lo, True), axis=0), ib).astype(BF16)
                for off, ib, lo in zip(offs, invb, lowers)]
        zero = jnp.zeros((s, CHUNK), BF16)
        full = []
        for hf, lo in zip(half, lowers):
            pieces = []
            for g in range(CHUNK // (2 * s)):
                piece = hf[g * s:(g + 1) * s]
                pieces += [zero, piece] if lo else [piece, zero]
            full.append(jnp.concatenate(pieces, axis=0))
        corr = [_dot(jnp.concatenate(rows_of(ib, lo, True), axis=0), hf) for ib, hf, lo in zip(invb, full, lowers)]
        new = []
        for inv, cr, lo in zip(invs, corr, lowers):
            keep = rows_of(inv, lo, False)
            moved = [m - cr[g * s:(g + 1) * s] for g, m in enumerate(rows_of(inv, lo, True))]
            pieces = []
            for k, m in zip(keep, moved):
                pieces += [k, m] if lo else [m, k]
            new.append(jnp.concatenate(pieces, axis=0))
        invs = new
    return invs


def _mixer_kernel(*refs, seq_len, with_query):
    n_chunk = seq_len // CHUNK
    if with_query:
        (lg_ref, gl_ref, rk_ref, rv_ref, gk_ref, gv_ref, rq_ref, gq_ref, rg_ref, gz_ref,
         cols_ref, grow_ref, cq_ref, ck_ref, cv_ref, s0_ref, gnw_ref, rmsw_ref,
         yret_ref, ygdn_ref,
         qes, oret, ogdn, ubuf, wbuf, pbuf, nbuf, cbuf, kvbuf, rtile, state) = refs
    else:
        (lg_ref, gl_ref, rk_ref, rv_ref, gk_ref, gv_ref,
         cols_ref, grow_ref, ck_ref, cv_ref, s0_ref,
         sfin_ref,
         nbuf, cbuf, kvbuf, rtile, state) = refs
    bi = pl.program_id(0)
    hi = pl.program_id(1)

    row = _iota((CHUNK, CHUNK), 0)
    colm = _iota((CHUNK, CHUNK), 1)
    rowf = row.astype(F32)
    colf = colm.astype(F32)

    edge_row = _iota((F32_TILE_ROWS, HEAD_DIM), 0)

    def conv_chunk(src_ref, w_ref, n):
        s = pl.multiple_of(n * CHUNK, CHUNK)
        x = src_ref[pl.ds(s, CHUNK), :].astype(F32)
        ps = pl.multiple_of(jnp.maximum(s - BF16_TILE_ROWS, 0), BF16_TILE_ROWS)
        ns = pl.multiple_of(jnp.minimum(s + CHUNK, seq_len - BF16_TILE_ROWS), BF16_TILE_ROWS)
        prev_row = src_ref[pl.ds(ps, BF16_TILE_ROWS), :].astype(F32)[BF16_TILE_ROWS - 1:BF16_TILE_ROWS, :]
        next_row = src_ref[pl.ds(ns, BF16_TILE_ROWS), :].astype(F32)[0:1, :]
        prev_row = prev_row * jnp.where(n > 0, 1.0, 0.0)
        next_row = next_row * jnp.where(n < n_chunk - 1, 1.0, 0.0)
        down, up, rt = pltpu.roll(x, 1, 0), pltpu.roll(x, CHUNK - 1, 0), F32_TILE_ROWS
        first = jnp.where(edge_row == 0, jnp.broadcast_to(prev_row, (rt, HEAD_DIM)), down[:rt])
        last = jnp.where(edge_row == rt - 1, jnp.broadcast_to(next_row, (rt, HEAD_DIM)), up[CHUNK - rt:])
        xp = jnp.concatenate([first, down[rt:]], axis=0)
        xn = jnp.concatenate([up[:CHUNK - rt], last], axis=0)
        taps = 0.5 * w_ref[...]
        h = taps[0:1, :] * xp + taps[1:2, :] * x + taps[2:3, :] * xn
        return h * jnp.tanh(h) + h

    def l2n(v):
        return v * lax.rsqrt(jnp.sum(v * v, axis=-1, keepdims=True) + EPS)

    for d in range(2):
        lg = lg_ref[d, hi]
        if d == 0:
            dist, pos_q, pos_k = rowf - colf, rowf + 1.0, (CHUNK - 1.0) - rowf
        else:
            dist, pos_q, pos_k = colf - rowf, CHUNK - rowf, rowf
        rtile[4 * d + 0] = jnp.where(dist >= 0, jnp.exp(lg * jnp.maximum(dist, 0.0)), 0.0)
        rtile[4 * d + 1] = jnp.exp(lg * pos_q)
        rtile[4 * d + 2] = jnp.exp(lg * pos_k)
        rtile[4 * d + 3] = jnp.exp(lg * jnp.full((CHUNK, CHUNK), float(CHUNK), F32))

    state[...] = s0_ref[...]

    def bcast_col(cols, j):
        return jnp.broadcast_to(cols[:, j:j + 1], (CHUNK, CHUNK))

    pre_chunks = min(8, n_chunk)

    def prepass(m, carry):
        chunks = [m * pre_chunks + j for j in range(pre_chunks)]
        chunk_cs = [pl.ds(pl.multiple_of(n * CHUNK, CHUNK), CHUNK) for n in chunks]
        conv_k, conv_v, conv_q = [], [], []
        for n, cs in zip(chunks, chunk_cs):
            conv_k.append(l2n(conv_chunk(gk_ref, ck_ref, n)).astype(BF16))
            conv_v.append(conv_chunk(gv_ref, cv_ref, n).astype(BF16))
            if with_query:
                conv_q.append((l2n(conv_chunk(gq_ref, cq_ref, n)) * (HEAD_DIM ** -0.5)).astype(BF16))
                ogdn[cs, :] = jnp.zeros((CHUNK, HEAD_DIM), F32)

        jobs = [(j, d) for j in range(pre_chunks) for d in range(2)]
        css = [chunk_cs[j] for j, _ in jobs]
        slots = [d * n_chunk + chunks[j] for j, d in jobs]
        dirs = [d for _, d in jobs]
        ks = [rk_ref[cs, :] for cs in css]
        vs = [rv_ref[cs, :] for cs in css]
        if with_query:
            scs = [(_dot_nt(rq_ref[cs, :], k) * rtile[4 * d + 0]).astype(BF16) for cs, k, d in zip(css, ks, dirs)]
            outs = [_dot(sc, v) for sc, v in zip(scs, vs)]
            for j, cs in enumerate(chunk_cs):
                oret[cs, :] = outs[2 * j] + outs[2 * j + 1]
        kvs = [_dot_tn((k.astype(F32) * rtile[4 * d + 2]).astype(BF16), v) for k, v, d in zip(ks, vs, dirs)]
        for slot, kv in zip(slots, kvs):
            kvbuf[slot] = kv
        ks = [conv_k[j] for j, _ in jobs]
        vs = [conv_v[j] for j, _ in jobs]
        colss = [cols_ref[cs, :] for cs in css]
        betas = [bcast_col(cols, 2 + d) for cols, d in zip(colss, dirs)]
        incls = [(row >= colm) if d == 0 else (row <= colm) for d in dirs]
        stricts = [(row > colm) if d == 0 else (row < colm) for d in dirs]
        decs = [jnp.exp(bcast_col(cols, d) - jnp.broadcast_to(grow_ref[d:d + 1, cs], (CHUNK, CHUNK)))
                for cols, d, cs in zip(colss, dirs, css)]
        kks = [_dot_nt(k, k) for k in ks]
        mats = [kk * beta * jnp.where(strict, dec, 0.0) for kk, beta, strict, dec in zip(kks, betas, stricts, decs)]
        tinvs = [t.astype(BF16) for t in _unit_triangular_inverses(mats, [d == 0 for d in dirs])]
        rhs = [jnp.concatenate([(beta * v.astype(F32)).astype(BF16),
                                (beta * bcast_col(cols, 4 + d) * k.astype(F32)).astype(BF16)], axis=1)
               for beta, v, cols, d, k in zip(betas, vs, colss, dirs, ks)]
        uws = [_dot(t, r) for t, r in zip(tinvs, rhs)]
        us = [uw[:, :HEAD_DIM] for uw in uws]
        ws = [uw[:, HEAD_DIM:].astype(BF16) for uw in uws]
        kts = [(k.astype(F32) * bcast_col(cols, 6 + d)).astype(BF16) for k, cols, d in zip(ks, colss, dirs)]
        ncs = [_dot_tn(kt, uw.astype(BF16)) for kt, uw in zip(kts, uws)]
        for slot, nc_mat in zip(slots, ncs):
            cbuf[slot] = nc_mat[:, :HEAD_DIM]
            nbuf[slot] = nc_mat[:, HEAD_DIM:].astype(BF16)
        if with_query:
            for slot, u, w in zip(slots, us, ws):
                ubuf[slot] = u.astype(BF16)
                wbuf[slot] = w
            qks = [_dot_nt(conv_q[j], k) for (j, _), k in zip(jobs, ks)]
            for slot, qk, incl, dec in zip(slots, qks, incls, decs):
                pbuf[slot] = (qk * jnp.where(incl, dec, 0.0)).astype(BF16)
            for (j, d), cs, cols in zip(jobs, css, colss):
                qes[d, cs, :] = (conv_q[j].astype(F32) * bcast_col(cols, 4 + d)).astype(BF16)
        return carry

    lax.fori_loop(0, n_chunk // pre_chunks, prepass, 0)

    def finish(cs):
        ro = oret[cs, :]
        rc = ro - jnp.mean(ro, axis=-1, keepdims=True)
        ry = rc * lax.rsqrt(jnp.mean(rc * rc, axis=-1, keepdims=True) + EPS)
        yret_ref[cs, :] = (ry * gnw_ref[...] * _silu(rg_ref[cs, :].astype(F32))).astype(BF16)
        go = ogdn[cs, :]
        gy = go * lax.rsqrt(jnp.mean(go * go, axis=-1, keepdims=True) + EPS)
        ygdn_ref[cs, :] = (gy * rmsw_ref[...] * _silu(gz_ref[cs, :].astype(F32))).astype(BF16)

    def chunk_ids(n):
        nds = [n, n_chunk - 1 - n]
        css = [pl.ds(nd * CHUNK if isinstance(nd, int) else pl.multiple_of(nd * CHUNK, CHUNK), CHUNK) for nd in nds]
        slots = [d * n_chunk + nd for d, nd in enumerate(nds)]
        return nds, css, slots

    def late_outputs(n, vnb, finishing):
        _, css, slots = chunk_ids(n)
        for d in range(2):
            ogdn[css[d], :] += _dot(pbuf[slots[d]], vnb[d])
            if finishing:
                finish(css[d])

    def scan_step(n, vnb_prev, has_prev, finish_prev):
        nds, css, slots = chunk_ids(n)
        ret_st = [state[d] for d in range(2)]
        gdn_st = [state[2 + d] for d in range(2)]
        gdn_stb = [st.astype(BF16) for st in gdn_st]
        shrink = [_dot(nbuf[slot], stb) for slot, stb in zip(slots, gdn_stb)]
        for d in range(2):
            state[2 + d] = gl_ref[bi * n_chunk + nds[d], 8 * hi + d] * gdn_st[d] - shrink[d] + cbuf[slots[d]]
            state[d] = rtile[4 * d + 3] * ret_st[d] + kvbuf[slots[d]]
        if not with_query:
            return vnb_prev
        vnb = tuple((ubuf[slot].astype(F32) - _dot(wbuf[slot], stb)).astype(BF16)
                    for slot, stb in zip(slots, gdn_stb))
        for d in range(2):
            oret[css[d], :] += _dot(rq_ref[css[d], :], ret_st[d].astype(BF16)) * rtile[4 * d + 1]
            ogdn[css[d], :] += _dot(qes[d, css[d], :], gdn_stb[d])
        if has_prev:
            late_outputs(n - 1, vnb_prev, finish_prev)
        return vnb

    if with_query:
        half = n_chunk // 2
        zero = jnp.zeros((CHUNK, HEAD_DIM), BF16)
        vnb = scan_step(0, (zero, zero), has_prev=False, finish_prev=False)
        vnb = lax.fori_loop(1, half + 1, functools.partial(scan_step, has_prev=True, finish_prev=False), vnb)
        vnb = lax.fori_loop(half + 1, n_chunk, functools.partial(scan_step, has_prev=True, finish_prev=True), vnb)
        late_outputs(n_chunk - 1, vnb, True)
    else:
        lax.fori_loop(0, n_chunk, functools.partial(scan_step, has_prev=False, finish_prev=False), 0)
        sfin_ref[...] = state[...]


def _mixers(lg, gl, feats, dec, conv, s0, gnw, rmsw, with_query):
    cols, grow = dec
    b, l, _ = feats["rk"].shape
    n_slot = 2 * (l // CHUNK)
    assert (l // CHUNK) % min(8, l // CHUNK) == 0
    smem = pl.BlockSpec(memory_space=pltpu.SMEM)
    head = pl.BlockSpec((None, l, HEAD_DIM), lambda i, j: (i, 0, j))
    colspec = pl.BlockSpec((None, None, l, 8), lambda i, j: (i, j, 0, 0))
    rowspec = pl.BlockSpec((None, None, 8, l), lambda i, j: (i, j, 0, 0))
    tile_f32 = pltpu.VMEM((n_slot, CHUNK, CHUNK), F32)
    tile_bf16 = pltpu.VMEM((n_slot, CHUNK, CHUNK), BF16)
    st_spec = pl.BlockSpec((None, None, 4, HEAD_DIM, HEAD_DIM), lambda i, j: (i, j, 0, 0, 0))

    def conv_spec(which):
        return pl.BlockSpec((SHORT_CONV, HEAD_DIM), lambda i, j, w=which: (0, w * N_HEADS + j))

    def gain_spec():
        return pl.BlockSpec((1, HEAD_DIM), lambda i, j: (0, j))

    if with_query:
        args = [lg, gl, feats["rk"], feats["rv"], feats["gk"], feats["gv"], feats["rq"], feats["gq"], feats["rg"],
                feats["gz"], cols, grow, conv, conv, conv, s0, gnw, rmsw]
        in_specs = [smem, smem] + [head] * 8 + [colspec, rowspec, conv_spec(0), conv_spec(1), conv_spec(2),
                                                st_spec, gain_spec(), gain_spec()]
        out_shape = [jax.ShapeDtypeStruct((b, l, MIX_W), BF16)] * 2
        out_specs = [head, head]
        scratch = ([pltpu.VMEM((2, l, HEAD_DIM), BF16)] + [pltpu.VMEM((l, HEAD_DIM), F32)] * 2
                   + [tile_bf16, tile_bf16, tile_bf16, tile_bf16, tile_f32, tile_f32])
    else:
        args = [lg, gl, feats["rk"], feats["rv"], feats["gk"], feats["gv"], cols, grow, conv, conv, s0]
        in_specs = [smem, smem] + [head] * 4 + [colspec, rowspec, conv_spec(1), conv_spec(2), st_spec]
        out_shape = jax.ShapeDtypeStruct((b, N_HEADS, 4, HEAD_DIM, HEAD_DIM), F32)
        out_specs = st_spec
        scratch = [tile_bf16, tile_f32, tile_f32]
    scratch = scratch + [pltpu.VMEM((8, CHUNK, CHUNK), F32), pltpu.VMEM((4, HEAD_DIM, HEAD_DIM), F32)]
    return pl.pallas_call(
        functools.partial(_mixer_kernel, seq_len=l, with_query=with_query),
        grid=(b, N_HEADS),
        in_specs=in_specs,
        out_specs=out_specs,
        out_shape=out_shape,
        scratch_shapes=scratch,
        compiler_params=_cparams(("arbitrary", "arbitrary")),
        name="mixers_latent" if with_query else "mixers_context",
    )(*args)


def _split_bf16(v):
    hi = v.astype(BF16)
    return hi, (v - hi.astype(F32)).astype(BF16)


def _route(h2, wrt_ref, bias_ref, cand_ref, sel_ref, wd_ref):
    tm = h2.shape[0]
    h_hi, h_lo = _split_bf16(h2)
    w_hi, w_lo = _split_bf16(wrt_ref[...])
    logits = _dot_nt(w_hi, h_hi) + (_dot_nt(w_hi, h_lo) + _dot_nt(w_lo, h_hi))
    scores = _sigmoid(logits)
    biased = scores + bias_ref[...]
    neg_inf = float("-inf")
    sub = _iota((GROUP_SIZE, tm), 0).astype(F32)
    group_score = []
    for g in range(N_GROUPS):
        blk = biased[g * GROUP_SIZE:(g + 1) * GROUP_SIZE, :]
        m1 = jnp.max(blk, axis=0, keepdims=True)
        first = jnp.min(jnp.where(blk == m1, sub, float(GROUP_SIZE)), axis=0, keepdims=True)
        m2 = jnp.max(jnp.where(sub == first, neg_inf, blk), axis=0, keepdims=True)
        group_score.append(m1 + m2)
    for g in range(N_GROUPS):
        ahead = jnp.zeros((1, tm), I32)
        for g2 in range(N_GROUPS):
            if g2 == g:
                continue
            before = (group_score[g2] > group_score[g])
            if g2 < g:
                before = before | (group_score[g2] == group_score[g])
            ahead = ahead + before.astype(I32)
        keep = jnp.broadcast_to(ahead, (GROUP_SIZE, tm)) < TOPK_GROUPS
        cand_ref[g * GROUP_SIZE:(g + 1) * GROUP_SIZE, :] = jnp.where(
            keep, biased[g * GROUP_SIZE:(g + 1) * GROUP_SIZE, :], neg_inf)
    work = cand_ref[...]
    eidx = _iota((N_EXPERTS, tm), 0).astype(F32)
    sel = jnp.zeros((N_EXPERTS, tm), jnp.bool_)
    for _ in range(TOP_K):
        best = jnp.max(work, axis=0, keepdims=True)
        first = jnp.min(jnp.where(work == best, eidx, float(N_EXPERTS)), axis=0, keepdims=True)
        pick = eidx == first
        sel = sel | pick
        work = jnp.where(pick, neg_inf, work)
    picked = jnp.where(sel, scores, 0.0)
    wsum = jnp.sum(picked, axis=0, keepdims=True)
    sel_ref[...] = sel.astype(F32)
    wd_ref[...] = picked / wsum * ROUTED_SCALE


def _mixout_kernel(x_ref, yr_ref, yg_ref, g0_ref, g1_ref, g2_ref, g3_ref, wr_ref, wg_ref, wo_ref,
                   npost_ref, gate1_ref, nffn_ref, sc2_ref, sh2_ref, wrt_ref, bias_ref,
                   x1_ref, h2_ref, sel_ref, wd_ref, cand_ref):
    r = _dot(yr_ref[...], wr_ref[...])
    g = _dot(yg_ref[...], wg_ref[...])
    half = r.shape[1] // 2
    merged = jnp.concatenate(
        [g0_ref[...].astype(F32) * r[:, :half] + g2_ref[...].astype(F32) * g[:, :half],
         g1_ref[...].astype(F32) * r[:, half:] + g3_ref[...].astype(F32) * g[:, half:]], axis=1)
    mo = _dot(merged.astype(BF16), wo_ref[...])
    x1 = x_ref[...] + gate1_ref[...] * _rms(mo, npost_ref[...])
    x1_ref[...] = x1
    h2 = _rms(x1, nffn_ref[...]) * (1.0 + sc2_ref[...]) + sh2_ref[...]
    h2_ref[...] = h2.astype(BF16)
    _route(h2, wrt_ref, bias_ref, cand_ref, sel_ref, wd_ref)


def _mixer_out(x, yr, yg, gates, wr, wg, wo, npost, gate1, nffn, sc2, sh2, wrt, bias):
    b, l, d = x.shape
    tm = min(MIXOUT_TM, l)
    tiles = l // tm
    tok = lambda w: pl.BlockSpec((None, tm, w), lambda i, j: (i, j, 0))
    const = functools.partial(pl.BlockSpec, pipeline_mode=pl.Buffered(1))
    vec = lambda: const((1, d), lambda i, j: (0, 0))
    bvec = lambda: pl.BlockSpec((None, 1, d), lambda i, j: (i, 0, 0))
    emap = pl.BlockSpec((N_EXPERTS, tm), lambda i, j: (0, i * tiles + j))
    in_specs = ([tok(d), tok(MIX_W), tok(MIX_W)] + [tok(MIX_W)] * 4
                + [const((MIX_W, d), lambda i, j: (0, 0)), const((MIX_W, d), lambda i, j: (0, 0)),
                   const((d, d), lambda i, j: (0, 0)),
                   vec(), bvec(), vec(), bvec(), bvec(),
                   const((N_EXPERTS, d), lambda i, j: (0, 0)), const((N_EXPERTS, 1), lambda i, j: (0, 0))])
    return pl.pallas_call(
        _mixout_kernel,
        grid=(b, tiles),
        in_specs=in_specs,
        out_specs=[tok(d), tok(d), emap, emap],
        out_shape=[jax.ShapeDtypeStruct((b, l, d), F32), jax.ShapeDtypeStruct((b, l, d), BF16),
                   jax.ShapeDtypeStruct((N_EXPERTS, b * l), F32), jax.ShapeDtypeStruct((N_EXPERTS, b * l), F32)],
        scratch_shapes=[pltpu.VMEM((N_EXPERTS, tm), F32)],
        compiler_params=_cparams(("arbitrary", "arbitrary")),
        name="mixer_out_router",
    )(x, yr, yg, *gates, wr, wg, wo, npost, gate1, nffn, sc2, sh2, wrt, bias)


def _tile_positions(sel):
    tm = sel.shape[1]
    selb = sel.astype(BF16)
    earlier = (_iota((tm, tm), 0) < _iota((tm, tm), 1)).astype(BF16)
    rank = _dot(selb, earlier)
    cnt = _dot(selb, jnp.ones((tm, tm), BF16))
    nmb = jnp.floor((cnt + (MB_ROWS - 1)) * (1.0 / MB_ROWS))
    below = (_iota((N_EXPERTS, N_EXPERTS), 1) < _iota((N_EXPERTS, N_EXPERTS), 0)).astype(BF16)
    offmb = _dot(below, nmb.astype(BF16))
    rank_hi = jnp.floor(rank * (1.0 / MB_ROWS))
    hi = jnp.where(sel > 0.0, offmb + rank_hi, NO_SLOT)
    lo = rank - rank_hi * MB_ROWS
    return hi, lo


def _micro_block_keys(sel, per_pair=()):
    tm = sel.shape[1]
    hi, lo = _tile_positions(sel)
    cnt_row = _dot_nt(jnp.ones((8, tm), BF16), sel.astype(BF16))
    nmb_row = jnp.floor((cnt_row + (MB_ROWS - 1)) * (1.0 / MB_ROWS))
    before = (_iota((N_EXPERTS, N_EXPERTS), 0) < _iota((N_EXPERTS, N_EXPERTS), 1)).astype(BF16)
    start = _dot(nmb_row.astype(BF16), before)[0:1, :]
    end = start + nmb_row[0:1, :]
    m_e = _iota((_TILE_MB, N_EXPERTS), 0).astype(F32)
    owner = ((m_e >= start) & (m_e < end)).astype(BF16)
    cols = [hi.astype(BF16), lo.astype(BF16)] + [v.astype(BF16) for v in per_pair]
    got = _dot(owner, jnp.concatenate(cols, axis=1))
    m_t = _iota((_TILE_MB, tm), 0).astype(F32)
    used = jnp.sum(nmb_row[0:1, :])
    key = jnp.where((got[:, :tm] == m_t) & (m_t < used), got[:, tm:2 * tm], -1.0)
    return key, cnt_row, used, [got[:, (2 + k) * tm:(3 + k) * tm] for k in range(len(per_pair))]


def _dispatch_kernel(h2_ref, sel_ref, xs_ref, cnt_ref, key_ref, hot_ref):
    last = pl.num_programs(0) - 1

    @pl.when(pl.program_id(0) < last)
    def _():
        _dispatch_tile(h2_ref, sel_ref, xs_ref, cnt_ref, key_ref, hot_ref)

    @pl.when(pl.program_id(0) == last)
    def _():
        xs_ref[...] = jnp.zeros_like(xs_ref)


def _dispatch_tile(h2_ref, sel_ref, xs_ref, cnt_ref, key_ref, hot_ref):
    tm = h2_ref.shape[0]
    key, cnt_row, used, _ = _micro_block_keys(sel_ref[...])
    key_ref[...] = key
    cnt_ref[...] = cnt_row
    x = h2_ref[...]
    rch = ROW_CHUNK
    in_block = _iota((MB_ROWS, tm), 0).astype(F32)

    def one_hot(rc):
        for m in range(rc * rch // MB_ROWS, (rc + 1) * rch // MB_ROWS):
            hit = jnp.broadcast_to(key_ref[m:m + 1, :], (MB_ROWS, tm)) == in_block
            hot_ref[m * MB_ROWS:(m + 1) * MB_ROWS, :] = jnp.where(hit, 1.0, 0.0).astype(BF16)

    n_rc = TILE_ROWS // rch
    n_sure = n_rc - TAIL_CHUNKS
    one_hot(0)
    for rc in range(n_sure):
        if rc + 1 < n_sure:
            one_hot(rc + 1)
        xs_ref[rc * rch:(rc + 1) * rch, :] = _dot(hot_ref[rc * rch:(rc + 1) * rch, :], x).astype(BF16)
    tail_used = used * MB_ROWS > n_sure * rch

    @pl.when(tail_used)
    def _():
        for rc in range(n_sure, n_rc):
            one_hot(rc)
            xs_ref[rc * rch:(rc + 1) * rch, :] = _dot(hot_ref[rc * rch:(rc + 1) * rch, :], x).astype(BF16)

    @pl.when(jnp.logical_not(tail_used))
    def _():
        xs_ref[n_sure * rch:, :] = jnp.zeros(((n_rc - n_sure) * rch, x.shape[1]), BF16)


def _dispatch(h2, sel_t):
    t, d = h2.shape
    tm = MOE_TM
    nt = t // tm
    return pl.pallas_call(
        _dispatch_kernel,
        grid=(nt + 1,),
        in_specs=[pl.BlockSpec((tm, d), lambda i: (jnp.minimum(i, nt - 1), 0)),
                  pl.BlockSpec((N_EXPERTS, tm), lambda i: (0, jnp.minimum(i, nt - 1)))],
        out_specs=[pl.BlockSpec((TILE_ROWS, d), lambda i: (i, 0)),
                   pl.BlockSpec((None, 8, N_EXPERTS), lambda i: (jnp.minimum(i, nt - 1), 0, 0))],
        out_shape=[jax.ShapeDtypeStruct((nt * TILE_ROWS + (EXP_OUT + 1) * EXP_BM, d), BF16),
                   jax.ShapeDtypeStruct((nt, 8, N_EXPERTS), F32)],
        scratch_shapes=[pltpu.VMEM((_TILE_MB, tm), F32), pltpu.VMEM((TILE_ROWS, tm), BF16)],
        compiler_params=_cparams(("arbitrary",)),
        name="moe_dispatch",
    )(h2, sel_t)


def _expert_plan(cnt, n_blk):
    nt = cnt.shape[0]
    nmb = (cnt + (MB_ROWS - 1)) // MB_ROWS
    offmb = jnp.cumsum(nmb, axis=1) - nmb
    per_e = nmb.T
    incl = jnp.cumsum(per_e, axis=1)
    excl = incl - per_e
    tot = incl[:, -1]
    nb = (tot + (EXP_MB - 1)) // EXP_MB
    bend = jnp.cumsum(nb)
    bstart = bend - nb
    b = jnp.arange(n_blk, dtype=I32)
    valid = b < bend[-1]
    last = jnp.maximum(bend[-1] - 1, 0)
    bq = jnp.where(valid, b, last)
    blk_e = jnp.minimum(jnp.sum((bend[None, :] <= bq[:, None]).astype(I32), axis=1), N_EXPERTS - 1)
    onehot_e = (blk_e[:, None] == jnp.arange(N_EXPERTS, dtype=I32)[None, :]).astype(I32)
    pick = lambda table: jnp.dot(onehot_e.astype(F32), table.astype(F32), precision=HIGHEST).astype(I32)
    bstart_b = jnp.sum(onehot_e * bstart[None, :], axis=1)
    tot_b = jnp.sum(onehot_e * tot[None, :], axis=1)
    p0 = (bq - bstart_b) * EXP_MB
    blk_nmb = jnp.where(valid, jnp.clip(tot_b - p0, 0, EXP_MB), 0).astype(I32)
    p = p0[:, None] + jnp.arange(EXP_MB, dtype=I32)[None, :]
    incl_b = pick(incl)
    passed = (incl_b[:, None, :] <= p[:, :, None]).astype(I32)
    tile = jnp.minimum(jnp.sum(passed, axis=2), nt - 1)
    shift_b = pick(offmb.T - excl)
    step_b = jnp.concatenate([shift_b[:, 1:] - shift_b[:, :-1], jnp.zeros_like(shift_b[:, :1])], axis=1)
    shift = shift_b[:, :1] + jnp.sum(passed * step_b[:, None, :], axis=2)
    where = tile * _TILE_MB + p + shift
    j = jnp.arange(EXP_MB, dtype=I32)[None, :]
    used = j < blk_nmb[:, None]
    spare = nt * _TILE_MB + (b[:, None] % EXP_OUT) * EXP_MB + j
    zeros_mb = nt * _TILE_MB + EXP_OUT * EXP_MB + j
    src = jnp.where(used, where, jnp.where(valid[:, None], where[:, :1], zeros_mb))
    dst = jnp.where(used, where, spare)
    dst = jnp.concatenate([nt * _TILE_MB + (EXP_OUT - 1) * EXP_MB + j, dst], axis=0)
    return bstart.astype(I32), nb.astype(I32), src.reshape(-1).astype(I32), dst.reshape(-1).astype(I32)


def _expert_kernel(bstart_ref, nb_ref, src_ref, dst_ref, xs_hbm, wg_ref, wu_ref, wd_ref, ys_hbm,
                   xbuf, ybuf, wgub, wdb, in_sem, out_sem):
    e = pl.program_id(0)
    n_in = xbuf.shape[0]
    n_out = ybuf.shape[0]

    def gather_copy(blk, slot, j):
        rows = pl.ds(pl.multiple_of(src_ref[blk * EXP_MB + j] * MB_ROWS, MB_ROWS), MB_ROWS)
        return pltpu.make_async_copy(xs_hbm.at[rows, :], xbuf.at[slot, j * MB_ROWS:(j + 1) * MB_ROWS, :],
                                     in_sem.at[slot])

    def scatter_copy(blk, slot, j):
        rows = pl.ds(pl.multiple_of(dst_ref[(blk + 1) * EXP_MB + j] * MB_ROWS, MB_ROWS), MB_ROWS)
        return pltpu.make_async_copy(ybuf.at[slot, j * MB_ROWS:(j + 1) * MB_ROWS, :], ys_hbm.at[rows, :],
                                     out_sem.at[slot])

    def start_gather(blk):
        for j in range(EXP_MB):
            gather_copy(blk, lax.rem(blk, n_in), j).start()

    def wait_gather(blk):
        for j in range(EXP_MB):
            gather_copy(blk, lax.rem(blk, n_in), j).wait()

    def start_scatter(blk):
        for j in range(EXP_MB):
            scatter_copy(blk, lax.rem(blk + n_out, n_out), j).start()

    def wait_scatter(slot):
        for j in range(EXP_MB):
            scatter_copy(0, slot, j).wait()

    @pl.when(e == 0)
    def _():
        ybuf[...] = jnp.zeros_like(ybuf)
        for ahead in range(EXP_LEAD):
            start_gather(ahead)
        spare_row0 = ys_hbm.shape[0] - (n_out + 1) * EXP_BM
        for slot in range(n_out - 1):
            for j in range(EXP_MB):
                spare = pl.ds(spare_row0 + (slot * EXP_MB + j) * MB_ROWS, MB_ROWS)
                pltpu.make_async_copy(ybuf.at[slot, j * MB_ROWS:(j + 1) * MB_ROWS, :], ys_hbm.at[spare, :],
                                      out_sem.at[slot]).start()

    de = wg_ref.shape[1]
    wgub[:, :de] = wg_ref[...].astype(BF16)
    wgub[:, de:] = wu_ref[...].astype(BF16)
    wdb[...] = wd_ref[...].astype(BF16)

    def block(b, carry):
        slot = lax.rem(b, n_out)
        wait_gather(b)
        wait_scatter(slot)
        x = xbuf[lax.rem(b, n_in)]
        gu = _dot(x, wgub[...])
        act = (_silu(gu[:, :de]) * gu[:, de:]).astype(BF16)
        start_gather(b + EXP_LEAD)
        start_scatter(b - 1)
        ybuf[slot] = _dot(act, wdb[...]).astype(BF16)
        return carry

    first = bstart_ref[e]
    lax.fori_loop(first, first + nb_ref[e], block, 0)

    @pl.when(e == pl.num_programs(0) - 1)
    def _():
        total = first + nb_ref[e]
        start_scatter(total - 1)
        for ahead in range(EXP_LEAD):
            wait_gather(total + ahead)
        for slot in range(n_out):
            wait_scatter(slot)


def _expert_ffn(xs, bstart, nb, src, dst, wg, wu, wd):
    rows, d = xs.shape
    de = wg.shape[2]
    grid_spec = pltpu.PrefetchScalarGridSpec(
        num_scalar_prefetch=4,
        grid=(wg.shape[0],),
        in_specs=[pl.BlockSpec(memory_space=pl.ANY),
                  pl.BlockSpec((None, d, de), lambda i, bs, bn, sr, ds: (i, 0, 0)),
                  pl.BlockSpec((None, d, de), lambda i, bs, bn, sr, ds: (i, 0, 0)),
                  pl.BlockSpec((None, de, d), lambda i, bs, bn, sr, ds: (i, 0, 0))],
        out_specs=pl.BlockSpec(memory_space=pl.ANY),
        scratch_shapes=[pltpu.VMEM((EXP_LEAD + 1, EXP_BM, d), BF16), pltpu.VMEM((EXP_OUT, EXP_BM, d), BF16),
                        pltpu.VMEM((d, 2 * de), BF16), pltpu.VMEM((de, d), BF16),
                        pltpu.SemaphoreType.DMA((EXP_LEAD + 1,)), pltpu.SemaphoreType.DMA((EXP_OUT,))],
    )
    return pl.pallas_call(
        _expert_kernel,
        grid_spec=grid_spec,
        out_shape=jax.ShapeDtypeStruct((rows, d), BF16),
        input_output_aliases={4: 0},
        compiler_params=_cparams(("arbitrary",)),
        name="moe_experts",
    )(bstart, nb, src, dst, xs, wg, wu, wd)


def _combine_kernel(ys_ref, sel_ref, wd_ref, h2_ref, x1_ref, wsg_ref, wsu_ref, wsd_ref, npost_ref, gate2_ref,
                    o_ref, key_ref, wmb_ref, c_ref, acc_ref):
    tm = h2_ref.shape[0]
    key, _, used, (wmb,) = _micro_block_keys(sel_ref[...], (wd_ref[...],))
    key_ref[...] = key
    wmb_ref[...] = wmb
    in_block = _iota((MB_ROWS, tm), 0).astype(F32)

    def build(lo_row, hi_row):
        for m in range(lo_row // MB_ROWS, hi_row // MB_ROWS):
            hit = jnp.broadcast_to(key_ref[m:m + 1, :], (MB_ROWS, tm)) == in_block
            weight = jnp.broadcast_to(wmb_ref[m:m + 1, :], (MB_ROWS, tm))
            c_ref[m * MB_ROWS:(m + 1) * MB_ROWS, :] = jnp.where(hit, weight, 0.0).astype(BF16)

    sure = TILE_ROWS - TAIL_CHUNKS * ROW_CHUNK
    build(0, sure)
    h2 = h2_ref[...]
    shared = _dot((_silu(_dot(h2, wsg_ref[...])) * _dot(h2, wsu_ref[...])).astype(BF16), wsd_ref[...])
    acc_ref[...] = shared + _dot_tn(c_ref[:sure, :], ys_ref[:sure, :])

    @pl.when(used * MB_ROWS > sure)
    def _():
        build(sure, TILE_ROWS)
        acc_ref[...] += _dot_tn(c_ref[sure:, :], ys_ref[sure:, :])

    o_ref[...] = x1_ref[...] + gate2_ref[...] * _rms(acc_ref[...], npost_ref[...])


def _combine(ys, sel_t, wd_t, h2, x1, wsg, wsu, wsd, npost, gate2, seq_len):
    t, d = h2.shape
    tm = MOE_TM
    nt = t // tm
    per_seq = seq_len // tm
    ds = wsg.shape[1]
    const = functools.partial(pl.BlockSpec, pipeline_mode=pl.Buffered(1))
    emap = pl.BlockSpec((N_EXPERTS, tm), lambda i: (0, i))
    return pl.pallas_call(
        _combine_kernel,
        grid=(nt,),
        in_specs=[pl.BlockSpec((TILE_ROWS, d), lambda i: (i, 0)), emap, emap,
                  pl.BlockSpec((tm, d), lambda i: (i, 0)), pl.BlockSpec((tm, d), lambda i: (i, 0)),
                  const((d, ds), lambda i: (0, 0)), const((d, ds), lambda i: (0, 0)), const((ds, d), lambda i: (0, 0)),
                  const((1, d), lambda i: (0, 0)),
                  pl.BlockSpec((None, 1, d), lambda i: (i // per_seq, 0, 0))],
        out_specs=pl.BlockSpec((tm, d), lambda i: (i, 0)),
        out_shape=jax.ShapeDtypeStruct((t, d), F32),
        scratch_shapes=[pltpu.VMEM((_TILE_MB, tm), F32), pltpu.VMEM((_TILE_MB, tm), F32),
                        pltpu.VMEM((TILE_ROWS, tm), BF16), pltpu.VMEM((tm, d), F32)],
        compiler_params=_cparams(("arbitrary",)),
        name="moe_combine",
    )(ys, sel_t, wd_t, h2, x1, wsg, wsu, wsd, npost, gate2)


def _rope_tables(n):
    rows = n // GRID_W
    pos_r = jnp.repeat(jnp.arange(rows, dtype=F32), GRID_W)
    pos_c = jnp.tile(jnp.arange(GRID_W, dtype=F32), rows)
    n_freq = HEAD_DIM // 4
    inv = ROPE_BASE ** (-jnp.arange(n_freq, dtype=F32) / n_freq)
    ang = jnp.concatenate([pos_r[:, None] * inv, pos_c[:, None] * inv], axis=-1)
    cos, sin = jnp.cos(ang), jnp.sin(ang)
    return jnp.concatenate([cos, cos], axis=-1), jnp.concatenate([-sin, sin], axis=-1)


def kernel(x, c, ctx, c_ctx, w_mod, b_mod, norm_mix_pre, norm_mix_post, norm_ffn_pre, norm_ffn_post, w_in, gdn_conv, ret_log_decay, gdn_a_log, gdn_dt_bias, ret_gn_w, gdn_norm_w, w_ret_out, w_gdn_out, w_o, w_router, router_bias, w_gate, w_up, w_down, w_sh_gate, w_sh_up, w_sh_down):
    b, n, d = x.shape
    depth = w_mod.shape[0]
    assert depth == 1, "single-layer block"
    assert all(n % min(tile, n) == 0 for tile in (PROJ_TM, MIXOUT_TM, MOE_TM)) and n % MOE_TM == 0
    assert ctx.shape[1] % CHUNK == 0 and (EXP_OUT + 1) * EXP_BM <= TILE_ROWS
    assert _TILE_MB < NO_SLOT and MOE_TM <= 256, "slot indices and ranks must stay exact in bf16"

    rows = -(-(b + 1) // 8) * 8
    cvec = jnp.zeros((rows, d), F32).at[:b].set(c).at[b].set(c_ctx)
    mod = _modulation(cvec, w_mod[0], b_mod[0][None, :])
    sh1, sc1, g1, sh2, sc2, g2 = [mod[:b, k * d:(k + 1) * d][:, None, :] for k in range(6)]
    ctx_shift = jnp.broadcast_to(mod[b, 0:d][None, None, :], (b, 1, d))
    ctx_scale = jnp.broadcast_to(mod[b, d:2 * d][None, None, :], (b, 1, d))

    w_in0 = w_in[0]
    n_main = 4 * MIX_W
    w_state = w_in0[:, :n_main].astype(BF16)
    slots = ((0, 0), (0, 1), (1, 0), (1, 1), (0, 0), (0, 1), (0, 0), (0, 1))
    gab_cols = n_main + jnp.array([[ab * 2 * N_HEADS + dr * N_HEADS + hh for ab, dr in slots]
                                   for hh in range(N_HEADS)], I32)
    w_gab = w_in0[:, gab_cols.reshape(-1)].astype(BF16)
    n_state = n_main + N_GAB
    w_query = w_in0[:, n_state:].astype(BF16)
    a_coef = -jnp.exp(gdn_a_log[0].astype(F32))
    dtb = gdn_dt_bias[0].astype(F32)
    is_alpha = jnp.array([ab == 0 for ab, _ in slots])
    dirs = jnp.array([dr for _, dr in slots], I32)
    prow = jnp.stack([jnp.where(is_alpha[None, :], a_coef.T[:, dirs], 0.0).reshape(-1),
                      jnp.where(is_alpha[None, :], dtb.T[:, dirs], 0.0).reshape(-1)], axis=0)
    first2 = jnp.tile(jnp.arange(8) < 2, N_HEADS)
    w_gab_t = w_gab.T * first2[:, None].astype(BF16)
    pcol = prow.T * first2[:, None]
    gain_mix = norm_mix_pre[0][None, :]
    cos2, sin2 = _rope_tables(n)

    state_kinds = (_ROPE_SCALED, _PLAIN, _PLAIN, _PLAIN)
    query_kinds = (_ROPE, _PLAIN, _PLAIN, _PLAIN, _SIGMOID, _SIGMOID, _SIGMOID, _SIGMOID)
    lg = ret_log_decay[0].astype(F32)
    conv = gdn_conv[0].astype(F32)

    lc = ctx.shape[1]
    cfe, (ccols, cgrow, cgl) = _in_projection(
        ctx, gain_mix, ctx_scale, ctx_shift, (w_state,), w_gab, w_gab_t, prow, pcol,
        cos2[:lc], sin2[:lc], state_kinds, rope=False)
    cfeats = dict(zip(("rk", "rv", "gk", "gv"), cfe))
    zero_state = jnp.zeros((b, N_HEADS, 4, HEAD_DIM, HEAD_DIM), F32)
    init = _mixers(lg, cgl, cfeats, (ccols, cgrow), conv, zero_state, None, None, with_query=False)

    fe, (cols, grow, gl) = _in_projection(
        x, gain_mix, sc1, sh1, (w_state, w_query), w_gab, w_gab_t, prow, pcol, cos2, sin2,
        state_kinds + query_kinds, rope=True)
    feats = dict(zip(("rk", "rv", "gk", "gv", "rq", "rg", "gq", "gz"), fe[:8]))
    gates = fe[8:]
    y_ret, y_gdn = _mixers(lg, gl, feats, (cols, grow), conv, init,
                           ret_gn_w[0][None, :], gdn_norm_w[0][None, :], with_query=True)
    x1, h2, sel_t, wd_t = _mixer_out(
        x, y_ret, y_gdn, gates, w_ret_out[0].astype(BF16), w_gdn_out[0].astype(BF16), w_o[0].astype(BF16),
        norm_mix_post[0][None, :], g1, norm_ffn_pre[0][None, :], sc2, sh2,
        w_router[0].T.astype(F32), router_bias[0].astype(F32)[:, None])

    t = b * n
    h2f = h2.reshape(t, d)
    xs, cnt = _dispatch(h2f, sel_t)
    nt = t // MOE_TM
    n_blk = nt * _TILE_MB // EXP_MB + N_EXPERTS + EXP_LEAD
    bstart, nb, src, dst = _expert_plan(cnt[:, 0, :].astype(I32), n_blk)
    ys = _expert_ffn(xs, bstart, nb, src, dst, w_gate[0], w_up[0], w_down[0])
    out = _combine(ys, sel_t, wd_t, h2f, x1.reshape(t, d), w_sh_gate[0].astype(BF16), w_sh_up[0].astype(BF16),
                   w_sh_down[0].astype(BF16), norm_ffn_post[0][None, :], g2, n)
    return out.reshape(b, n, d)
```

```python
import functools
import math

import jax
import jax.numpy as jnp
import numpy as np
from jax import lax
from jax.experimental import pallas as pl
from jax.experimental.pallas import tpu as pltpu

F32 = jnp.float32
BF16 = jnp.bfloat16
I32 = jnp.int32
HIGHEST = lax.Precision.HIGHEST

N_HEADS = 4
HEAD_DIM = 128
MIX_W = N_HEADS * HEAD_DIM
CHUNK = 128
SHORT_CONV = 3
ROPE_BASE = 10000.0
GRID_W = 64
N_EXPERTS = 64
TOP_K = 8
N_GROUPS = 8
TOPK_GROUPS = 4
GROUP_SIZE = N_EXPERTS // N_GROUPS
ROUTED_SCALE = 2.5
EPS = 1e-6
N_GAB = 4 * N_HEADS

LANES = 128
F32_TILE_ROWS = 8
BF16_TILE_ROWS = 16
VMEM_LIMIT_BYTES = 56 * 1024 * 1024

PROJ_TM = 1024
MIXOUT_TM = 1024
MOE_TM = 256
MB_ROWS = BF16_TILE_ROWS
_MB_SHIFT = MB_ROWS.bit_length() - 1
_TILE_MB = -(-(TOP_K * MOE_TM // MB_ROWS + N_EXPERTS * (MB_ROWS - 1) // MB_ROWS + 1) // 8) * 8
TILE_ROWS = _TILE_MB * MB_ROWS
ROW_CHUNK = 256
TAIL_CHUNKS = 2
NO_SLOT = 255.0
EXP_BM = 512
EXP_MB = EXP_BM // MB_ROWS
EXP_LEAD = 3
EXP_OUT = 3


def _cparams(sem):
    return pltpu.CompilerParams(dimension_semantics=sem, vmem_limit_bytes=VMEM_LIMIT_BYTES)


def _sigmoid(v):
    return 0.5 * jnp.tanh(0.5 * v) + 0.5


def _silu(v):
    return v * _sigmoid(v)


def _softplus(v):
    return jnp.maximum(v, 0.0) + jnp.log1p(jnp.exp(-jnp.abs(v)))


def _iota(shape, dim):
    return lax.broadcasted_iota(I32, shape, dim)


def _dot(a, b, **kw):
    return jnp.dot(a, b, preferred_element_type=F32, **kw)


def _dot_nt(a, b, **kw):
    return lax.dot_general(a, b, (((1,), (1,)), ((), ())), preferred_element_type=F32, **kw)


def _dot_tn(a, b, **kw):
    return lax.dot_general(a, b, (((0,), (0,)), ((), ())), preferred_element_type=F32, **kw)


def _rms(v, gain):
    return v * lax.rsqrt(jnp.mean(v * v, axis=-1, keepdims=True) + EPS) * gain


def _mod_kernel(c_ref, w_ref, b_ref, o_ref):
    o_ref[...] = _dot(_silu(c_ref[...]), w_ref[...], precision=HIGHEST) + b_ref[...]


def _modulation(cvec, w_mod, b_mod):
    rows, d = cvec.shape
    n = w_mod.shape[1]
    tn = 1024
    return pl.pallas_call(
        _mod_kernel,
        grid=(n // tn,),
        in_specs=[pl.BlockSpec((rows, d), lambda j: (0, 0)),
                  pl.BlockSpec((d, tn), lambda j: (0, j)),
                  pl.BlockSpec((1, tn), lambda j: (0, j))],
        out_specs=pl.BlockSpec((rows, tn), lambda j: (0, j)),
        out_shape=jax.ShapeDtypeStruct((rows, n), F32),
        compiler_params=_cparams(("arbitrary",)),
        name="adaln_modulation",
    )(cvec, w_mod, b_mod)


_PLAIN, _ROPE, _ROPE_SCALED, _SIGMOID = 0, 1, 2, 3


def _proj_kernel(x_ref, gain_ref, sc_ref, sh_ref, *refs, kinds, rope, n_w):
    w_refs = refs[:n_w]
    wg_ref, wgt_ref, prow_ref, pcol_ref, cos_ref, sin_ref = refs[n_w:n_w + 6]
    out_refs = refs[n_w + 6:]
    groups = [(w, c) for w in w_refs for c in range(0, w.shape[1], MIX_W)]
    n_feat = len(kinds)
    feat_refs = out_refs[:n_feat]
    cols_ref, grow_ref, gl_ref = out_refs[n_feat:]
    tm = x_ref.shape[0]

    x = x_ref[...]
    h = (_rms(x, gain_ref[...]) * (1.0 + sc_ref[...]) + sh_ref[...]).astype(BF16)

    r_i = _iota((CHUNK, CHUNK), 0)
    c_i = _iota((CHUNK, CHUNK), 1)
    lower_incl = (c_i <= r_i).astype(BF16)
    upper_incl = (c_i >= r_i).astype(BF16)

    def prefix_rows(v):
        hi, lo = _split_bf16(v)
        return _dot(lower_incl, hi) + _dot(lower_incl, lo)

    def prefix_lanes(v):
        hi, lo = _split_bf16(v)
        return _dot(hi, upper_incl) + _dot(lo, upper_incl)

    nc = N_HEADS * 8
    pg = _dot(h, wg_ref[...])
    colt = _iota((tm, nc), 1) & 7
    la = jnp.where((colt == 2) | (colt == 3), 0.0, prow_ref[0:1, :] * _softplus(pg + prow_ref[1:2, :]))
    beta = _sigmoid(pg)
    colc = _iota((CHUNK, nc), 1) & 7
    fwd_col = (colc & 1) == 0
    for c in range(tm // CHUNK):
        sl = slice(c * CHUNK, (c + 1) * CHUNK)
        la_c = la[sl]
        pre = prefix_rows(la_c)
        suf = pre[CHUNK - 1:CHUNK, :] - pre + la_c
        g_c = jnp.where(fwd_col, pre, suf)
        rest = jnp.where(fwd_col, suf, pre) - la_c
        vals = jnp.where(colc < 2, g_c, jnp.where(colc < 4, beta[sl], jnp.where(
            colc < 6, jnp.exp(g_c), jnp.exp(rest))))
        for hh in range(N_HEADS):
            cols_ref[hh, sl, :] = vals[:, 8 * hh:8 * hh + 8]
        gl_ref[c:c + 1, :] = jnp.exp(g_c[0:1, :] + rest[0:1, :])

    pgt = _dot_nt(wgt_ref[...], h)
    rowq = _iota((N_HEADS * 8, tm), 0) & 7
    lat = jnp.where(rowq < 2, pcol_ref[:, 0:1] * _softplus(pgt + pcol_ref[:, 1:2]), 0.0)
    rowc = _iota((N_HEADS * 8, CHUNK), 0) & 7
    for c in range(tm // CHUNK):
        sl = slice(c * CHUNK, (c + 1) * CHUNK)
        lat_c = lat[:, sl]
        pre_t = prefix_lanes(lat_c)
        suf_t = pre_t[:, CHUNK - 1:CHUNK] - pre_t + lat_c
        grow_ref[:, :, sl] = jnp.where(rowc == 0, pre_t, suf_t).reshape(N_HEADS, 8, CHUNK)

    if rope:
        cos2 = cos_ref[...]
        sin2 = sin_ref[...]

    for g, kind in enumerate(kinds):
        w_ref, c0 = groups[g]
        p = _dot(h, w_ref[:, c0:c0 + MIX_W])
        if kind == _SIGMOID:
            p = _sigmoid(p)
        elif kind in (_ROPE, _ROPE_SCALED) and rope:
            heads = []
            for hh in range(N_HEADS):
                t = p[:, hh * HEAD_DIM:(hh + 1) * HEAD_DIM]
                heads.append(t * cos2 + pltpu.roll(t, HEAD_DIM // 2, 1) * sin2)
            p = jnp.concatenate(heads, axis=1)
        if kind == _ROPE_SCALED:
            p = p * (HEAD_DIM ** -0.5)
        feat_refs[g][...] = p.astype(feat_refs[g].dtype)


def _in_projection(x, gain, scale, shift, w_mains, w_gab, w_gab_t, prow, pcol, cos2, sin2, kinds, rope):
    b, l, d = x.shape
    tm = min(PROJ_TM, l)
    tiles = l // tm
    n_chunk = tm // CHUNK
    feat_shapes = [jax.ShapeDtypeStruct((b, l, MIX_W), BF16) for _ in kinds]
    feat_specs = [pl.BlockSpec((None, tm, MIX_W), lambda i, j: (i, j, 0)) for _ in kinds]
    out_shape = feat_shapes + [jax.ShapeDtypeStruct((b, N_HEADS, l, 8), F32),
                               jax.ShapeDtypeStruct((b, N_HEADS, 8, l), F32),
                               jax.ShapeDtypeStruct((b, tiles, n_chunk, N_HEADS * 8), F32)]
    out_specs = feat_specs + [pl.BlockSpec((None, N_HEADS, tm, 8), lambda i, j: (i, 0, j, 0)),
                              pl.BlockSpec((None, N_HEADS, 8, tm), lambda i, j: (i, 0, 0, j)),
                              pl.BlockSpec((None, None, n_chunk, N_HEADS * 8), lambda i, j: (i, j, 0, 0))]
    const = functools.partial(pl.BlockSpec, pipeline_mode=pl.Buffered(1))
    in_specs = [
        pl.BlockSpec((None, tm, d), lambda i, j: (i, j, 0)),
        const((1, d), lambda i, j: (0, 0)),
        pl.BlockSpec((None, 1, d), lambda i, j: (i, 0, 0)),
        pl.BlockSpec((None, 1, d), lambda i, j: (i, 0, 0)),
    ] + [const(w.shape, lambda i, j: (0, 0)) for w in w_mains] + [
        const((d, N_HEADS * 8), lambda i, j: (0, 0)),
        const((N_HEADS * 8, d), lambda i, j: (0, 0)),
        const((2, N_HEADS * 8), lambda i, j: (0, 0)),
        const((N_HEADS * 8, 2), lambda i, j: (0, 0)),
        pl.BlockSpec((tm, HEAD_DIM), lambda i, j: (j, 0)),
        pl.BlockSpec((tm, HEAD_DIM), lambda i, j: (j, 0)),
    ]
    outs = pl.pallas_call(
        functools.partial(_proj_kernel, kinds=tuple(kinds), rope=rope, n_w=len(w_mains)),
        grid=(b, tiles),
        in_specs=in_specs,
        out_specs=out_specs,
        out_shape=out_shape,
        compiler_params=_cparams(("arbitrary", "arbitrary")),
        name="in_projection_rope" if rope else "in_projection_ctx",
    )(x, gain, scale, shift, *w_mains, w_gab, w_gab_t, prow, pcol, cos2, sin2)
    feats = outs[:len(kinds)]
    cols, grow, gl = outs[len(kinds):]
    return feats, (cols, grow, gl.reshape(b * (l // CHUNK), N_HEADS * 8))


def _unit_triangular_inverses(mats, lowers):
    r = _iota((CHUNK, CHUNK), 0)
    c = _iota((CHUNK, CHUNK), 1)
    eye = (r == c).astype(F32)
    invs = [eye - jnp.where((r >> 1) == (c >> 1), a, 0.0) for a in mats]
    for level in range(1, int(math.log2(CHUNK))):
        s = 1 << level
        mask = ((r >> (level + 1)) == (c >> (level + 1))) & ((r >> level) != (c >> level))
        invb = [inv.astype(BF16) for inv in invs]
        offs = [jnp.where(mask, a, 0.0).astype(BF16) for a in mats]
        if s < 8:
            half = [_dot(off, ib).astype(BF16) for off, ib in zip(offs, invb)]
            invs = [inv - _dot(ib, hf) for inv, ib, hf in zip(invs, invb, half)]
            continue
        def rows_of(x, lower, moving):
            first = s if (lower == moving) else 0
            return [x[g * 2 * s + first:g * 2 * s + first + s] for g in range(CHUNK // (2 * s))]

        half = [_dot(jnp.concatenate(rows_of(off, lo, True), axis=0), ib).astype(BF16)
                for off, ib, lo in zip(offs, invb, lowers)]
        zero = jnp.zeros((s, CHUNK), BF16)
        full = []
        for hf, lo in zip(half, lowers):
            pieces = []
            for g in range(CHUNK // (2 * s)):
                piece = hf[g * s:(g + 1) * s]
                pieces += [zero, piece] if lo else [piece, zero]
            full.append(jnp.concatenate(pieces, axis=0))
        corr = [_dot(jnp.concatenate(rows_of(ib, lo, True), axis=0), hf) for ib, hf, lo in zip(invb, full, lowers)]
        new = []
        for inv, cr, lo in zip(invs, corr, lowers):
            keep = rows_of(inv, lo, False)
            moved = [m - cr[g * s:(g + 1) * s] for g, m in enumerate(rows_of(inv, lo, True))]
            pieces = []
            for k, m in zip(keep, moved):
                pieces += [k, m] if lo else [m, k]
            new.append(jnp.concatenate(pieces, axis=0))
        invs = new
    return invs


def _mixer_kernel(*refs, seq_len, with_query):
    n_chunk = seq_len // CHUNK
    if with_query:
        (lg_ref, gl_ref, rk_ref, rv_ref, gk_ref, gv_ref, rq_ref, gq_ref, rg_ref, gz_ref,
         cols_ref, grow_ref, cq_ref, ck_ref, cv_ref, s0_ref, gnw_ref, rmsw_ref,
         yret_ref, ygdn_ref,
         qes, oret, ogdn, ubuf, wbuf, pbuf, nbuf, cbuf, kvbuf, rtile, state) = refs
    else:
        (lg_ref, gl_ref, rk_ref, rv_ref, gk_ref, gv_ref,
         cols_ref, grow_ref, ck_ref, cv_ref, s0_ref,
         sfin_ref,
         nbuf, cbuf, kvbuf, rtile, state) = refs
    bi = pl.program_id(0)
    hi = pl.program_id(1)

    row = _iota((CHUNK, CHUNK), 0)
    colm = _iota((CHUNK, CHUNK), 1)
    rowf = row.astype(F32)
    colf = colm.astype(F32)

    edge_row = _iota((F32_TILE_ROWS, HEAD_DIM), 0)

    def conv_chunk(src_ref, w_ref, n):
        s = pl.multiple_of(n * CHUNK, CHUNK)
        x = src_ref[pl.ds(s, CHUNK), :].astype(F32)
        ps = pl.multiple_of(jnp.maximum(s - BF16_TILE_ROWS, 0), BF16_TILE_ROWS)
        ns = pl.multiple_of(jnp.minimum(s + CHUNK, seq_len - BF16_TILE_ROWS), BF16_TILE_ROWS)
        prev_row = src_ref[pl.ds(ps, BF16_TILE_ROWS), :].astype(F32)[BF16_TILE_ROWS - 1:BF16_TILE_ROWS, :]
        next_row = src_ref[pl.ds(ns, BF16_TILE_ROWS), :].astype(F32)[0:1, :]
        prev_row = prev_row * jnp.where(n > 0, 1.0, 0.0)
        next_row = next_row * jnp.where(n < n_chunk - 1, 1.0, 0.0)
        down, up, rt = pltpu.roll(x, 1, 0), pltpu.roll(x, CHUNK - 1, 0), F32_TILE_ROWS
        first = jnp.where(edge_row == 0, jnp.broadcast_to(prev_row, (rt, HEAD_DIM)), down[:rt])
        last = jnp.where(edge_row == rt - 1, jnp.broadcast_to(next_row, (rt, HEAD_DIM)), up[CHUNK - rt:])
        xp = jnp.concatenate([first, down[rt:]], axis=0)
        xn = jnp.concatenate([up[:CHUNK - rt], last], axis=0)
        taps = 0.5 * w_ref[...]
        h = taps[0:1, :] * xp + taps[1:2, :] * x + taps[2:3, :] * xn
        return h * jnp.tanh(h) + h

    def l2n(v):
        return v * lax.rsqrt(jnp.sum(v * v, axis=-1, keepdims=True) + EPS)

    for d in range(2):
        lg = lg_ref[d, hi]
        if d == 0:
            dist, pos_q, pos_k = rowf - colf, rowf + 1.0, (CHUNK - 1.0) - rowf
        else:
            dist, pos_q, pos_k = colf - rowf, CHUNK - rowf, rowf
        rtile[4 * d + 0] = jnp.where(dist >= 0, jnp.exp(lg * jnp.maximum(dist, 0.0)), 0.0)
        rtile[4 * d + 1] = jnp.exp(lg * pos_q)
        rtile[4 * d + 2] = jnp.exp(lg * pos_k)
        rtile[4 * d + 3] = jnp.exp(lg * jnp.full((CHUNK, CHUNK), float(CHUNK), F32))

    state[...] = s0_ref[...]

    def bcast_col(cols, j):
        return jnp.broadcast_to(cols[:, j:j + 1], (CHUNK, CHUNK))

    pre_chunks = min(8, n_chunk)

    def prepass(m, carry):
        chunks = [m * pre_chunks + j for j in range(pre_chunks)]
        chunk_cs = [pl.ds(pl.multiple_of(n * CHUNK, CHUNK), CHUNK) for n in chunks]
        conv_k, conv_v, conv_q = [], [], []
        for n, cs in zip(chunks, chunk_cs):
            conv_k.append(l2n(conv_chunk(gk_ref, ck_ref, n)).astype(BF16))
            conv_v.append(conv_chunk(gv_ref, cv_ref, n).astype(BF16))
            if with_query:
                conv_q.append((l2n(conv_chunk(gq_ref, cq_ref, n)) * (HEAD_DIM ** -0.5)).astype(BF16))
                ogdn[cs, :] = jnp.zeros((CHUNK, HEAD_DIM), F32)

        jobs = [(j, d) for j in range(pre_chunks) for d in range(2)]
        css = [chunk_cs[j] for j, _ in jobs]
        slots = [d * n_chunk + chunks[j] for j, d in jobs]
        dirs = [d for _, d in jobs]
        ks = [rk_ref[cs, :] for cs in css]
        vs = [rv_ref[cs, :] for cs in css]
        if with_query:
            scs = [(_dot_nt(rq_ref[cs, :], k) * rtile[4 * d + 0]).astype(BF16) for cs, k, d in zip(css, ks, dirs)]
            outs = [_dot(sc, v) for sc, v in zip(scs, vs)]
            for j, cs in enumerate(chunk_cs):
                oret[cs, :] = outs[2 * j] + outs[2 * j + 1]
        kvs = [_dot_tn((k.astype(F32) * rtile[4 * d + 2]).astype(BF16), v) for k, v, d in zip(ks, vs, dirs)]
        for slot, kv in zip(slots, kvs):
            kvbuf[slot] = kv
        ks = [conv_k[j] for j, _ in jobs]
        vs = [conv_v[j] for j, _ in jobs]
        colss = [cols_ref[cs, :] for cs in css]
        betas = [bcast_col(cols, 2 + d) for cols, d in zip(colss, dirs)]
        incls = [(row >= colm) if d == 0 else (row <= colm) for d in dirs]
        stricts = [(row > colm) if d == 0 else (row < colm) for d in dirs]
        decs = [jnp.exp(bcast_col(cols, d) - jnp.broadcast_to(grow_ref[d:d + 1, cs], (CHUNK, CHUNK)))
                for cols, d, cs in zip(colss, dirs, css)]
        kks = [_dot_nt(k, k) for k in ks]
        mats = [kk * beta * jnp.where(strict, dec, 0.0) for kk, beta, strict, dec in zip(kks, betas, stricts, decs)]
        tinvs = [t.astype(BF16) for t in _unit_triangular_inverses(mats, [d == 0 for d in dirs])]
        rhs = [jnp.concatenate([(beta * v.astype(F32)).astype(BF16),
                                (beta * bcast_col(cols, 4 + d) * k.astype(F32)).astype(BF16)], axis=1)
               for beta, v, cols, d, k in zip(betas, vs, colss, dirs, ks)]
        uws = [_dot(t, r) for t, r in zip(tinvs, rhs)]
        us = [uw[:, :HEAD_DIM] for uw in uws]
        ws = [uw[:, HEAD_DIM:].astype(BF16) for uw in uws]
        kts = [(k.astype(F32) * bcast_col(cols, 6 + d)).astype(BF16) for k, cols, d in zip(ks, colss, dirs)]
        ncs = [_dot_tn(kt, uw.astype(BF16)) for kt, uw in zip(kts, uws)]
        for slot, nc_mat in zip(slots, ncs):
            cbuf[slot] = nc_mat[:, :HEAD_DIM]
            nbuf[slot] = nc_mat[:, HEAD_DIM:].astype(BF16)
        if with_query:
            for slot, u, w in zip(slots, us, ws):
                ubuf[slot] = u.astype(BF16)
                wbuf[slot] = w
            qks = [_dot_nt(conv_q[j], k) for (j, _), k in zip(jobs, ks)]
            for slot, qk, incl, dec in zip(slots, qks, incls, decs):
                pbuf[slot] = (qk * jnp.where(incl, dec, 0.0)).astype(BF16)
            for (j, d), cs, cols in zip(jobs, css, colss):
                qes[d, cs, :] = (conv_q[j].astype(F32) * bcast_col(cols, 4 + d)).astype(BF16)
        return carry

    lax.fori_loop(0, n_chunk // pre_chunks, prepass, 0)

    def finish(cs):
        ro = oret[cs, :]
        rc = ro - jnp.mean(ro, axis=-1, keepdims=True)
        ry = rc * lax.rsqrt(jnp.mean(rc * rc, axis=-1, keepdims=True) + EPS)
        yret_ref[cs, :] = (ry * gnw_ref[...] * _silu(rg_ref[cs, :].astype(F32))).astype(BF16)
        go = ogdn[cs, :]
        gy = go * lax.rsqrt(jnp.mean(go * go, axis=-1, keepdims=True) + EPS)
        ygdn_ref[cs, :] = (gy * rmsw_ref[...] * _silu(gz_ref[cs, :].astype(F32))).astype(BF16)

    def chunk_ids(n):
        nds = [n, n_chunk - 1 - n]
        css = [pl.ds(nd * CHUNK if isinstance(nd, int) else pl.multiple_of(nd * CHUNK, CHUNK), CHUNK) for nd in nds]
        slots = [d * n_chunk + nd for d, nd in enumerate(nds)]
        return nds, css, slots

    def late_outputs(n, vnb, finishing):
        _, css, slots = chunk_ids(n)
        for d in range(2):
            ogdn[css[d], :] += _dot(pbuf[slots[d]], vnb[d])
            if finishing:
                finish(css[d])

    def scan_step(n, vnb_prev, has_prev, finish_prev):
        nds, css, slots = chunk_ids(n)
        ret_st = [state[d] for d in range(2)]
        gdn_st = [state[2 + d] for d in range(2)]
        gdn_stb = [st.astype(BF16) for st in gdn_st]
        shrink = [_dot(nbuf[slot], stb) for slot, stb in zip(slots, gdn_stb)]
        for d in range(2):
            state[2 + d] = gl_ref[bi * n_chunk + nds[d], 8 * hi + d] * gdn_st[d] - shrink[d] + cbuf[slots[d]]
            state[d] = rtile[4 * d + 3] * ret_st[d] + kvbuf[slots[d]]
        if not with_query:
            return vnb_prev
        vnb = tuple((ubuf[slot].astype(F32) - _dot(wbuf[slot], stb)).astype(BF16)
                    for slot, stb in zip(slots, gdn_stb))
        for d in range(2):
            oret[css[d], :] += _dot(rq_ref[css[d], :], ret_st[d].astype(BF16)) * rtile[4 * d + 1]
            ogdn[css[d], :] += _dot(qes[d, css[d], :], gdn_stb[d])
        if has_prev:
            late_outputs(n - 1, vnb_prev, finish_prev)
        return vnb

    if with_query:
        half = n_chunk // 2
        zero = jnp.zeros((CHUNK, HEAD_DIM), BF16)
        vnb = scan_step(0, (zero, zero), has_prev=False, finish_prev=False)
        vnb = lax.fori_loop(1, half + 1, functools.partial(scan_step, has_prev=True, finish_prev=False), vnb)
        vnb = lax.fori_loop(half + 1, n_chunk, functools.partial(scan_step, has_prev=True, finish_prev=True), vnb)
        late_outputs(n_chunk - 1, vnb, True)
    else:
        lax.fori_loop(0, n_chunk, functools.partial(scan_step, has_prev=False, finish_prev=False), 0)
        sfin_ref[...] = state[...]


def _mixers(lg, gl, feats, dec, conv, s0, gnw, rmsw, with_query):
    cols, grow = dec
    b, l, _ = feats["rk"].shape
    n_slot = 2 * (l // CHUNK)
    assert (l // CHUNK) % min(8, l // CHUNK) == 0
    smem = pl.BlockSpec(memory_space=pltpu.SMEM)
    head = pl.BlockSpec((None, l, HEAD_DIM), lambda i, j: (i, 0, j))
    colspec = pl.BlockSpec((None, None, l, 8), lambda i, j: (i, j, 0, 0))
    rowspec = pl.BlockSpec((None, None, 8, l), lambda i, j: (i, j, 0, 0))
    tile_f32 = pltpu.VMEM((n_slot, CHUNK, CHUNK), F32)
    tile_bf16 = pltpu.VMEM((n_slot, CHUNK, CHUNK), BF16)
    st_spec = pl.BlockSpec((None, None, 4, HEAD_DIM, HEAD_DIM), lambda i, j: (i, j, 0, 0, 0))

    def conv_spec(which):
        return pl.BlockSpec((SHORT_CONV, HEAD_DIM), lambda i, j, w=which: (0, w * N_HEADS + j))

    def gain_spec():
        return pl.BlockSpec((1, HEAD_DIM), lambda i, j: (0, j))

    if with_query:
        args = [lg, gl, feats["rk"], feats["rv"], feats["gk"], feats["gv"], feats["rq"], feats["gq"], feats["rg"],
                feats["gz"], cols, grow, conv, conv, conv, s0, gnw, rmsw]
        in_specs = [smem, smem] + [head] * 8 + [colspec, rowspec, conv_spec(0), conv_spec(1), conv_spec(2),
                                                st_spec, gain_spec(), gain_spec()]
        out_shape = [jax.ShapeDtypeStruct((b, l, MIX_W), BF16)] * 2
        out_specs = [head, head]
        scratch = ([pltpu.VMEM((2, l, HEAD_DIM), BF16)] + [pltpu.VMEM((l, HEAD_DIM), F32)] * 2
                   + [tile_bf16, tile_bf16, tile_bf16, tile_bf16, tile_f32, tile_f32])
    else:
        args = [lg, gl, feats["rk"], feats["rv"], feats["gk"], feats["gv"], cols, grow, conv, conv, s0]
        in_specs = [smem, smem] + [head] * 4 + [colspec, rowspec, conv_spec(1), conv_spec(2), st_spec]
        out_shape = jax.ShapeDtypeStruct((b, N_HEADS, 4, HEAD_DIM, HEAD_DIM), F32)
        out_specs = st_spec
        scratch = [tile_bf16, tile_f32, tile_f32]
    scratch = scratch + [pltpu.VMEM((8, CHUNK, CHUNK), F32), pltpu.VMEM((4, HEAD_DIM, HEAD_DIM), F32)]
    return pl.pallas_call(
        functools.partial(_mixer_kernel, seq_len=l, with_query=with_query),
        grid=(b, N_HEADS),
        in_specs=in_specs,
        out_specs=out_specs,
        out_shape=out_shape,
        scratch_shapes=scratch,
        compiler_params=_cparams(("arbitrary", "arbitrary")),
        name="mixers_latent" if with_query else "mixers_context",
    )(*args)


def _split_bf16(v):
    hi = v.astype(BF16)
    return hi, (v - hi.astype(F32)).astype(BF16)


def _route(h2, wrt_ref, bias_ref, cand_ref, sel_ref, wd_ref):
    tm = h2.shape[0]
    h_hi, h_lo = _split_bf16(h2)
    w_hi, w_lo = _split_bf16(wrt_ref[...])
    logits = _dot_nt(w_hi, h_hi) + (_dot_nt(w_hi, h_lo) + _dot_nt(w_lo, h_hi))
    scores = _sigmoid(logits)
    biased = scores + bias_ref[...]
    neg_inf = float("-inf")
    sub = _iota((GROUP_SIZE, tm), 0).astype(F32)
    group_score = []
    for g in range(N_GROUPS):
        blk = biased[g * GROUP_SIZE:(g + 1) * GROUP_SIZE, :]
        m1 = jnp.max(blk, axis=0, keepdims=True)
        first = jnp.min(jnp.where(blk == m1, sub, float(GROUP_SIZE)), axis=0, keepdims=True)
        m2 = jnp.max(jnp.where(sub == first, neg_inf, blk), axis=0, keepdims=True)
        group_score.append(m1 + m2)
    for g in range(N_GROUPS):
        ahead = jnp.zeros((1, tm), I32)
        for g2 in range(N_GROUPS):
            if g2 == g:
                continue
            before = (group_score[g2] > group_score[g])
            if g2 < g:
                before = before | (group_score[g2] == group_score[g])
            ahead = ahead + before.astype(I32)
        keep = jnp.broadcast_to(ahead, (GROUP_SIZE, tm)) < TOPK_GROUPS
        cand_ref[g * GROUP_SIZE:(g + 1) * GROUP_SIZE, :] = jnp.where(
            keep, biased[g * GROUP_SIZE:(g + 1) * GROUP_SIZE, :], neg_inf)
    work = cand_ref[...]
    eidx = _iota((N_EXPERTS, tm), 0).astype(F32)
    sel = jnp.zeros((N_EXPERTS, tm), jnp.bool_)
    for _ in range(TOP_K):
        best = jnp.max(work, axis=0, keepdims=True)
        first = jnp.min(jnp.where(work == best, eidx, float(N_EXPERTS)), axis=0, keepdims=True)
        pick = eidx == first
        sel = sel | pick
        work = jnp.where(pick, neg_inf, work)
    picked = jnp.where(sel, scores, 0.0)
    wsum = jnp.sum(picked, axis=0, keepdims=True)
    sel_ref[...] = sel.astype(F32)
    wd_ref[...] = picked / wsum * ROUTED_SCALE


def _mixout_kernel(x_ref, yr_ref, yg_ref, g0_ref, g1_ref, g2_ref, g3_ref, wr_ref, wg_ref, wo_ref,
                   npost_ref, gate1_ref, nffn_ref, sc2_ref, sh2_ref, wrt_ref, bias_ref,
                   x1_ref, h2_ref, sel_ref, wd_ref, cand_ref):
    r = _dot(yr_ref[...], wr_ref[...])
    g = _dot(yg_ref[...], wg_ref[...])
    half = r.shape[1] // 2
    merged = jnp.concatenate(
        [g0_ref[...].astype(F32) * r[:, :half] + g2_ref[...].astype(F32) * g[:, :half],
         g1_ref[...].astype(F32) * r[:, half:] + g3_ref[...].astype(F32) * g[:, half:]], axis=1)
    mo = _dot(merged.astype(BF16), wo_ref[...])
    x1 = x_ref[...] + gate1_ref[...] * _rms(mo, npost_ref[...])
    x1_ref[...] = x1
    h2 = _rms(x1, nffn_ref[...]) * (1.0 + sc2_ref[...]) + sh2_ref[...]
    h2_ref[...] = h2.astype(BF16)
    _route(h2, wrt_ref, bias_ref, cand_ref, sel_ref, wd_ref)


def _mixer_out(x, yr, yg, gates, wr, wg, wo, npost, gate1, nffn, sc2, sh2, wrt, bias):
    b, l, d = x.shape
    tm = min(MIXOUT_TM, l)
    tiles = l // tm
    tok = lambda w: pl.BlockSpec((None, tm, w), lambda i, j: (i, j, 0))
    const = functools.partial(pl.BlockSpec, pipeline_mode=pl.Buffered(1))
    vec = lambda: const((1, d), lambda i, j: (0, 0))
    bvec = lambda: pl.BlockSpec((None, 1, d), lambda i, j: (i, 0, 0))
    emap = pl.BlockSpec((N_EXPERTS, tm), lambda i, j: (0, i * tiles + j))
    in_specs = ([tok(d), tok(MIX_W), tok(MIX_W)] + [tok(MIX_W)] * 4
                + [const((MIX_W, d), lambda i, j: (0, 0)), const((MIX_W, d), lambda i, j: (0, 0)),
                   const((d, d), lambda i, j: (0, 0)),
                   vec(), bvec(), vec(), bvec(), bvec(),
                   const((N_EXPERTS, d), lambda i, j: (0, 0)), const((N_EXPERTS, 1), lambda i, j: (0, 0))])
    return pl.pallas_call(
        _mixout_kernel,
        grid=(b, tiles),
        in_specs=in_specs,
        out_specs=[tok(d), tok(d), emap, emap],
        out_shape=[jax.ShapeDtypeStruct((b, l, d), F32), jax.ShapeDtypeStruct((b, l, d), BF16),
                   jax.ShapeDtypeStruct((N_EXPERTS, b * l), F32), jax.ShapeDtypeStruct((N_EXPERTS, b * l), F32)],
        scratch_shapes=[pltpu.VMEM((N_EXPERTS, tm), F32)],
        compiler_params=_cparams(("arbitrary", "arbitrary")),
        name="mixer_out_router",
    )(x, yr, yg, *gates, wr, wg, wo, npost, gate1, nffn, sc2, sh2, wrt, bias)


def _tile_positions(sel):
    tm = sel.shape[1]
    selb = sel.astype(BF16)
    earlier = (_iota((tm, tm), 0) < _iota((tm, tm), 1)).astype(BF16)
    rank = _dot(selb, earlier)
    cnt = _dot(selb, jnp.ones((tm, tm), BF16))
    nmb = jnp.floor((cnt + (MB_ROWS - 1)) * (1.0 / MB_ROWS))
    below = (_iota((N_EXPERTS, N_EXPERTS), 1) < _iota((N_EXPERTS, N_EXPERTS), 0)).astype(BF16)
    offmb = _dot(below, nmb.astype(BF16))
    rank_hi = jnp.floor(rank * (1.0 / MB_ROWS))
    hi = jnp.where(sel > 0.0, offmb + rank_hi, NO_SLOT)
    lo = rank - rank_hi * MB_ROWS
    return hi, lo


def _micro_block_keys(sel, per_pair=()):
    tm = sel.shape[1]
    hi, lo = _tile_positions(sel)
    cnt_row = _dot_nt(jnp.ones((8, tm), BF16), sel.astype(BF16))
    nmb_row = jnp.floor((cnt_row + (MB_ROWS - 1)) * (1.0 / MB_ROWS))
    before = (_iota((N_EXPERTS, N_EXPERTS), 0) < _iota((N_EXPERTS, N_EXPERTS), 1)).astype(BF16)
    start = _dot(nmb_row.astype(BF16), before)[0:1, :]
    end = start + nmb_row[0:1, :]
    m_e = _iota((_TILE_MB, N_EXPERTS), 0).astype(F32)
    owner = ((m_e >= start) & (m_e < end)).astype(BF16)
    cols = [hi.astype(BF16), lo.astype(BF16)] + [v.astype(BF16) for v in per_pair]
    got = _dot(owner, jnp.concatenate(cols, axis=1))
    m_t = _iota((_TILE_MB, tm), 0).astype(F32)
    used = jnp.sum(nmb_row[0:1, :])
    key = jnp.where((got[:, :tm] == m_t) & (m_t < used), got[:, tm:2 * tm], -1.0)
    return key, cnt_row, used, [got[:, (2 + k) * tm:(3 + k) * tm] for k in range(len(per_pair))]


def _dispatch_kernel(h2_ref, sel_ref, xs_ref, cnt_ref, key_ref, hot_ref):
    last = pl.num_programs(0) - 1

    @pl.when(pl.program_id(0) < last)
    def _():
        _dispatch_tile(h2_ref, sel_ref, xs_ref, cnt_ref, key_ref, hot_ref)

    @pl.when(pl.program_id(0) == last)
    def _():
        xs_ref[...] = jnp.zeros_like(xs_ref)


def _dispatch_tile(h2_ref, sel_ref, xs_ref, cnt_ref, key_ref, hot_ref):
    tm = h2_ref.shape[0]
    key, cnt_row, used, _ = _micro_block_keys(sel_ref[...])
    key_ref[...] = key
    cnt_ref[...] = cnt_row
    x = h2_ref[...]
    rch = ROW_CHUNK
    in_block = _iota((MB_ROWS, tm), 0).astype(F32)

    def one_hot(rc):
        for m in range(rc * rch // MB_ROWS, (rc + 1) * rch // MB_ROWS):
            hit = jnp.broadcast_to(key_ref[m:m + 1, :], (MB_ROWS, tm)) == in_block
            hot_ref[m * MB_ROWS:(m + 1) * MB_ROWS, :] = jnp.where(hit, 1.0, 0.0).astype(BF16)

    n_rc = TILE_ROWS // rch
    n_sure = n_rc - TAIL_CHUNKS
    one_hot(0)
    for rc in range(n_sure):
        if rc + 1 < n_sure:
            one_hot(rc + 1)
        xs_ref[rc * rch:(rc + 1) * rch, :] = _dot(hot_ref[rc * rch:(rc + 1) * rch, :], x).astype(BF16)
    tail_used = used * MB_ROWS > n_sure * rch

    @pl.when(tail_used)
    def _():
        for rc in range(n_sure, n_rc):
            one_hot(rc)
            xs_ref[rc * rch:(rc + 1) * rch, :] = _dot(hot_ref[rc * rch:(rc + 1) * rch, :], x).astype(BF16)

    @pl.when(jnp.logical_not(tail_used))
    def _():
        xs_ref[n_sure * rch:, :] = jnp.zeros(((n_rc - n_sure) * rch, x.shape[1]), BF16)


def _dispatch(h2, sel_t):
    t, d = h2.shape
    tm = MOE_TM
    nt = t // tm
    return pl.pallas_call(
        _dispatch_kernel,
        grid=(nt + 1,),
        in_specs=[pl.BlockSpec((tm, d), lambda i: (jnp.minimum(i, nt - 1), 0)),
                  pl.BlockSpec((N_EXPERTS, tm), lambda i: (0, jnp.minimum(i, nt - 1)))],
        out_specs=[pl.BlockSpec((TILE_ROWS, d), lambda i: (i, 0)),
                   pl.BlockSpec((None, 8, N_EXPERTS), lambda i: (jnp.minimum(i, nt - 1), 0, 0))],
        out_shape=[jax.ShapeDtypeStruct((nt * TILE_ROWS + (EXP_OUT + 1) * EXP_BM, d), BF16),
                   jax.ShapeDtypeStruct((nt, 8, N_EXPERTS), F32)],
        scratch_shapes=[pltpu.VMEM((_TILE_MB, tm), F32), pltpu.VMEM((TILE_ROWS, tm), BF16)],
        compiler_params=_cparams(("arbitrary",)),
        name="moe_dispatch",
    )(h2, sel_t)


def _expert_plan(cnt, n_blk):
    nt = cnt.shape[0]
    nmb = (cnt + (MB_ROWS - 1)) // MB_ROWS
    offmb = jnp.cumsum(nmb, axis=1) - nmb
    per_e = nmb.T
    incl = jnp.cumsum(per_e, axis=1)
    excl = incl - per_e
    tot = incl[:, -1]
    nb = (tot + (EXP_MB - 1)) // EXP_MB
    bend = jnp.cumsum(nb)
    bstart = bend - nb
    b = jnp.arange(n_blk, dtype=I32)
    valid = b < bend[-1]
    last = jnp.maximum(bend[-1] - 1, 0)
    bq = jnp.where(valid, b, last)
    blk_e = jnp.minimum(jnp.sum((bend[None, :] <= bq[:, None]).astype(I32), axis=1), N_EXPERTS - 1)
    onehot_e = (blk_e[:, None] == jnp.arange(N_EXPERTS, dtype=I32)[None, :]).astype(I32)
    pick = lambda table: jnp.dot(onehot_e.astype(F32), table.astype(F32), precision=HIGHEST).astype(I32)
    bstart_b = jnp.sum(onehot_e * bstart[None, :], axis=1)
    tot_b = jnp.sum(onehot_e * tot[None, :], axis=1)
    p0 = (bq - bstart_b) * EXP_MB
    blk_nmb = jnp.where(valid, jnp.clip(tot_b - p0, 0, EXP_MB), 0).astype(I32)
    p = p0[:, None] + jnp.arange(EXP_MB, dtype=I32)[None, :]
    incl_b = pick(incl)
    passed = (incl_b[:, None, :] <= p[:, :, None]).astype(I32)
    tile = jnp.minimum(jnp.sum(passed, axis=2), nt - 1)
    shift_b = pick(offmb.T - excl)
    step_b = jnp.concatenate([shift_b[:, 1:] - shift_b[:, :-1], jnp.zeros_like(shift_b[:, :1])], axis=1)
    shift = shift_b[:, :1] + jnp.sum(passed * step_b[:, None, :], axis=2)
    where = tile * _TILE_MB + p + shift
    j = jnp.arange(EXP_MB, dtype=I32)[None, :]
    used = j < blk_nmb[:, None]
    spare = nt * _TILE_MB + (b[:, None] % EXP_OUT) * EXP_MB + j
    zeros_mb = nt * _TILE_MB + EXP_OUT * EXP_MB + j
    src = jnp.where(used, where, jnp.where(valid[:, None], where[:, :1], zeros_mb))
    dst = jnp.where(used, where, spare)
    dst = jnp.concatenate([nt * _TILE_MB + (EXP_OUT - 1) * EXP_MB + j, dst], axis=0)
    return bstart.astype(I32), nb.astype(I32), src.reshape(-1).astype(I32), dst.reshape(-1).astype(I32)


def _expert_kernel(bstart_ref, nb_ref, src_ref, dst_ref, xs_hbm, wg_ref, wu_ref, wd_ref, ys_hbm,
                   xbuf, ybuf, wgub, wdb, in_sem, out_sem):
    e = pl.program_id(0)
    n_in = xbuf.shape[0]
    n_out = ybuf.shape[0]

    def gather_copy(blk, slot, j):
        rows = pl.ds(pl.multiple_of(src_ref[blk * EXP_MB + j] * MB_ROWS, MB_ROWS), MB_ROWS)
        return pltpu.make_async_copy(xs_hbm.at[rows, :], xbuf.at[slot, j * MB_ROWS:(j + 1) * MB_ROWS, :],
                                     in_sem.at[slot])

    def scatter_copy(blk, slot, j):
        rows = pl.ds(pl.multiple_of(dst_ref[(blk + 1) * EXP_MB + j] * MB_ROWS, MB_ROWS), MB_ROWS)
        return pltpu.make_async_copy(ybuf.at[slot, j * MB_ROWS:(j + 1) * MB_ROWS, :], ys_hbm.at[rows, :],
                                     out_sem.at[slot])

    def start_gather(blk):
        for j in range(EXP_MB):
            gather_copy(blk, lax.rem(blk, n_in), j).start()

    def wait_gather(blk):
        for j in range(EXP_MB):
            gather_copy(blk, lax.rem(blk, n_in), j).wait()

    def start_scatter(blk):
        for j in range(EXP_MB):
            scatter_copy(blk, lax.rem(blk + n_out, n_out), j).start()

    def wait_scatter(slot):
        for j in range(EXP_MB):
            scatter_copy(0, slot, j).wait()

    @pl.when(e == 0)
    def _():
        ybuf[...] = jnp.zeros_like(ybuf)
        for ahead in range(EXP_LEAD):
            start_gather(ahead)
        spare_row0 = ys_hbm.shape[0] - (n_out + 1) * EXP_BM
        for slot in range(n_out - 1):
            for j in range(EXP_MB):
                spare = pl.ds(spare_row0 + (slot * EXP_MB + j) * MB_ROWS, MB_ROWS)
                pltpu.make_async_copy(ybuf.at[slot, j * MB_ROWS:(j + 1) * MB_ROWS, :], ys_hbm.at[spare, :],
                                      out_sem.at[slot]).start()

    de = wg_ref.shape[1]
    wgub[:, :de] = wg_ref[...].astype(BF16)
    wgub[:, de:] = wu_ref[...].astype(BF16)
    wdb[...] = wd_ref[...].astype(BF16)

    def block(b, carry):
        slot = lax.rem(b, n_out)
        wait_gather(b)
        wait_scatter(slot)
        x = xbuf[lax.rem(b, n_in)]
        gu = _dot(x, wgub[...])
        act = (_silu(gu[:, :de]) * gu[:, de:]).astype(BF16)
        start_gather(b + EXP_LEAD)
        start_scatter(b - 1)
        ybuf[slot] = _dot(act, wdb[...]).astype(BF16)
        return carry

    first = bstart_ref[e]
    lax.fori_loop(first, first + nb_ref[e], block, 0)

    @pl.when(e == pl.num_programs(0) - 1)
    def _():
        total = first + nb_ref[e]
        start_scatter(total - 1)
        for ahead in range(EXP_LEAD):
            wait_gather(total + ahead)
        for slot in range(n_out):
            wait_scatter(slot)


def _expert_ffn(xs, bstart, nb, src, dst, wg, wu, wd):
    rows, d = xs.shape
    de = wg.shape[2]
    grid_spec = pltpu.PrefetchScalarGridSpec(
        num_scalar_prefetch=4,
        grid=(wg.shape[0],),
        in_specs=[pl.BlockSpec(memory_space=pl.ANY),
                  pl.BlockSpec((None, d, de), lambda i, bs, bn, sr, ds: (i, 0, 0)),
                  pl.BlockSpec((None, d, de), lambda i, bs, bn, sr, ds: (i, 0, 0)),
                  pl.BlockSpec((None, de, d), lambda i, bs, bn, sr, ds: (i, 0, 0))],
        out_specs=pl.BlockSpec(memory_space=pl.ANY),
        scratch_shapes=[pltpu.VMEM((EXP_LEAD + 1, EXP_BM, d), BF16), pltpu.VMEM((EXP_OUT, EXP_BM, d), BF16),
                        pltpu.VMEM((d, 2 * de), BF16), pltpu.VMEM((de, d), BF16),
                        pltpu.SemaphoreType.DMA((EXP_LEAD + 1,)), pltpu.SemaphoreType.DMA((EXP_OUT,))],
    )
    return pl.pallas_call(
        _expert_kernel,
        grid_spec=grid_spec,
        out_shape=jax.ShapeDtypeStruct((rows, d), BF16),
        input_output_aliases={4: 0},
        compiler_params=_cparams(("arbitrary",)),
        name="moe_experts",
    )(bstart, nb, src, dst, xs, wg, wu, wd)


def _combine_kernel(ys_ref, sel_ref, wd_ref, h2_ref, x1_ref, wsg_ref, wsu_ref, wsd_ref, npost_ref, gate2_ref,
                    o_ref, key_ref, wmb_ref, c_ref, acc_ref):
    tm = h2_ref.shape[0]
    key, _, used, (wmb,) = _micro_block_keys(sel_ref[...], (wd_ref[...],))
    key_ref[...] = key
    wmb_ref[...] = wmb
    in_block = _iota((MB_ROWS, tm), 0).astype(F32)

    def build(lo_row, hi_row):
        for m in range(lo_row // MB_ROWS, hi_row // MB_ROWS):
            hit = jnp.broadcast_to(key_ref[m:m + 1, :], (MB_ROWS, tm)) == in_block
            weight = jnp.broadcast_to(wmb_ref[m:m + 1, :], (MB_ROWS, tm))
            c_ref[m * MB_ROWS:(m + 1) * MB_ROWS, :] = jnp.where(hit, weight, 0.0).astype(BF16)

    sure = TILE_ROWS - TAIL_CHUNKS * ROW_CHUNK
    build(0, sure)
    h2 = h2_ref[...]
    shared = _dot((_silu(_dot(h2, wsg_ref[...])) * _dot(h2, wsu_ref[...])).astype(BF16), wsd_ref[...])
    acc_ref[...] = shared + _dot_tn(c_ref[:sure, :], ys_ref[:sure, :])

    @pl.when(used * MB_ROWS > sure)
    def _():
        build(sure, TILE_ROWS)
        acc_ref[...] += _dot_tn(c_ref[sure:, :], ys_ref[sure:, :])

    o_ref[...] = x1_ref[...] + gate2_ref[...] * _rms(acc_ref[...], npost_ref[...])


def _combine(ys, sel_t, wd_t, h2, x1, wsg, wsu, wsd, npost, gate2, seq_len):
    t, d = h2.shape
    tm = MOE_TM
    nt = t // tm
    per_seq = seq_len // tm
    ds = wsg.shape[1]
    const = functools.partial(pl.BlockSpec, pipeline_mode=pl.Buffered(1))
    emap = pl.BlockSpec((N_EXPERTS, tm), lambda i: (0, i))
    return pl.pallas_call(
        _combine_kernel,
        grid=(nt,),
        in_specs=[pl.BlockSpec((TILE_ROWS, d), lambda i: (i, 0)), emap, emap,
                  pl.BlockSpec((tm, d), lambda i: (i, 0)), pl.BlockSpec((tm, d), lambda i: (i, 0)),
                  const((d, ds), lambda i: (0, 0)), const((d, ds), lambda i: (0, 0)), const((ds, d), lambda i: (0, 0)),
                  const((1, d), lambda i: (0, 0)),
                  pl.BlockSpec((None, 1, d), lambda i: (i // per_seq, 0, 0))],
        out_specs=pl.BlockSpec((tm, d), lambda i: (i, 0)),
        out_shape=jax.ShapeDtypeStruct((t, d), F32),
        scratch_shapes=[pltpu.VMEM((_TILE_MB, tm), F32), pltpu.VMEM((_TILE_MB, tm), F32),
                        pltpu.VMEM((TILE_ROWS, tm), BF16), pltpu.VMEM((tm, d), F32)],
        compiler_params=_cparams(("arbitrary",)),
        name="moe_combine",
    )(ys, sel_t, wd_t, h2, x1, wsg, wsu, wsd, npost, gate2)


def _rope_tables(n):
    rows = n // GRID_W
    pos_r = np.repeat(np.arange(rows, dtype=np.float32), GRID_W)
    pos_c = np.tile(np.arange(GRID_W, dtype=np.float32), rows)
    n_freq = HEAD_DIM // 4
    inv = (np.float32(ROPE_BASE) ** (-np.arange(n_freq, dtype=np.float32) / np.float32(n_freq))).astype(np.float32)
    ang = np.concatenate([pos_r[:, None] * inv, pos_c[:, None] * inv], axis=-1)
    cos, sin = np.cos(ang), np.sin(ang)
    return jnp.asarray(np.concatenate([cos, cos], axis=-1)), jnp.asarray(np.concatenate([-sin, sin], axis=-1))


def kernel(x, c, ctx, c_ctx, w_mod, b_mod, norm_mix_pre, norm_mix_post, norm_ffn_pre, norm_ffn_post, w_in, gdn_conv, ret_log_decay, gdn_a_log, gdn_dt_bias, ret_gn_w, gdn_norm_w, w_ret_out, w_gdn_out, w_o, w_router, router_bias, w_gate, w_up, w_down, w_sh_gate, w_sh_up, w_sh_down):
    b, n, d = x.shape
    depth = w_mod.shape[0]
    assert depth == 1, "single-layer block"
    assert all(n % min(tile, n) == 0 for tile in (PROJ_TM, MIXOUT_TM, MOE_TM)) and n % MOE_TM == 0
    assert ctx.shape[1] % CHUNK == 0 and (EXP_OUT + 1) * EXP_BM <= TILE_ROWS
    assert _TILE_MB < NO_SLOT and MOE_TM <= 256, "slot indices and ranks must stay exact in bf16"

    rows = -(-(b + 1) // 8) * 8
    cvec = jnp.zeros((rows, d), F32).at[:b].set(c).at[b].set(c_ctx)
    mod = _modulation(cvec, w_mod[0], b_mod[0][None, :])
    sh1, sc1, g1, sh2, sc2, g2 = [mod[:b, k * d:(k + 1) * d][:, None, :] for k in range(6)]
    ctx_shift = jnp.broadcast_to(mod[b, 0:d][None, None, :], (b, 1, d))
    ctx_scale = jnp.broadcast_to(mod[b, d:2 * d][None, None, :], (b, 1, d))

    w_in0 = w_in[0]
    n_main = 4 * MIX_W
    w_state = w_in0[:, :n_main].astype(BF16)
    slots = ((0, 0), (0, 1), (1, 0), (1, 1), (0, 0), (0, 1), (0, 0), (0, 1))
    gab_cols = n_main + jnp.array([[ab * 2 * N_HEADS + dr * N_HEADS + hh for ab, dr in slots]
                                   for hh in range(N_HEADS)], I32)
    w_gab = w_in0[:, gab_cols.reshape(-1)].astype(BF16)
    n_state = n_main + N_GAB
    w_query = w_in0[:, n_state:].astype(BF16)
    a_coef = -jnp.exp(gdn_a_log[0].astype(F32))
    dtb = gdn_dt_bias[0].astype(F32)
    is_alpha = jnp.array([ab == 0 for ab, _ in slots])
    dirs = jnp.array([dr for _, dr in slots], I32)
    prow = jnp.stack([jnp.where(is_alpha[None, :], a_coef.T[:, dirs], 0.0).reshape(-1),
                      jnp.where(is_alpha[None, :], dtb.T[:, dirs], 0.0).reshape(-1)], axis=0)
    first2 = jnp.tile(jnp.arange(8) < 2, N_HEADS)
    w_gab_t = w_gab.T * first2[:, None].astype(BF16)
    pcol = prow.T * first2[:, None]
    gain_mix = norm_mix_pre[0][None, :]
    cos2, sin2 = _rope_tables(n)

    state_kinds = (_ROPE_SCALED, _PLAIN, _PLAIN, _PLAIN)
    query_kinds = (_ROPE, _PLAIN, _PLAIN, _PLAIN, _SIGMOID, _SIGMOID, _SIGMOID, _SIGMOID)
    lg = ret_log_decay[0].astype(F32)
    conv = gdn_conv[0].astype(F32)

    lc = ctx.shape[1]
    cfe, (ccols, cgrow, cgl) = _in_projection(
        ctx, gain_mix, ctx_scale, ctx_shift, (w_state,), w_gab, w_gab_t, prow, pcol,
        cos2[:lc], sin2[:lc], state_kinds, rope=False)
    cfeats = dict(zip(("rk", "rv", "gk", "gv"), cfe))
    zero_state = jnp.zeros((b, N_HEADS, 4, HEAD_DIM, HEAD_DIM), F32)
    init = _mixers(lg, cgl, cfeats, (ccols, cgrow), conv, zero_state, None, None, with_query=False)

    fe, (cols, grow, gl) = _in_projection(
        x, gain_mix, sc1, sh1, (w_state, w_query), w_gab, w_gab_t, prow, pcol, cos2, sin2,
        state_kinds + query_kinds, rope=True)
    feats = dict(zip(("rk", "rv", "gk", "gv", "rq", "rg", "gq", "gz"), fe[:8]))
    gates = fe[8:]
    y_ret, y_gdn = _mixers(lg, gl, feats, (cols, grow), conv, init,
                           ret_gn_w[0][None, :], gdn_norm_w[0][None, :], with_query=True)
    x1, h2, sel_t, wd_t = _mixer_out(
        x, y_ret, y_gdn, gates, w_ret_out[0].astype(BF16), w_gdn_out[0].astype(BF16), w_o[0].astype(BF16),
        norm_mix_post[0][None, :], g1, norm_ffn_pre[0][None, :], sc2, sh2,
        w_router[0].T.astype(F32), router_bias[0].astype(F32)[:, None])

    t = b * n
    h2f = h2.reshape(t, d)
    xs, cnt = _dispatch(h2f, sel_t)
    nt = t // MOE_TM
    n_blk = nt * _TILE_MB // EXP_MB + N_EXPERTS + EXP_LEAD
    bstart, nb, src, dst = _expert_plan(cnt[:, 0, :].astype(I32), n_blk)
    ys = _expert_ffn(xs, bstart, nb, src, dst, w_gate[0], w_up[0], w_down[0])
    out = _combine(ys, sel_t, wd_t, h2f, x1.reshape(t, d), w_sh_gate[0].astype(BF16), w_sh_up[0].astype(BF16),
                   w_sh_down[0].astype(BF16), norm_ffn_post[0][None, :], g2, n)
    return out.reshape(b, n, d)
```

```python
import functools
import math

import jax
import jax.numpy as jnp
import numpy as np
from jax import lax
from jax.experimental import pallas as pl
from jax.experimental.pallas import tpu as pltpu

F32 = jnp.float32
BF16 = jnp.bfloat16
I32 = jnp.int32
HIGHEST = lax.Precision.HIGHEST

N_HEADS = 4
HEAD_DIM = 128
MIX_W = N_HEADS * HEAD_DIM
CHUNK = 128
SHORT_CONV = 3
ROPE_BASE = 10000.0
GRID_W = 64
N_EXPERTS = 64
TOP_K = 8
N_GROUPS = 8
TOPK_GROUPS = 4
GROUP_SIZE = N_EXPERTS // N_GROUPS
ROUTED_SCALE = 2.5
EPS = 1e-6
N_GAB = 4 * N_HEADS

LANES = 128
F32_TILE_ROWS = 8
BF16_TILE_ROWS = 16
VMEM_LIMIT_BYTES = 56 * 1024 * 1024

PROJ_TM = 1024
MIXOUT_TM = 1024
MOE_TM = 256
MB_ROWS = BF16_TILE_ROWS
_MB_SHIFT = MB_ROWS.bit_length() - 1
_TILE_MB = -(-(TOP_K * MOE_TM // MB_ROWS + N_EXPERTS * (MB_ROWS - 1) // MB_ROWS + 1) // 8) * 8
TILE_ROWS = _TILE_MB * MB_ROWS
ROW_CHUNK = 256
TAIL_CHUNKS = 2
NO_SLOT = 255.0
EXP_BM = 512
EXP_MB = EXP_BM // MB_ROWS
EXP_LEAD = 3
EXP_OUT = 3
COMBINE_SLOTS = 3


def _cparams(sem):
    return pltpu.CompilerParams(dimension_semantics=sem, vmem_limit_bytes=VMEM_LIMIT_BYTES)


def _sigmoid(v):
    return 0.5 * jnp.tanh(0.5 * v) + 0.5


def _silu(v):
    return v * _sigmoid(v)


def _softplus(v):
    return jnp.maximum(v, 0.0) + jnp.log1p(jnp.exp(-jnp.abs(v)))


def _iota(shape, dim):
    return lax.broadcasted_iota(I32, shape, dim)


def _dot(a, b, **kw):
    return jnp.dot(a, b, preferred_element_type=F32, **kw)


def _dot_nt(a, b, **kw):
    return lax.dot_general(a, b, (((1,), (1,)), ((), ())), preferred_element_type=F32, **kw)


def _dot_tn(a, b, **kw):
    return lax.dot_general(a, b, (((0,), (0,)), ((), ())), preferred_element_type=F32, **kw)


def _rms(v, gain):
    return v * lax.rsqrt(jnp.mean(v * v, axis=-1, keepdims=True) + EPS) * gain


def _mod_kernel(c_ref, w_ref, b_ref, o_ref):
    o_ref[...] = _dot(_silu(c_ref[...]), w_ref[...], precision=HIGHEST) + b_ref[...]


def _modulation(cvec, w_mod, b_mod):
    rows, d = cvec.shape
    n = w_mod.shape[1]
    tn = 1024
    return pl.pallas_call(
        _mod_kernel,
        grid=(n // tn,),
        in_specs=[pl.BlockSpec((rows, d), lambda j: (0, 0)),
                  pl.BlockSpec((d, tn), lambda j: (0, j)),
                  pl.BlockSpec((1, tn), lambda j: (0, j))],
        out_specs=pl.BlockSpec((rows, tn), lambda j: (0, j)),
        out_shape=jax.ShapeDtypeStruct((rows, n), F32),
        compiler_params=_cparams(("arbitrary",)),
        name="adaln_modulation",
    )(cvec, w_mod, b_mod)


_PLAIN, _ROPE, _ROPE_SCALED, _SIGMOID = 0, 1, 2, 3


def _proj_kernel(x_ref, gain_ref, sc_ref, sh_ref, *refs, kinds, rope, n_w):
    w_refs = refs[:n_w]
    wg_ref, wgt_ref, prow_ref, pcol_ref, cos_ref, sin_ref = refs[n_w:n_w + 6]
    out_refs = refs[n_w + 6:]
    groups = [(w, c) for w in w_refs for c in range(0, w.shape[1], MIX_W)]
    n_feat = len(kinds)
    feat_refs = out_refs[:n_feat]
    cols_ref, grow_ref, gl_ref = out_refs[n_feat:]
    tm = x_ref.shape[0]

    x = x_ref[...]
    h = (_rms(x, gain_ref[...]) * (1.0 + sc_ref[...]) + sh_ref[...]).astype(BF16)

    r_i = _iota((CHUNK, CHUNK), 0)
    c_i = _iota((CHUNK, CHUNK), 1)
    lower_incl = (c_i <= r_i).astype(BF16)
    upper_incl = (c_i >= r_i).astype(BF16)

    def prefix_rows(v):
        hi, lo = _split_bf16(v)
        return _dot(lower_incl, hi) + _dot(lower_incl, lo)

    def prefix_lanes(v):
        hi, lo = _split_bf16(v)
        return _dot(hi, upper_incl) + _dot(lo, upper_incl)

    nc = N_HEADS * 8
    pg = _dot(h, wg_ref[...])
    colt = _iota((tm, nc), 1) & 7
    la = jnp.where((colt == 2) | (colt == 3), 0.0, prow_ref[0:1, :] * _softplus(pg + prow_ref[1:2, :]))
    beta = _sigmoid(pg)
    colc = _iota((CHUNK, nc), 1) & 7
    fwd_col = (colc & 1) == 0
    for c in range(tm // CHUNK):
        sl = slice(c * CHUNK, (c + 1) * CHUNK)
        la_c = la[sl]
        pre = prefix_rows(la_c)
        suf = pre[CHUNK - 1:CHUNK, :] - pre + la_c
        g_c = jnp.where(fwd_col, pre, suf)
        rest = jnp.where(fwd_col, suf, pre) - la_c
        vals = jnp.where(colc < 2, g_c, jnp.where(colc < 4, beta[sl], jnp.where(
            colc < 6, jnp.exp(g_c), jnp.exp(rest))))
        for hh in range(N_HEADS):
            cols_ref[hh, sl, :] = vals[:, 8 * hh:8 * hh + 8]
        gl_ref[c:c + 1, :] = jnp.exp(g_c[0:1, :] + rest[0:1, :])

    pgt = _dot_nt(wgt_ref[...], h)
    rowq = _iota((N_HEADS * 8, tm), 0) & 7
    lat = jnp.where(rowq < 2, pcol_ref[:, 0:1] * _softplus(pgt + pcol_ref[:, 1:2]), 0.0)
    rowc = _iota((N_HEADS * 8, CHUNK), 0) & 7
    for c in range(tm // CHUNK):
        sl = slice(c * CHUNK, (c + 1) * CHUNK)
        lat_c = lat[:, sl]
        pre_t = prefix_lanes(lat_c)
        suf_t = pre_t[:, CHUNK - 1:CHUNK] - pre_t + lat_c
        grow_ref[:, :, sl] = jnp.where(rowc == 0, pre_t, suf_t).reshape(N_HEADS, 8, CHUNK)

    if rope:
        cos2 = cos_ref[...]
        sin2 = sin_ref[...]

    for g, kind in enumerate(kinds):
        w_ref, c0 = groups[g]
        p = _dot(h, w_ref[:, c0:c0 + MIX_W])
        if kind == _SIGMOID:
            p = _sigmoid(p)
        elif kind in (_ROPE, _ROPE_SCALED) and rope:
            heads = []
            for hh in range(N_HEADS):
                t = p[:, hh * HEAD_DIM:(hh + 1) * HEAD_DIM]
                heads.append(t * cos2 + pltpu.roll(t, HEAD_DIM // 2, 1) * sin2)
            p = jnp.concatenate(heads, axis=1)
        if kind == _ROPE_SCALED:
            p = p * (HEAD_DIM ** -0.5)
        feat_refs[g][...] = p.astype(feat_refs[g].dtype)


def _in_projection(x, gain, scale, shift, w_mains, w_gab, w_gab_t, prow, pcol, cos2, sin2, kinds, rope):
    b, l, d = x.shape
    tm = min(PROJ_TM, l)
    tiles = l // tm
    n_chunk = tm // CHUNK
    feat_shapes = [jax.ShapeDtypeStruct((b, l, MIX_W), BF16) for _ in kinds]
    feat_specs = [pl.BlockSpec((None, tm, MIX_W), lambda i, j: (i, j, 0)) for _ in kinds]
    out_shape = feat_shapes + [jax.ShapeDtypeStruct((b, N_HEADS, l, 8), F32),
                               jax.ShapeDtypeStruct((b, N_HEADS, 8, l), F32),
                               jax.ShapeDtypeStruct((b, tiles, n_chunk, N_HEADS * 8), F32)]
    out_specs = feat_specs + [pl.BlockSpec((None, N_HEADS, tm, 8), lambda i, j: (i, 0, j, 0)),
                              pl.BlockSpec((None, N_HEADS, 8, tm), lambda i, j: (i, 0, 0, j)),
                              pl.BlockSpec((None, None, n_chunk, N_HEADS * 8), lambda i, j: (i, j, 0, 0))]
    const = functools.partial(pl.BlockSpec, pipeline_mode=pl.Buffered(1))
    in_specs = [
        pl.BlockSpec((None, tm, d), lambda i, j: (i, j, 0)),
        const((1, d), lambda i, j: (0, 0)),
        pl.BlockSpec((None, 1, d), lambda i, j: (i, 0, 0)),
        pl.BlockSpec((None, 1, d), lambda i, j: (i, 0, 0)),
    ] + [const(w.shape, lambda i, j: (0, 0)) for w in w_mains] + [
        const((d, N_HEADS * 8), lambda i, j: (0, 0)),
        const((N_HEADS * 8, d), lambda i, j: (0, 0)),
        const((2, N_HEADS * 8), lambda i, j: (0, 0)),
        const((N_HEADS * 8, 2), lambda i, j: (0, 0)),
        pl.BlockSpec((tm, HEAD_DIM), lambda i, j: (j, 0)),
        pl.BlockSpec((tm, HEAD_DIM), lambda i, j: (j, 0)),
    ]
    outs = pl.pallas_call(
        functools.partial(_proj_kernel, kinds=tuple(kinds), rope=rope, n_w=len(w_mains)),
        grid=(b, tiles),
        in_specs=in_specs,
        out_specs=out_specs,
        out_shape=out_shape,
        compiler_params=_cparams(("arbitrary", "arbitrary")),
        name="in_projection_rope" if rope else "in_projection_ctx",
    )(x, gain, scale, shift, *w_mains, w_gab, w_gab_t, prow, pcol, cos2, sin2)
    feats = outs[:len(kinds)]
    cols, grow, gl = outs[len(kinds):]
    return feats, (cols, grow, gl.reshape(b * (l // CHUNK), N_HEADS * 8))


def _unit_triangular_inverses(mats, lowers):
    r = _iota((CHUNK, CHUNK), 0)
    c = _iota((CHUNK, CHUNK), 1)
    eye = (r == c).astype(F32)
    invs = [eye - jnp.where((r >> 1) == (c >> 1), a, 0.0) for a in mats]
    for level in range(1, int(math.log2(CHUNK))):
        s = 1 << level
        mask = ((r >> (level + 1)) == (c >> (level + 1))) & ((r >> level) != (c >> level))
        invb = [inv.astype(BF16) for inv in invs]
        offs = [jnp.where(mask, a, 0.0).astype(BF16) for a in mats]
        if s < 8:
            half = [_dot(off, ib).astype(BF16) for off, ib in zip(offs, invb)]
            invs = [inv - _dot(ib, hf) for inv, ib, hf in zip(invs, invb, half)]
            continue
        def rows_of(x, lower, moving):
            first = s if (lower == moving) else 0
            return [x[g * 2 * s + first:g * 2 * s + first + s] for g in range(CHUNK // (2 * s))]

        half = [_dot(jnp.concatenate(rows_of(off, lo, True), axis=0), ib).astype(BF16)
                for off, ib, lo in zip(offs, invb, lowers)]
        zero = jnp.zeros((s, CHUNK), BF16)
        full = []
        for hf, lo in zip(half, lowers):
            pieces = []
            for g in range(CHUNK // (2 * s)):
                piece = hf[g * s:(g + 1) * s]
                pieces += [zero, piece] if lo else [piece, zero]
            full.append(jnp.concatenate(pieces, axis=0))
        corr = [_dot(jnp.concatenate(rows_of(ib, lo, True), axis=0), hf) for ib, hf, lo in zip(invb, full, lowers)]
        new = []
        for inv, cr, lo in zip(invs, corr, lowers):
            keep = rows_of(inv, lo, False)
            moved = [m - cr[g * s:(g + 1) * s] for g, m in enumerate(rows_of(inv, lo, True))]
            pieces = []
            for k, m in zip(keep, moved):
                pieces += [k, m] if lo else [m, k]
            new.append(jnp.concatenate(pieces, axis=0))
        invs = new
    return invs


def _mixer_kernel(*refs, seq_len, with_query):
    n_chunk = seq_len // CHUNK
    if with_query:
        (lg_ref, gl_ref, rk_ref, rv_ref, gk_ref, gv_ref, rq_ref, gq_ref, rg_ref, gz_ref,
         cols_ref, grow_ref, cq_ref, ck_ref, cv_ref, s0_ref, gnw_ref, rmsw_ref,
         yret_ref, ygdn_ref,
         qes, oret, ogdn, ubuf, wbuf, pbuf, nbuf, cbuf, kvbuf, rtile, state) = refs
    else:
        (lg_ref, gl_ref, rk_ref, rv_ref, gk_ref, gv_ref,
         cols_ref, grow_ref, ck_ref, cv_ref, s0_ref,
         sfin_ref,
         nbuf, cbuf, kvbuf, rtile, state) = refs
    bi = pl.program_id(0)
    hi = pl.program_id(1)

    row = _iota((CHUNK, CHUNK), 0)
    colm = _iota((CHUNK, CHUNK), 1)
    rowf = row.astype(F32)
    colf = colm.astype(F32)

    edge_row = _iota((F32_TILE_ROWS, HEAD_DIM), 0)

    def conv_chunk(src_ref, w_ref, n):
        s = pl.multiple_of(n * CHUNK, CHUNK)
        x = src_ref[pl.ds(s, CHUNK), :].astype(F32)
        ps = pl.multiple_of(jnp.maximum(s - BF16_TILE_ROWS, 0), BF16_TILE_ROWS)
        ns = pl.multiple_of(jnp.minimum(s + CHUNK, seq_len - BF16_TILE_ROWS), BF16_TILE_ROWS)
        prev_row = src_ref[pl.ds(ps, BF16_TILE_ROWS), :].astype(F32)[BF16_TILE_ROWS - 1:BF16_TILE_ROWS, :]
        next_row = src_ref[pl.ds(ns, BF16_TILE_ROWS), :].astype(F32)[0:1, :]
        prev_row = prev_row * jnp.where(n > 0, 1.0, 0.0)
        next_row = next_row * jnp.where(n < n_chunk - 1, 1.0, 0.0)
        down, up, rt = pltpu.roll(x, 1, 0), pltpu.roll(x, CHUNK - 1, 0), F32_TILE_ROWS
        first = jnp.where(edge_row == 0, jnp.broadcast_to(prev_row, (rt, HEAD_DIM)), down[:rt])
        last = jnp.where(edge_row == rt - 1, jnp.broadcast_to(next_row, (rt, HEAD_DIM)), up[CHUNK - rt:])
        xp = jnp.concatenate([first, down[rt:]], axis=0)
        xn = jnp.concatenate([up[:CHUNK - rt], last], axis=0)
        taps = 0.5 * w_ref[...]
        h = taps[0:1, :] * xp + taps[1:2, :] * x + taps[2:3, :] * xn
        return h * jnp.tanh(h) + h

    def l2n(v):
        return v * lax.rsqrt(jnp.sum(v * v, axis=-1, keepdims=True) + EPS)

    for d in range(2):
        lg = lg_ref[d, hi]
        if d == 0:
            dist, pos_q, pos_k = rowf - colf, rowf + 1.0, (CHUNK - 1.0) - rowf
        else:
            dist, pos_q, pos_k = colf - rowf, CHUNK - rowf, rowf
        rtile[4 * d + 0] = jnp.where(dist >= 0, jnp.exp(lg * jnp.maximum(dist, 0.0)), 0.0)
        rtile[4 * d + 1] = jnp.exp(lg * pos_q)
        rtile[4 * d + 2] = jnp.exp(lg * pos_k)
        rtile[4 * d + 3] = jnp.exp(lg * jnp.full((CHUNK, CHUNK), float(CHUNK), F32))

    state[...] = s0_ref[...]

    def bcast_col(cols, j):
        return jnp.broadcast_to(cols[:, j:j + 1], (CHUNK, CHUNK))

    pre_chunks = min(8, n_chunk)

    def prepass(m, carry):
        chunks = [m * pre_chunks + j for j in range(pre_chunks)]
        chunk_cs = [pl.ds(pl.multiple_of(n * CHUNK, CHUNK), CHUNK) for n in chunks]
        conv_k, conv_v, conv_q = [], [], []
        for n, cs in zip(chunks, chunk_cs):
            conv_k.append(l2n(conv_chunk(gk_ref, ck_ref, n)).astype(BF16))
            conv_v.append(conv_chunk(gv_ref, cv_ref, n).astype(BF16))
            if with_query:
                conv_q.append((l2n(conv_chunk(gq_ref, cq_ref, n)) * (HEAD_DIM ** -0.5)).astype(BF16))
                ogdn[cs, :] = jnp.zeros((CHUNK, HEAD_DIM), F32)

        jobs = [(j, d) for j in range(pre_chunks) for d in range(2)]
        css = [chunk_cs[j] for j, _ in jobs]
        slots = [d * n_chunk + chunks[j] for j, d in jobs]
        dirs = [d for _, d in jobs]
        ks = [rk_ref[cs, :] for cs in css]
        vs = [rv_ref[cs, :] for cs in css]
        if with_query:
            scs = [(_dot_nt(rq_ref[cs, :], k) * rtile[4 * d + 0]).astype(BF16) for cs, k, d in zip(css, ks, dirs)]
            outs = [_dot(sc, v) for sc, v in zip(scs, vs)]
            for j, cs in enumerate(chunk_cs):
                oret[cs, :] = outs[2 * j] + outs[2 * j + 1]
        kvs = [_dot_tn((k.astype(F32) * rtile[4 * d + 2]).astype(BF16), v) for k, v, d in zip(ks, vs, dirs)]
        for slot, kv in zip(slots, kvs):
            kvbuf[slot] = kv
        ks = [conv_k[j] for j, _ in jobs]
        vs = [conv_v[j] for j, _ in jobs]
        colss = [cols_ref[cs, :] for cs in css]
        betas = [bcast_col(cols, 2 + d) for cols, d in zip(colss, dirs)]
        incls = [(row >= colm) if d == 0 else (row <= colm) for d in dirs]
        stricts = [(row > colm) if d == 0 else (row < colm) for d in dirs]
        decs = [jnp.exp(bcast_col(cols, d) - jnp.broadcast_to(grow_ref[d:d + 1, cs], (CHUNK, CHUNK)))
                for cols, d, cs in zip(colss, dirs, css)]
        kks = [_dot_nt(k, k) for k in ks]
        mats = [kk * beta * jnp.where(strict, dec, 0.0) for kk, beta, strict, dec in zip(kks, betas, stricts, decs)]
        tinvs = [t.astype(BF16) for t in _unit_triangular_inverses(mats, [d == 0 for d in dirs])]
        rhs = [jnp.concatenate([(beta * v.astype(F32)).astype(BF16),
                                (beta * bcast_col(cols, 4 + d) * k.astype(F32)).astype(BF16)], axis=1)
               for beta, v, cols, d, k in zip(betas, vs, colss, dirs, ks)]
        uws = [_dot(t, r) for t, r in zip(tinvs, rhs)]
        us = [uw[:, :HEAD_DIM] for uw in uws]
        ws = [uw[:, HEAD_DIM:].astype(BF16) for uw in uws]
        kts = [(k.astype(F32) * bcast_col(cols, 6 + d)).astype(BF16) for k, cols, d in zip(ks, colss, dirs)]
        ncs = [_dot_tn(kt, uw.astype(BF16)) for kt, uw in zip(kts, uws)]
        for slot, nc_mat in zip(slots, ncs):
            cbuf[slot] = nc_mat[:, :HEAD_DIM]
            nbuf[slot] = nc_mat[:, HEAD_DIM:].astype(BF16)
        if with_query:
            for slot, u, w in zip(slots, us, ws):
                ubuf[slot] = u.astype(BF16)
                wbuf[slot] = w
            qks = [_dot_nt(conv_q[j], k) for (j, _), k in zip(jobs, ks)]
            for slot, qk, incl, dec in zip(slots, qks, incls, decs):
                pbuf[slot] = (qk * jnp.where(incl, dec, 0.0)).astype(BF16)
            for (j, d), cs, cols in zip(jobs, css, colss):
                qes[d, cs, :] = (conv_q[j].astype(F32) * bcast_col(cols, 4 + d)).astype(BF16)
        return carry

    lax.fori_loop(0, n_chunk // pre_chunks, prepass, 0)

    def finish(cs):
        ro = oret[cs, :]
        rc = ro - jnp.mean(ro, axis=-1, keepdims=True)
        ry = rc * lax.rsqrt(jnp.mean(rc * rc, axis=-1, keepdims=True) + EPS)
        yret_ref[cs, :] = (ry * gnw_ref[...] * _silu(rg_ref[cs, :].astype(F32))).astype(BF16)
        go = ogdn[cs, :]
        gy = go * lax.rsqrt(jnp.mean(go * go, axis=-1, keepdims=True) + EPS)
        ygdn_ref[cs, :] = (gy * rmsw_ref[...] * _silu(gz_ref[cs, :].astype(F32))).astype(BF16)

    def chunk_ids(n):
        nds = [n, n_chunk - 1 - n]
        css = [pl.ds(nd * CHUNK if isinstance(nd, int) else pl.multiple_of(nd * CHUNK, CHUNK), CHUNK) for nd in nds]
        slots = [d * n_chunk + nd for d, nd in enumerate(nds)]
        return nds, css, slots

    def late_outputs(n, vnb, finishing):
        _, css, slots = chunk_ids(n)
        for d in range(2):
            ogdn[css[d], :] += _dot(pbuf[slots[d]], vnb[d])
            if finishing:
                finish(css[d])

    def scan_step(n, vnb_prev, has_prev, finish_prev):
        nds, css, slots = chunk_ids(n)
        ret_st = [state[d] for d in range(2)]
        gdn_st = [state[2 + d] for d in range(2)]
        gdn_stb = [st.astype(BF16) for st in gdn_st]
        shrink = [_dot(nbuf[slot], stb) for slot, stb in zip(slots, gdn_stb)]
        for d in range(2):
            state[2 + d] = gl_ref[bi * n_chunk + nds[d], 8 * hi + d] * gdn_st[d] - shrink[d] + cbuf[slots[d]]
            state[d] = rtile[4 * d + 3] * ret_st[d] + kvbuf[slots[d]]
        if not with_query:
            return vnb_prev
        vnb = tuple((ubuf[slot].astype(F32) - _dot(wbuf[slot], stb)).astype(BF16)
                    for slot, stb in zip(slots, gdn_stb))
        for d in range(2):
            oret[css[d], :] += _dot(rq_ref[css[d], :], ret_st[d].astype(BF16)) * rtile[4 * d + 1]
            ogdn[css[d], :] += _dot(qes[d, css[d], :], gdn_stb[d])
        if has_prev:
            late_outputs(n - 1, vnb_prev, finish_prev)
        return vnb

    if with_query:
        half = n_chunk // 2
        zero = jnp.zeros((CHUNK, HEAD_DIM), BF16)
        vnb = scan_step(0, (zero, zero), has_prev=False, finish_prev=False)
        vnb = lax.fori_loop(1, half + 1, functools.partial(scan_step, has_prev=True, finish_prev=False), vnb)
        vnb = lax.fori_loop(half + 1, n_chunk, functools.partial(scan_step, has_prev=True, finish_prev=True), vnb)
        late_outputs(n_chunk - 1, vnb, True)
    else:
        lax.fori_loop(0, n_chunk, functools.partial(scan_step, has_prev=False, finish_prev=False), 0)
        sfin_ref[...] = state[...]


def _mixers(lg, gl, feats, dec, conv, s0, gnw, rmsw, with_query):
    cols, grow = dec
    b, l, _ = feats["rk"].shape
    n_slot = 2 * (l // CHUNK)
    assert (l // CHUNK) % min(8, l // CHUNK) == 0
    smem = pl.BlockSpec(memory_space=pltpu.SMEM)
    head = pl.BlockSpec((None, l, HEAD_DIM), lambda i, j: (i, 0, j))
    colspec = pl.BlockSpec((None, None, l, 8), lambda i, j: (i, j, 0, 0))
    rowspec = pl.BlockSpec((None, None, 8, l), lambda i, j: (i, j, 0, 0))
    tile_f32 = pltpu.VMEM((n_slot, CHUNK, CHUNK), F32)
    tile_bf16 = pltpu.VMEM((n_slot, CHUNK, CHUNK), BF16)
    st_spec = pl.BlockSpec((None, None, 4, HEAD_DIM, HEAD_DIM), lambda i, j: (i, j, 0, 0, 0))

    def conv_spec(which):
        return pl.BlockSpec((SHORT_CONV, HEAD_DIM), lambda i, j, w=which: (0, w * N_HEADS + j))

    def gain_spec():
        return pl.BlockSpec((1, HEAD_DIM), lambda i, j: (0, j))

    if with_query:
        args = [lg, gl, feats["rk"], feats["rv"], feats["gk"], feats["gv"], feats["rq"], feats["gq"], feats["rg"],
                feats["gz"], cols, grow, conv, conv, conv, s0, gnw, rmsw]
        in_specs = [smem, smem] + [head] * 8 + [colspec, rowspec, conv_spec(0), conv_spec(1), conv_spec(2),
                                                st_spec, gain_spec(), gain_spec()]
        out_shape = [jax.ShapeDtypeStruct((b, l, MIX_W), BF16)] * 2
        out_specs = [head, head]
        scratch = ([pltpu.VMEM((2, l, HEAD_DIM), BF16)] + [pltpu.VMEM((l, HEAD_DIM), F32)] * 2
                   + [tile_bf16, tile_bf16, tile_bf16, tile_bf16, tile_f32, tile_f32])
    else:
        args = [lg, gl, feats["rk"], feats["rv"], feats["gk"], feats["gv"], cols, grow, conv, conv, s0]
        in_specs = [smem, smem] + [head] * 4 + [colspec, rowspec, conv_spec(1), conv_spec(2), st_spec]
        out_shape = jax.ShapeDtypeStruct((b, N_HEADS, 4, HEAD_DIM, HEAD_DIM), F32)
        out_specs = st_spec
        scratch = [tile_bf16, tile_f32, tile_f32]
    scratch = scratch + [pltpu.VMEM((8, CHUNK, CHUNK), F32), pltpu.VMEM((4, HEAD_DIM, HEAD_DIM), F32)]
    return pl.pallas_call(
        functools.partial(_mixer_kernel, seq_len=l, with_query=with_query),
        grid=(b, N_HEADS),
        in_specs=in_specs,
        out_specs=out_specs,
        out_shape=out_shape,
        scratch_shapes=scratch,
        compiler_params=_cparams(("arbitrary", "arbitrary")),
        name="mixers_latent" if with_query else "mixers_context",
    )(*args)


def _split_bf16(v):
    hi = v.astype(BF16)
    return hi, (v - hi.astype(F32)).astype(BF16)


def _route(h2, wrt_ref, bias_ref, cand_ref, sel_ref, wd_ref):
    tm = h2.shape[0]
    h_hi, h_lo = _split_bf16(h2)
    w_hi, w_lo = _split_bf16(wrt_ref[...])
    logits = _dot_nt(w_hi, h_hi) + (_dot_nt(w_hi, h_lo) + _dot_nt(w_lo, h_hi))
    scores = _sigmoid(logits)
    biased = scores + bias_ref[...]
    neg_inf = float("-inf")
    sub = _iota((GROUP_SIZE, tm), 0).astype(F32)
    group_score = []
    for g in range(N_GROUPS):
        blk = biased[g * GROUP_SIZE:(g + 1) * GROUP_SIZE, :]
        m1 = jnp.max(blk, axis=0, keepdims=True)
        first = jnp.min(jnp.where(blk == m1, sub, float(GROUP_SIZE)), axis=0, keepdims=True)
        m2 = jnp.max(jnp.where(sub == first, neg_inf, blk), axis=0, keepdims=True)
        group_score.append(m1 + m2)
    for g in range(N_GROUPS):
        ahead = jnp.zeros((1, tm), I32)
        for g2 in range(N_GROUPS):
            if g2 == g:
                continue
            before = (group_score[g2] > group_score[g])
            if g2 < g:
                before = before | (group_score[g2] == group_score[g])
            ahead = ahead + before.astype(I32)
        keep = jnp.broadcast_to(ahead, (GROUP_SIZE, tm)) < TOPK_GROUPS
        cand_ref[g * GROUP_SIZE:(g + 1) * GROUP_SIZE, :] = jnp.where(
            keep, biased[g * GROUP_SIZE:(g + 1) * GROUP_SIZE, :], neg_inf)
    work = cand_ref[...]
    eidx = _iota((N_EXPERTS, tm), 0).astype(F32)
    sel = jnp.zeros((N_EXPERTS, tm), jnp.bool_)
    for _ in range(TOP_K):
        best = jnp.max(work, axis=0, keepdims=True)
        first = jnp.min(jnp.where(work == best, eidx, float(N_EXPERTS)), axis=0, keepdims=True)
        pick = eidx == first
        sel = sel | pick
        work = jnp.where(pick, neg_inf, work)
    picked = jnp.where(sel, scores, 0.0)
    wsum = jnp.sum(picked, axis=0, keepdims=True)
    sel_ref[...] = sel.astype(F32)
    wd_ref[...] = picked / wsum * ROUTED_SCALE


def _mixout_kernel(x_ref, yr_ref, yg_ref, g0_ref, g1_ref, g2_ref, g3_ref, wr_ref, wg_ref, wo_ref,
                   npost_ref, gate1_ref, nffn_ref, sc2_ref, sh2_ref, wrt_ref, bias_ref,
                   x1_ref, h2_ref, sel_ref, wd_ref, cand_ref):
    r = _dot(yr_ref[...], wr_ref[...])
    g = _dot(yg_ref[...], wg_ref[...])
    half = r.shape[1] // 2
    merged = jnp.concatenate(
        [g0_ref[...].astype(F32) * r[:, :half] + g2_ref[...].astype(F32) * g[:, :half],
         g1_ref[...].astype(F32) * r[:, half:] + g3_ref[...].astype(F32) * g[:, half:]], axis=1)
    mo = _dot(merged.astype(BF16), wo_ref[...])
    x1 = x_ref[...] + gate1_ref[...] * _rms(mo, npost_ref[...])
    x1_ref[...] = x1
    h2 = _rms(x1, nffn_ref[...]) * (1.0 + sc2_ref[...]) + sh2_ref[...]
    h2_ref[...] = h2.astype(BF16)
    _route(h2, wrt_ref, bias_ref, cand_ref, sel_ref, wd_ref)


def _mixer_out(x, yr, yg, gates, wr, wg, wo, npost, gate1, nffn, sc2, sh2, wrt, bias):
    b, l, d = x.shape
    tm = min(MIXOUT_TM, l)
    tiles = l // tm
    tok = lambda w: pl.BlockSpec((None, tm, w), lambda i, j: (i, j, 0))
    const = functools.partial(pl.BlockSpec, pipeline_mode=pl.Buffered(1))
    vec = lambda: const((1, d), lambda i, j: (0, 0))
    bvec = lambda: pl.BlockSpec((None, 1, d), lambda i, j: (i, 0, 0))
    emap = pl.BlockSpec((N_EXPERTS, tm), lambda i, j: (0, i * tiles + j))
    in_specs = ([tok(d), tok(MIX_W), tok(MIX_W)] + [tok(MIX_W)] * 4
                + [const((MIX_W, d), lambda i, j: (0, 0)), const((MIX_W, d), lambda i, j: (0, 0)),
                   const((d, d), lambda i, j: (0, 0)),
                   vec(), bvec(), vec(), bvec(), bvec(),
                   const((N_EXPERTS, d), lambda i, j: (0, 0)), const((N_EXPERTS, 1), lambda i, j: (0, 0))])
    return pl.pallas_call(
        _mixout_kernel,
        grid=(b, tiles),
        in_specs=in_specs,
        out_specs=[tok(d), tok(d), emap, emap],
        out_shape=[jax.ShapeDtypeStruct((b, l, d), F32), jax.ShapeDtypeStruct((b, l, d), BF16),
                   jax.ShapeDtypeStruct((N_EXPERTS, b * l), F32), jax.ShapeDtypeStruct((N_EXPERTS, b * l), F32)],
        scratch_shapes=[pltpu.VMEM((N_EXPERTS, tm), F32)],
        compiler_params=_cparams(("arbitrary", "arbitrary")),
        name="mixer_out_router",
    )(x, yr, yg, *gates, wr, wg, wo, npost, gate1, nffn, sc2, sh2, wrt, bias)


def _tile_positions(sel):
    tm = sel.shape[1]
    selb = sel.astype(BF16)
    earlier = (_iota((tm, tm), 0) < _iota((tm, tm), 1)).astype(BF16)
    rank = _dot(selb, earlier)
    cnt = _dot(selb, jnp.ones((tm, tm), BF16))
    nmb = jnp.floor((cnt + (MB_ROWS - 1)) * (1.0 / MB_ROWS))
    below = (_iota((N_EXPERTS, N_EXPERTS), 1) < _iota((N_EXPERTS, N_EXPERTS), 0)).astype(BF16)
    offmb = _dot(below, nmb.astype(BF16))
    rank_hi = jnp.floor(rank * (1.0 / MB_ROWS))
    hi = jnp.where(sel > 0.0, offmb + rank_hi, NO_SLOT)
    lo = rank - rank_hi * MB_ROWS
    return hi, lo


def _micro_block_keys(sel, per_pair=()):
    tm = sel.shape[1]
    hi, lo = _tile_positions(sel)
    cnt_row = _dot_nt(jnp.ones((8, tm), BF16), sel.astype(BF16))
    nmb_row = jnp.floor((cnt_row + (MB_ROWS - 1)) * (1.0 / MB_ROWS))
    before = (_iota((N_EXPERTS, N_EXPERTS), 0) < _iota((N_EXPERTS, N_EXPERTS), 1)).astype(BF16)
    start = _dot(nmb_row.astype(BF16), before)[0:1, :]
    end = start + nmb_row[0:1, :]
    m_e = _iota((_TILE_MB, N_EXPERTS), 0).astype(F32)
    owner = ((m_e >= start) & (m_e < end)).astype(BF16)
    cols = [hi.astype(BF16), lo.astype(BF16)] + [v.astype(BF16) for v in per_pair]
    got = _dot(owner, jnp.concatenate(cols, axis=1))
    m_t = _iota((_TILE_MB, tm), 0).astype(F32)
    used = jnp.sum(nmb_row[0:1, :])
    key = jnp.where((got[:, :tm] == m_t) & (m_t < used), got[:, tm:2 * tm], -1.0)
    return key, cnt_row, used, [got[:, (2 + k) * tm:(3 + k) * tm] for k in range(len(per_pair))]


def _dispatch_kernel(h2_ref, sel_ref, xs_ref, cnt_ref, key_ref, hot_ref):
    last = pl.num_programs(0) - 1

    @pl.when(pl.program_id(0) < last)
    def _():
        _dispatch_tile(h2_ref, sel_ref, xs_ref, cnt_ref, key_ref, hot_ref)

    @pl.when(pl.program_id(0) == last)
    def _():
        xs_ref[...] = jnp.zeros_like(xs_ref)


def _dispatch_tile(h2_ref, sel_ref, xs_ref, cnt_ref, key_ref, hot_ref):
    tm = h2_ref.shape[0]
    key, cnt_row, used, _ = _micro_block_keys(sel_ref[...])
    key_ref[...] = key
    cnt_ref[...] = cnt_row
    x = h2_ref[...]
    rch = ROW_CHUNK
    in_block = _iota((MB_ROWS, tm), 0).astype(F32)

    def one_hot(rc):
        for m in range(rc * rch // MB_ROWS, (rc + 1) * rch // MB_ROWS):
            hit = jnp.broadcast_to(key_ref[m:m + 1, :], (MB_ROWS, tm)) == in_block
            hot_ref[m * MB_ROWS:(m + 1) * MB_ROWS, :] = jnp.where(hit, 1.0, 0.0).astype(BF16)

    n_rc = TILE_ROWS // rch
    n_sure = n_rc - TAIL_CHUNKS
    one_hot(0)
    for rc in range(n_sure):
        if rc + 1 < n_sure:
            one_hot(rc + 1)
        xs_ref[rc * rch:(rc + 1) * rch, :] = _dot(hot_ref[rc * rch:(rc + 1) * rch, :], x).astype(BF16)
    tail_used = used * MB_ROWS > n_sure * rch

    @pl.when(tail_used)
    def _():
        for rc in range(n_sure, n_rc):
            one_hot(rc)
            xs_ref[rc * rch:(rc + 1) * rch, :] = _dot(hot_ref[rc * rch:(rc + 1) * rch, :], x).astype(BF16)

    @pl.when(jnp.logical_not(tail_used))
    def _():
        xs_ref[n_sure * rch:, :] = jnp.zeros(((n_rc - n_sure) * rch, x.shape[1]), BF16)


def _dispatch(h2, sel_t):
    t, d = h2.shape
    tm = MOE_TM
    nt = t // tm
    return pl.pallas_call(
        _dispatch_kernel,
        grid=(nt + 1,),
        in_specs=[pl.BlockSpec((tm, d), lambda i: (jnp.minimum(i, nt - 1), 0)),
                  pl.BlockSpec((N_EXPERTS, tm), lambda i: (0, jnp.minimum(i, nt - 1)))],
        out_specs=[pl.BlockSpec((TILE_ROWS, d), lambda i: (i, 0)),
                   pl.BlockSpec((None, 8, N_EXPERTS), lambda i: (jnp.minimum(i, nt - 1), 0, 0))],
        out_shape=[jax.ShapeDtypeStruct((nt * TILE_ROWS + (EXP_OUT + 1) * EXP_BM, d), BF16),
                   jax.ShapeDtypeStruct((nt, 8, N_EXPERTS), F32)],
        scratch_shapes=[pltpu.VMEM((_TILE_MB, tm), F32), pltpu.VMEM((TILE_ROWS, tm), BF16)],
        compiler_params=_cparams(("arbitrary",)),
        name="moe_dispatch",
    )(h2, sel_t)


def _expert_plan(cnt, n_blk):
    nt = cnt.shape[0]
    nmb = (cnt + (MB_ROWS - 1)) // MB_ROWS
    offmb = jnp.cumsum(nmb, axis=1) - nmb
    per_e = nmb.T
    incl = jnp.cumsum(per_e, axis=1)
    excl = incl - per_e
    tot = incl[:, -1]
    nb = (tot + (EXP_MB - 1)) // EXP_MB
    bend = jnp.cumsum(nb)
    bstart = bend - nb
    b = jnp.arange(n_blk, dtype=I32)
    valid = b < bend[-1]
    last = jnp.maximum(bend[-1] - 1, 0)
    bq = jnp.where(valid, b, last)
    blk_e = jnp.minimum(jnp.sum((bend[None, :] <= bq[:, None]).astype(I32), axis=1), N_EXPERTS - 1)
    onehot_e = (blk_e[:, None] == jnp.arange(N_EXPERTS, dtype=I32)[None, :]).astype(I32)
    pick = lambda table: jnp.dot(onehot_e.astype(F32), table.astype(F32), precision=HIGHEST).astype(I32)
    bstart_b = jnp.sum(onehot_e * bstart[None, :], axis=1)
    tot_b = jnp.sum(onehot_e * tot[None, :], axis=1)
    p0 = (bq - bstart_b) * EXP_MB
    blk_nmb = jnp.where(valid, jnp.clip(tot_b - p0, 0, EXP_MB), 0).astype(I32)
    p = p0[:, None] + jnp.arange(EXP_MB, dtype=I32)[None, :]
    incl_b = pick(incl)
    passed = (incl_b[:, None, :] <= p[:, :, None]).astype(I32)
    tile = jnp.minimum(jnp.sum(passed, axis=2), nt - 1)
    shift_b = pick(offmb.T - excl)
    step_b = jnp.concatenate([shift_b[:, 1:] - shift_b[:, :-1], jnp.zeros_like(shift_b[:, :1])], axis=1)
    shift = shift_b[:, :1] + jnp.sum(passed * step_b[:, None, :], axis=2)
    where = tile * _TILE_MB + p + shift
    j = jnp.arange(EXP_MB, dtype=I32)[None, :]
    used = j < blk_nmb[:, None]
    spare = nt * _TILE_MB + (b[:, None] % EXP_OUT) * EXP_MB + j
    zeros_mb = nt * _TILE_MB + EXP_OUT * EXP_MB + j
    src = jnp.where(used, where, jnp.where(valid[:, None], where[:, :1], zeros_mb))
    dst = jnp.where(used, where, spare)
    dst = jnp.concatenate([nt * _TILE_MB + (EXP_OUT - 1) * EXP_MB + j, dst], axis=0)
    return bstart.astype(I32), nb.astype(I32), src.reshape(-1).astype(I32), dst.reshape(-1).astype(I32)


def _expert_kernel(bstart_ref, nb_ref, src_ref, dst_ref, xs_hbm, wg_ref, wu_ref, wd_ref, ys_hbm,
                   xbuf, ybuf, wgub, wdb, in_sem, out_sem):
    e = pl.program_id(0)
    n_in = xbuf.shape[0]
    n_out = ybuf.shape[0]

    def gather_copy(blk, slot, j):
        rows = pl.ds(pl.multiple_of(src_ref[blk * EXP_MB + j] * MB_ROWS, MB_ROWS), MB_ROWS)
        return pltpu.make_async_copy(xs_hbm.at[rows, :], xbuf.at[slot, j * MB_ROWS:(j + 1) * MB_ROWS, :],
                                     in_sem.at[slot])

    def scatter_copy(blk, slot, j):
        rows = pl.ds(pl.multiple_of(dst_ref[(blk + 1) * EXP_MB + j] * MB_ROWS, MB_ROWS), MB_ROWS)
        return pltpu.make_async_copy(ybuf.at[slot, j * MB_ROWS:(j + 1) * MB_ROWS, :], ys_hbm.at[rows, :],
                                     out_sem.at[slot])

    def start_gather(blk):
        for j in range(EXP_MB):
            gather_copy(blk, lax.rem(blk, n_in), j).start()

    def wait_gather(blk):
        for j in range(EXP_MB):
            gather_copy(blk, lax.rem(blk, n_in), j).wait()

    def start_scatter(blk):
        for j in range(EXP_MB):
            scatter_copy(blk, lax.rem(blk + n_out, n_out), j).start()

    def wait_scatter(slot):
        for j in range(EXP_MB):
            scatter_copy(0, slot, j).wait()

    @pl.when(e == 0)
    def _():
        ybuf[...] = jnp.zeros_like(ybuf)
        for ahead in range(EXP_LEAD):
            start_gather(ahead)
        spare_row0 = ys_hbm.shape[0] - (n_out + 1) * EXP_BM
        for slot in range(n_out - 1):
            for j in range(EXP_MB):
                spare = pl.ds(spare_row0 + (slot * EXP_MB + j) * MB_ROWS, MB_ROWS)
                pltpu.make_async_copy(ybuf.at[slot, j * MB_ROWS:(j + 1) * MB_ROWS, :], ys_hbm.at[spare, :],
                                      out_sem.at[slot]).start()

    de = wg_ref.shape[1]
    wgub[:, :de] = wg_ref[...].astype(BF16)
    wgub[:, de:] = wu_ref[...].astype(BF16)
    wdb[...] = wd_ref[...].astype(BF16)

    def block(b, carry):
        slot = lax.rem(b, n_out)
        wait_gather(b)
        wait_scatter(slot)
        x = xbuf[lax.rem(b, n_in)]
        gu = _dot(x, wgub[...])
        act = (_silu(gu[:, :de]) * gu[:, de:]).astype(BF16)
        start_gather(b + EXP_LEAD)
        start_scatter(b - 1)
        ybuf[slot] = _dot(act, wdb[...]).astype(BF16)
        return carry

    first = bstart_ref[e]
    lax.fori_loop(first, first + nb_ref[e], block, 0)

    @pl.when(e == pl.num_programs(0) - 1)
    def _():
        total = first + nb_ref[e]
        start_scatter(total - 1)
        for ahead in range(EXP_LEAD):
            wait_gather(total + ahead)
        for slot in range(n_out):
            wait_scatter(slot)


def _expert_ffn(xs, bstart, nb, src, dst, wg, wu, wd):
    rows, d = xs.shape
    de = wg.shape[2]
    grid_spec = pltpu.PrefetchScalarGridSpec(
        num_scalar_prefetch=4,
        grid=(wg.shape[0],),
        in_specs=[pl.BlockSpec(memory_space=pl.ANY),
                  pl.BlockSpec((None, d, de), lambda i, bs, bn, sr, ds: (i, 0, 0)),
                  pl.BlockSpec((None, d, de), lambda i, bs, bn, sr, ds: (i, 0, 0)),
                  pl.BlockSpec((None, de, d), lambda i, bs, bn, sr, ds: (i, 0, 0))],
        out_specs=pl.BlockSpec(memory_space=pl.ANY),
        scratch_shapes=[pltpu.VMEM((EXP_LEAD + 1, EXP_BM, d), BF16), pltpu.VMEM((EXP_OUT, EXP_BM, d), BF16),
                        pltpu.VMEM((d, 2 * de), BF16), pltpu.VMEM((de, d), BF16),
                        pltpu.SemaphoreType.DMA((EXP_LEAD + 1,)), pltpu.SemaphoreType.DMA((EXP_OUT,))],
    )
    return pl.pallas_call(
        _expert_kernel,
        grid_spec=grid_spec,
        out_shape=jax.ShapeDtypeStruct((rows, d), BF16),
        input_output_aliases={4: 0},
        compiler_params=_cparams(("arbitrary",)),
        name="moe_experts",
    )(bstart, nb, src, dst, xs, wg, wu, wd)


def _combine_kernel(ys_hbm, sel_ref, wd_ref, h2_ref, x1_ref, wsg_ref, wsu_ref, wsd_ref, npost_ref, gate2_ref,
                    o_ref, key_ref, wmb_ref, c_ref, acc_ref, ybuf, ysem, *, n_tiles):
    tm = h2_ref.shape[0]
    i = pl.program_id(0)
    n_slot = ybuf.shape[0]

    def fetch(tile):
        rows = pl.ds(pl.multiple_of(tile * TILE_ROWS, TILE_ROWS), TILE_ROWS)
        slot = lax.rem(tile, n_slot)
        return pltpu.make_async_copy(ys_hbm.at[rows, :], ybuf.at[slot], ysem.at[slot])

    @pl.when(i == 0)
    def _():
        for tile in range(min(n_slot - 1, n_tiles)):
            fetch(tile).start()

    @pl.when(i + (n_slot - 1) < n_tiles)
    def _():
        fetch(i + (n_slot - 1)).start()

    fetch(i).wait()
    ys_ref = ybuf.at[lax.rem(i, n_slot)]
    key, _, used, (wmb,) = _micro_block_keys(sel_ref[...], (wd_ref[...],))
    key_ref[...] = key
    wmb_ref[...] = wmb
    in_block = _iota((MB_ROWS, tm), 0).astype(F32)

    def build(lo_row, hi_row):
        for m in range(lo_row // MB_ROWS, hi_row // MB_ROWS):
            hit = jnp.broadcast_to(key_ref[m:m + 1, :], (MB_ROWS, tm)) == in_block
            weight = jnp.broadcast_to(wmb_ref[m:m + 1, :], (MB_ROWS, tm))
            c_ref[m * MB_ROWS:(m + 1) * MB_ROWS, :] = jnp.where(hit, weight, 0.0).astype(BF16)

    sure = TILE_ROWS - TAIL_CHUNKS * ROW_CHUNK
    build(0, sure)
    h2 = h2_ref[...]
    shared = _dot((_silu(_dot(h2, wsg_ref[...])) * _dot(h2, wsu_ref[...])).astype(BF16), wsd_ref[...])
    acc_ref[...] = shared + _dot_tn(c_ref[:sure, :], ys_ref[:sure, :])

    @pl.when(used * MB_ROWS > sure)
    def _():
        build(sure, TILE_ROWS)
        acc_ref[...] += _dot_tn(c_ref[sure:, :], ys_ref[sure:, :])

    o_ref[...] = x1_ref[...] + gate2_ref[...] * _rms(acc_ref[...], npost_ref[...])


def _combine(ys, sel_t, wd_t, h2, x1, wsg, wsu, wsd, npost, gate2, seq_len):
    t, d = h2.shape
    tm = MOE_TM
    nt = t // tm
    per_seq = seq_len // tm
    ds = wsg.shape[1]
    const = functools.partial(pl.BlockSpec, pipeline_mode=pl.Buffered(1))
    emap = pl.BlockSpec((N_EXPERTS, tm), lambda i: (0, i))
    return pl.pallas_call(
        functools.partial(_combine_kernel, n_tiles=nt),
        grid=(nt,),
        in_specs=[pl.BlockSpec(memory_space=pl.ANY), emap, emap,
                  pl.BlockSpec((tm, d), lambda i: (i, 0)), pl.BlockSpec((tm, d), lambda i: (i, 0)),
                  const((d, ds), lambda i: (0, 0)), const((d, ds), lambda i: (0, 0)), const((ds, d), lambda i: (0, 0)),
                  const((1, d), lambda i: (0, 0)),
                  pl.BlockSpec((None, 1, d), lambda i: (i // per_seq, 0, 0))],
        out_specs=pl.BlockSpec((tm, d), lambda i: (i, 0)),
        out_shape=jax.ShapeDtypeStruct((t, d), F32),
        scratch_shapes=[pltpu.VMEM((_TILE_MB, tm), F32), pltpu.VMEM((_TILE_MB, tm), F32),
                        pltpu.VMEM((TILE_ROWS, tm), BF16), pltpu.VMEM((tm, d), F32),
                        pltpu.VMEM((COMBINE_SLOTS, TILE_ROWS, d), BF16), pltpu.SemaphoreType.DMA((COMBINE_SLOTS,))],
        compiler_params=_cparams(("arbitrary",)),
        name="moe_combine",
    )(ys, sel_t, wd_t, h2, x1, wsg, wsu, wsd, npost, gate2)


def _rope_tables(n):
    rows = n // GRID_W
    pos_r = np.repeat(np.arange(rows, dtype=np.float32), GRID_W)
    pos_c = np.tile(np.arange(GRID_W, dtype=np.float32), rows)
    n_freq = HEAD_DIM // 4
    inv = (np.float32(ROPE_BASE) ** (-np.arange(n_freq, dtype=np.float32) / np.float32(n_freq))).astype(np.float32)
    ang = np.concatenate([pos_r[:, None] * inv, pos_c[:, None] * inv], axis=-1)
    cos, sin = np.cos(ang), np.sin(ang)
    return jnp.asarray(np.concatenate([cos, cos], axis=-1)), jnp.asarray(np.concatenate([-sin, sin], axis=-1))


def kernel(x, c, ctx, c_ctx, w_mod, b_mod, norm_mix_pre, norm_mix_post, norm_ffn_pre, norm_ffn_post, w_in, gdn_conv, ret_log_decay, gdn_a_log, gdn_dt_bias, ret_gn_w, gdn_norm_w, w_ret_out, w_gdn_out, w_o, w_router, router_bias, w_gate, w_up, w_down, w_sh_gate, w_sh_up, w_sh_down):
    b, n, d = x.shape
    depth = w_mod.shape[0]
    assert depth == 1, "single-layer block"
    assert all(n % min(tile, n) == 0 for tile in (PROJ_TM, MIXOUT_TM, MOE_TM)) and n % MOE_TM == 0
    assert ctx.shape[1] % CHUNK == 0 and (EXP_OUT + 1) * EXP_BM <= TILE_ROWS
    assert _TILE_MB < NO_SLOT and MOE_TM <= 256, "slot indices and ranks must stay exact in bf16"

    rows = -(-(b + 1) // 8) * 8
    cvec = jnp.zeros((rows, d), F32).at[:b].set(c).at[b].set(c_ctx)
    mod = _modulation(cvec, w_mod[0], b_mod[0][None, :])
    sh1, sc1, g1, sh2, sc2, g2 = [mod[:b, k * d:(k + 1) * d][:, None, :] for k in range(6)]
    ctx_shift = jnp.broadcast_to(mod[b, 0:d][None, None, :], (b, 1, d))
    ctx_scale = jnp.broadcast_to(mod[b, d:2 * d][None, None, :], (b, 1, d))

    w_in0 = w_in[0]
    n_main = 4 * MIX_W
    w_state = w_in0[:, :n_main].astype(BF16)
    slots = ((0, 0), (0, 1), (1, 0), (1, 1), (0, 0), (0, 1), (0, 0), (0, 1))
    gab_cols = [ab * 2 * N_HEADS + dr * N_HEADS + hh for hh in range(N_HEADS) for ab, dr in slots]
    pick = np.zeros((N_GAB, len(gab_cols)), np.float32)
    pick[gab_cols, np.arange(len(gab_cols))] = 1.0
    w_gab = _dot(w_in0[:, n_main:n_main + N_GAB].astype(BF16), jnp.asarray(pick, BF16)).astype(BF16)
    n_state = n_main + N_GAB
    w_query = w_in0[:, n_state:].astype(BF16)
    a_coef = -jnp.exp(gdn_a_log[0].astype(F32))
    dtb = gdn_dt_bias[0].astype(F32)
    is_alpha = jnp.array([ab == 0 for ab, _ in slots])
    dirs = jnp.array([dr for _, dr in slots], I32)
    prow = jnp.stack([jnp.where(is_alpha[None, :], a_coef.T[:, dirs], 0.0).reshape(-1),
                      jnp.where(is_alpha[None, :], dtb.T[:, dirs], 0.0).reshape(-1)], axis=0)
    first2 = jnp.tile(jnp.arange(8) < 2, N_HEADS)
    w_gab_t = w_gab.T * first2[:, None].astype(BF16)
    pcol = prow.T * first2[:, None]
    gain_mix = norm_mix_pre[0][None, :]
    cos2, sin2 = _rope_tables(n)

    state_kinds = (_ROPE_SCALED, _PLAIN, _PLAIN, _PLAIN)
    query_kinds = (_ROPE, _PLAIN, _PLAIN, _PLAIN, _SIGMOID, _SIGMOID, _SIGMOID, _SIGMOID)
    lg = ret_log_decay[0].astype(F32)
    conv = gdn_conv[0].astype(F32)

    lc = ctx.shape[1]
    cfe, (ccols, cgrow, cgl) = _in_projection(
        ctx, gain_mix, ctx_scale, ctx_shift, (w_state,), w_gab, w_gab_t, prow, pcol,
        cos2[:lc], sin2[:lc], state_kinds, rope=False)
    cfeats = dict(zip(("rk", "rv", "gk", "gv"), cfe))
    zero_state = jnp.zeros((b, N_HEADS, 4, HEAD_DIM, HEAD_DIM), F32)
    init = _mixers(lg, cgl, cfeats, (ccols, cgrow), conv, zero_state, None, None, with_query=False)

    fe, (cols, grow, gl) = _in_projection(
        x, gain_mix, sc1, sh1, (w_state, w_query), w_gab, w_gab_t, prow, pcol, cos2, sin2,
        state_kinds + query_kinds, rope=True)
    feats = dict(zip(("rk", "rv", "gk", "gv", "rq", "rg", "gq", "gz"), fe[:8]))
    gates = fe[8:]
    y_ret, y_gdn = _mixers(lg, gl, feats, (cols, grow), conv, init,
                           ret_gn_w[0][None, :], gdn_norm_w[0][None, :], with_query=True)
    x1, h2, sel_t, wd_t = _mixer_out(
        x, y_ret, y_gdn, gates, w_ret_out[0].astype(BF16), w_gdn_out[0].astype(BF16), w_o[0].astype(BF16),
        norm_mix_post[0][None, :], g1, norm_ffn_pre[0][None, :], sc2, sh2,
        w_router[0].T.astype(F32), router_bias[0].astype(F32)[:, None])

    t = b * n
    h2f = h2.reshape(t, d)
    xs, cnt = _dispatch(h2f, sel_t)
    nt = t // MOE_TM
    n_blk = nt * _TILE_MB // EXP_MB + N_EXPERTS + EXP_LEAD
    bstart, nb, src, dst = _expert_plan(cnt[:, 0, :].astype(I32), n_blk)
    ys = _expert_ffn(xs, bstart, nb, src, dst, w_gate[0], w_up[0], w_down[0])
    out = _combine(ys, sel_t, wd_t, h2f, x1.reshape(t, d), w_sh_gate[0].astype(BF16), w_sh_up[0].astype(BF16),
                   w_sh_down[0].astype(BF16), norm_ffn_post[0][None, :], g2, n)
    return out.reshape(b, n, d)
```
